```python
import jax, jax.numpy as jnp
from jax import lax
import numpy as np

D_MODEL = 1024
BATCH = 8
SEQ = 2048
DEPTH = 2

D_CONV = D_MODEL // 2
CONV_K = 31
D_SG = D_MODEL // 2
SG_GROUPS = 4
SG_CHUNK = 128
N_HEADS = 8
QK_NOPE = 64
QK_ROPE = 32
V_HEAD = 64
Q_LORA = 384
KV_LORA = 256
ROPE_THETA = 10000.0
Q_BLOCK = 128
N_BRANCH = 3
D_FF = 2816
FFN_K = 3
EPS = 1e-6

IN_SPLITS = (2 * D_CONV, 2 * D_SG, Q_LORA, KV_LORA, QK_ROPE, N_BRANCH * D_MODEL)
D_IN = 2 * D_CONV + 2 * D_SG + Q_LORA + KV_LORA + QK_ROPE + N_BRANCH * D_MODEL

kernel_name = "hybrid_gated_conv_sgu_mla_block"


def rmsnorm(x, g):
    x32 = x.astype(jnp.float32)
    y = x32 * lax.rsqrt(jnp.mean(x32 * x32, axis=-1, keepdims=True) + EPS)
    return (y * g.astype(jnp.float32)).astype(x.dtype)


def layernorm(x, g, b):
    x32 = x.astype(jnp.float32)
    mu = jnp.mean(x32, axis=-1, keepdims=True)
    var = jnp.mean(jnp.square(x32 - mu), axis=-1, keepdims=True)
    y = (x32 - mu) * lax.rsqrt(var + EPS) * g.astype(jnp.float32) + b.astype(jnp.float32)
    return y.astype(x.dtype)


def causal_dwconv(x, w, b):
    k, c = w.shape
    y = lax.conv_general_dilated(
        x, w[:, None, :].astype(x.dtype), window_strides=(1,), padding=[(k - 1, 0)],
        dimension_numbers=("NWC", "WIO", "NWC"), feature_group_count=c)
    return y + b.astype(x.dtype)


def rope_tables(positions):
    inv = ROPE_THETA ** (-jnp.arange(0, QK_ROPE, 2, dtype=jnp.float32) / QK_ROPE)
    ang = positions.astype(jnp.float32)[..., None] * inv
    return jnp.cos(ang), jnp.sin(ang)


def apply_rope(x, cos, sin):
    cos = cos.astype(x.dtype)
    sin = sin.astype(x.dtype)
    x1, x2 = jnp.split(x, 2, axis=-1)
    return jnp.concatenate([x1 * cos - x2 * sin, x2 * cos + x1 * sin], axis=-1)


def conv_module(a, w_dw, b_dw, ln_g, ln_b, w_p):
    val, gate = jnp.split(a, 2, axis=-1)
    z = val * jax.nn.sigmoid(gate)
    z = causal_dwconv(z, w_dw, b_dw)
    z = jax.nn.silu(layernorm(z, ln_g, ln_b))
    return z @ w_p


def spatial_gating(a, ln_g, ln_b, w_s, b_s, w_p):
    bsz, s, _ = a.shape
    u, v = jnp.split(jax.nn.gelu(a), 2, axis=-1)
    v = layernorm(v, ln_g, ln_b)
    nc = s // SG_CHUNK
    v = v.reshape(bsz, nc, SG_CHUNK, SG_GROUPS, D_SG // SG_GROUPS)
    causal = jnp.tril(jnp.ones((SG_CHUNK, SG_CHUNK), dtype=bool))
    w = jnp.where(causal[None], w_s, 0.0).astype(v.dtype)
    mixed = jnp.einsum("gts,bnsgc->bntgc", w, v) + b_s.T.astype(v.dtype)[None, None, :, :, None]
    mixed = mixed.reshape(bsz, s, D_SG)
    return (u * mixed) @ w_p


def mla(q_lat, kv_lat, k_rope, cos, sin, q_norm_g, w_uq, kv_norm_g, w_ukv, w_o):
    bsz, s, _ = q_lat.shape
    q = (rmsnorm(q_lat, q_norm_g) @ w_uq).reshape(bsz, s, N_HEADS, QK_NOPE + QK_ROPE)
    q_nope = q[..., :QK_NOPE]
    q_pe = apply_rope(q[..., QK_NOPE:], cos[:, :, None, :], sin[:, :, None, :])
    kv = (rmsnorm(kv_lat, kv_norm_g) @ w_ukv).reshape(bsz, s, N_HEADS, QK_NOPE + V_HEAD)
    k_nope = kv[..., :QK_NOPE]
    v = kv[..., QK_NOPE:]
    k_pe = apply_rope(k_rope, cos, sin)
    scale = (QK_NOPE + QK_ROPE) ** -0.5
    neg = jnp.finfo(jnp.float32).min
    outs = []
    for i in range(s // Q_BLOCK):
        q0, q1 = i * Q_BLOCK, (i + 1) * Q_BLOCK
        sc = (jnp.einsum("bqhd,bkhd->bhqk", q_nope[:, q0:q1], k_nope[:, :q1])
              + jnp.einsum("bqhr,bkr->bhqk", q_pe[:, q0:q1], k_pe[:, :q1])).astype(jnp.float32) * scale
        mask = (q0 + jnp.arange(Q_BLOCK))[:, None] >= jnp.arange(q1)[None, :]
        p = jax.nn.softmax(jnp.where(mask, sc, neg), axis=-1).astype(v.dtype)
        outs.append(jnp.einsum("bhqk,bkhd->bqhd", p, v[:, :q1]))
    o = jnp.concatenate(outs, axis=1).reshape(bsz, s, N_HEADS * V_HEAD)
    return o @ w_o


def conv_ffn(h, w_up, w_dw, b_dw, w_down):
    z = causal_dwconv(h @ w_up, w_dw, b_dw)
    g, v = jnp.split(z, 2, axis=-1)
    return (jax.nn.gelu(g, approximate=True) * v) @ w_down


def _fwd_setup_inputs(seed: int = 0) -> dict:
    key = jax.random.key(seed)
    ks = jax.random.split(key, 32)

    def nrm(k, shape, scale):
        return jax.random.normal(k, shape, dtype=jnp.float32) * scale

    def gain(k, n):
        return 1.0 + nrm(k, (DEPTH, n), 0.1)

    L = DEPTH
    x = nrm(ks[0], (BATCH, SEQ, D_MODEL), 1.0)
    offs = jax.random.randint(ks[1], (BATCH, 1), 0, SEQ, dtype=jnp.int32)
    positions = offs + jnp.arange(SEQ, dtype=jnp.int32)[None, :]
    return {
        "x": x,
        "positions": positions,
        "mix_pre_g": gain(ks[2], D_MODEL),
        "mix_post_g": gain(ks[3], D_MODEL),
        "ffn_pre_g": gain(ks[4], D_MODEL),
        "ffn_post_g": gain(ks[5], D_MODEL),
        "w_in": nrm(ks[6], (L, D_MODEL, D_IN), D_MODEL ** -0.5),
        "conv_dw_w": nrm(ks[7], (L, CONV_K, D_CONV), CONV_K ** -0.5),
        "conv_dw_b": nrm(ks[8], (L, D_CONV), 0.02),
        "conv_ln_g": gain(ks[9], D_CONV),
        "conv_ln_b": nrm(ks[10], (L, D_CONV), 0.02),
        "conv_out_w": nrm(ks[11], (L, D_CONV, D_MODEL), D_CONV ** -0.5),
        "sg_ln_g": gain(ks[12], D_SG),
        "sg_ln_b": nrm(ks[13], (L, D_SG), 0.02),
        "sg_w": nrm(ks[14], (L, SG_GROUPS, SG_CHUNK, SG_CHUNK), SG_CHUNK ** -0.5),
        "sg_b": 1.0 + nrm(ks[15], (L, SG_GROUPS, SG_CHUNK), 0.1),
        "sg_out_w": nrm(ks[16], (L, D_SG, D_MODEL), D_SG ** -0.5),
        "mla_q_norm_g": gain(ks[17], Q_LORA),
        "mla_w_uq": nrm(ks[18], (L, Q_LORA, N_HEADS * (QK_NOPE + QK_ROPE)), Q_LORA ** -0.5),
        "mla_kv_norm_g": gain(ks[19], KV_LORA),
        "mla_w_ukv": nrm(ks[20], (L, KV_LORA, N_HEADS * (QK_NOPE + V_HEAD)), KV_LORA ** -0.5),
        "mla_w_o": nrm(ks[21], (L, N_HEADS * V_HEAD, D_MODEL), (N_HEADS * V_HEAD) ** -0.5),
        "w_out": nrm(ks[22], (L, D_MODEL, D_MODEL), D_MODEL ** -0.5),
        "ffn_w_up": nrm(ks[23], (L, D_MODEL, 2 * D_FF), D_MODEL ** -0.5),
        "ffn_dw_w": nrm(ks[24], (L, FFN_K, 2 * D_FF), FFN_K ** -0.5),
        "ffn_dw_b": nrm(ks[25], (L, 2 * D_FF), 0.02),
        "ffn_w_down": nrm(ks[26], (L, D_FF, D_MODEL), D_FF ** -0.5),
    }


def _fwd_reference(x, positions, mix_pre_g, mix_post_g, ffn_pre_g, ffn_post_g, w_in,
              conv_dw_w, conv_dw_b, conv_ln_g, conv_ln_b, conv_out_w,
              sg_ln_g, sg_ln_b, sg_w, sg_b, sg_out_w,
              mla_q_norm_g, mla_w_uq, mla_kv_norm_g, mla_w_ukv, mla_w_o,
              w_out, ffn_w_up, ffn_dw_w, ffn_dw_b, ffn_w_down):
    bsz, s, d = x.shape
    cos, sin = rope_tables(positions)
    cuts = np.cumsum(IN_SPLITS)[:-1].tolist()
    for l in range(DEPTH):
        h = rmsnorm(x, mix_pre_g[l])
        a_in, b_in, q_lat, kv_lat, k_rope, gates = jnp.split(h @ w_in[l], cuts, axis=-1)
        y_a = conv_module(a_in, conv_dw_w[l], conv_dw_b[l], conv_ln_g[l], conv_ln_b[l], conv_out_w[l])
        y_b = spatial_gating(b_in, sg_ln_g[l], sg_ln_b[l], sg_w[l], sg_b[l], sg_out_w[l])
        y_c = mla(q_lat, kv_lat, k_rope, cos, sin, mla_q_norm_g[l], mla_w_uq[l],
                  mla_kv_norm_g[l], mla_w_ukv[l], mla_w_o[l])
        g = jax.nn.sigmoid(gates.reshape(bsz, s, N_BRANCH, d))
        merged = g[:, :, 0] * y_a + g[:, :, 1] * y_b + g[:, :, 2] * y_c
        x = x + rmsnorm(merged @ w_out[l], mix_post_g[l])
        h = rmsnorm(x, ffn_pre_g[l])
        x = x + rmsnorm(conv_ffn(h, ffn_w_up[l], ffn_dw_w[l], ffn_dw_b[l], ffn_w_down[l]), ffn_post_g[l])
    return x


import jax as _jax
import jax.numpy as _jnp

TWIN_FORMAT = 'train_step'
FWD_PARAMS = ['x', 'positions', 'mix_pre_g', 'mix_post_g', 'ffn_pre_g', 'ffn_post_g', 'w_in', 'conv_dw_w', 'conv_dw_b', 'conv_ln_g', 'conv_ln_b', 'conv_out_w', 'sg_ln_g', 'sg_ln_b', 'sg_w', 'sg_b', 'sg_out_w', 'mla_q_norm_g', 'mla_w_uq', 'mla_kv_norm_g', 'mla_w_ukv', 'mla_w_o', 'w_out', 'ffn_w_up', 'ffn_dw_w', 'ffn_dw_b', 'ffn_w_down']
TWIN_WEIGHTS = ['mix_pre_g', 'mix_post_g', 'ffn_pre_g', 'ffn_post_g', 'w_in', 'conv_dw_w', 'conv_dw_b', 'conv_ln_g', 'conv_ln_b', 'conv_out_w', 'sg_ln_g', 'sg_ln_b', 'sg_w', 'sg_b', 'sg_out_w', 'mla_q_norm_g', 'mla_w_uq', 'mla_kv_norm_g', 'mla_w_ukv', 'mla_w_o', 'w_out', 'ffn_w_up', 'ffn_dw_w', 'ffn_dw_b', 'ffn_w_down']
TWIN_DIFF_INPUT = 'x'
TWIN_INPUTS = ['x', 'positions', 'mix_pre_g', 'mix_post_g', 'ffn_pre_g', 'ffn_post_g', 'w_in', 'conv_dw_w', 'conv_dw_b', 'conv_ln_g', 'conv_ln_b', 'conv_out_w', 'sg_ln_g', 'sg_ln_b', 'sg_w', 'sg_b', 'sg_out_w', 'mla_q_norm_g', 'mla_w_uq', 'mla_kv_norm_g', 'mla_w_ukv', 'mla_w_o', 'w_out', 'ffn_w_up', 'ffn_dw_w', 'ffn_dw_b', 'ffn_w_down', 'loss_target', 'm_mix_pre_g', 'm_mix_post_g', 'm_ffn_pre_g', 'm_ffn_post_g', 'm_w_in', 'm_conv_dw_w', 'm_conv_dw_b', 'm_conv_ln_g', 'm_conv_ln_b', 'm_conv_out_w', 'm_sg_ln_g', 'm_sg_ln_b', 'm_sg_w', 'm_sg_b', 'm_sg_out_w', 'm_mla_q_norm_g', 'm_mla_w_uq', 'm_mla_kv_norm_g', 'm_mla_w_ukv', 'm_mla_w_o', 'm_w_out', 'm_ffn_w_up', 'm_ffn_dw_w', 'm_ffn_dw_b', 'm_ffn_w_down', 'v_mix_pre_g', 'v_mix_post_g', 'v_ffn_pre_g', 'v_ffn_post_g', 'v_w_in', 'v_conv_dw_w', 'v_conv_dw_b', 'v_conv_ln_g', 'v_conv_ln_b', 'v_conv_out_w', 'v_sg_ln_g', 'v_sg_ln_b', 'v_sg_w', 'v_sg_b', 'v_sg_out_w', 'v_mla_q_norm_g', 'v_mla_w_uq', 'v_mla_kv_norm_g', 'v_mla_w_ukv', 'v_mla_w_o', 'v_w_out', 'v_ffn_w_up', 'v_ffn_dw_w', 'v_ffn_dw_b', 'v_ffn_w_down']
TWIN_OUTPUTS = ['loss', 'grad_x', 'grad_mix_pre_g', 'grad_mix_post_g', 'grad_ffn_pre_g', 'grad_ffn_post_g', 'grad_w_in', 'grad_conv_dw_w', 'grad_conv_dw_b', 'grad_conv_ln_g', 'grad_conv_ln_b', 'grad_conv_out_w', 'grad_sg_ln_g', 'grad_sg_ln_b', 'grad_sg_w', 'grad_sg_b', 'grad_sg_out_w', 'grad_mla_q_norm_g', 'grad_mla_w_uq', 'grad_mla_kv_norm_g', 'grad_mla_w_ukv', 'grad_mla_w_o', 'grad_w_out', 'grad_ffn_w_up', 'grad_ffn_dw_w', 'grad_ffn_dw_b', 'grad_ffn_w_down', 'delta_mix_pre_g', 'delta_mix_post_g', 'delta_ffn_pre_g', 'delta_ffn_post_g', 'delta_w_in', 'delta_conv_dw_w', 'delta_conv_dw_b', 'delta_conv_ln_g', 'delta_conv_ln_b', 'delta_conv_out_w', 'delta_sg_ln_g', 'delta_sg_ln_b', 'delta_sg_w', 'delta_sg_b', 'delta_sg_out_w', 'delta_mla_q_norm_g', 'delta_mla_w_uq', 'delta_mla_kv_norm_g', 'delta_mla_w_ukv', 'delta_mla_w_o', 'delta_w_out', 'delta_ffn_w_up', 'delta_ffn_dw_w', 'delta_ffn_dw_b', 'delta_ffn_w_down', 'new_m_mix_pre_g', 'new_m_mix_post_g', 'new_m_ffn_pre_g', 'new_m_ffn_post_g', 'new_m_w_in', 'new_m_conv_dw_w', 'new_m_conv_dw_b', 'new_m_conv_ln_g', 'new_m_conv_ln_b', 'new_m_conv_out_w', 'new_m_sg_ln_g', 'new_m_sg_ln_b', 'new_m_sg_w', 'new_m_sg_b', 'new_m_sg_out_w', 'new_m_mla_q_norm_g', 'new_m_mla_w_uq', 'new_m_mla_kv_norm_g', 'new_m_mla_w_ukv', 'new_m_mla_w_o', 'new_m_w_out', 'new_m_ffn_w_up', 'new_m_ffn_dw_w', 'new_m_ffn_dw_b', 'new_m_ffn_w_down', 'new_v_mix_pre_g', 'new_v_mix_post_g', 'new_v_ffn_pre_g', 'new_v_ffn_post_g', 'new_v_w_in', 'new_v_conv_dw_w', 'new_v_conv_dw_b', 'new_v_conv_ln_g', 'new_v_conv_ln_b', 'new_v_conv_out_w', 'new_v_sg_ln_g', 'new_v_sg_ln_b', 'new_v_sg_w', 'new_v_sg_b', 'new_v_sg_out_w', 'new_v_mla_q_norm_g', 'new_v_mla_w_uq', 'new_v_mla_kv_norm_g', 'new_v_mla_w_ukv', 'new_v_mla_w_o', 'new_v_w_out', 'new_v_ffn_w_up', 'new_v_ffn_dw_w', 'new_v_ffn_dw_b', 'new_v_ffn_w_down']
TWIN_LEAF_KINDS = {'loss': 'loss', 'grad_x': 'grad_x', 'grad_mix_pre_g': 'grad_w', 'grad_mix_post_g': 'grad_w', 'grad_ffn_pre_g': 'grad_w', 'grad_ffn_post_g': 'grad_w', 'grad_w_in': 'grad_w', 'grad_conv_dw_w': 'grad_w', 'grad_conv_dw_b': 'grad_w', 'grad_conv_ln_g': 'grad_w', 'grad_conv_ln_b': 'grad_w', 'grad_conv_out_w': 'grad_w', 'grad_sg_ln_g': 'grad_w', 'grad_sg_ln_b': 'grad_w', 'grad_sg_w': 'grad_w', 'grad_sg_b': 'grad_w', 'grad_sg_out_w': 'grad_w', 'grad_mla_q_norm_g': 'grad_w', 'grad_mla_w_uq': 'grad_w', 'grad_mla_kv_norm_g': 'grad_w', 'grad_mla_w_ukv': 'grad_w', 'grad_mla_w_o': 'grad_w', 'grad_w_out': 'grad_w', 'grad_ffn_w_up': 'grad_w', 'grad_ffn_dw_w': 'grad_w', 'grad_ffn_dw_b': 'grad_w', 'grad_ffn_w_down': 'grad_w', 'delta_mix_pre_g': 'delta_w', 'delta_mix_post_g': 'delta_w', 'delta_ffn_pre_g': 'delta_w', 'delta_ffn_post_g': 'delta_w', 'delta_w_in': 'delta_w', 'delta_conv_dw_w': 'delta_w', 'delta_conv_dw_b': 'delta_w', 'delta_conv_ln_g': 'delta_w', 'delta_conv_ln_b': 'delta_w', 'delta_conv_out_w': 'delta_w', 'delta_sg_ln_g': 'delta_w', 'delta_sg_ln_b': 'delta_w', 'delta_sg_w': 'delta_w', 'delta_sg_b': 'delta_w', 'delta_sg_out_w': 'delta_w', 'delta_mla_q_norm_g': 'delta_w', 'delta_mla_w_uq': 'delta_w', 'delta_mla_kv_norm_g': 'delta_w', 'delta_mla_w_ukv': 'delta_w', 'delta_mla_w_o': 'delta_w', 'delta_w_out': 'delta_w', 'delta_ffn_w_up': 'delta_w', 'delta_ffn_dw_w': 'delta_w', 'delta_ffn_dw_b': 'delta_w', 'delta_ffn_w_down': 'delta_w', 'new_m_mix_pre_g': 'new_m', 'new_m_mix_post_g': 'new_m', 'new_m_ffn_pre_g': 'new_m', 'new_m_ffn_post_g': 'new_m', 'new_m_w_in': 'new_m', 'new_m_conv_dw_w': 'new_m', 'new_m_conv_dw_b': 'new_m', 'new_m_conv_ln_g': 'new_m', 'new_m_conv_ln_b': 'new_m', 'new_m_conv_out_w': 'new_m', 'new_m_sg_ln_g': 'new_m', 'new_m_sg_ln_b': 'new_m', 'new_m_sg_w': 'new_m', 'new_m_sg_b': 'new_m', 'new_m_sg_out_w': 'new_m', 'new_m_mla_q_norm_g': 'new_m', 'new_m_mla_w_uq': 'new_m', 'new_m_mla_kv_norm_g': 'new_m', 'new_m_mla_w_ukv': 'new_m', 'new_m_mla_w_o': 'new_m', 'new_m_w_out': 'new_m', 'new_m_ffn_w_up': 'new_m', 'new_m_ffn_dw_w': 'new_m', 'new_m_ffn_dw_b': 'new_m', 'new_m_ffn_w_down': 'new_m', 'new_v_mix_pre_g': 'new_v', 'new_v_mix_post_g': 'new_v', 'new_v_ffn_pre_g': 'new_v', 'new_v_ffn_post_g': 'new_v', 'new_v_w_in': 'new_v', 'new_v_conv_dw_w': 'new_v', 'new_v_conv_dw_b': 'new_v', 'new_v_conv_ln_g': 'new_v', 'new_v_conv_ln_b': 'new_v', 'new_v_conv_out_w': 'new_v', 'new_v_sg_ln_g': 'new_v', 'new_v_sg_ln_b': 'new_v', 'new_v_sg_w': 'new_v', 'new_v_sg_b': 'new_v', 'new_v_sg_out_w': 'new_v', 'new_v_mla_q_norm_g': 'new_v', 'new_v_mla_w_uq': 'new_v', 'new_v_mla_kv_norm_g': 'new_v', 'new_v_mla_w_ukv': 'new_v', 'new_v_mla_w_o': 'new_v', 'new_v_w_out': 'new_v', 'new_v_ffn_w_up': 'new_v', 'new_v_ffn_dw_w': 'new_v', 'new_v_ffn_dw_b': 'new_v', 'new_v_ffn_w_down': 'new_v'}


def _forward(args):
    return _fwd_reference(*[args[k] for k in FWD_PARAMS])


def _output_shape():
    out = _jax.eval_shape(lambda: _forward(_fwd_setup_inputs(0)))
    return out.shape, out.dtype

N_MICROBATCH = 1
ADAM_LR = 0.001
ADAM_B1 = 0.9
ADAM_B2 = 0.999
ADAM_EPS = 1e-08
ADAM_WD = 0.01
ADAM_STEP = 10
PER_EXAMPLE_BATCH_AXIS = {'x': 0, 'positions': 0, 'loss_target': 0}
SHARED_INPUTS = []
_WEIGHT_DTYPES = {'mix_pre_g': _jnp.float32, 'mix_post_g': _jnp.float32, 'ffn_pre_g': _jnp.float32, 'ffn_post_g': _jnp.float32, 'w_in': _jnp.float32, 'conv_dw_w': _jnp.float32, 'conv_dw_b': _jnp.float32, 'conv_ln_g': _jnp.float32, 'conv_ln_b': _jnp.float32, 'conv_out_w': _jnp.float32, 'sg_ln_g': _jnp.float32, 'sg_ln_b': _jnp.float32, 'sg_w': _jnp.float32, 'sg_b': _jnp.float32, 'sg_out_w': _jnp.float32, 'mla_q_norm_g': _jnp.float32, 'mla_w_uq': _jnp.float32, 'mla_kv_norm_g': _jnp.float32, 'mla_w_ukv': _jnp.float32, 'mla_w_o': _jnp.float32, 'w_out': _jnp.float32, 'ffn_w_up': _jnp.float32, 'ffn_dw_w': _jnp.float32, 'ffn_dw_b': _jnp.float32, 'ffn_w_down': _jnp.float32}
MOMENT_SCALE = {'mix_pre_g': 9.986532e-01, 'mix_post_g': 1.676239e+01, 'ffn_pre_g': 1.111808e+00, 'ffn_post_g': 1.611425e+01, 'w_in': 4.266209e-01, 'conv_dw_w': 7.607947e-01, 'conv_dw_b': 9.283850e+00, 'conv_ln_g': 3.569985e+00, 'conv_ln_b': 5.281458e+00, 'conv_out_w': 1.537227e+00, 'sg_ln_g': 3.402240e-01, 'sg_ln_b': 3.653074e-01, 'sg_w': 3.213757e-01, 'sg_b': 4.600618e-01, 'sg_out_w': 2.044580e+00, 'mla_q_norm_g': 1.483843e-01, 'mla_w_uq': 1.072851e-01, 'mla_kv_norm_g': 1.422504e+00, 'mla_w_ukv': 7.477946e-01, 'mla_w_o': 7.606520e-01, 'w_out': 2.633415e+00, 'ffn_w_up': 4.680195e-01, 'ffn_dw_w': 5.299793e-01, 'ffn_dw_b': 2.024957e+00, 'ffn_w_down': 9.889485e-01}


def _to_microbatches(a, axis):
    t = _jnp.moveaxis(a, axis, 0)
    t = t.reshape((N_MICROBATCH, t.shape[0] // N_MICROBATCH) + t.shape[1:])
    return _jnp.moveaxis(t, 1, axis + 1)


def setup_inputs(seed: int = 0) -> dict:
    inp = _fwd_setup_inputs(seed)
    key = _jax.random.fold_in(_jax.random.key(seed), 7919)
    shape, _ = _output_shape()
    out = dict(inp)
    out["loss_target"] = _jax.random.normal(_jax.random.fold_in(key, 0), shape, _jnp.float32)
    for i, name in enumerate(TWIN_WEIGHTS):
        w = inp[name].astype(_jnp.float32)
        if MOMENT_SCALE is None:
            s = _jnp.sqrt(_jnp.mean(_jnp.square(w)) + 1e-30)
        else:
            s = MOMENT_SCALE[name]
        km, kv = _jax.random.split(_jax.random.fold_in(key, i + 1))
        out[name] = w
        out["m_" + name] = s * _jax.random.normal(km, w.shape, _jnp.float32)
        out["v_" + name] = (s * s) * _jax.random.uniform(kv, w.shape, _jnp.float32, 0.5, 1.5)
    if N_MICROBATCH > 1:
        for name, axis in PER_EXAMPLE_BATCH_AXIS.items():
            out[name] = _to_microbatches(out[name], axis)
    return {'x': out['x'], 'positions': out['positions'], 'mix_pre_g': out['mix_pre_g'], 'mix_post_g': out['mix_post_g'], 'ffn_pre_g': out['ffn_pre_g'], 'ffn_post_g': out['ffn_post_g'], 'w_in': out['w_in'], 'conv_dw_w': out['conv_dw_w'], 'conv_dw_b': out['conv_dw_b'], 'conv_ln_g': out['conv_ln_g'], 'conv_ln_b': out['conv_ln_b'], 'conv_out_w': out['conv_out_w'], 'sg_ln_g': out['sg_ln_g'], 'sg_ln_b': out['sg_ln_b'], 'sg_w': out['sg_w'], 'sg_b': out['sg_b'], 'sg_out_w': out['sg_out_w'], 'mla_q_norm_g': out['mla_q_norm_g'], 'mla_w_uq': out['mla_w_uq'], 'mla_kv_norm_g': out['mla_kv_norm_g'], 'mla_w_ukv': out['mla_w_ukv'], 'mla_w_o': out['mla_w_o'], 'w_out': out['w_out'], 'ffn_w_up': out['ffn_w_up'], 'ffn_dw_w': out['ffn_dw_w'], 'ffn_dw_b': out['ffn_dw_b'], 'ffn_w_down': out['ffn_w_down'], 'loss_target': out['loss_target'], 'm_mix_pre_g': out['m_mix_pre_g'], 'm_mix_post_g': out['m_mix_post_g'], 'm_ffn_pre_g': out['m_ffn_pre_g'], 'm_ffn_post_g': out['m_ffn_post_g'], 'm_w_in': out['m_w_in'], 'm_conv_dw_w': out['m_conv_dw_w'], 'm_conv_dw_b': out['m_conv_dw_b'], 'm_conv_ln_g': out['m_conv_ln_g'], 'm_conv_ln_b': out['m_conv_ln_b'], 'm_conv_out_w': out['m_conv_out_w'], 'm_sg_ln_g': out['m_sg_ln_g'], 'm_sg_ln_b': out['m_sg_ln_b'], 'm_sg_w': out['m_sg_w'], 'm_sg_b': out['m_sg_b'], 'm_sg_out_w': out['m_sg_out_w'], 'm_mla_q_norm_g': out['m_mla_q_norm_g'], 'm_mla_w_uq': out['m_mla_w_uq'], 'm_mla_kv_norm_g': out['m_mla_kv_norm_g'], 'm_mla_w_ukv': out['m_mla_w_ukv'], 'm_mla_w_o': out['m_mla_w_o'], 'm_w_out': out['m_w_out'], 'm_ffn_w_up': out['m_ffn_w_up'], 'm_ffn_dw_w': out['m_ffn_dw_w'], 'm_ffn_dw_b': out['m_ffn_dw_b'], 'm_ffn_w_down': out['m_ffn_w_down'], 'v_mix_pre_g': out['v_mix_pre_g'], 'v_mix_post_g': out['v_mix_post_g'], 'v_ffn_pre_g': out['v_ffn_pre_g'], 'v_ffn_post_g': out['v_ffn_post_g'], 'v_w_in': out['v_w_in'], 'v_conv_dw_w': out['v_conv_dw_w'], 'v_conv_dw_b': out['v_conv_dw_b'], 'v_conv_ln_g': out['v_conv_ln_g'], 'v_conv_ln_b': out['v_conv_ln_b'], 'v_conv_out_w': out['v_conv_out_w'], 'v_sg_ln_g': out['v_sg_ln_g'], 'v_sg_ln_b': out['v_sg_ln_b'], 'v_sg_w': out['v_sg_w'], 'v_sg_b': out['v_sg_b'], 'v_sg_out_w': out['v_sg_out_w'], 'v_mla_q_norm_g': out['v_mla_q_norm_g'], 'v_mla_w_uq': out['v_mla_w_uq'], 'v_mla_kv_norm_g': out['v_mla_kv_norm_g'], 'v_mla_w_ukv': out['v_mla_w_ukv'], 'v_mla_w_o': out['v_mla_w_o'], 'v_w_out': out['v_w_out'], 'v_ffn_w_up': out['v_ffn_w_up'], 'v_ffn_dw_w': out['v_ffn_dw_w'], 'v_ffn_dw_b': out['v_ffn_dw_b'], 'v_ffn_w_down': out['v_ffn_w_down']}


def _loss(weights, diff, rest, loss_target):
    with _jax.named_scope("forward"):
        args = {**rest, TWIN_DIFF_INPUT: diff, **{k: w.astype(_WEIGHT_DTYPES[k]) for k, w in weights.items()}}
        y = _forward(args)
    with _jax.named_scope("loss_head"):
        err = _jnp.square(y.astype(_jnp.float32) - loss_target)
        return 0.5 * _jnp.sum(_jnp.mean(err, axis=-1)) if err.ndim else 0.5 * err


def _adamw(w, g, m, v):
    m = ADAM_B1 * m + (1.0 - ADAM_B1) * g
    v = ADAM_B2 * v + (1.0 - ADAM_B2) * _jnp.square(g)
    m_hat = m / (1.0 - ADAM_B1 ** ADAM_STEP)
    v_hat = v / (1.0 - ADAM_B2 ** ADAM_STEP)
    delta = -ADAM_LR * (m_hat / (_jnp.sqrt(v_hat) + ADAM_EPS) + ADAM_WD * w)
    return delta, m, v


def reference(x, positions, mix_pre_g, mix_post_g, ffn_pre_g, ffn_post_g, w_in, conv_dw_w, conv_dw_b, conv_ln_g, conv_ln_b, conv_out_w, sg_ln_g, sg_ln_b, sg_w, sg_b, sg_out_w, mla_q_norm_g, mla_w_uq, mla_kv_norm_g, mla_w_ukv, mla_w_o, w_out, ffn_w_up, ffn_dw_w, ffn_dw_b, ffn_w_down, loss_target, m_mix_pre_g, m_mix_post_g, m_ffn_pre_g, m_ffn_post_g, m_w_in, m_conv_dw_w, m_conv_dw_b, m_conv_ln_g, m_conv_ln_b, m_conv_out_w, m_sg_ln_g, m_sg_ln_b, m_sg_w, m_sg_b, m_sg_out_w, m_mla_q_norm_g, m_mla_w_uq, m_mla_kv_norm_g, m_mla_w_ukv, m_mla_w_o, m_w_out, m_ffn_w_up, m_ffn_dw_w, m_ffn_dw_b, m_ffn_w_down, v_mix_pre_g, v_mix_post_g, v_ffn_pre_g, v_ffn_post_g, v_w_in, v_conv_dw_w, v_conv_dw_b, v_conv_ln_g, v_conv_ln_b, v_conv_out_w, v_sg_ln_g, v_sg_ln_b, v_sg_w, v_sg_b, v_sg_out_w, v_mla_q_norm_g, v_mla_w_uq, v_mla_kv_norm_g, v_mla_w_ukv, v_mla_w_o, v_w_out, v_ffn_w_up, v_ffn_dw_w, v_ffn_dw_b, v_ffn_w_down):
    given = dict(x=x, positions=positions, mix_pre_g=mix_pre_g, mix_post_g=mix_post_g, ffn_pre_g=ffn_pre_g, ffn_post_g=ffn_post_g, w_in=w_in, conv_dw_w=conv_dw_w, conv_dw_b=conv_dw_b, conv_ln_g=conv_ln_g, conv_ln_b=conv_ln_b, conv_out_w=conv_out_w, sg_ln_g=sg_ln_g, sg_ln_b=sg_ln_b, sg_w=sg_w, sg_b=sg_b, sg_out_w=sg_out_w, mla_q_norm_g=mla_q_norm_g, mla_w_uq=mla_w_uq, mla_kv_norm_g=mla_kv_norm_g, mla_w_ukv=mla_w_ukv, mla_w_o=mla_w_o, w_out=w_out, ffn_w_up=ffn_w_up, ffn_dw_w=ffn_dw_w, ffn_dw_b=ffn_dw_b, ffn_w_down=ffn_w_down, loss_target=loss_target, m_mix_pre_g=m_mix_pre_g, m_mix_post_g=m_mix_post_g, m_ffn_pre_g=m_ffn_pre_g, m_ffn_post_g=m_ffn_post_g, m_w_in=m_w_in, m_conv_dw_w=m_conv_dw_w, m_conv_dw_b=m_conv_dw_b, m_conv_ln_g=m_conv_ln_g, m_conv_ln_b=m_conv_ln_b, m_conv_out_w=m_conv_out_w, m_sg_ln_g=m_sg_ln_g, m_sg_ln_b=m_sg_ln_b, m_sg_w=m_sg_w, m_sg_b=m_sg_b, m_sg_out_w=m_sg_out_w, m_mla_q_norm_g=m_mla_q_norm_g, m_mla_w_uq=m_mla_w_uq, m_mla_kv_norm_g=m_mla_kv_norm_g, m_mla_w_ukv=m_mla_w_ukv, m_mla_w_o=m_mla_w_o, m_w_out=m_w_out, m_ffn_w_up=m_ffn_w_up, m_ffn_dw_w=m_ffn_dw_w, m_ffn_dw_b=m_ffn_dw_b, m_ffn_w_down=m_ffn_w_down, v_mix_pre_g=v_mix_pre_g, v_mix_post_g=v_mix_post_g, v_ffn_pre_g=v_ffn_pre_g, v_ffn_post_g=v_ffn_post_g, v_w_in=v_w_in, v_conv_dw_w=v_conv_dw_w, v_conv_dw_b=v_conv_dw_b, v_conv_ln_g=v_conv_ln_g, v_conv_ln_b=v_conv_ln_b, v_conv_out_w=v_conv_out_w, v_sg_ln_g=v_sg_ln_g, v_sg_ln_b=v_sg_ln_b, v_sg_w=v_sg_w, v_sg_b=v_sg_b, v_sg_out_w=v_sg_out_w, v_mla_q_norm_g=v_mla_q_norm_g, v_mla_w_uq=v_mla_w_uq, v_mla_kv_norm_g=v_mla_kv_norm_g, v_mla_w_ukv=v_mla_w_ukv, v_mla_w_o=v_mla_w_o, v_w_out=v_w_out, v_ffn_w_up=v_ffn_w_up, v_ffn_dw_w=v_ffn_dw_w, v_ffn_dw_b=v_ffn_dw_b, v_ffn_w_down=v_ffn_w_down)
    weights = {n: given[n] for n in TWIN_WEIGHTS}
    shared = {n: given[n] for n in SHARED_INPUTS}
    per_example = {n: given[n] for n in ['x', 'positions']}
    grad_fn = _jax.value_and_grad(_loss, argnums=(0, 1))

    def one_microbatch(ex, loss_target):
        ex = dict(ex)
        diff = ex.pop(TWIN_DIFF_INPUT)
        return grad_fn(weights, diff, {**shared, **ex}, loss_target)

    if N_MICROBATCH == 1:
        loss, (grad_w, grad_x) = one_microbatch(per_example, given["loss_target"])
    else:
        def body(carry, xs):
            loss_sum, grad_sum = carry
            l_k, (gw_k, gx_k) = one_microbatch(xs[0], xs[1])
            with _jax.named_scope("update"):
                return (loss_sum + l_k, _jax.tree.map(_jnp.add, grad_sum, gw_k)), gx_k

        init = (_jnp.zeros((), _jnp.float32), _jax.tree.map(_jnp.zeros_like, weights))
        (loss, grad_w), grad_x = _jax.lax.scan(body, init, (per_example, given["loss_target"]))
    with _jax.named_scope("update"):
        delta_w, new_m, new_v = {}, {}, {}
        for n in TWIN_WEIGHTS:
            delta_w[n], new_m[n], new_v[n] = _adamw(weights[n], grad_w[n], given["m_" + n], given["v_" + n])
    return (loss, grad_x, *[grad_w[n] for n in TWIN_WEIGHTS], *[delta_w[n] for n in TWIN_WEIGHTS],
            *[new_m[n] for n in TWIN_WEIGHTS], *[new_v[n] for n in TWIN_WEIGHTS])
```

```python
import functools
import math

import jax
import jax.numpy as jnp
from jax import lax
from jax.experimental import pallas as pl
from jax.experimental.pallas import tpu as pltpu

F32 = jnp.float32
_MXU_DT = jnp.bfloat16
_VMEM_LIMIT = 48 * 1024 * 1024
_LANES = 128
_MESH = pl.DeviceIdType.MESH

N_HEADS = 8
QK_NOPE = 64
QK_ROPE = 32
V_HEAD = 64
HEAD_PAD = 128
SG_GROUPS = 4
SG_CHUNK = 128
CONV_K = 31
FFN_K = 3
ROPE_THETA = 10000.0
EPS = 1e-6
ADAM_LR, ADAM_B1, ADAM_B2, ADAM_EPS, ADAM_WD, ADAM_STEP = 0.001, 0.9, 0.999, 1e-08, 0.01, 10

SHARDED_LAST = ("w_in", "conv_dw_w", "conv_out_w", "sg_out_w", "mla_w_uq", "mla_w_ukv", "mla_w_o", "ffn_w_up",
                "ffn_dw_w")
SHARDED_MID = ("w_out", "ffn_w_down")
SHARDED = SHARDED_LAST + SHARDED_MID
WEIGHTS = ("mix_pre_g", "mix_post_g", "ffn_pre_g", "ffn_post_g", "w_in", "conv_dw_w", "conv_dw_b", "conv_ln_g",
           "conv_ln_b", "conv_out_w", "sg_ln_g", "sg_ln_b", "sg_w", "sg_b", "sg_out_w", "mla_q_norm_g", "mla_w_uq",
           "mla_kv_norm_g", "mla_w_ukv", "mla_w_o", "w_out", "ffn_w_up", "ffn_dw_w", "ffn_dw_b", "ffn_w_down")
REPLICATED = tuple(n for n in WEIGHTS if n not in SHARDED)


def _cparams(sem=None):
    return pltpu.CompilerParams(dimension_semantics=sem, vmem_limit_bytes=_VMEM_LIMIT)


def _pick(n, cands):
    for c in cands:
        if n % c == 0:
            return c
    return n


def _mm(name, a, b, *, ta=False, tb=False, out_dtype=F32):
    (kdim, m) = a.shape if ta else a.shape[::-1]
    (n, kdim2) = b.shape if tb else b.shape[::-1]
    assert kdim == kdim2, (a.shape, b.shape, ta, tb)
    tm = _pick(m, (512, 256, 128))
    tn = _pick(n, (512, 256, 128))
    tk = _pick(kdim, (1024, 512, 256, 128))
    nk = kdim // tk
    dims = (((0 if ta else 1,), (1 if tb else 0,)), ((), ()))

    def body(a_ref, b_ref, o_ref, acc_ref):
        k = pl.program_id(2)

        @pl.when(k == 0)
        def _():
            acc_ref[...] = jnp.zeros_like(acc_ref)

        acc_ref[...] += lax.dot_general(a_ref[...].astype(_MXU_DT), b_ref[...].astype(_MXU_DT), dims,
                                        preferred_element_type=F32)

        @pl.when(k == nk - 1)
        def _():
            o_ref[...] = acc_ref[...].astype(o_ref.dtype)

    a_spec = pl.BlockSpec((tk, tm), lambda i, j, k: (k, i)) if ta else pl.BlockSpec((tm, tk), lambda i, j, k: (i, k))
    b_spec = pl.BlockSpec((tn, tk), lambda i, j, k: (j, k)) if tb else pl.BlockSpec((tk, tn), lambda i, j, k: (k, j))
    return pl.pallas_call(
        body, name=name, grid=(m // tm, n // tn, nk),
        in_specs=[a_spec, b_spec], out_specs=pl.BlockSpec((tm, tn), lambda i, j, k: (i, j)),
        out_shape=jax.ShapeDtypeStruct((m, n), out_dtype),
        scratch_shapes=[pltpu.VMEM((tm, tn), F32)],
        compiler_params=_cparams(("parallel", "parallel", "arbitrary")),
    )(a, b)


def _row_spec(tm, width, idx):
    return pl.BlockSpec((tm, width), lambda i: (i, idx))


def _full_spec(shape):
    zeros = (0,) * len(shape)
    return pl.BlockSpec(shape, lambda i: zeros)


def _row_fwd(name, fn, rows, params, outs, tm=256):
    s = rows[0][0].shape[0]
    nr, npar = len(rows), len(params)

    def body(*refs):
        vals = [r[...].astype(F32) for r in refs[:nr + npar]]
        res = fn(*vals)
        for o_ref, r in zip(refs[nr + npar:], res):
            o_ref[...] = r.astype(o_ref.dtype)

    return pl.pallas_call(
        body, name=name, grid=(s // tm,),
        in_specs=[_row_spec(tm, w, i) for _, w, i in rows] + [_full_spec(p.shape) for p in params],
        out_specs=[_row_spec(tm, w, 0) for w, _ in outs],
        out_shape=[jax.ShapeDtypeStruct((s, w), dt) for w, dt in outs],
        compiler_params=_cparams(("parallel",)),
    )(*[r[0] for r in rows], *params)


def _row_bwd(name, fn, rows, params, cots, row_grads, param_grads, tm=256):
    s = rows[0][0].shape[0]
    nr, npar, nc = len(rows), len(params), len(cots)

    def body(*refs):
        i = pl.program_id(0)
        vals = [r[...].astype(F32) for r in refs[:nr + npar]]
        cvals = tuple(r[...].astype(F32) for r in refs[nr + npar:nr + npar + nc])
        _, vjp = jax.vjp(fn, *vals)
        grads = vjp(cvals)
        outs = refs[nr + npar + nc:]
        for o_ref, (idx, _) in zip(outs, row_grads):
            o_ref[...] = grads[idx].astype(o_ref.dtype)
        for o_ref, idx in zip(outs[len(row_grads):], param_grads):
            @pl.when(i == 0)
            def _(o_ref=o_ref):
                o_ref[...] = jnp.zeros_like(o_ref)

            o_ref[...] += grads[nr + idx]

    return pl.pallas_call(
        body, name=name, grid=(s // tm,),
        in_specs=([_row_spec(tm, w, i) for _, w, i in rows] + [_full_spec(p.shape) for p in params]
                  + [_row_spec(tm, w, i) for _, w, i in cots]),
        out_specs=([_row_spec(tm, rows[idx][1], 0) for idx, _ in row_grads]
                   + [_full_spec(params[idx].shape) for idx in param_grads]),
        out_shape=([jax.ShapeDtypeStruct((s, rows[idx][1]), dt) for idx, dt in row_grads]
                   + [jax.ShapeDtypeStruct(params[idx].shape, F32) for idx in param_grads]),
        compiler_params=_cparams(("arbitrary",)),
    )(*[r[0] for r in rows], *params, *[c[0] for c in cots])


def _rms(x, g):
    return x * lax.rsqrt(jnp.mean(x * x, axis=-1, keepdims=True) + EPS) * g


def _ln(x, g, b):
    mu = jnp.mean(x, axis=-1, keepdims=True)
    xc = x - mu
    var = jnp.mean(xc * xc, axis=-1, keepdims=True)
    return xc * lax.rsqrt(var + EPS) * g + b


def _sigmoid(x):
    return 1.0 / (1.0 + jnp.exp(-x))


def _gelu(x):
    return x * (0.5 * (1.0 + jnp.tanh(math.sqrt(2.0 / math.pi) * (x + 0.044715 * (x * x * x)))))


def _f_rms(x, g):
    return (_rms(x, g),)


def _f_x_rms(x, g):
    return (x, _rms(x, g))


def _f_ln_silu(z, g, b):
    y = _ln(z, g, b)
    return (y * _sigmoid(y),)


def _f_sg_pre(bu, bv, g, b):
    return (_gelu(bu), _ln(_gelu(bv), g, b))


def _f_merge(g0, g1, g2, ya, yb, yc):
    return (_sigmoid(g0) * ya + _sigmoid(g1) * yb + _sigmoid(g2) * yc,)


def _f_resid_rms(x, t, g_post):
    return (x + _rms(t, g_post),)


def _f_resid_rms_rms(x, t, g_post, g_next):
    x1 = x + _rms(t, g_post)
    return (x1, _rms(x1, g_next))


def _f_geglu(zg, zv):
    return _gelu(zg) * zv


_CONV_TR = 128


def _conv_tile(zp_ref, w_ref, bias, k_taps, off, r0):
    acc = jnp.broadcast_to(bias, (_CONV_TR, zp_ref.shape[1]))
    for k in range(k_taps):
        acc = acc + w_ref[k:k + 1, :] * zp_ref[r0 + off + k:r0 + off + k + _CONV_TR, :]
    return acc


def _conv_bwd_input_tile(dzp_ref, w_ref, k_taps, r0):
    acc = jnp.zeros((_CONV_TR, dzp_ref.shape[1]), F32)
    for k in range(k_taps):
        s0 = r0 + (k_taps - 1) - k
        acc = acc + w_ref[k:k + 1, :] * dzp_ref[s0:s0 + _CONV_TR, :]
    return acc


def _conv_bwd_weight(dzp_ref, zp_ref, dw_ref, db_ref, k_taps, off, s):
    c = zp_ref.shape[1]
    fold = lambda v: jnp.sum(v.reshape(_CONV_TR // 8, 8, c), axis=0)
    for k in range(k_taps):
        acc = jnp.zeros((8, c), F32)
        for r in range(s // _CONV_TR):
            r0 = r * _CONV_TR
            acc = acc + fold(dzp_ref[r0:r0 + _CONV_TR, :] * zp_ref[r0 + off + k:r0 + off + k + _CONV_TR, :])
        dw_ref[k:k + 1, :] = jnp.sum(acc, axis=0, keepdims=True)
    acc = jnp.zeros((8, c), F32)
    for r in range(s // _CONV_TR):
        acc = acc + fold(dzp_ref[r * _CONV_TR:(r + 1) * _CONV_TR, :])
    db_ref[...] = jnp.sum(acc, axis=0, keepdims=True)


def _glu_conv_fwd(p, val_idx, gate_idx, w, b):
    s = p.shape[0]
    k_taps, c = w.shape
    cb, pad = _LANES, 32
    off = pad - (k_taps - 1)

    def body(val_ref, gate_ref, w_ref, b_ref, o_ref, zp_ref):
        zp_ref[0:pad, :] = jnp.zeros((pad, cb), F32)
        zp_ref[pad:pad + s, :] = val_ref[...] * _sigmoid(gate_ref[...])
        for r in range(s // _CONV_TR):
            o_ref[r * _CONV_TR:(r + 1) * _CONV_TR, :] = _conv_tile(zp_ref, w_ref, b_ref[...], k_taps, off,
                                                                    r * _CONV_TR)

    return pl.pallas_call(
        body, name="glu_conv_fwd", grid=(c // cb,),
        in_specs=[pl.BlockSpec((s, cb), lambda j: (0, val_idx + j)), pl.BlockSpec((s, cb), lambda j: (0, gate_idx + j)),
                  pl.BlockSpec((k_taps, cb), lambda j: (0, j)), pl.BlockSpec((1, cb), lambda j: (0, j))],
        out_specs=pl.BlockSpec((s, cb), lambda j: (0, j)),
        out_shape=jax.ShapeDtypeStruct((s, c), F32),
        scratch_shapes=[pltpu.VMEM((s + pad, cb), F32)],
        compiler_params=_cparams(("parallel",)),
    )(p, p, w, b)


def _glu_conv_bwd(p, val_idx, gate_idx, w, dz):
    s = p.shape[0]
    k_taps, c = w.shape
    cb, pad = _LANES, 32
    off = pad - (k_taps - 1)

    def body(val_ref, gate_ref, w_ref, dz_ref, dval_ref, dgate_ref, dw_ref, db_ref, zp_ref, dzp_ref):
        zp_ref[0:pad, :] = jnp.zeros((pad, cb), F32)
        zp_ref[pad:pad + s, :] = val_ref[...] * _sigmoid(gate_ref[...])
        dzp_ref[0:s, :] = dz_ref[...]
        dzp_ref[s:s + pad, :] = jnp.zeros((pad, cb), F32)
        for r in range(s // _CONV_TR):
            rows = slice(r * _CONV_TR, (r + 1) * _CONV_TR)
            dz0 = _conv_bwd_input_tile(dzp_ref, w_ref, k_taps, r * _CONV_TR)
            sg = _sigmoid(gate_ref[rows, :])
            dval_ref[rows, :] = (dz0 * sg).astype(dval_ref.dtype)
            dgate_ref[rows, :] = (dz0 * val_ref[rows, :] * sg * (1.0 - sg)).astype(dgate_ref.dtype)
        _conv_bwd_weight(dzp_ref, zp_ref, dw_ref, db_ref, k_taps, off, s)

    return pl.pallas_call(
        body, name="glu_conv_bwd", grid=(c // cb,),
        in_specs=[pl.BlockSpec((s, cb), lambda j: (0, val_idx + j)), pl.BlockSpec((s, cb), lambda j: (0, gate_idx + j)),
                  pl.BlockSpec((k_taps, cb), lambda j: (0, j)), pl.BlockSpec((s, cb), lambda j: (0, j))],
        out_specs=[pl.BlockSpec((s, cb), lambda j: (0, j)), pl.BlockSpec((s, cb), lambda j: (0, j)),
                   pl.BlockSpec((k_taps, cb), lambda j: (0, j)), pl.BlockSpec((1, cb), lambda j: (0, j))],
        out_shape=[jax.ShapeDtypeStruct((s, c), _MXU_DT), jax.ShapeDtypeStruct((s, c), _MXU_DT),
                   jax.ShapeDtypeStruct((k_taps, c), F32), jax.ShapeDtypeStruct((1, c), F32)],
        scratch_shapes=[pltpu.VMEM((s + pad, cb), F32), pltpu.VMEM((s + pad, cb), F32)],
        compiler_params=_cparams(("parallel",)),
    )(p, p, w, dz)


def _conv_geglu_fwd(up, w, b):
    s, f2 = up.shape
    f = f2 // 2
    k_taps = w.shape[0]
    cb, pad = 256, 8
    off = pad - (k_taps - 1)
    nb = f // cb

    def body(ug_ref, uv_ref, wg_ref, wv_ref, bg_ref, bv_ref, o_ref, zg_ref, zv_ref):
        for u_ref, z_ref in ((ug_ref, zg_ref), (uv_ref, zv_ref)):
            z_ref[0:pad, :] = jnp.zeros((pad, cb), F32)
            z_ref[pad:pad + s, :] = u_ref[...]
        for r in range(s // _CONV_TR):
            zg = _conv_tile(zg_ref, wg_ref, bg_ref[...], k_taps, off, r * _CONV_TR)
            zv = _conv_tile(zv_ref, wv_ref, bv_ref[...], k_taps, off, r * _CONV_TR)
            o_ref[r * _CONV_TR:(r + 1) * _CONV_TR, :] = _f_geglu(zg, zv).astype(o_ref.dtype)

    return pl.pallas_call(
        body, name="conv_geglu_fwd", grid=(nb,),
        in_specs=[pl.BlockSpec((s, cb), lambda j: (0, j)), pl.BlockSpec((s, cb), lambda j: (0, nb + j)),
                  pl.BlockSpec((k_taps, cb), lambda j: (0, j)), pl.BlockSpec((k_taps, cb), lambda j: (0, nb + j)),
                  pl.BlockSpec((1, cb), lambda j: (0, j)), pl.BlockSpec((1, cb), lambda j: (0, nb + j))],
        out_specs=pl.BlockSpec((s, cb), lambda j: (0, j)),
        out_shape=jax.ShapeDtypeStruct((s, f), _MXU_DT),
        scratch_shapes=[pltpu.VMEM((s + pad, cb), F32), pltpu.VMEM((s + pad, cb), F32)],
        compiler_params=_cparams(("parallel",)),
    )(up, up, w, w, b, b)


def _conv_geglu_bwd(up, w, b, dact):
    s, f2 = up.shape
    f = f2 // 2
    k_taps = w.shape[0]
    cb, pad = 256, 8
    off = pad - (k_taps - 1)
    nb = f // cb

    def body(ug_ref, uv_ref, wg_ref, wv_ref, bg_ref, bv_ref, da_ref, dug_ref, duv_ref, dwg_ref, dwv_ref, dbg_ref,
             dbv_ref, zg_ref, zv_ref, dzg_ref, dzv_ref):
        for u_ref, z_ref, dz_ref in ((ug_ref, zg_ref, dzg_ref), (uv_ref, zv_ref, dzv_ref)):
            z_ref[0:pad, :] = jnp.zeros((pad, cb), F32)
            z_ref[pad:pad + s, :] = u_ref[...]
            dz_ref[s:s + pad, :] = jnp.zeros((pad, cb), F32)
        for r in range(s // _CONV_TR):
            rows = slice(r * _CONV_TR, (r + 1) * _CONV_TR)
            zg = _conv_tile(zg_ref, wg_ref, bg_ref[...], k_taps, off, r * _CONV_TR)
            zv = _conv_tile(zv_ref, wv_ref, bv_ref[...], k_taps, off, r * _CONV_TR)
            _, vjp = jax.vjp(_f_geglu, zg, zv)
            dzg, dzv = vjp(da_ref[rows, :].astype(F32))
            dzg_ref[rows, :] = dzg
            dzv_ref[rows, :] = dzv
        for r in range(s // _CONV_TR):
            rows = slice(r * _CONV_TR, (r + 1) * _CONV_TR)
            dug_ref[rows, :] = _conv_bwd_input_tile(dzg_ref, wg_ref, k_taps, r * _CONV_TR).astype(dug_ref.dtype)
            duv_ref[rows, :] = _conv_bwd_input_tile(dzv_ref, wv_ref, k_taps, r * _CONV_TR).astype(duv_ref.dtype)
        _conv_bwd_weight(dzg_ref, zg_ref, dwg_ref, dbg_ref, k_taps, off, s)
        _conv_bwd_weight(dzv_ref, zv_ref, dwv_ref, dbv_ref, k_taps, off, s)

    col = lambda rows_: pl.BlockSpec((rows_, cb), lambda j: (0, j))
    outs = pl.pallas_call(
        body, name="conv_geglu_bwd", grid=(nb,),
        in_specs=[pl.BlockSpec((s, cb), lambda j: (0, j)), pl.BlockSpec((s, cb), lambda j: (0, nb + j)),
                  pl.BlockSpec((k_taps, cb), lambda j: (0, j)), pl.BlockSpec((k_taps, cb), lambda j: (0, nb + j)),
                  pl.BlockSpec((1, cb), lambda j: (0, j)), pl.BlockSpec((1, cb), lambda j: (0, nb + j)),
                  pl.BlockSpec((s, cb), lambda j: (0, j))],
        out_specs=[col(s), col(s), col(k_taps), col(k_taps), col(1), col(1)],
        out_shape=[jax.ShapeDtypeStruct((s, f), _MXU_DT), jax.ShapeDtypeStruct((s, f), _MXU_DT),
                   jax.ShapeDtypeStruct((k_taps, f), F32), jax.ShapeDtypeStruct((k_taps, f), F32),
                   jax.ShapeDtypeStruct((1, f), F32), jax.ShapeDtypeStruct((1, f), F32)],
        scratch_shapes=[pltpu.VMEM((s + pad, cb), F32) for _ in range(4)],
        compiler_params=_cparams(("parallel",)),
    )(up, up, w, w, b, b, dact)
    dug, duv, dwg, dwv, dbg, dbv = outs
    return (jnp.concatenate([dug, duv], axis=1), jnp.concatenate([dwg, dwv], axis=1),
            jnp.concatenate([dbg, dbv], axis=1))


def _tril_mask():
    t = lax.broadcasted_iota(jnp.int32, (SG_CHUNK, SG_CHUNK), 0)
    s = lax.broadcasted_iota(jnp.int32, (SG_CHUNK, SG_CHUNK), 1)
    return t >= s


def _sg_mix_fwd(u, vn, w, bcol):
    s, c = u.shape
    gw = c // SG_GROUPS

    def body(u_ref, v_ref, w_ref, b_ref, o_ref):
        wm = jnp.where(_tril_mask(), w_ref[0], 0.0).astype(_MXU_DT)
        for n in range(s // SG_CHUNK):
            rows = slice(n * SG_CHUNK, (n + 1) * SG_CHUNK)
            mixed = jnp.dot(wm, v_ref[rows, :], preferred_element_type=F32) + b_ref[0]
            o_ref[rows, :] = (u_ref[rows, :] * mixed).astype(o_ref.dtype)

    return pl.pallas_call(
        body, name="sg_mix_fwd", grid=(SG_GROUPS,),
        in_specs=[pl.BlockSpec((s, gw), lambda g: (0, g)), pl.BlockSpec((s, gw), lambda g: (0, g)),
                  pl.BlockSpec((1, SG_CHUNK, SG_CHUNK), lambda g: (g, 0, 0)),
                  pl.BlockSpec((1, SG_CHUNK, 1), lambda g: (g, 0, 0))],
        out_specs=pl.BlockSpec((s, gw), lambda g: (0, g)),
        out_shape=jax.ShapeDtypeStruct((s, c), _MXU_DT),
        compiler_params=_cparams(("parallel",)),
    )(u, vn, w, bcol)


def _sg_mix_bwd(u, vn, w, bcol, dub):
    s, c = u.shape
    gw = c // SG_GROUPS

    def body(u_ref, v_ref, w_ref, b_ref, d_ref, du_ref, dv_ref, dw_ref, db_ref):
        mask = _tril_mask()
        wm = jnp.where(mask, w_ref[0], 0.0).astype(_MXU_DT)
        dw = jnp.zeros((SG_CHUNK, SG_CHUNK), F32)
        db = jnp.zeros((SG_CHUNK, 1), F32)
        for n in range(s // SG_CHUNK):
            rows = slice(n * SG_CHUNK, (n + 1) * SG_CHUNK)
            v = v_ref[rows, :]
            d = d_ref[rows, :].astype(F32)
            mixed = jnp.dot(wm, v, preferred_element_type=F32) + b_ref[0]
            du_ref[rows, :] = d * mixed
            dmix = d * u_ref[rows, :]
            dmix_lo = dmix.astype(_MXU_DT)
            dv_ref[rows, :] = lax.dot_general(wm, dmix_lo, (((0,), (0,)), ((), ())), preferred_element_type=F32)
            dw = dw + lax.dot_general(dmix_lo, v, (((1,), (1,)), ((), ())), preferred_element_type=F32)
            db = db + jnp.sum(dmix, axis=1, keepdims=True)
        dw_ref[0] = jnp.where(mask, dw, 0.0)
        db_ref[0] = db

    return pl.pallas_call(
        body, name="sg_mix_bwd", grid=(SG_GROUPS,),
        in_specs=[pl.BlockSpec((s, gw), lambda g: (0, g)), pl.BlockSpec((s, gw), lambda g: (0, g)),
                  pl.BlockSpec((1, SG_CHUNK, SG_CHUNK), lambda g: (g, 0, 0)),
                  pl.BlockSpec((1, SG_CHUNK, 1), lambda g: (g, 0, 0)), pl.BlockSpec((s, gw), lambda g: (0, g))],
        out_specs=[pl.BlockSpec((s, gw), lambda g: (0, g)), pl.BlockSpec((s, gw), lambda g: (0, g)),
                   pl.BlockSpec((1, SG_CHUNK, SG_CHUNK), lambda g: (g, 0, 0)),
                   pl.BlockSpec((1, SG_CHUNK, 1), lambda g: (g, 0, 0))],
        out_shape=[jax.ShapeDtypeStruct((s, c), F32), jax.ShapeDtypeStruct((s, c), F32),
                   jax.ShapeDtypeStruct((SG_GROUPS, SG_CHUNK, SG_CHUNK), F32),
                   jax.ShapeDtypeStruct((SG_GROUPS, SG_CHUNK, 1), F32)],
        compiler_params=_cparams(("parallel",)),
    )(u, vn, w, bcol, dub)


def _rope_fwd(q2, kv2, p, krm_idx, krs_idx, tc, ts):
    s = q2.shape[0]
    hw = N_HEADS * HEAD_PAD
    tm = 256

    def body(qm_ref, qs_ref, kn_ref, v_ref, krm_ref, krs_ref, tc_ref, ts_ref, q_ref, k_ref, vo_ref):
        tcv, tsv = tc_ref[...], ts_ref[...]
        kpe = krm_ref[...] * tcv + krs_ref[...] * tsv
        for h in range(N_HEADS):
            cols = slice(h * HEAD_PAD, (h + 1) * HEAD_PAD)
            q_ref[:, cols] = (qm_ref[:, cols] * tcv + qs_ref[:, cols] * tsv).astype(q_ref.dtype)
            k_ref[:, cols] = (kn_ref[:, cols] + kpe).astype(k_ref.dtype)
        vo_ref[...] = v_ref[...].astype(vo_ref.dtype)

    return pl.pallas_call(
        body, name="rope_fwd", grid=(s // tm,),
        in_specs=[_row_spec(tm, hw, 0), _row_spec(tm, hw, 1), _row_spec(tm, hw, 0), _row_spec(tm, hw, 1),
                  _row_spec(tm, HEAD_PAD, krm_idx), _row_spec(tm, HEAD_PAD, krs_idx),
                  _row_spec(tm, HEAD_PAD, 0), _row_spec(tm, HEAD_PAD, 0)],
        out_specs=[_row_spec(tm, hw, 0)] * 3,
        out_shape=[jax.ShapeDtypeStruct((s, hw), _MXU_DT)] * 3,
        compiler_params=_cparams(("parallel",)),
    )(q2, q2, kv2, kv2, p, p, tc, ts)


def _rope_bwd(dq, dk, dv, tc, ts):
    s = dq.shape[0]
    hw = N_HEADS * HEAD_PAD
    tm = 256

    def body(dq_ref, dk_ref, dv_ref, tc_ref, ts_ref, dq2_ref, dkv2_ref, dkrm_ref, dkrs_ref):
        tcv, tsv = tc_ref[...], ts_ref[...]
        dkpe = jnp.zeros((tm, HEAD_PAD), F32)
        for h in range(N_HEADS):
            cols = slice(h * HEAD_PAD, (h + 1) * HEAD_PAD)
            dqh = dq_ref[:, cols]
            dq2_ref[:, cols] = (dqh * tcv).astype(dq2_ref.dtype)
            dq2_ref[:, hw + h * HEAD_PAD:hw + (h + 1) * HEAD_PAD] = (dqh * tsv).astype(dq2_ref.dtype)
            dkpe = dkpe + dk_ref[:, cols]
        dkv2_ref[:, 0:hw] = dk_ref[...].astype(dkv2_ref.dtype)
        dkv2_ref[:, hw:2 * hw] = dv_ref[...].astype(dkv2_ref.dtype)
        dkrm_ref[...] = (dkpe * tcv).astype(dkrm_ref.dtype)
        dkrs_ref[...] = (dkpe * tsv).astype(dkrs_ref.dtype)

    return pl.pallas_call(
        body, name="rope_bwd", grid=(s // tm,),
        in_specs=[_row_spec(tm, hw, 0)] * 3 + [_row_spec(tm, HEAD_PAD, 0)] * 2,
        out_specs=[_row_spec(tm, 2 * hw, 0), _row_spec(tm, 2 * hw, 0), _row_spec(tm, HEAD_PAD, 0),
                   _row_spec(tm, HEAD_PAD, 0)],
        out_shape=[jax.ShapeDtypeStruct((s, 2 * hw), _MXU_DT), jax.ShapeDtypeStruct((s, 2 * hw), _MXU_DT),
                   jax.ShapeDtypeStruct((s, HEAD_PAD), _MXU_DT), jax.ShapeDtypeStruct((s, HEAD_PAD), _MXU_DT)],
        compiler_params=_cparams(("parallel",)),
    )(dq, dk, dv, tc, ts)


_ATTN_TQ = 256
_ATTN_SCALE = (QK_NOPE + QK_ROPE) ** -0.5


def _attn_probs(q, k, i, s):
    sc = lax.dot_general(q, k, (((1,), (1,)), ((), ())), preferred_element_type=F32) * _ATTN_SCALE
    row = i * _ATTN_TQ + lax.broadcasted_iota(jnp.int32, (_ATTN_TQ, s), 0)
    col = lax.broadcasted_iota(jnp.int32, (_ATTN_TQ, s), 1)
    sc = jnp.where(row >= col, sc, jnp.finfo(F32).min)
    e = jnp.exp(sc - jnp.max(sc, axis=1, keepdims=True))
    return e / jnp.sum(e, axis=1, keepdims=True)


def _attn_fwd(q, k, v):
    s = q.shape[0]

    def body(q_ref, k_ref, v_ref, o_ref):
        p = _attn_probs(q_ref[...], k_ref[...], pl.program_id(1), s)
        o_ref[...] = jnp.dot(p.astype(_MXU_DT), v_ref[...], preferred_element_type=F32).astype(o_ref.dtype)

    qspec = pl.BlockSpec((_ATTN_TQ, HEAD_PAD), lambda h, i: (i, h))
    kspec = pl.BlockSpec((s, HEAD_PAD), lambda h, i: (0, h))
    return pl.pallas_call(
        body, name="attn_fwd", grid=(N_HEADS, s // _ATTN_TQ),
        in_specs=[qspec, kspec, kspec], out_specs=qspec,
        out_shape=jax.ShapeDtypeStruct(q.shape, _MXU_DT),
        compiler_params=_cparams(("parallel", "parallel")),
    )(q, k, v)


def _attn_bwd(q, k, v, do):
    s = q.shape[0]

    def body(q_ref, k_ref, v_ref, do_ref, dq_ref, dk_ref, dv_ref):
        i = pl.program_id(1)

        @pl.when(i == 0)
        def _():
            dk_ref[...] = jnp.zeros_like(dk_ref)
            dv_ref[...] = jnp.zeros_like(dv_ref)

        qv, kv, dov = q_ref[...], k_ref[...], do_ref[...]
        p = _attn_probs(qv, kv, i, s)
        dp = lax.dot_general(dov, v_ref[...], (((1,), (1,)), ((), ())), preferred_element_type=F32)
        delta = jnp.sum(p * dp, axis=1, keepdims=True)
        ds = (p * (dp - delta) * _ATTN_SCALE).astype(_MXU_DT)
        dq_ref[...] = jnp.dot(ds, kv, preferred_element_type=F32)
        dk_ref[...] += lax.dot_general(ds, qv, (((0,), (0,)), ((), ())), preferred_element_type=F32)
        dv_ref[...] += lax.dot_general(p.astype(_MXU_DT), dov, (((0,), (0,)), ((), ())), preferred_element_type=F32)

    qspec = pl.BlockSpec((_ATTN_TQ, HEAD_PAD), lambda h, i: (i, h))
    kspec = pl.BlockSpec((s, HEAD_PAD), lambda h, i: (0, h))
    return pl.pallas_call(
        body, name="attn_bwd", grid=(N_HEADS, s // _ATTN_TQ),
        in_specs=[qspec, kspec, kspec, qspec], out_specs=[qspec, kspec, kspec],
        out_shape=[jax.ShapeDtypeStruct(q.shape, F32)] * 3,
        compiler_params=_cparams(("parallel", "arbitrary")),
    )(q, k, v, do)


def _loss_head(y, target):
    s, d = y.shape
    tm = 256

    def body(y_ref, t_ref, loss_ref, dy_ref):
        @pl.when(pl.program_id(0) == 0)
        def _():
            loss_ref[...] = jnp.zeros_like(loss_ref)

        err = y_ref[...] - t_ref[...]
        loss_ref[...] += 0.5 * jnp.sum(jnp.mean(err * err, axis=-1, keepdims=True), axis=0, keepdims=True)
        dy_ref[...] = err * (1.0 / d)

    return pl.pallas_call(
        body, name="loss_head", grid=(s // tm,),
        in_specs=[_row_spec(tm, d, 0), _row_spec(tm, d, 0)],
        out_specs=[_full_spec((1, 1)), _row_spec(tm, d, 0)],
        out_shape=[jax.ShapeDtypeStruct((1, 1), F32), jax.ShapeDtypeStruct((s, d), F32)],
        compiler_params=_cparams(("arbitrary",)),
    )(y, target)


def _adamw(name, w, g, m, v):
    r = w.shape[0]
    tr = _pick(r, (1024, 512, 256, 128, 64, 32, 16, 8))

    def body(w_ref, g_ref, m_ref, v_ref, d_ref, mo_ref, vo_ref):
        gv = g_ref[...]
        mn = ADAM_B1 * m_ref[...] + (1.0 - ADAM_B1) * gv
        vn = ADAM_B2 * v_ref[...] + (1.0 - ADAM_B2) * (gv * gv)
        m_hat = mn / (1.0 - ADAM_B1 ** ADAM_STEP)
        v_hat = vn / (1.0 - ADAM_B2 ** ADAM_STEP)
        d_ref[...] = -ADAM_LR * (m_hat / (jnp.sqrt(v_hat) + ADAM_EPS) + ADAM_WD * w_ref[...])
        mo_ref[...] = mn
        vo_ref[...] = vn

    spec = pl.BlockSpec((tr, _LANES), lambda i: (i, 0))
    return pl.pallas_call(
        body, name=name, grid=(r // tr,), in_specs=[spec] * 4, out_specs=[spec] * 3,
        out_shape=[jax.ShapeDtypeStruct((r, _LANES), F32)] * 3,
        compiler_params=_cparams(("parallel",)),
    )(w, g, m, v)


_ANY = pl.BlockSpec(memory_space=pl.ANY)


def _mesh_pos():
    return lax.axis_index("x"), lax.axis_index("y"), lax.axis_index("c")


def _other_chips(x, y):
    return [(1 - x, y), (x, 1 - y), (1 - x, 1 - y)]


def _all_gather_xy(shard):
    r = shard.shape[0]
    half = r // 2

    def body(x_ref, out_ref, send_sems, recv_sems, local_sem):
        x, y, c = _mesh_pos()
        sibling = (x, y, 1 - c)
        chips = _other_chips(x, y)

        def blk(chip, hc):
            return out_ref.at[2 * chip[0] + chip[1], pl.ds(hc * half, half), :]

        def copy(k, src, dst, to):
            return pltpu.make_async_remote_copy(src_ref=src, dst_ref=dst, send_sem=send_sems.at[k],
                                                recv_sem=recv_sems.at[k], device_id=to, device_id_type=_MESH)

        mine = pltpu.make_async_copy(x_ref, out_ref.at[2 * x + y], local_sem)
        mine.start()
        my_half = x_ref.at[pl.ds(c * half, half), :]
        first = [copy(j, my_half, blk((x, y), c), (*chip, c)) for j, chip in enumerate(chips)]
        for cp in first:
            cp.start()
        passed = [copy(3 + j, blk(chip, c), blk(chip, c), sibling) for j, chip in enumerate(chips)]
        for j, chip in enumerate(chips):
            copy(j, my_half, blk(chip, c), (*chip, c)).wait_recv()
            passed[j].start()
        for j, chip in enumerate(chips):
            copy(3 + j, my_half, blk(chip, 1 - c), sibling).wait_recv()
        for cp in first + passed:
            cp.wait_send()
        mine.wait()

    return pl.pallas_call(
        body, name="all_gather_xy", in_specs=[_ANY], out_specs=_ANY,
        out_shape=jax.ShapeDtypeStruct((4, r, _LANES), shard.dtype),
        scratch_shapes=[pltpu.SemaphoreType.DMA((6,)), pltpu.SemaphoreType.DMA((6,)), pltpu.SemaphoreType.DMA],
    )(shard)


def _swap_halves(g):
    _, _, half, _ = g.shape

    def body(g_ref, out_ref, send_sem, recv_sem):
        x, y, c = _mesh_pos()
        cp = pltpu.make_async_remote_copy(src_ref=g_ref.at[:, 1 - c], dst_ref=out_ref, send_sem=send_sem,
                                          recv_sem=recv_sem, device_id=(x, y, 1 - c), device_id_type=_MESH)
        cp.start()
        cp.wait()

    return pl.pallas_call(
        body, name="rs_swap_halves", in_specs=[_ANY], out_specs=_ANY,
        out_shape=jax.ShapeDtypeStruct((4, half, _LANES), g.dtype),
        scratch_shapes=[pltpu.SemaphoreType.DMA, pltpu.SemaphoreType.DMA],
    )(g)


def _scatter_chips(t):
    _, half, _ = t.shape

    def body(t_ref, out_ref, send_sems, recv_sems):
        x, y, c = _mesh_pos()
        copies = []
        for k, chip in enumerate(_other_chips(x, y)):
            cp = pltpu.make_async_remote_copy(src_ref=t_ref.at[2 * chip[0] + chip[1]], dst_ref=out_ref.at[k],
                                              send_sem=send_sems.at[k], recv_sem=recv_sems.at[k],
                                              device_id=(*chip, c), device_id_type=_MESH)
            cp.start()
            copies.append(cp)
        for cp in copies:
            cp.wait()

    return pl.pallas_call(
        body, name="rs_scatter_chips", in_specs=[_ANY], out_specs=_ANY,
        out_shape=jax.ShapeDtypeStruct((3, half, _LANES), t.dtype),
        scratch_shapes=[pltpu.SemaphoreType.DMA((3,)), pltpu.SemaphoreType.DMA((3,))],
    )(t)


def _join_halves(u):
    half = u.shape[0]

    def body(u_ref, out_ref, send_sem, recv_sem, local_sem):
        x, y, c = _mesh_pos()
        mine = pltpu.make_async_copy(u_ref, out_ref.at[c], local_sem)
        mine.start()
        cp = pltpu.make_async_remote_copy(src_ref=u_ref, dst_ref=out_ref.at[c], send_sem=send_sem,
                                          recv_sem=recv_sem, device_id=(x, y, 1 - c), device_id_type=_MESH)
        cp.start()
        cp.wait()
        mine.wait()

    return pl.pallas_call(
        body, name="rs_join_halves", in_specs=[_ANY], out_specs=_ANY,
        out_shape=jax.ShapeDtypeStruct((2, half, _LANES), u.dtype),
        scratch_shapes=[pltpu.SemaphoreType.DMA, pltpu.SemaphoreType.DMA, pltpu.SemaphoreType.DMA],
    )(u)


def _add_half(g, a, c_arr):
    _, _, half, _ = g.shape
    tr = _pick(half, (1024, 512, 256, 128, 64, 32, 16, 8))

    def body(c_ref, g_ref, a_ref, o_ref):
        o_ref[...] = g_ref[0] + a_ref[...]

    return pl.pallas_call(
        body, name="rs_add_half",
        grid_spec=pltpu.PrefetchScalarGridSpec(
            num_scalar_prefetch=1, grid=(4, half // tr),
            in_specs=[pl.BlockSpec((1, 1, tr, _LANES), lambda j, i, c_ref: (j, c_ref[0], i, 0)),
                      pl.BlockSpec((1, tr, _LANES), lambda j, i, c_ref: (j, i, 0))],
            out_specs=pl.BlockSpec((1, tr, _LANES), lambda j, i, c_ref: (j, i, 0))),
        out_shape=jax.ShapeDtypeStruct((4, half, _LANES), F32),
        compiler_params=_cparams(("parallel", "parallel")),
    )(c_arr, g, a)


def _add_chips(t, b, chip_arr):
    _, half, _ = t.shape
    tr = _pick(half, (1024, 512, 256, 128, 64, 32, 16, 8))

    def body(chip_ref, t_ref, b_ref, o_ref):
        o_ref[...] = ((t_ref[0] + b_ref[0]) + b_ref[1]) + b_ref[2]

    return pl.pallas_call(
        body, name="rs_add_chips",
        grid_spec=pltpu.PrefetchScalarGridSpec(
            num_scalar_prefetch=1, grid=(half // tr,),
            in_specs=[pl.BlockSpec((1, tr, _LANES), lambda i, chip_ref: (chip_ref[0], i, 0)),
                      pl.BlockSpec((3, tr, _LANES), lambda i, chip_ref: (0, i, 0))],
            out_specs=pl.BlockSpec((tr, _LANES), lambda i, chip_ref: (i, 0))),
        out_shape=jax.ShapeDtypeStruct((half, _LANES), F32),
        compiler_params=_cparams(("parallel",)),
    )(chip_arr, t, b)


def _reduce_scatter(g):
    _, r, _ = g.shape
    half = r // 2
    x, y, c = _mesh_pos()
    g = g.reshape(4, 2, half, _LANES)
    a = _swap_halves(g)
    t = _add_half(g, a, jnp.reshape(c, (1,)).astype(jnp.int32))
    b = _scatter_chips(t)
    u = _add_chips(t, b, jnp.reshape(2 * x + y, (1,)).astype(jnp.int32))
    return _join_halves(u).reshape(r, _LANES)


def _all_reduce_small(v):
    r = v.shape[0]

    def body(v_ref, out_ref, slots, send_sems, recv_sems):
        x, y, c = _mesh_pos()
        me = 4 * x + 2 * y + c
        slots[me] = v_ref[...]
        copies = []
        for rel in range(1, 8):
            bx, by, bc = (rel >> 2) & 1, (rel >> 1) & 1, rel & 1
            peer = (1 - x if bx else x, 1 - y if by else y, 1 - c if bc else c)
            cp = pltpu.make_async_remote_copy(src_ref=v_ref, dst_ref=slots.at[me], send_sem=send_sems.at[rel - 1],
                                              recv_sem=recv_sems.at[rel - 1], device_id=peer, device_id_type=_MESH)
            cp.start()
            copies.append(cp)
        for cp in copies:
            cp.wait()
        acc = slots[0]
        for d in range(1, 8):
            acc = acc + slots[d]
        out_ref[...] = acc

    vmem = pl.BlockSpec(memory_space=pltpu.VMEM)
    return pl.pallas_call(
        body, name="all_reduce_small", in_specs=[vmem], out_specs=vmem,
        out_shape=jax.ShapeDtypeStruct((r, _LANES), F32),
        scratch_shapes=[pltpu.VMEM((8, r, _LANES), F32), pltpu.SemaphoreType.DMA((7,)), pltpu.SemaphoreType.DMA((7,))],
        compiler_params=pltpu.CompilerParams(vmem_limit_bytes=_VMEM_LIMIT),
    )(v)


def _pack(arrays, row_multiple, dtype):
    flat = jnp.concatenate([a.reshape(-1).astype(dtype) for a in arrays])
    unit = row_multiple * _LANES
    padded = -(-flat.shape[0] // unit) * unit
    return jnp.pad(flat, (0, padded - flat.shape[0])).reshape(-1, _LANES)


def _unpack(slab, shapes):
    flat = slab.reshape(-1)
    out, pos = [], 0
    for shp in shapes:
        n = math.prod(shp)
        out.append(flat[pos:pos + n].reshape(shp))
        pos += n
    return out


def _swap_rope(a):
    h = QK_ROPE // 2
    return jnp.concatenate([a[..., h:], a[..., :h]], axis=-1)


class _InLayout:
    def __init__(self, d):
        self.d = d
        self.gates = 0
        self.a = 3 * d
        self.b = 4 * d
        self.kv = 5 * d
        self.q = self.kv + 256
        self.krm = self.q + 384
        self.krs = self.krm + HEAD_PAD
        self.width = self.krs + 2 * HEAD_PAD


def _prep_layer(wl, d):
    lay = _InLayout(d)
    w_in = wl["w_in"]
    dt = w_in.dtype
    a, b = w_in[:, 0:d], w_in[:, d:2 * d]
    q, kv = w_in[:, 2 * d:2 * d + 384], w_in[:, 2 * d + 384:2 * d + 640]
    kr = w_in[:, 2 * d + 640:2 * d + 640 + QK_ROPE]
    gates = w_in[:, 2 * d + 640 + QK_ROPE:]
    z = lambda n: jnp.zeros((d, n), dt)
    krm = jnp.concatenate([z(QK_NOPE), kr, z(HEAD_PAD - QK_NOPE - QK_ROPE)], axis=1)
    krs = jnp.concatenate([z(QK_NOPE), _swap_rope(kr), z(HEAD_PAD - QK_NOPE - QK_ROPE)], axis=1)
    out = dict(wl)
    out["w_in"] = jnp.concatenate([gates, a, b, kv, q, krm, krs, z(lay.width - lay.krs - HEAD_PAD)], axis=1)
    uq = wl["mla_w_uq"].reshape(-1, N_HEADS, QK_NOPE + QK_ROPE)
    nq = uq.shape[0]
    nope, pe = uq[..., :QK_NOPE], uq[..., QK_NOPE:]
    zq = lambda n: jnp.zeros((nq, N_HEADS, n), dt)
    main = jnp.concatenate([nope, pe, zq(HEAD_PAD - QK_NOPE - QK_ROPE)], axis=-1).reshape(nq, -1)
    swapped = jnp.concatenate([zq(QK_NOPE), _swap_rope(pe), zq(HEAD_PAD - QK_NOPE - QK_ROPE)], axis=-1).reshape(nq, -1)
    out["mla_w_uq"] = jnp.concatenate([main, swapped], axis=1)
    ukv = wl["mla_w_ukv"].reshape(-1, N_HEADS, QK_NOPE + V_HEAD)
    nkv = ukv.shape[0]
    zk = jnp.zeros((nkv, N_HEADS, HEAD_PAD - QK_NOPE), dt)
    zv = jnp.zeros((nkv, N_HEADS, HEAD_PAD - V_HEAD), dt)
    out["mla_w_ukv"] = jnp.concatenate([jnp.concatenate([ukv[..., :QK_NOPE], zk], axis=-1).reshape(nkv, -1),
                                        jnp.concatenate([ukv[..., QK_NOPE:], zv], axis=-1).reshape(nkv, -1)], axis=1)
    wo = wl["mla_w_o"].reshape(N_HEADS, V_HEAD, -1)
    out["mla_w_o"] = jnp.concatenate([wo, jnp.zeros((N_HEADS, HEAD_PAD - V_HEAD, wo.shape[-1]), dt)],
                                     axis=1).reshape(N_HEADS * HEAD_PAD, -1)
    return out


def _unprep_grads(g, d):
    lay = _InLayout(d)
    gi = g["w_in"]
    kr = (gi[:, lay.krm + QK_NOPE:lay.krm + QK_NOPE + QK_ROPE]
          + _swap_rope(gi[:, lay.krs + QK_NOPE:lay.krs + QK_NOPE + QK_ROPE]))
    out = dict(g)
    out["w_in"] = jnp.concatenate([gi[:, lay.a:lay.a + d], gi[:, lay.b:lay.b + d], gi[:, lay.q:lay.q + 384],
                                   gi[:, lay.kv:lay.kv + 256], kr, gi[:, lay.gates:lay.gates + 3 * d]], axis=1)
    hw = N_HEADS * HEAD_PAD
    gq = g["mla_w_uq"]
    nq = gq.shape[0]
    main = gq[:, :hw].reshape(nq, N_HEADS, HEAD_PAD)
    swapped = gq[:, hw:].reshape(nq, N_HEADS, HEAD_PAD)
    pe = main[..., QK_NOPE:QK_NOPE + QK_ROPE] + _swap_rope(swapped[..., QK_NOPE:QK_NOPE + QK_ROPE])
    out["mla_w_uq"] = jnp.concatenate([main[..., :QK_NOPE], pe], axis=-1).reshape(nq, -1)
    gkv = g["mla_w_ukv"]
    nkv = gkv.shape[0]
    out["mla_w_ukv"] = jnp.concatenate([gkv[:, :hw].reshape(nkv, N_HEADS, HEAD_PAD)[..., :QK_NOPE],
                                        gkv[:, hw:].reshape(nkv, N_HEADS, HEAD_PAD)[..., :V_HEAD]],
                                       axis=-1).reshape(nkv, -1)
    go = g["mla_w_o"]
    out["mla_w_o"] = go.reshape(N_HEADS, HEAD_PAD, -1)[:, :V_HEAD].reshape(N_HEADS * V_HEAD, -1)
    return out


def _rope_tables(positions):
    s = positions.shape[0]
    inv = ROPE_THETA ** (-jnp.arange(0, QK_ROPE, 2, dtype=F32) / QK_ROPE)
    ang = positions.astype(F32)[:, None] * inv
    cos, sin = jnp.cos(ang), jnp.sin(ang)
    tail = jnp.zeros((s, HEAD_PAD - QK_NOPE - QK_ROPE), F32)
    tc = jnp.concatenate([jnp.ones((s, QK_NOPE), F32), cos, cos, tail], axis=1)
    ts = jnp.concatenate([jnp.zeros((s, QK_NOPE), F32), -sin, sin, tail], axis=1)
    return tc, ts


def _row(v):
    return v.reshape(1, -1)


def _layer_fwd(x, h, w, g_next, tc, ts):
    d = x.shape[1]
    lay = _InLayout(d)
    cw = d // 2
    blk = lambda off, width: off // width
    p = _mm("mm_in", h, w["w_in"])
    z1 = _glu_conv_fwd(p, blk(lay.a, _LANES), blk(lay.a + cw, _LANES), w["conv_dw_w"], _row(w["conv_dw_b"]))
    ln_a = [_row(w["conv_ln_g"]), _row(w["conv_ln_b"])]
    (z3,) = _row_fwd("ln_silu_fwd", _f_ln_silu, [(z1, cw, 0)], ln_a, [(cw, _MXU_DT)])
    ya = _mm("mm_conv_out", z3, w["conv_out_w"])
    ln_b = [_row(w["sg_ln_g"]), _row(w["sg_ln_b"])]
    u, vn = _row_fwd("sg_pre_fwd", _f_sg_pre, [(p, cw, blk(lay.b, cw)), (p, cw, blk(lay.b + cw, cw))], ln_b,
                     [(cw, F32), (cw, _MXU_DT)])
    bcol = w["sg_b"].reshape(SG_GROUPS, SG_CHUNK, 1)
    ub = _sg_mix_fwd(u, vn, w["sg_w"], bcol)
    yb = _mm("mm_sg_out", ub, w["sg_out_w"])
    (qn,) = _row_fwd("q_norm_fwd", _f_rms, [(p, 384, blk(lay.q, 384))], [_row(w["mla_q_norm_g"])], [(384, _MXU_DT)])
    (kvn,) = _row_fwd("kv_norm_fwd", _f_rms, [(p, 256, blk(lay.kv, 256))], [_row(w["mla_kv_norm_g"])],
                      [(256, _MXU_DT)])
    q2 = _mm("mm_uq", qn, w["mla_w_uq"])
    kv2 = _mm("mm_ukv", kvn, w["mla_w_ukv"])
    qf, kf, vf = _rope_fwd(q2, kv2, p, blk(lay.krm, HEAD_PAD), blk(lay.krs, HEAD_PAD), tc, ts)
    o = _attn_fwd(qf, kf, vf)
    yc = _mm("mm_o", o, w["mla_w_o"])
    gate_rows = [(p, d, 0), (p, d, 1), (p, d, 2)]
    (merged,) = _row_fwd("merge_fwd", _f_merge, gate_rows + [(ya, d, 0), (yb, d, 0), (yc, d, 0)], [], [(d, _MXU_DT)])
    t = _mm("mm_out", merged, w["w_out"])
    x1, h2 = _row_fwd("resid_mix_fwd", _f_resid_rms_rms, [(x, d, 0), (t, d, 0)],
                      [_row(w["mix_post_g"]), _row(w["ffn_pre_g"])], [(d, F32), (d, _MXU_DT)])
    up = _mm("mm_up", h2, w["ffn_w_up"])
    act = _conv_geglu_fwd(up, w["ffn_dw_w"], _row(w["ffn_dw_b"]))
    dn = _mm("mm_down", act, w["ffn_w_down"])
    if g_next is None:
        (x2,) = _row_fwd("resid_ffn_last_fwd", _f_resid_rms, [(x1, d, 0), (dn, d, 0)], [_row(w["ffn_post_g"])],
                         [(d, F32)])
        h_next = None
    else:
        x2, h_next = _row_fwd("resid_ffn_fwd", _f_resid_rms_rms, [(x1, d, 0), (dn, d, 0)],
                              [_row(w["ffn_post_g"]), _row(g_next)], [(d, F32), (d, _MXU_DT)])
    saved = dict(x=x, h=h, p=p, z1=z1, z3=z3, ya=ya, u=u, vn=vn, ub=ub, yb=yb, qn=qn, kvn=kvn, qf=qf, kf=kf, vf=vf, o=o,
                 yc=yc, merged=merged, t=t, x1=x1, h2=h2, up=up, act=act, dn=dn, bcol=bcol)
    return x2, h_next, saved


def _layer_bwd(dx2, dh_next, w, g_next, sv, tc, ts):
    d = dx2.shape[1]
    lay = _InLayout(d)
    cw = d // 2
    blk = lambda off, width: off // width
    lo = _MXU_DT
    g = {}
    x1, dn = sv["x1"], sv["dn"]
    if dh_next is None:
        dx1, ddn, g["ffn_post_g"] = _row_bwd("resid_ffn_last_bwd", _f_resid_rms, [(x1, d, 0), (dn, d, 0)],
                                             [_row(w["ffn_post_g"])], [(dx2, d, 0)], [(0, F32), (1, lo)], [0])
    else:
        dx1, ddn, g["ffn_post_g"], g["next_pre_g"] = _row_bwd(
            "resid_ffn_bwd", _f_resid_rms_rms, [(x1, d, 0), (dn, d, 0)], [_row(w["ffn_post_g"]), _row(g_next)],
            [(dx2, d, 0), (dh_next, d, 0)], [(0, F32), (1, lo)], [0, 1])
    dact = _mm("mm_down_dx", ddn, w["ffn_w_down"], tb=True)
    g["ffn_w_down"] = _mm("mm_down_dw", sv["act"], ddn, ta=True)
    dup, g["ffn_dw_w"], g["ffn_dw_b"] = _conv_geglu_bwd(sv["up"], w["ffn_dw_w"], _row(w["ffn_dw_b"]), dact)
    dh2 = _mm("mm_up_dx", dup, w["ffn_w_up"], tb=True)
    g["ffn_w_up"] = _mm("mm_up_dw", sv["h2"], dup, ta=True)
    dx, dt, g["mix_post_g"], g["ffn_pre_g"] = _row_bwd(
        "resid_mix_bwd", _f_resid_rms_rms, [(sv["x"], d, 0), (sv["t"], d, 0)],
        [_row(w["mix_post_g"]), _row(w["ffn_pre_g"])], [(dx1, d, 0), (dh2, d, 0)], [(0, F32), (1, lo)], [0, 1])
    dmerged = _mm("mm_out_dx", dt, w["w_out"], tb=True)
    g["w_out"] = _mm("mm_out_dw", sv["merged"], dt, ta=True)
    p = sv["p"]
    gate_rows = [(p, d, 0), (p, d, 1), (p, d, 2)]
    dg0, dg1, dg2, dya, dyb, dyc = _row_bwd(
        "merge_bwd", _f_merge, gate_rows + [(sv["ya"], d, 0), (sv["yb"], d, 0), (sv["yc"], d, 0)], [],
        [(dmerged, d, 0)], [(i, lo) for i in range(6)], [])
    do = _mm("mm_o_dx", dyc, w["mla_w_o"], tb=True, out_dtype=lo)
    g["mla_w_o"] = _mm("mm_o_dw", sv["o"], dyc, ta=True)
    dqf, dkf, dvf = _attn_bwd(sv["qf"], sv["kf"], sv["vf"], do)
    dq2, dkv2, dkrm, dkrs = _rope_bwd(dqf, dkf, dvf, tc, ts)
    dkvn = _mm("mm_ukv_dx", dkv2, w["mla_w_ukv"], tb=True)
    g["mla_w_ukv"] = _mm("mm_ukv_dw", sv["kvn"], dkv2, ta=True)
    dqn = _mm("mm_uq_dx", dq2, w["mla_w_uq"], tb=True)
    g["mla_w_uq"] = _mm("mm_uq_dw", sv["qn"], dq2, ta=True)
    dq_lat, g["mla_q_norm_g"] = _row_bwd("q_norm_bwd", _f_rms, [(p, 384, blk(lay.q, 384))],
                                         [_row(w["mla_q_norm_g"])], [(dqn, 384, 0)], [(0, lo)], [0])
    dkv_lat, g["mla_kv_norm_g"] = _row_bwd("kv_norm_bwd", _f_rms, [(p, 256, blk(lay.kv, 256))],
                                           [_row(w["mla_kv_norm_g"])], [(dkvn, 256, 0)], [(0, lo)], [0])
    dub = _mm("mm_sg_out_dx", dyb, w["sg_out_w"], tb=True)
    g["sg_out_w"] = _mm("mm_sg_out_dw", sv["ub"], dyb, ta=True)
    du, dvn, g["sg_w"], dbcol = _sg_mix_bwd(sv["u"], sv["vn"], w["sg_w"], sv["bcol"], dub)
    g["sg_b"] = dbcol.reshape(SG_GROUPS, SG_CHUNK)
    dbu, dbv, g["sg_ln_g"], g["sg_ln_b"] = _row_bwd(
        "sg_pre_bwd", _f_sg_pre, [(p, cw, blk(lay.b, cw)), (p, cw, blk(lay.b + cw, cw))],
        [_row(w["sg_ln_g"]), _row(w["sg_ln_b"])], [(du, cw, 0), (dvn, cw, 0)], [(0, lo), (1, lo)], [0, 1])
    dz3 = _mm("mm_conv_out_dx", dya, w["conv_out_w"], tb=True)
    g["conv_out_w"] = _mm("mm_conv_out_dw", sv["z3"], dya, ta=True)
    dz1, g["conv_ln_g"], g["conv_ln_b"] = _row_bwd(
        "ln_silu_bwd", _f_ln_silu, [(sv["z1"], cw, 0)], [_row(w["conv_ln_g"]), _row(w["conv_ln_b"])],
        [(dz3, cw, 0)], [(0, F32)], [0, 1])
    dval, dgate, g["conv_dw_w"], g["conv_dw_b"] = _glu_conv_bwd(p, blk(lay.a, _LANES), blk(lay.a + cw, _LANES),
                                                                w["conv_dw_w"], dz1)
    dp = jnp.concatenate([dg0, dg1, dg2, dval, dgate, dbu, dbv, dkv_lat, dq_lat, dkrm, dkrs, jnp.zeros_like(dkrs)],
                         axis=1)
    dh = _mm("mm_in_dx", dp, w["w_in"], tb=True)
    g["w_in"] = _mm("mm_in_dw", sv["h"], dp, ta=True)
    return dx, dh, g


def _local_step(x, positions, target, layers):
    d = x.shape[1]
    tc, ts = _rope_tables(positions)
    ws = [_prep_layer(wl, d) for wl in layers]
    depth = len(ws)
    (h,) = _row_fwd("rms_first_fwd", _f_rms, [(x, d, 0)], [_row(ws[0]["mix_pre_g"])], [(d, _MXU_DT)])
    saved = []
    for l in range(depth):
        g_next = ws[l + 1]["mix_pre_g"] if l + 1 < depth else None
        x, h, sv = _layer_fwd(x, h, ws[l], g_next, tc, ts)
        saved.append(sv)
    loss, dx = _loss_head(x, target)
    grads = [None] * depth
    dh = None
    for l in reversed(range(depth)):
        g_next = ws[l + 1]["mix_pre_g"] if l + 1 < depth else None
        dx, dh, g = _layer_bwd(dx, dh, ws[l], g_next, saved[l], tc, ts)
        if "next_pre_g" in g:
            grads[l + 1]["mix_pre_g"] = g.pop("next_pre_g")
        grads[l] = g
    x0 = saved[0]["x"]
    grad_x, grads[0]["mix_pre_g"] = _row_bwd("rms_first_bwd", _f_x_rms, [(x0, d, 0)], [_row(ws[0]["mix_pre_g"])],
                                             [(dx, d, 0), (dh, d, 0)], [(0, F32)], [0])
    return loss, grad_x, [_unprep_grads(g, d) for g in grads]


_MATRICES = ("w_in", "conv_out_w", "sg_out_w", "mla_w_uq", "mla_w_ukv", "mla_w_o", "w_out", "ffn_w_up", "ffn_w_down")
_F32_GATHERED = ("conv_dw_w", "ffn_dw_w")
_SHARD_ROWS = 2048
_REPL_ROWS = 256


def _gather_weights(wshard):
    names = list(SHARDED)
    hi = {n: wshard[n].astype(jnp.bfloat16) for n in names}
    lo = {n: (wshard[n] - hi[n].astype(F32)).astype(jnp.bfloat16) for n in _F32_GATHERED}
    arrays = [hi[n] for n in names] + [lo[n] for n in _F32_GATHERED]
    slab = _pack(arrays, _SHARD_ROWS, jnp.bfloat16)
    full = _all_gather_xy(slab)
    parts = [_unpack(full[j], [a.shape for a in arrays]) for j in range(4)]
    out = {}
    for i, n in enumerate(names + list(_F32_GATHERED)):
        axis = 1 if n in SHARDED_MID else wshard[n].ndim - 1
        whole = jnp.concatenate([parts[j][i] for j in range(4)], axis=axis)
        out[n] = out[n].astype(F32) + whole.astype(F32) if n in out else whole
    return out


def kernel(x, positions, mix_pre_g, mix_post_g, ffn_pre_g, ffn_post_g, w_in, conv_dw_w, conv_dw_b, conv_ln_g, conv_ln_b, conv_out_w, sg_ln_g, sg_ln_b, sg_w, sg_b, sg_out_w, mla_q_norm_g, mla_w_uq, mla_kv_norm_g, mla_w_ukv, mla_w_o, w_out, ffn_w_up, ffn_dw_w, ffn_dw_b, ffn_w_down, loss_target, m_mix_pre_g, m_mix_post_g, m_ffn_pre_g, m_ffn_post_g, m_w_in, m_conv_dw_w, m_conv_dw_b, m_conv_ln_g, m_conv_ln_b, m_conv_out_w, m_sg_ln_g, m_sg_ln_b, m_sg_w, m_sg_b, m_sg_out_w, m_mla_q_norm_g, m_mla_w_uq, m_mla_kv_norm_g, m_mla_w_ukv, m_mla_w_o, m_w_out, m_ffn_w_up, m_ffn_dw_w, m_ffn_dw_b, m_ffn_w_down, v_mix_pre_g, v_mix_post_g, v_ffn_pre_g, v_ffn_post_g, v_w_in, v_conv_dw_w, v_conv_dw_b, v_conv_ln_g, v_conv_ln_b, v_conv_out_w, v_sg_ln_g, v_sg_ln_b, v_sg_w, v_sg_b, v_sg_out_w, v_mla_q_norm_g, v_mla_w_uq, v_mla_kv_norm_g, v_mla_w_ukv, v_mla_w_o, v_w_out, v_ffn_w_up, v_ffn_dw_w, v_ffn_dw_b, v_ffn_w_down):
    args = dict(locals())
    w = {n: args[n] for n in WEIGHTS}
    m = {n: args["m_" + n] for n in WEIGHTS}
    v = {n: args["v_" + n] for n in WEIGHTS}
    depth = mix_pre_g.shape[0]

    full = _gather_weights({n: w[n] for n in SHARDED})
    layers = []
    for l in range(depth):
        wl = {n: w[n][l] for n in REPLICATED}
        for n in SHARDED:
            wl[n] = full[n][l].astype(_MXU_DT if n in _MATRICES else F32)
        layers.append(wl)

    loss, grad_x, grads = _local_step(x[0], positions[0], loss_target[0], layers)
    loss = lax.psum(loss[0, 0], ("x", "y", "c"))
    gfull = {n: jnp.stack([grads[l][n].reshape(w[n].shape[1:] if n in REPLICATED else full[n].shape[1:])
                           for l in range(depth)]) for n in WEIGHTS}

    by_chip = []
    for j in range(4):
        blocks = []
        for n in SHARDED:
            axis = 1 if n in SHARDED_MID else w[n].ndim - 1
            size = w[n].shape[axis]
            blocks.append(lax.slice_in_dim(gfull[n], j * size, (j + 1) * size, axis=axis))
        by_chip.append(_pack(blocks, _SHARD_ROWS, F32))
    g_sh = _reduce_scatter(jnp.stack(by_chip))
    sh_shapes = [w[n].shape for n in SHARDED]
    pack_sh = lambda t: _pack([t[n] for n in SHARDED], _SHARD_ROWS, F32)
    d_sh, m_sh, v_sh = _adamw("adamw_sharded", pack_sh(w), g_sh, pack_sh(m), pack_sh(v))
    pack_rep = lambda t: _pack([t[n] for n in REPLICATED], _REPL_ROWS, F32)
    g_rep = _all_reduce_small(pack_rep(gfull))
    rep_shapes = [w[n].shape for n in REPLICATED]
    d_rep, m_rep, v_rep = _adamw("adamw_replicated", pack_rep(w), g_rep, pack_rep(m), pack_rep(v))

    def table(sh_slab, rep_slab):
        t = dict(zip(SHARDED, _unpack(sh_slab, sh_shapes)))
        t.update(zip(REPLICATED, _unpack(rep_slab, rep_shapes)))
        return [t[n] for n in WEIGHTS]

    return (loss, grad_x[None], *table(g_sh, g_rep), *table(d_sh, d_rep), *table(m_sh, m_rep), *table(v_sh, v_rep))
```

```python
import functools
import math

import jax
import jax.numpy as jnp
from jax import lax
from jax.experimental import pallas as pl
from jax.experimental.pallas import tpu as pltpu

F32 = jnp.float32
_MXU_DT = jnp.bfloat16
_VMEM_LIMIT = 48 * 1024 * 1024
_LANES = 128
_MESH = pl.DeviceIdType.MESH

N_HEADS = 8
QK_NOPE = 64
QK_ROPE = 32
V_HEAD = 64
HEAD_PAD = 128
SG_GROUPS = 4
SG_CHUNK = 128
CONV_K = 31
FFN_K = 3
ROPE_THETA = 10000.0
EPS = 1e-6
ADAM_LR, ADAM_B1, ADAM_B2, ADAM_EPS, ADAM_WD, ADAM_STEP = 0.001, 0.9, 0.999, 1e-08, 0.01, 10

SHARDED_LAST = ("w_in", "conv_dw_w", "conv_out_w", "sg_out_w", "mla_w_uq", "mla_w_ukv", "mla_w_o", "ffn_w_up",
                "ffn_dw_w")
SHARDED_MID = ("w_out", "ffn_w_down")
SHARDED = SHARDED_LAST + SHARDED_MID
WEIGHTS = ("mix_pre_g", "mix_post_g", "ffn_pre_g", "ffn_post_g", "w_in", "conv_dw_w", "conv_dw_b", "conv_ln_g",
           "conv_ln_b", "conv_out_w", "sg_ln_g", "sg_ln_b", "sg_w", "sg_b", "sg_out_w", "mla_q_norm_g", "mla_w_uq",
           "mla_kv_norm_g", "mla_w_ukv", "mla_w_o", "w_out", "ffn_w_up", "ffn_dw_w", "ffn_dw_b", "ffn_w_down")
REPLICATED = tuple(n for n in WEIGHTS if n not in SHARDED)


def _cparams(sem=None):
    return pltpu.CompilerParams(dimension_semantics=sem, vmem_limit_bytes=_VMEM_LIMIT)


def _pick(n, cands):
    for c in cands:
        if n % c == 0:
            return c
    return n


def _mm(name, a, b, *, ta=False, tb=False, out_dtype=F32):
    (kdim, m) = a.shape if ta else a.shape[::-1]
    (n, kdim2) = b.shape if tb else b.shape[::-1]
    assert kdim == kdim2, (a.shape, b.shape, ta, tb)
    tm = _pick(m, (512, 256, 128))
    tn = _pick(n, (512, 256, 128))
    tk = _pick(kdim, (1024, 512, 256, 128))
    nk = kdim // tk
    dims = (((0 if ta else 1,), (1 if tb else 0,)), ((), ()))

    def body(a_ref, b_ref, o_ref, acc_ref):
        k = pl.program_id(2)

        @pl.when(k == 0)
        def _():
            acc_ref[...] = jnp.zeros_like(acc_ref)

        acc_ref[...] += lax.dot_general(a_ref[...].astype(_MXU_DT), b_ref[...].astype(_MXU_DT), dims,
                                        preferred_element_type=F32)

        @pl.when(k == nk - 1)
        def _():
            o_ref[...] = acc_ref[...].astype(o_ref.dtype)

    a_spec = pl.BlockSpec((tk, tm), lambda i, j, k: (k, i)) if ta else pl.BlockSpec((tm, tk), lambda i, j, k: (i, k))
    b_spec = pl.BlockSpec((tn, tk), lambda i, j, k: (j, k)) if tb else pl.BlockSpec((tk, tn), lambda i, j, k: (k, j))
    return pl.pallas_call(
        body, name=name, grid=(m // tm, n // tn, nk),
        in_specs=[a_spec, b_spec], out_specs=pl.BlockSpec((tm, tn), lambda i, j, k: (i, j)),
        out_shape=jax.ShapeDtypeStruct((m, n), out_dtype),
        scratch_shapes=[pltpu.VMEM((tm, tn), F32)],
        compiler_params=_cparams(("parallel", "parallel", "arbitrary")),
    )(a, b)


def _row_spec(tm, width, idx):
    return pl.BlockSpec((tm, width), lambda i: (i, idx))


def _full_spec(shape):
    zeros = (0,) * len(shape)
    return pl.BlockSpec(shape, lambda i: zeros)


def _row_fwd(name, fn, rows, params, outs, tm=256):
    s = rows[0][0].shape[0]
    nr, npar = len(rows), len(params)

    def body(*refs):
        vals = [r[...].astype(F32) for r in refs[:nr + npar]]
        res = fn(*vals)
        for o_ref, r in zip(refs[nr + npar:], res):
            o_ref[...] = r.astype(o_ref.dtype)

    return pl.pallas_call(
        body, name=name, grid=(s // tm,),
        in_specs=[_row_spec(tm, w, i) for _, w, i in rows] + [_full_spec(p.shape) for p in params],
        out_specs=[_row_spec(tm, w, 0) for w, _ in outs],
        out_shape=[jax.ShapeDtypeStruct((s, w), dt) for w, dt in outs],
        compiler_params=_cparams(("parallel",)),
    )(*[r[0] for r in rows], *params)


def _row_bwd(name, fn, rows, params, cots, row_grads, param_grads, tm=256, place=None, into=None):
    s = rows[0][0].shape[0]
    nr, npar, nc = len(rows), len(params), len(cots)
    row_grads = [((idxs,) if isinstance(idxs, int) else tuple(idxs), dt) for idxs, dt in row_grads]
    widths = [sum(rows[i][1] for i in idxs) for idxs, _ in row_grads]

    def body(*refs):
        i = pl.program_id(0)
        vals = [r[...].astype(F32) for r in refs[:nr + npar]]
        cvals = tuple(r[...].astype(F32) for r in refs[nr + npar:nr + npar + nc])
        _, vjp = jax.vjp(fn, *vals)
        grads = vjp(cvals)
        outs = refs[nr + npar + nc + (into is not None):]
        for o_ref, (idxs, _) in zip(outs, row_grads):
            pos = 0
            for idx in idxs:
                o_ref[:, pos:pos + rows[idx][1]] = grads[idx].astype(o_ref.dtype)
                pos += rows[idx][1]
        for o_ref, idx in zip(outs[len(row_grads):], param_grads):
            @pl.when(i == 0)
            def _(o_ref=o_ref):
                o_ref[...] = jnp.zeros_like(o_ref)

            o_ref[...] += grads[nr + idx]

    out_specs = [_row_spec(tm, w, 0) for w in widths] + [_full_spec(params[idx].shape) for idx in param_grads]
    out_shape = ([jax.ShapeDtypeStruct((s, w), dt) for w, (_, dt) in zip(widths, row_grads)]
                 + [jax.ShapeDtypeStruct(params[idx].shape, F32) for idx in param_grads])
    extra, aliases = [], {}
    if place is not None:
        out_specs[0] = _row_spec(tm, widths[0], place[1])
        out_shape[0] = jax.ShapeDtypeStruct((s, place[0]), row_grads[0][1])
    if into is not None:
        extra, aliases = [into], {nr + npar + nc: 0}
    return pl.pallas_call(
        body, name=name, grid=(s // tm,),
        in_specs=([_row_spec(tm, w, i) for _, w, i in rows] + [_full_spec(p.shape) for p in params]
                  + [_row_spec(tm, w, i) for _, w, i in cots] + [_ANY] * len(extra)),
        out_specs=out_specs, out_shape=out_shape, input_output_aliases=aliases,
        compiler_params=_cparams(("arbitrary",)),
    )(*[r[0] for r in rows], *params, *[c[0] for c in cots], *extra)


def _rms(x, g):
    return x * lax.rsqrt(jnp.mean(x * x, axis=-1, keepdims=True) + EPS) * g


def _ln(x, g, b):
    mu = jnp.mean(x, axis=-1, keepdims=True)
    xc = x - mu
    var = jnp.mean(xc * xc, axis=-1, keepdims=True)
    return xc * lax.rsqrt(var + EPS) * g + b


def _sigmoid(x):
    return 1.0 / (1.0 + jnp.exp(-x))


def _gelu(x):
    return x * (0.5 * (1.0 + jnp.tanh(math.sqrt(2.0 / math.pi) * (x + 0.044715 * (x * x * x)))))


def _f_rms(x, g):
    return (_rms(x, g),)


def _f_x_rms(x, g):
    return (x, _rms(x, g))


def _f_ln_silu(z, g, b):
    y = _ln(z, g, b)
    return (y * _sigmoid(y),)


def _f_sg_pre(bu, bv, g, b):
    return (_gelu(bu), _ln(_gelu(bv), g, b))


def _f_merge(g0, g1, g2, ya, yb, yc):
    return (_sigmoid(g0) * ya + _sigmoid(g1) * yb + _sigmoid(g2) * yc,)


def _f_resid_rms(x, t, g_post):
    return (x + _rms(t, g_post),)


def _f_resid_rms_rms(x, t, g_post, g_next):
    x1 = x + _rms(t, g_post)
    return (x1, _rms(x1, g_next))


def _f_geglu(zg, zv):
    return _gelu(zg) * zv


_CONV_TILE_ELEMS = 16 * 1024


def _conv_tr(c):
    return _CONV_TILE_ELEMS // c


def _conv_tile(zp_ref, w_ref, bias, k_taps, off, r0):
    c = zp_ref.shape[1]
    tr = _conv_tr(c)
    acc = jnp.broadcast_to(bias, (tr, c))
    for k in range(k_taps):
        acc = acc + w_ref[k:k + 1, :] * zp_ref[r0 + off + k:r0 + off + k + tr, :]
    return acc


def _conv_bwd_input_tile(dzp_ref, w_ref, k_taps, r0):
    c = dzp_ref.shape[1]
    tr = _conv_tr(c)
    acc = jnp.zeros((tr, c), F32)
    for k in range(k_taps):
        s0 = r0 + (k_taps - 1) - k
        acc = acc + w_ref[k:k + 1, :] * dzp_ref[s0:s0 + tr, :]
    return acc


def _conv_bwd_weight(dzp_ref, zp_ref, dw_ref, db_ref, k_taps, off, s):
    c = zp_ref.shape[1]
    tr = _conv_tr(c)
    fold = lambda v: jnp.sum(v.reshape(tr // 8, 8, c), axis=0)
    for k in range(k_taps):
        acc = jnp.zeros((8, c), F32)
        for r in range(s // tr):
            r0 = r * tr
            acc = acc + fold(dzp_ref[r0:r0 + tr, :] * zp_ref[r0 + off + k:r0 + off + k + tr, :])
        dw_ref[k:k + 1, :] = jnp.sum(acc, axis=0, keepdims=True)
    acc = jnp.zeros((8, c), F32)
    for r in range(s // tr):
        acc = acc + fold(dzp_ref[r * tr:(r + 1) * tr, :])
    db_ref[...] = jnp.sum(acc, axis=0, keepdims=True)


def _glu_conv_fwd(p, blk0, w, b):
    s = p.shape[0]
    k_taps, c = w.shape
    cb, pad = _LANES, 32
    off = pad - (k_taps - 1)

    def body(a_ref, w_ref, b_ref, o_ref, zp_ref):
        zp_ref[0:pad, :] = jnp.zeros((pad, cb), F32)
        zp_ref[pad:pad + s, :] = a_ref[:, 0:cb] * _sigmoid(a_ref[:, cb:2 * cb])
        tr = _conv_tr(cb)
        for r in range(s // tr):
            o_ref[r * tr:(r + 1) * tr, :] = _conv_tile(zp_ref, w_ref, b_ref[...], k_taps, off, r * tr)

    return pl.pallas_call(
        body, name="glu_conv_fwd", grid=(c // cb,),
        in_specs=[pl.BlockSpec((s, 2 * cb), lambda j: (0, blk0 + j)),
                  pl.BlockSpec((k_taps, cb), lambda j: (0, j)), pl.BlockSpec((1, cb), lambda j: (0, j))],
        out_specs=pl.BlockSpec((s, cb), lambda j: (0, j)),
        out_shape=jax.ShapeDtypeStruct((s, c), F32),
        scratch_shapes=[pltpu.VMEM((s + pad, cb), F32)],
        compiler_params=_cparams(("parallel",)),
    )(p, w, b)


def _glu_conv_bwd(p, blk0, w, dz, dp):
    s = p.shape[0]
    k_taps, c = w.shape
    cb, pad = _LANES, 32
    off = pad - (k_taps - 1)

    def body(a_ref, w_ref, dz_ref, dp_in, da_ref, dw_ref, db_ref, zp_ref, dzp_ref):
        zp_ref[0:pad, :] = jnp.zeros((pad, cb), F32)
        zp_ref[pad:pad + s, :] = a_ref[:, 0:cb] * _sigmoid(a_ref[:, cb:2 * cb])
        dzp_ref[0:s, :] = dz_ref[...]
        dzp_ref[s:s + pad, :] = jnp.zeros((pad, cb), F32)
        tr = _conv_tr(cb)
        for r in range(s // tr):
            rows = slice(r * tr, (r + 1) * tr)
            dz0 = _conv_bwd_input_tile(dzp_ref, w_ref, k_taps, r * tr)
            sg = _sigmoid(a_ref[rows, cb:2 * cb])
            da_ref[rows, 0:cb] = (dz0 * sg).astype(da_ref.dtype)
            da_ref[rows, cb:2 * cb] = (dz0 * a_ref[rows, 0:cb] * sg * (1.0 - sg)).astype(da_ref.dtype)
        _conv_bwd_weight(dzp_ref, zp_ref, dw_ref, db_ref, k_taps, off, s)

    return pl.pallas_call(
        body, name="glu_conv_bwd", grid=(c // cb,),
        in_specs=[pl.BlockSpec((s, 2 * cb), lambda j: (0, blk0 + j)),
                  pl.BlockSpec((k_taps, cb), lambda j: (0, j)), pl.BlockSpec((s, cb), lambda j: (0, j)), _ANY],
        out_specs=[pl.BlockSpec((s, 2 * cb), lambda j: (0, blk0 + j)),
                   pl.BlockSpec((k_taps, cb), lambda j: (0, j)), pl.BlockSpec((1, cb), lambda j: (0, j))],
        out_shape=[jax.ShapeDtypeStruct(dp.shape, dp.dtype),
                   jax.ShapeDtypeStruct((k_taps, c), F32), jax.ShapeDtypeStruct((1, c), F32)],
        scratch_shapes=[pltpu.VMEM((s + pad, cb), F32), pltpu.VMEM((s + pad, cb), F32)],
        input_output_aliases={3: 0},
        compiler_params=_cparams(("parallel",)),
    )(p, w, dz, dp)


def _conv_geglu_fwd(up, w, b):
    s, f2 = up.shape
    f = f2 // 2
    k_taps = w.shape[0]
    cb, pad = _FFN_CB, 8
    off = pad - (k_taps - 1)
    nb = f // cb

    def body(u_ref, w_ref, b_ref, o_ref, z_ref):
        z_ref[0:pad, :] = jnp.zeros((pad, 2 * cb), F32)
        z_ref[pad:pad + s, :] = u_ref[...]
        tr = _conv_tr(2 * cb)
        for r in range(s // tr):
            z = _conv_tile(z_ref, w_ref, b_ref[...], k_taps, off, r * tr)
            o_ref[r * tr:(r + 1) * tr, :] = _f_geglu(z[:, 0:cb], z[:, cb:2 * cb]).astype(o_ref.dtype)

    return pl.pallas_call(
        body, name="conv_geglu_fwd", grid=(nb,),
        in_specs=[pl.BlockSpec((s, 2 * cb), lambda j: (0, j)), pl.BlockSpec((k_taps, 2 * cb), lambda j: (0, j)),
                  pl.BlockSpec((1, 2 * cb), lambda j: (0, j))],
        out_specs=pl.BlockSpec((s, cb), lambda j: (0, j)),
        out_shape=jax.ShapeDtypeStruct((s, f), _MXU_DT),
        scratch_shapes=[pltpu.VMEM((s + pad, 2 * cb), F32)],
        compiler_params=_cparams(("parallel",)),
    )(up, w, b)


def _conv_geglu_bwd(up, w, b, dact):
    s, f2 = up.shape
    f = f2 // 2
    k_taps = w.shape[0]
    cb, pad = _FFN_CB, 8
    off = pad - (k_taps - 1)
    nb = f // cb

    def body(u_ref, w_ref, b_ref, da_ref, du_ref, dw_ref, db_ref, z_ref, dz_ref):
        z_ref[0:pad, :] = jnp.zeros((pad, 2 * cb), F32)
        z_ref[pad:pad + s, :] = u_ref[...]
        dz_ref[s:s + pad, :] = jnp.zeros((pad, 2 * cb), F32)
        tr = _conv_tr(2 * cb)
        for r in range(s // tr):
            rows = slice(r * tr, (r + 1) * tr)
            z = _conv_tile(z_ref, w_ref, b_ref[...], k_taps, off, r * tr)
            _, vjp = jax.vjp(_f_geglu, z[:, 0:cb], z[:, cb:2 * cb])
            dzg, dzv = vjp(da_ref[rows, :].astype(F32))
            dz_ref[rows, 0:cb] = dzg
            dz_ref[rows, cb:2 * cb] = dzv
        for r in range(s // tr):
            rows = slice(r * tr, (r + 1) * tr)
            du_ref[rows, :] = _conv_bwd_input_tile(dz_ref, w_ref, k_taps, r * tr).astype(du_ref.dtype)
        _conv_bwd_weight(dz_ref, z_ref, dw_ref, db_ref, k_taps, off, s)

    col = lambda rows_: pl.BlockSpec((rows_, 2 * cb), lambda j: (0, j))
    return pl.pallas_call(
        body, name="conv_geglu_bwd", grid=(nb,),
        in_specs=[col(s), col(k_taps), col(1), pl.BlockSpec((s, cb), lambda j: (0, j))],
        out_specs=[col(s), col(k_taps), col(1)],
        out_shape=[jax.ShapeDtypeStruct((s, f2), _MXU_DT), jax.ShapeDtypeStruct((k_taps, f2), F32),
                   jax.ShapeDtypeStruct((1, f2), F32)],
        scratch_shapes=[pltpu.VMEM((s + pad, 2 * cb), F32), pltpu.VMEM((s + pad, 2 * cb), F32)],
        compiler_params=_cparams(("parallel",)),
    )(up, w, b, dact)


def _tril_mask():
    t = lax.broadcasted_iota(jnp.int32, (SG_CHUNK, SG_CHUNK), 0)
    s = lax.broadcasted_iota(jnp.int32, (SG_CHUNK, SG_CHUNK), 1)
    return t >= s


def _sg_mix_fwd(u, vn, w, bcol):
    s, c = u.shape
    gw = c // SG_GROUPS

    def body(u_ref, v_ref, w_ref, b_ref, o_ref):
        wm = jnp.where(_tril_mask(), w_ref[0], 0.0).astype(_MXU_DT)
        for n in range(s // SG_CHUNK):
            rows = slice(n * SG_CHUNK, (n + 1) * SG_CHUNK)
            mixed = jnp.dot(wm, v_ref[rows, :], preferred_element_type=F32) + b_ref[0]
            o_ref[rows, :] = (u_ref[rows, :] * mixed).astype(o_ref.dtype)

    return pl.pallas_call(
        body, name="sg_mix_fwd", grid=(SG_GROUPS,),
        in_specs=[pl.BlockSpec((s, gw), lambda g: (0, g)), pl.BlockSpec((s, gw), lambda g: (0, g)),
                  pl.BlockSpec((1, SG_CHUNK, SG_CHUNK), lambda g: (g, 0, 0)),
                  pl.BlockSpec((1, SG_CHUNK, 1), lambda g: (g, 0, 0))],
        out_specs=pl.BlockSpec((s, gw), lambda g: (0, g)),
        out_shape=jax.ShapeDtypeStruct((s, c), _MXU_DT),
        compiler_params=_cparams(("parallel",)),
    )(u, vn, w, bcol)


def _sg_mix_bwd(u, vn, w, bcol, dub):
    s, c = u.shape
    gw = c // SG_GROUPS

    def body(u_ref, v_ref, w_ref, b_ref, d_ref, du_ref, dv_ref, dw_ref, db_ref):
        mask = _tril_mask()
        wm = jnp.where(mask, w_ref[0], 0.0).astype(_MXU_DT)
        dw = jnp.zeros((SG_CHUNK, SG_CHUNK), F32)
        db = jnp.zeros((SG_CHUNK, 1), F32)
        for n in range(s // SG_CHUNK):
            rows = slice(n * SG_CHUNK, (n + 1) * SG_CHUNK)
            v = v_ref[rows, :]
            d = d_ref[rows, :].astype(F32)
            mixed = jnp.dot(wm, v, preferred_element_type=F32) + b_ref[0]
            du_ref[rows, :] = d * mixed
            dmix = d * u_ref[rows, :]
            dmix_lo = dmix.astype(_MXU_DT)
            dv_ref[rows, :] = lax.dot_general(wm, dmix_lo, (((0,), (0,)), ((), ())), preferred_element_type=F32)
            dw = dw + lax.dot_general(dmix_lo, v, (((1,), (1,)), ((), ())), preferred_element_type=F32)
            db = db + jnp.sum(dmix, axis=1, keepdims=True)
        dw_ref[0] = jnp.where(mask, dw, 0.0)
        db_ref[0] = db

    return pl.pallas_call(
        body, name="sg_mix_bwd", grid=(SG_GROUPS,),
        in_specs=[pl.BlockSpec((s, gw), lambda g: (0, g)), pl.BlockSpec((s, gw), lambda g: (0, g)),
                  pl.BlockSpec((1, SG_CHUNK, SG_CHUNK), lambda g: (g, 0, 0)),
                  pl.BlockSpec((1, SG_CHUNK, 1), lambda g: (g, 0, 0)), pl.BlockSpec((s, gw), lambda g: (0, g))],
        out_specs=[pl.BlockSpec((s, gw), lambda g: (0, g)), pl.BlockSpec((s, gw), lambda g: (0, g)),
                   pl.BlockSpec((1, SG_CHUNK, SG_CHUNK), lambda g: (g, 0, 0)),
                   pl.BlockSpec((1, SG_CHUNK, 1), lambda g: (g, 0, 0))],
        out_shape=[jax.ShapeDtypeStruct((s, c), F32), jax.ShapeDtypeStruct((s, c), F32),
                   jax.ShapeDtypeStruct((SG_GROUPS, SG_CHUNK, SG_CHUNK), F32),
                   jax.ShapeDtypeStruct((SG_GROUPS, SG_CHUNK, 1), F32)],
        compiler_params=_cparams(("parallel",)),
    )(u, vn, w, bcol, dub)


def _rope_fwd(q2, kv2, p, krm_idx, krs_idx, tc, ts):
    s = q2.shape[0]
    hw = N_HEADS * HEAD_PAD
    tm = 256

    def body(qm_ref, qs_ref, kn_ref, v_ref, krm_ref, krs_ref, tc_ref, ts_ref, q_ref, k_ref, vo_ref):
        tcv, tsv = tc_ref[...], ts_ref[...]
        kpe = krm_ref[...] * tcv + krs_ref[...] * tsv
        for h in range(N_HEADS):
            cols = slice(h * HEAD_PAD, (h + 1) * HEAD_PAD)
            q_ref[:, cols] = (qm_ref[:, cols] * tcv + qs_ref[:, cols] * tsv).astype(q_ref.dtype)
            k_ref[:, cols] = (kn_ref[:, cols] + kpe).astype(k_ref.dtype)
        vo_ref[...] = v_ref[...].astype(vo_ref.dtype)

    return pl.pallas_call(
        body, name="rope_fwd", grid=(s // tm,),
        in_specs=[_row_spec(tm, hw, 0), _row_spec(tm, hw, 1), _row_spec(tm, hw, 0), _row_spec(tm, hw, 1),
                  _row_spec(tm, HEAD_PAD, krm_idx), _row_spec(tm, HEAD_PAD, krs_idx),
                  _row_spec(tm, HEAD_PAD, 0), _row_spec(tm, HEAD_PAD, 0)],
        out_specs=[_row_spec(tm, hw, 0)] * 3,
        out_shape=[jax.ShapeDtypeStruct((s, hw), _MXU_DT)] * 3,
        compiler_params=_cparams(("parallel",)),
    )(q2, q2, kv2, kv2, p, p, tc, ts)


def _rope_bwd(dq, dk, dv, tc, ts, dp, kr_blk):
    s = dq.shape[0]
    hw = N_HEADS * HEAD_PAD
    tm = 256

    def body(dq_ref, dk_ref, dv_ref, tc_ref, ts_ref, dp_in, dq2_ref, dkv2_ref, dkr_ref):
        tcv, tsv = tc_ref[...], ts_ref[...]
        dkpe = jnp.zeros((tm, HEAD_PAD), F32)
        for h in range(N_HEADS):
            cols = slice(h * HEAD_PAD, (h + 1) * HEAD_PAD)
            dqh = dq_ref[:, cols]
            dq2_ref[:, cols] = (dqh * tcv).astype(dq2_ref.dtype)
            dq2_ref[:, hw + h * HEAD_PAD:hw + (h + 1) * HEAD_PAD] = (dqh * tsv).astype(dq2_ref.dtype)
            dkpe = dkpe + dk_ref[:, cols]
        dkv2_ref[:, 0:hw] = dk_ref[...].astype(dkv2_ref.dtype)
        dkv2_ref[:, hw:2 * hw] = dv_ref[...].astype(dkv2_ref.dtype)
        dkr_ref[:, 0:HEAD_PAD] = (dkpe * tcv).astype(dkr_ref.dtype)
        dkr_ref[:, HEAD_PAD:2 * HEAD_PAD] = (dkpe * tsv).astype(dkr_ref.dtype)
        dkr_ref[:, 2 * HEAD_PAD:3 * HEAD_PAD] = jnp.zeros((tm, HEAD_PAD), dkr_ref.dtype)

    return pl.pallas_call(
        body, name="rope_bwd", grid=(s // tm,),
        in_specs=[_row_spec(tm, hw, 0)] * 3 + [_row_spec(tm, HEAD_PAD, 0)] * 2 + [_ANY],
        out_specs=[_row_spec(tm, 2 * hw, 0), _row_spec(tm, 2 * hw, 0), _row_spec(tm, 3 * HEAD_PAD, kr_blk)],
        out_shape=[jax.ShapeDtypeStruct((s, 2 * hw), _MXU_DT), jax.ShapeDtypeStruct((s, 2 * hw), _MXU_DT),
                   jax.ShapeDtypeStruct(dp.shape, dp.dtype)],
        input_output_aliases={5: 2},
        compiler_params=_cparams(("parallel",)),
    )(dq, dk, dv, tc, ts, dp)


_ATTN_TQ = 256
_ATTN_SCALE = (QK_NOPE + QK_ROPE) ** -0.5


def _attn_probs(q, k, i, s):
    sc = lax.dot_general(q, k, (((1,), (1,)), ((), ())), preferred_element_type=F32) * _ATTN_SCALE
    row = i * _ATTN_TQ + lax.broadcasted_iota(jnp.int32, (_ATTN_TQ, s), 0)
    col = lax.broadcasted_iota(jnp.int32, (_ATTN_TQ, s), 1)
    sc = jnp.where(row >= col, sc, jnp.finfo(F32).min)
    e = jnp.exp(sc - jnp.max(sc, axis=1, keepdims=True))
    return e / jnp.sum(e, axis=1, keepdims=True)


def _attn_fwd(q, k, v):
    s = q.shape[0]

    def body(q_ref, k_ref, v_ref, o_ref):
        p = _attn_probs(q_ref[...], k_ref[...], pl.program_id(1), s)
        o_ref[...] = jnp.dot(p.astype(_MXU_DT), v_ref[...], preferred_element_type=F32).astype(o_ref.dtype)

    qspec = pl.BlockSpec((_ATTN_TQ, HEAD_PAD), lambda h, i: (i, h))
    kspec = pl.BlockSpec((s, HEAD_PAD), lambda h, i: (0, h))
    return pl.pallas_call(
        body, name="attn_fwd", grid=(N_HEADS, s // _ATTN_TQ),
        in_specs=[qspec, kspec, kspec], out_specs=qspec,
        out_shape=jax.ShapeDtypeStruct(q.shape, _MXU_DT),
        compiler_params=_cparams(("parallel", "parallel")),
    )(q, k, v)


def _attn_bwd(q, k, v, do):
    s = q.shape[0]

    def body(q_ref, k_ref, v_ref, do_ref, dq_ref, dk_ref, dv_ref):
        i = pl.program_id(1)

        @pl.when(i == 0)
        def _():
            dk_ref[...] = jnp.zeros_like(dk_ref)
            dv_ref[...] = jnp.zeros_like(dv_ref)

        qv, kv, dov = q_ref[...], k_ref[...], do_ref[...]
        p = _attn_probs(qv, kv, i, s)
        dp = lax.dot_general(dov, v_ref[...], (((1,), (1,)), ((), ())), preferred_element_type=F32)
        delta = jnp.sum(p * dp, axis=1, keepdims=True)
        ds = (p * (dp - delta) * _ATTN_SCALE).astype(_MXU_DT)
        dq_ref[...] = jnp.dot(ds, kv, preferred_element_type=F32)
        dk_ref[...] += lax.dot_general(ds, qv, (((0,), (0,)), ((), ())), preferred_element_type=F32)
        dv_ref[...] += lax.dot_general(p.astype(_MXU_DT), dov, (((0,), (0,)), ((), ())), preferred_element_type=F32)

    qspec = pl.BlockSpec((_ATTN_TQ, HEAD_PAD), lambda h, i: (i, h))
    kspec = pl.BlockSpec((s, HEAD_PAD), lambda h, i: (0, h))
    return pl.pallas_call(
        body, name="attn_bwd", grid=(N_HEADS, s // _ATTN_TQ),
        in_specs=[qspec, kspec, kspec, qspec], out_specs=[qspec, kspec, kspec],
        out_shape=[jax.ShapeDtypeStruct(q.shape, F32)] * 3,
        compiler_params=_cparams(("parallel", "arbitrary")),
    )(q, k, v, do)


def _loss_head(y, target):
    s, d = y.shape
    tm = 256

    def body(y_ref, t_ref, loss_ref, dy_ref):
        @pl.when(pl.program_id(0) == 0)
        def _():
            loss_ref[...] = jnp.zeros_like(loss_ref)

        err = y_ref[...] - t_ref[...]
        loss_ref[...] += 0.5 * jnp.sum(jnp.mean(err * err, axis=-1, keepdims=True), axis=0, keepdims=True)
        dy_ref[...] = err * (1.0 / d)

    return pl.pallas_call(
        body, name="loss_head", grid=(s // tm,),
        in_specs=[_row_spec(tm, d, 0), _row_spec(tm, d, 0)],
        out_specs=[_full_spec((1, 1)), _row_spec(tm, d, 0)],
        out_shape=[jax.ShapeDtypeStruct((1, 1), F32), jax.ShapeDtypeStruct((s, d), F32)],
        compiler_params=_cparams(("arbitrary",)),
    )(y, target)


def _adamw_math(w, g, m, v):
    mn = ADAM_B1 * m + (1.0 - ADAM_B1) * g
    vn = ADAM_B2 * v + (1.0 - ADAM_B2) * (g * g)
    m_hat = mn / (1.0 - ADAM_B1 ** ADAM_STEP)
    v_hat = vn / (1.0 - ADAM_B2 ** ADAM_STEP)
    return -ADAM_LR * (m_hat / (jnp.sqrt(v_hat) + ADAM_EPS) + ADAM_WD * w), mn, vn


def _adamw(name, w, g, m, v):
    l, k, n = w.shape
    tk = _slab_rows(k, n)

    def body(w_ref, g_ref, m_ref, v_ref, d_ref, mo_ref, vo_ref):
        d_ref[...], mo_ref[...], vo_ref[...] = _adamw_math(w_ref[...], g_ref[...], m_ref[...], v_ref[...])

    spec = pl.BlockSpec((1, tk, n), lambda i, j: (i, j, 0))
    return pl.pallas_call(
        body, name=name, grid=(l, k // tk), in_specs=[spec] * 4, out_specs=[spec] * 3,
        out_shape=[jax.ShapeDtypeStruct(w.shape, F32)] * 3,
        compiler_params=_cparams(("parallel", "parallel")),
    )(w, g, m, v)


_ANY = pl.BlockSpec(memory_space=pl.ANY)


def _mesh_pos():
    return lax.axis_index("x"), lax.axis_index("y"), lax.axis_index("c")


def _other_chips(x, y):
    return [(1 - x, y), (x, 1 - y), (1 - x, 1 - y)]


def _remote(src, dst, send_sem, recv_sem, to):
    return pltpu.make_async_remote_copy(src_ref=src, dst_ref=dst, send_sem=send_sem, recv_sem=recv_sem,
                                        device_id=to, device_id_type=_MESH)


def _all_gather_xy(shards):
    n = len(shards)

    def body(*refs):
        ins, outs, (send_sems, recv_sems) = refs[:n], refs[n:2 * n], refs[2 * n:]
        x, y, c = _mesh_pos()
        sibling = (x, y, 1 - c)
        chips = _other_chips(x, y)
        idx = lambda chip: 2 * chip[0] + chip[1]

        def copy(a, k, src, dst, to):
            return _remote(src, dst, send_sems.at[a, k], recv_sems.at[a, k], to)

        started = []
        for k, chip in enumerate(chips):
            for a in range(n):
                cp = copy(a, k, ins[a].at[c], outs[a].at[2 * x + y, c], (*chip, c))
                cp.start()
                started.append(cp)
        for k, chip in enumerate(chips):
            for a in range(n):
                landed = outs[a].at[idx(chip), c]
                copy(a, k, ins[a].at[c], landed, (*chip, c)).wait_recv()
                cp = copy(a, 3 + k, landed, landed, sibling)
                cp.start()
                started.append(cp)
        for k, chip in enumerate(chips):
            for a in range(n):
                copy(a, 3 + k, ins[a].at[c], outs[a].at[idx(chip), 1 - c], sibling).wait_recv()
        for cp in started:
            cp.wait_send()

    return pl.pallas_call(
        body, name="all_gather_xy", in_specs=[_ANY] * n, out_specs=[_ANY] * n,
        out_shape=[jax.ShapeDtypeStruct((4,) + a.shape, a.dtype) for a in shards],
        scratch_shapes=[pltpu.SemaphoreType.DMA((n, 6)), pltpu.SemaphoreType.DMA((n, 6))],
    )(*shards)


def _swap_layers(gs):
    n = len(gs)

    def body(*refs):
        ins, outs, (send_sems, recv_sems) = refs[:n], refs[n:2 * n], refs[2 * n:]
        x, y, c = _mesh_pos()
        copies = [_remote(ins[a].at[:, 1 - c], outs[a], send_sems.at[a], recv_sems.at[a], (x, y, 1 - c))
                  for a in range(n)]
        for cp in copies:
            cp.start()
        for cp in copies:
            cp.wait()

    return pl.pallas_call(
        body, name="rs_swap_layers", in_specs=[_ANY] * n, out_specs=[_ANY] * n,
        out_shape=[jax.ShapeDtypeStruct((4,) + g.shape[2:], g.dtype) for g in gs],
        scratch_shapes=[pltpu.SemaphoreType.DMA((n,)), pltpu.SemaphoreType.DMA((n,))],
    )(*gs)


def _scatter_chips(ts):
    n = len(ts)

    def body(*refs):
        ins, outs, (send_sems, recv_sems) = refs[:n], refs[n:2 * n], refs[2 * n:]
        x, y, c = _mesh_pos()
        copies = []
        for k, chip in enumerate(_other_chips(x, y)):
            for a in range(n):
                cp = _remote(ins[a].at[2 * chip[0] + chip[1]], outs[a].at[k], send_sems.at[a, k], recv_sems.at[a, k],
                             (*chip, c))
                cp.start()
                copies.append(cp)
        for cp in copies:
            cp.wait()

    return pl.pallas_call(
        body, name="rs_scatter_chips", in_specs=[_ANY] * n, out_specs=[_ANY] * n,
        out_shape=[jax.ShapeDtypeStruct((3,) + t.shape[1:], t.dtype) for t in ts],
        scratch_shapes=[pltpu.SemaphoreType.DMA((n, 3)), pltpu.SemaphoreType.DMA((n, 3))],
    )(*ts)


def _join_layers(us):
    n = len(us)

    def body(*refs):
        ins, outs, (send_sems, recv_sems) = refs[:n], refs[n:2 * n], refs[2 * n:]
        x, y, c = _mesh_pos()
        copies = [_remote(ins[a].at[c], outs[a].at[c], send_sems.at[a], recv_sems.at[a], (x, y, 1 - c))
                  for a in range(n)]
        for cp in copies:
            cp.start()
        for a, cp in enumerate(copies):
            cp.wait_send()
            _remote(ins[a].at[c], outs[a].at[1 - c], send_sems.at[a], recv_sems.at[a], (x, y, 1 - c)).wait_recv()

    return pl.pallas_call(
        body, name="rs_join_layers", in_specs=[_ANY] * n, out_specs=[_ANY] * n,
        out_shape=[jax.ShapeDtypeStruct(u.shape, u.dtype) for u in us],
        input_output_aliases={a: a for a in range(n)},
        scratch_shapes=[pltpu.SemaphoreType.DMA((n,)), pltpu.SemaphoreType.DMA((n,))],
    )(*us)


def _slab_rows(k, n, itemsize=4):
    tk = (1 << 20) // (n * itemsize) // 16 * 16
    while 0 < tk < k and k % tk:
        tk -= 16
    return tk if 0 < tk < k else k


def _add_layer(name, g, a, c_arr):
    _, _, k, n = g.shape
    tk = _slab_rows(k, n)

    def body(c_ref, g_ref, a_ref, o_ref):
        o_ref[...] = (g_ref[0] + a_ref[...]).astype(o_ref.dtype)

    return pl.pallas_call(
        body, name=name,
        grid_spec=pltpu.PrefetchScalarGridSpec(
            num_scalar_prefetch=1, grid=(4, k // tk),
            in_specs=[pl.BlockSpec((1, 1, tk, n), lambda j, i, c_ref: (j, c_ref[0], i, 0)),
                      pl.BlockSpec((1, tk, n), lambda j, i, c_ref: (j, i, 0))],
            out_specs=pl.BlockSpec((1, tk, n), lambda j, i, c_ref: (j, i, 0))),
        out_shape=jax.ShapeDtypeStruct((4, k, n), _RS_DT),
        compiler_params=_cparams(("parallel", "parallel")),
    )(c_arr, g, a)


def _add_chips(name, t, b, pos_arr):
    _, k, n = t.shape
    tk = _slab_rows(k, n)

    def body(pos_ref, t_ref, b_ref, o_ref):
        f = lambda v: v.astype(F32)
        o_ref[0] = ((f(t_ref[0]) + f(b_ref[0])) + f(b_ref[1])) + f(b_ref[2])

    return pl.pallas_call(
        body, name=name,
        grid_spec=pltpu.PrefetchScalarGridSpec(
            num_scalar_prefetch=1, grid=(k // tk,),
            in_specs=[pl.BlockSpec((1, tk, n), lambda i, pos_ref: (pos_ref[0], i, 0)),
                      pl.BlockSpec((3, tk, n), lambda i, pos_ref: (0, i, 0))],
            out_specs=pl.BlockSpec((1, tk, n), lambda i, pos_ref: (pos_ref[1], i, 0))),
        out_shape=jax.ShapeDtypeStruct((2, k, n), F32),
        compiler_params=_cparams(("parallel",)),
    )(pos_arr, t, b)


def _reduce_scatter(names, gs):
    x, y, c = _mesh_pos()
    c_arr = jnp.reshape(c, (1,)).astype(jnp.int32)
    pos_arr = jnp.stack([2 * x + y, c]).astype(jnp.int32)
    sib = _swap_layers(gs)
    ts = [_add_layer("rs_add_layer_" + n, g, a, c_arr) for n, g, a in zip(names, gs, sib)]
    bs = _scatter_chips(ts)
    us = [_add_chips("rs_add_chips_" + n, t, b, pos_arr) for n, t, b in zip(names, ts, bs)]
    return _join_layers(us)


def _all_reduce_adamw(gs, ws, ms, vs):
    n = len(gs)

    def body(*refs):
        g_refs, w_refs, m_refs, v_refs = (refs[i * n:(i + 1) * n] for i in range(4))
        gsum, delta, m_out, v_out = (refs[(4 + i) * n:(5 + i) * n] for i in range(4))
        slots = refs[8 * n:9 * n]
        send_sems, recv_sems = refs[9 * n:]
        x, y, c = _mesh_pos()
        me = 4 * x + 2 * y + c
        copies = []
        for rel in range(1, 8):
            bx, by, bc = (rel >> 2) & 1, (rel >> 1) & 1, rel & 1
            peer = (1 - x if bx else x, 1 - y if by else y, 1 - c if bc else c)
            for a in range(n):
                cp = _remote(g_refs[a], slots[a].at[me], send_sems.at[a, rel - 1], recv_sems.at[a, rel - 1], peer)
                cp.start()
                copies.append(cp)
        for a in range(n):
            slots[a][me] = g_refs[a][...]
        for cp in copies:
            cp.wait()
        for a in range(n):
            acc = slots[a][0]
            for d in range(1, 8):
                acc = acc + slots[a][d]
            gsum[a][...] = acc
            delta[a][...], m_out[a][...], v_out[a][...] = _adamw_math(w_refs[a][...], acc, m_refs[a][...],
                                                                      v_refs[a][...])

    vmem = pl.BlockSpec(memory_space=pltpu.VMEM)
    outs = pl.pallas_call(
        body, name="all_reduce_adamw", in_specs=[vmem] * (4 * n), out_specs=[vmem] * (4 * n),
        out_shape=[jax.ShapeDtypeStruct(g.shape, F32) for g in gs] * 4,
        scratch_shapes=([pltpu.VMEM((8,) + g.shape, F32) for g in gs]
                        + [pltpu.SemaphoreType.DMA((n, 7)), pltpu.SemaphoreType.DMA((n, 7))]),
        compiler_params=pltpu.CompilerParams(vmem_limit_bytes=_VMEM_LIMIT),
    )(*gs, *ws, *ms, *vs)
    return outs[:n], outs[n:2 * n], outs[2 * n:3 * n], outs[3 * n:]


def _swap_rope(a):
    h = QK_ROPE // 2
    return jnp.concatenate([a[..., h:], a[..., :h]], axis=-1)


_FFN_CB = 256


def _interleave(a, cb):
    r, c = a.shape
    return a.reshape(r, 2, c // (2 * cb), cb).transpose(0, 2, 1, 3).reshape(r, c)


def _deinterleave(a, cb):
    r, c = a.shape
    return a.reshape(r, c // (2 * cb), 2, cb).transpose(0, 2, 1, 3).reshape(r, c)


class _InLayout:
    def __init__(self, d):
        self.d = d
        self.gates = 0
        self.a = 3 * d
        self.b = 4 * d
        self.kv = 5 * d
        self.q = self.kv + 256
        self.krm = self.q + 384
        self.krs = self.krm + HEAD_PAD
        self.width = self.krs + 2 * HEAD_PAD


def _prep_layer(wl, d):
    lay = _InLayout(d)
    w_in = wl["w_in"]
    dt = w_in.dtype
    a, b = w_in[:, 0:d], w_in[:, d:2 * d]
    q, kv = w_in[:, 2 * d:2 * d + 384], w_in[:, 2 * d + 384:2 * d + 640]
    kr = w_in[:, 2 * d + 640:2 * d + 640 + QK_ROPE]
    gates = w_in[:, 2 * d + 640 + QK_ROPE:]
    z = lambda n: jnp.zeros((d, n), dt)
    krm = jnp.concatenate([z(QK_NOPE), kr, z(HEAD_PAD - QK_NOPE - QK_ROPE)], axis=1)
    krs = jnp.concatenate([z(QK_NOPE), _swap_rope(kr), z(HEAD_PAD - QK_NOPE - QK_ROPE)], axis=1)
    out = dict(wl)
    out["w_in"] = jnp.concatenate([gates, _interleave(a, _LANES), b, kv, q, krm, krs,
                                   z(lay.width - lay.krs - HEAD_PAD)], axis=1)
    out["ffn_w_up"] = _interleave(wl["ffn_w_up"], _FFN_CB)
    out["ffn_dw_w"] = _interleave(wl["ffn_dw_w"], _FFN_CB)
    out["ffn_dw_b"] = _interleave(wl["ffn_dw_b"].reshape(1, -1), _FFN_CB)
    uq = wl["mla_w_uq"].reshape(-1, N_HEADS, QK_NOPE + QK_ROPE)
    nq = uq.shape[0]
    nope, pe = uq[..., :QK_NOPE], uq[..., QK_NOPE:]
    zq = lambda n: jnp.zeros((nq, N_HEADS, n), dt)
    main = jnp.concatenate([nope, pe, zq(HEAD_PAD - QK_NOPE - QK_ROPE)], axis=-1).reshape(nq, -1)
    swapped = jnp.concatenate([zq(QK_NOPE), _swap_rope(pe), zq(HEAD_PAD - QK_NOPE - QK_ROPE)], axis=-1).reshape(nq, -1)
    out["mla_w_uq"] = jnp.concatenate([main, swapped], axis=1)
    ukv = wl["mla_w_ukv"].reshape(-1, N_HEADS, QK_NOPE + V_HEAD)
    nkv = ukv.shape[0]
    zk = jnp.zeros((nkv, N_HEADS, HEAD_PAD - QK_NOPE), dt)
    zv = jnp.zeros((nkv, N_HEADS, HEAD_PAD - V_HEAD), dt)
    out["mla_w_ukv"] = jnp.concatenate([jnp.concatenate([ukv[..., :QK_NOPE], zk], axis=-1).reshape(nkv, -1),
                                        jnp.concatenate([ukv[..., QK_NOPE:], zv], axis=-1).reshape(nkv, -1)], axis=1)
    wo = wl["mla_w_o"].reshape(N_HEADS, V_HEAD, -1)
    out["mla_w_o"] = jnp.concatenate([wo, jnp.zeros((N_HEADS, HEAD_PAD - V_HEAD, wo.shape[-1]), dt)],
                                     axis=1).reshape(N_HEADS * HEAD_PAD, -1)
    return out


def _unprep_grads(g, d):
    lay = _InLayout(d)
    gi = g["w_in"]
    kr = (gi[:, lay.krm + QK_NOPE:lay.krm + QK_NOPE + QK_ROPE]
          + _swap_rope(gi[:, lay.krs + QK_NOPE:lay.krs + QK_NOPE + QK_ROPE]))
    out = dict(g)
    out["w_in"] = jnp.concatenate([_deinterleave(gi[:, lay.a:lay.a + d], _LANES), gi[:, lay.b:lay.b + d],
                                   gi[:, lay.q:lay.q + 384], gi[:, lay.kv:lay.kv + 256], kr,
                                   gi[:, lay.gates:lay.gates + 3 * d]], axis=1)
    for n in ("ffn_w_up", "ffn_dw_w", "ffn_dw_b"):
        out[n] = _deinterleave(g[n], _FFN_CB)
    hw = N_HEADS * HEAD_PAD
    gq = g["mla_w_uq"]
    nq = gq.shape[0]
    main = gq[:, :hw].reshape(nq, N_HEADS, HEAD_PAD)
    swapped = gq[:, hw:].reshape(nq, N_HEADS, HEAD_PAD)
    pe = main[..., QK_NOPE:QK_NOPE + QK_ROPE] + _swap_rope(swapped[..., QK_NOPE:QK_NOPE + QK_ROPE])
    out["mla_w_uq"] = jnp.concatenate([main[..., :QK_NOPE], pe], axis=-1).reshape(nq, -1)
    gkv = g["mla_w_ukv"]
    nkv = gkv.shape[0]
    out["mla_w_ukv"] = jnp.concatenate([gkv[:, :hw].reshape(nkv, N_HEADS, HEAD_PAD)[..., :QK_NOPE],
                                        gkv[:, hw:].reshape(nkv, N_HEADS, HEAD_PAD)[..., :V_HEAD]],
                                       axis=-1).reshape(nkv, -1)
    go = g["mla_w_o"]
    out["mla_w_o"] = go.reshape(N_HEADS, HEAD_PAD, -1)[:, :V_HEAD].reshape(N_HEADS * V_HEAD, -1)
    return out


def _rope_tables(positions):
    s = positions.shape[0]
    inv = ROPE_THETA ** (-jnp.arange(0, QK_ROPE, 2, dtype=F32) / QK_ROPE)
    ang = positions.astype(F32)[:, None] * inv
    cos, sin = jnp.cos(ang), jnp.sin(ang)
    tail = jnp.zeros((s, HEAD_PAD - QK_NOPE - QK_ROPE), F32)
    tc = jnp.concatenate([jnp.ones((s, QK_NOPE), F32), cos, cos, tail], axis=1)
    ts = jnp.concatenate([jnp.zeros((s, QK_NOPE), F32), -sin, sin, tail], axis=1)
    return tc, ts


def _row(v):
    return v.reshape(1, -1)


def _layer_fwd(x, h, w, g_next, tc, ts):
    d = x.shape[1]
    lay = _InLayout(d)
    cw = d // 2
    blk = lambda off, width: off // width
    p = _mm("mm_in", h, w["w_in"])
    z1 = _glu_conv_fwd(p, blk(lay.a, 2 * _LANES), w["conv_dw_w"], _row(w["conv_dw_b"]))
    ln_a = [_row(w["conv_ln_g"]), _row(w["conv_ln_b"])]
    (z3,) = _row_fwd("ln_silu_fwd", _f_ln_silu, [(z1, cw, 0)], ln_a, [(cw, _MXU_DT)])
    ya = _mm("mm_conv_out", z3, w["conv_out_w"])
    ln_b = [_row(w["sg_ln_g"]), _row(w["sg_ln_b"])]
    u, vn = _row_fwd("sg_pre_fwd", _f_sg_pre, [(p, cw, blk(lay.b, cw)), (p, cw, blk(lay.b + cw, cw))], ln_b,
                     [(cw, F32), (cw, _MXU_DT)])
    bcol = w["sg_b"].reshape(SG_GROUPS, SG_CHUNK, 1)
    ub = _sg_mix_fwd(u, vn, w["sg_w"], bcol)
    yb = _mm("mm_sg_out", ub, w["sg_out_w"])
    (qn,) = _row_fwd("q_norm_fwd", _f_rms, [(p, 384, blk(lay.q, 384))], [_row(w["mla_q_norm_g"])], [(384, _MXU_DT)])
    (kvn,) = _row_fwd("kv_norm_fwd", _f_rms, [(p, 256, blk(lay.kv, 256))], [_row(w["mla_kv_norm_g"])],
                      [(256, _MXU_DT)])
    q2 = _mm("mm_uq", qn, w["mla_w_uq"])
    kv2 = _mm("mm_ukv", kvn, w["mla_w_ukv"])
    qf, kf, vf = _rope_fwd(q2, kv2, p, blk(lay.krm, HEAD_PAD), blk(lay.krs, HEAD_PAD), tc, ts)
    o = _attn_fwd(qf, kf, vf)
    yc = _mm("mm_o", o, w["mla_w_o"])
    gate_rows = [(p, d, 0), (p, d, 1), (p, d, 2)]
    (merged,) = _row_fwd("merge_fwd", _f_merge, gate_rows + [(ya, d, 0), (yb, d, 0), (yc, d, 0)], [], [(d, _MXU_DT)])
    t = _mm("mm_out", merged, w["w_out"])
    x1, h2 = _row_fwd("resid_mix_fwd", _f_resid_rms_rms, [(x, d, 0), (t, d, 0)],
                      [_row(w["mix_post_g"]), _row(w["ffn_pre_g"])], [(d, F32), (d, _MXU_DT)])
    up = _mm("mm_up", h2, w["ffn_w_up"])
    act = _conv_geglu_fwd(up, w["ffn_dw_w"], _row(w["ffn_dw_b"]))
    dn = _mm("mm_down", act, w["ffn_w_down"])
    if g_next is None:
        (x2,) = _row_fwd("resid_ffn_last_fwd", _f_resid_rms, [(x1, d, 0), (dn, d, 0)], [_row(w["ffn_post_g"])],
                         [(d, F32)])
        h_next = None
    else:
        x2, h_next = _row_fwd("resid_ffn_fwd", _f_resid_rms_rms, [(x1, d, 0), (dn, d, 0)],
                              [_row(w["ffn_post_g"]), _row(g_next)], [(d, F32), (d, _MXU_DT)])
    saved = dict(x=x, h=h, p=p, z1=z1, z3=z3, ya=ya, u=u, vn=vn, ub=ub, yb=yb, qn=qn, kvn=kvn, qf=qf, kf=kf, vf=vf, o=o,
                 yc=yc, merged=merged, t=t, x1=x1, h2=h2, up=up, act=act, dn=dn, bcol=bcol)
    return x2, h_next, saved


def _layer_bwd(dx2, dh_next, w, g_next, sv, tc, ts):
    d = dx2.shape[1]
    lay = _InLayout(d)
    cw = d // 2
    blk = lambda off, width: off // width
    lo = _MXU_DT
    g = {}
    x1, dn = sv["x1"], sv["dn"]
    if dh_next is None:
        dx1, ddn, g["ffn_post_g"] = _row_bwd("resid_ffn_last_bwd", _f_resid_rms, [(x1, d, 0), (dn, d, 0)],
                                             [_row(w["ffn_post_g"])], [(dx2, d, 0)], [(0, F32), (1, lo)], [0])
    else:
        dx1, ddn, g["ffn_post_g"], g["next_pre_g"] = _row_bwd(
            "resid_ffn_bwd", _f_resid_rms_rms, [(x1, d, 0), (dn, d, 0)], [_row(w["ffn_post_g"]), _row(g_next)],
            [(dx2, d, 0), (dh_next, d, 0)], [(0, F32), (1, lo)], [0, 1])
    dact = _mm("mm_down_dx", ddn, w["ffn_w_down"], tb=True)
    g["ffn_w_down"] = _mm("mm_down_dw", sv["act"], ddn, ta=True)
    dup, g["ffn_dw_w"], g["ffn_dw_b"] = _conv_geglu_bwd(sv["up"], w["ffn_dw_w"], _row(w["ffn_dw_b"]), dact)
    dh2 = _mm("mm_up_dx", dup, w["ffn_w_up"], tb=True)
    g["ffn_w_up"] = _mm("mm_up_dw", sv["h2"], dup, ta=True)
    dx, dt, g["mix_post_g"], g["ffn_pre_g"] = _row_bwd(
        "resid_mix_bwd", _f_resid_rms_rms, [(sv["x"], d, 0), (sv["t"], d, 0)],
        [_row(w["mix_post_g"]), _row(w["ffn_pre_g"])], [(dx1, d, 0), (dh2, d, 0)], [(0, F32), (1, lo)], [0, 1])
    dmerged = _mm("mm_out_dx", dt, w["w_out"], tb=True)
    g["w_out"] = _mm("mm_out_dw", sv["merged"], dt, ta=True)
    p = sv["p"]
    gate_rows = [(p, d, 0), (p, d, 1), (p, d, 2)]
    dp, dya, dyb, dyc = _row_bwd(
        "merge_bwd", _f_merge, gate_rows + [(sv["ya"], d, 0), (sv["yb"], d, 0), (sv["yc"], d, 0)], [],
        [(dmerged, d, 0)], [((0, 1, 2), lo), ((3,), lo), ((4,), lo), ((5,), lo)], [], place=(lay.width, 0))
    do = _mm("mm_o_dx", dyc, w["mla_w_o"], tb=True, out_dtype=lo)
    g["mla_w_o"] = _mm("mm_o_dw", sv["o"], dyc, ta=True)
    dqf, dkf, dvf = _attn_bwd(sv["qf"], sv["kf"], sv["vf"], do)
    dq2, dkv2, dp = _rope_bwd(dqf, dkf, dvf, tc, ts, dp, blk(lay.krm, 3 * HEAD_PAD))
    dkvn = _mm("mm_ukv_dx", dkv2, w["mla_w_ukv"], tb=True)
    g["mla_w_ukv"] = _mm("mm_ukv_dw", sv["kvn"], dkv2, ta=True)
    dqn = _mm("mm_uq_dx", dq2, w["mla_w_uq"], tb=True)
    g["mla_w_uq"] = _mm("mm_uq_dw", sv["qn"], dq2, ta=True)
    dp, g["mla_q_norm_g"] = _row_bwd("q_norm_bwd", _f_rms, [(p, 384, blk(lay.q, 384))], [_row(w["mla_q_norm_g"])],
                                     [(dqn, 384, 0)], [((0,), lo)], [0], place=(lay.width, blk(lay.q, 384)), into=dp)
    dp, g["mla_kv_norm_g"] = _row_bwd("kv_norm_bwd", _f_rms, [(p, 256, blk(lay.kv, 256))],
                                      [_row(w["mla_kv_norm_g"])], [(dkvn, 256, 0)], [((0,), lo)], [0],
                                      place=(lay.width, blk(lay.kv, 256)), into=dp)
    dub = _mm("mm_sg_out_dx", dyb, w["sg_out_w"], tb=True)
    g["sg_out_w"] = _mm("mm_sg_out_dw", sv["ub"], dyb, ta=True)
    du, dvn, g["sg_w"], dbcol = _sg_mix_bwd(sv["u"], sv["vn"], w["sg_w"], sv["bcol"], dub)
    g["sg_b"] = dbcol.reshape(SG_GROUPS, SG_CHUNK)
    dp, g["sg_ln_g"], g["sg_ln_b"] = _row_bwd(
        "sg_pre_bwd", _f_sg_pre, [(p, cw, blk(lay.b, cw)), (p, cw, blk(lay.b + cw, cw))],
        [_row(w["sg_ln_g"]), _row(w["sg_ln_b"])], [(du, cw, 0), (dvn, cw, 0)], [((0, 1), lo)], [0, 1],
        place=(lay.width, blk(lay.b, d)), into=dp)
    dz3 = _mm("mm_conv_out_dx", dya, w["conv_out_w"], tb=True)
    g["conv_out_w"] = _mm("mm_conv_out_dw", sv["z3"], dya, ta=True)
    dz1, g["conv_ln_g"], g["conv_ln_b"] = _row_bwd(
        "ln_silu_bwd", _f_ln_silu, [(sv["z1"], cw, 0)], [_row(w["conv_ln_g"]), _row(w["conv_ln_b"])],
        [(dz3, cw, 0)], [((0,), F32)], [0, 1])
    dp, g["conv_dw_w"], g["conv_dw_b"] = _glu_conv_bwd(p, blk(lay.a, 2 * _LANES), w["conv_dw_w"], dz1, dp)
    dh = _mm("mm_in_dx", dp, w["w_in"], tb=True)
    g["w_in"] = _mm("mm_in_dw", sv["h"], dp, ta=True)
    return dx, dh, g


def _local_step(x, positions, target, layers):
    d = x.shape[1]
    tc, ts = _rope_tables(positions)
    ws = [_prep_layer(wl, d) for wl in layers]
    depth = len(ws)
    (h,) = _row_fwd("rms_first_fwd", _f_rms, [(x, d, 0)], [_row(ws[0]["mix_pre_g"])], [(d, _MXU_DT)])
    saved = []
    for l in range(depth):
        g_next = ws[l + 1]["mix_pre_g"] if l + 1 < depth else None
        x, h, sv = _layer_fwd(x, h, ws[l], g_next, tc, ts)
        saved.append(sv)
    loss, dx = _loss_head(x, target)
    grads = [None] * depth
    dh = None
    for l in reversed(range(depth)):
        g_next = ws[l + 1]["mix_pre_g"] if l + 1 < depth else None
        dx, dh, g = _layer_bwd(dx, dh, ws[l], g_next, saved[l], tc, ts)
        if "next_pre_g" in g:
            grads[l + 1]["mix_pre_g"] = g.pop("next_pre_g")
        grads[l] = g
    x0 = saved[0]["x"]
    grad_x, grads[0]["mix_pre_g"] = _row_bwd("rms_first_bwd", _f_x_rms, [(x0, d, 0)], [_row(ws[0]["mix_pre_g"])],
                                             [(dx, d, 0), (dh, d, 0)], [(0, F32)], [0])
    return loss, grad_x, [_unprep_grads(g, d) for g in grads]


_MATRICES = ("w_in", "conv_out_w", "sg_out_w", "mla_w_uq", "mla_w_ukv", "mla_w_o", "w_out", "ffn_w_up", "ffn_w_down")
_F32_GATHERED = ("conv_dw_w", "ffn_dw_w")
_RS_DT = jnp.bfloat16


def _gather_weights(wshard, my_chip):
    names = list(SHARDED)
    hi = {n: wshard[n].astype(jnp.bfloat16) for n in names}
    lo = {n: (wshard[n] - hi[n].astype(F32)).astype(jnp.bfloat16) for n in _F32_GATHERED}
    arrays = [hi[n] for n in names] + [lo[n] for n in _F32_GATHERED]
    gathered = _all_gather_xy(arrays)
    out = {}
    for i, n in enumerate(names + list(_F32_GATHERED)):
        blocks = [jnp.where(my_chip == j, arrays[i], gathered[i][j]) for j in range(4)]
        whole = jnp.concatenate(blocks, axis=1 if n in SHARDED_MID else 2)
        out[n] = out[n].astype(F32) + whole.astype(F32) if n in out else whole
    return out


def _by_destination(name, per_layer):
    out = []
    for gl in per_layer:
        k, n = gl.shape
        if name in SHARDED_MID:
            out.append(gl.reshape(4, k // 4, n))
        else:
            out.append(gl.reshape(k, 4, n // 4).transpose(1, 0, 2))
    return jnp.stack(out, axis=1)


def kernel(x, positions, mix_pre_g, mix_post_g, ffn_pre_g, ffn_post_g, w_in, conv_dw_w, conv_dw_b, conv_ln_g, conv_ln_b, conv_out_w, sg_ln_g, sg_ln_b, sg_w, sg_b, sg_out_w, mla_q_norm_g, mla_w_uq, mla_kv_norm_g, mla_w_ukv, mla_w_o, w_out, ffn_w_up, ffn_dw_w, ffn_dw_b, ffn_w_down, loss_target, m_mix_pre_g, m_mix_post_g, m_ffn_pre_g, m_ffn_post_g, m_w_in, m_conv_dw_w, m_conv_dw_b, m_conv_ln_g, m_conv_ln_b, m_conv_out_w, m_sg_ln_g, m_sg_ln_b, m_sg_w, m_sg_b, m_sg_out_w, m_mla_q_norm_g, m_mla_w_uq, m_mla_kv_norm_g, m_mla_w_ukv, m_mla_w_o, m_w_out, m_ffn_w_up, m_ffn_dw_w, m_ffn_dw_b, m_ffn_w_down, v_mix_pre_g, v_mix_post_g, v_ffn_pre_g, v_ffn_post_g, v_w_in, v_conv_dw_w, v_conv_dw_b, v_conv_ln_g, v_conv_ln_b, v_conv_out_w, v_sg_ln_g, v_sg_ln_b, v_sg_w, v_sg_b, v_sg_out_w, v_mla_q_norm_g, v_mla_w_uq, v_mla_kv_norm_g, v_mla_w_ukv, v_mla_w_o, v_w_out, v_ffn_w_up, v_ffn_dw_w, v_ffn_dw_b, v_ffn_w_down):
    args = dict(locals())
    w = {n: args[n] for n in WEIGHTS}
    m = {n: args["m_" + n] for n in WEIGHTS}
    v = {n: args["v_" + n] for n in WEIGHTS}
    depth = mix_pre_g.shape[0]

    assert depth == 2, "the two cores of a chip split the communication by layer"
    mesh_x, mesh_y, _ = _mesh_pos()
    full = _gather_weights({n: w[n] for n in SHARDED}, 2 * mesh_x + mesh_y)
    layers = []
    for l in range(depth):
        wl = {n: w[n][l] for n in REPLICATED}
        for n in SHARDED:
            wl[n] = full[n][l].astype(_MXU_DT if n in _MATRICES else F32)
        layers.append(wl)

    loss, grad_x, grads = _local_step(x[0], positions[0], loss_target[0], layers)
    loss = lax.psum(loss[0, 0], ("x", "y", "c"))

    out = {}
    names = list(SHARDED)
    reduced = _reduce_scatter(names, [_by_destination(n, [grads[l][n] for l in range(depth)]) for n in names])
    for n, gr in zip(names, reduced):
        out[n] = (gr, *_adamw("adamw_" + n, w[n], gr, m[n], v[n]))
    rep = list(REPLICATED)
    g_rep = [jnp.stack([grads[l][n].reshape(w[n].shape[1:]) for l in range(depth)]) for n in rep]
    for n, *res in zip(rep, *_all_reduce_adamw(g_rep, [w[n] for n in rep], [m[n] for n in rep], [v[n] for n in rep])):
        out[n] = tuple(res)
    return (loss, grad_x[None], *[out[n][i] for i in range(4) for n in WEIGHTS])
```

```python
import functools
import math

import jax
import jax.numpy as jnp
from jax import lax
from jax.experimental import pallas as pl
from jax.experimental.pallas import tpu as pltpu

F32 = jnp.float32
_MXU_DT = jnp.bfloat16
_VMEM_LIMIT = 48 * 1024 * 1024
_LANES = 128
_MESH = pl.DeviceIdType.MESH

N_HEADS = 8
QK_NOPE = 64
QK_ROPE = 32
V_HEAD = 64
HEAD_PAD = 128
SG_GROUPS = 4
SG_CHUNK = 128
CONV_K = 31
FFN_K = 3
ROPE_THETA = 10000.0
EPS = 1e-6
ADAM_LR, ADAM_B1, ADAM_B2, ADAM_EPS, ADAM_WD, ADAM_STEP = 0.001, 0.9, 0.999, 1e-08, 0.01, 10

SHARDED_LAST = ("w_in", "conv_dw_w", "conv_out_w", "sg_out_w", "mla_w_uq", "mla_w_ukv", "mla_w_o", "ffn_w_up",
                "ffn_dw_w")
SHARDED_MID = ("w_out", "ffn_w_down")
SHARDED = SHARDED_LAST + SHARDED_MID
WEIGHTS = ("mix_pre_g", "mix_post_g", "ffn_pre_g", "ffn_post_g", "w_in", "conv_dw_w", "conv_dw_b", "conv_ln_g",
           "conv_ln_b", "conv_out_w", "sg_ln_g", "sg_ln_b", "sg_w", "sg_b", "sg_out_w", "mla_q_norm_g", "mla_w_uq",
           "mla_kv_norm_g", "mla_w_ukv", "mla_w_o", "w_out", "ffn_w_up", "ffn_dw_w", "ffn_dw_b", "ffn_w_down")
REPLICATED = tuple(n for n in WEIGHTS if n not in SHARDED)


def _cparams(sem=None):
    return pltpu.CompilerParams(dimension_semantics=sem, vmem_limit_bytes=_VMEM_LIMIT)


def _pick(n, cands):
    for c in cands:
        if n % c == 0:
            return c
    return n


def _largest_tile(dim, cap):
    for t in range(min(cap, dim) // _LANES * _LANES, 0, -_LANES):
        if dim % t == 0:
            return t
    return dim


_MM_VMEM_BUDGET = 36 * 1024 * 1024
_MM_TM_CAP, _MM_TN_CAP, _MM_TK_CAP = 1024, 1536, 3072


def _mm(name, a, b, *, ta=False, tb=False, out_dtype=F32, a_halves=False, b_halves=False, out_quarters=False):
    assert not (a_halves and ta) and not (b_halves and tb)
    if a_halves:
        m, kdim = a.shape[1], 2 * a.shape[2]
    else:
        (kdim, m) = a.shape if ta else a.shape[::-1]
    if b_halves:
        kdim2, n = b.shape[1], 2 * b.shape[2]
    else:
        (n, kdim2) = b.shape if tb else b.shape[::-1]
    assert kdim == kdim2, (a.shape, b.shape, ta, tb)
    tk = _largest_tile(kdim // 2 if a_halves else kdim, _MM_TK_CAP)
    tn = _largest_tile(n // 4 if out_quarters else (n // 2 if b_halves else n), _MM_TN_CAP)
    nk = kdim // tk
    ab, bb, ob = a.dtype.itemsize, b.dtype.itemsize, jnp.dtype(out_dtype).itemsize
    tm = _largest_tile(m, _MM_TM_CAP)
    vmem = lambda t: 2 * (t * tk * ab + tk * tn * bb + t * tn * ob) + (t * tn * 4 if nk > 1 else 0)
    while vmem(tm) > _MM_VMEM_BUDGET and tm > _LANES:
        tm = _largest_tile(m, tm - _LANES)
    dims = (((0 if ta else 1,), (1 if tb else 0,)), ((), ()))

    def dot(a_ref, b_ref):
        return lax.dot_general(a_ref[...].astype(_MXU_DT), b_ref[...].astype(_MXU_DT), dims,
                               preferred_element_type=F32)

    def body_one(a_ref, b_ref, o_ref):
        o_ref[...] = dot(a_ref, b_ref).astype(o_ref.dtype)

    def body_acc(a_ref, b_ref, o_ref, acc_ref):
        k = pl.program_id(2)

        @pl.when(k == 0)
        def _():
            acc_ref[...] = jnp.zeros_like(acc_ref)

        acc_ref[...] += dot(a_ref, b_ref)

        @pl.when(k == nk - 1)
        def _():
            o_ref[...] = acc_ref[...].astype(o_ref.dtype)

    if a_halves:
        per = nk // 2
        a_spec = pl.BlockSpec((None, tm, tk), lambda i, j, k: (k // per, i, k % per))
    elif ta:
        a_spec = pl.BlockSpec((tk, tm), lambda i, j, k: (k, i))
    else:
        a_spec = pl.BlockSpec((tm, tk), lambda i, j, k: (i, k))
    if b_halves:
        per_b = n // 2 // tn
        b_spec = pl.BlockSpec((None, tk, tn), lambda i, j, k: (j // per_b, k, j % per_b))
    elif tb:
        b_spec = pl.BlockSpec((tn, tk), lambda i, j, k: (j, k))
    else:
        b_spec = pl.BlockSpec((tk, tn), lambda i, j, k: (k, j))
    if out_quarters:
        per_o = n // 4 // tn
        o_spec = pl.BlockSpec((None, tm, tn), lambda i, j, k: (j // per_o, i, j % per_o))
        o_shape = jax.ShapeDtypeStruct((4, m, n // 4), out_dtype)
    else:
        o_spec = pl.BlockSpec((tm, tn), lambda i, j, k: (i, j))
        o_shape = jax.ShapeDtypeStruct((m, n), out_dtype)
    return pl.pallas_call(
        body_one if nk == 1 else body_acc, name=name, grid=(m // tm, n // tn, nk),
        in_specs=[a_spec, b_spec], out_specs=o_spec, out_shape=o_shape,
        scratch_shapes=[] if nk == 1 else [pltpu.VMEM((tm, tn), F32)],
        compiler_params=_cparams(("parallel", "parallel", "arbitrary")),
    )(a, b)


def _row_spec(tm, width, idx):
    return pl.BlockSpec((tm, width), lambda i: (i, idx))


def _full_spec(shape):
    zeros = (0,) * len(shape)
    return pl.BlockSpec(shape, lambda i: zeros)


def _row_fwd(name, fn, rows, params, outs, tm=256):
    s = rows[0][0].shape[0]
    nr, npar = len(rows), len(params)

    def body(*refs):
        vals = [r[...].astype(F32) for r in refs[:nr + npar]]
        res = fn(*vals)
        for o_ref, r in zip(refs[nr + npar:], res):
            o_ref[...] = r.astype(o_ref.dtype)

    return pl.pallas_call(
        body, name=name, grid=(s // tm,),
        in_specs=[_row_spec(tm, w, i) for _, w, i in rows] + [_full_spec(p.shape) for p in params],
        out_specs=[_row_spec(tm, w, 0) for w, _ in outs],
        out_shape=[jax.ShapeDtypeStruct((s, w), dt) for w, dt in outs],
        compiler_params=_cparams(("parallel",)),
    )(*[r[0] for r in rows], *params)


def _row_bwd(name, fn, rows, params, cots, row_grads, param_grads, tm=256, place=None, into=None):
    s = rows[0][0].shape[0]
    nr, npar, nc = len(rows), len(params), len(cots)
    row_grads = [((idxs,) if isinstance(idxs, int) else tuple(idxs), dt) for idxs, dt in row_grads]
    widths = [sum(rows[i][1] for i in idxs) for idxs, _ in row_grads]

    def body(*refs):
        i = pl.program_id(0)
        vals = [r[...].astype(F32) for r in refs[:nr + npar]]
        cvals = tuple(r[...].astype(F32) for r in refs[nr + npar:nr + npar + nc])
        _, vjp = jax.vjp(fn, *vals)
        grads = vjp(cvals)
        outs = refs[nr + npar + nc + (into is not None):]
        for o_ref, (idxs, _) in zip(outs, row_grads):
            pos = 0
            for idx in idxs:
                o_ref[:, pos:pos + rows[idx][1]] = grads[idx].astype(o_ref.dtype)
                pos += rows[idx][1]
        for o_ref, idx in zip(outs[len(row_grads):], param_grads):
            @pl.when(i == 0)
            def _(o_ref=o_ref):
                o_ref[...] = jnp.zeros_like(o_ref)

            o_ref[...] += grads[nr + idx]

    out_specs = [_row_spec(tm, w, 0) for w in widths] + [_full_spec(params[idx].shape) for idx in param_grads]
    out_shape = ([jax.ShapeDtypeStruct((s, w), dt) for w, (_, dt) in zip(widths, row_grads)]
                 + [jax.ShapeDtypeStruct(params[idx].shape, F32) for idx in param_grads])
    extra, aliases = [], {}
    if place is not None:
        out_specs[0] = _row_spec(tm, widths[0], place[1])
        out_shape[0] = jax.ShapeDtypeStruct((s, place[0]), row_grads[0][1])
    if into is not None:
        extra, aliases = [into], {nr + npar + nc: 0}
    return pl.pallas_call(
        body, name=name, grid=(s // tm,),
        in_specs=([_row_spec(tm, w, i) for _, w, i in rows] + [_full_spec(p.shape) for p in params]
                  + [_row_spec(tm, w, i) for _, w, i in cots] + [_ANY] * len(extra)),
        out_specs=out_specs, out_shape=out_shape, input_output_aliases=aliases,
        compiler_params=_cparams(("arbitrary",)),
    )(*[r[0] for r in rows], *params, *[c[0] for c in cots], *extra)


def _rms(x, g):
    return x * lax.rsqrt(jnp.mean(x * x, axis=-1, keepdims=True) + EPS) * g


def _ln(x, g, b):
    mu = jnp.mean(x, axis=-1, keepdims=True)
    xc = x - mu
    var = jnp.mean(xc * xc, axis=-1, keepdims=True)
    return xc * lax.rsqrt(var + EPS) * g + b


def _sigmoid(x):
    return 1.0 / (1.0 + jnp.exp(-x))


def _gelu(x):
    return x * (0.5 * (1.0 + jnp.tanh(math.sqrt(2.0 / math.pi) * (x + 0.044715 * (x * x * x)))))


def _f_rms(x, g):
    return (_rms(x, g),)


def _f_x_rms(x, g):
    return (x, _rms(x, g))


def _f_ln_silu(z, g, b):
    y = _ln(z, g, b)
    return (y * _sigmoid(y),)


def _f_sg_pre(bu, bv, g, b):
    return (_gelu(bu), _ln(_gelu(bv), g, b))


def _f_merge(g0, g1, g2, ya, yb, yc):
    return (_sigmoid(g0) * ya + _sigmoid(g1) * yb + _sigmoid(g2) * yc,)


def _f_resid_rms(x, t, g_post):
    return (x + _rms(t, g_post),)


def _f_resid_rms_rms(x, t, g_post, g_next):
    x1 = x + _rms(t, g_post)
    return (x1, _rms(x1, g_next))


def _f_geglu(zg, zv):
    return _gelu(zg) * zv


_CONV_TILE_ELEMS = 16 * 1024


def _conv_tr(c):
    return _CONV_TILE_ELEMS // c


def _conv_tile(zp_ref, w_ref, bias, k_taps, off, r0):
    c = zp_ref.shape[1]
    tr = _conv_tr(c)
    acc = jnp.broadcast_to(bias, (tr, c))
    for k in range(k_taps):
        acc = acc + w_ref[k:k + 1, :] * zp_ref[r0 + off + k:r0 + off + k + tr, :]
    return acc


def _conv_bwd_input_tile(dzp_ref, w_ref, k_taps, r0):
    c = dzp_ref.shape[1]
    tr = _conv_tr(c)
    acc = jnp.zeros((tr, c), F32)
    for k in range(k_taps):
        s0 = r0 + (k_taps - 1) - k
        acc = acc + w_ref[k:k + 1, :] * dzp_ref[s0:s0 + tr, :]
    return acc


def _conv_bwd_weight(dzp_ref, zp_ref, dw_ref, db_ref, k_taps, off, s):
    c = zp_ref.shape[1]
    tr = _conv_tr(c)
    fold = lambda v: jnp.sum(v.reshape(tr // 8, 8, c), axis=0)
    for k in range(k_taps):
        acc = jnp.zeros((8, c), F32)
        for r in range(s // tr):
            r0 = r * tr
            acc = acc + fold(dzp_ref[r0:r0 + tr, :] * zp_ref[r0 + off + k:r0 + off + k + tr, :])
        dw_ref[k:k + 1, :] = jnp.sum(acc, axis=0, keepdims=True)
    acc = jnp.zeros((8, c), F32)
    for r in range(s // tr):
        acc = acc + fold(dzp_ref[r * tr:(r + 1) * tr, :])
    db_ref[...] = jnp.sum(acc, axis=0, keepdims=True)


def _glu_conv_fwd(p, blk0, w, b):
    s = p.shape[0]
    k_taps, c = w.shape
    cb, pad = _LANES, 32
    off = pad - (k_taps - 1)

    def body(a_ref, w_ref, b_ref, o_ref, zp_ref):
        zp_ref[0:pad, :] = jnp.zeros((pad, cb), F32)
        zp_ref[pad:pad + s, :] = a_ref[:, 0:cb] * _sigmoid(a_ref[:, cb:2 * cb])
        tr = _conv_tr(cb)
        for r in range(s // tr):
            o_ref[r * tr:(r + 1) * tr, :] = _conv_tile(zp_ref, w_ref, b_ref[...], k_taps, off, r * tr)

    return pl.pallas_call(
        body, name="glu_conv_fwd", grid=(c // cb,),
        in_specs=[pl.BlockSpec((s, 2 * cb), lambda j: (0, blk0 + j)),
                  pl.BlockSpec((k_taps, cb), lambda j: (0, j)), pl.BlockSpec((1, cb), lambda j: (0, j))],
        out_specs=pl.BlockSpec((s, cb), lambda j: (0, j)),
        out_shape=jax.ShapeDtypeStruct((s, c), F32),
        scratch_shapes=[pltpu.VMEM((s + pad, cb), F32)],
        compiler_params=_cparams(("parallel",)),
    )(p, w, b)


def _glu_conv_bwd(p, blk0, w, dz, dp):
    s = p.shape[0]
    k_taps, c = w.shape
    cb, pad = _LANES, 32
    off = pad - (k_taps - 1)

    def body(a_ref, w_ref, dz_ref, dp_in, da_ref, dw_ref, db_ref, zp_ref, dzp_ref):
        zp_ref[0:pad, :] = jnp.zeros((pad, cb), F32)
        zp_ref[pad:pad + s, :] = a_ref[:, 0:cb] * _sigmoid(a_ref[:, cb:2 * cb])
        dzp_ref[0:s, :] = dz_ref[...]
        dzp_ref[s:s + pad, :] = jnp.zeros((pad, cb), F32)
        tr = _conv_tr(cb)
        for r in range(s // tr):
            rows = slice(r * tr, (r + 1) * tr)
            dz0 = _conv_bwd_input_tile(dzp_ref, w_ref, k_taps, r * tr)
            sg = _sigmoid(a_ref[rows, cb:2 * cb])
            da_ref[rows, 0:cb] = (dz0 * sg).astype(da_ref.dtype)
            da_ref[rows, cb:2 * cb] = (dz0 * a_ref[rows, 0:cb] * sg * (1.0 - sg)).astype(da_ref.dtype)
        _conv_bwd_weight(dzp_ref, zp_ref, dw_ref, db_ref, k_taps, off, s)

    return pl.pallas_call(
        body, name="glu_conv_bwd", grid=(c // cb,),
        in_specs=[pl.BlockSpec((s, 2 * cb), lambda j: (0, blk0 + j)),
                  pl.BlockSpec((k_taps, cb), lambda j: (0, j)), pl.BlockSpec((s, cb), lambda j: (0, j)), _ANY],
        out_specs=[pl.BlockSpec((s, 2 * cb), lambda j: (0, blk0 + j)),
                   pl.BlockSpec((k_taps, cb), lambda j: (0, j)), pl.BlockSpec((1, cb), lambda j: (0, j))],
        out_shape=[jax.ShapeDtypeStruct(dp.shape, dp.dtype),
                   jax.ShapeDtypeStruct((k_taps, c), F32), jax.ShapeDtypeStruct((1, c), F32)],
        scratch_shapes=[pltpu.VMEM((s + pad, cb), F32), pltpu.VMEM((s + pad, cb), F32)],
        input_output_aliases={3: 0},
        compiler_params=_cparams(("parallel",)),
    )(p, w, dz, dp)


def _conv_geglu_fwd(up, w, b):
    s, f2 = up.shape
    f = f2 // 2
    k_taps = w.shape[0]
    cb, pad = _FFN_CB, 8
    off = pad - (k_taps - 1)
    nb = f // cb

    def body(ug_ref, uv_ref, wg_ref, wv_ref, bg_ref, bv_ref, o_ref, z_ref, w_ref, b_ref):
        _pair(w_ref, wg_ref[...], wv_ref[...], cb)
        _pair(b_ref, bg_ref[...], bv_ref[...], cb)
        z_ref[0:pad, :] = jnp.zeros((pad, 2 * cb), F32)
        z_ref[pad:pad + s, 0:cb] = ug_ref[...]
        z_ref[pad:pad + s, cb:2 * cb] = uv_ref[...]
        tr = _conv_tr(2 * cb)
        for r in range(s // tr):
            z = _conv_tile(z_ref, w_ref, b_ref[...], k_taps, off, r * tr)
            o_ref[r * tr:(r + 1) * tr, :] = _f_geglu(z[:, 0:cb], z[:, cb:2 * cb]).astype(o_ref.dtype)

    two = lambda rows_: [pl.BlockSpec((rows_, cb), lambda j: (0, j)), pl.BlockSpec((rows_, cb), lambda j: (0, nb + j))]
    return pl.pallas_call(
        body, name="conv_geglu_fwd", grid=(nb,),
        in_specs=two(s) + two(k_taps) + two(1),
        out_specs=pl.BlockSpec((s, cb), lambda j: (0, j)),
        out_shape=jax.ShapeDtypeStruct((s, f), _MXU_DT),
        scratch_shapes=[pltpu.VMEM((s + pad, 2 * cb), F32), pltpu.VMEM((k_taps, 2 * cb), F32),
                        pltpu.VMEM((1, 2 * cb), F32)],
        compiler_params=_cparams(("parallel",)),
    )(up, up, w, w, b, b)


def _pair(dst_ref, first, second, cb):
    dst_ref[:, 0:cb] = first
    dst_ref[:, cb:2 * cb] = second


def _conv_geglu_bwd(up, w, b, dact):
    s, f2 = up.shape
    f = f2 // 2
    k_taps = w.shape[0]
    cb, pad = _FFN_CB, 8
    off = pad - (k_taps - 1)
    nb = f // cb

    def body(ug_ref, uv_ref, wg_ref, wv_ref, bg_ref, bv_ref, da_ref, du_ref, dw_ref, db_ref, z_ref, dz_ref, w_ref,
             b_ref, dw_sc, db_sc):
        _pair(w_ref, wg_ref[...], wv_ref[...], cb)
        _pair(b_ref, bg_ref[...], bv_ref[...], cb)
        z_ref[0:pad, :] = jnp.zeros((pad, 2 * cb), F32)
        z_ref[pad:pad + s, 0:cb] = ug_ref[...]
        z_ref[pad:pad + s, cb:2 * cb] = uv_ref[...]
        dz_ref[s:s + pad, :] = jnp.zeros((pad, 2 * cb), F32)
        tr = _conv_tr(2 * cb)
        for r in range(s // tr):
            rows = slice(r * tr, (r + 1) * tr)
            z = _conv_tile(z_ref, w_ref, b_ref[...], k_taps, off, r * tr)
            _, vjp = jax.vjp(_f_geglu, z[:, 0:cb], z[:, cb:2 * cb])
            dzg, dzv = vjp(da_ref[rows, :].astype(F32))
            dz_ref[rows, 0:cb] = dzg
            dz_ref[rows, cb:2 * cb] = dzv
        for r in range(s // tr):
            rows = slice(r * tr, (r + 1) * tr)
            du = _conv_bwd_input_tile(dz_ref, w_ref, k_taps, r * tr).astype(du_ref.dtype)
            du_ref[0, rows, :] = du[:, 0:cb]
            du_ref[1, rows, :] = du[:, cb:2 * cb]
        _conv_bwd_weight(dz_ref, z_ref, dw_sc, db_sc, k_taps, off, s)
        for half in range(2):
            dw_ref[half] = dw_sc[:, half * cb:(half + 1) * cb]
            db_ref[half] = db_sc[:, half * cb:(half + 1) * cb]

    two = lambda rows_: [pl.BlockSpec((rows_, cb), lambda j: (0, j)), pl.BlockSpec((rows_, cb), lambda j: (0, nb + j))]
    both = lambda rows_: pl.BlockSpec((2, rows_, cb), lambda j: (0, 0, j))
    return pl.pallas_call(
        body, name="conv_geglu_bwd", grid=(nb,),
        in_specs=two(s) + two(k_taps) + two(1) + [pl.BlockSpec((s, cb), lambda j: (0, j))],
        out_specs=[both(s), both(k_taps), both(1)],
        out_shape=[jax.ShapeDtypeStruct((2, s, f), _MXU_DT), jax.ShapeDtypeStruct((2, k_taps, f), F32),
                   jax.ShapeDtypeStruct((2, 1, f), F32)],
        scratch_shapes=[pltpu.VMEM((s + pad, 2 * cb), F32), pltpu.VMEM((s + pad, 2 * cb), F32),
                        pltpu.VMEM((k_taps, 2 * cb), F32), pltpu.VMEM((1, 2 * cb), F32),
                        pltpu.VMEM((k_taps, 2 * cb), F32), pltpu.VMEM((1, 2 * cb), F32)],
        compiler_params=_cparams(("parallel",)),
    )(up, up, w, w, b, b, dact)


def _tril_mask():
    t = lax.broadcasted_iota(jnp.int32, (SG_CHUNK, SG_CHUNK), 0)
    s = lax.broadcasted_iota(jnp.int32, (SG_CHUNK, SG_CHUNK), 1)
    return t >= s


def _sg_mix_fwd(u, vn, w, bcol):
    s, c = u.shape
    gw = c // SG_GROUPS

    def body(u_ref, v_ref, w_ref, b_ref, o_ref):
        wm = jnp.where(_tril_mask(), w_ref[0], 0.0).astype(_MXU_DT)
        for n in range(s // SG_CHUNK):
            rows = slice(n * SG_CHUNK, (n + 1) * SG_CHUNK)
            mixed = jnp.dot(wm, v_ref[rows, :], preferred_element_type=F32) + b_ref[0]
            o_ref[rows, :] = (u_ref[rows, :] * mixed).astype(o_ref.dtype)

    return pl.pallas_call(
        body, name="sg_mix_fwd", grid=(SG_GROUPS,),
        in_specs=[pl.BlockSpec((s, gw), lambda g: (0, g)), pl.BlockSpec((s, gw), lambda g: (0, g)),
                  pl.BlockSpec((1, SG_CHUNK, SG_CHUNK), lambda g: (g, 0, 0)),
                  pl.BlockSpec((1, SG_CHUNK, 1), lambda g: (g, 0, 0))],
        out_specs=pl.BlockSpec((s, gw), lambda g: (0, g)),
        out_shape=jax.ShapeDtypeStruct((s, c), _MXU_DT),
        compiler_params=_cparams(("parallel",)),
    )(u, vn, w, bcol)


def _sg_mix_bwd(u, vn, w, bcol, dub):
    s, c = u.shape
    gw = c // SG_GROUPS

    def body(u_ref, v_ref, w_ref, b_ref, d_ref, du_ref, dv_ref, dw_ref, db_ref):
        mask = _tril_mask()
        wm = jnp.where(mask, w_ref[0], 0.0).astype(_MXU_DT)
        dw = jnp.zeros((SG_CHUNK, SG_CHUNK), F32)
        db = jnp.zeros((SG_CHUNK, 1), F32)
        for n in range(s // SG_CHUNK):
            rows = slice(n * SG_CHUNK, (n + 1) * SG_CHUNK)
            v = v_ref[rows, :]
            d = d_ref[rows, :].astype(F32)
            mixed = jnp.dot(wm, v, preferred_element_type=F32) + b_ref[0]
            du_ref[rows, :] = d * mixed
            dmix = d * u_ref[rows, :]
            dmix_lo = dmix.astype(_MXU_DT)
            dv_ref[rows, :] = lax.dot_general(wm, dmix_lo, (((0,), (0,)), ((), ())), preferred_element_type=F32)
            dw = dw + lax.dot_general(dmix_lo, v, (((1,), (1,)), ((), ())), preferred_element_type=F32)
            db = db + jnp.sum(dmix, axis=1, keepdims=True)
        dw_ref[0] = jnp.where(mask, dw, 0.0)
        db_ref[0] = db

    return pl.pallas_call(
        body, name="sg_mix_bwd", grid=(SG_GROUPS,),
        in_specs=[pl.BlockSpec((s, gw), lambda g: (0, g)), pl.BlockSpec((s, gw), lambda g: (0, g)),
                  pl.BlockSpec((1, SG_CHUNK, SG_CHUNK), lambda g: (g, 0, 0)),
                  pl.BlockSpec((1, SG_CHUNK, 1), lambda g: (g, 0, 0)), pl.BlockSpec((s, gw), lambda g: (0, g))],
        out_specs=[pl.BlockSpec((s, gw), lambda g: (0, g)), pl.BlockSpec((s, gw), lambda g: (0, g)),
                   pl.BlockSpec((1, SG_CHUNK, SG_CHUNK), lambda g: (g, 0, 0)),
                   pl.BlockSpec((1, SG_CHUNK, 1), lambda g: (g, 0, 0))],
        out_shape=[jax.ShapeDtypeStruct((s, c), F32), jax.ShapeDtypeStruct((s, c), F32),
                   jax.ShapeDtypeStruct((SG_GROUPS, SG_CHUNK, SG_CHUNK), F32),
                   jax.ShapeDtypeStruct((SG_GROUPS, SG_CHUNK, 1), F32)],
        compiler_params=_cparams(("parallel",)),
    )(u, vn, w, bcol, dub)


def _rope_fwd(q2, kv2, p, krm_idx, krs_idx, tc, ts):
    s = q2.shape[0]
    hw = N_HEADS * HEAD_PAD
    tm = 256

    def body(qm_ref, qs_ref, kn_ref, v_ref, krm_ref, krs_ref, tc_ref, ts_ref, q_ref, k_ref, vo_ref):
        tcv, tsv = tc_ref[...], ts_ref[...]
        kpe = krm_ref[...] * tcv + krs_ref[...] * tsv
        for h in range(N_HEADS):
            cols = slice(h * HEAD_PAD, (h + 1) * HEAD_PAD)
            q_ref[:, cols] = (qm_ref[:, cols] * tcv + qs_ref[:, cols] * tsv).astype(q_ref.dtype)
            k_ref[:, cols] = (kn_ref[:, cols] + kpe).astype(k_ref.dtype)
        vo_ref[...] = v_ref[...].astype(vo_ref.dtype)

    return pl.pallas_call(
        body, name="rope_fwd", grid=(s // tm,),
        in_specs=[_row_spec(tm, hw, 0), _row_spec(tm, hw, 1), _row_spec(tm, hw, 0), _row_spec(tm, hw, 1),
                  _row_spec(tm, HEAD_PAD, krm_idx), _row_spec(tm, HEAD_PAD, krs_idx),
                  _row_spec(tm, HEAD_PAD, 0), _row_spec(tm, HEAD_PAD, 0)],
        out_specs=[_row_spec(tm, hw, 0)] * 3,
        out_shape=[jax.ShapeDtypeStruct((s, hw), _MXU_DT)] * 3,
        compiler_params=_cparams(("parallel",)),
    )(q2, q2, kv2, kv2, p, p, tc, ts)


def _rope_bwd(dq, dk, dv, tc, ts, dp, kr_blk):
    s = dq.shape[0]
    hw = N_HEADS * HEAD_PAD
    tm = 256

    def body(dq_ref, dk_ref, dv_ref, tc_ref, ts_ref, dp_in, dq2_ref, dkv2_ref, dkr_ref):
        tcv, tsv = tc_ref[...], ts_ref[...]
        dkpe = jnp.zeros((tm, HEAD_PAD), F32)
        for h in range(N_HEADS):
            cols = slice(h * HEAD_PAD, (h + 1) * HEAD_PAD)
            dqh = dq_ref[:, cols]
            dq2_ref[:, cols] = (dqh * tcv).astype(dq2_ref.dtype)
            dq2_ref[:, hw + h * HEAD_PAD:hw + (h + 1) * HEAD_PAD] = (dqh * tsv).astype(dq2_ref.dtype)
            dkpe = dkpe + dk_ref[:, cols]
        dkv2_ref[:, 0:hw] = dk_ref[...].astype(dkv2_ref.dtype)
        dkv2_ref[:, hw:2 * hw] = dv_ref[...].astype(dkv2_ref.dtype)
        dkr_ref[:, 0:HEAD_PAD] = (dkpe * tcv).astype(dkr_ref.dtype)
        dkr_ref[:, HEAD_PAD:2 * HEAD_PAD] = (dkpe * tsv).astype(dkr_ref.dtype)
        dkr_ref[:, 2 * HEAD_PAD:3 * HEAD_PAD] = jnp.zeros((tm, HEAD_PAD), dkr_ref.dtype)

    return pl.pallas_call(
        body, name="rope_bwd", grid=(s // tm,),
        in_specs=[_row_spec(tm, hw, 0)] * 3 + [_row_spec(tm, HEAD_PAD, 0)] * 2 + [_ANY],
        out_specs=[_row_spec(tm, 2 * hw, 0), _row_spec(tm, 2 * hw, 0), _row_spec(tm, 3 * HEAD_PAD, kr_blk)],
        out_shape=[jax.ShapeDtypeStruct((s, 2 * hw), _MXU_DT), jax.ShapeDtypeStruct((s, 2 * hw), _MXU_DT),
                   jax.ShapeDtypeStruct(dp.shape, dp.dtype)],
        input_output_aliases={5: 2},
        compiler_params=_cparams(("parallel",)),
    )(dq, dk, dv, tc, ts, dp)


_ATTN_TQ = 256
_ATTN_SCALE = (QK_NOPE + QK_ROPE) ** -0.5


def _attn_probs(q, k, i):
    s = k.shape[0]
    sc = lax.dot_general(q, k, (((1,), (1,)), ((), ())), preferred_element_type=F32) * _ATTN_SCALE
    row = i * _ATTN_TQ + lax.broadcasted_iota(jnp.int32, (_ATTN_TQ, s), 0)
    col = lax.broadcasted_iota(jnp.int32, (_ATTN_TQ, s), 1)
    sc = jnp.where(row >= col, sc, jnp.finfo(F32).min)
    e = jnp.exp(sc - jnp.max(sc, axis=1, keepdims=True))
    return e / jnp.sum(e, axis=1, keepdims=True)


def _per_query_block(s, fn):
    i = pl.program_id(1)
    for n in range(s // _ATTN_TQ):
        @pl.when(i == n)
        def _(n=n):
            fn(n, (n + 1) * _ATTN_TQ)


def _attn_fwd(q, k, v):
    s = q.shape[0]

    def body(q_ref, k_ref, v_ref, o_ref):
        def block(n, kl):
            p = _attn_probs(q_ref[...], k_ref[0:kl, :], n)
            o_ref[...] = jnp.dot(p.astype(_MXU_DT), v_ref[0:kl, :], preferred_element_type=F32).astype(o_ref.dtype)

        _per_query_block(s, block)

    qspec = pl.BlockSpec((_ATTN_TQ, HEAD_PAD), lambda h, i: (i, h))
    kspec = pl.BlockSpec((s, HEAD_PAD), lambda h, i: (0, h))
    return pl.pallas_call(
        body, name="attn_fwd", grid=(N_HEADS, s // _ATTN_TQ),
        in_specs=[qspec, kspec, kspec], out_specs=qspec,
        out_shape=jax.ShapeDtypeStruct(q.shape, _MXU_DT),
        compiler_params=_cparams(("parallel", "parallel")),
    )(q, k, v)


def _attn_bwd(q, k, v, do):
    s = q.shape[0]

    def body(q_ref, k_ref, v_ref, do_ref, dq_ref, dk_ref, dv_ref):
        i = pl.program_id(1)

        @pl.when(i == 0)
        def _():
            dk_ref[...] = jnp.zeros_like(dk_ref)
            dv_ref[...] = jnp.zeros_like(dv_ref)

        def block(n, kl):
            qv, kv, dov = q_ref[...], k_ref[0:kl, :], do_ref[...]
            p = _attn_probs(qv, kv, n)
            dp = lax.dot_general(dov, v_ref[0:kl, :], (((1,), (1,)), ((), ())), preferred_element_type=F32)
            delta = jnp.sum(p * dp, axis=1, keepdims=True)
            ds = (p * (dp - delta) * _ATTN_SCALE).astype(_MXU_DT)
            dq_ref[...] = jnp.dot(ds, kv, preferred_element_type=F32)
            dk_ref[0:kl, :] += lax.dot_general(ds, qv, (((0,), (0,)), ((), ())), preferred_element_type=F32)
            dv_ref[0:kl, :] += lax.dot_general(p.astype(_MXU_DT), dov, (((0,), (0,)), ((), ())),
                                               preferred_element_type=F32)

        _per_query_block(s, block)

    qspec = pl.BlockSpec((_ATTN_TQ, HEAD_PAD), lambda h, i: (i, h))
    kspec = pl.BlockSpec((s, HEAD_PAD), lambda h, i: (0, h))
    return pl.pallas_call(
        body, name="attn_bwd", grid=(N_HEADS, s // _ATTN_TQ),
        in_specs=[qspec, kspec, kspec, qspec], out_specs=[qspec, kspec, kspec],
        out_shape=[jax.ShapeDtypeStruct(q.shape, F32)] * 3,
        compiler_params=_cparams(("parallel", "arbitrary")),
    )(q, k, v, do)


def _loss_head(y, target):
    s, d = y.shape
    tm = 256

    def body(y_ref, t_ref, loss_ref, dy_ref):
        @pl.when(pl.program_id(0) == 0)
        def _():
            loss_ref[...] = jnp.zeros_like(loss_ref)

        err = y_ref[...] - t_ref[...]
        loss_ref[...] += 0.5 * jnp.sum(jnp.mean(err * err, axis=-1, keepdims=True), axis=0, keepdims=True)
        dy_ref[...] = err * (1.0 / d)

    return pl.pallas_call(
        body, name="loss_head", grid=(s // tm,),
        in_specs=[_row_spec(tm, d, 0), _row_spec(tm, d, 0)],
        out_specs=[_full_spec((1, 1)), _row_spec(tm, d, 0)],
        out_shape=[jax.ShapeDtypeStruct((1, 1), F32), jax.ShapeDtypeStruct((s, d), F32)],
        compiler_params=_cparams(("arbitrary",)),
    )(y, target)


def _adamw_math(w, g, m, v):
    mn = ADAM_B1 * m + (1.0 - ADAM_B1) * g
    vn = ADAM_B2 * v + (1.0 - ADAM_B2) * (g * g)
    m_hat = mn / (1.0 - ADAM_B1 ** ADAM_STEP)
    v_hat = vn / (1.0 - ADAM_B2 ** ADAM_STEP)
    return -ADAM_LR * (m_hat / (jnp.sqrt(v_hat) + ADAM_EPS) + ADAM_WD * w), mn, vn


def _adamw(name, w, g, m, v):
    l, k, n = w.shape
    tk = _slab_rows(k, n)

    def body(w_ref, g_ref, m_ref, v_ref, d_ref, mo_ref, vo_ref):
        d_ref[...], mo_ref[...], vo_ref[...] = _adamw_math(w_ref[...], g_ref[...], m_ref[...], v_ref[...])

    spec = pl.BlockSpec((1, tk, n), lambda i, j: (i, j, 0))
    return pl.pallas_call(
        body, name=name, grid=(l, k // tk), in_specs=[spec] * 4, out_specs=[spec] * 3,
        out_shape=[jax.ShapeDtypeStruct(w.shape, F32)] * 3,
        compiler_params=_cparams(("parallel", "parallel")),
    )(w, g, m, v)


_ANY = pl.BlockSpec(memory_space=pl.ANY)


def _mesh_pos():
    return lax.axis_index("x"), lax.axis_index("y"), lax.axis_index("c")


def _other_chips(x, y):
    return [(1 - x, y), (x, 1 - y), (1 - x, 1 - y)]


def _remote(src, dst, send_sem, recv_sem, to):
    return pltpu.make_async_remote_copy(src_ref=src, dst_ref=dst, send_sem=send_sem, recv_sem=recv_sem,
                                        device_id=to, device_id_type=_MESH)


def _all_gather_xy(shards):
    n = len(shards)

    def body(*refs):
        ins, outs, (send_sems, recv_sems) = refs[:n], refs[n:2 * n], refs[2 * n:]
        x, y, c = _mesh_pos()
        sibling = (x, y, 1 - c)
        chips = _other_chips(x, y)
        idx = lambda chip: 2 * chip[0] + chip[1]

        def copy(a, k, src, dst, to):
            return _remote(src, dst, send_sems.at[a, k], recv_sems.at[a, k], to)

        started = []
        for k, chip in enumerate(chips):
            for a in range(n):
                cp = copy(a, k, ins[a].at[c], outs[a].at[2 * x + y, c], (*chip, c))
                cp.start()
                started.append(cp)
        for k, chip in enumerate(chips):
            for a in range(n):
                landed = outs[a].at[idx(chip), c]
                copy(a, k, ins[a].at[c], landed, (*chip, c)).wait_recv()
                cp = copy(a, 3 + k, landed, landed, sibling)
                cp.start()
                started.append(cp)
        for k, chip in enumerate(chips):
            for a in range(n):
                copy(a, 3 + k, ins[a].at[c], outs[a].at[idx(chip), 1 - c], sibling).wait_recv()
        for cp in started:
            cp.wait_send()

    return pl.pallas_call(
        body, name="all_gather_xy", in_specs=[_ANY] * n, out_specs=[_ANY] * n,
        out_shape=[jax.ShapeDtypeStruct((4,) + a.shape, a.dtype) for a in shards],
        scratch_shapes=[pltpu.SemaphoreType.DMA((n, 6)), pltpu.SemaphoreType.DMA((n, 6))],
    )(*shards)


def _swap_layers(gs):
    n = len(gs)

    def body(*refs):
        ins, outs, (send_sems, recv_sems) = refs[:n], refs[n:2 * n], refs[2 * n:]
        x, y, c = _mesh_pos()
        copies = [_remote(ins[a].at[:, 1 - c], outs[a], send_sems.at[a], recv_sems.at[a], (x, y, 1 - c))
                  for a in range(n)]
        for cp in copies:
            cp.start()
        for cp in copies:
            cp.wait()

    return pl.pallas_call(
        body, name="rs_swap_layers", in_specs=[_ANY] * n, out_specs=[_ANY] * n,
        out_shape=[jax.ShapeDtypeStruct((4,) + g.shape[2:], g.dtype) for g in gs],
        scratch_shapes=[pltpu.SemaphoreType.DMA((n,)), pltpu.SemaphoreType.DMA((n,))],
    )(*gs)


def _scatter_chips(ts):
    n = len(ts)

    def body(*refs):
        ins, outs, (send_sems, recv_sems) = refs[:n], refs[n:2 * n], refs[2 * n:]
        x, y, c = _mesh_pos()
        copies = []
        for k, chip in enumerate(_other_chips(x, y)):
            for a in range(n):
                cp = _remote(ins[a].at[2 * chip[0] + chip[1]], outs[a].at[k], send_sems.at[a, k], recv_sems.at[a, k],
                             (*chip, c))
                cp.start()
                copies.append(cp)
        for cp in copies:
            cp.wait()

    return pl.pallas_call(
        body, name="rs_scatter_chips", in_specs=[_ANY] * n, out_specs=[_ANY] * n,
        out_shape=[jax.ShapeDtypeStruct((3,) + t.shape[1:], t.dtype) for t in ts],
        scratch_shapes=[pltpu.SemaphoreType.DMA((n, 3)), pltpu.SemaphoreType.DMA((n, 3))],
    )(*ts)


def _join_layers(us):
    n = len(us)

    def body(*refs):
        ins, outs, (send_sems, recv_sems) = refs[:n], refs[n:2 * n], refs[2 * n:]
        x, y, c = _mesh_pos()
        copies = [_remote(ins[a].at[c], outs[a].at[c], send_sems.at[a], recv_sems.at[a], (x, y, 1 - c))
                  for a in range(n)]
        for cp in copies:
            cp.start()
        for a, cp in enumerate(copies):
            cp.wait_send()
            _remote(ins[a].at[c], outs[a].at[1 - c], send_sems.at[a], recv_sems.at[a], (x, y, 1 - c)).wait_recv()

    return pl.pallas_call(
        body, name="rs_join_layers", in_specs=[_ANY] * n, out_specs=[_ANY] * n,
        out_shape=[jax.ShapeDtypeStruct(u.shape, u.dtype) for u in us],
        input_output_aliases={a: a for a in range(n)},
        scratch_shapes=[pltpu.SemaphoreType.DMA((n,)), pltpu.SemaphoreType.DMA((n,))],
    )(*us)


def _slab_rows(k, n, itemsize=4):
    tk = (1 << 20) // (n * itemsize) // 16 * 16
    while 0 < tk < k and k % tk:
        tk -= 16
    return tk if 0 < tk < k else k


def _add_layer(name, g, a, c_arr):
    _, _, k, n = g.shape
    tk = _slab_rows(k, n)

    def body(c_ref, g_ref, a_ref, o_ref):
        o_ref[...] = (g_ref[0] + a_ref[...]).astype(o_ref.dtype)

    return pl.pallas_call(
        body, name=name,
        grid_spec=pltpu.PrefetchScalarGridSpec(
            num_scalar_prefetch=1, grid=(4, k // tk),
            in_specs=[pl.BlockSpec((1, 1, tk, n), lambda j, i, c_ref: (j, c_ref[0], i, 0)),
                      pl.BlockSpec((1, tk, n), lambda j, i, c_ref: (j, i, 0))],
            out_specs=pl.BlockSpec((1, tk, n), lambda j, i, c_ref: (j, i, 0))),
        out_shape=jax.ShapeDtypeStruct((4, k, n), _RS_DT),
        compiler_params=_cparams(("parallel", "parallel")),
    )(c_arr, g, a)


def _add_chips(name, t, b, pos_arr):
    _, k, n = t.shape
    tk = _slab_rows(k, n)

    def body(pos_ref, t_ref, b_ref, o_ref):
        f = lambda v: v.astype(F32)
        o_ref[0] = ((f(t_ref[0]) + f(b_ref[0])) + f(b_ref[1])) + f(b_ref[2])

    return pl.pallas_call(
        body, name=name,
        grid_spec=pltpu.PrefetchScalarGridSpec(
            num_scalar_prefetch=1, grid=(k // tk,),
            in_specs=[pl.BlockSpec((1, tk, n), lambda i, pos_ref: (pos_ref[0], i, 0)),
                      pl.BlockSpec((3, tk, n), lambda i, pos_ref: (0, i, 0))],
            out_specs=pl.BlockSpec((1, tk, n), lambda i, pos_ref: (pos_ref[1], i, 0))),
        out_shape=jax.ShapeDtypeStruct((2, k, n), F32),
        compiler_params=_cparams(("parallel",)),
    )(pos_arr, t, b)


def _reduce_scatter(names, gs):
    x, y, c = _mesh_pos()
    c_arr = jnp.reshape(c, (1,)).astype(jnp.int32)
    pos_arr = jnp.stack([2 * x + y, c]).astype(jnp.int32)
    sib = _swap_layers(gs)
    ts = [_add_layer("rs_add_layer_" + n, g, a, c_arr) for n, g, a in zip(names, gs, sib)]
    bs = _scatter_chips(ts)
    us = [_add_chips("rs_add_chips_" + n, t, b, pos_arr) for n, t, b in zip(names, ts, bs)]
    return _join_layers(us)


def _all_reduce_adamw(gs, ws, ms, vs):
    n = len(gs)

    def body(*refs):
        g_refs, w_refs, m_refs, v_refs = (refs[i * n:(i + 1) * n] for i in range(4))
        gsum, delta, m_out, v_out = (refs[(4 + i) * n:(5 + i) * n] for i in range(4))
        slots = refs[8 * n:9 * n]
        send_sems, recv_sems = refs[9 * n:]
        x, y, c = _mesh_pos()
        me = 4 * x + 2 * y + c
        copies = []
        for rel in range(1, 8):
            bx, by, bc = (rel >> 2) & 1, (rel >> 1) & 1, rel & 1
            peer = (1 - x if bx else x, 1 - y if by else y, 1 - c if bc else c)
            for a in range(n):
                cp = _remote(g_refs[a], slots[a].at[me], send_sems.at[a, rel - 1], recv_sems.at[a, rel - 1], peer)
                cp.start()
                copies.append(cp)
        for a in range(n):
            slots[a][me] = g_refs[a][...]
        for cp in copies:
            cp.wait()
        for a in range(n):
            acc = slots[a][0]
            for d in range(1, 8):
                acc = acc + slots[a][d]
            gsum[a][...] = acc
            delta[a][...], m_out[a][...], v_out[a][...] = _adamw_math(w_refs[a][...], acc, m_refs[a][...],
                                                                      v_refs[a][...])

    vmem = pl.BlockSpec(memory_space=pltpu.VMEM)
    outs = pl.pallas_call(
        body, name="all_reduce_adamw", in_specs=[vmem] * (4 * n), out_specs=[vmem] * (4 * n),
        out_shape=[jax.ShapeDtypeStruct(g.shape, F32) for g in gs] * 4,
        scratch_shapes=([pltpu.VMEM((8,) + g.shape, F32) for g in gs]
                        + [pltpu.SemaphoreType.DMA((n, 7)), pltpu.SemaphoreType.DMA((n, 7))]),
        compiler_params=pltpu.CompilerParams(vmem_limit_bytes=_VMEM_LIMIT),
    )(*gs, *ws, *ms, *vs)
    return outs[:n], outs[n:2 * n], outs[2 * n:3 * n], outs[3 * n:]


def _swap_rope(a):
    h = QK_ROPE // 2
    return jnp.concatenate([a[..., h:], a[..., :h]], axis=-1)


_FFN_CB = 256


def _interleave(a, cb):
    r, c = a.shape
    return a.reshape(r, 2, c // (2 * cb), cb).transpose(0, 2, 1, 3).reshape(r, c)


def _deinterleave(a, cb):
    r, c = a.shape
    return a.reshape(r, c // (2 * cb), 2, cb).transpose(0, 2, 1, 3).reshape(r, c)


class _InLayout:
    def __init__(self, d):
        self.d = d
        self.gates = 0
        self.a = 3 * d
        self.b = 4 * d
        self.kv = 5 * d
        self.q = self.kv + 256
        self.krm = self.q + 384
        self.krs = self.krm + HEAD_PAD
        self.width = self.krs + 2 * HEAD_PAD


def _prep_layer(wl, d):
    lay = _InLayout(d)
    w_in = wl["w_in"]
    dt = w_in.dtype
    a, b = w_in[:, 0:d], w_in[:, d:2 * d]
    q, kv = w_in[:, 2 * d:2 * d + 384], w_in[:, 2 * d + 384:2 * d + 640]
    kr = w_in[:, 2 * d + 640:2 * d + 640 + QK_ROPE]
    gates = w_in[:, 2 * d + 640 + QK_ROPE:]
    z = lambda n: jnp.zeros((d, n), dt)
    krm = jnp.concatenate([z(QK_NOPE), kr, z(HEAD_PAD - QK_NOPE - QK_ROPE)], axis=1)
    krs = jnp.concatenate([z(QK_NOPE), _swap_rope(kr), z(HEAD_PAD - QK_NOPE - QK_ROPE)], axis=1)
    out = dict(wl)
    out["w_in"] = jnp.concatenate([gates, _interleave(a, _LANES), b, kv, q, krm, krs,
                                   z(lay.width - lay.krs - HEAD_PAD)], axis=1)
    uq = wl["mla_w_uq"].reshape(-1, N_HEADS, QK_NOPE + QK_ROPE)
    nq = uq.shape[0]
    nope, pe = uq[..., :QK_NOPE], uq[..., QK_NOPE:]
    zq = lambda n: jnp.zeros((nq, N_HEADS, n), dt)
    main = jnp.concatenate([nope, pe, zq(HEAD_PAD - QK_NOPE - QK_ROPE)], axis=-1).reshape(nq, -1)
    swapped = jnp.concatenate([zq(QK_NOPE), _swap_rope(pe), zq(HEAD_PAD - QK_NOPE - QK_ROPE)], axis=-1).reshape(nq, -1)
    out["mla_w_uq"] = jnp.concatenate([main, swapped], axis=1)
    ukv = wl["mla_w_ukv"].reshape(-1, N_HEADS, QK_NOPE + V_HEAD)
    nkv = ukv.shape[0]
    zk = jnp.zeros((nkv, N_HEADS, HEAD_PAD - QK_NOPE), dt)
    zv = jnp.zeros((nkv, N_HEADS, HEAD_PAD - V_HEAD), dt)
    out["mla_w_ukv"] = jnp.concatenate([jnp.concatenate([ukv[..., :QK_NOPE], zk], axis=-1).reshape(nkv, -1),
                                        jnp.concatenate([ukv[..., QK_NOPE:], zv], axis=-1).reshape(nkv, -1)], axis=1)
    wo = wl["mla_w_o"].reshape(N_HEADS, V_HEAD, -1)
    out["mla_w_o"] = jnp.concatenate([wo, jnp.zeros((N_HEADS, HEAD_PAD - V_HEAD, wo.shape[-1]), dt)],
                                     axis=1).reshape(N_HEADS * HEAD_PAD, -1)
    return out


def _unprep_grads(g, d):
    lay = _InLayout(d)
    gi = g["w_in"]
    kr = (gi[:, lay.krm + QK_NOPE:lay.krm + QK_NOPE + QK_ROPE]
          + _swap_rope(gi[:, lay.krs + QK_NOPE:lay.krs + QK_NOPE + QK_ROPE]))
    out = dict(g)
    out["w_in"] = jnp.concatenate([_deinterleave(gi[:, lay.a:lay.a + d], _LANES), gi[:, lay.b:lay.b + d],
                                   gi[:, lay.q:lay.q + 384], gi[:, lay.kv:lay.kv + 256], kr,
                                   gi[:, lay.gates:lay.gates + 3 * d]], axis=1)
    hw = N_HEADS * HEAD_PAD
    gq = g["mla_w_uq"]
    nq = gq.shape[0]
    main = gq[:, :hw].reshape(nq, N_HEADS, HEAD_PAD)
    swapped = gq[:, hw:].reshape(nq, N_HEADS, HEAD_PAD)
    pe = main[..., QK_NOPE:QK_NOPE + QK_ROPE] + _swap_rope(swapped[..., QK_NOPE:QK_NOPE + QK_ROPE])
    out["mla_w_uq"] = jnp.concatenate([main[..., :QK_NOPE], pe], axis=-1).reshape(nq, -1)
    gkv = g["mla_w_ukv"]
    nkv = gkv.shape[0]
    out["mla_w_ukv"] = jnp.concatenate([gkv[:, :hw].reshape(nkv, N_HEADS, HEAD_PAD)[..., :QK_NOPE],
                                        gkv[:, hw:].reshape(nkv, N_HEADS, HEAD_PAD)[..., :V_HEAD]],
                                       axis=-1).reshape(nkv, -1)
    go = g["mla_w_o"]
    out["mla_w_o"] = go.reshape(N_HEADS, HEAD_PAD, -1)[:, :V_HEAD].reshape(N_HEADS * V_HEAD, -1)
    return out


def _rope_tables(positions):
    s = positions.shape[0]
    inv = ROPE_THETA ** (-jnp.arange(0, QK_ROPE, 2, dtype=F32) / QK_ROPE)
    ang = positions.astype(F32)[:, None] * inv
    cos, sin = jnp.cos(ang), jnp.sin(ang)
    tail = jnp.zeros((s, HEAD_PAD - QK_NOPE - QK_ROPE), F32)
    tc = jnp.concatenate([jnp.ones((s, QK_NOPE), F32), cos, cos, tail], axis=1)
    ts = jnp.concatenate([jnp.zeros((s, QK_NOPE), F32), -sin, sin, tail], axis=1)
    return tc, ts


def _row(v):
    return v.reshape(1, -1)


def _layer_fwd(x, h, w, g_next, tc, ts):
    d = x.shape[1]
    lay = _InLayout(d)
    cw = d // 2
    blk = lambda off, width: off // width
    p = _mm("mm_in", h, w["w_in"])
    z1 = _glu_conv_fwd(p, blk(lay.a, 2 * _LANES), w["conv_dw_w"], _row(w["conv_dw_b"]))
    ln_a = [_row(w["conv_ln_g"]), _row(w["conv_ln_b"])]
    (z3,) = _row_fwd("ln_silu_fwd", _f_ln_silu, [(z1, cw, 0)], ln_a, [(cw, _MXU_DT)])
    ya = _mm("mm_conv_out", z3, w["conv_out_w"])
    ln_b = [_row(w["sg_ln_g"]), _row(w["sg_ln_b"])]
    u, vn = _row_fwd("sg_pre_fwd", _f_sg_pre, [(p, cw, blk(lay.b, cw)), (p, cw, blk(lay.b + cw, cw))], ln_b,
                     [(cw, F32), (cw, _MXU_DT)])
    bcol = w["sg_b"].reshape(SG_GROUPS, SG_CHUNK, 1)
    ub = _sg_mix_fwd(u, vn, w["sg_w"], bcol)
    yb = _mm("mm_sg_out", ub, w["sg_out_w"])
    (qn,) = _row_fwd("q_norm_fwd", _f_rms, [(p, 384, blk(lay.q, 384))], [_row(w["mla_q_norm_g"])], [(384, _MXU_DT)])
    (kvn,) = _row_fwd("kv_norm_fwd", _f_rms, [(p, 256, blk(lay.kv, 256))], [_row(w["mla_kv_norm_g"])],
                      [(256, _MXU_DT)])
    q2 = _mm("mm_uq", qn, w["mla_w_uq"])
    kv2 = _mm("mm_ukv", kvn, w["mla_w_ukv"])
    qf, kf, vf = _rope_fwd(q2, kv2, p, blk(lay.krm, HEAD_PAD), blk(lay.krs, HEAD_PAD), tc, ts)
    o = _attn_fwd(qf, kf, vf)
    yc = _mm("mm_o", o, w["mla_w_o"])
    gate_rows = [(p, d, 0), (p, d, 1), (p, d, 2)]
    (merged,) = _row_fwd("merge_fwd", _f_merge, gate_rows + [(ya, d, 0), (yb, d, 0), (yc, d, 0)], [], [(d, _MXU_DT)])
    t = _mm("mm_out", merged, w["w_out"])
    x1, h2 = _row_fwd("resid_mix_fwd", _f_resid_rms_rms, [(x, d, 0), (t, d, 0)],
                      [_row(w["mix_post_g"]), _row(w["ffn_pre_g"])], [(d, F32), (d, _MXU_DT)])
    up = _mm("mm_up", h2, w["ffn_w_up"])
    act = _conv_geglu_fwd(up, w["ffn_dw_w"], _row(w["ffn_dw_b"]))
    dn = _mm("mm_down", act, w["ffn_w_down"])
    if g_next is None:
        (x2,) = _row_fwd("resid_ffn_last_fwd", _f_resid_rms, [(x1, d, 0), (dn, d, 0)], [_row(w["ffn_post_g"])],
                         [(d, F32)])
        h_next = None
    else:
        x2, h_next = _row_fwd("resid_ffn_fwd", _f_resid_rms_rms, [(x1, d, 0), (dn, d, 0)],
                              [_row(w["ffn_post_g"]), _row(g_next)], [(d, F32), (d, _MXU_DT)])
    saved = dict(x=x, h=h, p=p, z1=z1, z3=z3, ya=ya, u=u, vn=vn, ub=ub, yb=yb, qn=qn, kvn=kvn, qf=qf, kf=kf, vf=vf, o=o,
                 yc=yc, merged=merged, t=t, x1=x1, h2=h2, up=up, act=act, dn=dn, bcol=bcol)
    return x2, h_next, saved


def _layer_bwd(dx2, dh_next, w, g_next, sv, tc, ts):
    d = dx2.shape[1]
    lay = _InLayout(d)
    cw = d // 2
    blk = lambda off, width: off // width
    lo = _MXU_DT
    g = {}
    x1, dn = sv["x1"], sv["dn"]
    if dh_next is None:
        dx1, ddn, g["ffn_post_g"] = _row_bwd("resid_ffn_last_bwd", _f_resid_rms, [(x1, d, 0), (dn, d, 0)],
                                             [_row(w["ffn_post_g"])], [(dx2, d, 0)], [(0, F32), (1, lo)], [0])
    else:
        dx1, ddn, g["ffn_post_g"], g["next_pre_g"] = _row_bwd(
            "resid_ffn_bwd", _f_resid_rms_rms, [(x1, d, 0), (dn, d, 0)], [_row(w["ffn_post_g"]), _row(g_next)],
            [(dx2, d, 0), (dh_next, d, 0)], [(0, F32), (1, lo)], [0, 1])
    dact = _mm("mm_down_dx", ddn, w["ffn_w_down"], tb=True)
    g["ffn_w_down"] = _mm("mm_down_dw", sv["act"], ddn, ta=True)
    dup, dw_halves, db_halves = _conv_geglu_bwd(sv["up"], w["ffn_dw_w"], _row(w["ffn_dw_b"]), dact)
    g["ffn_dw_w"] = jnp.concatenate([dw_halves[0], dw_halves[1]], axis=1)
    g["ffn_dw_b"] = jnp.concatenate([db_halves[0], db_halves[1]], axis=1)
    dh2 = _mm("mm_up_dx", dup, w["ffn_w_up"], tb=True, a_halves=True)
    g["ffn_w_up"] = _mm("mm_up_dw", sv["h2"], dup, ta=True, b_halves=True, out_quarters=True)
    dx, dt, g["mix_post_g"], g["ffn_pre_g"] = _row_bwd(
        "resid_mix_bwd", _f_resid_rms_rms, [(sv["x"], d, 0), (sv["t"], d, 0)],
        [_row(w["mix_post_g"]), _row(w["ffn_pre_g"])], [(dx1, d, 0), (dh2, d, 0)], [(0, F32), (1, lo)], [0, 1])
    dmerged = _mm("mm_out_dx", dt, w["w_out"], tb=True)
    g["w_out"] = _mm("mm_out_dw", sv["merged"], dt, ta=True)
    p = sv["p"]
    gate_rows = [(p, d, 0), (p, d, 1), (p, d, 2)]
    dp, dya, dyb, dyc = _row_bwd(
        "merge_bwd", _f_merge, gate_rows + [(sv["ya"], d, 0), (sv["yb"], d, 0), (sv["yc"], d, 0)], [],
        [(dmerged, d, 0)], [((0, 1, 2), lo), ((3,), lo), ((4,), lo), ((5,), lo)], [], place=(lay.width, 0))
    do = _mm("mm_o_dx", dyc, w["mla_w_o"], tb=True, out_dtype=lo)
    g["mla_w_o"] = _mm("mm_o_dw", sv["o"], dyc, ta=True)
    dqf, dkf, dvf = _attn_bwd(sv["qf"], sv["kf"], sv["vf"], do)
    dq2, dkv2, dp = _rope_bwd(dqf, dkf, dvf, tc, ts, dp, blk(lay.krm, 3 * HEAD_PAD))
    dkvn = _mm("mm_ukv_dx", dkv2, w["mla_w_ukv"], tb=True)
    g["mla_w_ukv"] = _mm("mm_ukv_dw", sv["kvn"], dkv2, ta=True)
    dqn = _mm("mm_uq_dx", dq2, w["mla_w_uq"], tb=True)
    g["mla_w_uq"] = _mm("mm_uq_dw", sv["qn"], dq2, ta=True)
    dp, g["mla_q_norm_g"] = _row_bwd("q_norm_bwd", _f_rms, [(p, 384, blk(lay.q, 384))], [_row(w["mla_q_norm_g"])],
                                     [(dqn, 384, 0)], [((0,), lo)], [0], place=(lay.width, blk(lay.q, 384)), into=dp)
    dp, g["mla_kv_norm_g"] = _row_bwd("kv_norm_bwd", _f_rms, [(p, 256, blk(lay.kv, 256))],
                                      [_row(w["mla_kv_norm_g"])], [(dkvn, 256, 0)], [((0,), lo)], [0],
                                      place=(lay.width, blk(lay.kv, 256)), into=dp)
    dub = _mm("mm_sg_out_dx", dyb, w["sg_out_w"], tb=True)
    g["sg_out_w"] = _mm("mm_sg_out_dw", sv["ub"], dyb, ta=True)
    du, dvn, g["sg_w"], dbcol = _sg_mix_bwd(sv["u"], sv["vn"], w["sg_w"], sv["bcol"], dub)
    g["sg_b"] = dbcol.reshape(SG_GROUPS, SG_CHUNK)
    dp, g["sg_ln_g"], g["sg_ln_b"] = _row_bwd(
        "sg_pre_bwd", _f_sg_pre, [(p, cw, blk(lay.b, cw)), (p, cw, blk(lay.b + cw, cw))],
        [_row(w["sg_ln_g"]), _row(w["sg_ln_b"])], [(du, cw, 0), (dvn, cw, 0)], [((0, 1), lo)], [0, 1],
        place=(lay.width, blk(lay.b, d)), into=dp)
    dz3 = _mm("mm_conv_out_dx", dya, w["conv_out_w"], tb=True)
    g["conv_out_w"] = _mm("mm_conv_out_dw", sv["z3"], dya, ta=True)
    dz1, g["conv_ln_g"], g["conv_ln_b"] = _row_bwd(
        "ln_silu_bwd", _f_ln_silu, [(sv["z1"], cw, 0)], [_row(w["conv_ln_g"]), _row(w["conv_ln_b"])],
        [(dz3, cw, 0)], [((0,), F32)], [0, 1])
    dp, g["conv_dw_w"], g["conv_dw_b"] = _glu_conv_bwd(p, blk(lay.a, 2 * _LANES), w["conv_dw_w"], dz1, dp)
    dh = _mm("mm_in_dx", dp, w["w_in"], tb=True)
    g["w_in"] = _mm("mm_in_dw", sv["h"], dp, ta=True)
    return dx, dh, g


def _local_step(x, positions, target, layers):
    d = x.shape[1]
    tc, ts = _rope_tables(positions)
    ws = [_prep_layer(wl, d) for wl in layers]
    depth = len(ws)
    (h,) = _row_fwd("rms_first_fwd", _f_rms, [(x, d, 0)], [_row(ws[0]["mix_pre_g"])], [(d, _MXU_DT)])
    saved = []
    for l in range(depth):
        g_next = ws[l + 1]["mix_pre_g"] if l + 1 < depth else None
        x, h, sv = _layer_fwd(x, h, ws[l], g_next, tc, ts)
        saved.append(sv)
    loss, dx = _loss_head(x, target)
    grads = [None] * depth
    dh = None
    for l in reversed(range(depth)):
        g_next = ws[l + 1]["mix_pre_g"] if l + 1 < depth else None
        dx, dh, g = _layer_bwd(dx, dh, ws[l], g_next, saved[l], tc, ts)
        if "next_pre_g" in g:
            grads[l + 1]["mix_pre_g"] = g.pop("next_pre_g")
        grads[l] = g
    x0 = saved[0]["x"]
    grad_x, grads[0]["mix_pre_g"] = _row_bwd("rms_first_bwd", _f_x_rms, [(x0, d, 0)], [_row(ws[0]["mix_pre_g"])],
                                             [(dx, d, 0), (dh, d, 0)], [(0, F32)], [0])
    return loss, grad_x, [_unprep_grads(g, d) for g in grads]


_MATRICES = ("w_in", "conv_out_w", "sg_out_w", "mla_w_uq", "mla_w_ukv", "mla_w_o", "w_out", "ffn_w_up", "ffn_w_down")
_F32_GATHERED = ("conv_dw_w", "ffn_dw_w")
_RS_DT = jnp.bfloat16


def _gather_weights(wshard, my_chip):
    names = list(SHARDED)
    hi = {n: wshard[n].astype(jnp.bfloat16) for n in names}
    lo = {n: (wshard[n] - hi[n].astype(F32)).astype(jnp.bfloat16) for n in _F32_GATHERED}
    arrays = [hi[n] for n in names] + [lo[n] for n in _F32_GATHERED]
    gathered = _all_gather_xy(arrays)
    out = {}
    for i, n in enumerate(names + list(_F32_GATHERED)):
        blocks = [jnp.where(my_chip == j, arrays[i], gathered[i][j]) for j in range(4)]
        whole = jnp.concatenate(blocks, axis=1 if n in SHARDED_MID else 2)
        out[n] = out[n].astype(F32) + whole.astype(F32) if n in out else whole
    return out


def _by_destination(name, per_layer):
    out = []
    for gl in per_layer:
        if gl.ndim == 3:
            out.append(gl)
            continue
        k, n = gl.shape
        if name in SHARDED_MID:
            out.append(gl.reshape(4, k // 4, n))
        else:
            out.append(gl.reshape(k, 4, n // 4).transpose(1, 0, 2))
    return jnp.stack(out, axis=1)


def kernel(x, positions, mix_pre_g, mix_post_g, ffn_pre_g, ffn_post_g, w_in, conv_dw_w, conv_dw_b, conv_ln_g, conv_ln_b, conv_out_w, sg_ln_g, sg_ln_b, sg_w, sg_b, sg_out_w, mla_q_norm_g, mla_w_uq, mla_kv_norm_g, mla_w_ukv, mla_w_o, w_out, ffn_w_up, ffn_dw_w, ffn_dw_b, ffn_w_down, loss_target, m_mix_pre_g, m_mix_post_g, m_ffn_pre_g, m_ffn_post_g, m_w_in, m_conv_dw_w, m_conv_dw_b, m_conv_ln_g, m_conv_ln_b, m_conv_out_w, m_sg_ln_g, m_sg_ln_b, m_sg_w, m_sg_b, m_sg_out_w, m_mla_q_norm_g, m_mla_w_uq, m_mla_kv_norm_g, m_mla_w_ukv, m_mla_w_o, m_w_out, m_ffn_w_up, m_ffn_dw_w, m_ffn_dw_b, m_ffn_w_down, v_mix_pre_g, v_mix_post_g, v_ffn_pre_g, v_ffn_post_g, v_w_in, v_conv_dw_w, v_conv_dw_b, v_conv_ln_g, v_conv_ln_b, v_conv_out_w, v_sg_ln_g, v_sg_ln_b, v_sg_w, v_sg_b, v_sg_out_w, v_mla_q_norm_g, v_mla_w_uq, v_mla_kv_norm_g, v_mla_w_ukv, v_mla_w_o, v_w_out, v_ffn_w_up, v_ffn_dw_w, v_ffn_dw_b, v_ffn_w_down):
    args = dict(locals())
    w = {n: args[n] for n in WEIGHTS}
    m = {n: args["m_" + n] for n in WEIGHTS}
    v = {n: args["v_" + n] for n in WEIGHTS}
    depth = mix_pre_g.shape[0]

    assert depth == 2, "the two cores of a chip split the communication by layer"
    mesh_x, mesh_y, _ = _mesh_pos()
    full = _gather_weights({n: w[n] for n in SHARDED}, 2 * mesh_x + mesh_y)
    layers = []
    for l in range(depth):
        wl = {n: w[n][l] for n in REPLICATED}
        for n in SHARDED:
            wl[n] = full[n][l].astype(_MXU_DT if n in _MATRICES else F32)
        layers.append(wl)

    loss, grad_x, grads = _local_step(x[0], positions[0], loss_target[0], layers)
    loss = lax.psum(loss[0, 0], ("x", "y", "c"))

    out = {}
    names = list(SHARDED)
    reduced = _reduce_scatter(names, [_by_destination(n, [grads[l][n] for l in range(depth)]) for n in names])
    for n, gr in zip(names, reduced):
        out[n] = (gr, *_adamw("adamw_" + n, w[n], gr, m[n], v[n]))
    rep = list(REPLICATED)
    g_rep = [jnp.stack([grads[l][n].reshape(w[n].shape[1:]) for l in range(depth)]) for n in rep]
    for n, *res in zip(rep, *_all_reduce_adamw(g_rep, [w[n] for n in rep], [m[n] for n in rep], [v[n] for n in rep])):
        out[n] = tuple(res)
    return (loss, grad_x[None], *[out[n][i] for i in range(4) for n in WEIGHTS])
```

```python
import functools
import math

import jax
import jax.numpy as jnp
from jax import lax
from jax.experimental import pallas as pl
from jax.experimental.pallas import tpu as pltpu

F32 = jnp.float32
_MXU_DT = jnp.bfloat16
_VMEM_LIMIT = 48 * 1024 * 1024
_LANES = 128
_MESH = pl.DeviceIdType.MESH

N_HEADS = 8
QK_NOPE = 64
QK_ROPE = 32
V_HEAD = 64
HEAD_PAD = 128
SG_GROUPS = 4
SG_CHUNK = 128
CONV_K = 31
FFN_K = 3
ROPE_THETA = 10000.0
EPS = 1e-6
ADAM_LR, ADAM_B1, ADAM_B2, ADAM_EPS, ADAM_WD, ADAM_STEP = 0.001, 0.9, 0.999, 1e-08, 0.01, 10

SHARDED_LAST = ("w_in", "conv_dw_w", "conv_out_w", "sg_out_w", "mla_w_uq", "mla_w_ukv", "mla_w_o", "ffn_w_up",
                "ffn_dw_w")
SHARDED_MID = ("w_out", "ffn_w_down")
SHARDED = SHARDED_LAST + SHARDED_MID
WEIGHTS = ("mix_pre_g", "mix_post_g", "ffn_pre_g", "ffn_post_g", "w_in", "conv_dw_w", "conv_dw_b", "conv_ln_g",
           "conv_ln_b", "conv_out_w", "sg_ln_g", "sg_ln_b", "sg_w", "sg_b", "sg_out_w", "mla_q_norm_g", "mla_w_uq",
           "mla_kv_norm_g", "mla_w_ukv", "mla_w_o", "w_out", "ffn_w_up", "ffn_dw_w", "ffn_dw_b", "ffn_w_down")
REPLICATED = tuple(n for n in WEIGHTS if n not in SHARDED)


def _cparams(sem=None):
    return pltpu.CompilerParams(dimension_semantics=sem, vmem_limit_bytes=_VMEM_LIMIT)


def _pick(n, cands):
    for c in cands:
        if n % c == 0:
            return c
    return n


def _largest_tile(dim, cap):
    for t in range(min(cap, dim) // _LANES * _LANES, 0, -_LANES):
        if dim % t == 0:
            return t
    return dim


_MM_VMEM_BUDGET = 36 * 1024 * 1024
_MM_TM_CAP, _MM_TN_CAP, _MM_TK_CAP = 1024, 1536, 3072


def _mm(name, a, b, *, ta=False, tb=False, out_dtype=F32, a_halves=False, b_halves=False, out_quarters=False):
    assert not (a_halves and ta) and not (b_halves and tb)
    if a_halves:
        m, kdim = a.shape[1], 2 * a.shape[2]
    else:
        (kdim, m) = a.shape if ta else a.shape[::-1]
    if b_halves:
        kdim2, n = b.shape[1], 2 * b.shape[2]
    else:
        (n, kdim2) = b.shape if tb else b.shape[::-1]
    assert kdim == kdim2, (a.shape, b.shape, ta, tb)
    tk = _largest_tile(kdim // 2 if a_halves else kdim, _MM_TK_CAP)
    tn = _largest_tile(n // 4 if out_quarters else (n // 2 if b_halves else n), _MM_TN_CAP)
    nk = kdim // tk
    ab, bb, ob = a.dtype.itemsize, b.dtype.itemsize, jnp.dtype(out_dtype).itemsize
    tm = _largest_tile(m, _MM_TM_CAP)
    vmem = lambda t: 2 * (t * tk * ab + tk * tn * bb + t * tn * ob) + (t * tn * 4 if nk > 1 else 0)
    while vmem(tm) > _MM_VMEM_BUDGET and tm > _LANES:
        tm = _largest_tile(m, tm - _LANES)
    dims = (((0 if ta else 1,), (1 if tb else 0,)), ((), ()))

    def dot(a_ref, b_ref):
        return lax.dot_general(a_ref[...].astype(_MXU_DT), b_ref[...].astype(_MXU_DT), dims,
                               preferred_element_type=F32)

    def body_one(a_ref, b_ref, o_ref):
        o_ref[...] = dot(a_ref, b_ref).astype(o_ref.dtype)

    def body_acc(a_ref, b_ref, o_ref, acc_ref):
        k = pl.program_id(2)

        @pl.when(k == 0)
        def _():
            acc_ref[...] = jnp.zeros_like(acc_ref)

        acc_ref[...] += dot(a_ref, b_ref)

        @pl.when(k == nk - 1)
        def _():
            o_ref[...] = acc_ref[...].astype(o_ref.dtype)

    if a_halves:
        per = nk // 2
        a_spec = pl.BlockSpec((None, tm, tk), lambda i, j, k: (k // per, i, k % per))
    elif ta:
        a_spec = pl.BlockSpec((tk, tm), lambda i, j, k: (k, i))
    else:
        a_spec = pl.BlockSpec((tm, tk), lambda i, j, k: (i, k))
    if b_halves:
        per_b = n // 2 // tn
        b_spec = pl.BlockSpec((None, tk, tn), lambda i, j, k: (j // per_b, k, j % per_b))
    elif tb:
        b_spec = pl.BlockSpec((tn, tk), lambda i, j, k: (j, k))
    else:
        b_spec = pl.BlockSpec((tk, tn), lambda i, j, k: (k, j))
    if out_quarters:
        per_o = n // 4 // tn
        o_spec = pl.BlockSpec((None, tm, tn), lambda i, j, k: (j // per_o, i, j % per_o))
        o_shape = jax.ShapeDtypeStruct((4, m, n // 4), out_dtype)
    else:
        o_spec = pl.BlockSpec((tm, tn), lambda i, j, k: (i, j))
        o_shape = jax.ShapeDtypeStruct((m, n), out_dtype)
    return pl.pallas_call(
        body_one if nk == 1 else body_acc, name=name, grid=(m // tm, n // tn, nk),
        in_specs=[a_spec, b_spec], out_specs=o_spec, out_shape=o_shape,
        scratch_shapes=[] if nk == 1 else [pltpu.VMEM((tm, tn), F32)],
        compiler_params=_cparams(("parallel", "parallel", "arbitrary")),
    )(a, b)


def _row_spec(tm, width, idx):
    return pl.BlockSpec((tm, width), lambda i: (i, idx))


def _full_spec(shape):
    zeros = (0,) * len(shape)
    return pl.BlockSpec(shape, lambda i: zeros)


def _row_fwd(name, fn, rows, params, outs, tm=256):
    s = rows[0][0].shape[0]
    nr, npar = len(rows), len(params)

    def body(*refs):
        vals = [r[...].astype(F32) for r in refs[:nr + npar]]
        res = fn(*vals)
        for o_ref, r in zip(refs[nr + npar:], res):
            o_ref[...] = r.astype(o_ref.dtype)

    return pl.pallas_call(
        body, name=name, grid=(s // tm,),
        in_specs=[_row_spec(tm, w, i) for _, w, i in rows] + [_full_spec(p.shape) for p in params],
        out_specs=[_row_spec(tm, w, 0) for w, _ in outs],
        out_shape=[jax.ShapeDtypeStruct((s, w), dt) for w, dt in outs],
        compiler_params=_cparams(("parallel",)),
    )(*[r[0] for r in rows], *params)


def _row_bwd(name, fn, rows, params, cots, row_grads, param_grads, tm=256, place=None, into=None):
    s = rows[0][0].shape[0]
    nr, npar, nc = len(rows), len(params), len(cots)
    row_grads = [((idxs,) if isinstance(idxs, int) else tuple(idxs), dt) for idxs, dt in row_grads]
    widths = [sum(rows[i][1] for i in idxs) for idxs, _ in row_grads]

    def body(*refs):
        i = pl.program_id(0)
        vals = [r[...].astype(F32) for r in refs[:nr + npar]]
        cvals = tuple(r[...].astype(F32) for r in refs[nr + npar:nr + npar + nc])
        _, vjp = jax.vjp(fn, *vals)
        grads = vjp(cvals)
        outs = refs[nr + npar + nc + (into is not None):]
        for o_ref, (idxs, _) in zip(outs, row_grads):
            pos = 0
            for idx in idxs:
                o_ref[:, pos:pos + rows[idx][1]] = grads[idx].astype(o_ref.dtype)
                pos += rows[idx][1]
        for o_ref, idx in zip(outs[len(row_grads):], param_grads):
            @pl.when(i == 0)
            def _(o_ref=o_ref):
                o_ref[...] = jnp.zeros_like(o_ref)

            o_ref[...] += grads[nr + idx]

    out_specs = [_row_spec(tm, w, 0) for w in widths] + [_full_spec(params[idx].shape) for idx in param_grads]
    out_shape = ([jax.ShapeDtypeStruct((s, w), dt) for w, (_, dt) in zip(widths, row_grads)]
                 + [jax.ShapeDtypeStruct(params[idx].shape, F32) for idx in param_grads])
    extra, aliases = [], {}
    if place is not None:
        out_specs[0] = _row_spec(tm, widths[0], place[1])
        out_shape[0] = jax.ShapeDtypeStruct((s, place[0]), row_grads[0][1])
    if into is not None:
        extra, aliases = [into], {nr + npar + nc: 0}
    return pl.pallas_call(
        body, name=name, grid=(s // tm,),
        in_specs=([_row_spec(tm, w, i) for _, w, i in rows] + [_full_spec(p.shape) for p in params]
                  + [_row_spec(tm, w, i) for _, w, i in cots] + [_ANY] * len(extra)),
        out_specs=out_specs, out_shape=out_shape, input_output_aliases=aliases,
        compiler_params=_cparams(("arbitrary",)),
    )(*[r[0] for r in rows], *params, *[c[0] for c in cots], *extra)


def _rms(x, g):
    return x * lax.rsqrt(jnp.mean(x * x, axis=-1, keepdims=True) + EPS) * g


def _ln(x, g, b):
    mu = jnp.mean(x, axis=-1, keepdims=True)
    xc = x - mu
    var = jnp.mean(xc * xc, axis=-1, keepdims=True)
    return xc * lax.rsqrt(var + EPS) * g + b


def _sigmoid(x):
    return 1.0 / (1.0 + jnp.exp(-x))


def _gelu(x):
    return x * (0.5 * (1.0 + jnp.tanh(math.sqrt(2.0 / math.pi) * (x + 0.044715 * (x * x * x)))))


def _f_rms(x, g):
    return (_rms(x, g),)


def _f_x_rms(x, g):
    return (x, _rms(x, g))


def _f_ln_silu(z, g, b):
    y = _ln(z, g, b)
    return (y * _sigmoid(y),)


def _f_sg_pre(bu, bv, g, b):
    return (_gelu(bu), _ln(_gelu(bv), g, b))


def _f_merge(g0, g1, g2, ya, yb, yc):
    return (_sigmoid(g0) * ya + _sigmoid(g1) * yb + _sigmoid(g2) * yc,)


def _f_resid_rms(x, t, g_post):
    return (x + _rms(t, g_post),)


def _f_resid_rms_rms(x, t, g_post, g_next):
    x1 = x + _rms(t, g_post)
    return (x1, _rms(x1, g_next))


def _f_geglu(zg, zv):
    return _gelu(zg) * zv


_CONV_TILE_ELEMS = 16 * 1024


def _conv_tr(c):
    return _CONV_TILE_ELEMS // c


def _conv_tile(zp_ref, w_ref, bias, k_taps, off, r0):
    c = zp_ref.shape[1]
    tr = _conv_tr(c)
    acc = jnp.broadcast_to(bias, (tr, c))
    for k in range(k_taps):
        acc = acc + w_ref[k:k + 1, :] * zp_ref[r0 + off + k:r0 + off + k + tr, :]
    return acc


def _conv_bwd_input_tile(dzp_ref, w_ref, k_taps, r0):
    c = dzp_ref.shape[1]
    tr = _conv_tr(c)
    acc = jnp.zeros((tr, c), F32)
    for k in range(k_taps):
        s0 = r0 + (k_taps - 1) - k
        acc = acc + w_ref[k:k + 1, :] * dzp_ref[s0:s0 + tr, :]
    return acc


def _conv_bwd_weight(dzp_ref, zp_ref, dw_ref, db_ref, k_taps, off, s):
    c = zp_ref.shape[1]
    tr = _conv_tr(c)
    fold = lambda v: jnp.sum(v.reshape(tr // 8, 8, c), axis=0)
    for k in range(k_taps):
        acc = jnp.zeros((8, c), F32)
        for r in range(s // tr):
            r0 = r * tr
            acc = acc + fold(dzp_ref[r0:r0 + tr, :] * zp_ref[r0 + off + k:r0 + off + k + tr, :])
        dw_ref[k:k + 1, :] = jnp.sum(acc, axis=0, keepdims=True)
    acc = jnp.zeros((8, c), F32)
    for r in range(s // tr):
        acc = acc + fold(dzp_ref[r * tr:(r + 1) * tr, :])
    db_ref[...] = jnp.sum(acc, axis=0, keepdims=True)


def _glu_conv_fwd(p, blk0, w, b):
    s = p.shape[0]
    k_taps, c = w.shape
    cb, pad = _LANES, 32
    off = pad - (k_taps - 1)

    def body(a_ref, w_ref, b_ref, o_ref, zp_ref):
        zp_ref[0:pad, :] = jnp.zeros((pad, cb), F32)
        zp_ref[pad:pad + s, :] = a_ref[:, 0:cb] * _sigmoid(a_ref[:, cb:2 * cb])
        tr = _conv_tr(cb)
        for r in range(s // tr):
            o_ref[r * tr:(r + 1) * tr, :] = _conv_tile(zp_ref, w_ref, b_ref[...], k_taps, off, r * tr)

    return pl.pallas_call(
        body, name="glu_conv_fwd", grid=(c // cb,),
        in_specs=[pl.BlockSpec((s, 2 * cb), lambda j: (0, blk0 + j)),
                  pl.BlockSpec((k_taps, cb), lambda j: (0, j)), pl.BlockSpec((1, cb), lambda j: (0, j))],
        out_specs=pl.BlockSpec((s, cb), lambda j: (0, j)),
        out_shape=jax.ShapeDtypeStruct((s, c), F32),
        scratch_shapes=[pltpu.VMEM((s + pad, cb), F32)],
        compiler_params=_cparams(("parallel",)),
    )(p, w, b)


def _glu_conv_bwd(p, blk0, w, dz, dp):
    s = p.shape[0]
    k_taps, c = w.shape
    cb, pad = _LANES, 32
    off = pad - (k_taps - 1)

    def body(a_ref, w_ref, dz_ref, dp_in, da_ref, dw_ref, db_ref, zp_ref, dzp_ref):
        zp_ref[0:pad, :] = jnp.zeros((pad, cb), F32)
        zp_ref[pad:pad + s, :] = a_ref[:, 0:cb] * _sigmoid(a_ref[:, cb:2 * cb])
        dzp_ref[0:s, :] = dz_ref[...]
        dzp_ref[s:s + pad, :] = jnp.zeros((pad, cb), F32)
        tr = _conv_tr(cb)
        for r in range(s // tr):
            rows = slice(r * tr, (r + 1) * tr)
            dz0 = _conv_bwd_input_tile(dzp_ref, w_ref, k_taps, r * tr)
            sg = _sigmoid(a_ref[rows, cb:2 * cb])
            da_ref[rows, 0:cb] = (dz0 * sg).astype(da_ref.dtype)
            da_ref[rows, cb:2 * cb] = (dz0 * a_ref[rows, 0:cb] * sg * (1.0 - sg)).astype(da_ref.dtype)
        _conv_bwd_weight(dzp_ref, zp_ref, dw_ref, db_ref, k_taps, off, s)

    return pl.pallas_call(
        body, name="glu_conv_bwd", grid=(c // cb,),
        in_specs=[pl.BlockSpec((s, 2 * cb), lambda j: (0, blk0 + j)),
                  pl.BlockSpec((k_taps, cb), lambda j: (0, j)), pl.BlockSpec((s, cb), lambda j: (0, j)), _ANY],
        out_specs=[pl.BlockSpec((s, 2 * cb), lambda j: (0, blk0 + j)),
                   pl.BlockSpec((k_taps, cb), lambda j: (0, j)), pl.BlockSpec((1, cb), lambda j: (0, j))],
        out_shape=[jax.ShapeDtypeStruct(dp.shape, dp.dtype),
                   jax.ShapeDtypeStruct((k_taps, c), F32), jax.ShapeDtypeStruct((1, c), F32)],
        scratch_shapes=[pltpu.VMEM((s + pad, cb), F32), pltpu.VMEM((s + pad, cb), F32)],
        input_output_aliases={3: 0},
        compiler_params=_cparams(("parallel",)),
    )(p, w, dz, dp)


def _conv_geglu_fwd(up, w, b):
    s, f2 = up.shape
    f = f2 // 2
    k_taps = w.shape[0]
    cb, pad = _FFN_CB, 8
    off = pad - (k_taps - 1)
    nb = f // cb

    def body(ug_ref, uv_ref, wg_ref, wv_ref, bg_ref, bv_ref, o_ref, z_ref, w_ref, b_ref):
        _pair(w_ref, wg_ref[...], wv_ref[...], cb)
        _pair(b_ref, bg_ref[...], bv_ref[...], cb)
        z_ref[0:pad, :] = jnp.zeros((pad, 2 * cb), F32)
        z_ref[pad:pad + s, 0:cb] = ug_ref[...]
        z_ref[pad:pad + s, cb:2 * cb] = uv_ref[...]
        tr = _conv_tr(2 * cb)
        for r in range(s // tr):
            z = _conv_tile(z_ref, w_ref, b_ref[...], k_taps, off, r * tr)
            o_ref[r * tr:(r + 1) * tr, :] = _f_geglu(z[:, 0:cb], z[:, cb:2 * cb]).astype(o_ref.dtype)

    two = lambda rows_: [pl.BlockSpec((rows_, cb), lambda j: (0, j)), pl.BlockSpec((rows_, cb), lambda j: (0, nb + j))]
    return pl.pallas_call(
        body, name="conv_geglu_fwd", grid=(nb,),
        in_specs=two(s) + two(k_taps) + two(1),
        out_specs=pl.BlockSpec((s, cb), lambda j: (0, j)),
        out_shape=jax.ShapeDtypeStruct((s, f), _MXU_DT),
        scratch_shapes=[pltpu.VMEM((s + pad, 2 * cb), F32), pltpu.VMEM((k_taps, 2 * cb), F32),
                        pltpu.VMEM((1, 2 * cb), F32)],
        compiler_params=_cparams(("parallel",)),
    )(up, up, w, w, b, b)


def _pair(dst_ref, first, second, cb):
    dst_ref[:, 0:cb] = first
    dst_ref[:, cb:2 * cb] = second


def _conv_geglu_bwd(up, w, b, dact):
    s, f2 = up.shape
    f = f2 // 2
    k_taps = w.shape[0]
    cb, pad = _FFN_CB, 8
    off = pad - (k_taps - 1)
    nb = f // cb

    def body(ug_ref, uv_ref, wg_ref, wv_ref, bg_ref, bv_ref, da_ref, du_ref, dw_ref, db_ref, z_ref, dz_ref, w_ref,
             b_ref, dw_sc, db_sc):
        _pair(w_ref, wg_ref[...], wv_ref[...], cb)
        _pair(b_ref, bg_ref[...], bv_ref[...], cb)
        z_ref[0:pad, :] = jnp.zeros((pad, 2 * cb), F32)
        z_ref[pad:pad + s, 0:cb] = ug_ref[...]
        z_ref[pad:pad + s, cb:2 * cb] = uv_ref[...]
        dz_ref[s:s + pad, :] = jnp.zeros((pad, 2 * cb), F32)
        tr = _conv_tr(2 * cb)
        for r in range(s // tr):
            rows = slice(r * tr, (r + 1) * tr)
            z = _conv_tile(z_ref, w_ref, b_ref[...], k_taps, off, r * tr)
            _, vjp = jax.vjp(_f_geglu, z[:, 0:cb], z[:, cb:2 * cb])
            dzg, dzv = vjp(da_ref[rows, :].astype(F32))
            dz_ref[rows, 0:cb] = dzg
            dz_ref[rows, cb:2 * cb] = dzv
        for r in range(s // tr):
            rows = slice(r * tr, (r + 1) * tr)
            du = _conv_bwd_input_tile(dz_ref, w_ref, k_taps, r * tr).astype(du_ref.dtype)
            du_ref[0, rows, :] = du[:, 0:cb]
            du_ref[1, rows, :] = du[:, cb:2 * cb]
        _conv_bwd_weight(dz_ref, z_ref, dw_sc, db_sc, k_taps, off, s)
        for half in range(2):
            dw_ref[half] = dw_sc[:, half * cb:(half + 1) * cb]
            db_ref[half] = db_sc[:, half * cb:(half + 1) * cb]

    two = lambda rows_: [pl.BlockSpec((rows_, cb), lambda j: (0, j)), pl.BlockSpec((rows_, cb), lambda j: (0, nb + j))]
    both = lambda rows_: pl.BlockSpec((2, rows_, cb), lambda j: (0, 0, j))
    return pl.pallas_call(
        body, name="conv_geglu_bwd", grid=(nb,),
        in_specs=two(s) + two(k_taps) + two(1) + [pl.BlockSpec((s, cb), lambda j: (0, j))],
        out_specs=[both(s), both(k_taps), both(1)],
        out_shape=[jax.ShapeDtypeStruct((2, s, f), _MXU_DT), jax.ShapeDtypeStruct((2, k_taps, f), F32),
                   jax.ShapeDtypeStruct((2, 1, f), F32)],
        scratch_shapes=[pltpu.VMEM((s + pad, 2 * cb), F32), pltpu.VMEM((s + pad, 2 * cb), F32),
                        pltpu.VMEM((k_taps, 2 * cb), F32), pltpu.VMEM((1, 2 * cb), F32),
                        pltpu.VMEM((k_taps, 2 * cb), F32), pltpu.VMEM((1, 2 * cb), F32)],
        compiler_params=_cparams(("parallel",)),
    )(up, up, w, w, b, b, dact)


def _tril_mask():
    t = lax.broadcasted_iota(jnp.int32, (SG_CHUNK, SG_CHUNK), 0)
    s = lax.broadcasted_iota(jnp.int32, (SG_CHUNK, SG_CHUNK), 1)
    return t >= s


def _sg_mix_fwd(u, vn, w, bcol):
    s, c = u.shape
    gw = c // SG_GROUPS

    def body(u_ref, v_ref, w_ref, b_ref, o_ref):
        wm = jnp.where(_tril_mask(), w_ref[0], 0.0).astype(_MXU_DT)
        for n in range(s // SG_CHUNK):
            rows = slice(n * SG_CHUNK, (n + 1) * SG_CHUNK)
            mixed = jnp.dot(wm, v_ref[rows, :], preferred_element_type=F32) + b_ref[0]
            o_ref[rows, :] = (u_ref[rows, :] * mixed).astype(o_ref.dtype)

    return pl.pallas_call(
        body, name="sg_mix_fwd", grid=(SG_GROUPS,),
        in_specs=[pl.BlockSpec((s, gw), lambda g: (0, g)), pl.BlockSpec((s, gw), lambda g: (0, g)),
                  pl.BlockSpec((1, SG_CHUNK, SG_CHUNK), lambda g: (g, 0, 0)),
                  pl.BlockSpec((1, SG_CHUNK, 1), lambda g: (g, 0, 0))],
        out_specs=pl.BlockSpec((s, gw), lambda g: (0, g)),
        out_shape=jax.ShapeDtypeStruct((s, c), _MXU_DT),
        compiler_params=_cparams(("parallel",)),
    )(u, vn, w, bcol)


def _sg_mix_bwd(u, vn, w, bcol, dub):
    s, c = u.shape
    gw = c // SG_GROUPS

    def body(u_ref, v_ref, w_ref, b_ref, d_ref, du_ref, dv_ref, dw_ref, db_ref):
        mask = _tril_mask()
        wm = jnp.where(mask, w_ref[0], 0.0).astype(_MXU_DT)
        dw = jnp.zeros((SG_CHUNK, SG_CHUNK), F32)
        db = jnp.zeros((SG_CHUNK, 1), F32)
        for n in range(s // SG_CHUNK):
            rows = slice(n * SG_CHUNK, (n + 1) * SG_CHUNK)
            v = v_ref[rows, :]
            d = d_ref[rows, :].astype(F32)
            mixed = jnp.dot(wm, v, preferred_element_type=F32) + b_ref[0]
            du_ref[rows, :] = d * mixed
            dmix = d * u_ref[rows, :]
            dmix_lo = dmix.astype(_MXU_DT)
            dv_ref[rows, :] = lax.dot_general(wm, dmix_lo, (((0,), (0,)), ((), ())), preferred_element_type=F32)
            dw = dw + lax.dot_general(dmix_lo, v, (((1,), (1,)), ((), ())), preferred_element_type=F32)
            db = db + jnp.sum(dmix, axis=1, keepdims=True)
        dw_ref[0] = jnp.where(mask, dw, 0.0)
        db_ref[0] = db

    return pl.pallas_call(
        body, name="sg_mix_bwd", grid=(SG_GROUPS,),
        in_specs=[pl.BlockSpec((s, gw), lambda g: (0, g)), pl.BlockSpec((s, gw), lambda g: (0, g)),
                  pl.BlockSpec((1, SG_CHUNK, SG_CHUNK), lambda g: (g, 0, 0)),
                  pl.BlockSpec((1, SG_CHUNK, 1), lambda g: (g, 0, 0)), pl.BlockSpec((s, gw), lambda g: (0, g))],
        out_specs=[pl.BlockSpec((s, gw), lambda g: (0, g)), pl.BlockSpec((s, gw), lambda g: (0, g)),
                   pl.BlockSpec((1, SG_CHUNK, SG_CHUNK), lambda g: (g, 0, 0)),
                   pl.BlockSpec((1, SG_CHUNK, 1), lambda g: (g, 0, 0))],
        out_shape=[jax.ShapeDtypeStruct((s, c), F32), jax.ShapeDtypeStruct((s, c), F32),
                   jax.ShapeDtypeStruct((SG_GROUPS, SG_CHUNK, SG_CHUNK), F32),
                   jax.ShapeDtypeStruct((SG_GROUPS, SG_CHUNK, 1), F32)],
        compiler_params=_cparams(("parallel",)),
    )(u, vn, w, bcol, dub)


def _rope_fwd(q2, kv2, p, krm_idx, krs_idx, tc, ts):
    s = q2.shape[0]
    hw = N_HEADS * HEAD_PAD
    tm = 256

    def body(qm_ref, qs_ref, kn_ref, v_ref, krm_ref, krs_ref, tc_ref, ts_ref, q_ref, k_ref, vo_ref):
        tcv, tsv = tc_ref[...], ts_ref[...]
        kpe = krm_ref[...] * tcv + krs_ref[...] * tsv
        for h in range(N_HEADS):
            cols = slice(h * HEAD_PAD, (h + 1) * HEAD_PAD)
            q_ref[:, cols] = (qm_ref[:, cols] * tcv + qs_ref[:, cols] * tsv).astype(q_ref.dtype)
            k_ref[:, cols] = (kn_ref[:, cols] + kpe).astype(k_ref.dtype)
        vo_ref[...] = v_ref[...].astype(vo_ref.dtype)

    return pl.pallas_call(
        body, name="rope_fwd", grid=(s // tm,),
        in_specs=[_row_spec(tm, hw, 0), _row_spec(tm, hw, 1), _row_spec(tm, hw, 0), _row_spec(tm, hw, 1),
                  _row_spec(tm, HEAD_PAD, krm_idx), _row_spec(tm, HEAD_PAD, krs_idx),
                  _row_spec(tm, HEAD_PAD, 0), _row_spec(tm, HEAD_PAD, 0)],
        out_specs=[_row_spec(tm, hw, 0)] * 3,
        out_shape=[jax.ShapeDtypeStruct((s, hw), _MXU_DT)] * 3,
        compiler_params=_cparams(("parallel",)),
    )(q2, q2, kv2, kv2, p, p, tc, ts)


def _rope_bwd(dq, dk, dv, tc, ts, dp, kr_blk):
    s = dq.shape[0]
    hw = N_HEADS * HEAD_PAD
    tm = 256

    def body(dq_ref, dk_ref, dv_ref, tc_ref, ts_ref, dp_in, dq2_ref, dkv2_ref, dkr_ref):
        tcv, tsv = tc_ref[...], ts_ref[...]
        dkpe = jnp.zeros((tm, HEAD_PAD), F32)
        for h in range(N_HEADS):
            cols = slice(h * HEAD_PAD, (h + 1) * HEAD_PAD)
            dqh = dq_ref[:, cols]
            dq2_ref[:, cols] = (dqh * tcv).astype(dq2_ref.dtype)
            dq2_ref[:, hw + h * HEAD_PAD:hw + (h + 1) * HEAD_PAD] = (dqh * tsv).astype(dq2_ref.dtype)
            dkpe = dkpe + dk_ref[:, cols]
        dkv2_ref[:, 0:hw] = dk_ref[...].astype(dkv2_ref.dtype)
        dkv2_ref[:, hw:2 * hw] = dv_ref[...].astype(dkv2_ref.dtype)
        dkr_ref[:, 0:HEAD_PAD] = (dkpe * tcv).astype(dkr_ref.dtype)
        dkr_ref[:, HEAD_PAD:2 * HEAD_PAD] = (dkpe * tsv).astype(dkr_ref.dtype)
        dkr_ref[:, 2 * HEAD_PAD:3 * HEAD_PAD] = jnp.zeros((tm, HEAD_PAD), dkr_ref.dtype)

    return pl.pallas_call(
        body, name="rope_bwd", grid=(s // tm,),
        in_specs=[_row_spec(tm, hw, 0)] * 3 + [_row_spec(tm, HEAD_PAD, 0)] * 2 + [_ANY],
        out_specs=[_row_spec(tm, 2 * hw, 0), _row_spec(tm, 2 * hw, 0), _row_spec(tm, 3 * HEAD_PAD, kr_blk)],
        out_shape=[jax.ShapeDtypeStruct((s, 2 * hw), _MXU_DT), jax.ShapeDtypeStruct((s, 2 * hw), _MXU_DT),
                   jax.ShapeDtypeStruct(dp.shape, dp.dtype)],
        input_output_aliases={5: 2},
        compiler_params=_cparams(("parallel",)),
    )(dq, dk, dv, tc, ts, dp)


_ATTN_TQ = 256
_ATTN_SCALE = (QK_NOPE + QK_ROPE) ** -0.5


def _attn_probs(q, k, i):
    s = k.shape[0]
    sc = lax.dot_general(q, k, (((1,), (1,)), ((), ())), preferred_element_type=F32) * _ATTN_SCALE
    row = i * _ATTN_TQ + lax.broadcasted_iota(jnp.int32, (_ATTN_TQ, s), 0)
    col = lax.broadcasted_iota(jnp.int32, (_ATTN_TQ, s), 1)
    sc = jnp.where(row >= col, sc, jnp.finfo(F32).min)
    e = jnp.exp(sc - jnp.max(sc, axis=1, keepdims=True))
    return e / jnp.sum(e, axis=1, keepdims=True)


def _per_query_block(s, fn):
    i = pl.program_id(1)
    for n in range(s // _ATTN_TQ):
        @pl.when(i == n)
        def _(n=n):
            fn(n, (n + 1) * _ATTN_TQ)


def _attn_fwd(q, k, v):
    s = q.shape[0]

    def body(q_ref, k_ref, v_ref, o_ref):
        def block(n, kl):
            p = _attn_probs(q_ref[...], k_ref[0:kl, :], n)
            o_ref[...] = jnp.dot(p.astype(_MXU_DT), v_ref[0:kl, :], preferred_element_type=F32).astype(o_ref.dtype)

        _per_query_block(s, block)

    qspec = pl.BlockSpec((_ATTN_TQ, HEAD_PAD), lambda h, i: (i, h))
    kspec = pl.BlockSpec((s, HEAD_PAD), lambda h, i: (0, h))
    return pl.pallas_call(
        body, name="attn_fwd", grid=(N_HEADS, s // _ATTN_TQ),
        in_specs=[qspec, kspec, kspec], out_specs=qspec,
        out_shape=jax.ShapeDtypeStruct(q.shape, _MXU_DT),
        compiler_params=_cparams(("parallel", "parallel")),
    )(q, k, v)


def _attn_bwd(q, k, v, do):
    s = q.shape[0]

    def body(q_ref, k_ref, v_ref, do_ref, dq_ref, dk_ref, dv_ref):
        i = pl.program_id(1)

        @pl.when(i == 0)
        def _():
            dk_ref[...] = jnp.zeros_like(dk_ref)
            dv_ref[...] = jnp.zeros_like(dv_ref)

        def block(n, kl):
            qv, kv, dov = q_ref[...], k_ref[0:kl, :], do_ref[...]
            p = _attn_probs(qv, kv, n)
            dp = lax.dot_general(dov, v_ref[0:kl, :], (((1,), (1,)), ((), ())), preferred_element_type=F32)
            delta = jnp.sum(p * dp, axis=1, keepdims=True)
            ds = (p * (dp - delta) * _ATTN_SCALE).astype(_MXU_DT)
            dq_ref[...] = jnp.dot(ds, kv, preferred_element_type=F32)
            dk_ref[0:kl, :] += lax.dot_general(ds, qv, (((0,), (0,)), ((), ())), preferred_element_type=F32)
            dv_ref[0:kl, :] += lax.dot_general(p.astype(_MXU_DT), dov, (((0,), (0,)), ((), ())),
                                               preferred_element_type=F32)

        _per_query_block(s, block)

    qspec = pl.BlockSpec((_ATTN_TQ, HEAD_PAD), lambda h, i: (i, h))
    kspec = pl.BlockSpec((s, HEAD_PAD), lambda h, i: (0, h))
    return pl.pallas_call(
        body, name="attn_bwd", grid=(N_HEADS, s // _ATTN_TQ),
        in_specs=[qspec, kspec, kspec, qspec], out_specs=[qspec, kspec, kspec],
        out_shape=[jax.ShapeDtypeStruct(q.shape, F32)] * 3,
        compiler_params=_cparams(("parallel", "arbitrary")),
    )(q, k, v, do)


def _loss_head(y, target):
    s, d = y.shape
    tm = 256

    def body(y_ref, t_ref, loss_ref, dy_ref):
        @pl.when(pl.program_id(0) == 0)
        def _():
            loss_ref[...] = jnp.zeros_like(loss_ref)

        err = y_ref[...] - t_ref[...]
        loss_ref[...] += 0.5 * jnp.sum(jnp.mean(err * err, axis=-1, keepdims=True), axis=0, keepdims=True)
        dy_ref[...] = err * (1.0 / d)

    return pl.pallas_call(
        body, name="loss_head", grid=(s // tm,),
        in_specs=[_row_spec(tm, d, 0), _row_spec(tm, d, 0)],
        out_specs=[_full_spec((1, 1)), _row_spec(tm, d, 0)],
        out_shape=[jax.ShapeDtypeStruct((1, 1), F32), jax.ShapeDtypeStruct((s, d), F32)],
        compiler_params=_cparams(("arbitrary",)),
    )(y, target)


def _adamw_math(w, g, m, v):
    mn = ADAM_B1 * m + (1.0 - ADAM_B1) * g
    vn = ADAM_B2 * v + (1.0 - ADAM_B2) * (g * g)
    m_hat = mn / (1.0 - ADAM_B1 ** ADAM_STEP)
    v_hat = vn / (1.0 - ADAM_B2 ** ADAM_STEP)
    return -ADAM_LR * (m_hat / (jnp.sqrt(v_hat) + ADAM_EPS) + ADAM_WD * w), mn, vn


def _adamw(name, w, g, m, v):
    l, k, n = w.shape
    tk, tn = _slab_block(k, n)

    def body(w_ref, g_ref, m_ref, v_ref, d_ref, mo_ref, vo_ref):
        d_ref[...], mo_ref[...], vo_ref[...] = _adamw_math(w_ref[...], g_ref[...], m_ref[...], v_ref[...])

    spec = pl.BlockSpec((1, tk, tn), lambda i, j, jn: (i, j, jn))
    return pl.pallas_call(
        body, name=name, grid=(l, k // tk, n // tn), in_specs=[spec] * 4, out_specs=[spec] * 3,
        out_shape=[jax.ShapeDtypeStruct(w.shape, F32)] * 3,
        compiler_params=_cparams(("parallel", "parallel", "parallel")),
    )(w, g, m, v)


_ANY = pl.BlockSpec(memory_space=pl.ANY)


def _mesh_pos():
    return lax.axis_index("x"), lax.axis_index("y"), lax.axis_index("c")


def _other_chips(x, y):
    return [(1 - x, y), (x, 1 - y), (1 - x, 1 - y)]


def _remote(src, dst, send_sem, recv_sem, to):
    return pltpu.make_async_remote_copy(src_ref=src, dst_ref=dst, send_sem=send_sem, recv_sem=recv_sem,
                                        device_id=to, device_id_type=_MESH)


def _all_gather_xy(shards):
    n = len(shards)

    def body(*refs):
        ins, outs, (send_sems, recv_sems) = refs[:n], refs[n:2 * n], refs[2 * n:]
        x, y, c = _mesh_pos()
        sibling = (x, y, 1 - c)
        chips = _other_chips(x, y)
        idx = lambda chip: 2 * chip[0] + chip[1]

        def copy(a, k, src, dst, to):
            return _remote(src, dst, send_sems.at[a, k], recv_sems.at[a, k], to)

        started = []
        for k, chip in enumerate(chips):
            for a in range(n):
                cp = copy(a, k, ins[a].at[c], outs[a].at[2 * x + y, c], (*chip, c))
                cp.start()
                started.append(cp)
        for k, chip in enumerate(chips):
            for a in range(n):
                landed = outs[a].at[idx(chip), c]
                copy(a, k, ins[a].at[c], landed, (*chip, c)).wait_recv()
                cp = copy(a, 3 + k, landed, landed, sibling)
                cp.start()
                started.append(cp)
        for a in range(n):
            cp = copy(a, 6, ins[a], outs[a].at[2 * x + y], sibling)
            cp.start()
            started.append(cp)
        for k, chip in enumerate(chips):
            for a in range(n):
                copy(a, 3 + k, ins[a].at[c], outs[a].at[idx(chip), 1 - c], sibling).wait_recv()
        for a in range(n):
            copy(a, 6, ins[a], outs[a].at[2 * x + y], sibling).wait_recv()
        for cp in started:
            cp.wait_send()

    return pl.pallas_call(
        body, name="all_gather_xy", in_specs=[_ANY] * n, out_specs=[_ANY] * n,
        out_shape=[jax.ShapeDtypeStruct((4,) + a.shape, a.dtype) for a in shards],
        scratch_shapes=[pltpu.SemaphoreType.DMA((n, 7)), pltpu.SemaphoreType.DMA((n, 7))],
    )(*shards)


def _swap_layers(gs):
    n = len(gs)

    def body(*refs):
        ins, outs, (send_sems, recv_sems) = refs[:n], refs[n:2 * n], refs[2 * n:]
        x, y, c = _mesh_pos()
        copies = [_remote(ins[a].at[:, 1 - c], outs[a], send_sems.at[a], recv_sems.at[a], (x, y, 1 - c))
                  for a in range(n)]
        for cp in copies:
            cp.start()
        for cp in copies:
            cp.wait()

    return pl.pallas_call(
        body, name="rs_swap_layers", in_specs=[_ANY] * n, out_specs=[_ANY] * n,
        out_shape=[jax.ShapeDtypeStruct((4,) + g.shape[2:], g.dtype) for g in gs],
        scratch_shapes=[pltpu.SemaphoreType.DMA((n,)), pltpu.SemaphoreType.DMA((n,))],
    )(*gs)


def _scatter_chips(ts):
    n = len(ts)

    def body(*refs):
        ins, outs, (send_sems, recv_sems) = refs[:n], refs[n:2 * n], refs[2 * n:]
        x, y, c = _mesh_pos()
        copies = []
        for k, chip in enumerate(_other_chips(x, y)):
            for a in range(n):
                cp = _remote(ins[a].at[2 * chip[0] + chip[1]], outs[a].at[k], send_sems.at[a, k], recv_sems.at[a, k],
                             (*chip, c))
                cp.start()
                copies.append(cp)
        for cp in copies:
            cp.wait()

    return pl.pallas_call(
        body, name="rs_scatter_chips", in_specs=[_ANY] * n, out_specs=[_ANY] * n,
        out_shape=[jax.ShapeDtypeStruct((3,) + t.shape[1:], t.dtype) for t in ts],
        scratch_shapes=[pltpu.SemaphoreType.DMA((n, 3)), pltpu.SemaphoreType.DMA((n, 3))],
    )(*ts)


def _join_layers(us):
    n = len(us)

    def body(*refs):
        ins, outs, (send_sems, recv_sems) = refs[:n], refs[n:2 * n], refs[2 * n:]
        x, y, c = _mesh_pos()
        copies = [_remote(ins[a].at[c], outs[a].at[c], send_sems.at[a], recv_sems.at[a], (x, y, 1 - c))
                  for a in range(n)]
        for cp in copies:
            cp.start()
        for a, cp in enumerate(copies):
            cp.wait_send()
            _remote(ins[a].at[c], outs[a].at[1 - c], send_sems.at[a], recv_sems.at[a], (x, y, 1 - c)).wait_recv()

    return pl.pallas_call(
        body, name="rs_join_layers", in_specs=[_ANY] * n, out_specs=[_ANY] * n,
        out_shape=[jax.ShapeDtypeStruct(u.shape, u.dtype) for u in us],
        input_output_aliases={a: a for a in range(n)},
        scratch_shapes=[pltpu.SemaphoreType.DMA((n,)), pltpu.SemaphoreType.DMA((n,))],
    )(*us)


def _slab_block(k, n, itemsize=4):
    tk = (1 << 20) // (n * itemsize) // 16 * 16
    while 0 < tk < k and k % tk:
        tk -= 16
    if 0 < tk < k:
        return tk, n
    if k * n * itemsize <= (2 << 20) or n % _LANES:
        return k, n
    tn = max(_LANES, (1 << 20) // (k * itemsize) // _LANES * _LANES)
    while n % tn:
        tn -= _LANES
    return k, tn


def _add_layer(name, g, a, c_arr):
    _, _, k, n = g.shape
    tk, tn = _slab_block(k, n)

    def body(c_ref, g_ref, a_ref, o_ref):
        o_ref[...] = (g_ref[0] + a_ref[...]).astype(o_ref.dtype)

    return pl.pallas_call(
        body, name=name,
        grid_spec=pltpu.PrefetchScalarGridSpec(
            num_scalar_prefetch=1, grid=(4, k // tk, n // tn),
            in_specs=[pl.BlockSpec((1, 1, tk, tn), lambda j, i, jn, c_ref: (j, c_ref[0], i, jn)),
                      pl.BlockSpec((1, tk, tn), lambda j, i, jn, c_ref: (j, i, jn))],
            out_specs=pl.BlockSpec((1, tk, tn), lambda j, i, jn, c_ref: (j, i, jn))),
        out_shape=jax.ShapeDtypeStruct((4, k, n), _RS_DT),
        compiler_params=_cparams(("parallel", "parallel", "parallel")),
    )(c_arr, g, a)


def _add_chips(name, t, b, pos_arr):
    _, k, n = t.shape
    tk, tn = _slab_block(k, n)

    def body(pos_ref, t_ref, b_ref, o_ref):
        f = lambda v: v.astype(F32)
        o_ref[0] = ((f(t_ref[0]) + f(b_ref[0])) + f(b_ref[1])) + f(b_ref[2])

    return pl.pallas_call(
        body, name=name,
        grid_spec=pltpu.PrefetchScalarGridSpec(
            num_scalar_prefetch=1, grid=(k // tk, n // tn),
            in_specs=[pl.BlockSpec((1, tk, tn), lambda i, jn, pos_ref: (pos_ref[0], i, jn)),
                      pl.BlockSpec((3, tk, tn), lambda i, jn, pos_ref: (0, i, jn))],
            out_specs=pl.BlockSpec((1, tk, tn), lambda i, jn, pos_ref: (pos_ref[1], i, jn))),
        out_shape=jax.ShapeDtypeStruct((2, k, n), F32),
        compiler_params=_cparams(("parallel", "parallel")),
    )(pos_arr, t, b)


def _reduce_scatter(names, gs):
    x, y, c = _mesh_pos()
    c_arr = jnp.reshape(c, (1,)).astype(jnp.int32)
    pos_arr = jnp.stack([2 * x + y, c]).astype(jnp.int32)
    sib = _swap_layers(gs)
    ts = [_add_layer("rs_add_layer_" + n, g, a, c_arr) for n, g, a in zip(names, gs, sib)]
    bs = _scatter_chips(ts)
    us = [_add_chips("rs_add_chips_" + n, t, b, pos_arr) for n, t, b in zip(names, ts, bs)]
    return _join_layers(us)


def _all_reduce_adamw(gs, ws, ms, vs):
    n = len(gs)

    def body(*refs):
        g_refs, w_refs, m_refs, v_refs = (refs[i * n:(i + 1) * n] for i in range(4))
        gsum, delta, m_out, v_out = (refs[(4 + i) * n:(5 + i) * n] for i in range(4))
        slots = refs[8 * n:9 * n]
        send_sems, recv_sems = refs[9 * n:]
        x, y, c = _mesh_pos()
        me = 4 * x + 2 * y + c
        copies = []
        for rel in range(1, 8):
            bx, by, bc = (rel >> 2) & 1, (rel >> 1) & 1, rel & 1
            peer = (1 - x if bx else x, 1 - y if by else y, 1 - c if bc else c)
            for a in range(n):
                cp = _remote(g_refs[a], slots[a].at[me], send_sems.at[a, rel - 1], recv_sems.at[a, rel - 1], peer)
                cp.start()
                copies.append(cp)
        for a in range(n):
            slots[a][me] = g_refs[a][...]
        for cp in copies:
            cp.wait()
        for a in range(n):
            acc = slots[a][0]
            for d in range(1, 8):
                acc = acc + slots[a][d]
            gsum[a][...] = acc
            delta[a][...], m_out[a][...], v_out[a][...] = _adamw_math(w_refs[a][...], acc, m_refs[a][...],
                                                                      v_refs[a][...])

    vmem = pl.BlockSpec(memory_space=pltpu.VMEM)
    outs = pl.pallas_call(
        body, name="all_reduce_adamw", in_specs=[vmem] * (4 * n), out_specs=[vmem] * (4 * n),
        out_shape=[jax.ShapeDtypeStruct(g.shape, F32) for g in gs] * 4,
        scratch_shapes=([pltpu.VMEM((8,) + g.shape, F32) for g in gs]
                        + [pltpu.SemaphoreType.DMA((n, 7)), pltpu.SemaphoreType.DMA((n, 7))]),
        compiler_params=pltpu.CompilerParams(vmem_limit_bytes=_VMEM_LIMIT),
    )(*gs, *ws, *ms, *vs)
    return outs[:n], outs[n:2 * n], outs[2 * n:3 * n], outs[3 * n:]


def _swap_rope(a):
    h = QK_ROPE // 2
    return jnp.concatenate([a[..., h:], a[..., :h]], axis=-1)


def _swap_rope_rows(a):
    h = QK_ROPE // 2
    return jnp.concatenate([a[h:], a[:h]], axis=0)


_FFN_CB = 256


def _interleave_rows(a, cb):
    r, c = a.shape
    return a.reshape(2, r // (2 * cb), cb, c).transpose(1, 0, 2, 3).reshape(r, c)


def _deinterleave_rows(a, cb):
    r, c = a.shape
    return a.reshape(r // (2 * cb), 2, cb, c).transpose(1, 0, 2, 3).reshape(r, c)


class _InLayout:
    def __init__(self, d):
        self.d = d
        self.gates = 0
        self.a = 3 * d
        self.b = 4 * d
        self.kv = 5 * d
        self.q = self.kv + 256
        self.krm = self.q + 384
        self.krs = self.krm + HEAD_PAD
        self.width = self.krs + 2 * HEAD_PAD


def _prep_layer(wl, d):
    lay = _InLayout(d)
    w_in = wl["w_in"]
    dt = w_in.dtype
    a, b = w_in[0:d], w_in[d:2 * d]
    q, kv = w_in[2 * d:2 * d + 384], w_in[2 * d + 384:2 * d + 640]
    kr = w_in[2 * d + 640:2 * d + 640 + QK_ROPE]
    gates = w_in[2 * d + 640 + QK_ROPE:]
    z = lambda n: jnp.zeros((n, d), dt)
    krm = jnp.concatenate([z(QK_NOPE), kr, z(HEAD_PAD - QK_NOPE - QK_ROPE)], axis=0)
    krs = jnp.concatenate([z(QK_NOPE), _swap_rope_rows(kr), z(HEAD_PAD - QK_NOPE - QK_ROPE)], axis=0)
    out = dict(wl)
    out["w_in"] = jnp.concatenate([gates, _interleave_rows(a, _LANES), b, kv, q, krm, krs,
                                   z(lay.width - lay.krs - HEAD_PAD)], axis=0)
    uq = wl["mla_w_uq"].reshape(-1, N_HEADS, QK_NOPE + QK_ROPE)
    nq = uq.shape[0]
    nope, pe = uq[..., :QK_NOPE], uq[..., QK_NOPE:]
    zq = lambda n: jnp.zeros((nq, N_HEADS, n), dt)
    main = jnp.concatenate([nope, pe, zq(HEAD_PAD - QK_NOPE - QK_ROPE)], axis=-1).reshape(nq, -1)
    swapped = jnp.concatenate([zq(QK_NOPE), _swap_rope(pe), zq(HEAD_PAD - QK_NOPE - QK_ROPE)], axis=-1).reshape(nq, -1)
    out["mla_w_uq"] = jnp.concatenate([main, swapped], axis=1)
    ukv = wl["mla_w_ukv"].reshape(-1, N_HEADS, QK_NOPE + V_HEAD)
    nkv = ukv.shape[0]
    zk = jnp.zeros((nkv, N_HEADS, HEAD_PAD - QK_NOPE), dt)
    zv = jnp.zeros((nkv, N_HEADS, HEAD_PAD - V_HEAD), dt)
    out["mla_w_ukv"] = jnp.concatenate([jnp.concatenate([ukv[..., :QK_NOPE], zk], axis=-1).reshape(nkv, -1),
                                        jnp.concatenate([ukv[..., QK_NOPE:], zv], axis=-1).reshape(nkv, -1)], axis=1)
    wo = wl["mla_w_o"].reshape(N_HEADS, V_HEAD, -1)
    out["mla_w_o"] = jnp.concatenate([wo, jnp.zeros((N_HEADS, HEAD_PAD - V_HEAD, wo.shape[-1]), dt)],
                                     axis=1).reshape(N_HEADS * HEAD_PAD, -1)
    return out


def _unprep_grads(g, d):
    lay = _InLayout(d)
    gi = g["w_in"]
    kr = (gi[lay.krm + QK_NOPE:lay.krm + QK_NOPE + QK_ROPE]
          + _swap_rope_rows(gi[lay.krs + QK_NOPE:lay.krs + QK_NOPE + QK_ROPE]))
    out = dict(g)
    out["w_in"] = jnp.concatenate([_deinterleave_rows(gi[lay.a:lay.a + d], _LANES), gi[lay.b:lay.b + d],
                                   gi[lay.q:lay.q + 384], gi[lay.kv:lay.kv + 256], kr,
                                   gi[lay.gates:lay.gates + 3 * d]], axis=0)
    hw = N_HEADS * HEAD_PAD
    gq = g["mla_w_uq"]
    nq = gq.shape[0]
    main = gq[:, :hw].reshape(nq, N_HEADS, HEAD_PAD)
    swapped = gq[:, hw:].reshape(nq, N_HEADS, HEAD_PAD)
    pe = main[..., QK_NOPE:QK_NOPE + QK_ROPE] + _swap_rope(swapped[..., QK_NOPE:QK_NOPE + QK_ROPE])
    out["mla_w_uq"] = jnp.concatenate([main[..., :QK_NOPE], pe], axis=-1).reshape(nq, -1)
    gkv = g["mla_w_ukv"]
    nkv = gkv.shape[0]
    out["mla_w_ukv"] = jnp.concatenate([gkv[:, :hw].reshape(nkv, N_HEADS, HEAD_PAD)[..., :QK_NOPE],
                                        gkv[:, hw:].reshape(nkv, N_HEADS, HEAD_PAD)[..., :V_HEAD]],
                                       axis=-1).reshape(nkv, -1)
    go = g["mla_w_o"]
    out["mla_w_o"] = go.reshape(N_HEADS, HEAD_PAD, -1)[:, :V_HEAD].reshape(N_HEADS * V_HEAD, -1)
    return out


def _rope_tables(positions):
    s = positions.shape[0]
    inv = ROPE_THETA ** (-jnp.arange(0, QK_ROPE, 2, dtype=F32) / QK_ROPE)
    ang = positions.astype(F32)[:, None] * inv
    cos, sin = jnp.cos(ang), jnp.sin(ang)
    tail = jnp.zeros((s, HEAD_PAD - QK_NOPE - QK_ROPE), F32)
    tc = jnp.concatenate([jnp.ones((s, QK_NOPE), F32), cos, cos, tail], axis=1)
    ts = jnp.concatenate([jnp.zeros((s, QK_NOPE), F32), -sin, sin, tail], axis=1)
    return tc, ts


def _row(v):
    return v.reshape(1, -1)


def _layer_fwd(x, h, w, g_next, tc, ts):
    d = x.shape[1]
    lay = _InLayout(d)
    cw = d // 2
    blk = lambda off, width: off // width
    p = _mm("mm_in", h, w["w_in"], tb=True)
    z1 = _glu_conv_fwd(p, blk(lay.a, 2 * _LANES), w["conv_dw_w"], _row(w["conv_dw_b"]))
    ln_a = [_row(w["conv_ln_g"]), _row(w["conv_ln_b"])]
    (z3,) = _row_fwd("ln_silu_fwd", _f_ln_silu, [(z1, cw, 0)], ln_a, [(cw, _MXU_DT)])
    ya = _mm("mm_conv_out", z3, w["conv_out_w"])
    ln_b = [_row(w["sg_ln_g"]), _row(w["sg_ln_b"])]
    u, vn = _row_fwd("sg_pre_fwd", _f_sg_pre, [(p, cw, blk(lay.b, cw)), (p, cw, blk(lay.b + cw, cw))], ln_b,
                     [(cw, F32), (cw, _MXU_DT)])
    bcol = w["sg_b"].reshape(SG_GROUPS, SG_CHUNK, 1)
    ub = _sg_mix_fwd(u, vn, w["sg_w"], bcol)
    yb = _mm("mm_sg_out", ub, w["sg_out_w"])
    (qn,) = _row_fwd("q_norm_fwd", _f_rms, [(p, 384, blk(lay.q, 384))], [_row(w["mla_q_norm_g"])], [(384, _MXU_DT)])
    (kvn,) = _row_fwd("kv_norm_fwd", _f_rms, [(p, 256, blk(lay.kv, 256))], [_row(w["mla_kv_norm_g"])],
                      [(256, _MXU_DT)])
    q2 = _mm("mm_uq", qn, w["mla_w_uq"])
    kv2 = _mm("mm_ukv", kvn, w["mla_w_ukv"])
    qf, kf, vf = _rope_fwd(q2, kv2, p, blk(lay.krm, HEAD_PAD), blk(lay.krs, HEAD_PAD), tc, ts)
    o = _attn_fwd(qf, kf, vf)
    yc = _mm("mm_o", o, w["mla_w_o"])
    gate_rows = [(p, d, 0), (p, d, 1), (p, d, 2)]
    (merged,) = _row_fwd("merge_fwd", _f_merge, gate_rows + [(ya, d, 0), (yb, d, 0), (yc, d, 0)], [], [(d, _MXU_DT)])
    t = _mm("mm_out", merged, w["w_out"])
    x1, h2 = _row_fwd("resid_mix_fwd", _f_resid_rms_rms, [(x, d, 0), (t, d, 0)],
                      [_row(w["mix_post_g"]), _row(w["ffn_pre_g"])], [(d, F32), (d, _MXU_DT)])
    up = _mm("mm_up", h2, w["ffn_w_up"])
    act = _conv_geglu_fwd(up, w["ffn_dw_w"], _row(w["ffn_dw_b"]))
    dn = _mm("mm_down", act, w["ffn_w_down"])
    if g_next is None:
        (x2,) = _row_fwd("resid_ffn_last_fwd", _f_resid_rms, [(x1, d, 0), (dn, d, 0)], [_row(w["ffn_post_g"])],
                         [(d, F32)])
        h_next = None
    else:
        x2, h_next = _row_fwd("resid_ffn_fwd", _f_resid_rms_rms, [(x1, d, 0), (dn, d, 0)],
                              [_row(w["ffn_post_g"]), _row(g_next)], [(d, F32), (d, _MXU_DT)])
    saved = dict(x=x, h=h, p=p, z1=z1, z3=z3, ya=ya, u=u, vn=vn, ub=ub, yb=yb, qn=qn, kvn=kvn, qf=qf, kf=kf, vf=vf, o=o,
                 yc=yc, merged=merged, t=t, x1=x1, h2=h2, up=up, act=act, dn=dn, bcol=bcol)
    return x2, h_next, saved


def _layer_bwd(dx2, dh_next, w, g_next, sv, tc, ts):
    d = dx2.shape[1]
    lay = _InLayout(d)
    cw = d // 2
    blk = lambda off, width: off // width
    lo = _MXU_DT
    g = {}
    x1, dn = sv["x1"], sv["dn"]
    if dh_next is None:
        dx1, ddn, g["ffn_post_g"] = _row_bwd("resid_ffn_last_bwd", _f_resid_rms, [(x1, d, 0), (dn, d, 0)],
                                             [_row(w["ffn_post_g"])], [(dx2, d, 0)], [(0, F32), (1, lo)], [0])
    else:
        dx1, ddn, g["ffn_post_g"], g["next_pre_g"] = _row_bwd(
            "resid_ffn_bwd", _f_resid_rms_rms, [(x1, d, 0), (dn, d, 0)], [_row(w["ffn_post_g"]), _row(g_next)],
            [(dx2, d, 0), (dh_next, d, 0)], [(0, F32), (1, lo)], [0, 1])
    dact = _mm("mm_down_dx", ddn, w["ffn_w_down"], tb=True)
    g["ffn_w_down"] = _mm("mm_down_dw", sv["act"], ddn, ta=True)
    dup, dw_halves, db_halves = _conv_geglu_bwd(sv["up"], w["ffn_dw_w"], _row(w["ffn_dw_b"]), dact)
    g["ffn_dw_w"] = jnp.concatenate([dw_halves[0], dw_halves[1]], axis=1)
    g["ffn_dw_b"] = jnp.concatenate([db_halves[0], db_halves[1]], axis=1)
    dh2 = _mm("mm_up_dx", dup, w["ffn_w_up"], tb=True, a_halves=True)
    g["ffn_w_up"] = _mm("mm_up_dw", sv["h2"], dup, ta=True, b_halves=True, out_quarters=True)
    dx, dt, g["mix_post_g"], g["ffn_pre_g"] = _row_bwd(
        "resid_mix_bwd", _f_resid_rms_rms, [(sv["x"], d, 0), (sv["t"], d, 0)],
        [_row(w["mix_post_g"]), _row(w["ffn_pre_g"])], [(dx1, d, 0), (dh2, d, 0)], [(0, F32), (1, lo)], [0, 1])
    dmerged = _mm("mm_out_dx", dt, w["w_out"], tb=True)
    g["w_out"] = _mm("mm_out_dw", sv["merged"], dt, ta=True)
    p = sv["p"]
    gate_rows = [(p, d, 0), (p, d, 1), (p, d, 2)]
    dp, dya, dyb, dyc = _row_bwd(
        "merge_bwd", _f_merge, gate_rows + [(sv["ya"], d, 0), (sv["yb"], d, 0), (sv["yc"], d, 0)], [],
        [(dmerged, d, 0)], [((0, 1, 2), lo), ((3,), lo), ((4,), lo), ((5,), lo)], [], place=(lay.width, 0))
    do = _mm("mm_o_dx", dyc, w["mla_w_o"], tb=True, out_dtype=lo)
    g["mla_w_o"] = _mm("mm_o_dw", sv["o"], dyc, ta=True)
    dqf, dkf, dvf = _attn_bwd(sv["qf"], sv["kf"], sv["vf"], do)
    dq2, dkv2, dp = _rope_bwd(dqf, dkf, dvf, tc, ts, dp, blk(lay.krm, 3 * HEAD_PAD))
    dkvn = _mm("mm_ukv_dx", dkv2, w["mla_w_ukv"], tb=True)
    g["mla_w_ukv"] = _mm("mm_ukv_dw", sv["kvn"], dkv2, ta=True)
    dqn = _mm("mm_uq_dx", dq2, w["mla_w_uq"], tb=True)
    g["mla_w_uq"] = _mm("mm_uq_dw", sv["qn"], dq2, ta=True)
    dp, g["mla_q_norm_g"] = _row_bwd("q_norm_bwd", _f_rms, [(p, 384, blk(lay.q, 384))], [_row(w["mla_q_norm_g"])],
                                     [(dqn, 384, 0)], [((0,), lo)], [0], place=(lay.width, blk(lay.q, 384)), into=dp)
    dp, g["mla_kv_norm_g"] = _row_bwd("kv_norm_bwd", _f_rms, [(p, 256, blk(lay.kv, 256))],
                                      [_row(w["mla_kv_norm_g"])], [(dkvn, 256, 0)], [((0,), lo)], [0],
                                      place=(lay.width, blk(lay.kv, 256)), into=dp)
    dub = _mm("mm_sg_out_dx", dyb, w["sg_out_w"], tb=True)
    g["sg_out_w"] = _mm("mm_sg_out_dw", sv["ub"], dyb, ta=True)
    du, dvn, g["sg_w"], dbcol = _sg_mix_bwd(sv["u"], sv["vn"], w["sg_w"], sv["bcol"], dub)
    g["sg_b"] = dbcol.reshape(SG_GROUPS, SG_CHUNK)
    dp, g["sg_ln_g"], g["sg_ln_b"] = _row_bwd(
        "sg_pre_bwd", _f_sg_pre, [(p, cw, blk(lay.b, cw)), (p, cw, blk(lay.b + cw, cw))],
        [_row(w["sg_ln_g"]), _row(w["sg_ln_b"])], [(du, cw, 0), (dvn, cw, 0)], [((0, 1), lo)], [0, 1],
        place=(lay.width, blk(lay.b, d)), into=dp)
    dz3 = _mm("mm_conv_out_dx", dya, w["conv_out_w"], tb=True)
    g["conv_out_w"] = _mm("mm_conv_out_dw", sv["z3"], dya, ta=True)
    dz1, g["conv_ln_g"], g["conv_ln_b"] = _row_bwd(
        "ln_silu_bwd", _f_ln_silu, [(sv["z1"], cw, 0)], [_row(w["conv_ln_g"]), _row(w["conv_ln_b"])],
        [(dz3, cw, 0)], [((0,), F32)], [0, 1])
    dp, g["conv_dw_w"], g["conv_dw_b"] = _glu_conv_bwd(p, blk(lay.a, 2 * _LANES), w["conv_dw_w"], dz1, dp)
    dh = _mm("mm_in_dx", dp, w["w_in"])
    g["w_in"] = _mm("mm_in_dw", dp, sv["h"], ta=True)
    return dx, dh, g


def _local_step(x, positions, target, layers):
    d = x.shape[1]
    tc, ts = _rope_tables(positions)
    ws = [_prep_layer(wl, d) for wl in layers]
    depth = len(ws)
    (h,) = _row_fwd("rms_first_fwd", _f_rms, [(x, d, 0)], [_row(ws[0]["mix_pre_g"])], [(d, _MXU_DT)])
    saved = []
    for l in range(depth):
        g_next = ws[l + 1]["mix_pre_g"] if l + 1 < depth else None
        x, h, sv = _layer_fwd(x, h, ws[l], g_next, tc, ts)
        saved.append(sv)
    loss, dx = _loss_head(x, target)
    grads = [None] * depth
    dh = None
    for l in reversed(range(depth)):
        g_next = ws[l + 1]["mix_pre_g"] if l + 1 < depth else None
        dx, dh, g = _layer_bwd(dx, dh, ws[l], g_next, saved[l], tc, ts)
        if "next_pre_g" in g:
            grads[l + 1]["mix_pre_g"] = g.pop("next_pre_g")
        grads[l] = g
    x0 = saved[0]["x"]
    grad_x, grads[0]["mix_pre_g"] = _row_bwd("rms_first_bwd", _f_x_rms, [(x0, d, 0)], [_row(ws[0]["mix_pre_g"])],
                                             [(dx, d, 0), (dh, d, 0)], [(0, F32)], [0])
    return loss, grad_x, [_unprep_grads(g, d) for g in grads]


_MATRICES = ("w_in", "conv_out_w", "sg_out_w", "mla_w_uq", "mla_w_ukv", "mla_w_o", "w_out", "ffn_w_up", "ffn_w_down")
_F32_GATHERED = ("conv_dw_w", "ffn_dw_w")
_RS_DT = jnp.bfloat16


_ROW_SHARDED = SHARDED_MID + ("w_in",)


def _gather_weights(wshard):
    names = list(SHARDED)
    hi = {n: wshard[n].astype(jnp.bfloat16) for n in names}
    lo = {n: (wshard[n] - hi[n].astype(F32)).astype(jnp.bfloat16) for n in _F32_GATHERED}
    arrays = [hi[n] for n in names] + [lo[n] for n in _F32_GATHERED]
    gathered = _all_gather_xy(arrays)
    out = {}
    for i, n in enumerate(names + list(_F32_GATHERED)):
        whole = [jnp.concatenate([gathered[i][j, l] for j in range(4)], axis=0 if n in _ROW_SHARDED else 1)
                 for l in range(2)]
        out[n] = [a.astype(F32) + b.astype(F32) for a, b in zip(out[n], whole)] if n in out else whole
    return out


def _by_destination(name, per_layer):
    out = []
    for gl in per_layer:
        if gl.ndim == 3:
            out.append(gl)
            continue
        k, n = gl.shape
        if name in _ROW_SHARDED:
            out.append(gl.reshape(4, k // 4, n))
        else:
            out.append(gl.reshape(k, 4, n // 4).transpose(1, 0, 2))
    return jnp.stack(out, axis=1)


def kernel(x, positions, mix_pre_g, mix_post_g, ffn_pre_g, ffn_post_g, w_in, conv_dw_w, conv_dw_b, conv_ln_g, conv_ln_b, conv_out_w, sg_ln_g, sg_ln_b, sg_w, sg_b, sg_out_w, mla_q_norm_g, mla_w_uq, mla_kv_norm_g, mla_w_ukv, mla_w_o, w_out, ffn_w_up, ffn_dw_w, ffn_dw_b, ffn_w_down, loss_target, m_mix_pre_g, m_mix_post_g, m_ffn_pre_g, m_ffn_post_g, m_w_in, m_conv_dw_w, m_conv_dw_b, m_conv_ln_g, m_conv_ln_b, m_conv_out_w, m_sg_ln_g, m_sg_ln_b, m_sg_w, m_sg_b, m_sg_out_w, m_mla_q_norm_g, m_mla_w_uq, m_mla_kv_norm_g, m_mla_w_ukv, m_mla_w_o, m_w_out, m_ffn_w_up, m_ffn_dw_w, m_ffn_dw_b, m_ffn_w_down, v_mix_pre_g, v_mix_post_g, v_ffn_pre_g, v_ffn_post_g, v_w_in, v_conv_dw_w, v_conv_dw_b, v_conv_ln_g, v_conv_ln_b, v_conv_out_w, v_sg_ln_g, v_sg_ln_b, v_sg_w, v_sg_b, v_sg_out_w, v_mla_q_norm_g, v_mla_w_uq, v_mla_kv_norm_g, v_mla_w_ukv, v_mla_w_o, v_w_out, v_ffn_w_up, v_ffn_dw_w, v_ffn_dw_b, v_ffn_w_down):
    args = dict(locals())
    w = {n: args[n] for n in WEIGHTS}
    m = {n: args["m_" + n] for n in WEIGHTS}
    v = {n: args["v_" + n] for n in WEIGHTS}
    depth = mix_pre_g.shape[0]

    assert depth == 2, "the two cores of a chip split the communication by layer"
    for t in (w, m, v):
        t["w_in"] = jnp.swapaxes(t["w_in"], 1, 2)
    full = _gather_weights({n: w[n] for n in SHARDED})
    layers = []
    for l in range(depth):
        wl = {n: w[n][l] for n in REPLICATED}
        for n in SHARDED:
            wl[n] = full[n][l].astype(_MXU_DT if n in _MATRICES else F32)
        layers.append(wl)

    loss, grad_x, grads = _local_step(x[0], positions[0], loss_target[0], layers)
    loss = lax.psum(loss[0, 0], ("x", "y", "c"))

    out = {}
    names = list(SHARDED)
    reduced = _reduce_scatter(names, [_by_destination(n, [grads[l][n] for l in range(depth)]) for n in names])
    for n, gr in zip(names, reduced):
        out[n] = (gr, *_adamw("adamw_" + n, w[n], gr, m[n], v[n]))
    out["w_in"] = tuple(jnp.swapaxes(a, 1, 2) for a in out["w_in"])
    rep = list(REPLICATED)
    g_rep = [jnp.stack([grads[l][n].reshape(w[n].shape[1:]) for l in range(depth)]) for n in rep]
    for n, *res in zip(rep, *_all_reduce_adamw(g_rep, [w[n] for n in rep], [m[n] for n in rep], [v[n] for n in rep])):
        out[n] = tuple(res)
    return (loss, grad_x[None], *[out[n][i] for i in range(4) for n in WEIGHTS])
```

```python
import functools
import math

import jax
import jax.numpy as jnp
from jax import lax
from jax.experimental import pallas as pl
from jax.experimental.pallas import tpu as pltpu

F32 = jnp.float32
_MXU_DT = jnp.bfloat16
_VMEM_LIMIT = 48 * 1024 * 1024
_LANES = 128
_MESH = pl.DeviceIdType.MESH

N_HEADS = 8
QK_NOPE = 64
QK_ROPE = 32
V_HEAD = 64
HEAD_PAD = 128
SG_GROUPS = 4
SG_CHUNK = 128
CONV_K = 31
FFN_K = 3
ROPE_THETA = 10000.0
EPS = 1e-6
ADAM_LR, ADAM_B1, ADAM_B2, ADAM_EPS, ADAM_WD, ADAM_STEP = 0.001, 0.9, 0.999, 1e-08, 0.01, 10

SHARDED_LAST = ("w_in", "conv_dw_w", "conv_out_w", "sg_out_w", "mla_w_uq", "mla_w_ukv", "mla_w_o", "ffn_w_up",
                "ffn_dw_w")
SHARDED_MID = ("w_out", "ffn_w_down")
SHARDED = SHARDED_LAST + SHARDED_MID
WEIGHTS = ("mix_pre_g", "mix_post_g", "ffn_pre_g", "ffn_post_g", "w_in", "conv_dw_w", "conv_dw_b", "conv_ln_g",
           "conv_ln_b", "conv_out_w", "sg_ln_g", "sg_ln_b", "sg_w", "sg_b", "sg_out_w", "mla_q_norm_g", "mla_w_uq",
           "mla_kv_norm_g", "mla_w_ukv", "mla_w_o", "w_out", "ffn_w_up", "ffn_dw_w", "ffn_dw_b", "ffn_w_down")
REPLICATED = tuple(n for n in WEIGHTS if n not in SHARDED)


def _cparams(sem=None):
    return pltpu.CompilerParams(dimension_semantics=sem, vmem_limit_bytes=_VMEM_LIMIT)


def _pick(n, cands):
    for c in cands:
        if n % c == 0:
            return c
    return n


def _largest_tile(dim, cap):
    for t in range(min(cap, dim) // _LANES * _LANES, 0, -_LANES):
        if dim % t == 0:
            return t
    return dim


_MM_VMEM_BUDGET = 36 * 1024 * 1024
_MM_TM_CAP, _MM_TN_CAP, _MM_TK_CAP = 1024, 1536, 3072


def _mm(name, a, b, *, ta=False, tb=False, out_dtype=F32, a_halves=False, b_halves=False, out_quarters=False):
    assert not (a_halves and ta) and not (b_halves and tb)
    if a_halves:
        m, kdim = a.shape[1], 2 * a.shape[2]
    else:
        (kdim, m) = a.shape if ta else a.shape[::-1]
    if b_halves:
        kdim2, n = b.shape[1], 2 * b.shape[2]
    else:
        (n, kdim2) = b.shape if tb else b.shape[::-1]
    assert kdim == kdim2, (a.shape, b.shape, ta, tb)
    tk = _largest_tile(kdim // 2 if a_halves else kdim, _MM_TK_CAP)
    tn = _largest_tile(n // 4 if out_quarters else (n // 2 if b_halves else n), _MM_TN_CAP)
    nk = kdim // tk
    ab, bb, ob = a.dtype.itemsize, b.dtype.itemsize, jnp.dtype(out_dtype).itemsize
    tm = _largest_tile(m, _MM_TM_CAP)
    vmem = lambda t: 2 * (t * tk * ab + tk * tn * bb + t * tn * ob) + (t * tn * 4 if nk > 1 else 0)
    while vmem(tm) > _MM_VMEM_BUDGET and tm > _LANES:
        tm = _largest_tile(m, tm - _LANES)
    dims = (((0 if ta else 1,), (1 if tb else 0,)), ((), ()))

    def dot(a_ref, b_ref):
        return lax.dot_general(a_ref[...].astype(_MXU_DT), b_ref[...].astype(_MXU_DT), dims,
                               preferred_element_type=F32)

    def body_one(a_ref, b_ref, o_ref):
        o_ref[...] = dot(a_ref, b_ref).astype(o_ref.dtype)

    def body_acc(a_ref, b_ref, o_ref, acc_ref):
        k = pl.program_id(2)

        @pl.when(k == 0)
        def _():
            acc_ref[...] = jnp.zeros_like(acc_ref)

        acc_ref[...] += dot(a_ref, b_ref)

        @pl.when(k == nk - 1)
        def _():
            o_ref[...] = acc_ref[...].astype(o_ref.dtype)

    if a_halves:
        per = nk // 2
        a_spec = pl.BlockSpec((None, tm, tk), lambda i, j, k: (k // per, i, k % per))
    elif ta:
        a_spec = pl.BlockSpec((tk, tm), lambda i, j, k: (k, i))
    else:
        a_spec = pl.BlockSpec((tm, tk), lambda i, j, k: (i, k))
    if b_halves:
        per_b = n // 2 // tn
        b_spec = pl.BlockSpec((None, tk, tn), lambda i, j, k: (j // per_b, k, j % per_b))
    elif tb:
        b_spec = pl.BlockSpec((tn, tk), lambda i, j, k: (j, k))
    else:
        b_spec = pl.BlockSpec((tk, tn), lambda i, j, k: (k, j))
    if out_quarters:
        per_o = n // 4 // tn
        o_spec = pl.BlockSpec((None, tm, tn), lambda i, j, k: (j // per_o, i, j % per_o))
        o_shape = jax.ShapeDtypeStruct((4, m, n // 4), out_dtype)
    else:
        o_spec = pl.BlockSpec((tm, tn), lambda i, j, k: (i, j))
        o_shape = jax.ShapeDtypeStruct((m, n), out_dtype)
    return pl.pallas_call(
        body_one if nk == 1 else body_acc, name=name, grid=(m // tm, n // tn, nk),
        in_specs=[a_spec, b_spec], out_specs=o_spec, out_shape=o_shape,
        scratch_shapes=[] if nk == 1 else [pltpu.VMEM((tm, tn), F32)],
        compiler_params=_cparams(("parallel", "parallel", "arbitrary")),
    )(a, b)


def _row_spec(tm, width, idx):
    return pl.BlockSpec((tm, width), lambda i: (i, idx))


def _full_spec(shape):
    zeros = (0,) * len(shape)
    return pl.BlockSpec(shape, lambda i: zeros)


def _row_fwd(name, fn, rows, params, outs, tm=256):
    s = rows[0][0].shape[0]
    nr, npar = len(rows), len(params)

    def body(*refs):
        vals = [r[...].astype(F32) for r in refs[:nr + npar]]
        res = fn(*vals)
        for o_ref, r in zip(refs[nr + npar:], res):
            o_ref[...] = r.astype(o_ref.dtype)

    return pl.pallas_call(
        body, name=name, grid=(s // tm,),
        in_specs=[_row_spec(tm, w, i) for _, w, i in rows] + [_full_spec(p.shape) for p in params],
        out_specs=[_row_spec(tm, w, 0) for w, _ in outs],
        out_shape=[jax.ShapeDtypeStruct((s, w), dt) for w, dt in outs],
        compiler_params=_cparams(("parallel",)),
    )(*[r[0] for r in rows], *params)


def _row_bwd(name, fn, rows, params, cots, row_grads, param_grads, tm=256, place=None, into=None):
    s = rows[0][0].shape[0]
    nr, npar, nc = len(rows), len(params), len(cots)
    row_grads = [((idxs,) if isinstance(idxs, int) else tuple(idxs), dt) for idxs, dt in row_grads]
    widths = [sum(rows[i][1] for i in idxs) for idxs, _ in row_grads]

    def body(*refs):
        i = pl.program_id(0)
        vals = [r[...].astype(F32) for r in refs[:nr + npar]]
        cvals = tuple(r[...].astype(F32) for r in refs[nr + npar:nr + npar + nc])
        _, vjp = jax.vjp(fn, *vals)
        grads = vjp(cvals)
        outs = refs[nr + npar + nc + (into is not None):]
        for o_ref, (idxs, _) in zip(outs, row_grads):
            pos = 0
            for idx in idxs:
                o_ref[:, pos:pos + rows[idx][1]] = grads[idx].astype(o_ref.dtype)
                pos += rows[idx][1]
        for o_ref, idx in zip(outs[len(row_grads):], param_grads):
            @pl.when(i == 0)
            def _(o_ref=o_ref):
                o_ref[...] = jnp.zeros_like(o_ref)

            o_ref[...] += grads[nr + idx]

    out_specs = [_row_spec(tm, w, 0) for w in widths] + [_full_spec(params[idx].shape) for idx in param_grads]
    out_shape = ([jax.ShapeDtypeStruct((s, w), dt) for w, (_, dt) in zip(widths, row_grads)]
                 + [jax.ShapeDtypeStruct(params[idx].shape, F32) for idx in param_grads])
    extra, aliases = [], {}
    if place is not None:
        out_specs[0] = _row_spec(tm, widths[0], place[1])
        out_shape[0] = jax.ShapeDtypeStruct((s, place[0]), row_grads[0][1])
    if into is not None:
        extra, aliases = [into], {nr + npar + nc: 0}
    return pl.pallas_call(
        body, name=name, grid=(s // tm,),
        in_specs=([_row_spec(tm, w, i) for _, w, i in rows] + [_full_spec(p.shape) for p in params]
                  + [_row_spec(tm, w, i) for _, w, i in cots] + [_ANY] * len(extra)),
        out_specs=out_specs, out_shape=out_shape, input_output_aliases=aliases,
        compiler_params=_cparams(("arbitrary",)),
    )(*[r[0] for r in rows], *params, *[c[0] for c in cots], *extra)


def _rms(x, g):
    return x * lax.rsqrt(jnp.mean(x * x, axis=-1, keepdims=True) + EPS) * g


def _ln(x, g, b):
    mu = jnp.mean(x, axis=-1, keepdims=True)
    xc = x - mu
    var = jnp.mean(xc * xc, axis=-1, keepdims=True)
    return xc * lax.rsqrt(var + EPS) * g + b


def _sigmoid(x):
    return 1.0 / (1.0 + jnp.exp(-x))


def _gelu(x):
    return x * (0.5 * (1.0 + jnp.tanh(math.sqrt(2.0 / math.pi) * (x + 0.044715 * (x * x * x)))))


def _f_rms(x, g):
    return (_rms(x, g),)


def _f_x_rms(x, g):
    return (x, _rms(x, g))


def _f_ln_silu(z, g, b):
    y = _ln(z, g, b)
    return (y * _sigmoid(y),)


def _f_sg_pre(bu, bv, g, b):
    return (_gelu(bu), _ln(_gelu(bv), g, b))


def _f_merge(g0, g1, g2, ya, yb, yc):
    return (_sigmoid(g0) * ya + _sigmoid(g1) * yb + _sigmoid(g2) * yc,)


def _f_resid_rms(x, t, g_post):
    return (x + _rms(t, g_post),)


def _f_resid_rms_rms(x, t, g_post, g_next):
    x1 = x + _rms(t, g_post)
    return (x1, _rms(x1, g_next))


def _f_geglu(zg, zv):
    return _gelu(zg) * zv


_CONV_TILE_ELEMS = 16 * 1024


def _conv_tr(c):
    return _CONV_TILE_ELEMS // c


def _conv_tile(zp_ref, w_ref, bias, k_taps, off, r0):
    c = zp_ref.shape[1]
    tr = _conv_tr(c)
    acc = jnp.broadcast_to(bias, (tr, c))
    for k in range(k_taps):
        acc = acc + w_ref[k:k + 1, :] * zp_ref[r0 + off + k:r0 + off + k + tr, :]
    return acc


def _conv_bwd_input_tile(dzp_ref, w_ref, k_taps, r0):
    c = dzp_ref.shape[1]
    tr = _conv_tr(c)
    acc = jnp.zeros((tr, c), F32)
    for k in range(k_taps):
        s0 = r0 + (k_taps - 1) - k
        acc = acc + w_ref[k:k + 1, :] * dzp_ref[s0:s0 + tr, :]
    return acc


def _conv_bwd_weight(dzp_ref, zp_ref, dw_ref, db_ref, k_taps, off, s):
    c = zp_ref.shape[1]
    tr = _conv_tr(c)
    fold = lambda v: jnp.sum(v.reshape(tr // 8, 8, c), axis=0)
    for k in range(k_taps):
        acc = jnp.zeros((8, c), F32)
        for r in range(s // tr):
            r0 = r * tr
            acc = acc + fold(dzp_ref[r0:r0 + tr, :] * zp_ref[r0 + off + k:r0 + off + k + tr, :])
        dw_ref[k:k + 1, :] = jnp.sum(acc, axis=0, keepdims=True)
    acc = jnp.zeros((8, c), F32)
    for r in range(s // tr):
        acc = acc + fold(dzp_ref[r * tr:(r + 1) * tr, :])
    db_ref[...] = jnp.sum(acc, axis=0, keepdims=True)


def _glu_conv_fwd(p, blk0, w, b):
    s = p.shape[0]
    k_taps, c = w.shape
    cb, pad = _LANES, 32
    off = pad - (k_taps - 1)

    def body(a_ref, w_ref, b_ref, o_ref, zp_ref):
        zp_ref[0:pad, :] = jnp.zeros((pad, cb), F32)
        zp_ref[pad:pad + s, :] = a_ref[:, 0:cb] * _sigmoid(a_ref[:, cb:2 * cb])
        tr = _conv_tr(cb)
        for r in range(s // tr):
            o_ref[r * tr:(r + 1) * tr, :] = _conv_tile(zp_ref, w_ref, b_ref[...], k_taps, off, r * tr)

    return pl.pallas_call(
        body, name="glu_conv_fwd", grid=(c // cb,),
        in_specs=[pl.BlockSpec((s, 2 * cb), lambda j: (0, blk0 + j)),
                  pl.BlockSpec((k_taps, cb), lambda j: (0, j)), pl.BlockSpec((1, cb), lambda j: (0, j))],
        out_specs=pl.BlockSpec((s, cb), lambda j: (0, j)),
        out_shape=jax.ShapeDtypeStruct((s, c), F32),
        scratch_shapes=[pltpu.VMEM((s + pad, cb), F32)],
        compiler_params=_cparams(("parallel",)),
    )(p, w, b)


def _glu_conv_bwd(p, blk0, w, dz, dp):
    s = p.shape[0]
    k_taps, c = w.shape
    cb, pad = _LANES, 32
    off = pad - (k_taps - 1)

    def body(a_ref, w_ref, dz_ref, dp_in, da_ref, dw_ref, db_ref, zp_ref, dzp_ref):
        zp_ref[0:pad, :] = jnp.zeros((pad, cb), F32)
        zp_ref[pad:pad + s, :] = a_ref[:, 0:cb] * _sigmoid(a_ref[:, cb:2 * cb])
        dzp_ref[0:s, :] = dz_ref[...]
        dzp_ref[s:s + pad, :] = jnp.zeros((pad, cb), F32)
        tr = _conv_tr(cb)
        for r in range(s // tr):
            rows = slice(r * tr, (r + 1) * tr)
            dz0 = _conv_bwd_input_tile(dzp_ref, w_ref, k_taps, r * tr)
            sg = _sigmoid(a_ref[rows, cb:2 * cb])
            da_ref[rows, 0:cb] = (dz0 * sg).astype(da_ref.dtype)
            da_ref[rows, cb:2 * cb] = (dz0 * a_ref[rows, 0:cb] * sg * (1.0 - sg)).astype(da_ref.dtype)
        _conv_bwd_weight(dzp_ref, zp_ref, dw_ref, db_ref, k_taps, off, s)

    return pl.pallas_call(
        body, name="glu_conv_bwd", grid=(c // cb,),
        in_specs=[pl.BlockSpec((s, 2 * cb), lambda j: (0, blk0 + j)),
                  pl.BlockSpec((k_taps, cb), lambda j: (0, j)), pl.BlockSpec((s, cb), lambda j: (0, j)), _ANY],
        out_specs=[pl.BlockSpec((s, 2 * cb), lambda j: (0, blk0 + j)),
                   pl.BlockSpec((k_taps, cb), lambda j: (0, j)), pl.BlockSpec((1, cb), lambda j: (0, j))],
        out_shape=[jax.ShapeDtypeStruct(dp.shape, dp.dtype),
                   jax.ShapeDtypeStruct((k_taps, c), F32), jax.ShapeDtypeStruct((1, c), F32)],
        scratch_shapes=[pltpu.VMEM((s + pad, cb), F32), pltpu.VMEM((s + pad, cb), F32)],
        input_output_aliases={3: 0},
        compiler_params=_cparams(("parallel",)),
    )(p, w, dz, dp)


def _conv_geglu_fwd(up, w, b):
    s, f2 = up.shape
    f = f2 // 2
    k_taps = w.shape[0]
    cb, pad = _FFN_CB, 8
    off = pad - (k_taps - 1)
    nb = f // cb

    def body(ug_ref, uv_ref, wg_ref, wv_ref, bg_ref, bv_ref, o_ref, z_ref, w_ref, b_ref):
        _pair(w_ref, wg_ref[...], wv_ref[...], cb)
        _pair(b_ref, bg_ref[...], bv_ref[...], cb)
        z_ref[0:pad, :] = jnp.zeros((pad, 2 * cb), F32)
        z_ref[pad:pad + s, 0:cb] = ug_ref[...]
        z_ref[pad:pad + s, cb:2 * cb] = uv_ref[...]
        tr = _conv_tr(2 * cb)
        for r in range(s // tr):
            z = _conv_tile(z_ref, w_ref, b_ref[...], k_taps, off, r * tr)
            o_ref[r * tr:(r + 1) * tr, :] = _f_geglu(z[:, 0:cb], z[:, cb:2 * cb]).astype(o_ref.dtype)

    two = lambda rows_: [pl.BlockSpec((rows_, cb), lambda j: (0, j)), pl.BlockSpec((rows_, cb), lambda j: (0, nb + j))]
    return pl.pallas_call(
        body, name="conv_geglu_fwd", grid=(nb,),
        in_specs=two(s) + two(k_taps) + two(1),
        out_specs=pl.BlockSpec((s, cb), lambda j: (0, j)),
        out_shape=jax.ShapeDtypeStruct((s, f), _MXU_DT),
        scratch_shapes=[pltpu.VMEM((s + pad, 2 * cb), F32), pltpu.VMEM((k_taps, 2 * cb), F32),
                        pltpu.VMEM((1, 2 * cb), F32)],
        compiler_params=_cparams(("parallel",)),
    )(up, up, w, w, b, b)


def _pair(dst_ref, first, second, cb):
    dst_ref[:, 0:cb] = first
    dst_ref[:, cb:2 * cb] = second


def _conv_geglu_bwd(up, w, b, dact):
    s, f2 = up.shape
    f = f2 // 2
    k_taps = w.shape[0]
    cb, pad = _FFN_CB, 8
    off = pad - (k_taps - 1)
    nb = f // cb

    def body(ug_ref, uv_ref, wg_ref, wv_ref, bg_ref, bv_ref, da_ref, du_ref, dw_ref, db_ref, z_ref, dz_ref, w_ref,
             b_ref, dw_sc, db_sc):
        _pair(w_ref, wg_ref[...], wv_ref[...], cb)
        _pair(b_ref, bg_ref[...], bv_ref[...], cb)
        z_ref[0:pad, :] = jnp.zeros((pad, 2 * cb), F32)
        z_ref[pad:pad + s, 0:cb] = ug_ref[...]
        z_ref[pad:pad + s, cb:2 * cb] = uv_ref[...]
        dz_ref[s:s + pad, :] = jnp.zeros((pad, 2 * cb), F32)
        tr = _conv_tr(2 * cb)
        for r in range(s // tr):
            rows = slice(r * tr, (r + 1) * tr)
            z = _conv_tile(z_ref, w_ref, b_ref[...], k_taps, off, r * tr)
            _, vjp = jax.vjp(_f_geglu, z[:, 0:cb], z[:, cb:2 * cb])
            dzg, dzv = vjp(da_ref[rows, :].astype(F32))
            dz_ref[rows, 0:cb] = dzg
            dz_ref[rows, cb:2 * cb] = dzv
        for r in range(s // tr):
            rows = slice(r * tr, (r + 1) * tr)
            du = _conv_bwd_input_tile(dz_ref, w_ref, k_taps, r * tr).astype(du_ref.dtype)
            du_ref[0, rows, :] = du[:, 0:cb]
            du_ref[1, rows, :] = du[:, cb:2 * cb]
        _conv_bwd_weight(dz_ref, z_ref, dw_sc, db_sc, k_taps, off, s)
        for half in range(2):
            dw_ref[half] = dw_sc[:, half * cb:(half + 1) * cb]
            db_ref[half] = db_sc[:, half * cb:(half + 1) * cb]

    two = lambda rows_: [pl.BlockSpec((rows_, cb), lambda j: (0, j)), pl.BlockSpec((rows_, cb), lambda j: (0, nb + j))]
    both = lambda rows_: pl.BlockSpec((2, rows_, cb), lambda j: (0, 0, j))
    return pl.pallas_call(
        body, name="conv_geglu_bwd", grid=(nb,),
        in_specs=two(s) + two(k_taps) + two(1) + [pl.BlockSpec((s, cb), lambda j: (0, j))],
        out_specs=[both(s), both(k_taps), both(1)],
        out_shape=[jax.ShapeDtypeStruct((2, s, f), _MXU_DT), jax.ShapeDtypeStruct((2, k_taps, f), F32),
                   jax.ShapeDtypeStruct((2, 1, f), F32)],
        scratch_shapes=[pltpu.VMEM((s + pad, 2 * cb), F32), pltpu.VMEM((s + pad, 2 * cb), F32),
                        pltpu.VMEM((k_taps, 2 * cb), F32), pltpu.VMEM((1, 2 * cb), F32),
                        pltpu.VMEM((k_taps, 2 * cb), F32), pltpu.VMEM((1, 2 * cb), F32)],
        compiler_params=_cparams(("parallel",)),
    )(up, up, w, w, b, b, dact)


def _tril_mask():
    t = lax.broadcasted_iota(jnp.int32, (SG_CHUNK, SG_CHUNK), 0)
    s = lax.broadcasted_iota(jnp.int32, (SG_CHUNK, SG_CHUNK), 1)
    return t >= s


def _sg_mix_fwd(u, vn, w, bcol):
    s, c = u.shape
    gw = c // SG_GROUPS

    def body(u_ref, v_ref, w_ref, b_ref, o_ref):
        wm = jnp.where(_tril_mask(), w_ref[0], 0.0).astype(_MXU_DT)
        for n in range(s // SG_CHUNK):
            rows = slice(n * SG_CHUNK, (n + 1) * SG_CHUNK)
            mixed = jnp.dot(wm, v_ref[rows, :], preferred_element_type=F32) + b_ref[0]
            o_ref[rows, :] = (u_ref[rows, :] * mixed).astype(o_ref.dtype)

    return pl.pallas_call(
        body, name="sg_mix_fwd", grid=(SG_GROUPS,),
        in_specs=[pl.BlockSpec((s, gw), lambda g: (0, g)), pl.BlockSpec((s, gw), lambda g: (0, g)),
                  pl.BlockSpec((1, SG_CHUNK, SG_CHUNK), lambda g: (g, 0, 0)),
                  pl.BlockSpec((1, SG_CHUNK, 1), lambda g: (g, 0, 0))],
        out_specs=pl.BlockSpec((s, gw), lambda g: (0, g)),
        out_shape=jax.ShapeDtypeStruct((s, c), _MXU_DT),
        compiler_params=_cparams(("parallel",)),
    )(u, vn, w, bcol)


def _sg_mix_bwd(u, vn, w, bcol, dub):
    s, c = u.shape
    gw = c // SG_GROUPS

    def body(u_ref, v_ref, w_ref, b_ref, d_ref, du_ref, dv_ref, dw_ref, db_ref):
        mask = _tril_mask()
        wm = jnp.where(mask, w_ref[0], 0.0).astype(_MXU_DT)
        dw = jnp.zeros((SG_CHUNK, SG_CHUNK), F32)
        db = jnp.zeros((SG_CHUNK, 1), F32)
        for n in range(s // SG_CHUNK):
            rows = slice(n * SG_CHUNK, (n + 1) * SG_CHUNK)
            v = v_ref[rows, :]
            d = d_ref[rows, :].astype(F32)
            mixed = jnp.dot(wm, v, preferred_element_type=F32) + b_ref[0]
            du_ref[rows, :] = d * mixed
            dmix = d * u_ref[rows, :]
            dmix_lo = dmix.astype(_MXU_DT)
            dv_ref[rows, :] = lax.dot_general(wm, dmix_lo, (((0,), (0,)), ((), ())), preferred_element_type=F32)
            dw = dw + lax.dot_general(dmix_lo, v, (((1,), (1,)), ((), ())), preferred_element_type=F32)
            db = db + jnp.sum(dmix, axis=1, keepdims=True)
        dw_ref[0] = jnp.where(mask, dw, 0.0)
        db_ref[0] = db

    return pl.pallas_call(
        body, name="sg_mix_bwd", grid=(SG_GROUPS,),
        in_specs=[pl.BlockSpec((s, gw), lambda g: (0, g)), pl.BlockSpec((s, gw), lambda g: (0, g)),
                  pl.BlockSpec((1, SG_CHUNK, SG_CHUNK), lambda g: (g, 0, 0)),
                  pl.BlockSpec((1, SG_CHUNK, 1), lambda g: (g, 0, 0)), pl.BlockSpec((s, gw), lambda g: (0, g))],
        out_specs=[pl.BlockSpec((s, gw), lambda g: (0, g)), pl.BlockSpec((s, gw), lambda g: (0, g)),
                   pl.BlockSpec((1, SG_CHUNK, SG_CHUNK), lambda g: (g, 0, 0)),
                   pl.BlockSpec((1, SG_CHUNK, 1), lambda g: (g, 0, 0))],
        out_shape=[jax.ShapeDtypeStruct((s, c), F32), jax.ShapeDtypeStruct((s, c), F32),
                   jax.ShapeDtypeStruct((SG_GROUPS, SG_CHUNK, SG_CHUNK), F32),
                   jax.ShapeDtypeStruct((SG_GROUPS, SG_CHUNK, 1), F32)],
        compiler_params=_cparams(("parallel",)),
    )(u, vn, w, bcol, dub)


def _rope_fwd(q2, kv2, p, krm_idx, krs_idx, tc, ts):
    s = q2.shape[0]
    hw = N_HEADS * HEAD_PAD
    tm = 256

    def body(qm_ref, qs_ref, kn_ref, v_ref, krm_ref, krs_ref, tc_ref, ts_ref, q_ref, k_ref, vo_ref):
        tcv, tsv = tc_ref[...], ts_ref[...]
        kpe = krm_ref[...] * tcv + krs_ref[...] * tsv
        for h in range(N_HEADS):
            cols = slice(h * HEAD_PAD, (h + 1) * HEAD_PAD)
            q_ref[:, cols] = (qm_ref[:, cols] * tcv + qs_ref[:, cols] * tsv).astype(q_ref.dtype)
            k_ref[:, cols] = (kn_ref[:, cols] + kpe).astype(k_ref.dtype)
        vo_ref[...] = v_ref[...].astype(vo_ref.dtype)

    return pl.pallas_call(
        body, name="rope_fwd", grid=(s // tm,),
        in_specs=[_row_spec(tm, hw, 0), _row_spec(tm, hw, 1), _row_spec(tm, hw, 0), _row_spec(tm, hw, 1),
                  _row_spec(tm, HEAD_PAD, krm_idx), _row_spec(tm, HEAD_PAD, krs_idx),
                  _row_spec(tm, HEAD_PAD, 0), _row_spec(tm, HEAD_PAD, 0)],
        out_specs=[_row_spec(tm, hw, 0)] * 3,
        out_shape=[jax.ShapeDtypeStruct((s, hw), _MXU_DT)] * 3,
        compiler_params=_cparams(("parallel",)),
    )(q2, q2, kv2, kv2, p, p, tc, ts)


def _rope_bwd(dq, dk, dv, tc, ts, dp, kr_blk):
    s = dq.shape[0]
    hw = N_HEADS * HEAD_PAD
    tm = 256

    def body(dq_ref, dk_ref, dv_ref, tc_ref, ts_ref, dp_in, dq2_ref, dkv2_ref, dkr_ref):
        tcv, tsv = tc_ref[...], ts_ref[...]
        dkpe = jnp.zeros((tm, HEAD_PAD), F32)
        for h in range(N_HEADS):
            cols = slice(h * HEAD_PAD, (h + 1) * HEAD_PAD)
            dqh = dq_ref[:, cols]
            dq2_ref[:, cols] = (dqh * tcv).astype(dq2_ref.dtype)
            dq2_ref[:, hw + h * HEAD_PAD:hw + (h + 1) * HEAD_PAD] = (dqh * tsv).astype(dq2_ref.dtype)
            dkpe = dkpe + dk_ref[:, cols]
        dkv2_ref[:, 0:hw] = dk_ref[...].astype(dkv2_ref.dtype)
        dkv2_ref[:, hw:2 * hw] = dv_ref[...].astype(dkv2_ref.dtype)
        dkr_ref[:, 0:HEAD_PAD] = (dkpe * tcv).astype(dkr_ref.dtype)
        dkr_ref[:, HEAD_PAD:2 * HEAD_PAD] = (dkpe * tsv).astype(dkr_ref.dtype)
        dkr_ref[:, 2 * HEAD_PAD:3 * HEAD_PAD] = jnp.zeros((tm, HEAD_PAD), dkr_ref.dtype)

    return pl.pallas_call(
        body, name="rope_bwd", grid=(s // tm,),
        in_specs=[_row_spec(tm, hw, 0)] * 3 + [_row_spec(tm, HEAD_PAD, 0)] * 2 + [_ANY],
        out_specs=[_row_spec(tm, 2 * hw, 0), _row_spec(tm, 2 * hw, 0), _row_spec(tm, 3 * HEAD_PAD, kr_blk)],
        out_shape=[jax.ShapeDtypeStruct((s, 2 * hw), _MXU_DT), jax.ShapeDtypeStruct((s, 2 * hw), _MXU_DT),
                   jax.ShapeDtypeStruct(dp.shape, dp.dtype)],
        input_output_aliases={5: 2},
        compiler_params=_cparams(("parallel",)),
    )(dq, dk, dv, tc, ts, dp)


_ATTN_TQ = 256
_ATTN_SCALE = (QK_NOPE + QK_ROPE) ** -0.5


def _attn_probs(q, k, i):
    s = k.shape[0]
    sc = lax.dot_general(q, k, (((1,), (1,)), ((), ())), preferred_element_type=F32) * _ATTN_SCALE
    row = i * _ATTN_TQ + lax.broadcasted_iota(jnp.int32, (_ATTN_TQ, s), 0)
    col = lax.broadcasted_iota(jnp.int32, (_ATTN_TQ, s), 1)
    sc = jnp.where(row >= col, sc, jnp.finfo(F32).min)
    e = jnp.exp(sc - jnp.max(sc, axis=1, keepdims=True))
    return e / jnp.sum(e, axis=1, keepdims=True)


def _per_query_block(s, fn):
    i = pl.program_id(1)
    for n in range(s // _ATTN_TQ):
        @pl.when(i == n)
        def _(n=n):
            fn(n, (n + 1) * _ATTN_TQ)


def _attn_fwd(q, k, v):
    s = q.shape[0]

    def body(q_ref, k_ref, v_ref, o_ref):
        def block(n, kl):
            p = _attn_probs(q_ref[...], k_ref[0:kl, :], n)
            o_ref[...] = jnp.dot(p.astype(_MXU_DT), v_ref[0:kl, :], preferred_element_type=F32).astype(o_ref.dtype)

        _per_query_block(s, block)

    qspec = pl.BlockSpec((_ATTN_TQ, HEAD_PAD), lambda h, i: (i, h))
    kspec = pl.BlockSpec((s, HEAD_PAD), lambda h, i: (0, h))
    return pl.pallas_call(
        body, name="attn_fwd", grid=(N_HEADS, s // _ATTN_TQ),
        in_specs=[qspec, kspec, kspec], out_specs=qspec,
        out_shape=jax.ShapeDtypeStruct(q.shape, _MXU_DT),
        compiler_params=_cparams(("parallel", "parallel")),
    )(q, k, v)


def _attn_bwd(q, k, v, do):
    s = q.shape[0]

    def body(q_ref, k_ref, v_ref, do_ref, dq_ref, dk_ref, dv_ref):
        i = pl.program_id(1)

        @pl.when(i == 0)
        def _():
            dk_ref[...] = jnp.zeros_like(dk_ref)
            dv_ref[...] = jnp.zeros_like(dv_ref)

        def block(n, kl):
            qv, kv, dov = q_ref[...], k_ref[0:kl, :], do_ref[...]
            p = _attn_probs(qv, kv, n)
            dp = lax.dot_general(dov, v_ref[0:kl, :], (((1,), (1,)), ((), ())), preferred_element_type=F32)
            delta = jnp.sum(p * dp, axis=1, keepdims=True)
            ds = (p * (dp - delta) * _ATTN_SCALE).astype(_MXU_DT)
            dq_ref[...] = jnp.dot(ds, kv, preferred_element_type=F32)
            dk_ref[0:kl, :] += lax.dot_general(ds, qv, (((0,), (0,)), ((), ())), preferred_element_type=F32)
            dv_ref[0:kl, :] += lax.dot_general(p.astype(_MXU_DT), dov, (((0,), (0,)), ((), ())),
                                               preferred_element_type=F32)

        _per_query_block(s, block)

    qspec = pl.BlockSpec((_ATTN_TQ, HEAD_PAD), lambda h, i: (i, h))
    kspec = pl.BlockSpec((s, HEAD_PAD), lambda h, i: (0, h))
    return pl.pallas_call(
        body, name="attn_bwd", grid=(N_HEADS, s // _ATTN_TQ),
        in_specs=[qspec, kspec, kspec, qspec], out_specs=[qspec, kspec, kspec],
        out_shape=[jax.ShapeDtypeStruct(q.shape, F32)] * 3,
        compiler_params=_cparams(("parallel", "arbitrary")),
    )(q, k, v, do)


def _loss_head(y, target):
    s, d = y.shape
    tm = 256

    def body(y_ref, t_ref, loss_ref, dy_ref):
        @pl.when(pl.program_id(0) == 0)
        def _():
            loss_ref[...] = jnp.zeros_like(loss_ref)

        err = y_ref[...] - t_ref[...]
        loss_ref[...] += 0.5 * jnp.sum(jnp.mean(err * err, axis=-1, keepdims=True), axis=0, keepdims=True)
        dy_ref[...] = err * (1.0 / d)

    return pl.pallas_call(
        body, name="loss_head", grid=(s // tm,),
        in_specs=[_row_spec(tm, d, 0), _row_spec(tm, d, 0)],
        out_specs=[_full_spec((1, 1)), _row_spec(tm, d, 0)],
        out_shape=[jax.ShapeDtypeStruct((1, 1), F32), jax.ShapeDtypeStruct((s, d), F32)],
        compiler_params=_cparams(("arbitrary",)),
    )(y, target)


def _adamw_math(w, g, m, v):
    mn = ADAM_B1 * m + (1.0 - ADAM_B1) * g
    vn = ADAM_B2 * v + (1.0 - ADAM_B2) * (g * g)
    m_hat = mn / (1.0 - ADAM_B1 ** ADAM_STEP)
    v_hat = vn / (1.0 - ADAM_B2 ** ADAM_STEP)
    return -ADAM_LR * (m_hat / (jnp.sqrt(v_hat) + ADAM_EPS) + ADAM_WD * w), mn, vn


def _adamw(name, w, g, m, v):
    l, k, n = w.shape
    tk, tn = _slab_block(k, n)

    def body(w_ref, g_ref, m_ref, v_ref, d_ref, mo_ref, vo_ref):
        d_ref[...], mo_ref[...], vo_ref[...] = _adamw_math(w_ref[...], g_ref[...], m_ref[...], v_ref[...])

    spec = pl.BlockSpec((1, tk, tn), lambda i, j, jn: (i, j, jn))
    return pl.pallas_call(
        body, name=name, grid=(l, k // tk, n // tn), in_specs=[spec] * 4, out_specs=[spec] * 3,
        out_shape=[jax.ShapeDtypeStruct(w.shape, F32)] * 3,
        compiler_params=_cparams(("parallel", "parallel", "parallel")),
    )(w, g, m, v)


_ANY = pl.BlockSpec(memory_space=pl.ANY)


def _mesh_pos():
    return lax.axis_index("x"), lax.axis_index("y"), lax.axis_index("c")


def _other_chips(x, y):
    return [(1 - x, y), (x, 1 - y), (1 - x, 1 - y)]


def _remote(src, dst, send_sem, recv_sem, to):
    return pltpu.make_async_remote_copy(src_ref=src, dst_ref=dst, send_sem=send_sem, recv_sem=recv_sem,
                                        device_id=to, device_id_type=_MESH)


def _all_gather_xy(shards):
    n = len(shards)

    def body(*refs):
        ins, outs, (send_sems, recv_sems) = refs[:n], refs[n:2 * n], refs[2 * n:]
        x, y, c = _mesh_pos()
        sibling = (x, y, 1 - c)
        chips = _other_chips(x, y)
        idx = lambda chip: 2 * chip[0] + chip[1]

        def copy(a, k, src, dst, to):
            return _remote(src, dst, send_sems.at[a, k], recv_sems.at[a, k], to)

        started = []
        for k, chip in enumerate(chips):
            for a in range(n):
                cp = copy(a, k, ins[a].at[c], outs[a].at[2 * x + y, c], (*chip, c))
                cp.start()
                started.append(cp)
        for k, chip in enumerate(chips):
            for a in range(n):
                landed = outs[a].at[idx(chip), c]
                copy(a, k, ins[a].at[c], landed, (*chip, c)).wait_recv()
                cp = copy(a, 3 + k, landed, landed, sibling)
                cp.start()
                started.append(cp)
        for a in range(n):
            cp = copy(a, 6, ins[a], outs[a].at[2 * x + y], sibling)
            cp.start()
            started.append(cp)
        for k, chip in enumerate(chips):
            for a in range(n):
                copy(a, 3 + k, ins[a].at[c], outs[a].at[idx(chip), 1 - c], sibling).wait_recv()
        for a in range(n):
            copy(a, 6, ins[a], outs[a].at[2 * x + y], sibling).wait_recv()
        for cp in started:
            cp.wait_send()

    return pl.pallas_call(
        body, name="all_gather_xy", in_specs=[_ANY] * n, out_specs=[_ANY] * n,
        out_shape=[jax.ShapeDtypeStruct((4,) + a.shape, a.dtype) for a in shards],
        scratch_shapes=[pltpu.SemaphoreType.DMA((n, 7)), pltpu.SemaphoreType.DMA((n, 7))],
    )(*shards)


def _swap_layers(gs):
    n = len(gs)

    def body(*refs):
        ins, outs, (send_sems, recv_sems) = refs[:n], refs[n:2 * n], refs[2 * n:]
        x, y, c = _mesh_pos()
        copies = [_remote(ins[a].at[:, 1 - c], outs[a], send_sems.at[a], recv_sems.at[a], (x, y, 1 - c))
                  for a in range(n)]
        for cp in copies:
            cp.start()
        for cp in copies:
            cp.wait()

    return pl.pallas_call(
        body, name="rs_swap_layers", in_specs=[_ANY] * n, out_specs=[_ANY] * n,
        out_shape=[jax.ShapeDtypeStruct((4,) + g.shape[2:], g.dtype) for g in gs],
        scratch_shapes=[pltpu.SemaphoreType.DMA((n,)), pltpu.SemaphoreType.DMA((n,))],
    )(*gs)


def _scatter_chips(ts):
    n = len(ts)

    def body(*refs):
        ins, outs, (send_sems, recv_sems) = refs[:n], refs[n:2 * n], refs[2 * n:]
        x, y, c = _mesh_pos()
        copies = []
        for k, chip in enumerate(_other_chips(x, y)):
            for a in range(n):
                cp = _remote(ins[a].at[2 * chip[0] + chip[1]], outs[a].at[k], send_sems.at[a, k], recv_sems.at[a, k],
                             (*chip, c))
                cp.start()
                copies.append(cp)
        for cp in copies:
            cp.wait()

    return pl.pallas_call(
        body, name="rs_scatter_chips", in_specs=[_ANY] * n, out_specs=[_ANY] * n,
        out_shape=[jax.ShapeDtypeStruct((3,) + t.shape[1:], t.dtype) for t in ts],
        scratch_shapes=[pltpu.SemaphoreType.DMA((n, 3)), pltpu.SemaphoreType.DMA((n, 3))],
    )(*ts)


def _join_layers(us):
    n = len(us)

    def body(*refs):
        ins, outs, (send_sems, recv_sems) = refs[:n], refs[n:2 * n], refs[2 * n:]
        x, y, c = _mesh_pos()
        copies = [_remote(ins[a].at[c], outs[a].at[c], send_sems.at[a], recv_sems.at[a], (x, y, 1 - c))
                  for a in range(n)]
        for cp in copies:
            cp.start()
        for a, cp in enumerate(copies):
            cp.wait_send()
            _remote(ins[a].at[c], outs[a].at[1 - c], send_sems.at[a], recv_sems.at[a], (x, y, 1 - c)).wait_recv()

    return pl.pallas_call(
        body, name="rs_join_layers", in_specs=[_ANY] * n, out_specs=[_ANY] * n,
        out_shape=[jax.ShapeDtypeStruct(u.shape, u.dtype) for u in us],
        input_output_aliases={a: a for a in range(n)},
        scratch_shapes=[pltpu.SemaphoreType.DMA((n,)), pltpu.SemaphoreType.DMA((n,))],
    )(*us)


_HBM = pl.BlockSpec(memory_space=pltpu.HBM)
_SEM = pl.BlockSpec(memory_space=pltpu.SEMAPHORE)
_EFFECT = pltpu.SideEffectType.DATAFLOW_SIDE_EFFECTING


def _in_hbm(a):
    return pltpu.with_memory_space_constraint(a, pltpu.HBM)


def _chip_index(chip):
    return 2 * chip[0] + chip[1]


def _ag_forward(shards, lands, layer, have_remote):
    n = len(shards)

    def body(*refs):
        ins = refs[:n]
        outs = refs[2 * n:3 * n] if lands is not None else refs[n:2 * n]
        send_sems, recv_sems = refs[-2:]
        x, y, c = _mesh_pos()
        sibling = (x, y, 1 - c)
        chips = _other_chips(x, y)

        def copy(a, k, src, dst, to):
            return _remote(src, dst, send_sems.at[a, k], recv_sems.at[a, k], to)

        own = [copy(a, 6, ins[a], outs[a].at[2 * x + y], sibling) for a in range(n)]
        for cp in own:
            cp.start()

        @pl.when(c == layer)
        def _():
            started = []
            if not have_remote:
                for k, chip in enumerate(chips):
                    for a in range(n):
                        cp = copy(a, k, ins[a], outs[a].at[2 * x + y], (*chip, c))
                        cp.start()
                        started.append(cp)
            for k, chip in enumerate(chips):
                for a in range(n):
                    landed = outs[a].at[_chip_index(chip)]
                    if not have_remote:
                        copy(a, k, ins[a], landed, (*chip, c)).wait_recv()
                    cp = copy(a, 3 + k, landed, landed, sibling)
                    cp.start()
                    started.append(cp)
            for cp in started:
                cp.wait_send()

        @pl.when(c != layer)
        def _():
            for k, chip in enumerate(chips):
                for a in range(n):
                    copy(a, 3 + k, ins[a], outs[a].at[_chip_index(chip)], sibling).wait_recv()

        for cp in own:
            cp.wait()

    out_shape = [jax.ShapeDtypeStruct((4,) + a.shape, a.dtype) for a in shards]
    extra = [] if lands is None else list(lands)
    return pl.pallas_call(
        body, name="ag_layer%d" % layer, in_specs=[_ANY] * (n + len(extra)), out_specs=[_ANY] * n,
        out_shape=out_shape, input_output_aliases={n + a: a for a in range(len(extra))},
        scratch_shapes=[pltpu.SemaphoreType.DMA((n, 7)), pltpu.SemaphoreType.DMA((n, 7))],
    )(*shards, *extra)


def _owner_sends(layer, srcs, dsts, send_sems, recv_sems, do):
    x, y, c = _mesh_pos()

    @pl.when(c == layer)
    def _():
        for k, chip in enumerate(_other_chips(x, y)):
            for a in range(len(srcs)):
                do(_remote(srcs[a](chip, k), dsts[a](chip, k), send_sems.at[3 * a + k], recv_sems.at[3 * a + k],
                           (*chip, c)))


def _split_start(name, layer, sources, land_shapes, src_of, dst_of):
    n = len(sources)

    def body(*refs):
        srcs, lands = refs[:n], refs[n:2 * n]
        send_sems, recv_sems = refs[2 * n], refs[2 * n + 1]
        token = refs[-1]
        _owner_sends(layer, [functools.partial(src_of, srcs[a]) for a in range(n)],
                     [functools.partial(dst_of, lands[a]) for a in range(n)], send_sems, recv_sems,
                     lambda cp: cp.start())
        token[...] = jnp.zeros_like(token)

    lands = [_in_hbm(lax.empty(s.shape, s.dtype)) for s in land_shapes]
    outs = pl.pallas_call(
        body, name=name,
        out_shape=([pltpu.SemaphoreType.DMA((3 * n,)), pltpu.SemaphoreType.DMA((3 * n,))]
                   + [pltpu.HBM(a.shape, a.dtype) for a in sources] + [pltpu.HBM(s.shape, s.dtype) for s in land_shapes]
                   + [jax.ShapeDtypeStruct((8, _LANES), F32)]),
        in_specs=[_HBM] * (2 * n),
        out_specs=[_SEM, _SEM] + [_HBM] * (2 * n) + [pl.BlockSpec(memory_space=pltpu.VMEM)],
        input_output_aliases={i: 2 + i for i in range(2 * n)},
        compiler_params=pltpu.CompilerParams(has_side_effects=_EFFECT),
    )(*[_in_hbm(a) for a in sources], *lands)
    return outs[0], outs[1], outs[2:2 + n], outs[2 + n:2 + 2 * n], outs[-1]


def _split_wait(name, layer, send_sems, recv_sems, sources, lands, after, src_of, dst_of):
    n = len(sources)

    def body(*refs):
        srcs, lnds = refs[:n], refs[n:2 * n]
        s_sems, r_sems = refs[2 * n], refs[2 * n + 1]

        def wait(cp):
            cp.wait_send()
            cp.wait_recv()

        _owner_sends(layer, [functools.partial(src_of, srcs[a]) for a in range(n)],
                     [functools.partial(dst_of, lnds[a]) for a in range(n)], s_sems, r_sems, wait)

    outs = pl.pallas_call(
        body, name=name,
        out_shape=[pltpu.HBM(a.shape, a.dtype) for a in sources] + [pltpu.HBM(a.shape, a.dtype) for a in lands],
        in_specs=[_HBM] * (2 * n) + [_SEM, _SEM, _ANY], out_specs=[_HBM] * (2 * n),
        input_output_aliases={i: i for i in range(2 * n)},
        compiler_params=pltpu.CompilerParams(has_side_effects=_EFFECT),
    )(*sources, *lands, send_sems, recv_sems, after)
    return outs[:n], outs[n:]


def _rs_swap(gds, layer):
    n = len(gds)

    def body(*refs):
        ins, outs, (send_sems, recv_sems) = refs[:n], refs[n:2 * n], refs[2 * n:]
        x, y, c = _mesh_pos()
        copies = [_remote(ins[a], outs[a], send_sems.at[a], recv_sems.at[a], (x, y, 1 - c)) for a in range(n)]

        @pl.when(c != layer)
        def _():
            for cp in copies:
                cp.start()
            for cp in copies:
                cp.wait_send()

        @pl.when(c == layer)
        def _():
            for cp in copies:
                cp.wait_recv()

    return pl.pallas_call(
        body, name="rs_swap%d" % layer, in_specs=[_ANY] * n, out_specs=[_ANY] * n,
        out_shape=[jax.ShapeDtypeStruct(g.shape, g.dtype) for g in gds],
        scratch_shapes=[pltpu.SemaphoreType.DMA((n,)), pltpu.SemaphoreType.DMA((n,))],
    )(*gds)


def _rs_scatter(ts, layer):
    n = len(ts)

    def body(*refs):
        ins, outs, (send_sems, recv_sems) = refs[:n], refs[n:2 * n], refs[2 * n:]
        srcs = [lambda chip, k, r=ins[a]: r.at[_chip_index(chip)] for a in range(n)]
        dsts = [lambda chip, k, r=outs[a]: r.at[k] for a in range(n)]
        _owner_sends(layer, srcs, dsts, send_sems, recv_sems, lambda cp: cp.start())
        _owner_sends(layer, srcs, dsts, send_sems, recv_sems, lambda cp: cp.wait())

    return pl.pallas_call(
        body, name="rs_scatter%d" % layer, in_specs=[_ANY] * n, out_specs=[_ANY] * n,
        out_shape=[jax.ShapeDtypeStruct((3,) + t.shape[1:], t.dtype) for t in ts],
        scratch_shapes=[pltpu.SemaphoreType.DMA((3 * n,)), pltpu.SemaphoreType.DMA((3 * n,))],
    )(*ts)


def _add_pair(name, g, a):
    _, k, n = g.shape
    tk, tn = _slab_block(k, n)

    def body(g_ref, a_ref, o_ref):
        o_ref[...] = (g_ref[...] + a_ref[...]).astype(o_ref.dtype)

    spec = pl.BlockSpec((1, tk, tn), lambda j, i, jn: (j, i, jn))
    return pl.pallas_call(
        body, name=name, grid=(4, k // tk, n // tn), in_specs=[spec, spec], out_specs=spec,
        out_shape=jax.ShapeDtypeStruct(g.shape, _RS_DT),
        compiler_params=_cparams(("parallel", "parallel", "parallel")),
    )(g, a)


def _add_quads(name, t, b, layer, flags, into):
    _, k, n = t.shape
    tk, tn = _slab_block(k, n)

    def body(flags_ref, t_ref, b_ref, *rest):
        o_ref = rest[-1]
        f = lambda v: v.astype(F32)
        o_ref[0] = ((f(t_ref[0]) + f(b_ref[0])) + f(b_ref[1])) + f(b_ref[2])

    extra = [] if into is None else [into]
    return pl.pallas_call(
        body, name=name,
        grid_spec=pltpu.PrefetchScalarGridSpec(
            num_scalar_prefetch=1, grid=(k // tk, n // tn),
            in_specs=[pl.BlockSpec((1, tk, tn), lambda i, jn, fl: (fl[0], i * fl[1], jn * fl[1])),
                      pl.BlockSpec((3, tk, tn), lambda i, jn, fl: (0, i * fl[1], jn * fl[1]))] + [_ANY] * len(extra),
            out_specs=pl.BlockSpec((1, tk, tn), lambda i, jn, fl: (layer, i * fl[1], jn * fl[1]))),
        out_shape=jax.ShapeDtypeStruct((2, k, n), F32),
        input_output_aliases={3: 0} if extra else {},
        compiler_params=_cparams(("arbitrary", "arbitrary")),
    )(flags, t, b, *extra)


def _slab_block(k, n, itemsize=4):
    tk = (1 << 20) // (n * itemsize) // 16 * 16
    while 0 < tk < k and k % tk:
        tk -= 16
    if 0 < tk < k:
        return tk, n
    if k * n * itemsize <= (2 << 20) or n % _LANES:
        return k, n
    tn = max(_LANES, (1 << 20) // (k * itemsize) // _LANES * _LANES)
    while n % tn:
        tn -= _LANES
    return k, tn


def _add_layer(name, g, a, c_arr):
    _, _, k, n = g.shape
    tk, tn = _slab_block(k, n)

    def body(c_ref, g_ref, a_ref, o_ref):
        o_ref[...] = (g_ref[0] + a_ref[...]).astype(o_ref.dtype)

    return pl.pallas_call(
        body, name=name,
        grid_spec=pltpu.PrefetchScalarGridSpec(
            num_scalar_prefetch=1, grid=(4, k // tk, n // tn),
            in_specs=[pl.BlockSpec((1, 1, tk, tn), lambda j, i, jn, c_ref: (j, c_ref[0], i, jn)),
                      pl.BlockSpec((1, tk, tn), lambda j, i, jn, c_ref: (j, i, jn))],
            out_specs=pl.BlockSpec((1, tk, tn), lambda j, i, jn, c_ref: (j, i, jn))),
        out_shape=jax.ShapeDtypeStruct((4, k, n), _RS_DT),
        compiler_params=_cparams(("parallel", "parallel", "parallel")),
    )(c_arr, g, a)


def _add_chips(name, t, b, pos_arr):
    _, k, n = t.shape
    tk, tn = _slab_block(k, n)

    def body(pos_ref, t_ref, b_ref, o_ref):
        f = lambda v: v.astype(F32)
        o_ref[0] = ((f(t_ref[0]) + f(b_ref[0])) + f(b_ref[1])) + f(b_ref[2])

    return pl.pallas_call(
        body, name=name,
        grid_spec=pltpu.PrefetchScalarGridSpec(
            num_scalar_prefetch=1, grid=(k // tk, n // tn),
            in_specs=[pl.BlockSpec((1, tk, tn), lambda i, jn, pos_ref: (pos_ref[0], i, jn)),
                      pl.BlockSpec((3, tk, tn), lambda i, jn, pos_ref: (0, i, jn))],
            out_specs=pl.BlockSpec((1, tk, tn), lambda i, jn, pos_ref: (pos_ref[1], i, jn))),
        out_shape=jax.ShapeDtypeStruct((2, k, n), F32),
        compiler_params=_cparams(("parallel", "parallel")),
    )(pos_arr, t, b)


def _reduce_scatter(names, gs):
    x, y, c = _mesh_pos()
    c_arr = jnp.reshape(c, (1,)).astype(jnp.int32)
    pos_arr = jnp.stack([2 * x + y, c]).astype(jnp.int32)
    sib = _swap_layers(gs)
    ts = [_add_layer("rs_add_layer_" + n, g, a, c_arr) for n, g, a in zip(names, gs, sib)]
    bs = _scatter_chips(ts)
    us = [_add_chips("rs_add_chips_" + n, t, b, pos_arr) for n, t, b in zip(names, ts, bs)]
    return _join_layers(us)


def _all_reduce_adamw(gs, ws, ms, vs):
    n = len(gs)

    def body(*refs):
        g_refs, w_refs, m_refs, v_refs = (refs[i * n:(i + 1) * n] for i in range(4))
        gsum, delta, m_out, v_out = (refs[(4 + i) * n:(5 + i) * n] for i in range(4))
        slots = refs[8 * n:9 * n]
        send_sems, recv_sems = refs[9 * n:]
        x, y, c = _mesh_pos()
        me = 4 * x + 2 * y + c
        copies = []
        for rel in range(1, 8):
            bx, by, bc = (rel >> 2) & 1, (rel >> 1) & 1, rel & 1
            peer = (1 - x if bx else x, 1 - y if by else y, 1 - c if bc else c)
            for a in range(n):
                cp = _remote(g_refs[a], slots[a].at[me], send_sems.at[a, rel - 1], recv_sems.at[a, rel - 1], peer)
                cp.start()
                copies.append(cp)
        for a in range(n):
            slots[a][me] = g_refs[a][...]
        for cp in copies:
            cp.wait()
        for a in range(n):
            acc = slots[a][0]
            for d in range(1, 8):
                acc = acc + slots[a][d]
            gsum[a][...] = acc
            delta[a][...], m_out[a][...], v_out[a][...] = _adamw_math(w_refs[a][...], acc, m_refs[a][...],
                                                                      v_refs[a][...])

    vmem = pl.BlockSpec(memory_space=pltpu.VMEM)
    outs = pl.pallas_call(
        body, name="all_reduce_adamw", in_specs=[vmem] * (4 * n), out_specs=[vmem] * (4 * n),
        out_shape=[jax.ShapeDtypeStruct(g.shape, F32) for g in gs] * 4,
        scratch_shapes=([pltpu.VMEM((8,) + g.shape, F32) for g in gs]
                        + [pltpu.SemaphoreType.DMA((n, 7)), pltpu.SemaphoreType.DMA((n, 7))]),
        compiler_params=pltpu.CompilerParams(vmem_limit_bytes=_VMEM_LIMIT),
    )(*gs, *ws, *ms, *vs)
    return outs[:n], outs[n:2 * n], outs[2 * n:3 * n], outs[3 * n:]


def _swap_rope(a):
    h = QK_ROPE // 2
    return jnp.concatenate([a[..., h:], a[..., :h]], axis=-1)


def _swap_rope_rows(a):
    h = QK_ROPE // 2
    return jnp.concatenate([a[h:], a[:h]], axis=0)


_FFN_CB = 256


def _interleave_rows(a, cb):
    r, c = a.shape
    return a.reshape(2, r // (2 * cb), cb, c).transpose(1, 0, 2, 3).reshape(r, c)


def _deinterleave_rows(a, cb):
    r, c = a.shape
    return a.reshape(r // (2 * cb), 2, cb, c).transpose(1, 0, 2, 3).reshape(r, c)


class _InLayout:
    def __init__(self, d):
        self.d = d
        self.gates = 0
        self.a = 3 * d
        self.b = 4 * d
        self.kv = 5 * d
        self.q = self.kv + 256
        self.krm = self.q + 384
        self.krs = self.krm + HEAD_PAD
        self.width = self.krs + 2 * HEAD_PAD


def _prep_layer(wl, d):
    lay = _InLayout(d)
    w_in = wl["w_in"]
    dt = w_in.dtype
    a, b = w_in[0:d], w_in[d:2 * d]
    q, kv = w_in[2 * d:2 * d + 384], w_in[2 * d + 384:2 * d + 640]
    kr = w_in[2 * d + 640:2 * d + 640 + QK_ROPE]
    gates = w_in[2 * d + 640 + QK_ROPE:]
    z = lambda n: jnp.zeros((n, d), dt)
    krm = jnp.concatenate([z(QK_NOPE), kr, z(HEAD_PAD - QK_NOPE - QK_ROPE)], axis=0)
    krs = jnp.concatenate([z(QK_NOPE), _swap_rope_rows(kr), z(HEAD_PAD - QK_NOPE - QK_ROPE)], axis=0)
    out = dict(wl)
    out["w_in"] = jnp.concatenate([gates, _interleave_rows(a, _LANES), b, kv, q, krm, krs,
                                   z(lay.width - lay.krs - HEAD_PAD)], axis=0)
    uq = wl["mla_w_uq"].reshape(-1, N_HEADS, QK_NOPE + QK_ROPE)
    nq = uq.shape[0]
    nope, pe = uq[..., :QK_NOPE], uq[..., QK_NOPE:]
    zq = lambda n: jnp.zeros((nq, N_HEADS, n), dt)
    main = jnp.concatenate([nope, pe, zq(HEAD_PAD - QK_NOPE - QK_ROPE)], axis=-1).reshape(nq, -1)
    swapped = jnp.concatenate([zq(QK_NOPE), _swap_rope(pe), zq(HEAD_PAD - QK_NOPE - QK_ROPE)], axis=-1).reshape(nq, -1)
    out["mla_w_uq"] = jnp.concatenate([main, swapped], axis=1)
    ukv = wl["mla_w_ukv"].reshape(-1, N_HEADS, QK_NOPE + V_HEAD)
    nkv = ukv.shape[0]
    zk = jnp.zeros((nkv, N_HEADS, HEAD_PAD - QK_NOPE), dt)
    zv = jnp.zeros((nkv, N_HEADS, HEAD_PAD - V_HEAD), dt)
    out["mla_w_ukv"] = jnp.concatenate([jnp.concatenate([ukv[..., :QK_NOPE], zk], axis=-1).reshape(nkv, -1),
                                        jnp.concatenate([ukv[..., QK_NOPE:], zv], axis=-1).reshape(nkv, -1)], axis=1)
    wo = wl["mla_w_o"].reshape(N_HEADS, V_HEAD, -1)
    out["mla_w_o"] = jnp.concatenate([wo, jnp.zeros((N_HEADS, HEAD_PAD - V_HEAD, wo.shape[-1]), dt)],
                                     axis=1).reshape(N_HEADS * HEAD_PAD, -1)
    return out


def _unprep_grads(g, d):
    lay = _InLayout(d)
    gi = g["w_in"]
    kr = (gi[lay.krm + QK_NOPE:lay.krm + QK_NOPE + QK_ROPE]
          + _swap_rope_rows(gi[lay.krs + QK_NOPE:lay.krs + QK_NOPE + QK_ROPE]))
    out = dict(g)
    out["w_in"] = jnp.concatenate([_deinterleave_rows(gi[lay.a:lay.a + d], _LANES), gi[lay.b:lay.b + d],
                                   gi[lay.q:lay.q + 384], gi[lay.kv:lay.kv + 256], kr,
                                   gi[lay.gates:lay.gates + 3 * d]], axis=0)
    hw = N_HEADS * HEAD_PAD
    gq = g["mla_w_uq"]
    nq = gq.shape[0]
    main = gq[:, :hw].reshape(nq, N_HEADS, HEAD_PAD)
    swapped = gq[:, hw:].reshape(nq, N_HEADS, HEAD_PAD)
    pe = main[..., QK_NOPE:QK_NOPE + QK_ROPE] + _swap_rope(swapped[..., QK_NOPE:QK_NOPE + QK_ROPE])
    out["mla_w_uq"] = jnp.concatenate([main[..., :QK_NOPE], pe], axis=-1).reshape(nq, -1)
    gkv = g["mla_w_ukv"]
    nkv = gkv.shape[0]
    out["mla_w_ukv"] = jnp.concatenate([gkv[:, :hw].reshape(nkv, N_HEADS, HEAD_PAD)[..., :QK_NOPE],
                                        gkv[:, hw:].reshape(nkv, N_HEADS, HEAD_PAD)[..., :V_HEAD]],
                                       axis=-1).reshape(nkv, -1)
    go = g["mla_w_o"]
    out["mla_w_o"] = go.reshape(N_HEADS, HEAD_PAD, -1)[:, :V_HEAD].reshape(N_HEADS * V_HEAD, -1)
    return out


def _rope_tables(positions):
    s = positions.shape[0]
    inv = ROPE_THETA ** (-jnp.arange(0, QK_ROPE, 2, dtype=F32) / QK_ROPE)
    ang = positions.astype(F32)[:, None] * inv
    cos, sin = jnp.cos(ang), jnp.sin(ang)
    tail = jnp.zeros((s, HEAD_PAD - QK_NOPE - QK_ROPE), F32)
    tc = jnp.concatenate([jnp.ones((s, QK_NOPE), F32), cos, cos, tail], axis=1)
    ts = jnp.concatenate([jnp.zeros((s, QK_NOPE), F32), -sin, sin, tail], axis=1)
    return tc, ts


def _row(v):
    return v.reshape(1, -1)


def _layer_fwd(x, h, w, g_next, tc, ts):
    d = x.shape[1]
    lay = _InLayout(d)
    cw = d // 2
    blk = lambda off, width: off // width
    p = _mm("mm_in", h, w["w_in"], tb=True)
    z1 = _glu_conv_fwd(p, blk(lay.a, 2 * _LANES), w["conv_dw_w"], _row(w["conv_dw_b"]))
    ln_a = [_row(w["conv_ln_g"]), _row(w["conv_ln_b"])]
    (z3,) = _row_fwd("ln_silu_fwd", _f_ln_silu, [(z1, cw, 0)], ln_a, [(cw, _MXU_DT)])
    ya = _mm("mm_conv_out", z3, w["conv_out_w"])
    ln_b = [_row(w["sg_ln_g"]), _row(w["sg_ln_b"])]
    u, vn = _row_fwd("sg_pre_fwd", _f_sg_pre, [(p, cw, blk(lay.b, cw)), (p, cw, blk(lay.b + cw, cw))], ln_b,
                     [(cw, F32), (cw, _MXU_DT)])
    bcol = w["sg_b"].reshape(SG_GROUPS, SG_CHUNK, 1)
    ub = _sg_mix_fwd(u, vn, w["sg_w"], bcol)
    yb = _mm("mm_sg_out", ub, w["sg_out_w"])
    (qn,) = _row_fwd("q_norm_fwd", _f_rms, [(p, 384, blk(lay.q, 384))], [_row(w["mla_q_norm_g"])], [(384, _MXU_DT)])
    (kvn,) = _row_fwd("kv_norm_fwd", _f_rms, [(p, 256, blk(lay.kv, 256))], [_row(w["mla_kv_norm_g"])],
                      [(256, _MXU_DT)])
    q2 = _mm("mm_uq", qn, w["mla_w_uq"])
    kv2 = _mm("mm_ukv", kvn, w["mla_w_ukv"])
    qf, kf, vf = _rope_fwd(q2, kv2, p, blk(lay.krm, HEAD_PAD), blk(lay.krs, HEAD_PAD), tc, ts)
    o = _attn_fwd(qf, kf, vf)
    yc = _mm("mm_o", o, w["mla_w_o"])
    gate_rows = [(p, d, 0), (p, d, 1), (p, d, 2)]
    (merged,) = _row_fwd("merge_fwd", _f_merge, gate_rows + [(ya, d, 0), (yb, d, 0), (yc, d, 0)], [], [(d, _MXU_DT)])
    t = _mm("mm_out", merged, w["w_out"])
    x1, h2 = _row_fwd("resid_mix_fwd", _f_resid_rms_rms, [(x, d, 0), (t, d, 0)],
                      [_row(w["mix_post_g"]), _row(w["ffn_pre_g"])], [(d, F32), (d, _MXU_DT)])
    up = _mm("mm_up", h2, w["ffn_w_up"])
    act = _conv_geglu_fwd(up, w["ffn_dw_w"], _row(w["ffn_dw_b"]))
    dn = _mm("mm_down", act, w["ffn_w_down"])
    if g_next is None:
        (x2,) = _row_fwd("resid_ffn_last_fwd", _f_resid_rms, [(x1, d, 0), (dn, d, 0)], [_row(w["ffn_post_g"])],
                         [(d, F32)])
        h_next = None
    else:
        x2, h_next = _row_fwd("resid_ffn_fwd", _f_resid_rms_rms, [(x1, d, 0), (dn, d, 0)],
                              [_row(w["ffn_post_g"]), _row(g_next)], [(d, F32), (d, _MXU_DT)])
    saved = dict(x=x, h=h, p=p, z1=z1, z3=z3, ya=ya, u=u, vn=vn, ub=ub, yb=yb, qn=qn, kvn=kvn, qf=qf, kf=kf, vf=vf, o=o,
                 yc=yc, merged=merged, t=t, x1=x1, h2=h2, up=up, act=act, dn=dn, bcol=bcol)
    return x2, h_next, saved


def _layer_bwd(dx2, dh_next, w, g_next, sv, tc, ts):
    d = dx2.shape[1]
    lay = _InLayout(d)
    cw = d // 2
    blk = lambda off, width: off // width
    lo = _MXU_DT
    g = {}
    x1, dn = sv["x1"], sv["dn"]
    if dh_next is None:
        dx1, ddn, g["ffn_post_g"] = _row_bwd("resid_ffn_last_bwd", _f_resid_rms, [(x1, d, 0), (dn, d, 0)],
                                             [_row(w["ffn_post_g"])], [(dx2, d, 0)], [(0, F32), (1, lo)], [0])
    else:
        dx1, ddn, g["ffn_post_g"], g["next_pre_g"] = _row_bwd(
            "resid_ffn_bwd", _f_resid_rms_rms, [(x1, d, 0), (dn, d, 0)], [_row(w["ffn_post_g"]), _row(g_next)],
            [(dx2, d, 0), (dh_next, d, 0)], [(0, F32), (1, lo)], [0, 1])
    dact = _mm("mm_down_dx", ddn, w["ffn_w_down"], tb=True)
    g["ffn_w_down"] = _mm("mm_down_dw", sv["act"], ddn, ta=True)
    dup, dw_halves, db_halves = _conv_geglu_bwd(sv["up"], w["ffn_dw_w"], _row(w["ffn_dw_b"]), dact)
    g["ffn_dw_w"] = jnp.concatenate([dw_halves[0], dw_halves[1]], axis=1)
    g["ffn_dw_b"] = jnp.concatenate([db_halves[0], db_halves[1]], axis=1)
    dh2 = _mm("mm_up_dx", dup, w["ffn_w_up"], tb=True, a_halves=True)
    g["ffn_w_up"] = _mm("mm_up_dw", sv["h2"], dup, ta=True, b_halves=True, out_quarters=True)
    dx, dt, g["mix_post_g"], g["ffn_pre_g"] = _row_bwd(
        "resid_mix_bwd", _f_resid_rms_rms, [(sv["x"], d, 0), (sv["t"], d, 0)],
        [_row(w["mix_post_g"]), _row(w["ffn_pre_g"])], [(dx1, d, 0), (dh2, d, 0)], [(0, F32), (1, lo)], [0, 1])
    dmerged = _mm("mm_out_dx", dt, w["w_out"], tb=True)
    g["w_out"] = _mm("mm_out_dw", sv["merged"], dt, ta=True)
    p = sv["p"]
    gate_rows = [(p, d, 0), (p, d, 1), (p, d, 2)]
    dp, dya, dyb, dyc = _row_bwd(
        "merge_bwd", _f_merge, gate_rows + [(sv["ya"], d, 0), (sv["yb"], d, 0), (sv["yc"], d, 0)], [],
        [(dmerged, d, 0)], [((0, 1, 2), lo), ((3,), lo), ((4,), lo), ((5,), lo)], [], place=(lay.width, 0))
    do = _mm("mm_o_dx", dyc, w["mla_w_o"], tb=True, out_dtype=lo)
    g["mla_w_o"] = _mm("mm_o_dw", sv["o"], dyc, ta=True)
    dqf, dkf, dvf = _attn_bwd(sv["qf"], sv["kf"], sv["vf"], do)
    dq2, dkv2, dp = _rope_bwd(dqf, dkf, dvf, tc, ts, dp, blk(lay.krm, 3 * HEAD_PAD))
    dkvn = _mm("mm_ukv_dx", dkv2, w["mla_w_ukv"], tb=True)
    g["mla_w_ukv"] = _mm("mm_ukv_dw", sv["kvn"], dkv2, ta=True)
    dqn = _mm("mm_uq_dx", dq2, w["mla_w_uq"], tb=True)
    g["mla_w_uq"] = _mm("mm_uq_dw", sv["qn"], dq2, ta=True)
    dp, g["mla_q_norm_g"] = _row_bwd("q_norm_bwd", _f_rms, [(p, 384, blk(lay.q, 384))], [_row(w["mla_q_norm_g"])],
                                     [(dqn, 384, 0)], [((0,), lo)], [0], place=(lay.width, blk(lay.q, 384)), into=dp)
    dp, g["mla_kv_norm_g"] = _row_bwd("kv_norm_bwd", _f_rms, [(p, 256, blk(lay.kv, 256))],
                                      [_row(w["mla_kv_norm_g"])], [(dkvn, 256, 0)], [((0,), lo)], [0],
                                      place=(lay.width, blk(lay.kv, 256)), into=dp)
    dub = _mm("mm_sg_out_dx", dyb, w["sg_out_w"], tb=True)
    g["sg_out_w"] = _mm("mm_sg_out_dw", sv["ub"], dyb, ta=True)
    du, dvn, g["sg_w"], dbcol = _sg_mix_bwd(sv["u"], sv["vn"], w["sg_w"], sv["bcol"], dub)
    g["sg_b"] = dbcol.reshape(SG_GROUPS, SG_CHUNK)
    dp, g["sg_ln_g"], g["sg_ln_b"] = _row_bwd(
        "sg_pre_bwd", _f_sg_pre, [(p, cw, blk(lay.b, cw)), (p, cw, blk(lay.b + cw, cw))],
        [_row(w["sg_ln_g"]), _row(w["sg_ln_b"])], [(du, cw, 0), (dvn, cw, 0)], [((0, 1), lo)], [0, 1],
        place=(lay.width, blk(lay.b, d)), into=dp)
    dz3 = _mm("mm_conv_out_dx", dya, w["conv_out_w"], tb=True)
    g["conv_out_w"] = _mm("mm_conv_out_dw", sv["z3"], dya, ta=True)
    dz1, g["conv_ln_g"], g["conv_ln_b"] = _row_bwd(
        "ln_silu_bwd", _f_ln_silu, [(sv["z1"], cw, 0)], [_row(w["conv_ln_g"]), _row(w["conv_ln_b"])],
        [(dz3, cw, 0)], [((0,), F32)], [0, 1])
    dp, g["conv_dw_w"], g["conv_dw_b"] = _glu_conv_bwd(p, blk(lay.a, 2 * _LANES), w["conv_dw_w"], dz1, dp)
    dh = _mm("mm_in_dx", dp, w["w_in"])
    g["w_in"] = _mm("mm_in_dw", dp, sv["h"], ta=True)
    return dx, dh, g


def _local_step(x, positions, target, layers):
    d = x.shape[1]
    tc, ts = _rope_tables(positions)
    ws = [_prep_layer(wl, d) for wl in layers]
    depth = len(ws)
    (h,) = _row_fwd("rms_first_fwd", _f_rms, [(x, d, 0)], [_row(ws[0]["mix_pre_g"])], [(d, _MXU_DT)])
    saved = []
    for l in range(depth):
        g_next = ws[l + 1]["mix_pre_g"] if l + 1 < depth else None
        x, h, sv = _layer_fwd(x, h, ws[l], g_next, tc, ts)
        saved.append(sv)
    loss, dx = _loss_head(x, target)
    grads = [None] * depth
    dh = None
    for l in reversed(range(depth)):
        g_next = ws[l + 1]["mix_pre_g"] if l + 1 < depth else None
        dx, dh, g = _layer_bwd(dx, dh, ws[l], g_next, saved[l], tc, ts)
        if "next_pre_g" in g:
            grads[l + 1]["mix_pre_g"] = g.pop("next_pre_g")
        grads[l] = g
    x0 = saved[0]["x"]
    grad_x, grads[0]["mix_pre_g"] = _row_bwd("rms_first_bwd", _f_x_rms, [(x0, d, 0)], [_row(ws[0]["mix_pre_g"])],
                                             [(dx, d, 0), (dh, d, 0)], [(0, F32)], [0])
    return loss, grad_x, [_unprep_grads(g, d) for g in grads]


_MATRICES = ("w_in", "conv_out_w", "sg_out_w", "mla_w_uq", "mla_w_ukv", "mla_w_o", "w_out", "ffn_w_up", "ffn_w_down")
_F32_GATHERED = ("conv_dw_w", "ffn_dw_w")
_RS_DT = jnp.bfloat16


_ROW_SHARDED = SHARDED_MID + ("w_in",)


_GATHERED = SHARDED + _F32_GATHERED


def _layer_shards(w, l):
    hi = {n: w[n][l].astype(jnp.bfloat16) for n in SHARDED}
    lo = [(w[n][l] - hi[n].astype(F32)).astype(jnp.bfloat16) for n in _F32_GATHERED]
    return [hi[n] for n in SHARDED] + lo


def _layer_weights(w, l, gathered):
    wl = {n: w[n][l] for n in REPLICATED}
    for n, g in zip(_GATHERED, gathered):
        whole = jnp.concatenate([g[j] for j in range(4)], axis=0 if n in _ROW_SHARDED else 1)
        if n in wl:
            wl[n] = wl[n].astype(F32) + whole.astype(F32)
        else:
            wl[n] = whole.astype(_MXU_DT) if n in _MATRICES else whole
    return wl


def _by_destination(name, gl):
    if gl.ndim == 3:
        return gl
    k, n = gl.shape
    if name in _ROW_SHARDED:
        return gl.reshape(4, k // 4, n)
    return gl.reshape(k, 4, n // 4).transpose(1, 0, 2)


def kernel(x, positions, mix_pre_g, mix_post_g, ffn_pre_g, ffn_post_g, w_in, conv_dw_w, conv_dw_b, conv_ln_g, conv_ln_b, conv_out_w, sg_ln_g, sg_ln_b, sg_w, sg_b, sg_out_w, mla_q_norm_g, mla_w_uq, mla_kv_norm_g, mla_w_ukv, mla_w_o, w_out, ffn_w_up, ffn_dw_w, ffn_dw_b, ffn_w_down, loss_target, m_mix_pre_g, m_mix_post_g, m_ffn_pre_g, m_ffn_post_g, m_w_in, m_conv_dw_w, m_conv_dw_b, m_conv_ln_g, m_conv_ln_b, m_conv_out_w, m_sg_ln_g, m_sg_ln_b, m_sg_w, m_sg_b, m_sg_out_w, m_mla_q_norm_g, m_mla_w_uq, m_mla_kv_norm_g, m_mla_w_ukv, m_mla_w_o, m_w_out, m_ffn_w_up, m_ffn_dw_w, m_ffn_dw_b, m_ffn_w_down, v_mix_pre_g, v_mix_post_g, v_ffn_pre_g, v_ffn_post_g, v_w_in, v_conv_dw_w, v_conv_dw_b, v_conv_ln_g, v_conv_ln_b, v_conv_out_w, v_sg_ln_g, v_sg_ln_b, v_sg_w, v_sg_b, v_sg_out_w, v_mla_q_norm_g, v_mla_w_uq, v_mla_kv_norm_g, v_mla_w_ukv, v_mla_w_o, v_w_out, v_ffn_w_up, v_ffn_dw_w, v_ffn_dw_b, v_ffn_w_down):
    args = dict(locals())
    w = {n: args[n] for n in WEIGHTS}
    m = {n: args["m_" + n] for n in WEIGHTS}
    v = {n: args["v_" + n] for n in WEIGHTS}
    depth = mix_pre_g.shape[0]

    assert depth == 2, "the two cores of a chip split the communication by layer"
    for t in (w, m, v):
        t["w_in"] = jnp.swapaxes(t["w_in"], 1, 2)
    d = x.shape[-1]
    mesh_x, mesh_y, mesh_c = _mesh_pos()
    my_chip = 2 * mesh_x + mesh_y
    names = list(SHARDED)
    whole = lambda ref, chip, k: ref
    to_my_slot = lambda ref, chip, k: ref.at[2 * lax.axis_index("x") + lax.axis_index("y")]
    block_of_chip = lambda ref, chip, k: ref.at[_chip_index(chip)]
    slot_k = lambda ref, chip, k: ref.at[k]

    gathered0 = _ag_forward(_layer_shards(w, 0), None, 0, False)
    shards1 = _layer_shards(w, 1)
    lands = [jax.ShapeDtypeStruct((4,) + a.shape, a.dtype) for a in shards1]
    sems_s, sems_r, shards1, lands1, token = _split_start("ag1_start", 1, shards1, lands, whole, to_my_slot)
    tc, ts = _rope_tables(positions[0])
    ws0 = _prep_layer(_layer_weights(w, 0, gathered0), d)
    x0 = x[0] + token[0, 0]
    (h0,) = _row_fwd("rms_first_fwd", _f_rms, [(x0, d, 0)], [_row(ws0["mix_pre_g"])], [(d, _MXU_DT)])
    x1, h1, sv0 = _layer_fwd(x0, h0, ws0, w["mix_pre_g"][1], tc, ts)
    shards1, lands1 = _split_wait("ag1_wait", 1, sems_s, sems_r, shards1, lands1, x1, whole, to_my_slot)
    ws1 = _prep_layer(_layer_weights(w, 1, _ag_forward(shards1, lands1, 1, True)), d)
    x2, _, sv1 = _layer_fwd(x1, h1, ws1, None, tc, ts)
    loss, dx = _loss_head(x2, loss_target[0])
    loss = lax.psum(loss[0, 0], ("x", "y", "c"))

    dx, dh, gk1 = _layer_bwd(dx, None, ws1, None, sv1, tc, ts)
    grads1 = _unprep_grads(gk1, d)
    gd1 = [_by_destination(n, grads1[n]) for n in names]
    t1 = [_add_pair("rs_pair1_" + n, g, a) for n, g, a in zip(names, gd1, _rs_swap(gd1, 1))]
    lands = [jax.ShapeDtypeStruct((3,) + t.shape[1:], t.dtype) for t in t1]
    sems_s, sems_r, t1, b1, token = _split_start("rs1_start", 1, t1, lands, block_of_chip, slot_k)
    dx = dx + token[0, 0]
    dx, dh, gk0 = _layer_bwd(dx, dh, ws0, ws1["mix_pre_g"], sv0, tc, ts)
    grads1["mix_pre_g"] = gk0.pop("next_pre_g")
    grad_x, gk0["mix_pre_g"] = _row_bwd("rms_first_bwd", _f_x_rms, [(x0, d, 0)], [_row(ws0["mix_pre_g"])],
                                        [(dx, d, 0), (dh, d, 0)], [(0, F32)], [0])
    t1, b1 = _split_wait("rs1_wait", 1, sems_s, sems_r, t1, b1, grad_x, block_of_chip, slot_k)
    grads0 = _unprep_grads(gk0, d)
    gd0 = [_by_destination(n, grads0[n]) for n in names]
    t0 = [_add_pair("rs_pair0_" + n, g, a) for n, g, a in zip(names, gd0, _rs_swap(gd0, 0))]
    b0 = _rs_scatter(t0, 0)
    own = lambda l: jnp.stack([my_chip, (mesh_c == l).astype(jnp.int32)]).astype(jnp.int32)
    sums = [_add_quads("rs_sum0_" + n, t0[i], b0[i], 0, own(0),
                       _add_quads("rs_sum1_" + n, t1[i], b1[i], 1, own(1), None)) for i, n in enumerate(names)]

    out = {}
    for n, gr in zip(names, _join_layers(sums)):
        out[n] = (gr, *_adamw("adamw_" + n, w[n], gr, m[n], v[n]))
    out["w_in"] = tuple(jnp.swapaxes(a, 1, 2) for a in out["w_in"])
    rep = list(REPLICATED)
    grads = [grads0, grads1]
    g_rep = [jnp.stack([grads[l][n].reshape(w[n].shape[1:]) for l in range(depth)]) for n in rep]
    for n, *res in zip(rep, *_all_reduce_adamw(g_rep, [w[n] for n in rep], [m[n] for n in rep], [v[n] for n in rep])):
        out[n] = tuple(res)
    return (loss, grad_x[None], *[out[n][i] for i in range(4) for n in WEIGHTS])
```

```python
import functools
import math

import jax
import jax.numpy as jnp
from jax import lax
from jax.experimental import pallas as pl
from jax.experimental.pallas import tpu as pltpu

F32 = jnp.float32
_MXU_DT = jnp.bfloat16
_VMEM_LIMIT = 48 * 1024 * 1024
_LANES = 128
_MESH = pl.DeviceIdType.MESH

N_HEADS = 8
QK_NOPE = 64
QK_ROPE = 32
V_HEAD = 64
HEAD_PAD = 128
SG_GROUPS = 4
SG_CHUNK = 128
CONV_K = 31
FFN_K = 3
ROPE_THETA = 10000.0
EPS = 1e-6
ADAM_LR, ADAM_B1, ADAM_B2, ADAM_EPS, ADAM_WD, ADAM_STEP = 0.001, 0.9, 0.999, 1e-08, 0.01, 10

SHARDED_LAST = ("w_in", "conv_dw_w", "conv_out_w", "sg_out_w", "mla_w_uq", "mla_w_ukv", "mla_w_o", "ffn_w_up",
                "ffn_dw_w")
SHARDED_MID = ("w_out", "ffn_w_down")
SHARDED = SHARDED_LAST + SHARDED_MID
WEIGHTS = ("mix_pre_g", "mix_post_g", "ffn_pre_g", "ffn_post_g", "w_in", "conv_dw_w", "conv_dw_b", "conv_ln_g",
           "conv_ln_b", "conv_out_w", "sg_ln_g", "sg_ln_b", "sg_w", "sg_b", "sg_out_w", "mla_q_norm_g", "mla_w_uq",
           "mla_kv_norm_g", "mla_w_ukv", "mla_w_o", "w_out", "ffn_w_up", "ffn_dw_w", "ffn_dw_b", "ffn_w_down")
REPLICATED = tuple(n for n in WEIGHTS if n not in SHARDED)


def _cparams(sem=None):
    return pltpu.CompilerParams(dimension_semantics=sem, vmem_limit_bytes=_VMEM_LIMIT)


def _pick(n, cands):
    for c in cands:
        if n % c == 0:
            return c
    return n


def _largest_tile(dim, cap):
    for t in range(min(cap, dim) // _LANES * _LANES, 0, -_LANES):
        if dim % t == 0:
            return t
    return dim


_MM_VMEM_BUDGET = 36 * 1024 * 1024
_MM_TM_CAP, _MM_TN_CAP, _MM_TK_CAP = 1024, 1536, 3072


def _mm(name, a, b, *, ta=False, tb=False, out_dtype=F32, a_halves=False, b_halves=False, out_quarters=False):
    assert not (a_halves and ta) and not (b_halves and tb)
    if a_halves:
        m, kdim = a.shape[1], 2 * a.shape[2]
    else:
        (kdim, m) = a.shape if ta else a.shape[::-1]
    if b_halves:
        kdim2, n = b.shape[1], 2 * b.shape[2]
    else:
        (n, kdim2) = b.shape if tb else b.shape[::-1]
    assert kdim == kdim2, (a.shape, b.shape, ta, tb)
    tk = _largest_tile(kdim // 2 if a_halves else kdim, _MM_TK_CAP)
    tn = _largest_tile(n // 4 if out_quarters else (n // 2 if b_halves else n), _MM_TN_CAP)
    nk = kdim // tk
    ab, bb, ob = a.dtype.itemsize, b.dtype.itemsize, jnp.dtype(out_dtype).itemsize
    tm = _largest_tile(m, _MM_TM_CAP)
    vmem = lambda t: 2 * (t * tk * ab + tk * tn * bb + t * tn * ob) + (t * tn * 4 if nk > 1 else 0)
    while vmem(tm) > _MM_VMEM_BUDGET and tm > _LANES:
        tm = _largest_tile(m, tm - _LANES)
    dims = (((0 if ta else 1,), (1 if tb else 0,)), ((), ()))

    def dot(a_ref, b_ref):
        return lax.dot_general(a_ref[...].astype(_MXU_DT), b_ref[...].astype(_MXU_DT), dims,
                               preferred_element_type=F32)

    def body_one(a_ref, b_ref, o_ref):
        o_ref[...] = dot(a_ref, b_ref).astype(o_ref.dtype)

    def body_acc(a_ref, b_ref, o_ref, acc_ref):
        k = pl.program_id(2)

        @pl.when(k == 0)
        def _():
            acc_ref[...] = jnp.zeros_like(acc_ref)

        acc_ref[...] += dot(a_ref, b_ref)

        @pl.when(k == nk - 1)
        def _():
            o_ref[...] = acc_ref[...].astype(o_ref.dtype)

    if a_halves:
        per = nk // 2
        a_spec = pl.BlockSpec((None, tm, tk), lambda i, j, k: (k // per, i, k % per))
    elif ta:
        a_spec = pl.BlockSpec((tk, tm), lambda i, j, k: (k, i))
    else:
        a_spec = pl.BlockSpec((tm, tk), lambda i, j, k: (i, k))
    if b_halves:
        per_b = n // 2 // tn
        b_spec = pl.BlockSpec((None, tk, tn), lambda i, j, k: (j // per_b, k, j % per_b))
    elif tb:
        b_spec = pl.BlockSpec((tn, tk), lambda i, j, k: (j, k))
    else:
        b_spec = pl.BlockSpec((tk, tn), lambda i, j, k: (k, j))
    if out_quarters:
        per_o = n // 4 // tn
        o_spec = pl.BlockSpec((None, tm, tn), lambda i, j, k: (j // per_o, i, j % per_o))
        o_shape = jax.ShapeDtypeStruct((4, m, n // 4), out_dtype)
    else:
        o_spec = pl.BlockSpec((tm, tn), lambda i, j, k: (i, j))
        o_shape = jax.ShapeDtypeStruct((m, n), out_dtype)
    return pl.pallas_call(
        body_one if nk == 1 else body_acc, name=name, grid=(m // tm, n // tn, nk),
        in_specs=[a_spec, b_spec], out_specs=o_spec, out_shape=o_shape,
        scratch_shapes=[] if nk == 1 else [pltpu.VMEM((tm, tn), F32)],
        compiler_params=_cparams(("parallel", "parallel", "arbitrary")),
    )(a, b)


def _row_spec(tm, width, idx):
    return pl.BlockSpec((tm, width), lambda i: (i, idx))


def _full_spec(shape):
    zeros = (0,) * len(shape)
    return pl.BlockSpec(shape, lambda i: zeros)


def _row_fwd(name, fn, rows, params, outs, tm=256):
    s = rows[0][0].shape[0]
    nr, npar = len(rows), len(params)

    def body(*refs):
        vals = [r[...].astype(F32) for r in refs[:nr + npar]]
        res = fn(*vals)
        for o_ref, r in zip(refs[nr + npar:], res):
            o_ref[...] = r.astype(o_ref.dtype)

    return pl.pallas_call(
        body, name=name, grid=(s // tm,),
        in_specs=[_row_spec(tm, w, i) for _, w, i in rows] + [_full_spec(p.shape) for p in params],
        out_specs=[_row_spec(tm, w, 0) for w, _ in outs],
        out_shape=[jax.ShapeDtypeStruct((s, w), dt) for w, dt in outs],
        compiler_params=_cparams(("parallel",)),
    )(*[r[0] for r in rows], *params)


def _row_bwd(name, fn, rows, params, cots, row_grads, param_grads, tm=256, place=None, into=None):
    s = rows[0][0].shape[0]
    nr, npar, nc = len(rows), len(params), len(cots)
    row_grads = [((idxs,) if isinstance(idxs, int) else tuple(idxs), dt) for idxs, dt in row_grads]
    widths = [sum(rows[i][1] for i in idxs) for idxs, _ in row_grads]

    def body(*refs):
        i = pl.program_id(0)
        vals = [r[...].astype(F32) for r in refs[:nr + npar]]
        cvals = tuple(r[...].astype(F32) for r in refs[nr + npar:nr + npar + nc])
        _, vjp = jax.vjp(fn, *vals)
        grads = vjp(cvals)
        outs = refs[nr + npar + nc + (into is not None):]
        for o_ref, (idxs, _) in zip(outs, row_grads):
            pos = 0
            for idx in idxs:
                o_ref[:, pos:pos + rows[idx][1]] = grads[idx].astype(o_ref.dtype)
                pos += rows[idx][1]
        for o_ref, idx in zip(outs[len(row_grads):], param_grads):
            @pl.when(i == 0)
            def _(o_ref=o_ref):
                o_ref[...] = jnp.zeros_like(o_ref)

            o_ref[...] += grads[nr + idx]

    out_specs = [_row_spec(tm, w, 0) for w in widths] + [_full_spec(params[idx].shape) for idx in param_grads]
    out_shape = ([jax.ShapeDtypeStruct((s, w), dt) for w, (_, dt) in zip(widths, row_grads)]
                 + [jax.ShapeDtypeStruct(params[idx].shape, F32) for idx in param_grads])
    extra, aliases = [], {}
    if place is not None:
        out_specs[0] = _row_spec(tm, widths[0], place[1])
        out_shape[0] = jax.ShapeDtypeStruct((s, place[0]), row_grads[0][1])
    if into is not None:
        extra, aliases = [into], {nr + npar + nc: 0}
    return pl.pallas_call(
        body, name=name, grid=(s // tm,),
        in_specs=([_row_spec(tm, w, i) for _, w, i in rows] + [_full_spec(p.shape) for p in params]
                  + [_row_spec(tm, w, i) for _, w, i in cots] + [_ANY] * len(extra)),
        out_specs=out_specs, out_shape=out_shape, input_output_aliases=aliases,
        compiler_params=_cparams(("arbitrary",)),
    )(*[r[0] for r in rows], *params, *[c[0] for c in cots], *extra)


def _rms(x, g):
    return x * lax.rsqrt(jnp.mean(x * x, axis=-1, keepdims=True) + EPS) * g


def _ln(x, g, b):
    mu = jnp.mean(x, axis=-1, keepdims=True)
    xc = x - mu
    var = jnp.mean(xc * xc, axis=-1, keepdims=True)
    return xc * lax.rsqrt(var + EPS) * g + b


def _sigmoid(x):
    return 1.0 / (1.0 + jnp.exp(-x))


def _gelu(x):
    return x * (0.5 * (1.0 + jnp.tanh(math.sqrt(2.0 / math.pi) * (x + 0.044715 * (x * x * x)))))


def _f_rms(x, g):
    return (_rms(x, g),)


def _f_x_rms(x, g):
    return (x, _rms(x, g))


def _f_ln_silu(z, g, b):
    y = _ln(z, g, b)
    return (y * _sigmoid(y),)


def _f_sg_pre(bu, bv, g, b):
    return (_gelu(bu), _ln(_gelu(bv), g, b))


def _f_merge(g0, g1, g2, ya, yb, yc):
    return (_sigmoid(g0) * ya + _sigmoid(g1) * yb + _sigmoid(g2) * yc,)


def _f_resid_rms(x, t, g_post):
    return (x + _rms(t, g_post),)


def _f_resid_rms_rms(x, t, g_post, g_next):
    x1 = x + _rms(t, g_post)
    return (x1, _rms(x1, g_next))


def _f_geglu(zg, zv):
    return _gelu(zg) * zv


_CONV_TILE_ELEMS = 16 * 1024


def _conv_tr(c):
    return _CONV_TILE_ELEMS // c


def _conv_tile(zp_ref, w_ref, bias, k_taps, off, r0):
    c = zp_ref.shape[1]
    tr = _conv_tr(c)
    acc = jnp.broadcast_to(bias, (tr, c))
    for k in range(k_taps):
        acc = acc + w_ref[k:k + 1, :] * zp_ref[r0 + off + k:r0 + off + k + tr, :]
    return acc


def _conv_bwd_input_tile(dzp_ref, w_ref, k_taps, r0):
    c = dzp_ref.shape[1]
    tr = _conv_tr(c)
    acc = jnp.zeros((tr, c), F32)
    for k in range(k_taps):
        s0 = r0 + (k_taps - 1) - k
        acc = acc + w_ref[k:k + 1, :] * dzp_ref[s0:s0 + tr, :]
    return acc


def _conv_bwd_weight(dzp_ref, zp_ref, dw_ref, db_ref, k_taps, off, s):
    c = zp_ref.shape[1]
    tr = _conv_tr(c)
    fold = lambda v: jnp.sum(v.reshape(tr // 8, 8, c), axis=0)
    for k in range(k_taps):
        acc = jnp.zeros((8, c), F32)
        for r in range(s // tr):
            r0 = r * tr
            acc = acc + fold(dzp_ref[r0:r0 + tr, :] * zp_ref[r0 + off + k:r0 + off + k + tr, :])
        dw_ref[k:k + 1, :] = jnp.sum(acc, axis=0, keepdims=True)
    acc = jnp.zeros((8, c), F32)
    for r in range(s // tr):
        acc = acc + fold(dzp_ref[r * tr:(r + 1) * tr, :])
    db_ref[...] = jnp.sum(acc, axis=0, keepdims=True)


def _glu_conv_fwd(p, blk0, w, b):
    s = p.shape[0]
    k_taps, c = w.shape
    cb, pad = _LANES, 32
    off = pad - (k_taps - 1)

    def body(a_ref, w_ref, b_ref, o_ref, zp_ref):
        zp_ref[0:pad, :] = jnp.zeros((pad, cb), F32)
        zp_ref[pad:pad + s, :] = a_ref[:, 0:cb] * _sigmoid(a_ref[:, cb:2 * cb])
        tr = _conv_tr(cb)
        for r in range(s // tr):
            o_ref[r * tr:(r + 1) * tr, :] = _conv_tile(zp_ref, w_ref, b_ref[...], k_taps, off, r * tr)

    return pl.pallas_call(
        body, name="glu_conv_fwd", grid=(c // cb,),
        in_specs=[pl.BlockSpec((s, 2 * cb), lambda j: (0, blk0 + j)),
                  pl.BlockSpec((k_taps, cb), lambda j: (0, j)), pl.BlockSpec((1, cb), lambda j: (0, j))],
        out_specs=pl.BlockSpec((s, cb), lambda j: (0, j)),
        out_shape=jax.ShapeDtypeStruct((s, c), F32),
        scratch_shapes=[pltpu.VMEM((s + pad, cb), F32)],
        compiler_params=_cparams(("parallel",)),
    )(p, w, b)


def _glu_conv_bwd(p, blk0, w, dz, dp):
    s = p.shape[0]
    k_taps, c = w.shape
    cb, pad = _LANES, 32
    off = pad - (k_taps - 1)

    def body(a_ref, w_ref, dz_ref, dp_in, da_ref, dw_ref, db_ref, zp_ref, dzp_ref):
        zp_ref[0:pad, :] = jnp.zeros((pad, cb), F32)
        zp_ref[pad:pad + s, :] = a_ref[:, 0:cb] * _sigmoid(a_ref[:, cb:2 * cb])
        dzp_ref[0:s, :] = dz_ref[...]
        dzp_ref[s:s + pad, :] = jnp.zeros((pad, cb), F32)
        tr = _conv_tr(cb)
        for r in range(s // tr):
            rows = slice(r * tr, (r + 1) * tr)
            dz0 = _conv_bwd_input_tile(dzp_ref, w_ref, k_taps, r * tr)
            sg = _sigmoid(a_ref[rows, cb:2 * cb])
            da_ref[rows, 0:cb] = (dz0 * sg).astype(da_ref.dtype)
            da_ref[rows, cb:2 * cb] = (dz0 * a_ref[rows, 0:cb] * sg * (1.0 - sg)).astype(da_ref.dtype)
        _conv_bwd_weight(dzp_ref, zp_ref, dw_ref, db_ref, k_taps, off, s)

    return pl.pallas_call(
        body, name="glu_conv_bwd", grid=(c // cb,),
        in_specs=[pl.BlockSpec((s, 2 * cb), lambda j: (0, blk0 + j)),
                  pl.BlockSpec((k_taps, cb), lambda j: (0, j)), pl.BlockSpec((s, cb), lambda j: (0, j)), _ANY],
        out_specs=[pl.BlockSpec((s, 2 * cb), lambda j: (0, blk0 + j)),
                   pl.BlockSpec((k_taps, cb), lambda j: (0, j)), pl.BlockSpec((1, cb), lambda j: (0, j))],
        out_shape=[jax.ShapeDtypeStruct(dp.shape, dp.dtype),
                   jax.ShapeDtypeStruct((k_taps, c), F32), jax.ShapeDtypeStruct((1, c), F32)],
        scratch_shapes=[pltpu.VMEM((s + pad, cb), F32), pltpu.VMEM((s + pad, cb), F32)],
        input_output_aliases={3: 0},
        compiler_params=_cparams(("parallel",)),
    )(p, w, dz, dp)


def _conv_geglu_fwd(up, w, b):
    s, f2 = up.shape
    f = f2 // 2
    k_taps = w.shape[0]
    cb, pad = _FFN_CB, 8
    off = pad - (k_taps - 1)
    nb = f // cb

    def body(ug_ref, uv_ref, wg_ref, wv_ref, bg_ref, bv_ref, o_ref, z_ref, w_ref, b_ref):
        _pair(w_ref, wg_ref[...], wv_ref[...], cb)
        _pair(b_ref, bg_ref[...], bv_ref[...], cb)
        z_ref[0:pad, :] = jnp.zeros((pad, 2 * cb), F32)
        z_ref[pad:pad + s, 0:cb] = ug_ref[...]
        z_ref[pad:pad + s, cb:2 * cb] = uv_ref[...]
        tr = _conv_tr(2 * cb)
        for r in range(s // tr):
            z = _conv_tile(z_ref, w_ref, b_ref[...], k_taps, off, r * tr)
            o_ref[r * tr:(r + 1) * tr, :] = _f_geglu(z[:, 0:cb], z[:, cb:2 * cb]).astype(o_ref.dtype)

    two = lambda rows_: [pl.BlockSpec((rows_, cb), lambda j: (0, j)), pl.BlockSpec((rows_, cb), lambda j: (0, nb + j))]
    return pl.pallas_call(
        body, name="conv_geglu_fwd", grid=(nb,),
        in_specs=two(s) + two(k_taps) + two(1),
        out_specs=pl.BlockSpec((s, cb), lambda j: (0, j)),
        out_shape=jax.ShapeDtypeStruct((s, f), _MXU_DT),
        scratch_shapes=[pltpu.VMEM((s + pad, 2 * cb), F32), pltpu.VMEM((k_taps, 2 * cb), F32),
                        pltpu.VMEM((1, 2 * cb), F32)],
        compiler_params=_cparams(("parallel",)),
    )(up, up, w, w, b, b)


def _pair(dst_ref, first, second, cb):
    dst_ref[:, 0:cb] = first
    dst_ref[:, cb:2 * cb] = second


def _conv_geglu_bwd(up, w, b, dact):
    s, f2 = up.shape
    f = f2 // 2
    k_taps = w.shape[0]
    cb, pad = _FFN_CB, 8
    off = pad - (k_taps - 1)
    nb = f // cb

    def body(ug_ref, uv_ref, wg_ref, wv_ref, bg_ref, bv_ref, da_ref, du_ref, dw_ref, db_ref, z_ref, dz_ref, w_ref,
             b_ref, dw_sc, db_sc):
        _pair(w_ref, wg_ref[...], wv_ref[...], cb)
        _pair(b_ref, bg_ref[...], bv_ref[...], cb)
        z_ref[0:pad, :] = jnp.zeros((pad, 2 * cb), F32)
        z_ref[pad:pad + s, 0:cb] = ug_ref[...]
        z_ref[pad:pad + s, cb:2 * cb] = uv_ref[...]
        dz_ref[s:s + pad, :] = jnp.zeros((pad, 2 * cb), F32)
        tr = _conv_tr(2 * cb)
        for r in range(s // tr):
            rows = slice(r * tr, (r + 1) * tr)
            z = _conv_tile(z_ref, w_ref, b_ref[...], k_taps, off, r * tr)
            _, vjp = jax.vjp(_f_geglu, z[:, 0:cb], z[:, cb:2 * cb])
            dzg, dzv = vjp(da_ref[rows, :].astype(F32))
            dz_ref[rows, 0:cb] = dzg
            dz_ref[rows, cb:2 * cb] = dzv
        for r in range(s // tr):
            rows = slice(r * tr, (r + 1) * tr)
            du = _conv_bwd_input_tile(dz_ref, w_ref, k_taps, r * tr).astype(du_ref.dtype)
            du_ref[0, rows, :] = du[:, 0:cb]
            du_ref[1, rows, :] = du[:, cb:2 * cb]
        _conv_bwd_weight(dz_ref, z_ref, dw_sc, db_sc, k_taps, off, s)
        for half in range(2):
            dw_ref[half] = dw_sc[:, half * cb:(half + 1) * cb]
            db_ref[half] = db_sc[:, half * cb:(half + 1) * cb]

    two = lambda rows_: [pl.BlockSpec((rows_, cb), lambda j: (0, j)), pl.BlockSpec((rows_, cb), lambda j: (0, nb + j))]
    both = lambda rows_: pl.BlockSpec((2, rows_, cb), lambda j: (0, 0, j))
    return pl.pallas_call(
        body, name="conv_geglu_bwd", grid=(nb,),
        in_specs=two(s) + two(k_taps) + two(1) + [pl.BlockSpec((s, cb), lambda j: (0, j))],
        out_specs=[both(s), both(k_taps), both(1)],
        out_shape=[jax.ShapeDtypeStruct((2, s, f), _MXU_DT), jax.ShapeDtypeStruct((2, k_taps, f), F32),
                   jax.ShapeDtypeStruct((2, 1, f), F32)],
        scratch_shapes=[pltpu.VMEM((s + pad, 2 * cb), F32), pltpu.VMEM((s + pad, 2 * cb), F32),
                        pltpu.VMEM((k_taps, 2 * cb), F32), pltpu.VMEM((1, 2 * cb), F32),
                        pltpu.VMEM((k_taps, 2 * cb), F32), pltpu.VMEM((1, 2 * cb), F32)],
        compiler_params=_cparams(("parallel",)),
    )(up, up, w, w, b, b, dact)


def _tril_mask():
    t = lax.broadcasted_iota(jnp.int32, (SG_CHUNK, SG_CHUNK), 0)
    s = lax.broadcasted_iota(jnp.int32, (SG_CHUNK, SG_CHUNK), 1)
    return t >= s


def _sg_mix_fwd(u, vn, w, bcol):
    s, c = u.shape
    gw = c // SG_GROUPS

    def body(u_ref, v_ref, w_ref, b_ref, o_ref):
        wm = jnp.where(_tril_mask(), w_ref[0], 0.0).astype(_MXU_DT)
        for n in range(s // SG_CHUNK):
            rows = slice(n * SG_CHUNK, (n + 1) * SG_CHUNK)
            mixed = jnp.dot(wm, v_ref[rows, :], preferred_element_type=F32) + b_ref[0]
            o_ref[rows, :] = (u_ref[rows, :] * mixed).astype(o_ref.dtype)

    return pl.pallas_call(
        body, name="sg_mix_fwd", grid=(SG_GROUPS,),
        in_specs=[pl.BlockSpec((s, gw), lambda g: (0, g)), pl.BlockSpec((s, gw), lambda g: (0, g)),
                  pl.BlockSpec((1, SG_CHUNK, SG_CHUNK), lambda g: (g, 0, 0)),
                  pl.BlockSpec((1, SG_CHUNK, 1), lambda g: (g, 0, 0))],
        out_specs=pl.BlockSpec((s, gw), lambda g: (0, g)),
        out_shape=jax.ShapeDtypeStruct((s, c), _MXU_DT),
        compiler_params=_cparams(("parallel",)),
    )(u, vn, w, bcol)


def _sg_mix_bwd(u, vn, w, bcol, dub):
    s, c = u.shape
    gw = c // SG_GROUPS

    def body(u_ref, v_ref, w_ref, b_ref, d_ref, du_ref, dv_ref, dw_ref, db_ref):
        mask = _tril_mask()
        wm = jnp.where(mask, w_ref[0], 0.0).astype(_MXU_DT)
        dw = jnp.zeros((SG_CHUNK, SG_CHUNK), F32)
        db = jnp.zeros((SG_CHUNK, 1), F32)
        for n in range(s // SG_CHUNK):
            rows = slice(n * SG_CHUNK, (n + 1) * SG_CHUNK)
            v = v_ref[rows, :]
            d = d_ref[rows, :].astype(F32)
            mixed = jnp.dot(wm, v, preferred_element_type=F32) + b_ref[0]
            du_ref[rows, :] = d * mixed
            dmix = d * u_ref[rows, :]
            dmix_lo = dmix.astype(_MXU_DT)
            dv_ref[rows, :] = lax.dot_general(wm, dmix_lo, (((0,), (0,)), ((), ())), preferred_element_type=F32)
            dw = dw + lax.dot_general(dmix_lo, v, (((1,), (1,)), ((), ())), preferred_element_type=F32)
            db = db + jnp.sum(dmix, axis=1, keepdims=True)
        dw_ref[0] = jnp.where(mask, dw, 0.0)
        db_ref[0] = db

    return pl.pallas_call(
        body, name="sg_mix_bwd", grid=(SG_GROUPS,),
        in_specs=[pl.BlockSpec((s, gw), lambda g: (0, g)), pl.BlockSpec((s, gw), lambda g: (0, g)),
                  pl.BlockSpec((1, SG_CHUNK, SG_CHUNK), lambda g: (g, 0, 0)),
                  pl.BlockSpec((1, SG_CHUNK, 1), lambda g: (g, 0, 0)), pl.BlockSpec((s, gw), lambda g: (0, g))],
        out_specs=[pl.BlockSpec((s, gw), lambda g: (0, g)), pl.BlockSpec((s, gw), lambda g: (0, g)),
                   pl.BlockSpec((1, SG_CHUNK, SG_CHUNK), lambda g: (g, 0, 0)),
                   pl.BlockSpec((1, SG_CHUNK, 1), lambda g: (g, 0, 0))],
        out_shape=[jax.ShapeDtypeStruct((s, c), F32), jax.ShapeDtypeStruct((s, c), F32),
                   jax.ShapeDtypeStruct((SG_GROUPS, SG_CHUNK, SG_CHUNK), F32),
                   jax.ShapeDtypeStruct((SG_GROUPS, SG_CHUNK, 1), F32)],
        compiler_params=_cparams(("parallel",)),
    )(u, vn, w, bcol, dub)


def _rope_fwd(q2, kv2, p, krm_idx, krs_idx, tc, ts):
    s = q2.shape[0]
    hw = N_HEADS * HEAD_PAD
    tm = 256

    def body(qm_ref, qs_ref, kn_ref, v_ref, krm_ref, krs_ref, tc_ref, ts_ref, q_ref, k_ref, vo_ref):
        tcv, tsv = tc_ref[...], ts_ref[...]
        kpe = krm_ref[...] * tcv + krs_ref[...] * tsv
        for h in range(N_HEADS):
            cols = slice(h * HEAD_PAD, (h + 1) * HEAD_PAD)
            q_ref[:, cols] = (qm_ref[:, cols] * tcv + qs_ref[:, cols] * tsv).astype(q_ref.dtype)
            k_ref[:, cols] = (kn_ref[:, cols] + kpe).astype(k_ref.dtype)
        vo_ref[...] = v_ref[...].astype(vo_ref.dtype)

    return pl.pallas_call(
        body, name="rope_fwd", grid=(s // tm,),
        in_specs=[_row_spec(tm, hw, 0), _row_spec(tm, hw, 1), _row_spec(tm, hw, 0), _row_spec(tm, hw, 1),
                  _row_spec(tm, HEAD_PAD, krm_idx), _row_spec(tm, HEAD_PAD, krs_idx),
                  _row_spec(tm, HEAD_PAD, 0), _row_spec(tm, HEAD_PAD, 0)],
        out_specs=[_row_spec(tm, hw, 0)] * 3,
        out_shape=[jax.ShapeDtypeStruct((s, hw), _MXU_DT)] * 3,
        compiler_params=_cparams(("parallel",)),
    )(q2, q2, kv2, kv2, p, p, tc, ts)


def _rope_bwd(dq, dk, dv, tc, ts, dp, kr_blk):
    s = dq.shape[0]
    hw = N_HEADS * HEAD_PAD
    tm = 256

    def body(dq_ref, dk_ref, dv_ref, tc_ref, ts_ref, dp_in, dq2_ref, dkv2_ref, dkr_ref):
        tcv, tsv = tc_ref[...], ts_ref[...]
        dkpe = jnp.zeros((tm, HEAD_PAD), F32)
        for h in range(N_HEADS):
            cols = slice(h * HEAD_PAD, (h + 1) * HEAD_PAD)
            dqh = dq_ref[:, cols]
            dq2_ref[:, cols] = (dqh * tcv).astype(dq2_ref.dtype)
            dq2_ref[:, hw + h * HEAD_PAD:hw + (h + 1) * HEAD_PAD] = (dqh * tsv).astype(dq2_ref.dtype)
            dkpe = dkpe + dk_ref[:, cols]
        dkv2_ref[:, 0:hw] = dk_ref[...].astype(dkv2_ref.dtype)
        dkv2_ref[:, hw:2 * hw] = dv_ref[...].astype(dkv2_ref.dtype)
        dkr_ref[:, 0:HEAD_PAD] = (dkpe * tcv).astype(dkr_ref.dtype)
        dkr_ref[:, HEAD_PAD:2 * HEAD_PAD] = (dkpe * tsv).astype(dkr_ref.dtype)
        dkr_ref[:, 2 * HEAD_PAD:3 * HEAD_PAD] = jnp.zeros((tm, HEAD_PAD), dkr_ref.dtype)

    return pl.pallas_call(
        body, name="rope_bwd", grid=(s // tm,),
        in_specs=[_row_spec(tm, hw, 0)] * 3 + [_row_spec(tm, HEAD_PAD, 0)] * 2 + [_ANY],
        out_specs=[_row_spec(tm, 2 * hw, 0), _row_spec(tm, 2 * hw, 0), _row_spec(tm, 3 * HEAD_PAD, kr_blk)],
        out_shape=[jax.ShapeDtypeStruct((s, 2 * hw), _MXU_DT), jax.ShapeDtypeStruct((s, 2 * hw), _MXU_DT),
                   jax.ShapeDtypeStruct(dp.shape, dp.dtype)],
        input_output_aliases={5: 2},
        compiler_params=_cparams(("parallel",)),
    )(dq, dk, dv, tc, ts, dp)


_ATTN_TQ = 256
_ATTN_SCALE = (QK_NOPE + QK_ROPE) ** -0.5


def _attn_probs(q, k, i):
    s = k.shape[0]
    sc = lax.dot_general(q, k, (((1,), (1,)), ((), ())), preferred_element_type=F32) * _ATTN_SCALE
    row = i * _ATTN_TQ + lax.broadcasted_iota(jnp.int32, (_ATTN_TQ, s), 0)
    col = lax.broadcasted_iota(jnp.int32, (_ATTN_TQ, s), 1)
    sc = jnp.where(row >= col, sc, jnp.finfo(F32).min)
    e = jnp.exp(sc - jnp.max(sc, axis=1, keepdims=True))
    return e / jnp.sum(e, axis=1, keepdims=True)


def _per_query_block(s, fn):
    i = pl.program_id(1)
    for n in range(s // _ATTN_TQ):
        @pl.when(i == n)
        def _(n=n):
            fn(n, (n + 1) * _ATTN_TQ)


def _attn_fwd(q, k, v):
    s = q.shape[0]

    def body(q_ref, k_ref, v_ref, o_ref):
        def block(n, kl):
            p = _attn_probs(q_ref[...], k_ref[0:kl, :], n)
            o_ref[...] = jnp.dot(p.astype(_MXU_DT), v_ref[0:kl, :], preferred_element_type=F32).astype(o_ref.dtype)

        _per_query_block(s, block)

    qspec = pl.BlockSpec((_ATTN_TQ, HEAD_PAD), lambda h, i: (i, h))
    kspec = pl.BlockSpec((s, HEAD_PAD), lambda h, i: (0, h))
    return pl.pallas_call(
        body, name="attn_fwd", grid=(N_HEADS, s // _ATTN_TQ),
        in_specs=[qspec, kspec, kspec], out_specs=qspec,
        out_shape=jax.ShapeDtypeStruct(q.shape, _MXU_DT),
        compiler_params=_cparams(("parallel", "parallel")),
    )(q, k, v)


def _attn_bwd(q, k, v, do):
    s = q.shape[0]

    def body(q_ref, k_ref, v_ref, do_ref, dq_ref, dk_ref, dv_ref):
        i = pl.program_id(1)

        @pl.when(i == 0)
        def _():
            dk_ref[...] = jnp.zeros_like(dk_ref)
            dv_ref[...] = jnp.zeros_like(dv_ref)

        def block(n, kl):
            qv, kv, dov = q_ref[...], k_ref[0:kl, :], do_ref[...]
            p = _attn_probs(qv, kv, n)
            dp = lax.dot_general(dov, v_ref[0:kl, :], (((1,), (1,)), ((), ())), preferred_element_type=F32)
            delta = jnp.sum(p * dp, axis=1, keepdims=True)
            ds = (p * (dp - delta) * _ATTN_SCALE).astype(_MXU_DT)
            dq_ref[...] = jnp.dot(ds, kv, preferred_element_type=F32)
            dk_ref[0:kl, :] += lax.dot_general(ds, qv, (((0,), (0,)), ((), ())), preferred_element_type=F32)
            dv_ref[0:kl, :] += lax.dot_general(p.astype(_MXU_DT), dov, (((0,), (0,)), ((), ())),
                                               preferred_element_type=F32)

        _per_query_block(s, block)

    qspec = pl.BlockSpec((_ATTN_TQ, HEAD_PAD), lambda h, i: (i, h))
    kspec = pl.BlockSpec((s, HEAD_PAD), lambda h, i: (0, h))
    return pl.pallas_call(
        body, name="attn_bwd", grid=(N_HEADS, s // _ATTN_TQ),
        in_specs=[qspec, kspec, kspec, qspec], out_specs=[qspec, kspec, kspec],
        out_shape=[jax.ShapeDtypeStruct(q.shape, F32)] * 3,
        compiler_params=_cparams(("parallel", "arbitrary")),
    )(q, k, v, do)


def _loss_head(y, target):
    s, d = y.shape
    tm = 256

    def body(y_ref, t_ref, loss_ref, dy_ref):
        @pl.when(pl.program_id(0) == 0)
        def _():
            loss_ref[...] = jnp.zeros_like(loss_ref)

        err = y_ref[...] - t_ref[...]
        loss_ref[...] += 0.5 * jnp.sum(jnp.mean(err * err, axis=-1, keepdims=True), axis=0, keepdims=True)
        dy_ref[...] = err * (1.0 / d)

    return pl.pallas_call(
        body, name="loss_head", grid=(s // tm,),
        in_specs=[_row_spec(tm, d, 0), _row_spec(tm, d, 0)],
        out_specs=[_full_spec((1, 1)), _row_spec(tm, d, 0)],
        out_shape=[jax.ShapeDtypeStruct((1, 1), F32), jax.ShapeDtypeStruct((s, d), F32)],
        compiler_params=_cparams(("arbitrary",)),
    )(y, target)


def _adamw_math(w, g, m, v):
    mn = ADAM_B1 * m + (1.0 - ADAM_B1) * g
    vn = ADAM_B2 * v + (1.0 - ADAM_B2) * (g * g)
    m_hat = mn / (1.0 - ADAM_B1 ** ADAM_STEP)
    v_hat = vn / (1.0 - ADAM_B2 ** ADAM_STEP)
    return -ADAM_LR * (m_hat / (jnp.sqrt(v_hat) + ADAM_EPS) + ADAM_WD * w), mn, vn


def _adamw(name, w, g, m, v):
    l, k, n = w.shape
    tk, tn = _slab_block(k, n)

    def body(w_ref, g_ref, m_ref, v_ref, d_ref, mo_ref, vo_ref):
        d_ref[...], mo_ref[...], vo_ref[...] = _adamw_math(w_ref[...], g_ref[...], m_ref[...], v_ref[...])

    spec = pl.BlockSpec((1, tk, tn), lambda i, j, jn: (i, j, jn))
    return pl.pallas_call(
        body, name=name, grid=(l, k // tk, n // tn), in_specs=[spec] * 4, out_specs=[spec] * 3,
        out_shape=[jax.ShapeDtypeStruct(w.shape, F32)] * 3,
        compiler_params=_cparams(("parallel", "parallel", "parallel")),
    )(w, g, m, v)


_ANY = pl.BlockSpec(memory_space=pl.ANY)


def _mesh_pos():
    return lax.axis_index("x"), lax.axis_index("y"), lax.axis_index("c")


def _other_chips(x, y):
    return [(1 - x, y), (x, 1 - y), (1 - x, 1 - y)]


def _remote(src, dst, send_sem, recv_sem, to):
    return pltpu.make_async_remote_copy(src_ref=src, dst_ref=dst, send_sem=send_sem, recv_sem=recv_sem,
                                        device_id=to, device_id_type=_MESH)


_HBM = pl.BlockSpec(memory_space=pltpu.HBM)
_SEM = pl.BlockSpec(memory_space=pltpu.SEMAPHORE)
_EFFECT = pltpu.SideEffectType.DATAFLOW_SIDE_EFFECTING


def _in_hbm(a):
    return pltpu.with_memory_space_constraint(a, pltpu.HBM)


def _chip_index(chip):
    return 2 * chip[0] + chip[1]


def _ag_forward(shards, lands, layer, have_remote):
    n = len(shards)

    def body(*refs):
        ins = refs[:n]
        outs = refs[2 * n:3 * n] if lands is not None else refs[n:2 * n]
        send_sems, recv_sems = refs[-2:]
        x, y, c = _mesh_pos()
        sibling = (x, y, 1 - c)
        chips = _other_chips(x, y)

        def copy(a, k, src, dst, to):
            return _remote(src, dst, send_sems.at[a, k], recv_sems.at[a, k], to)

        own = [copy(a, 6, ins[a], outs[a].at[2 * x + y], sibling) for a in range(n)]
        for cp in own:
            cp.start()

        @pl.when(c == layer)
        def _():
            started = []
            if not have_remote:
                for k, chip in enumerate(chips):
                    for a in range(n):
                        cp = copy(a, k, ins[a], outs[a].at[2 * x + y], (*chip, c))
                        cp.start()
                        started.append(cp)
            for k, chip in enumerate(chips):
                for a in range(n):
                    landed = outs[a].at[_chip_index(chip)]
                    if not have_remote:
                        copy(a, k, ins[a], landed, (*chip, c)).wait_recv()
                    cp = copy(a, 3 + k, landed, landed, sibling)
                    cp.start()
                    started.append(cp)
            for cp in started:
                cp.wait_send()

        @pl.when(c != layer)
        def _():
            for k, chip in enumerate(chips):
                for a in range(n):
                    copy(a, 3 + k, ins[a], outs[a].at[_chip_index(chip)], sibling).wait_recv()

        for cp in own:
            cp.wait()

    out_shape = [jax.ShapeDtypeStruct((4,) + a.shape, a.dtype) for a in shards]
    extra = [] if lands is None else list(lands)
    return pl.pallas_call(
        body, name="ag_layer%d" % layer, in_specs=[_ANY] * (n + len(extra)), out_specs=[_ANY] * n,
        out_shape=out_shape, input_output_aliases={n + a: a for a in range(len(extra))},
        scratch_shapes=[pltpu.SemaphoreType.DMA((n, 7)), pltpu.SemaphoreType.DMA((n, 7))],
    )(*shards, *extra)


def _owner_sends(owners, srcs, dsts, send_sems, recv_sems, do):
    x, y, c = _mesh_pos()
    for core in (0, 1):
        mine = [a for a in range(len(srcs)) if owners[a] == core]
        if mine:
            @pl.when(c == core)
            def _(mine=mine):
                for k, chip in enumerate(_other_chips(x, y)):
                    for a in mine:
                        do(_remote(srcs[a](chip, k), dsts[a](chip, k), send_sems.at[3 * a + k],
                                   recv_sems.at[3 * a + k], (*chip, c)))


def _split_start(name, owners, sources, land_shapes, src_of, dst_of, after):
    n = len(sources)

    def body(*refs):
        srcs, lands = refs[:n], refs[n:2 * n]
        send_sems, recv_sems = refs[2 * n + 1], refs[2 * n + 2]
        token = refs[-1]
        _owner_sends(owners, [functools.partial(src_of, srcs[a]) for a in range(n)],
                     [functools.partial(dst_of, lands[a]) for a in range(n)], send_sems, recv_sems,
                     lambda cp: cp.start())
        token[...] = jnp.zeros_like(token)

    lands = [_in_hbm(lax.empty(s.shape, s.dtype)) for s in land_shapes]
    outs = pl.pallas_call(
        body, name=name,
        out_shape=([pltpu.SemaphoreType.DMA((3 * n,)), pltpu.SemaphoreType.DMA((3 * n,))]
                   + [pltpu.HBM(a.shape, a.dtype) for a in sources] + [pltpu.HBM(s.shape, s.dtype) for s in land_shapes]
                   + [jax.ShapeDtypeStruct((8, _LANES), F32)]),
        in_specs=[_HBM] * (2 * n) + [_ANY],
        out_specs=[_SEM, _SEM] + [_HBM] * (2 * n) + [pl.BlockSpec(memory_space=pltpu.VMEM)],
        input_output_aliases={i: 2 + i for i in range(2 * n)},
        compiler_params=pltpu.CompilerParams(has_side_effects=_EFFECT),
    )(*[_in_hbm(a) for a in sources], *lands, after)
    return outs[0], outs[1], outs[2:2 + n], outs[2 + n:2 + 2 * n], outs[-1]


def _split_wait(name, owners, send_sems, recv_sems, sources, lands, after, src_of, dst_of):
    n = len(sources)

    def body(*refs):
        srcs, lnds = refs[:n], refs[n:2 * n]
        s_sems, r_sems = refs[2 * n], refs[2 * n + 1]

        def wait(cp):
            cp.wait_send()
            cp.wait_recv()

        _owner_sends(owners, [functools.partial(src_of, srcs[a]) for a in range(n)],
                     [functools.partial(dst_of, lnds[a]) for a in range(n)], s_sems, r_sems, wait)

    outs = pl.pallas_call(
        body, name=name,
        out_shape=[pltpu.HBM(a.shape, a.dtype) for a in sources] + [pltpu.HBM(a.shape, a.dtype) for a in lands],
        in_specs=[_HBM] * (2 * n) + [_SEM, _SEM, _ANY], out_specs=[_HBM] * (2 * n),
        input_output_aliases={i: i for i in range(2 * n)},
        compiler_params=pltpu.CompilerParams(has_side_effects=_EFFECT),
    )(*sources, *lands, send_sems, recv_sems, after)
    return outs[:n], outs[n:]


def _pair_exchange(name, arrays, owners, to_owner):
    n = len(arrays)

    def body(*refs):
        ins, outs, (send_sems, recv_sems) = refs[:n], refs[n:2 * n], refs[2 * n:]
        x, y, c = _mesh_pos()
        copies = [_remote(ins[a], outs[a], send_sems.at[a], recv_sems.at[a], (x, y, 1 - c)) for a in range(n)]
        for core in (0, 1):
            sends = [copies[a] for a in range(n) if (owners[a] != core) == to_owner]
            recvs = [copies[a] for a in range(n) if (owners[a] == core) == to_owner]

            @pl.when(c == core)
            def _(sends=sends, recvs=recvs):
                for cp in sends:
                    cp.start()
                for cp in recvs:
                    cp.wait_recv()
                for cp in sends:
                    cp.wait_send()

    return pl.pallas_call(
        body, name=name, in_specs=[_ANY] * n, out_specs=[_ANY] * n,
        out_shape=[jax.ShapeDtypeStruct(g.shape, g.dtype) for g in arrays],
        input_output_aliases={} if to_owner else {a: a for a in range(n)},
        scratch_shapes=[pltpu.SemaphoreType.DMA((n,)), pltpu.SemaphoreType.DMA((n,))],
    )(*arrays)


def _rs_scatter(name, ts, owners):
    n = len(ts)

    def body(*refs):
        ins, outs, (send_sems, recv_sems) = refs[:n], refs[n:2 * n], refs[2 * n:]
        srcs = [lambda chip, k, r=ins[a]: r.at[_chip_index(chip)] for a in range(n)]
        dsts = [lambda chip, k, r=outs[a]: r.at[k] for a in range(n)]
        _owner_sends(owners, srcs, dsts, send_sems, recv_sems, lambda cp: cp.start())
        _owner_sends(owners, srcs, dsts, send_sems, recv_sems, lambda cp: cp.wait())

    return pl.pallas_call(
        body, name=name, in_specs=[_ANY] * n, out_specs=[_ANY] * n,
        out_shape=[jax.ShapeDtypeStruct((3,) + t.shape[1:], t.dtype) for t in ts],
        scratch_shapes=[pltpu.SemaphoreType.DMA((3 * n,)), pltpu.SemaphoreType.DMA((3 * n,))],
    )(*ts)


def _add_pair(name, g, a, flags):
    _, k, n = g.shape
    tk, tn = _slab_block(k, n)

    def body(flags_ref, g_ref, a_ref, o_ref):
        o_ref[...] = (g_ref[...] + a_ref[...]).astype(o_ref.dtype)

    spec = pl.BlockSpec((1, tk, tn), lambda j, i, jn, fl: (j * fl[1], i * fl[1], jn * fl[1]))
    return pl.pallas_call(
        body, name=name,
        grid_spec=pltpu.PrefetchScalarGridSpec(num_scalar_prefetch=1, grid=(4, k // tk, n // tn),
                                               in_specs=[spec, spec], out_specs=spec),
        out_shape=jax.ShapeDtypeStruct(g.shape, _RS_DT),
        compiler_params=_cparams(("arbitrary", "arbitrary", "arbitrary")),
    )(flags, g, a)


def _add_quads(name, t, b, layer, flags, into):
    _, k, n = t.shape
    tk, tn = _slab_block(k, n)

    def body(flags_ref, t_ref, b_ref, *rest):
        o_ref = rest[-1]
        f = lambda v: v.astype(F32)
        o_ref[0] = ((f(t_ref[0]) + f(b_ref[0])) + f(b_ref[1])) + f(b_ref[2])

    extra = [] if into is None else [into]
    return pl.pallas_call(
        body, name=name,
        grid_spec=pltpu.PrefetchScalarGridSpec(
            num_scalar_prefetch=1, grid=(k // tk, n // tn),
            in_specs=[pl.BlockSpec((1, tk, tn), lambda i, jn, fl: (fl[0], i * fl[1], jn * fl[1])),
                      pl.BlockSpec((3, tk, tn), lambda i, jn, fl: (0, i * fl[1], jn * fl[1]))] + [_ANY] * len(extra),
            out_specs=pl.BlockSpec((1, tk, tn), lambda i, jn, fl: (layer, i * fl[1], jn * fl[1]))),
        out_shape=jax.ShapeDtypeStruct((2, k, n), F32),
        input_output_aliases={3: 0} if extra else {},
        compiler_params=_cparams(("arbitrary", "arbitrary")),
    )(flags, t, b, *extra)


def _slab_block(k, n, itemsize=4):
    tk = (1 << 20) // (n * itemsize) // 16 * 16
    while 0 < tk < k and k % tk:
        tk -= 16
    if 0 < tk < k:
        return tk, n
    if k * n * itemsize <= (2 << 20) or n % _LANES:
        return k, n
    tn = max(_LANES, (1 << 20) // (k * itemsize) // _LANES * _LANES)
    while n % tn:
        tn -= _LANES
    return k, tn


def _all_reduce_adamw(gs, ws, ms, vs):
    n = len(gs)

    def body(*refs):
        g_refs, w_refs, m_refs, v_refs = (refs[i * n:(i + 1) * n] for i in range(4))
        gsum, delta, m_out, v_out = (refs[(4 + i) * n:(5 + i) * n] for i in range(4))
        slots = refs[8 * n:9 * n]
        send_sems, recv_sems = refs[9 * n:]
        x, y, c = _mesh_pos()
        me = 4 * x + 2 * y + c
        copies = []
        for rel in range(1, 8):
            bx, by, bc = (rel >> 2) & 1, (rel >> 1) & 1, rel & 1
            peer = (1 - x if bx else x, 1 - y if by else y, 1 - c if bc else c)
            for a in range(n):
                cp = _remote(g_refs[a], slots[a].at[me], send_sems.at[a, rel - 1], recv_sems.at[a, rel - 1], peer)
                cp.start()
                copies.append(cp)
        for a in range(n):
            slots[a][me] = g_refs[a][...]
        for cp in copies:
            cp.wait()
        for a in range(n):
            acc = slots[a][0]
            for d in range(1, 8):
                acc = acc + slots[a][d]
            gsum[a][...] = acc
            delta[a][...], m_out[a][...], v_out[a][...] = _adamw_math(w_refs[a][...], acc, m_refs[a][...],
                                                                      v_refs[a][...])

    vmem = pl.BlockSpec(memory_space=pltpu.VMEM)
    outs = pl.pallas_call(
        body, name="all_reduce_adamw", in_specs=[vmem] * (4 * n), out_specs=[vmem] * (4 * n),
        out_shape=[jax.ShapeDtypeStruct(g.shape, F32) for g in gs] * 4,
        scratch_shapes=([pltpu.VMEM((8,) + g.shape, F32) for g in gs]
                        + [pltpu.SemaphoreType.DMA((n, 7)), pltpu.SemaphoreType.DMA((n, 7))]),
        compiler_params=pltpu.CompilerParams(vmem_limit_bytes=_VMEM_LIMIT),
    )(*gs, *ws, *ms, *vs)
    return outs[:n], outs[n:2 * n], outs[2 * n:3 * n], outs[3 * n:]


def _swap_rope(a):
    h = QK_ROPE // 2
    return jnp.concatenate([a[..., h:], a[..., :h]], axis=-1)


def _swap_rope_rows(a):
    h = QK_ROPE // 2
    return jnp.concatenate([a[h:], a[:h]], axis=0)


_FFN_CB = 256


def _interleave_rows(a, cb):
    r, c = a.shape
    return a.reshape(2, r // (2 * cb), cb, c).transpose(1, 0, 2, 3).reshape(r, c)


def _deinterleave_rows(a, cb):
    r, c = a.shape
    return a.reshape(r // (2 * cb), 2, cb, c).transpose(1, 0, 2, 3).reshape(r, c)


class _InLayout:
    def __init__(self, d):
        self.d = d
        self.gates = 0
        self.a = 3 * d
        self.b = 4 * d
        self.kv = 5 * d
        self.q = self.kv + 256
        self.krm = self.q + 384
        self.krs = self.krm + HEAD_PAD
        self.width = self.krs + 2 * HEAD_PAD


def _prep_layer(wl, d):
    lay = _InLayout(d)
    w_in = wl["w_in"]
    dt = w_in.dtype
    a, b = w_in[0:d], w_in[d:2 * d]
    q, kv = w_in[2 * d:2 * d + 384], w_in[2 * d + 384:2 * d + 640]
    kr = w_in[2 * d + 640:2 * d + 640 + QK_ROPE]
    gates = w_in[2 * d + 640 + QK_ROPE:]
    z = lambda n: jnp.zeros((n, d), dt)
    krm = jnp.concatenate([z(QK_NOPE), kr, z(HEAD_PAD - QK_NOPE - QK_ROPE)], axis=0)
    krs = jnp.concatenate([z(QK_NOPE), _swap_rope_rows(kr), z(HEAD_PAD - QK_NOPE - QK_ROPE)], axis=0)
    out = dict(wl)
    out["w_in"] = jnp.concatenate([gates, _interleave_rows(a, _LANES), b, kv, q, krm, krs,
                                   z(lay.width - lay.krs - HEAD_PAD)], axis=0)
    uq = wl["mla_w_uq"].reshape(-1, N_HEADS, QK_NOPE + QK_ROPE)
    nq = uq.shape[0]
    nope, pe = uq[..., :QK_NOPE], uq[..., QK_NOPE:]
    zq = lambda n: jnp.zeros((nq, N_HEADS, n), dt)
    main = jnp.concatenate([nope, pe, zq(HEAD_PAD - QK_NOPE - QK_ROPE)], axis=-1).reshape(nq, -1)
    swapped = jnp.concatenate([zq(QK_NOPE), _swap_rope(pe), zq(HEAD_PAD - QK_NOPE - QK_ROPE)], axis=-1).reshape(nq, -1)
    out["mla_w_uq"] = jnp.concatenate([main, swapped], axis=1)
    ukv = wl["mla_w_ukv"].reshape(-1, N_HEADS, QK_NOPE + V_HEAD)
    nkv = ukv.shape[0]
    zk = jnp.zeros((nkv, N_HEADS, HEAD_PAD - QK_NOPE), dt)
    zv = jnp.zeros((nkv, N_HEADS, HEAD_PAD - V_HEAD), dt)
    out["mla_w_ukv"] = jnp.concatenate([jnp.concatenate([ukv[..., :QK_NOPE], zk], axis=-1).reshape(nkv, -1),
                                        jnp.concatenate([ukv[..., QK_NOPE:], zv], axis=-1).reshape(nkv, -1)], axis=1)
    wo = wl["mla_w_o"].reshape(N_HEADS, V_HEAD, -1)
    out["mla_w_o"] = jnp.concatenate([wo, jnp.zeros((N_HEADS, HEAD_PAD - V_HEAD, wo.shape[-1]), dt)],
                                     axis=1).reshape(N_HEADS * HEAD_PAD, -1)
    return out


def _unprep_grads(g, d):
    lay = _InLayout(d)
    gi = g["w_in"]
    kr = (gi[lay.krm + QK_NOPE:lay.krm + QK_NOPE + QK_ROPE]
          + _swap_rope_rows(gi[lay.krs + QK_NOPE:lay.krs + QK_NOPE + QK_ROPE]))
    out = dict(g)
    out["w_in"] = jnp.concatenate([_deinterleave_rows(gi[lay.a:lay.a + d], _LANES), gi[lay.b:lay.b + d],
                                   gi[lay.q:lay.q + 384], gi[lay.kv:lay.kv + 256], kr,
                                   gi[lay.gates:lay.gates + 3 * d]], axis=0)
    hw = N_HEADS * HEAD_PAD
    gq = g["mla_w_uq"]
    nq = gq.shape[0]
    main = gq[:, :hw].reshape(nq, N_HEADS, HEAD_PAD)
    swapped = gq[:, hw:].reshape(nq, N_HEADS, HEAD_PAD)
    pe = main[..., QK_NOPE:QK_NOPE + QK_ROPE] + _swap_rope(swapped[..., QK_NOPE:QK_NOPE + QK_ROPE])
    out["mla_w_uq"] = jnp.concatenate([main[..., :QK_NOPE], pe], axis=-1).reshape(nq, -1)
    gkv = g["mla_w_ukv"]
    nkv = gkv.shape[0]
    out["mla_w_ukv"] = jnp.concatenate([gkv[:, :hw].reshape(nkv, N_HEADS, HEAD_PAD)[..., :QK_NOPE],
                                        gkv[:, hw:].reshape(nkv, N_HEADS, HEAD_PAD)[..., :V_HEAD]],
                                       axis=-1).reshape(nkv, -1)
    go = g["mla_w_o"]
    out["mla_w_o"] = go.reshape(N_HEADS, HEAD_PAD, -1)[:, :V_HEAD].reshape(N_HEADS * V_HEAD, -1)
    return out


def _rope_tables(positions):
    s = positions.shape[0]
    inv = ROPE_THETA ** (-jnp.arange(0, QK_ROPE, 2, dtype=F32) / QK_ROPE)
    ang = positions.astype(F32)[:, None] * inv
    cos, sin = jnp.cos(ang), jnp.sin(ang)
    tail = jnp.zeros((s, HEAD_PAD - QK_NOPE - QK_ROPE), F32)
    tc = jnp.concatenate([jnp.ones((s, QK_NOPE), F32), cos, cos, tail], axis=1)
    ts = jnp.concatenate([jnp.zeros((s, QK_NOPE), F32), -sin, sin, tail], axis=1)
    return tc, ts


def _row(v):
    return v.reshape(1, -1)


def _layer_fwd(x, h, w, g_next, tc, ts):
    d = x.shape[1]
    lay = _InLayout(d)
    cw = d // 2
    blk = lambda off, width: off // width
    p = _mm("mm_in", h, w["w_in"], tb=True)
    z1 = _glu_conv_fwd(p, blk(lay.a, 2 * _LANES), w["conv_dw_w"], _row(w["conv_dw_b"]))
    ln_a = [_row(w["conv_ln_g"]), _row(w["conv_ln_b"])]
    (z3,) = _row_fwd("ln_silu_fwd", _f_ln_silu, [(z1, cw, 0)], ln_a, [(cw, _MXU_DT)])
    ya = _mm("mm_conv_out", z3, w["conv_out_w"])
    ln_b = [_row(w["sg_ln_g"]), _row(w["sg_ln_b"])]
    u, vn = _row_fwd("sg_pre_fwd", _f_sg_pre, [(p, cw, blk(lay.b, cw)), (p, cw, blk(lay.b + cw, cw))], ln_b,
                     [(cw, F32), (cw, _MXU_DT)])
    bcol = w["sg_b"].reshape(SG_GROUPS, SG_CHUNK, 1)
    ub = _sg_mix_fwd(u, vn, w["sg_w"], bcol)
    yb = _mm("mm_sg_out", ub, w["sg_out_w"])
    (qn,) = _row_fwd("q_norm_fwd", _f_rms, [(p, 384, blk(lay.q, 384))], [_row(w["mla_q_norm_g"])], [(384, _MXU_DT)])
    (kvn,) = _row_fwd("kv_norm_fwd", _f_rms, [(p, 256, blk(lay.kv, 256))], [_row(w["mla_kv_norm_g"])],
                      [(256, _MXU_DT)])
    q2 = _mm("mm_uq", qn, w["mla_w_uq"])
    kv2 = _mm("mm_ukv", kvn, w["mla_w_ukv"])
    qf, kf, vf = _rope_fwd(q2, kv2, p, blk(lay.krm, HEAD_PAD), blk(lay.krs, HEAD_PAD), tc, ts)
    o = _attn_fwd(qf, kf, vf)
    yc = _mm("mm_o", o, w["mla_w_o"])
    gate_rows = [(p, d, 0), (p, d, 1), (p, d, 2)]
    (merged,) = _row_fwd("merge_fwd", _f_merge, gate_rows + [(ya, d, 0), (yb, d, 0), (yc, d, 0)], [], [(d, _MXU_DT)])
    t = _mm("mm_out", merged, w["w_out"])
    x1, h2 = _row_fwd("resid_mix_fwd", _f_resid_rms_rms, [(x, d, 0), (t, d, 0)],
                      [_row(w["mix_post_g"]), _row(w["ffn_pre_g"])], [(d, F32), (d, _MXU_DT)])
    up = _mm("mm_up", h2, w["ffn_w_up"])
    act = _conv_geglu_fwd(up, w["ffn_dw_w"], _row(w["ffn_dw_b"]))
    dn = _mm("mm_down", act, w["ffn_w_down"])
    if g_next is None:
        (x2,) = _row_fwd("resid_ffn_last_fwd", _f_resid_rms, [(x1, d, 0), (dn, d, 0)], [_row(w["ffn_post_g"])],
                         [(d, F32)])
        h_next = None
    else:
        x2, h_next = _row_fwd("resid_ffn_fwd", _f_resid_rms_rms, [(x1, d, 0), (dn, d, 0)],
                              [_row(w["ffn_post_g"]), _row(g_next)], [(d, F32), (d, _MXU_DT)])
    saved = dict(x=x, h=h, p=p, z1=z1, z3=z3, ya=ya, u=u, vn=vn, ub=ub, yb=yb, qn=qn, kvn=kvn, qf=qf, kf=kf, vf=vf, o=o,
                 yc=yc, merged=merged, t=t, x1=x1, h2=h2, up=up, act=act, dn=dn, bcol=bcol)
    return x2, h_next, saved


def _layer_bwd(dx2, dh_next, w, g_next, sv, tc, ts):
    d = dx2.shape[1]
    lay = _InLayout(d)
    cw = d // 2
    blk = lambda off, width: off // width
    lo = _MXU_DT
    g = {}
    x1, dn = sv["x1"], sv["dn"]
    if dh_next is None:
        dx1, ddn, g["ffn_post_g"] = _row_bwd("resid_ffn_last_bwd", _f_resid_rms, [(x1, d, 0), (dn, d, 0)],
                                             [_row(w["ffn_post_g"])], [(dx2, d, 0)], [(0, F32), (1, lo)], [0])
    else:
        dx1, ddn, g["ffn_post_g"], g["next_pre_g"] = _row_bwd(
            "resid_ffn_bwd", _f_resid_rms_rms, [(x1, d, 0), (dn, d, 0)], [_row(w["ffn_post_g"]), _row(g_next)],
            [(dx2, d, 0), (dh_next, d, 0)], [(0, F32), (1, lo)], [0, 1])
    dact = _mm("mm_down_dx", ddn, w["ffn_w_down"], tb=True)
    g["ffn_w_down"] = _mm("mm_down_dw", sv["act"], ddn, ta=True)
    dup, dw_halves, db_halves = _conv_geglu_bwd(sv["up"], w["ffn_dw_w"], _row(w["ffn_dw_b"]), dact)
    g["ffn_dw_w"] = jnp.concatenate([dw_halves[0], dw_halves[1]], axis=1)
    g["ffn_dw_b"] = jnp.concatenate([db_halves[0], db_halves[1]], axis=1)
    dh2 = _mm("mm_up_dx", dup, w["ffn_w_up"], tb=True, a_halves=True)
    g["ffn_w_up"] = _mm("mm_up_dw", sv["h2"], dup, ta=True, b_halves=True, out_quarters=True)
    dx, dt, g["mix_post_g"], g["ffn_pre_g"] = _row_bwd(
        "resid_mix_bwd", _f_resid_rms_rms, [(sv["x"], d, 0), (sv["t"], d, 0)],
        [_row(w["mix_post_g"]), _row(w["ffn_pre_g"])], [(dx1, d, 0), (dh2, d, 0)], [(0, F32), (1, lo)], [0, 1])
    dmerged = _mm("mm_out_dx", dt, w["w_out"], tb=True)
    g["w_out"] = _mm("mm_out_dw", sv["merged"], dt, ta=True)
    p = sv["p"]
    gate_rows = [(p, d, 0), (p, d, 1), (p, d, 2)]
    dp, dya, dyb, dyc = _row_bwd(
        "merge_bwd", _f_merge, gate_rows + [(sv["ya"], d, 0), (sv["yb"], d, 0), (sv["yc"], d, 0)], [],
        [(dmerged, d, 0)], [((0, 1, 2), lo), ((3,), lo), ((4,), lo), ((5,), lo)], [], place=(lay.width, 0))
    do = _mm("mm_o_dx", dyc, w["mla_w_o"], tb=True, out_dtype=lo)
    g["mla_w_o"] = _mm("mm_o_dw", sv["o"], dyc, ta=True)
    dqf, dkf, dvf = _attn_bwd(sv["qf"], sv["kf"], sv["vf"], do)
    dq2, dkv2, dp = _rope_bwd(dqf, dkf, dvf, tc, ts, dp, blk(lay.krm, 3 * HEAD_PAD))
    dkvn = _mm("mm_ukv_dx", dkv2, w["mla_w_ukv"], tb=True)
    g["mla_w_ukv"] = _mm("mm_ukv_dw", sv["kvn"], dkv2, ta=True)
    dqn = _mm("mm_uq_dx", dq2, w["mla_w_uq"], tb=True)
    g["mla_w_uq"] = _mm("mm_uq_dw", sv["qn"], dq2, ta=True)
    dp, g["mla_q_norm_g"] = _row_bwd("q_norm_bwd", _f_rms, [(p, 384, blk(lay.q, 384))], [_row(w["mla_q_norm_g"])],
                                     [(dqn, 384, 0)], [((0,), lo)], [0], place=(lay.width, blk(lay.q, 384)), into=dp)
    dp, g["mla_kv_norm_g"] = _row_bwd("kv_norm_bwd", _f_rms, [(p, 256, blk(lay.kv, 256))],
                                      [_row(w["mla_kv_norm_g"])], [(dkvn, 256, 0)], [((0,), lo)], [0],
                                      place=(lay.width, blk(lay.kv, 256)), into=dp)
    dub = _mm("mm_sg_out_dx", dyb, w["sg_out_w"], tb=True)
    g["sg_out_w"] = _mm("mm_sg_out_dw", sv["ub"], dyb, ta=True)
    du, dvn, g["sg_w"], dbcol = _sg_mix_bwd(sv["u"], sv["vn"], w["sg_w"], sv["bcol"], dub)
    g["sg_b"] = dbcol.reshape(SG_GROUPS, SG_CHUNK)
    dp, g["sg_ln_g"], g["sg_ln_b"] = _row_bwd(
        "sg_pre_bwd", _f_sg_pre, [(p, cw, blk(lay.b, cw)), (p, cw, blk(lay.b + cw, cw))],
        [_row(w["sg_ln_g"]), _row(w["sg_ln_b"])], [(du, cw, 0), (dvn, cw, 0)], [((0, 1), lo)], [0, 1],
        place=(lay.width, blk(lay.b, d)), into=dp)
    dz3 = _mm("mm_conv_out_dx", dya, w["conv_out_w"], tb=True)
    g["conv_out_w"] = _mm("mm_conv_out_dw", sv["z3"], dya, ta=True)
    dz1, g["conv_ln_g"], g["conv_ln_b"] = _row_bwd(
        "ln_silu_bwd", _f_ln_silu, [(sv["z1"], cw, 0)], [_row(w["conv_ln_g"]), _row(w["conv_ln_b"])],
        [(dz3, cw, 0)], [((0,), F32)], [0, 1])
    dp, g["conv_dw_w"], g["conv_dw_b"] = _glu_conv_bwd(p, blk(lay.a, 2 * _LANES), w["conv_dw_w"], dz1, dp)
    dh = _mm("mm_in_dx", dp, w["w_in"])
    g["w_in"] = _mm("mm_in_dw", dp, sv["h"], ta=True)
    return dx, dh, g


def _local_step(x, positions, target, layers):
    d = x.shape[1]
    tc, ts = _rope_tables(positions)
    ws = [_prep_layer(wl, d) for wl in layers]
    depth = len(ws)
    (h,) = _row_fwd("rms_first_fwd", _f_rms, [(x, d, 0)], [_row(ws[0]["mix_pre_g"])], [(d, _MXU_DT)])
    saved = []
    for l in range(depth):
        g_next = ws[l + 1]["mix_pre_g"] if l + 1 < depth else None
        x, h, sv = _layer_fwd(x, h, ws[l], g_next, tc, ts)
        saved.append(sv)
    loss, dx = _loss_head(x, target)
    grads = [None] * depth
    dh = None
    for l in reversed(range(depth)):
        g_next = ws[l + 1]["mix_pre_g"] if l + 1 < depth else None
        dx, dh, g = _layer_bwd(dx, dh, ws[l], g_next, saved[l], tc, ts)
        if "next_pre_g" in g:
            grads[l + 1]["mix_pre_g"] = g.pop("next_pre_g")
        grads[l] = g
    x0 = saved[0]["x"]
    grad_x, grads[0]["mix_pre_g"] = _row_bwd("rms_first_bwd", _f_x_rms, [(x0, d, 0)], [_row(ws[0]["mix_pre_g"])],
                                             [(dx, d, 0), (dh, d, 0)], [(0, F32)], [0])
    return loss, grad_x, [_unprep_grads(g, d) for g in grads]


_MATRICES = ("w_in", "conv_out_w", "sg_out_w", "mla_w_uq", "mla_w_ukv", "mla_w_o", "w_out", "ffn_w_up", "ffn_w_down")
_F32_GATHERED = ("conv_dw_w", "ffn_dw_w")
_RS_DT = jnp.bfloat16


_ROW_SHARDED = SHARDED_MID + ("w_in",)


_GATHERED = SHARDED + _F32_GATHERED
_RS_CORE0 = ("w_in", "ffn_w_down")


def _layer_shards(w, l):
    hi = {n: w[n][l].astype(jnp.bfloat16) for n in SHARDED}
    lo = [(w[n][l] - hi[n].astype(F32)).astype(jnp.bfloat16) for n in _F32_GATHERED]
    return [hi[n] for n in SHARDED] + lo


def _layer_weights(w, l, gathered):
    wl = {n: w[n][l] for n in REPLICATED}
    for n, g in zip(_GATHERED, gathered):
        whole = jnp.concatenate([g[j] for j in range(4)], axis=0 if n in _ROW_SHARDED else 1)
        if n in wl:
            wl[n] = wl[n].astype(F32) + whole.astype(F32)
        else:
            wl[n] = whole.astype(_MXU_DT) if n in _MATRICES else whole
    return wl


def _by_destination(name, gl):
    if gl.ndim == 3:
        return gl
    k, n = gl.shape
    if name in _ROW_SHARDED:
        return gl.reshape(4, k // 4, n)
    return gl.reshape(k, 4, n // 4).transpose(1, 0, 2)


def kernel(x, positions, mix_pre_g, mix_post_g, ffn_pre_g, ffn_post_g, w_in, conv_dw_w, conv_dw_b, conv_ln_g, conv_ln_b, conv_out_w, sg_ln_g, sg_ln_b, sg_w, sg_b, sg_out_w, mla_q_norm_g, mla_w_uq, mla_kv_norm_g, mla_w_ukv, mla_w_o, w_out, ffn_w_up, ffn_dw_w, ffn_dw_b, ffn_w_down, loss_target, m_mix_pre_g, m_mix_post_g, m_ffn_pre_g, m_ffn_post_g, m_w_in, m_conv_dw_w, m_conv_dw_b, m_conv_ln_g, m_conv_ln_b, m_conv_out_w, m_sg_ln_g, m_sg_ln_b, m_sg_w, m_sg_b, m_sg_out_w, m_mla_q_norm_g, m_mla_w_uq, m_mla_kv_norm_g, m_mla_w_ukv, m_mla_w_o, m_w_out, m_ffn_w_up, m_ffn_dw_w, m_ffn_dw_b, m_ffn_w_down, v_mix_pre_g, v_mix_post_g, v_ffn_pre_g, v_ffn_post_g, v_w_in, v_conv_dw_w, v_conv_dw_b, v_conv_ln_g, v_conv_ln_b, v_conv_out_w, v_sg_ln_g, v_sg_ln_b, v_sg_w, v_sg_b, v_sg_out_w, v_mla_q_norm_g, v_mla_w_uq, v_mla_kv_norm_g, v_mla_w_ukv, v_mla_w_o, v_w_out, v_ffn_w_up, v_ffn_dw_w, v_ffn_dw_b, v_ffn_w_down):
    args = dict(locals())
    w = {n: args[n] for n in WEIGHTS}
    m = {n: args["m_" + n] for n in WEIGHTS}
    v = {n: args["v_" + n] for n in WEIGHTS}
    depth = mix_pre_g.shape[0]

    assert depth == 2, "the two cores of a chip split the communication by layer"
    for t in (w, m, v):
        t["w_in"] = jnp.swapaxes(t["w_in"], 1, 2)
    d = x.shape[-1]
    mesh_x, mesh_y, mesh_c = _mesh_pos()
    my_chip = 2 * mesh_x + mesh_y
    names = list(SHARDED)
    whole = lambda ref, chip, k: ref
    to_my_slot = lambda ref, chip, k: ref.at[2 * lax.axis_index("x") + lax.axis_index("y")]
    block_of_chip = lambda ref, chip, k: ref.at[_chip_index(chip)]
    slot_k = lambda ref, chip, k: ref.at[k]

    gathered0 = _ag_forward(_layer_shards(w, 0), None, 0, False)
    shards1 = _layer_shards(w, 1)
    lands = [jax.ShapeDtypeStruct((4,) + a.shape, a.dtype) for a in shards1]
    ag_owner = [1] * len(shards1)
    sems_s, sems_r, shards1, lands1, token = _split_start("ag1_start", ag_owner, shards1, lands, whole, to_my_slot,
                                                          gathered0[0])
    tc, ts = _rope_tables(positions[0])
    ws0 = _prep_layer(_layer_weights(w, 0, gathered0), d)
    x0 = x[0] + token[0, 0]
    (h0,) = _row_fwd("rms_first_fwd", _f_rms, [(x0, d, 0)], [_row(ws0["mix_pre_g"])], [(d, _MXU_DT)])
    x1, h1, sv0 = _layer_fwd(x0, h0, ws0, w["mix_pre_g"][1], tc, ts)
    shards1, lands1 = _split_wait("ag1_wait", ag_owner, sems_s, sems_r, shards1, lands1, x1, whole, to_my_slot)
    ws1 = _prep_layer(_layer_weights(w, 1, _ag_forward(shards1, lands1, 1, True)), d)
    x2, _, sv1 = _layer_fwd(x1, h1, ws1, None, tc, ts)
    loss, dx = _loss_head(x2, loss_target[0])
    loss = lax.psum(loss[0, 0], ("x", "y", "c"))

    owners = [0 if n in _RS_CORE0 else 1 for n in names]
    flags = [jnp.stack([my_chip, (mesh_c == o).astype(jnp.int32)]).astype(jnp.int32) for o in owners]

    def pair_sums(l, grads_l):
        gd = [_by_destination(n, grads_l[n]) for n in names]
        got = _pair_exchange("rs_swap%d" % l, gd, owners, True)
        return [_add_pair("rs_pair%d_%s" % (l, n), g, a, fl) for n, g, a, fl in zip(names, gd, got, flags)]

    dx, dh, gk1 = _layer_bwd(dx, None, ws1, None, sv1, tc, ts)
    grads1 = _unprep_grads(gk1, d)
    t1 = pair_sums(1, grads1)
    lands = [jax.ShapeDtypeStruct((3,) + t.shape[1:], t.dtype) for t in t1]
    sems_s, sems_r, t1, b1, token = _split_start("rs1_start", owners, t1, lands, block_of_chip, slot_k, dh)
    dx = dx + token[0, 0]
    dx, dh, gk0 = _layer_bwd(dx, dh, ws0, ws1["mix_pre_g"], sv0, tc, ts)
    grads1["mix_pre_g"] = gk0.pop("next_pre_g")
    grad_x, gk0["mix_pre_g"] = _row_bwd("rms_first_bwd", _f_x_rms, [(x0, d, 0)], [_row(ws0["mix_pre_g"])],
                                        [(dx, d, 0), (dh, d, 0)], [(0, F32)], [0])
    t1, b1 = _split_wait("rs1_wait", owners, sems_s, sems_r, t1, b1, grad_x, block_of_chip, slot_k)
    grads0 = _unprep_grads(gk0, d)
    t0 = pair_sums(0, grads0)
    b0 = _rs_scatter("rs_scatter0", t0, owners)
    sums = [_add_quads("rs_sum0_" + n, t0[i], b0[i], 0, flags[i],
                       _add_quads("rs_sum1_" + n, t1[i], b1[i], 1, flags[i], None)) for i, n in enumerate(names)]

    out = {}
    for n, gr in zip(names, _pair_exchange("rs_join", sums, owners, False)):
        out[n] = (gr, *_adamw("adamw_" + n, w[n], gr, m[n], v[n]))
    out["w_in"] = tuple(jnp.swapaxes(a, 1, 2) for a in out["w_in"])
    rep = list(REPLICATED)
    grads = [grads0, grads1]
    g_rep = [jnp.stack([grads[l][n].reshape(w[n].shape[1:]) for l in range(depth)]) for n in rep]
    for n, *res in zip(rep, *_all_reduce_adamw(g_rep, [w[n] for n in rep], [m[n] for n in rep], [v[n] for n in rep])):
        out[n] = tuple(res)
    return (loss, grad_x[None], *[out[n][i] for i in range(4) for n in WEIGHTS])
```

```python
import functools
import math

import jax
import jax.numpy as jnp
from jax import lax
from jax.experimental import pallas as pl
from jax.experimental.pallas import tpu as pltpu

F32 = jnp.float32
_MXU_DT = jnp.bfloat16
_VMEM_LIMIT = 48 * 1024 * 1024
_LANES = 128
_MESH = pl.DeviceIdType.MESH

N_HEADS = 8
QK_NOPE = 64
QK_ROPE = 32
V_HEAD = 64
HEAD_PAD = 128
SG_GROUPS = 4
SG_CHUNK = 128
CONV_K = 31
FFN_K = 3
ROPE_THETA = 10000.0
EPS = 1e-6
ADAM_LR, ADAM_B1, ADAM_B2, ADAM_EPS, ADAM_WD, ADAM_STEP = 0.001, 0.9, 0.999, 1e-08, 0.01, 10

SHARDED_LAST = ("w_in", "conv_dw_w", "conv_out_w", "sg_out_w", "mla_w_uq", "mla_w_ukv", "mla_w_o", "ffn_w_up",
                "ffn_dw_w")
SHARDED_MID = ("w_out", "ffn_w_down")
SHARDED = SHARDED_LAST + SHARDED_MID
WEIGHTS = ("mix_pre_g", "mix_post_g", "ffn_pre_g", "ffn_post_g", "w_in", "conv_dw_w", "conv_dw_b", "conv_ln_g",
           "conv_ln_b", "conv_out_w", "sg_ln_g", "sg_ln_b", "sg_w", "sg_b", "sg_out_w", "mla_q_norm_g", "mla_w_uq",
           "mla_kv_norm_g", "mla_w_ukv", "mla_w_o", "w_out", "ffn_w_up", "ffn_dw_w", "ffn_dw_b", "ffn_w_down")
REPLICATED = tuple(n for n in WEIGHTS if n not in SHARDED)


def _cparams(sem=None):
    return pltpu.CompilerParams(dimension_semantics=sem, vmem_limit_bytes=_VMEM_LIMIT)


def _pick(n, cands):
    for c in cands:
        if n % c == 0:
            return c
    return n


def _largest_tile(dim, cap):
    for t in range(min(cap, dim) // _LANES * _LANES, 0, -_LANES):
        if dim % t == 0:
            return t
    return dim


_MM_VMEM_BUDGET = 36 * 1024 * 1024
_MM_TM_CAP, _MM_TN_CAP, _MM_TK_CAP = 1024, 1536, 3072


def _mm(name, a, b, *, ta=False, tb=False, out_dtype=F32, a_halves=False, b_halves=False, out_quarters=False):
    assert not (a_halves and ta) and not (b_halves and tb)
    if a_halves:
        m, kdim = a.shape[1], 2 * a.shape[2]
    else:
        (kdim, m) = a.shape if ta else a.shape[::-1]
    if b_halves:
        kdim2, n = b.shape[1], 2 * b.shape[2]
    else:
        (n, kdim2) = b.shape if tb else b.shape[::-1]
    assert kdim == kdim2, (a.shape, b.shape, ta, tb)
    tk = _largest_tile(kdim // 2 if a_halves else kdim, _MM_TK_CAP)
    tn = _largest_tile(n // 4 if out_quarters else (n // 2 if b_halves else n), _MM_TN_CAP)
    nk = kdim // tk
    ab, bb, ob = a.dtype.itemsize, b.dtype.itemsize, jnp.dtype(out_dtype).itemsize
    tm = _largest_tile(m, _MM_TM_CAP)
    vmem = lambda t: 2 * (t * tk * ab + tk * tn * bb + t * tn * ob) + (t * tn * 4 if nk > 1 else 0)
    while vmem(tm) > _MM_VMEM_BUDGET and tm > _LANES:
        tm = _largest_tile(m, tm - _LANES)
    dims = (((0 if ta else 1,), (1 if tb else 0,)), ((), ()))

    def dot(a_ref, b_ref):
        return lax.dot_general(a_ref[...].astype(_MXU_DT), b_ref[...].astype(_MXU_DT), dims,
                               preferred_element_type=F32)

    def body_one(a_ref, b_ref, o_ref):
        o_ref[...] = dot(a_ref, b_ref).astype(o_ref.dtype)

    def body_acc(a_ref, b_ref, o_ref, acc_ref):
        k = pl.program_id(2)

        @pl.when(k == 0)
        def _():
            acc_ref[...] = jnp.zeros_like(acc_ref)

        acc_ref[...] += dot(a_ref, b_ref)

        @pl.when(k == nk - 1)
        def _():
            o_ref[...] = acc_ref[...].astype(o_ref.dtype)

    if a_halves:
        per = nk // 2
        a_spec = pl.BlockSpec((None, tm, tk), lambda i, j, k: (k // per, i, k % per))
    elif ta:
        a_spec = pl.BlockSpec((tk, tm), lambda i, j, k: (k, i))
    else:
        a_spec = pl.BlockSpec((tm, tk), lambda i, j, k: (i, k))
    if b_halves:
        per_b = n // 2 // tn
        b_spec = pl.BlockSpec((None, tk, tn), lambda i, j, k: (j // per_b, k, j % per_b))
    elif tb:
        b_spec = pl.BlockSpec((tn, tk), lambda i, j, k: (j, k))
    else:
        b_spec = pl.BlockSpec((tk, tn), lambda i, j, k: (k, j))
    if out_quarters:
        per_o = n // 4 // tn
        o_spec = pl.BlockSpec((None, tm, tn), lambda i, j, k: (j // per_o, i, j % per_o))
        o_shape = jax.ShapeDtypeStruct((4, m, n // 4), out_dtype)
    else:
        o_spec = pl.BlockSpec((tm, tn), lambda i, j, k: (i, j))
        o_shape = jax.ShapeDtypeStruct((m, n), out_dtype)
    return pl.pallas_call(
        body_one if nk == 1 else body_acc, name=name, grid=(m // tm, n // tn, nk),
        in_specs=[a_spec, b_spec], out_specs=o_spec, out_shape=o_shape,
        scratch_shapes=[] if nk == 1 else [pltpu.VMEM((tm, tn), F32)],
        compiler_params=_cparams(("parallel", "parallel", "arbitrary")),
    )(a, b)


def _row_spec(tm, width, idx):
    return pl.BlockSpec((tm, width), lambda i: (i, idx))


def _full_spec(shape):
    zeros = (0,) * len(shape)
    return pl.BlockSpec(shape, lambda i: zeros)


def _row_fwd(name, fn, rows, params, outs, tm=256):
    s = rows[0][0].shape[0]
    nr, npar = len(rows), len(params)

    def body(*refs):
        vals = [r[...].astype(F32) for r in refs[:nr + npar]]
        res = fn(*vals)
        for o_ref, r in zip(refs[nr + npar:], res):
            o_ref[...] = r.astype(o_ref.dtype)

    return pl.pallas_call(
        body, name=name, grid=(s // tm,),
        in_specs=[_row_spec(tm, w, i) for _, w, i in rows] + [_full_spec(p.shape) for p in params],
        out_specs=[_row_spec(tm, w, 0) for w, _ in outs],
        out_shape=[jax.ShapeDtypeStruct((s, w), dt) for w, dt in outs],
        compiler_params=_cparams(("parallel",)),
    )(*[r[0] for r in rows], *params)


def _row_bwd(name, fn, rows, params, cots, row_grads, param_grads, tm=256, place=None, into=None):
    s = rows[0][0].shape[0]
    nr, npar, nc = len(rows), len(params), len(cots)
    row_grads = [((idxs,) if isinstance(idxs, int) else tuple(idxs), dt) for idxs, dt in row_grads]
    widths = [sum(rows[i][1] for i in idxs) for idxs, _ in row_grads]

    def body(*refs):
        i = pl.program_id(0)
        vals = [r[...].astype(F32) for r in refs[:nr + npar]]
        cvals = tuple(r[...].astype(F32) for r in refs[nr + npar:nr + npar + nc])
        _, vjp = jax.vjp(fn, *vals)
        grads = vjp(cvals)
        outs = refs[nr + npar + nc + (into is not None):]
        for o_ref, (idxs, _) in zip(outs, row_grads):
            pos = 0
            for idx in idxs:
                o_ref[:, pos:pos + rows[idx][1]] = grads[idx].astype(o_ref.dtype)
                pos += rows[idx][1]
        for o_ref, idx in zip(outs[len(row_grads):], param_grads):
            @pl.when(i == 0)
            def _(o_ref=o_ref):
                o_ref[...] = jnp.zeros_like(o_ref)

            o_ref[...] += grads[nr + idx]

    out_specs = [_row_spec(tm, w, 0) for w in widths] + [_full_spec(params[idx].shape) for idx in param_grads]
    out_shape = ([jax.ShapeDtypeStruct((s, w), dt) for w, (_, dt) in zip(widths, row_grads)]
                 + [jax.ShapeDtypeStruct(params[idx].shape, F32) for idx in param_grads])
    extra, aliases = [], {}
    if place is not None:
        out_specs[0] = _row_spec(tm, widths[0], place[1])
        out_shape[0] = jax.ShapeDtypeStruct((s, place[0]), row_grads[0][1])
    if into is not None:
        extra, aliases = [into], {nr + npar + nc: 0}
    return pl.pallas_call(
        body, name=name, grid=(s // tm,),
        in_specs=([_row_spec(tm, w, i) for _, w, i in rows] + [_full_spec(p.shape) for p in params]
                  + [_row_spec(tm, w, i) for _, w, i in cots] + [_ANY] * len(extra)),
        out_specs=out_specs, out_shape=out_shape, input_output_aliases=aliases,
        compiler_params=_cparams(("arbitrary",)),
    )(*[r[0] for r in rows], *params, *[c[0] for c in cots], *extra)


def _rms(x, g):
    return x * lax.rsqrt(jnp.mean(x * x, axis=-1, keepdims=True) + EPS) * g


def _ln(x, g, b):
    mu = jnp.mean(x, axis=-1, keepdims=True)
    xc = x - mu
    var = jnp.mean(xc * xc, axis=-1, keepdims=True)
    return xc * lax.rsqrt(var + EPS) * g + b


def _sigmoid(x):
    return 1.0 / (1.0 + jnp.exp(-x))


def _gelu(x):
    return x * (0.5 * (1.0 + jnp.tanh(math.sqrt(2.0 / math.pi) * (x + 0.044715 * (x * x * x)))))


def _f_rms(x, g):
    return (_rms(x, g),)


def _f_x_rms(x, g):
    return (x, _rms(x, g))


def _f_ln_silu(z, g, b):
    y = _ln(z, g, b)
    return (y * _sigmoid(y),)


def _f_sg_pre(bu, bv, g, b):
    return (_gelu(bu), _ln(_gelu(bv), g, b))


def _f_merge(g0, g1, g2, ya, yb, yc):
    return (_sigmoid(g0) * ya + _sigmoid(g1) * yb + _sigmoid(g2) * yc,)


def _f_resid_rms(x, t, g_post):
    return (x + _rms(t, g_post),)


def _f_resid_rms_rms(x, t, g_post, g_next):
    x1 = x + _rms(t, g_post)
    return (x1, _rms(x1, g_next))


def _f_geglu(zg, zv):
    return _gelu(zg) * zv


_CONV_TILE_ELEMS = 16 * 1024


def _conv_tr(c):
    return _CONV_TILE_ELEMS // c


def _conv_tile(zp_ref, w_ref, bias, k_taps, off, r0):
    c = zp_ref.shape[1]
    tr = _conv_tr(c)
    acc = jnp.broadcast_to(bias, (tr, c))
    for k in range(k_taps):
        acc = acc + w_ref[k:k + 1, :] * zp_ref[r0 + off + k:r0 + off + k + tr, :]
    return acc


def _conv_bwd_input_tile(dzp_ref, w_ref, k_taps, r0):
    c = dzp_ref.shape[1]
    tr = _conv_tr(c)
    acc = jnp.zeros((tr, c), F32)
    for k in range(k_taps):
        s0 = r0 + (k_taps - 1) - k
        acc = acc + w_ref[k:k + 1, :] * dzp_ref[s0:s0 + tr, :]
    return acc


def _conv_bwd_weight(dzp_ref, zp_ref, dw_ref, db_ref, k_taps, off, s):
    c = zp_ref.shape[1]
    tr = _conv_tr(c)
    fold = lambda v: jnp.sum(v.reshape(tr // 8, 8, c), axis=0)
    for k in range(k_taps):
        acc = jnp.zeros((8, c), F32)
        for r in range(s // tr):
            r0 = r * tr
            acc = acc + fold(dzp_ref[r0:r0 + tr, :] * zp_ref[r0 + off + k:r0 + off + k + tr, :])
        dw_ref[k:k + 1, :] = jnp.sum(acc, axis=0, keepdims=True)
    acc = jnp.zeros((8, c), F32)
    for r in range(s // tr):
        acc = acc + fold(dzp_ref[r * tr:(r + 1) * tr, :])
    db_ref[...] = jnp.sum(acc, axis=0, keepdims=True)


def _glu_conv_fwd(p, blk0, w, b):
    s = p.shape[0]
    k_taps, c = w.shape
    cb, pad = _LANES, 32
    off = pad - (k_taps - 1)

    def body(a_ref, w_ref, b_ref, o_ref, zp_ref):
        zp_ref[0:pad, :] = jnp.zeros((pad, cb), F32)
        zp_ref[pad:pad + s, :] = a_ref[:, 0:cb] * _sigmoid(a_ref[:, cb:2 * cb])
        tr = _conv_tr(cb)
        for r in range(s // tr):
            o_ref[r * tr:(r + 1) * tr, :] = _conv_tile(zp_ref, w_ref, b_ref[...], k_taps, off, r * tr)

    return pl.pallas_call(
        body, name="glu_conv_fwd", grid=(c // cb,),
        in_specs=[pl.BlockSpec((s, 2 * cb), lambda j: (0, blk0 + j)),
                  pl.BlockSpec((k_taps, cb), lambda j: (0, j)), pl.BlockSpec((1, cb), lambda j: (0, j))],
        out_specs=pl.BlockSpec((s, cb), lambda j: (0, j)),
        out_shape=jax.ShapeDtypeStruct((s, c), F32),
        scratch_shapes=[pltpu.VMEM((s + pad, cb), F32)],
        compiler_params=_cparams(("parallel",)),
    )(p, w, b)


def _glu_conv_bwd(p, blk0, w, dz, dp):
    s = p.shape[0]
    k_taps, c = w.shape
    cb, pad = _LANES, 32
    off = pad - (k_taps - 1)

    def body(a_ref, w_ref, dz_ref, dp_in, da_ref, dw_ref, db_ref, zp_ref, dzp_ref):
        zp_ref[0:pad, :] = jnp.zeros((pad, cb), F32)
        zp_ref[pad:pad + s, :] = a_ref[:, 0:cb] * _sigmoid(a_ref[:, cb:2 * cb])
        dzp_ref[0:s, :] = dz_ref[...]
        dzp_ref[s:s + pad, :] = jnp.zeros((pad, cb), F32)
        tr = _conv_tr(cb)
        for r in range(s // tr):
            rows = slice(r * tr, (r + 1) * tr)
            dz0 = _conv_bwd_input_tile(dzp_ref, w_ref, k_taps, r * tr)
            sg = _sigmoid(a_ref[rows, cb:2 * cb])
            da_ref[rows, 0:cb] = (dz0 * sg).astype(da_ref.dtype)
            da_ref[rows, cb:2 * cb] = (dz0 * a_ref[rows, 0:cb] * sg * (1.0 - sg)).astype(da_ref.dtype)
        _conv_bwd_weight(dzp_ref, zp_ref, dw_ref, db_ref, k_taps, off, s)

    return pl.pallas_call(
        body, name="glu_conv_bwd", grid=(c // cb,),
        in_specs=[pl.BlockSpec((s, 2 * cb), lambda j: (0, blk0 + j)),
                  pl.BlockSpec((k_taps, cb), lambda j: (0, j)), pl.BlockSpec((s, cb), lambda j: (0, j)), _ANY],
        out_specs=[pl.BlockSpec((s, 2 * cb), lambda j: (0, blk0 + j)),
                   pl.BlockSpec((k_taps, cb), lambda j: (0, j)), pl.BlockSpec((1, cb), lambda j: (0, j))],
        out_shape=[jax.ShapeDtypeStruct(dp.shape, dp.dtype),
                   jax.ShapeDtypeStruct((k_taps, c), F32), jax.ShapeDtypeStruct((1, c), F32)],
        scratch_shapes=[pltpu.VMEM((s + pad, cb), F32), pltpu.VMEM((s + pad, cb), F32)],
        input_output_aliases={3: 0},
        compiler_params=_cparams(("parallel",)),
    )(p, w, dz, dp)


def _conv_geglu_fwd(up, w, b):
    s, f2 = up.shape
    f = f2 // 2
    k_taps = w.shape[0]
    cb, pad = _FFN_CB, 8
    off = pad - (k_taps - 1)
    nb = f // cb

    def body(ug_ref, uv_ref, wg_ref, wv_ref, bg_ref, bv_ref, o_ref, z_ref, w_ref, b_ref):
        _pair(w_ref, wg_ref[...], wv_ref[...], cb)
        _pair(b_ref, bg_ref[...], bv_ref[...], cb)
        z_ref[0:pad, :] = jnp.zeros((pad, 2 * cb), F32)
        z_ref[pad:pad + s, 0:cb] = ug_ref[...]
        z_ref[pad:pad + s, cb:2 * cb] = uv_ref[...]
        tr = _conv_tr(2 * cb)
        for r in range(s // tr):
            z = _conv_tile(z_ref, w_ref, b_ref[...], k_taps, off, r * tr)
            o_ref[r * tr:(r + 1) * tr, :] = _f_geglu(z[:, 0:cb], z[:, cb:2 * cb]).astype(o_ref.dtype)

    two = lambda rows_: [pl.BlockSpec((rows_, cb), lambda j: (0, j)), pl.BlockSpec((rows_, cb), lambda j: (0, nb + j))]
    return pl.pallas_call(
        body, name="conv_geglu_fwd", grid=(nb,),
        in_specs=two(s) + two(k_taps) + two(1),
        out_specs=pl.BlockSpec((s, cb), lambda j: (0, j)),
        out_shape=jax.ShapeDtypeStruct((s, f), _MXU_DT),
        scratch_shapes=[pltpu.VMEM((s + pad, 2 * cb), F32), pltpu.VMEM((k_taps, 2 * cb), F32),
                        pltpu.VMEM((1, 2 * cb), F32)],
        compiler_params=_cparams(("parallel",)),
    )(up, up, w, w, b, b)


def _pair(dst_ref, first, second, cb):
    dst_ref[:, 0:cb] = first
    dst_ref[:, cb:2 * cb] = second


def _conv_geglu_bwd(up, w, b, dact):
    s, f2 = up.shape
    f = f2 // 2
    k_taps = w.shape[0]
    cb, pad = _FFN_CB, 8
    off = pad - (k_taps - 1)
    nb = f // cb

    def body(ug_ref, uv_ref, wg_ref, wv_ref, bg_ref, bv_ref, da_ref, du_ref, dw_ref, db_ref, z_ref, dz_ref, w_ref,
             b_ref, dw_sc, db_sc):
        _pair(w_ref, wg_ref[...], wv_ref[...], cb)
        _pair(b_ref, bg_ref[...], bv_ref[...], cb)
        z_ref[0:pad, :] = jnp.zeros((pad, 2 * cb), F32)
        z_ref[pad:pad + s, 0:cb] = ug_ref[...]
        z_ref[pad:pad + s, cb:2 * cb] = uv_ref[...]
        dz_ref[s:s + pad, :] = jnp.zeros((pad, 2 * cb), F32)
        tr = _conv_tr(2 * cb)
        for r in range(s // tr):
            rows = slice(r * tr, (r + 1) * tr)
            z = _conv_tile(z_ref, w_ref, b_ref[...], k_taps, off, r * tr)
            _, vjp = jax.vjp(_f_geglu, z[:, 0:cb], z[:, cb:2 * cb])
            dzg, dzv = vjp(da_ref[rows, :].astype(F32))
            dz_ref[rows, 0:cb] = dzg
            dz_ref[rows, cb:2 * cb] = dzv
        for r in range(s // tr):
            rows = slice(r * tr, (r + 1) * tr)
            du = _conv_bwd_input_tile(dz_ref, w_ref, k_taps, r * tr).astype(du_ref.dtype)
            du_ref[0, rows, :] = du[:, 0:cb]
            du_ref[1, rows, :] = du[:, cb:2 * cb]
        _conv_bwd_weight(dz_ref, z_ref, dw_sc, db_sc, k_taps, off, s)
        for half in range(2):
            dw_ref[half] = dw_sc[:, half * cb:(half + 1) * cb]
            db_ref[half] = db_sc[:, half * cb:(half + 1) * cb]

    two = lambda rows_: [pl.BlockSpec((rows_, cb), lambda j: (0, j)), pl.BlockSpec((rows_, cb), lambda j: (0, nb + j))]
    both = lambda rows_: pl.BlockSpec((2, rows_, cb), lambda j: (0, 0, j))
    return pl.pallas_call(
        body, name="conv_geglu_bwd", grid=(nb,),
        in_specs=two(s) + two(k_taps) + two(1) + [pl.BlockSpec((s, cb), lambda j: (0, j))],
        out_specs=[both(s), both(k_taps), both(1)],
        out_shape=[jax.ShapeDtypeStruct((2, s, f), _MXU_DT), jax.ShapeDtypeStruct((2, k_taps, f), F32),
                   jax.ShapeDtypeStruct((2, 1, f), F32)],
        scratch_shapes=[pltpu.VMEM((s + pad, 2 * cb), F32), pltpu.VMEM((s + pad, 2 * cb), F32),
                        pltpu.VMEM((k_taps, 2 * cb), F32), pltpu.VMEM((1, 2 * cb), F32),
                        pltpu.VMEM((k_taps, 2 * cb), F32), pltpu.VMEM((1, 2 * cb), F32)],
        compiler_params=_cparams(("parallel",)),
    )(up, up, w, w, b, b, dact)


def _tril_mask():
    t = lax.broadcasted_iota(jnp.int32, (SG_CHUNK, SG_CHUNK), 0)
    s = lax.broadcasted_iota(jnp.int32, (SG_CHUNK, SG_CHUNK), 1)
    return t >= s


def _sg_mix_fwd(u, vn, w, bcol):
    s, c = u.shape
    gw = c // SG_GROUPS

    def body(u_ref, v_ref, w_ref, b_ref, o_ref):
        wm = jnp.where(_tril_mask(), w_ref[0], 0.0).astype(_MXU_DT)
        for n in range(s // SG_CHUNK):
            rows = slice(n * SG_CHUNK, (n + 1) * SG_CHUNK)
            mixed = jnp.dot(wm, v_ref[rows, :], preferred_element_type=F32) + b_ref[0]
            o_ref[rows, :] = (u_ref[rows, :] * mixed).astype(o_ref.dtype)

    return pl.pallas_call(
        body, name="sg_mix_fwd", grid=(SG_GROUPS,),
        in_specs=[pl.BlockSpec((s, gw), lambda g: (0, g)), pl.BlockSpec((s, gw), lambda g: (0, g)),
                  pl.BlockSpec((1, SG_CHUNK, SG_CHUNK), lambda g: (g, 0, 0)),
                  pl.BlockSpec((1, SG_CHUNK, 1), lambda g: (g, 0, 0))],
        out_specs=pl.BlockSpec((s, gw), lambda g: (0, g)),
        out_shape=jax.ShapeDtypeStruct((s, c), _MXU_DT),
        compiler_params=_cparams(("parallel",)),
    )(u, vn, w, bcol)


def _sg_mix_bwd(u, vn, w, bcol, dub):
    s, c = u.shape
    gw = c // SG_GROUPS

    def body(u_ref, v_ref, w_ref, b_ref, d_ref, du_ref, dv_ref, dw_ref, db_ref):
        mask = _tril_mask()
        wm = jnp.where(mask, w_ref[0], 0.0).astype(_MXU_DT)
        dw = jnp.zeros((SG_CHUNK, SG_CHUNK), F32)
        db = jnp.zeros((SG_CHUNK, 1), F32)
        for n in range(s // SG_CHUNK):
            rows = slice(n * SG_CHUNK, (n + 1) * SG_CHUNK)
            v = v_ref[rows, :]
            d = d_ref[rows, :].astype(F32)
            mixed = jnp.dot(wm, v, preferred_element_type=F32) + b_ref[0]
            du_ref[rows, :] = d * mixed
            dmix = d * u_ref[rows, :]
            dmix_lo = dmix.astype(_MXU_DT)
            dv_ref[rows, :] = lax.dot_general(wm, dmix_lo, (((0,), (0,)), ((), ())), preferred_element_type=F32)
            dw = dw + lax.dot_general(dmix_lo, v, (((1,), (1,)), ((), ())), preferred_element_type=F32)
            db = db + jnp.sum(dmix, axis=1, keepdims=True)
        dw_ref[0] = jnp.where(mask, dw, 0.0)
        db_ref[0] = db

    return pl.pallas_call(
        body, name="sg_mix_bwd", grid=(SG_GROUPS,),
        in_specs=[pl.BlockSpec((s, gw), lambda g: (0, g)), pl.BlockSpec((s, gw), lambda g: (0, g)),
                  pl.BlockSpec((1, SG_CHUNK, SG_CHUNK), lambda g: (g, 0, 0)),
                  pl.BlockSpec((1, SG_CHUNK, 1), lambda g: (g, 0, 0)), pl.BlockSpec((s, gw), lambda g: (0, g))],
        out_specs=[pl.BlockSpec((s, gw), lambda g: (0, g)), pl.BlockSpec((s, gw), lambda g: (0, g)),
                   pl.BlockSpec((1, SG_CHUNK, SG_CHUNK), lambda g: (g, 0, 0)),
                   pl.BlockSpec((1, SG_CHUNK, 1), lambda g: (g, 0, 0))],
        out_shape=[jax.ShapeDtypeStruct((s, c), F32), jax.ShapeDtypeStruct((s, c), F32),
                   jax.ShapeDtypeStruct((SG_GROUPS, SG_CHUNK, SG_CHUNK), F32),
                   jax.ShapeDtypeStruct((SG_GROUPS, SG_CHUNK, 1), F32)],
        compiler_params=_cparams(("parallel",)),
    )(u, vn, w, bcol, dub)


def _rope_fwd(q2, kv2, p, krm_idx, krs_idx, tc, ts):
    s = q2.shape[0]
    hw = N_HEADS * HEAD_PAD
    tm = 256

    def body(qm_ref, qs_ref, kn_ref, v_ref, krm_ref, krs_ref, tc_ref, ts_ref, q_ref, k_ref, vo_ref):
        tcv, tsv = tc_ref[...], ts_ref[...]
        kpe = krm_ref[...] * tcv + krs_ref[...] * tsv
        for h in range(N_HEADS):
            cols = slice(h * HEAD_PAD, (h + 1) * HEAD_PAD)
            q_ref[:, cols] = (qm_ref[:, cols] * tcv + qs_ref[:, cols] * tsv).astype(q_ref.dtype)
            k_ref[:, cols] = (kn_ref[:, cols] + kpe).astype(k_ref.dtype)
        vo_ref[...] = v_ref[...].astype(vo_ref.dtype)

    return pl.pallas_call(
        body, name="rope_fwd", grid=(s // tm,),
        in_specs=[_row_spec(tm, hw, 0), _row_spec(tm, hw, 1), _row_spec(tm, hw, 0), _row_spec(tm, hw, 1),
                  _row_spec(tm, HEAD_PAD, krm_idx), _row_spec(tm, HEAD_PAD, krs_idx),
                  _row_spec(tm, HEAD_PAD, 0), _row_spec(tm, HEAD_PAD, 0)],
        out_specs=[_row_spec(tm, hw, 0)] * 3,
        out_shape=[jax.ShapeDtypeStruct((s, hw), _MXU_DT)] * 3,
        compiler_params=_cparams(("parallel",)),
    )(q2, q2, kv2, kv2, p, p, tc, ts)


def _rope_bwd(dq, dk, dv, tc, ts, dp, kr_blk):
    s = dq.shape[0]
    hw = N_HEADS * HEAD_PAD
    tm = 256

    def body(dq_ref, dk_ref, dv_ref, tc_ref, ts_ref, dp_in, dq2_ref, dkv2_ref, dkr_ref):
        tcv, tsv = tc_ref[...], ts_ref[...]
        dkpe = jnp.zeros((tm, HEAD_PAD), F32)
        for h in range(N_HEADS):
            cols = slice(h * HEAD_PAD, (h + 1) * HEAD_PAD)
            dqh = dq_ref[:, cols]
            dq2_ref[:, cols] = (dqh * tcv).astype(dq2_ref.dtype)
            dq2_ref[:, hw + h * HEAD_PAD:hw + (h + 1) * HEAD_PAD] = (dqh * tsv).astype(dq2_ref.dtype)
            dkpe = dkpe + dk_ref[:, cols]
        dkv2_ref[:, 0:hw] = dk_ref[...].astype(dkv2_ref.dtype)
        dkv2_ref[:, hw:2 * hw] = dv_ref[...].astype(dkv2_ref.dtype)
        dkr_ref[:, 0:HEAD_PAD] = (dkpe * tcv).astype(dkr_ref.dtype)
        dkr_ref[:, HEAD_PAD:2 * HEAD_PAD] = (dkpe * tsv).astype(dkr_ref.dtype)
        dkr_ref[:, 2 * HEAD_PAD:3 * HEAD_PAD] = jnp.zeros((tm, HEAD_PAD), dkr_ref.dtype)

    return pl.pallas_call(
        body, name="rope_bwd", grid=(s // tm,),
        in_specs=[_row_spec(tm, hw, 0)] * 3 + [_row_spec(tm, HEAD_PAD, 0)] * 2 + [_ANY],
        out_specs=[_row_spec(tm, 2 * hw, 0), _row_spec(tm, 2 * hw, 0), _row_spec(tm, 3 * HEAD_PAD, kr_blk)],
        out_shape=[jax.ShapeDtypeStruct((s, 2 * hw), _MXU_DT), jax.ShapeDtypeStruct((s, 2 * hw), _MXU_DT),
                   jax.ShapeDtypeStruct(dp.shape, dp.dtype)],
        input_output_aliases={5: 2},
        compiler_params=_cparams(("parallel",)),
    )(dq, dk, dv, tc, ts, dp)


_ATTN_TQ = 256
_ATTN_SCALE = (QK_NOPE + QK_ROPE) ** -0.5


def _attn_probs(q, k, i):
    s = k.shape[0]
    sc = lax.dot_general(q, k, (((1,), (1,)), ((), ())), preferred_element_type=F32) * _ATTN_SCALE
    row = i * _ATTN_TQ + lax.broadcasted_iota(jnp.int32, (_ATTN_TQ, s), 0)
    col = lax.broadcasted_iota(jnp.int32, (_ATTN_TQ, s), 1)
    sc = jnp.where(row >= col, sc, jnp.finfo(F32).min)
    e = jnp.exp(sc - jnp.max(sc, axis=1, keepdims=True))
    return e * (1.0 / jnp.sum(e, axis=1, keepdims=True))


def _per_query_block(s, fn):
    i = pl.program_id(1)
    for n in range(s // _ATTN_TQ):
        @pl.when(i == n)
        def _(n=n):
            fn(n, (n + 1) * _ATTN_TQ)


def _attn_fwd(q, k, v):
    s = q.shape[0]

    def body(q_ref, k_ref, v_ref, o_ref):
        def block(n, kl):
            p = _attn_probs(q_ref[...], k_ref[0:kl, :], n)
            o_ref[...] = jnp.dot(p.astype(_MXU_DT), v_ref[0:kl, :], preferred_element_type=F32).astype(o_ref.dtype)

        _per_query_block(s, block)

    qspec = pl.BlockSpec((_ATTN_TQ, HEAD_PAD), lambda h, i: (i, h))
    kspec = pl.BlockSpec((s, HEAD_PAD), lambda h, i: (0, h))
    return pl.pallas_call(
        body, name="attn_fwd", grid=(N_HEADS, s // _ATTN_TQ),
        in_specs=[qspec, kspec, kspec], out_specs=qspec,
        out_shape=jax.ShapeDtypeStruct(q.shape, _MXU_DT),
        compiler_params=_cparams(("parallel", "parallel")),
    )(q, k, v)


def _attn_bwd(q, k, v, do):
    s = q.shape[0]

    def body(q_ref, k_ref, v_ref, do_ref, dq_ref, dk_ref, dv_ref):
        i = pl.program_id(1)

        @pl.when(i == 0)
        def _():
            dk_ref[...] = jnp.zeros_like(dk_ref)
            dv_ref[...] = jnp.zeros_like(dv_ref)

        def block(n, kl):
            qv, kv, dov = q_ref[...], k_ref[0:kl, :], do_ref[...]
            p = _attn_probs(qv, kv, n)
            dp = lax.dot_general(dov, v_ref[0:kl, :], (((1,), (1,)), ((), ())), preferred_element_type=F32)
            delta = jnp.sum(p * dp, axis=1, keepdims=True)
            ds = (p * (dp - delta) * _ATTN_SCALE).astype(_MXU_DT)
            dq_ref[...] = jnp.dot(ds, kv, preferred_element_type=F32)
            dk_ref[0:kl, :] += lax.dot_general(ds, qv, (((0,), (0,)), ((), ())), preferred_element_type=F32)
            dv_ref[0:kl, :] += lax.dot_general(p.astype(_MXU_DT), dov, (((0,), (0,)), ((), ())),
                                               preferred_element_type=F32)

        _per_query_block(s, block)

    qspec = pl.BlockSpec((_ATTN_TQ, HEAD_PAD), lambda h, i: (i, h))
    kspec = pl.BlockSpec((s, HEAD_PAD), lambda h, i: (0, h))
    return pl.pallas_call(
        body, name="attn_bwd", grid=(N_HEADS, s // _ATTN_TQ),
        in_specs=[qspec, kspec, kspec, qspec], out_specs=[qspec, kspec, kspec],
        out_shape=[jax.ShapeDtypeStruct(q.shape, F32)] * 3,
        compiler_params=_cparams(("parallel", "arbitrary")),
    )(q, k, v, do)


def _loss_head(y, target):
    s, d = y.shape
    tm = 256

    def body(y_ref, t_ref, loss_ref, dy_ref):
        @pl.when(pl.program_id(0) == 0)
        def _():
            loss_ref[...] = jnp.zeros_like(loss_ref)

        err = y_ref[...] - t_ref[...]
        loss_ref[...] += 0.5 * jnp.sum(jnp.mean(err * err, axis=-1, keepdims=True), axis=0, keepdims=True)
        dy_ref[...] = err * (1.0 / d)

    return pl.pallas_call(
        body, name="loss_head", grid=(s // tm,),
        in_specs=[_row_spec(tm, d, 0), _row_spec(tm, d, 0)],
        out_specs=[_full_spec((1, 1)), _row_spec(tm, d, 0)],
        out_shape=[jax.ShapeDtypeStruct((1, 1), F32), jax.ShapeDtypeStruct((s, d), F32)],
        compiler_params=_cparams(("arbitrary",)),
    )(y, target)


def _adamw_math(w, g, m, v):
    mn = ADAM_B1 * m + (1.0 - ADAM_B1) * g
    vn = ADAM_B2 * v + (1.0 - ADAM_B2) * (g * g)
    m_hat = mn / (1.0 - ADAM_B1 ** ADAM_STEP)
    v_hat = vn / (1.0 - ADAM_B2 ** ADAM_STEP)
    return -ADAM_LR * (m_hat / (jnp.sqrt(v_hat) + ADAM_EPS) + ADAM_WD * w), mn, vn


def _adamw(name, w, g, m, v, layer, into):
    _, k, n = w.shape
    tk, tn = _slab_block(k, n)

    def body(w_ref, g_ref, m_ref, v_ref, *rest):
        d_ref, mo_ref, vo_ref = rest[-3:]
        d_ref[...], mo_ref[...], vo_ref[...] = _adamw_math(w_ref[...], g_ref[...], m_ref[...], v_ref[...])

    spec = pl.BlockSpec((1, tk, tn), lambda j, jn: (layer, j, jn))
    extra = [] if into is None else list(into)
    return pl.pallas_call(
        body, name=name, grid=(k // tk, n // tn), in_specs=[spec] * 4 + [_ANY] * len(extra), out_specs=[spec] * 3,
        out_shape=[jax.ShapeDtypeStruct(w.shape, F32)] * 3,
        input_output_aliases={4 + i: i for i in range(len(extra))},
        compiler_params=_cparams(("parallel", "parallel")),
    )(w, g, m, v, *extra)


_ANY = pl.BlockSpec(memory_space=pl.ANY)


def _mesh_pos():
    return lax.axis_index("x"), lax.axis_index("y"), lax.axis_index("c")


def _other_chips(x, y):
    return [(1 - x, y), (x, 1 - y), (1 - x, 1 - y)]


def _remote(src, dst, send_sem, recv_sem, to):
    return pltpu.make_async_remote_copy(src_ref=src, dst_ref=dst, send_sem=send_sem, recv_sem=recv_sem,
                                        device_id=to, device_id_type=_MESH)


_HBM = pl.BlockSpec(memory_space=pltpu.HBM)
_SEM = pl.BlockSpec(memory_space=pltpu.SEMAPHORE)
_EFFECT = pltpu.SideEffectType.DATAFLOW_SIDE_EFFECTING


def _in_hbm(a):
    return pltpu.with_memory_space_constraint(a, pltpu.HBM)


def _chip_index(chip):
    return 2 * chip[0] + chip[1]


def _ag_forward(shards, lands, layer, have_remote):
    n = len(shards)

    def body(*refs):
        ins = refs[:n]
        outs = refs[2 * n:3 * n] if lands is not None else refs[n:2 * n]
        send_sems, recv_sems = refs[-2:]
        x, y, c = _mesh_pos()
        sibling = (x, y, 1 - c)
        chips = _other_chips(x, y)

        def copy(a, k, src, dst, to):
            return _remote(src, dst, send_sems.at[a, k], recv_sems.at[a, k], to)

        own = [copy(a, 6, ins[a], outs[a].at[2 * x + y], sibling) for a in range(n)]
        for cp in own:
            cp.start()

        @pl.when(c == layer)
        def _():
            started = []
            if not have_remote:
                for k, chip in enumerate(chips):
                    for a in range(n):
                        cp = copy(a, k, ins[a], outs[a].at[2 * x + y], (*chip, c))
                        cp.start()
                        started.append(cp)
            for k, chip in enumerate(chips):
                for a in range(n):
                    landed = outs[a].at[_chip_index(chip)]
                    if not have_remote:
                        copy(a, k, ins[a], landed, (*chip, c)).wait_recv()
                    cp = copy(a, 3 + k, landed, landed, sibling)
                    cp.start()
                    started.append(cp)
            for cp in started:
                cp.wait_send()

        @pl.when(c != layer)
        def _():
            for k, chip in enumerate(chips):
                for a in range(n):
                    copy(a, 3 + k, ins[a], outs[a].at[_chip_index(chip)], sibling).wait_recv()

        for cp in own:
            cp.wait()

    out_shape = [jax.ShapeDtypeStruct((4,) + a.shape, a.dtype) for a in shards]
    extra = [] if lands is None else list(lands)
    return pl.pallas_call(
        body, name="ag_layer%d" % layer, in_specs=[_ANY] * (n + len(extra)), out_specs=[_ANY] * n,
        out_shape=out_shape, input_output_aliases={n + a: a for a in range(len(extra))},
        scratch_shapes=[pltpu.SemaphoreType.DMA((n, 7)), pltpu.SemaphoreType.DMA((n, 7))],
    )(*shards, *extra)


def _owner_sends(owners, srcs, dsts, send_sems, recv_sems, do):
    x, y, c = _mesh_pos()
    for core in (0, 1):
        mine = [a for a in range(len(srcs)) if owners[a] == core]
        if mine:
            @pl.when(c == core)
            def _(mine=mine):
                for k, chip in enumerate(_other_chips(x, y)):
                    for a in mine:
                        do(_remote(srcs[a](chip, k), dsts[a](chip, k), send_sems.at[3 * a + k],
                                   recv_sems.at[3 * a + k], (*chip, c)))


def _split_start(name, owners, sources, land_shapes, src_of, dst_of, after):
    n = len(sources)

    def body(*refs):
        srcs, lands = refs[:n], refs[n:2 * n]
        send_sems, recv_sems = refs[2 * n + 1], refs[2 * n + 2]
        token = refs[-1]
        _owner_sends(owners, [functools.partial(src_of, srcs[a]) for a in range(n)],
                     [functools.partial(dst_of, lands[a]) for a in range(n)], send_sems, recv_sems,
                     lambda cp: cp.start())
        token[...] = jnp.zeros_like(token)

    lands = [_in_hbm(lax.empty(s.shape, s.dtype)) for s in land_shapes]
    outs = pl.pallas_call(
        body, name=name,
        out_shape=([pltpu.SemaphoreType.DMA((3 * n,)), pltpu.SemaphoreType.DMA((3 * n,))]
                   + [pltpu.HBM(a.shape, a.dtype) for a in sources] + [pltpu.HBM(s.shape, s.dtype) for s in land_shapes]
                   + [jax.ShapeDtypeStruct((8, _LANES), F32)]),
        in_specs=[_HBM] * (2 * n) + [_ANY],
        out_specs=[_SEM, _SEM] + [_HBM] * (2 * n) + [pl.BlockSpec(memory_space=pltpu.VMEM)],
        input_output_aliases={i: 2 + i for i in range(2 * n)},
        compiler_params=pltpu.CompilerParams(has_side_effects=_EFFECT),
    )(*[_in_hbm(a) for a in sources], *lands, after)
    return outs[0], outs[1], outs[2:2 + n], outs[2 + n:2 + 2 * n], outs[-1]


def _split_wait(name, owners, send_sems, recv_sems, sources, lands, after, src_of, dst_of):
    n = len(sources)

    def body(*refs):
        srcs, lnds = refs[:n], refs[n:2 * n]
        s_sems, r_sems = refs[2 * n], refs[2 * n + 1]

        def wait(cp):
            cp.wait_send()
            cp.wait_recv()

        _owner_sends(owners, [functools.partial(src_of, srcs[a]) for a in range(n)],
                     [functools.partial(dst_of, lnds[a]) for a in range(n)], s_sems, r_sems, wait)

    outs = pl.pallas_call(
        body, name=name,
        out_shape=[pltpu.HBM(a.shape, a.dtype) for a in sources] + [pltpu.HBM(a.shape, a.dtype) for a in lands],
        in_specs=[_HBM] * (2 * n) + [_SEM, _SEM, _ANY], out_specs=[_HBM] * (2 * n),
        input_output_aliases={i: i for i in range(2 * n)},
        compiler_params=pltpu.CompilerParams(has_side_effects=_EFFECT),
    )(*sources, *lands, send_sems, recv_sems, after)
    return outs[:n], outs[n:]


def _pair_exchange(name, arrays, owners, to_owner, layer=None):
    n = len(arrays)

    def body(*refs):
        ins, outs, (send_sems, recv_sems) = refs[:n], refs[n:2 * n], refs[2 * n:]
        x, y, c = _mesh_pos()
        part = (lambda r: r) if layer is None else (lambda r: r.at[layer])
        copies = [_remote(part(ins[a]), part(outs[a]), send_sems.at[a], recv_sems.at[a], (x, y, 1 - c))
                  for a in range(n)]
        for core in (0, 1):
            sends = [copies[a] for a in range(n) if (owners[a] != core) == to_owner]
            recvs = [copies[a] for a in range(n) if (owners[a] == core) == to_owner]

            @pl.when(c == core)
            def _(sends=sends, recvs=recvs):
                for cp in sends:
                    cp.start()
                for cp in recvs:
                    cp.wait_recv()
                for cp in sends:
                    cp.wait_send()

    return pl.pallas_call(
        body, name=name, in_specs=[_ANY] * n, out_specs=[_ANY] * n,
        out_shape=[jax.ShapeDtypeStruct(g.shape, g.dtype) for g in arrays],
        input_output_aliases={} if to_owner else {a: a for a in range(n)},
        scratch_shapes=[pltpu.SemaphoreType.DMA((n,)), pltpu.SemaphoreType.DMA((n,))],
    )(*arrays)


def _rs_scatter(name, ts, owners):
    n = len(ts)

    def body(*refs):
        ins, outs, (send_sems, recv_sems) = refs[:n], refs[n:2 * n], refs[2 * n:]
        srcs = [lambda chip, k, r=ins[a]: r.at[_chip_index(chip)] for a in range(n)]
        dsts = [lambda chip, k, r=outs[a]: r.at[k] for a in range(n)]
        _owner_sends(owners, srcs, dsts, send_sems, recv_sems, lambda cp: cp.start())
        _owner_sends(owners, srcs, dsts, send_sems, recv_sems, lambda cp: cp.wait())

    return pl.pallas_call(
        body, name=name, in_specs=[_ANY] * n, out_specs=[_ANY] * n,
        out_shape=[jax.ShapeDtypeStruct((3,) + t.shape[1:], t.dtype) for t in ts],
        scratch_shapes=[pltpu.SemaphoreType.DMA((3 * n,)), pltpu.SemaphoreType.DMA((3 * n,))],
    )(*ts)


def _add_pair(name, g, a, flags):
    _, k, n = g.shape
    tk, tn = _slab_block(k, n)

    def body(flags_ref, g_ref, a_ref, o_ref):
        o_ref[...] = (g_ref[...] + a_ref[...]).astype(o_ref.dtype)

    spec = pl.BlockSpec((1, tk, tn), lambda j, i, jn, fl: (j * fl[1], i * fl[1], jn * fl[1]))
    return pl.pallas_call(
        body, name=name,
        grid_spec=pltpu.PrefetchScalarGridSpec(num_scalar_prefetch=1, grid=(4, k // tk, n // tn),
                                               in_specs=[spec, spec], out_specs=spec),
        out_shape=jax.ShapeDtypeStruct(g.shape, _RS_DT),
        compiler_params=_cparams(("arbitrary", "arbitrary", "arbitrary")),
    )(flags, g, a)


def _add_quads(name, t, b, layer, flags, into, after=None):
    _, k, n = t.shape
    tk, tn = _slab_block(k, n)

    def body(flags_ref, t_ref, b_ref, *rest):
        o_ref = rest[-1]
        f = lambda v: v.astype(F32)
        o_ref[0] = ((f(t_ref[0]) + f(b_ref[0])) + f(b_ref[1])) + f(b_ref[2])

    extra = ([] if into is None else [into]) + ([] if after is None else [after])
    return pl.pallas_call(
        body, name=name,
        grid_spec=pltpu.PrefetchScalarGridSpec(
            num_scalar_prefetch=1, grid=(k // tk, n // tn),
            in_specs=[pl.BlockSpec((1, tk, tn), lambda i, jn, fl: (fl[0], i * fl[1], jn * fl[1])),
                      pl.BlockSpec((3, tk, tn), lambda i, jn, fl: (0, i * fl[1], jn * fl[1]))] + [_ANY] * len(extra),
            out_specs=pl.BlockSpec((1, tk, tn), lambda i, jn, fl: (layer, i * fl[1], jn * fl[1]))),
        out_shape=jax.ShapeDtypeStruct((2, k, n), F32),
        input_output_aliases={} if into is None else {3: 0},
        compiler_params=_cparams(("arbitrary", "arbitrary")),
    )(flags, t, b, *extra)


def _slab_block(k, n, itemsize=4):
    tk = (1 << 20) // (n * itemsize) // 16 * 16
    while 0 < tk < k and k % tk:
        tk -= 16
    if 0 < tk < k:
        return tk, n
    if k * n * itemsize <= (2 << 20) or n % _LANES:
        return k, n
    tn = max(_LANES, (1 << 20) // (k * itemsize) // _LANES * _LANES)
    while n % tn:
        tn -= _LANES
    return k, tn


def _all_reduce_adamw(gs, ws, ms, vs):
    n = len(gs)

    def body(*refs):
        g_refs, w_refs, m_refs, v_refs = (refs[i * n:(i + 1) * n] for i in range(4))
        gsum, delta, m_out, v_out = (refs[(4 + i) * n:(5 + i) * n] for i in range(4))
        slots = refs[8 * n:9 * n]
        send_sems, recv_sems = refs[9 * n:]
        x, y, c = _mesh_pos()
        me = 4 * x + 2 * y + c
        copies = []
        for rel in range(1, 8):
            bx, by, bc = (rel >> 2) & 1, (rel >> 1) & 1, rel & 1
            peer = (1 - x if bx else x, 1 - y if by else y, 1 - c if bc else c)
            for a in range(n):
                cp = _remote(g_refs[a], slots[a].at[me], send_sems.at[a, rel - 1], recv_sems.at[a, rel - 1], peer)
                cp.start()
                copies.append(cp)
        for a in range(n):
            slots[a][me] = g_refs[a][...]
        for cp in copies:
            cp.wait()
        for a in range(n):
            acc = slots[a][0]
            for d in range(1, 8):
                acc = acc + slots[a][d]
            gsum[a][...] = acc
            delta[a][...], m_out[a][...], v_out[a][...] = _adamw_math(w_refs[a][...], acc, m_refs[a][...],
                                                                      v_refs[a][...])

    vmem = pl.BlockSpec(memory_space=pltpu.VMEM)
    outs = pl.pallas_call(
        body, name="all_reduce_adamw", in_specs=[vmem] * (4 * n), out_specs=[vmem] * (4 * n),
        out_shape=[jax.ShapeDtypeStruct(g.shape, F32) for g in gs] * 4,
        scratch_shapes=([pltpu.VMEM((8,) + g.shape, F32) for g in gs]
                        + [pltpu.SemaphoreType.DMA((n, 7)), pltpu.SemaphoreType.DMA((n, 7))]),
        compiler_params=pltpu.CompilerParams(vmem_limit_bytes=_VMEM_LIMIT),
    )(*gs, *ws, *ms, *vs)
    return outs[:n], outs[n:2 * n], outs[2 * n:3 * n], outs[3 * n:]


def _swap_rope(a):
    h = QK_ROPE // 2
    return jnp.concatenate([a[..., h:], a[..., :h]], axis=-1)


def _swap_rope_rows(a):
    h = QK_ROPE // 2
    return jnp.concatenate([a[h:], a[:h]], axis=0)


_FFN_CB = 256


def _interleave_rows(a, cb):
    r, c = a.shape
    return a.reshape(2, r // (2 * cb), cb, c).transpose(1, 0, 2, 3).reshape(r, c)


def _deinterleave_rows(a, cb):
    r, c = a.shape
    return a.reshape(r // (2 * cb), 2, cb, c).transpose(1, 0, 2, 3).reshape(r, c)


class _InLayout:
    def __init__(self, d):
        self.d = d
        self.gates = 0
        self.a = 3 * d
        self.b = 4 * d
        self.kv = 5 * d
        self.q = self.kv + 256
        self.krm = self.q + 384
        self.krs = self.krm + HEAD_PAD
        self.width = self.krs + 2 * HEAD_PAD


def _prep_layer(wl, d):
    lay = _InLayout(d)
    w_in = wl["w_in"]
    dt = w_in.dtype
    a, b = w_in[0:d], w_in[d:2 * d]
    q, kv = w_in[2 * d:2 * d + 384], w_in[2 * d + 384:2 * d + 640]
    kr = w_in[2 * d + 640:2 * d + 640 + QK_ROPE]
    gates = w_in[2 * d + 640 + QK_ROPE:]
    z = lambda n: jnp.zeros((n, d), dt)
    krm = jnp.concatenate([z(QK_NOPE), kr, z(HEAD_PAD - QK_NOPE - QK_ROPE)], axis=0)
    krs = jnp.concatenate([z(QK_NOPE), _swap_rope_rows(kr), z(HEAD_PAD - QK_NOPE - QK_ROPE)], axis=0)
    out = dict(wl)
    out["w_in"] = jnp.concatenate([gates, _interleave_rows(a, _LANES), b, kv, q, krm, krs,
                                   z(lay.width - lay.krs - HEAD_PAD)], axis=0)
    uq = wl["mla_w_uq"].reshape(-1, N_HEADS, QK_NOPE + QK_ROPE)
    nq = uq.shape[0]
    nope, pe = uq[..., :QK_NOPE], uq[..., QK_NOPE:]
    zq = lambda n: jnp.zeros((nq, N_HEADS, n), dt)
    main = jnp.concatenate([nope, pe, zq(HEAD_PAD - QK_NOPE - QK_ROPE)], axis=-1).reshape(nq, -1)
    swapped = jnp.concatenate([zq(QK_NOPE), _swap_rope(pe), zq(HEAD_PAD - QK_NOPE - QK_ROPE)], axis=-1).reshape(nq, -1)
    out["mla_w_uq"] = jnp.concatenate([main, swapped], axis=1)
    ukv = wl["mla_w_ukv"].reshape(-1, N_HEADS, QK_NOPE + V_HEAD)
    nkv = ukv.shape[0]
    zk = jnp.zeros((nkv, N_HEADS, HEAD_PAD - QK_NOPE), dt)
    zv = jnp.zeros((nkv, N_HEADS, HEAD_PAD - V_HEAD), dt)
    out["mla_w_ukv"] = jnp.concatenate([jnp.concatenate([ukv[..., :QK_NOPE], zk], axis=-1).reshape(nkv, -1),
                                        jnp.concatenate([ukv[..., QK_NOPE:], zv], axis=-1).reshape(nkv, -1)], axis=1)
    wo = wl["mla_w_o"].reshape(N_HEADS, V_HEAD, -1)
    out["mla_w_o"] = jnp.concatenate([wo, jnp.zeros((N_HEADS, HEAD_PAD - V_HEAD, wo.shape[-1]), dt)],
                                     axis=1).reshape(N_HEADS * HEAD_PAD, -1)
    return out


def _unprep_grads(g, d):
    lay = _InLayout(d)
    gi = g["w_in"]
    kr = (gi[lay.krm + QK_NOPE:lay.krm + QK_NOPE + QK_ROPE]
          + _swap_rope_rows(gi[lay.krs + QK_NOPE:lay.krs + QK_NOPE + QK_ROPE]))
    out = dict(g)
    out["w_in"] = jnp.concatenate([_deinterleave_rows(gi[lay.a:lay.a + d], _LANES), gi[lay.b:lay.b + d],
                                   gi[lay.q:lay.q + 384], gi[lay.kv:lay.kv + 256], kr,
                                   gi[lay.gates:lay.gates + 3 * d]], axis=0)
    hw = N_HEADS * HEAD_PAD
    gq = g["mla_w_uq"]
    nq = gq.shape[0]
    main = gq[:, :hw].reshape(nq, N_HEADS, HEAD_PAD)
    swapped = gq[:, hw:].reshape(nq, N_HEADS, HEAD_PAD)
    pe = main[..., QK_NOPE:QK_NOPE + QK_ROPE] + _swap_rope(swapped[..., QK_NOPE:QK_NOPE + QK_ROPE])
    out["mla_w_uq"] = jnp.concatenate([main[..., :QK_NOPE], pe], axis=-1).reshape(nq, -1)
    gkv = g["mla_w_ukv"]
    nkv = gkv.shape[0]
    out["mla_w_ukv"] = jnp.concatenate([gkv[:, :hw].reshape(nkv, N_HEADS, HEAD_PAD)[..., :QK_NOPE],
                                        gkv[:, hw:].reshape(nkv, N_HEADS, HEAD_PAD)[..., :V_HEAD]],
                                       axis=-1).reshape(nkv, -1)
    go = g["mla_w_o"]
    out["mla_w_o"] = go.reshape(N_HEADS, HEAD_PAD, -1)[:, :V_HEAD].reshape(N_HEADS * V_HEAD, -1)
    return out


def _rope_tables(positions):
    s = positions.shape[0]
    inv = ROPE_THETA ** (-jnp.arange(0, QK_ROPE, 2, dtype=F32) / QK_ROPE)
    ang = positions.astype(F32)[:, None] * inv
    cos, sin = jnp.cos(ang), jnp.sin(ang)
    tail = jnp.zeros((s, HEAD_PAD - QK_NOPE - QK_ROPE), F32)
    tc = jnp.concatenate([jnp.ones((s, QK_NOPE), F32), cos, cos, tail], axis=1)
    ts = jnp.concatenate([jnp.zeros((s, QK_NOPE), F32), -sin, sin, tail], axis=1)
    return tc, ts


def _row(v):
    return v.reshape(1, -1)


def _layer_fwd(x, h, w, g_next, tc, ts):
    d = x.shape[1]
    lay = _InLayout(d)
    cw = d // 2
    blk = lambda off, width: off // width
    p = _mm("mm_in", h, w["w_in"], tb=True)
    z1 = _glu_conv_fwd(p, blk(lay.a, 2 * _LANES), w["conv_dw_w"], _row(w["conv_dw_b"]))
    ln_a = [_row(w["conv_ln_g"]), _row(w["conv_ln_b"])]
    (z3,) = _row_fwd("ln_silu_fwd", _f_ln_silu, [(z1, cw, 0)], ln_a, [(cw, _MXU_DT)])
    ya = _mm("mm_conv_out", z3, w["conv_out_w"])
    ln_b = [_row(w["sg_ln_g"]), _row(w["sg_ln_b"])]
    u, vn = _row_fwd("sg_pre_fwd", _f_sg_pre, [(p, cw, blk(lay.b, cw)), (p, cw, blk(lay.b + cw, cw))], ln_b,
                     [(cw, F32), (cw, _MXU_DT)])
    bcol = w["sg_b"].reshape(SG_GROUPS, SG_CHUNK, 1)
    ub = _sg_mix_fwd(u, vn, w["sg_w"], bcol)
    yb = _mm("mm_sg_out", ub, w["sg_out_w"])
    (qn,) = _row_fwd("q_norm_fwd", _f_rms, [(p, 384, blk(lay.q, 384))], [_row(w["mla_q_norm_g"])], [(384, _MXU_DT)])
    (kvn,) = _row_fwd("kv_norm_fwd", _f_rms, [(p, 256, blk(lay.kv, 256))], [_row(w["mla_kv_norm_g"])],
                      [(256, _MXU_DT)])
    q2 = _mm("mm_uq", qn, w["mla_w_uq"])
    kv2 = _mm("mm_ukv", kvn, w["mla_w_ukv"])
    qf, kf, vf = _rope_fwd(q2, kv2, p, blk(lay.krm, HEAD_PAD), blk(lay.krs, HEAD_PAD), tc, ts)
    o = _attn_fwd(qf, kf, vf)
    yc = _mm("mm_o", o, w["mla_w_o"])
    gate_rows = [(p, d, 0), (p, d, 1), (p, d, 2)]
    (merged,) = _row_fwd("merge_fwd", _f_merge, gate_rows + [(ya, d, 0), (yb, d, 0), (yc, d, 0)], [], [(d, _MXU_DT)])
    t = _mm("mm_out", merged, w["w_out"])
    x1, h2 = _row_fwd("resid_mix_fwd", _f_resid_rms_rms, [(x, d, 0), (t, d, 0)],
                      [_row(w["mix_post_g"]), _row(w["ffn_pre_g"])], [(d, F32), (d, _MXU_DT)])
    up = _mm("mm_up", h2, w["ffn_w_up"])
    act = _conv_geglu_fwd(up, w["ffn_dw_w"], _row(w["ffn_dw_b"]))
    dn = _mm("mm_down", act, w["ffn_w_down"])
    if g_next is None:
        (x2,) = _row_fwd("resid_ffn_last_fwd", _f_resid_rms, [(x1, d, 0), (dn, d, 0)], [_row(w["ffn_post_g"])],
                         [(d, F32)])
        h_next = None
    else:
        x2, h_next = _row_fwd("resid_ffn_fwd", _f_resid_rms_rms, [(x1, d, 0), (dn, d, 0)],
                              [_row(w["ffn_post_g"]), _row(g_next)], [(d, F32), (d, _MXU_DT)])
    saved = dict(x=x, h=h, p=p, z1=z1, z3=z3, ya=ya, u=u, vn=vn, ub=ub, yb=yb, qn=qn, kvn=kvn, qf=qf, kf=kf, vf=vf, o=o,
                 yc=yc, merged=merged, t=t, x1=x1, h2=h2, up=up, act=act, dn=dn, bcol=bcol)
    return x2, h_next, saved


def _layer_bwd(dx2, dh_next, w, g_next, sv, tc, ts):
    d = dx2.shape[1]
    lay = _InLayout(d)
    cw = d // 2
    blk = lambda off, width: off // width
    lo = _MXU_DT
    g = {}
    x1, dn = sv["x1"], sv["dn"]
    if dh_next is None:
        dx1, ddn, g["ffn_post_g"] = _row_bwd("resid_ffn_last_bwd", _f_resid_rms, [(x1, d, 0), (dn, d, 0)],
                                             [_row(w["ffn_post_g"])], [(dx2, d, 0)], [(0, F32), (1, lo)], [0])
    else:
        dx1, ddn, g["ffn_post_g"], g["next_pre_g"] = _row_bwd(
            "resid_ffn_bwd", _f_resid_rms_rms, [(x1, d, 0), (dn, d, 0)], [_row(w["ffn_post_g"]), _row(g_next)],
            [(dx2, d, 0), (dh_next, d, 0)], [(0, F32), (1, lo)], [0, 1])
    dact = _mm("mm_down_dx", ddn, w["ffn_w_down"], tb=True)
    g["ffn_w_down"] = _mm("mm_down_dw", sv["act"], ddn, ta=True)
    dup, dw_halves, db_halves = _conv_geglu_bwd(sv["up"], w["ffn_dw_w"], _row(w["ffn_dw_b"]), dact)
    g["ffn_dw_w"] = jnp.concatenate([dw_halves[0], dw_halves[1]], axis=1)
    g["ffn_dw_b"] = jnp.concatenate([db_halves[0], db_halves[1]], axis=1)
    dh2 = _mm("mm_up_dx", dup, w["ffn_w_up"], tb=True, a_halves=True)
    g["ffn_w_up"] = _mm("mm_up_dw", sv["h2"], dup, ta=True, b_halves=True, out_quarters=True)
    dx, dt, g["mix_post_g"], g["ffn_pre_g"] = _row_bwd(
        "resid_mix_bwd", _f_resid_rms_rms, [(sv["x"], d, 0), (sv["t"], d, 0)],
        [_row(w["mix_post_g"]), _row(w["ffn_pre_g"])], [(dx1, d, 0), (dh2, d, 0)], [(0, F32), (1, lo)], [0, 1])
    dmerged = _mm("mm_out_dx", dt, w["w_out"], tb=True)
    g["w_out"] = _mm("mm_out_dw", sv["merged"], dt, ta=True)
    p = sv["p"]
    gate_rows = [(p, d, 0), (p, d, 1), (p, d, 2)]
    dp, dya, dyb, dyc = _row_bwd(
        "merge_bwd", _f_merge, gate_rows + [(sv["ya"], d, 0), (sv["yb"], d, 0), (sv["yc"], d, 0)], [],
        [(dmerged, d, 0)], [((0, 1, 2), lo), ((3,), lo), ((4,), lo), ((5,), lo)], [], place=(lay.width, 0))
    do = _mm("mm_o_dx", dyc, w["mla_w_o"], tb=True, out_dtype=lo)
    g["mla_w_o"] = _mm("mm_o_dw", sv["o"], dyc, ta=True)
    dqf, dkf, dvf = _attn_bwd(sv["qf"], sv["kf"], sv["vf"], do)
    dq2, dkv2, dp = _rope_bwd(dqf, dkf, dvf, tc, ts, dp, blk(lay.krm, 3 * HEAD_PAD))
    dkvn = _mm("mm_ukv_dx", dkv2, w["mla_w_ukv"], tb=True)
    g["mla_w_ukv"] = _mm("mm_ukv_dw", sv["kvn"], dkv2, ta=True)
    dqn = _mm("mm_uq_dx", dq2, w["mla_w_uq"], tb=True)
    g["mla_w_uq"] = _mm("mm_uq_dw", sv["qn"], dq2, ta=True)
    dp, g["mla_q_norm_g"] = _row_bwd("q_norm_bwd", _f_rms, [(p, 384, blk(lay.q, 384))], [_row(w["mla_q_norm_g"])],
                                     [(dqn, 384, 0)], [((0,), lo)], [0], place=(lay.width, blk(lay.q, 384)), into=dp)
    dp, g["mla_kv_norm_g"] = _row_bwd("kv_norm_bwd", _f_rms, [(p, 256, blk(lay.kv, 256))],
                                      [_row(w["mla_kv_norm_g"])], [(dkvn, 256, 0)], [((0,), lo)], [0],
                                      place=(lay.width, blk(lay.kv, 256)), into=dp)
    dub = _mm("mm_sg_out_dx", dyb, w["sg_out_w"], tb=True)
    g["sg_out_w"] = _mm("mm_sg_out_dw", sv["ub"], dyb, ta=True)
    du, dvn, g["sg_w"], dbcol = _sg_mix_bwd(sv["u"], sv["vn"], w["sg_w"], sv["bcol"], dub)
    g["sg_b"] = dbcol.reshape(SG_GROUPS, SG_CHUNK)
    dp, g["sg_ln_g"], g["sg_ln_b"] = _row_bwd(
        "sg_pre_bwd", _f_sg_pre, [(p, cw, blk(lay.b, cw)), (p, cw, blk(lay.b + cw, cw))],
        [_row(w["sg_ln_g"]), _row(w["sg_ln_b"])], [(du, cw, 0), (dvn, cw, 0)], [((0, 1), lo)], [0, 1],
        place=(lay.width, blk(lay.b, d)), into=dp)
    dz3 = _mm("mm_conv_out_dx", dya, w["conv_out_w"], tb=True)
    g["conv_out_w"] = _mm("mm_conv_out_dw", sv["z3"], dya, ta=True)
    dz1, g["conv_ln_g"], g["conv_ln_b"] = _row_bwd(
        "ln_silu_bwd", _f_ln_silu, [(sv["z1"], cw, 0)], [_row(w["conv_ln_g"]), _row(w["conv_ln_b"])],
        [(dz3, cw, 0)], [((0,), F32)], [0, 1])
    dp, g["conv_dw_w"], g["conv_dw_b"] = _glu_conv_bwd(p, blk(lay.a, 2 * _LANES), w["conv_dw_w"], dz1, dp)
    dh = _mm("mm_in_dx", dp, w["w_in"])
    g["w_in"] = _mm("mm_in_dw", dp, sv["h"], ta=True)
    return dx, dh, g


def _local_step(x, positions, target, layers):
    d = x.shape[1]
    tc, ts = _rope_tables(positions)
    ws = [_prep_layer(wl, d) for wl in layers]
    depth = len(ws)
    (h,) = _row_fwd("rms_first_fwd", _f_rms, [(x, d, 0)], [_row(ws[0]["mix_pre_g"])], [(d, _MXU_DT)])
    saved = []
    for l in range(depth):
        g_next = ws[l + 1]["mix_pre_g"] if l + 1 < depth else None
        x, h, sv = _layer_fwd(x, h, ws[l], g_next, tc, ts)
        saved.append(sv)
    loss, dx = _loss_head(x, target)
    grads = [None] * depth
    dh = None
    for l in reversed(range(depth)):
        g_next = ws[l + 1]["mix_pre_g"] if l + 1 < depth else None
        dx, dh, g = _layer_bwd(dx, dh, ws[l], g_next, saved[l], tc, ts)
        if "next_pre_g" in g:
            grads[l + 1]["mix_pre_g"] = g.pop("next_pre_g")
        grads[l] = g
    x0 = saved[0]["x"]
    grad_x, grads[0]["mix_pre_g"] = _row_bwd("rms_first_bwd", _f_x_rms, [(x0, d, 0)], [_row(ws[0]["mix_pre_g"])],
                                             [(dx, d, 0), (dh, d, 0)], [(0, F32)], [0])
    return loss, grad_x, [_unprep_grads(g, d) for g in grads]


_MATRICES = ("w_in", "conv_out_w", "sg_out_w", "mla_w_uq", "mla_w_ukv", "mla_w_o", "w_out", "ffn_w_up", "ffn_w_down")
_F32_GATHERED = ("conv_dw_w", "ffn_dw_w")
_RS_DT = jnp.bfloat16


_ROW_SHARDED = SHARDED_MID + ("w_in",)


_GATHERED = SHARDED + _F32_GATHERED
_RS_CORE0 = ("w_in", "ffn_w_down")


def _layer_shards(w, l):
    hi = {n: w[n][l].astype(jnp.bfloat16) for n in SHARDED}
    lo = [(w[n][l] - hi[n].astype(F32)).astype(jnp.bfloat16) for n in _F32_GATHERED]
    return [hi[n] for n in SHARDED] + lo


def _layer_weights(w, l, gathered):
    wl = {n: w[n][l] for n in REPLICATED}
    for n, g in zip(_GATHERED, gathered):
        whole = jnp.concatenate([g[j] for j in range(4)], axis=0 if n in _ROW_SHARDED else 1)
        if n in wl:
            wl[n] = wl[n].astype(F32) + whole.astype(F32)
        else:
            wl[n] = whole.astype(_MXU_DT) if n in _MATRICES else whole
    return wl


def _by_destination(name, gl):
    if gl.ndim == 3:
        return gl
    k, n = gl.shape
    if name in _ROW_SHARDED:
        return gl.reshape(4, k // 4, n)
    return gl.reshape(k, 4, n // 4).transpose(1, 0, 2)


def kernel(x, positions, mix_pre_g, mix_post_g, ffn_pre_g, ffn_post_g, w_in, conv_dw_w, conv_dw_b, conv_ln_g, conv_ln_b, conv_out_w, sg_ln_g, sg_ln_b, sg_w, sg_b, sg_out_w, mla_q_norm_g, mla_w_uq, mla_kv_norm_g, mla_w_ukv, mla_w_o, w_out, ffn_w_up, ffn_dw_w, ffn_dw_b, ffn_w_down, loss_target, m_mix_pre_g, m_mix_post_g, m_ffn_pre_g, m_ffn_post_g, m_w_in, m_conv_dw_w, m_conv_dw_b, m_conv_ln_g, m_conv_ln_b, m_conv_out_w, m_sg_ln_g, m_sg_ln_b, m_sg_w, m_sg_b, m_sg_out_w, m_mla_q_norm_g, m_mla_w_uq, m_mla_kv_norm_g, m_mla_w_ukv, m_mla_w_o, m_w_out, m_ffn_w_up, m_ffn_dw_w, m_ffn_dw_b, m_ffn_w_down, v_mix_pre_g, v_mix_post_g, v_ffn_pre_g, v_ffn_post_g, v_w_in, v_conv_dw_w, v_conv_dw_b, v_conv_ln_g, v_conv_ln_b, v_conv_out_w, v_sg_ln_g, v_sg_ln_b, v_sg_w, v_sg_b, v_sg_out_w, v_mla_q_norm_g, v_mla_w_uq, v_mla_kv_norm_g, v_mla_w_ukv, v_mla_w_o, v_w_out, v_ffn_w_up, v_ffn_dw_w, v_ffn_dw_b, v_ffn_w_down):
    args = dict(locals())
    w = {n: args[n] for n in WEIGHTS}
    m = {n: args["m_" + n] for n in WEIGHTS}
    v = {n: args["v_" + n] for n in WEIGHTS}
    depth = mix_pre_g.shape[0]

    assert depth == 2, "the two cores of a chip split the communication by layer"
    for t in (w, m, v):
        t["w_in"] = jnp.swapaxes(t["w_in"], 1, 2)
    d = x.shape[-1]
    mesh_x, mesh_y, mesh_c = _mesh_pos()
    my_chip = 2 * mesh_x + mesh_y
    names = list(SHARDED)
    whole = lambda ref, chip, k: ref
    to_my_slot = lambda ref, chip, k: ref.at[2 * lax.axis_index("x") + lax.axis_index("y")]
    block_of_chip = lambda ref, chip, k: ref.at[_chip_index(chip)]
    slot_k = lambda ref, chip, k: ref.at[k]

    gathered0 = _ag_forward(_layer_shards(w, 0), None, 0, False)
    shards1 = _layer_shards(w, 1)
    lands = [jax.ShapeDtypeStruct((4,) + a.shape, a.dtype) for a in shards1]
    ag_owner = [1] * len(shards1)
    sems_s, sems_r, shards1, lands1, token = _split_start("ag1_start", ag_owner, shards1, lands, whole, to_my_slot,
                                                          gathered0[0])
    tc, ts = _rope_tables(positions[0])
    ws0 = _prep_layer(_layer_weights(w, 0, gathered0), d)
    x0 = x[0] + token[0, 0]
    (h0,) = _row_fwd("rms_first_fwd", _f_rms, [(x0, d, 0)], [_row(ws0["mix_pre_g"])], [(d, _MXU_DT)])
    x1, h1, sv0 = _layer_fwd(x0, h0, ws0, w["mix_pre_g"][1], tc, ts)
    shards1, lands1 = _split_wait("ag1_wait", ag_owner, sems_s, sems_r, shards1, lands1, x1, whole, to_my_slot)
    ws1 = _prep_layer(_layer_weights(w, 1, _ag_forward(shards1, lands1, 1, True)), d)
    x2, _, sv1 = _layer_fwd(x1, h1, ws1, None, tc, ts)
    loss, dx = _loss_head(x2, loss_target[0])
    loss = lax.psum(loss[0, 0], ("x", "y", "c"))

    owners = [0 if n in _RS_CORE0 else 1 for n in names]
    flags = [jnp.stack([my_chip, (mesh_c == o).astype(jnp.int32)]).astype(jnp.int32) for o in owners]

    def pair_sums(l, grads_l):
        gd = [_by_destination(n, grads_l[n]) for n in names]
        got = _pair_exchange("rs_swap%d" % l, gd, owners, True)
        return [_add_pair("rs_pair%d_%s" % (l, n), g, a, fl) for n, g, a, fl in zip(names, gd, got, flags)]

    dx, dh, gk1 = _layer_bwd(dx, None, ws1, None, sv1, tc, ts)
    grads1 = _unprep_grads(gk1, d)
    t1 = pair_sums(1, grads1)
    lands = [jax.ShapeDtypeStruct((3,) + t.shape[1:], t.dtype) for t in t1]
    sems_s, sems_r, t1, b1, token = _split_start("rs1_start", owners, t1, lands, block_of_chip, slot_k, dh)
    dx = dx + token[0, 0]
    dx, dh, gk0 = _layer_bwd(dx, dh, ws0, ws1["mix_pre_g"], sv0, tc, ts)
    grads1["mix_pre_g"] = gk0.pop("next_pre_g")
    grad_x, gk0["mix_pre_g"] = _row_bwd("rms_first_bwd", _f_x_rms, [(x0, d, 0)], [_row(ws0["mix_pre_g"])],
                                        [(dx, d, 0), (dh, d, 0)], [(0, F32)], [0])
    t1, b1 = _split_wait("rs1_wait", owners, sems_s, sems_r, t1, b1, grad_x, block_of_chip, slot_k)
    grads0 = _unprep_grads(gk0, d)
    t0 = pair_sums(0, grads0)
    lands = [jax.ShapeDtypeStruct((3,) + t.shape[1:], t.dtype) for t in t0]
    sems_s, sems_r, t0, b0, token = _split_start("rs0_start", owners, t0, lands, block_of_chip, slot_k, grad_x)

    def finish(l, sums, updates):
        sums = _pair_exchange("rs_join%d" % l, sums, owners, False, layer=l)
        updates = [_adamw("adamw%d_%s" % (l, n), w[n], sums[i], m[n], v[n], l, updates[i])
                   for i, n in enumerate(names)]
        return sums, updates

    sums = [_add_quads("rs_sum1_" + n, t1[i], b1[i], 1, flags[i], None, token) for i, n in enumerate(names)]
    sums, updates = finish(1, sums, [None] * len(names))
    out = {}
    rep = list(REPLICATED)
    grads = [grads0, grads1]
    g_rep = [jnp.stack([grads[l][n].reshape(w[n].shape[1:]) for l in range(depth)]) for n in rep]
    for n, *res in zip(rep, *_all_reduce_adamw(g_rep, [w[n] for n in rep], [m[n] for n in rep], [v[n] for n in rep])):
        out[n] = tuple(res)
    t0, b0 = _split_wait("rs0_wait", owners, sems_s, sems_r, t0, b0, updates[0][0], block_of_chip, slot_k)
    sums = [_add_quads("rs_sum0_" + n, t0[i], b0[i], 0, flags[i], sums[i]) for i, n in enumerate(names)]
    sums, updates = finish(0, sums, updates)
    for n, gr, upd in zip(names, sums, updates):
        out[n] = (gr, *upd)
    out["w_in"] = tuple(jnp.swapaxes(a, 1, 2) for a in out["w_in"])
    return (loss, grad_x[None], *[out[n][i] for i in range(4) for n in WEIGHTS])
```

```python
import functools
import math

import jax
import jax.numpy as jnp
from jax import lax
from jax.experimental import pallas as pl
from jax.experimental.pallas import tpu as pltpu

F32 = jnp.float32
_MXU_DT = jnp.bfloat16
_VMEM_LIMIT = 48 * 1024 * 1024
_LANES = 128
_MESH = pl.DeviceIdType.MESH

N_HEADS = 8
QK_NOPE = 64
QK_ROPE = 32
V_HEAD = 64
HEAD_PAD = 128
SG_GROUPS = 4
SG_CHUNK = 128
CONV_K = 31
FFN_K = 3
ROPE_THETA = 10000.0
EPS = 1e-6
ADAM_LR, ADAM_B1, ADAM_B2, ADAM_EPS, ADAM_WD, ADAM_STEP = 0.001, 0.9, 0.999, 1e-08, 0.01, 10

SHARDED_LAST = ("w_in", "conv_dw_w", "conv_out_w", "sg_out_w", "mla_w_uq", "mla_w_ukv", "mla_w_o", "ffn_w_up",
                "ffn_dw_w")
SHARDED_MID = ("w_out", "ffn_w_down")
SHARDED = SHARDED_LAST + SHARDED_MID
WEIGHTS = ("mix_pre_g", "mix_post_g", "ffn_pre_g", "ffn_post_g", "w_in", "conv_dw_w", "conv_dw_b", "conv_ln_g",
           "conv_ln_b", "conv_out_w", "sg_ln_g", "sg_ln_b", "sg_w", "sg_b", "sg_out_w", "mla_q_norm_g", "mla_w_uq",
           "mla_kv_norm_g", "mla_w_ukv", "mla_w_o", "w_out", "ffn_w_up", "ffn_dw_w", "ffn_dw_b", "ffn_w_down")
REPLICATED = tuple(n for n in WEIGHTS if n not in SHARDED)


def _cparams(sem=None):
    return pltpu.CompilerParams(dimension_semantics=sem, vmem_limit_bytes=_VMEM_LIMIT)


def _pick(n, cands):
    for c in cands:
        if n % c == 0:
            return c
    return n


def _largest_tile(dim, cap):
    for t in range(min(cap, dim) // _LANES * _LANES, 0, -_LANES):
        if dim % t == 0:
            return t
    return dim


_MM_VMEM_BUDGET = 36 * 1024 * 1024
_MM_TM_CAP, _MM_TN_CAP, _MM_TK_CAP = 1024, 1536, 3072


def _mm(name, a, b, *, ta=False, tb=False, out_dtype=F32, a_halves=False, b_halves=False, out_quarters=False):
    assert not (a_halves and ta) and not (b_halves and tb)
    if a_halves:
        m, kdim = a.shape[1], 2 * a.shape[2]
    else:
        (kdim, m) = a.shape if ta else a.shape[::-1]
    if b_halves:
        kdim2, n = b.shape[1], 2 * b.shape[2]
    else:
        (n, kdim2) = b.shape if tb else b.shape[::-1]
    assert kdim == kdim2, (a.shape, b.shape, ta, tb)
    tk = _largest_tile(kdim // 2 if a_halves else kdim, _MM_TK_CAP)
    tn = _largest_tile(n // 4 if out_quarters else (n // 2 if b_halves else n), _MM_TN_CAP)
    nk = kdim // tk
    ab, bb, ob = a.dtype.itemsize, b.dtype.itemsize, jnp.dtype(out_dtype).itemsize
    tm = _largest_tile(m, _MM_TM_CAP)
    vmem = lambda t: 2 * (t * tk * ab + tk * tn * bb + t * tn * ob) + (t * tn * 4 if nk > 1 else 0)
    while vmem(tm) > _MM_VMEM_BUDGET and tm > _LANES:
        tm = _largest_tile(m, tm - _LANES)
    dims = (((0 if ta else 1,), (1 if tb else 0,)), ((), ()))

    def dot(a_ref, b_ref):
        return lax.dot_general(a_ref[...].astype(_MXU_DT), b_ref[...].astype(_MXU_DT), dims,
                               preferred_element_type=F32)

    def body_one(a_ref, b_ref, o_ref):
        o_ref[...] = dot(a_ref, b_ref).astype(o_ref.dtype)

    def body_acc(a_ref, b_ref, o_ref, acc_ref):
        k = pl.program_id(2)

        @pl.when(k == 0)
        def _():
            acc_ref[...] = jnp.zeros_like(acc_ref)

        acc_ref[...] += dot(a_ref, b_ref)

        @pl.when(k == nk - 1)
        def _():
            o_ref[...] = acc_ref[...].astype(o_ref.dtype)

    if a_halves:
        per = nk // 2
        a_spec = pl.BlockSpec((None, tm, tk), lambda i, j, k: (k // per, i, k % per))
    elif ta:
        a_spec = pl.BlockSpec((tk, tm), lambda i, j, k: (k, i))
    else:
        a_spec = pl.BlockSpec((tm, tk), lambda i, j, k: (i, k))
    if b_halves:
        per_b = n // 2 // tn
        b_spec = pl.BlockSpec((None, tk, tn), lambda i, j, k: (j // per_b, k, j % per_b))
    elif tb:
        b_spec = pl.BlockSpec((tn, tk), lambda i, j, k: (j, k))
    else:
        b_spec = pl.BlockSpec((tk, tn), lambda i, j, k: (k, j))
    if out_quarters:
        per_o = n // 4 // tn
        o_spec = pl.BlockSpec((None, tm, tn), lambda i, j, k: (j // per_o, i, j % per_o))
        o_shape = jax.ShapeDtypeStruct((4, m, n // 4), out_dtype)
    else:
        o_spec = pl.BlockSpec((tm, tn), lambda i, j, k: (i, j))
        o_shape = jax.ShapeDtypeStruct((m, n), out_dtype)
    return pl.pallas_call(
        body_one if nk == 1 else body_acc, name=name, grid=(m // tm, n // tn, nk),
        in_specs=[a_spec, b_spec], out_specs=o_spec, out_shape=o_shape,
        scratch_shapes=[] if nk == 1 else [pltpu.VMEM((tm, tn), F32)],
        compiler_params=_cparams(("parallel", "parallel", "arbitrary")),
    )(a, b)


def _row_spec(tm, width, idx):
    return pl.BlockSpec((tm, width), lambda i: (i, idx))


def _full_spec(shape):
    zeros = (0,) * len(shape)
    return pl.BlockSpec(shape, lambda i: zeros)


def _row_fwd(name, fn, rows, params, outs, tm=256):
    s = rows[0][0].shape[0]
    nr, npar = len(rows), len(params)

    def body(*refs):
        vals = [r[...].astype(F32) for r in refs[:nr + npar]]
        res = fn(*vals)
        for o_ref, r in zip(refs[nr + npar:], res):
            o_ref[...] = r.astype(o_ref.dtype)

    return pl.pallas_call(
        body, name=name, grid=(s // tm,),
        in_specs=[_row_spec(tm, w, i) for _, w, i in rows] + [_full_spec(p.shape) for p in params],
        out_specs=[_row_spec(tm, w, 0) for w, _ in outs],
        out_shape=[jax.ShapeDtypeStruct((s, w), dt) for w, dt in outs],
        compiler_params=_cparams(("parallel",)),
    )(*[r[0] for r in rows], *params)


def _row_bwd(name, fn, rows, params, cots, row_grads, param_grads, tm=256, place=None, into=None):
    s = rows[0][0].shape[0]
    nr, npar, nc = len(rows), len(params), len(cots)
    row_grads = [((idxs,) if isinstance(idxs, int) else tuple(idxs), dt) for idxs, dt in row_grads]
    widths = [sum(rows[i][1] for i in idxs) for idxs, _ in row_grads]

    def body(*refs):
        i = pl.program_id(0)
        vals = [r[...].astype(F32) for r in refs[:nr + npar]]
        cvals = tuple(r[...].astype(F32) for r in refs[nr + npar:nr + npar + nc])
        _, vjp = jax.vjp(fn, *vals)
        grads = vjp(cvals)
        outs = refs[nr + npar + nc + (into is not None):]
        for o_ref, (idxs, _) in zip(outs, row_grads):
            pos = 0
            for idx in idxs:
                o_ref[:, pos:pos + rows[idx][1]] = grads[idx].astype(o_ref.dtype)
                pos += rows[idx][1]
        for o_ref, idx in zip(outs[len(row_grads):], param_grads):
            @pl.when(i == 0)
            def _(o_ref=o_ref):
                o_ref[...] = jnp.zeros_like(o_ref)

            o_ref[...] += grads[nr + idx]

    out_specs = [_row_spec(tm, w, 0) for w in widths] + [_full_spec(params[idx].shape) for idx in param_grads]
    out_shape = ([jax.ShapeDtypeStruct((s, w), dt) for w, (_, dt) in zip(widths, row_grads)]
                 + [jax.ShapeDtypeStruct(params[idx].shape, F32) for idx in param_grads])
    extra, aliases = [], {}
    if place is not None:
        out_specs[0] = _row_spec(tm, widths[0], place[1])
        out_shape[0] = jax.ShapeDtypeStruct((s, place[0]), row_grads[0][1])
    if into is not None:
        extra, aliases = [into], {nr + npar + nc: 0}
    return pl.pallas_call(
        body, name=name, grid=(s // tm,),
        in_specs=([_row_spec(tm, w, i) for _, w, i in rows] + [_full_spec(p.shape) for p in params]
                  + [_row_spec(tm, w, i) for _, w, i in cots] + [_ANY] * len(extra)),
        out_specs=out_specs, out_shape=out_shape, input_output_aliases=aliases,
        compiler_params=_cparams(("arbitrary",)),
    )(*[r[0] for r in rows], *params, *[c[0] for c in cots], *extra)


def _rms(x, g):
    return x * lax.rsqrt(jnp.mean(x * x, axis=-1, keepdims=True) + EPS) * g


def _ln(x, g, b):
    mu = jnp.mean(x, axis=-1, keepdims=True)
    xc = x - mu
    var = jnp.mean(xc * xc, axis=-1, keepdims=True)
    return xc * lax.rsqrt(var + EPS) * g + b


def _sigmoid(x):
    return 1.0 / (1.0 + jnp.exp(-x))


def _gelu(x):
    return x * (0.5 * (1.0 + jnp.tanh(math.sqrt(2.0 / math.pi) * (x + 0.044715 * (x * x * x)))))


def _f_rms(x, g):
    return (_rms(x, g),)


def _f_x_rms(x, g):
    return (x, _rms(x, g))


def _f_ln_silu(z, g, b):
    y = _ln(z, g, b)
    return (y * _sigmoid(y),)


def _f_sg_pre(bu, bv, g, b):
    return (_gelu(bu), _ln(_gelu(bv), g, b))


def _f_merge(g0, g1, g2, ya, yb, yc):
    return (_sigmoid(g0) * ya + _sigmoid(g1) * yb + _sigmoid(g2) * yc,)


def _f_resid_rms(x, t, g_post):
    return (x + _rms(t, g_post),)


def _f_resid_rms_rms(x, t, g_post, g_next):
    x1 = x + _rms(t, g_post)
    return (x1, _rms(x1, g_next))


def _f_geglu(zg, zv):
    return _gelu(zg) * zv


_CONV_TILE_ELEMS = 16 * 1024


def _conv_tr(c):
    return _CONV_TILE_ELEMS // c


def _conv_tile(zp_ref, w_ref, bias, k_taps, off, r0):
    c = zp_ref.shape[1]
    tr = _conv_tr(c)
    acc = jnp.broadcast_to(bias, (tr, c))
    for k in range(k_taps):
        acc = acc + w_ref[k:k + 1, :] * zp_ref[r0 + off + k:r0 + off + k + tr, :]
    return acc


def _conv_bwd_input_tile(dzp_ref, w_ref, k_taps, r0):
    c = dzp_ref.shape[1]
    tr = _conv_tr(c)
    acc = jnp.zeros((tr, c), F32)
    for k in range(k_taps):
        s0 = r0 + (k_taps - 1) - k
        acc = acc + w_ref[k:k + 1, :] * dzp_ref[s0:s0 + tr, :]
    return acc


def _conv_bwd_weight(dzp_ref, zp_ref, dw_ref, db_ref, k_taps, off, s):
    c = zp_ref.shape[1]
    tr = _conv_tr(c)
    fold = lambda v: jnp.sum(v.reshape(tr // 8, 8, c), axis=0)
    for k in range(k_taps):
        acc = jnp.zeros((8, c), F32)
        for r in range(s // tr):
            r0 = r * tr
            acc = acc + fold(dzp_ref[r0:r0 + tr, :] * zp_ref[r0 + off + k:r0 + off + k + tr, :])
        dw_ref[k:k + 1, :] = jnp.sum(acc, axis=0, keepdims=True)
    acc = jnp.zeros((8, c), F32)
    for r in range(s // tr):
        acc = acc + fold(dzp_ref[r * tr:(r + 1) * tr, :])
    db_ref[...] = jnp.sum(acc, axis=0, keepdims=True)


def _glu_conv_fwd(p, blk0, w, b):
    s = p.shape[0]
    k_taps, c = w.shape
    cb, pad = _LANES, 32
    off = pad - (k_taps - 1)

    def body(a_ref, w_ref, b_ref, o_ref, zp_ref):
        zp_ref[0:pad, :] = jnp.zeros((pad, cb), F32)
        zp_ref[pad:pad + s, :] = a_ref[:, 0:cb] * _sigmoid(a_ref[:, cb:2 * cb])
        tr = _conv_tr(cb)
        for r in range(s // tr):
            o_ref[r * tr:(r + 1) * tr, :] = _conv_tile(zp_ref, w_ref, b_ref[...], k_taps, off, r * tr)

    return pl.pallas_call(
        body, name="glu_conv_fwd", grid=(c // cb,),
        in_specs=[pl.BlockSpec((s, 2 * cb), lambda j: (0, blk0 + j)),
                  pl.BlockSpec((k_taps, cb), lambda j: (0, j)), pl.BlockSpec((1, cb), lambda j: (0, j))],
        out_specs=pl.BlockSpec((s, cb), lambda j: (0, j)),
        out_shape=jax.ShapeDtypeStruct((s, c), F32),
        scratch_shapes=[pltpu.VMEM((s + pad, cb), F32)],
        compiler_params=_cparams(("parallel",)),
    )(p, w, b)


def _glu_conv_bwd(p, blk0, w, dz, dp):
    s = p.shape[0]
    k_taps, c = w.shape
    cb, pad = _LANES, 32
    off = pad - (k_taps - 1)

    def body(a_ref, w_ref, dz_ref, dp_in, da_ref, dw_ref, db_ref, zp_ref, dzp_ref):
        zp_ref[0:pad, :] = jnp.zeros((pad, cb), F32)
        zp_ref[pad:pad + s, :] = a_ref[:, 0:cb] * _sigmoid(a_ref[:, cb:2 * cb])
        dzp_ref[0:s, :] = dz_ref[...]
        dzp_ref[s:s + pad, :] = jnp.zeros((pad, cb), F32)
        tr = _conv_tr(cb)
        for r in range(s // tr):
            rows = slice(r * tr, (r + 1) * tr)
            dz0 = _conv_bwd_input_tile(dzp_ref, w_ref, k_taps, r * tr)
            sg = _sigmoid(a_ref[rows, cb:2 * cb])
            da_ref[rows, 0:cb] = (dz0 * sg).astype(da_ref.dtype)
            da_ref[rows, cb:2 * cb] = (dz0 * a_ref[rows, 0:cb] * sg * (1.0 - sg)).astype(da_ref.dtype)
        _conv_bwd_weight(dzp_ref, zp_ref, dw_ref, db_ref, k_taps, off, s)

    return pl.pallas_call(
        body, name="glu_conv_bwd", grid=(c // cb,),
        in_specs=[pl.BlockSpec((s, 2 * cb), lambda j: (0, blk0 + j)),
                  pl.BlockSpec((k_taps, cb), lambda j: (0, j)), pl.BlockSpec((s, cb), lambda j: (0, j)), _ANY],
        out_specs=[pl.BlockSpec((s, 2 * cb), lambda j: (0, blk0 + j)),
                   pl.BlockSpec((k_taps, cb), lambda j: (0, j)), pl.BlockSpec((1, cb), lambda j: (0, j))],
        out_shape=[jax.ShapeDtypeStruct(dp.shape, dp.dtype),
                   jax.ShapeDtypeStruct((k_taps, c), F32), jax.ShapeDtypeStruct((1, c), F32)],
        scratch_shapes=[pltpu.VMEM((s + pad, cb), F32), pltpu.VMEM((s + pad, cb), F32)],
        input_output_aliases={3: 0},
        compiler_params=_cparams(("parallel",)),
    )(p, w, dz, dp)


def _conv_geglu_fwd(up, w, b):
    s, f2 = up.shape
    f = f2 // 2
    k_taps = w.shape[0]
    cb, pad = _FFN_CB, 8
    off = pad - (k_taps - 1)
    nb = f // cb

    def body(ug_ref, uv_ref, wg_ref, wv_ref, bg_ref, bv_ref, o_ref, z_ref, w_ref, b_ref):
        _pair(w_ref, wg_ref[...], wv_ref[...], cb)
        _pair(b_ref, bg_ref[...], bv_ref[...], cb)
        z_ref[0:pad, :] = jnp.zeros((pad, 2 * cb), F32)
        z_ref[pad:pad + s, 0:cb] = ug_ref[...]
        z_ref[pad:pad + s, cb:2 * cb] = uv_ref[...]
        tr = _conv_tr(2 * cb)
        for r in range(s // tr):
            z = _conv_tile(z_ref, w_ref, b_ref[...], k_taps, off, r * tr)
            o_ref[r * tr:(r + 1) * tr, :] = _f_geglu(z[:, 0:cb], z[:, cb:2 * cb]).astype(o_ref.dtype)

    two = lambda rows_: [pl.BlockSpec((rows_, cb), lambda j: (0, j)), pl.BlockSpec((rows_, cb), lambda j: (0, nb + j))]
    return pl.pallas_call(
        body, name="conv_geglu_fwd", grid=(nb,),
        in_specs=two(s) + two(k_taps) + two(1),
        out_specs=pl.BlockSpec((s, cb), lambda j: (0, j)),
        out_shape=jax.ShapeDtypeStruct((s, f), _MXU_DT),
        scratch_shapes=[pltpu.VMEM((s + pad, 2 * cb), F32), pltpu.VMEM((k_taps, 2 * cb), F32),
                        pltpu.VMEM((1, 2 * cb), F32)],
        compiler_params=_cparams(("parallel",)),
    )(up, up, w, w, b, b)


def _pair(dst_ref, first, second, cb):
    dst_ref[:, 0:cb] = first
    dst_ref[:, cb:2 * cb] = second


def _conv_geglu_bwd(up, w, b, dact):
    s, f2 = up.shape
    f = f2 // 2
    k_taps = w.shape[0]
    cb, pad = _FFN_CB, 8
    off = pad - (k_taps - 1)
    nb = f // cb

    def body(ug_ref, uv_ref, wg_ref, wv_ref, bg_ref, bv_ref, da_ref, du_ref, dw_ref, db_ref, z_ref, dz_ref, w_ref,
             b_ref, dw_sc, db_sc):
        _pair(w_ref, wg_ref[...], wv_ref[...], cb)
        _pair(b_ref, bg_ref[...], bv_ref[...], cb)
        z_ref[0:pad, :] = jnp.zeros((pad, 2 * cb), F32)
        z_ref[pad:pad + s, 0:cb] = ug_ref[...]
        z_ref[pad:pad + s, cb:2 * cb] = uv_ref[...]
        dz_ref[s:s + pad, :] = jnp.zeros((pad, 2 * cb), F32)
        tr = _conv_tr(2 * cb)
        for r in range(s // tr):
            rows = slice(r * tr, (r + 1) * tr)
            z = _conv_tile(z_ref, w_ref, b_ref[...], k_taps, off, r * tr)
            _, vjp = jax.vjp(_f_geglu, z[:, 0:cb], z[:, cb:2 * cb])
            dzg, dzv = vjp(da_ref[rows, :].astype(F32))
            dz_ref[rows, 0:cb] = dzg
            dz_ref[rows, cb:2 * cb] = dzv
        for r in range(s // tr):
            rows = slice(r * tr, (r + 1) * tr)
            du = _conv_bwd_input_tile(dz_ref, w_ref, k_taps, r * tr).astype(du_ref.dtype)
            du_ref[0, rows, :] = du[:, 0:cb]
            du_ref[1, rows, :] = du[:, cb:2 * cb]
        _conv_bwd_weight(dz_ref, z_ref, dw_sc, db_sc, k_taps, off, s)
        for half in range(2):
            dw_ref[half] = dw_sc[:, half * cb:(half + 1) * cb]
            db_ref[half] = db_sc[:, half * cb:(half + 1) * cb]

    two = lambda rows_: [pl.BlockSpec((rows_, cb), lambda j: (0, j)), pl.BlockSpec((rows_, cb), lambda j: (0, nb + j))]
    both = lambda rows_: pl.BlockSpec((2, rows_, cb), lambda j: (0, 0, j))
    return pl.pallas_call(
        body, name="conv_geglu_bwd", grid=(nb,),
        in_specs=two(s) + two(k_taps) + two(1) + [pl.BlockSpec((s, cb), lambda j: (0, j))],
        out_specs=[both(s), both(k_taps), both(1)],
        out_shape=[jax.ShapeDtypeStruct((2, s, f), _MXU_DT), jax.ShapeDtypeStruct((2, k_taps, f), F32),
                   jax.ShapeDtypeStruct((2, 1, f), F32)],
        scratch_shapes=[pltpu.VMEM((s + pad, 2 * cb), F32), pltpu.VMEM((s + pad, 2 * cb), F32),
                        pltpu.VMEM((k_taps, 2 * cb), F32), pltpu.VMEM((1, 2 * cb), F32),
                        pltpu.VMEM((k_taps, 2 * cb), F32), pltpu.VMEM((1, 2 * cb), F32)],
        compiler_params=_cparams(("parallel",)),
    )(up, up, w, w, b, b, dact)


def _tril_mask():
    t = lax.broadcasted_iota(jnp.int32, (SG_CHUNK, SG_CHUNK), 0)
    s = lax.broadcasted_iota(jnp.int32, (SG_CHUNK, SG_CHUNK), 1)
    return t >= s


def _sg_mix_fwd(u, vn, w, bcol):
    s, c = u.shape
    gw = c // SG_GROUPS

    def body(u_ref, v_ref, w_ref, b_ref, o_ref):
        wm = jnp.where(_tril_mask(), w_ref[0], 0.0).astype(_MXU_DT)
        for n in range(s // SG_CHUNK):
            rows = slice(n * SG_CHUNK, (n + 1) * SG_CHUNK)
            mixed = jnp.dot(wm, v_ref[rows, :], preferred_element_type=F32) + b_ref[0]
            o_ref[rows, :] = (u_ref[rows, :] * mixed).astype(o_ref.dtype)

    return pl.pallas_call(
        body, name="sg_mix_fwd", grid=(SG_GROUPS,),
        in_specs=[pl.BlockSpec((s, gw), lambda g: (0, g)), pl.BlockSpec((s, gw), lambda g: (0, g)),
                  pl.BlockSpec((1, SG_CHUNK, SG_CHUNK), lambda g: (g, 0, 0)),
                  pl.BlockSpec((1, SG_CHUNK, 1), lambda g: (g, 0, 0))],
        out_specs=pl.BlockSpec((s, gw), lambda g: (0, g)),
        out_shape=jax.ShapeDtypeStruct((s, c), _MXU_DT),
        compiler_params=_cparams(("parallel",)),
    )(u, vn, w, bcol)


def _sg_mix_bwd(u, vn, w, bcol, dub):
    s, c = u.shape
    gw = c // SG_GROUPS

    def body(u_ref, v_ref, w_ref, b_ref, d_ref, du_ref, dv_ref, dw_ref, db_ref):
        mask = _tril_mask()
        wm = jnp.where(mask, w_ref[0], 0.0).astype(_MXU_DT)
        dw = jnp.zeros((SG_CHUNK, SG_CHUNK), F32)
        db = jnp.zeros((SG_CHUNK, 1), F32)
        for n in range(s // SG_CHUNK):
            rows = slice(n * SG_CHUNK, (n + 1) * SG_CHUNK)
            v = v_ref[rows, :]
            d = d_ref[rows, :].astype(F32)
            mixed = jnp.dot(wm, v, preferred_element_type=F32) + b_ref[0]
            du_ref[rows, :] = d * mixed
            dmix = d * u_ref[rows, :]
            dmix_lo = dmix.astype(_MXU_DT)
            dv_ref[rows, :] = lax.dot_general(wm, dmix_lo, (((0,), (0,)), ((), ())), preferred_element_type=F32)
            dw = dw + lax.dot_general(dmix_lo, v, (((1,), (1,)), ((), ())), preferred_element_type=F32)
            db = db + jnp.sum(dmix, axis=1, keepdims=True)
        dw_ref[0] = jnp.where(mask, dw, 0.0)
        db_ref[0] = db

    return pl.pallas_call(
        body, name="sg_mix_bwd", grid=(SG_GROUPS,),
        in_specs=[pl.BlockSpec((s, gw), lambda g: (0, g)), pl.BlockSpec((s, gw), lambda g: (0, g)),
                  pl.BlockSpec((1, SG_CHUNK, SG_CHUNK), lambda g: (g, 0, 0)),
                  pl.BlockSpec((1, SG_CHUNK, 1), lambda g: (g, 0, 0)), pl.BlockSpec((s, gw), lambda g: (0, g))],
        out_specs=[pl.BlockSpec((s, gw), lambda g: (0, g)), pl.BlockSpec((s, gw), lambda g: (0, g)),
                   pl.BlockSpec((1, SG_CHUNK, SG_CHUNK), lambda g: (g, 0, 0)),
                   pl.BlockSpec((1, SG_CHUNK, 1), lambda g: (g, 0, 0))],
        out_shape=[jax.ShapeDtypeStruct((s, c), F32), jax.ShapeDtypeStruct((s, c), F32),
                   jax.ShapeDtypeStruct((SG_GROUPS, SG_CHUNK, SG_CHUNK), F32),
                   jax.ShapeDtypeStruct((SG_GROUPS, SG_CHUNK, 1), F32)],
        compiler_params=_cparams(("parallel",)),
    )(u, vn, w, bcol, dub)


def _rope_fwd(q2, kv2, p, krm_idx, krs_idx, tc, ts):
    s = q2.shape[0]
    hw = N_HEADS * HEAD_PAD
    tm = 256

    def body(qm_ref, qs_ref, kn_ref, v_ref, krm_ref, krs_ref, tc_ref, ts_ref, q_ref, k_ref, vo_ref):
        tcv, tsv = tc_ref[...], ts_ref[...]
        kpe = krm_ref[...] * tcv + krs_ref[...] * tsv
        for h in range(N_HEADS):
            cols = slice(h * HEAD_PAD, (h + 1) * HEAD_PAD)
            q_ref[:, cols] = (qm_ref[:, cols] * tcv + qs_ref[:, cols] * tsv).astype(q_ref.dtype)
            k_ref[:, cols] = (kn_ref[:, cols] + kpe).astype(k_ref.dtype)
        vo_ref[...] = v_ref[...].astype(vo_ref.dtype)

    return pl.pallas_call(
        body, name="rope_fwd", grid=(s // tm,),
        in_specs=[_row_spec(tm, hw, 0), _row_spec(tm, hw, 1), _row_spec(tm, hw, 0), _row_spec(tm, hw, 1),
                  _row_spec(tm, HEAD_PAD, krm_idx), _row_spec(tm, HEAD_PAD, krs_idx),
                  _row_spec(tm, HEAD_PAD, 0), _row_spec(tm, HEAD_PAD, 0)],
        out_specs=[_row_spec(tm, hw, 0)] * 3,
        out_shape=[jax.ShapeDtypeStruct((s, hw), _MXU_DT)] * 3,
        compiler_params=_cparams(("parallel",)),
    )(q2, q2, kv2, kv2, p, p, tc, ts)


def _rope_bwd(dq, dk, dv, tc, ts, dp, kr_blk):
    s = dq.shape[0]
    hw = N_HEADS * HEAD_PAD
    tm = 256

    def body(dq_ref, dk_ref, dv_ref, tc_ref, ts_ref, dp_in, dq2_ref, dkv2_ref, dkr_ref):
        tcv, tsv = tc_ref[...], ts_ref[...]
        dkpe = jnp.zeros((tm, HEAD_PAD), F32)
        for h in range(N_HEADS):
            cols = slice(h * HEAD_PAD, (h + 1) * HEAD_PAD)
            dqh = dq_ref[:, cols]
            dq2_ref[:, cols] = (dqh * tcv).astype(dq2_ref.dtype)
            dq2_ref[:, hw + h * HEAD_PAD:hw + (h + 1) * HEAD_PAD] = (dqh * tsv).astype(dq2_ref.dtype)
            dkpe = dkpe + dk_ref[:, cols]
        dkv2_ref[:, 0:hw] = dk_ref[...].astype(dkv2_ref.dtype)
        dkv2_ref[:, hw:2 * hw] = dv_ref[...].astype(dkv2_ref.dtype)
        dkr_ref[:, 0:HEAD_PAD] = (dkpe * tcv).astype(dkr_ref.dtype)
        dkr_ref[:, HEAD_PAD:2 * HEAD_PAD] = (dkpe * tsv).astype(dkr_ref.dtype)
        dkr_ref[:, 2 * HEAD_PAD:3 * HEAD_PAD] = jnp.zeros((tm, HEAD_PAD), dkr_ref.dtype)

    return pl.pallas_call(
        body, name="rope_bwd", grid=(s // tm,),
        in_specs=[_row_spec(tm, hw, 0)] * 3 + [_row_spec(tm, HEAD_PAD, 0)] * 2 + [_ANY],
        out_specs=[_row_spec(tm, 2 * hw, 0), _row_spec(tm, 2 * hw, 0), _row_spec(tm, 3 * HEAD_PAD, kr_blk)],
        out_shape=[jax.ShapeDtypeStruct((s, 2 * hw), _MXU_DT), jax.ShapeDtypeStruct((s, 2 * hw), _MXU_DT),
                   jax.ShapeDtypeStruct(dp.shape, dp.dtype)],
        input_output_aliases={5: 2},
        compiler_params=_cparams(("parallel",)),
    )(dq, dk, dv, tc, ts, dp)


_ATTN_TQ = 256
_ATTN_SCALE = (QK_NOPE + QK_ROPE) ** -0.5


def _attn_probs(q, k, i):
    s = k.shape[0]
    sc = lax.dot_general(q, k, (((1,), (1,)), ((), ())), preferred_element_type=F32) * _ATTN_SCALE
    row = i * _ATTN_TQ + lax.broadcasted_iota(jnp.int32, (_ATTN_TQ, s), 0)
    col = lax.broadcasted_iota(jnp.int32, (_ATTN_TQ, s), 1)
    sc = jnp.where(row >= col, sc, jnp.finfo(F32).min)
    e = jnp.exp(sc - jnp.max(sc, axis=1, keepdims=True))
    return e * (1.0 / jnp.sum(e, axis=1, keepdims=True))


def _per_query_block(s, fn):
    i = pl.program_id(1)
    for n in range(s // _ATTN_TQ):
        @pl.when(i == n)
        def _(n=n):
            fn(n, (n + 1) * _ATTN_TQ)


def _attn_fwd(q, k, v):
    s = q.shape[0]

    def body(q_ref, k_ref, v_ref, o_ref):
        def block(n, kl):
            p = _attn_probs(q_ref[...], k_ref[0:kl, :], n)
            o_ref[...] = jnp.dot(p.astype(_MXU_DT), v_ref[0:kl, :], preferred_element_type=F32).astype(o_ref.dtype)

        _per_query_block(s, block)

    qspec = pl.BlockSpec((_ATTN_TQ, HEAD_PAD), lambda h, i: (i, h))
    kspec = pl.BlockSpec((s, HEAD_PAD), lambda h, i: (0, h))
    return pl.pallas_call(
        body, name="attn_fwd", grid=(N_HEADS, s // _ATTN_TQ),
        in_specs=[qspec, kspec, kspec], out_specs=qspec,
        out_shape=jax.ShapeDtypeStruct(q.shape, _MXU_DT),
        compiler_params=_cparams(("parallel", "parallel")),
    )(q, k, v)


def _attn_bwd(q, k, v, do):
    s = q.shape[0]

    def body(q_ref, k_ref, v_ref, do_ref, dq_ref, dk_ref, dv_ref):
        i = pl.program_id(1)

        @pl.when(i == 0)
        def _():
            dk_ref[...] = jnp.zeros_like(dk_ref)
            dv_ref[...] = jnp.zeros_like(dv_ref)

        def block(n, kl):
            qv, kv, dov = q_ref[...], k_ref[0:kl, :], do_ref[...]
            p = _attn_probs(qv, kv, n)
            dp = lax.dot_general(dov, v_ref[0:kl, :], (((1,), (1,)), ((), ())), preferred_element_type=F32)
            delta = jnp.sum(p * dp, axis=1, keepdims=True)
            ds = (p * (dp - delta) * _ATTN_SCALE).astype(_MXU_DT)
            dq_ref[...] = jnp.dot(ds, kv, preferred_element_type=F32)
            dk_ref[0:kl, :] += lax.dot_general(ds, qv, (((0,), (0,)), ((), ())), preferred_element_type=F32)
            dv_ref[0:kl, :] += lax.dot_general(p.astype(_MXU_DT), dov, (((0,), (0,)), ((), ())),
                                               preferred_element_type=F32)

        _per_query_block(s, block)

    qspec = pl.BlockSpec((_ATTN_TQ, HEAD_PAD), lambda h, i: (i, h))
    kspec = pl.BlockSpec((s, HEAD_PAD), lambda h, i: (0, h))
    return pl.pallas_call(
        body, name="attn_bwd", grid=(N_HEADS, s // _ATTN_TQ),
        in_specs=[qspec, kspec, kspec, qspec], out_specs=[qspec, kspec, kspec],
        out_shape=[jax.ShapeDtypeStruct(q.shape, F32)] * 3,
        compiler_params=_cparams(("parallel", "arbitrary")),
    )(q, k, v, do)


def _loss_head(y, target):
    s, d = y.shape
    tm = 256

    def body(y_ref, t_ref, loss_ref, dy_ref):
        @pl.when(pl.program_id(0) == 0)
        def _():
            loss_ref[...] = jnp.zeros_like(loss_ref)

        err = y_ref[...] - t_ref[...]
        loss_ref[...] += 0.5 * jnp.sum(jnp.mean(err * err, axis=-1, keepdims=True), axis=0, keepdims=True)
        dy_ref[...] = err * (1.0 / d)

    return pl.pallas_call(
        body, name="loss_head", grid=(s // tm,),
        in_specs=[_row_spec(tm, d, 0), _row_spec(tm, d, 0)],
        out_specs=[_full_spec((1, 1)), _row_spec(tm, d, 0)],
        out_shape=[jax.ShapeDtypeStruct((1, 1), F32), jax.ShapeDtypeStruct((s, d), F32)],
        compiler_params=_cparams(("arbitrary",)),
    )(y, target)


def _adamw_math(w, g, m, v):
    mn = ADAM_B1 * m + (1.0 - ADAM_B1) * g
    vn = ADAM_B2 * v + (1.0 - ADAM_B2) * (g * g)
    m_hat = mn / (1.0 - ADAM_B1 ** ADAM_STEP)
    v_hat = vn / (1.0 - ADAM_B2 ** ADAM_STEP)
    return -ADAM_LR * (m_hat / (jnp.sqrt(v_hat) + ADAM_EPS) + ADAM_WD * w), mn, vn


def _adamw(name, w, g, m, v, layer, into):
    _, k, n = w.shape
    tk, tn = _slab_block(k, n)

    def body(w_ref, g_ref, m_ref, v_ref, *rest):
        d_ref, mo_ref, vo_ref = rest[-3:]
        d_ref[...], mo_ref[...], vo_ref[...] = _adamw_math(w_ref[...], g_ref[...], m_ref[...], v_ref[...])

    spec = pl.BlockSpec((1, tk, tn), lambda j, jn: (layer, j, jn))
    extra = [] if into is None else list(into)
    return pl.pallas_call(
        body, name=name, grid=(k // tk, n // tn), in_specs=[spec] * 4 + [_ANY] * len(extra), out_specs=[spec] * 3,
        out_shape=[jax.ShapeDtypeStruct(w.shape, F32)] * 3,
        input_output_aliases={4 + i: i for i in range(len(extra))},
        compiler_params=_cparams(("parallel", "parallel")),
    )(w, g, m, v, *extra)


_ANY = pl.BlockSpec(memory_space=pl.ANY)


def _mesh_pos():
    return lax.axis_index("x"), lax.axis_index("y"), lax.axis_index("c")


def _other_chips(x, y):
    return [(1 - x, y), (x, 1 - y), (1 - x, 1 - y)]


def _remote(src, dst, send_sem, recv_sem, to):
    return pltpu.make_async_remote_copy(src_ref=src, dst_ref=dst, send_sem=send_sem, recv_sem=recv_sem,
                                        device_id=to, device_id_type=_MESH)


_HBM = pl.BlockSpec(memory_space=pltpu.HBM)
_SEM = pl.BlockSpec(memory_space=pltpu.SEMAPHORE)
_EFFECT = pltpu.SideEffectType.DATAFLOW_SIDE_EFFECTING


def _in_hbm(a):
    return pltpu.with_memory_space_constraint(a, pltpu.HBM)


def _chip_index(chip):
    return 2 * chip[0] + chip[1]


def _ag_forward(name, shards, lands, layer, have_remote):
    n = len(shards)

    def body(*refs):
        ins = refs[:n]
        outs = refs[2 * n:3 * n] if lands is not None else refs[n:2 * n]
        send_sems, recv_sems = refs[-2:]
        x, y, c = _mesh_pos()
        sibling = (x, y, 1 - c)
        chips = _other_chips(x, y)

        def copy(a, k, src, dst, to):
            return _remote(src, dst, send_sems.at[a, k], recv_sems.at[a, k], to)

        own = [copy(a, 6, ins[a], outs[a].at[2 * x + y], sibling) for a in range(n)]
        for cp in own:
            cp.start()

        @pl.when(c == layer)
        def _():
            started = []
            if not have_remote:
                for k, chip in enumerate(chips):
                    for a in range(n):
                        cp = copy(a, k, ins[a], outs[a].at[2 * x + y], (*chip, c))
                        cp.start()
                        started.append(cp)
            for k, chip in enumerate(chips):
                for a in range(n):
                    landed = outs[a].at[_chip_index(chip)]
                    if not have_remote:
                        copy(a, k, ins[a], landed, (*chip, c)).wait_recv()
                    cp = copy(a, 3 + k, landed, landed, sibling)
                    cp.start()
                    started.append(cp)
            for cp in started:
                cp.wait_send()

        @pl.when(c != layer)
        def _():
            for k, chip in enumerate(chips):
                for a in range(n):
                    copy(a, 3 + k, ins[a], outs[a].at[_chip_index(chip)], sibling).wait_recv()

        for cp in own:
            cp.wait()

    out_shape = [jax.ShapeDtypeStruct((4,) + a.shape, a.dtype) for a in shards]
    extra = [] if lands is None else list(lands)
    return pl.pallas_call(
        body, name=name, in_specs=[_ANY] * (n + len(extra)), out_specs=[_ANY] * n,
        out_shape=out_shape, input_output_aliases={n + a: a for a in range(len(extra))},
        scratch_shapes=[pltpu.SemaphoreType.DMA((n, 7)), pltpu.SemaphoreType.DMA((n, 7))],
    )(*shards, *extra)


def _owner_sends(owners, srcs, dsts, send_sems, recv_sems, do):
    x, y, c = _mesh_pos()
    for core in (0, 1):
        mine = [a for a in range(len(srcs)) if owners[a] == core]
        if mine:
            @pl.when(c == core)
            def _(mine=mine):
                for k, chip in enumerate(_other_chips(x, y)):
                    for a in mine:
                        do(_remote(srcs[a](chip, k), dsts[a](chip, k), send_sems.at[3 * a + k],
                                   recv_sems.at[3 * a + k], (*chip, c)))


def _split_start(name, owners, sources, land_shapes, src_of, dst_of, after):
    n = len(sources)

    def body(*refs):
        srcs, lands = refs[:n], refs[n:2 * n]
        send_sems, recv_sems = refs[2 * n + 1], refs[2 * n + 2]
        token = refs[-1]
        _owner_sends(owners, [functools.partial(src_of, srcs[a]) for a in range(n)],
                     [functools.partial(dst_of, lands[a]) for a in range(n)], send_sems, recv_sems,
                     lambda cp: cp.start())
        token[...] = jnp.zeros_like(token)

    lands = [_in_hbm(lax.empty(s.shape, s.dtype)) for s in land_shapes]
    outs = pl.pallas_call(
        body, name=name,
        out_shape=([pltpu.SemaphoreType.DMA((3 * n,)), pltpu.SemaphoreType.DMA((3 * n,))]
                   + [pltpu.HBM(a.shape, a.dtype) for a in sources] + [pltpu.HBM(s.shape, s.dtype) for s in land_shapes]
                   + [jax.ShapeDtypeStruct((8, _LANES), F32)]),
        in_specs=[_HBM] * (2 * n) + [_ANY],
        out_specs=[_SEM, _SEM] + [_HBM] * (2 * n) + [pl.BlockSpec(memory_space=pltpu.VMEM)],
        input_output_aliases={i: 2 + i for i in range(2 * n)},
        compiler_params=pltpu.CompilerParams(has_side_effects=_EFFECT),
    )(*[_in_hbm(a) for a in sources], *lands, after)
    return outs[0], outs[1], outs[2:2 + n], outs[2 + n:2 + 2 * n], outs[-1]


def _split_wait(name, owners, send_sems, recv_sems, sources, lands, after, src_of, dst_of):
    n = len(sources)

    def body(*refs):
        srcs, lnds = refs[:n], refs[n:2 * n]
        s_sems, r_sems = refs[2 * n], refs[2 * n + 1]

        def wait(cp):
            cp.wait_send()
            cp.wait_recv()

        _owner_sends(owners, [functools.partial(src_of, srcs[a]) for a in range(n)],
                     [functools.partial(dst_of, lnds[a]) for a in range(n)], s_sems, r_sems, wait)

    outs = pl.pallas_call(
        body, name=name,
        out_shape=[pltpu.HBM(a.shape, a.dtype) for a in sources] + [pltpu.HBM(a.shape, a.dtype) for a in lands],
        in_specs=[_HBM] * (2 * n) + [_SEM, _SEM] + [_ANY] * len(after), out_specs=[_HBM] * (2 * n),
        input_output_aliases={i: i for i in range(2 * n)},
        compiler_params=pltpu.CompilerParams(has_side_effects=_EFFECT),
    )(*sources, *lands, send_sems, recv_sems, *after)
    return outs[:n], outs[n:]


def _pair_exchange(name, arrays, owners, to_owner, layer=None):
    n = len(arrays)

    def body(*refs):
        ins, outs, (send_sems, recv_sems) = refs[:n], refs[n:2 * n], refs[2 * n:]
        x, y, c = _mesh_pos()
        part = (lambda r: r) if layer is None else (lambda r: r.at[layer])
        copies = [_remote(part(ins[a]), part(outs[a]), send_sems.at[a], recv_sems.at[a], (x, y, 1 - c))
                  for a in range(n)]
        for core in (0, 1):
            sends = [copies[a] for a in range(n) if (owners[a] != core) == to_owner]
            recvs = [copies[a] for a in range(n) if (owners[a] == core) == to_owner]

            @pl.when(c == core)
            def _(sends=sends, recvs=recvs):
                for cp in sends:
                    cp.start()
                for cp in recvs:
                    cp.wait_recv()
                for cp in sends:
                    cp.wait_send()

    return pl.pallas_call(
        body, name=name, in_specs=[_ANY] * n, out_specs=[_ANY] * n,
        out_shape=[jax.ShapeDtypeStruct(g.shape, g.dtype) for g in arrays],
        input_output_aliases={} if to_owner else {a: a for a in range(n)},
        scratch_shapes=[pltpu.SemaphoreType.DMA((n,)), pltpu.SemaphoreType.DMA((n,))],
    )(*arrays)


def _rs_scatter(name, ts, owners):
    n = len(ts)

    def body(*refs):
        ins, outs, (send_sems, recv_sems) = refs[:n], refs[n:2 * n], refs[2 * n:]
        srcs = [lambda chip, k, r=ins[a]: r.at[_chip_index(chip)] for a in range(n)]
        dsts = [lambda chip, k, r=outs[a]: r.at[k] for a in range(n)]
        _owner_sends(owners, srcs, dsts, send_sems, recv_sems, lambda cp: cp.start())
        _owner_sends(owners, srcs, dsts, send_sems, recv_sems, lambda cp: cp.wait())

    return pl.pallas_call(
        body, name=name, in_specs=[_ANY] * n, out_specs=[_ANY] * n,
        out_shape=[jax.ShapeDtypeStruct((3,) + t.shape[1:], t.dtype) for t in ts],
        scratch_shapes=[pltpu.SemaphoreType.DMA((3 * n,)), pltpu.SemaphoreType.DMA((3 * n,))],
    )(*ts)


def _add_pair(name, g, a, flags):
    _, k, n = g.shape
    tk, tn = _slab_block(k, n)

    def body(flags_ref, g_ref, a_ref, o_ref):
        o_ref[...] = (g_ref[...] + a_ref[...]).astype(o_ref.dtype)

    spec = pl.BlockSpec((1, tk, tn), lambda j, i, jn, fl: (j * fl[1], i * fl[1], jn * fl[1]))
    return pl.pallas_call(
        body, name=name,
        grid_spec=pltpu.PrefetchScalarGridSpec(num_scalar_prefetch=1, grid=(4, k // tk, n // tn),
                                               in_specs=[spec, spec], out_specs=spec),
        out_shape=jax.ShapeDtypeStruct(g.shape, _RS_DT),
        compiler_params=_cparams(("arbitrary", "arbitrary", "arbitrary")),
    )(flags, g, a)


def _add_quads(name, t, b, layer, flags, into, after=None):
    _, k, n = t.shape
    tk, tn = _slab_block(k, n)

    def body(flags_ref, t_ref, b_ref, *rest):
        o_ref = rest[-1]
        f = lambda v: v.astype(F32)
        o_ref[0] = ((f(t_ref[0]) + f(b_ref[0])) + f(b_ref[1])) + f(b_ref[2])

    extra = ([] if into is None else [into]) + ([] if after is None else [after])
    return pl.pallas_call(
        body, name=name,
        grid_spec=pltpu.PrefetchScalarGridSpec(
            num_scalar_prefetch=1, grid=(k // tk, n // tn),
            in_specs=[pl.BlockSpec((1, tk, tn), lambda i, jn, fl: (fl[0], i * fl[1], jn * fl[1])),
                      pl.BlockSpec((3, tk, tn), lambda i, jn, fl: (0, i * fl[1], jn * fl[1]))] + [_ANY] * len(extra),
            out_specs=pl.BlockSpec((1, tk, tn), lambda i, jn, fl: (layer, i * fl[1], jn * fl[1]))),
        out_shape=jax.ShapeDtypeStruct((2, k, n), F32),
        input_output_aliases={} if into is None else {3: 0},
        compiler_params=_cparams(("arbitrary", "arbitrary")),
    )(flags, t, b, *extra)


def _slab_block(k, n, itemsize=4):
    tk = (1 << 20) // (n * itemsize) // 16 * 16
    while 0 < tk < k and k % tk:
        tk -= 16
    if 0 < tk < k:
        return tk, n
    if k * n * itemsize <= (2 << 20) or n % _LANES:
        return k, n
    tn = max(_LANES, (1 << 20) // (k * itemsize) // _LANES * _LANES)
    while n % tn:
        tn -= _LANES
    return k, tn


def _all_reduce_adamw(gs, ws, ms, vs):
    n = len(gs)

    def body(*refs):
        g_refs, w_refs, m_refs, v_refs = (refs[i * n:(i + 1) * n] for i in range(4))
        gsum, delta, m_out, v_out = (refs[(4 + i) * n:(5 + i) * n] for i in range(4))
        slots = refs[8 * n:9 * n]
        send_sems, recv_sems = refs[9 * n:]
        x, y, c = _mesh_pos()
        me = 4 * x + 2 * y + c
        copies = []
        for rel in range(1, 8):
            bx, by, bc = (rel >> 2) & 1, (rel >> 1) & 1, rel & 1
            peer = (1 - x if bx else x, 1 - y if by else y, 1 - c if bc else c)
            for a in range(n):
                cp = _remote(g_refs[a], slots[a].at[me], send_sems.at[a, rel - 1], recv_sems.at[a, rel - 1], peer)
                cp.start()
                copies.append(cp)
        for a in range(n):
            slots[a][me] = g_refs[a][...]
        for cp in copies:
            cp.wait()
        for a in range(n):
            acc = slots[a][0]
            for d in range(1, 8):
                acc = acc + slots[a][d]
            gsum[a][...] = acc
            delta[a][...], m_out[a][...], v_out[a][...] = _adamw_math(w_refs[a][...], acc, m_refs[a][...],
                                                                      v_refs[a][...])

    vmem = pl.BlockSpec(memory_space=pltpu.VMEM)
    outs = pl.pallas_call(
        body, name="all_reduce_adamw", in_specs=[vmem] * (4 * n), out_specs=[vmem] * (4 * n),
        out_shape=[jax.ShapeDtypeStruct(g.shape, F32) for g in gs] * 4,
        scratch_shapes=([pltpu.VMEM((8,) + g.shape, F32) for g in gs]
                        + [pltpu.SemaphoreType.DMA((n, 7)), pltpu.SemaphoreType.DMA((n, 7))]),
        compiler_params=pltpu.CompilerParams(vmem_limit_bytes=_VMEM_LIMIT),
    )(*gs, *ws, *ms, *vs)
    return outs[:n], outs[n:2 * n], outs[2 * n:3 * n], outs[3 * n:]


def _swap_rope(a):
    h = QK_ROPE // 2
    return jnp.concatenate([a[..., h:], a[..., :h]], axis=-1)


def _swap_rope_rows(a):
    h = QK_ROPE // 2
    return jnp.concatenate([a[h:], a[:h]], axis=0)


_FFN_CB = 256


def _interleave_rows(a, cb):
    r, c = a.shape
    return a.reshape(2, r // (2 * cb), cb, c).transpose(1, 0, 2, 3).reshape(r, c)


def _deinterleave_rows(a, cb):
    r, c = a.shape
    return a.reshape(r // (2 * cb), 2, cb, c).transpose(1, 0, 2, 3).reshape(r, c)


class _InLayout:
    def __init__(self, d):
        self.d = d
        self.gates = 0
        self.a = 3 * d
        self.b = 4 * d
        self.kv = 5 * d
        self.q = self.kv + 256
        self.krm = self.q + 384
        self.krs = self.krm + HEAD_PAD
        self.width = self.krs + 2 * HEAD_PAD


def _prep_layer(wl, d):
    lay = _InLayout(d)
    w_in = wl["w_in"]
    dt = w_in.dtype
    a, b = w_in[0:d], w_in[d:2 * d]
    q, kv = w_in[2 * d:2 * d + 384], w_in[2 * d + 384:2 * d + 640]
    kr = w_in[2 * d + 640:2 * d + 640 + QK_ROPE]
    gates = w_in[2 * d + 640 + QK_ROPE:]
    z = lambda n: jnp.zeros((n, d), dt)
    krm = jnp.concatenate([z(QK_NOPE), kr, z(HEAD_PAD - QK_NOPE - QK_ROPE)], axis=0)
    krs = jnp.concatenate([z(QK_NOPE), _swap_rope_rows(kr), z(HEAD_PAD - QK_NOPE - QK_ROPE)], axis=0)
    out = dict(wl)
    out["w_in"] = jnp.concatenate([gates, _interleave_rows(a, _LANES), b, kv, q, krm, krs,
                                   z(lay.width - lay.krs - HEAD_PAD)], axis=0)
    uq = wl["mla_w_uq"].reshape(-1, N_HEADS, QK_NOPE + QK_ROPE)
    nq = uq.shape[0]
    nope, pe = uq[..., :QK_NOPE], uq[..., QK_NOPE:]
    zq = lambda n: jnp.zeros((nq, N_HEADS, n), dt)
    main = jnp.concatenate([nope, pe, zq(HEAD_PAD - QK_NOPE - QK_ROPE)], axis=-1).reshape(nq, -1)
    swapped = jnp.concatenate([zq(QK_NOPE), _swap_rope(pe), zq(HEAD_PAD - QK_NOPE - QK_ROPE)], axis=-1).reshape(nq, -1)
    out["mla_w_uq"] = jnp.concatenate([main, swapped], axis=1)
    ukv = wl["mla_w_ukv"].reshape(-1, N_HEADS, QK_NOPE + V_HEAD)
    nkv = ukv.shape[0]
    zk = jnp.zeros((nkv, N_HEADS, HEAD_PAD - QK_NOPE), dt)
    zv = jnp.zeros((nkv, N_HEADS, HEAD_PAD - V_HEAD), dt)
    out["mla_w_ukv"] = jnp.concatenate([jnp.concatenate([ukv[..., :QK_NOPE], zk], axis=-1).reshape(nkv, -1),
                                        jnp.concatenate([ukv[..., QK_NOPE:], zv], axis=-1).reshape(nkv, -1)], axis=1)
    wo = wl["mla_w_o"].reshape(N_HEADS, V_HEAD, -1)
    out["mla_w_o"] = jnp.concatenate([wo, jnp.zeros((N_HEADS, HEAD_PAD - V_HEAD, wo.shape[-1]), dt)],
                                     axis=1).reshape(N_HEADS * HEAD_PAD, -1)
    return out


def _unprep_grads(g, d):
    lay = _InLayout(d)
    gi = g["w_in"]
    kr = (gi[lay.krm + QK_NOPE:lay.krm + QK_NOPE + QK_ROPE]
          + _swap_rope_rows(gi[lay.krs + QK_NOPE:lay.krs + QK_NOPE + QK_ROPE]))
    out = dict(g)
    out["w_in"] = jnp.concatenate([_deinterleave_rows(gi[lay.a:lay.a + d], _LANES), gi[lay.b:lay.b + d],
                                   gi[lay.q:lay.q + 384], gi[lay.kv:lay.kv + 256], kr,
                                   gi[lay.gates:lay.gates + 3 * d]], axis=0)
    hw = N_HEADS * HEAD_PAD
    gq = g["mla_w_uq"]
    nq = gq.shape[0]
    main = gq[:, :hw].reshape(nq, N_HEADS, HEAD_PAD)
    swapped = gq[:, hw:].reshape(nq, N_HEADS, HEAD_PAD)
    pe = main[..., QK_NOPE:QK_NOPE + QK_ROPE] + _swap_rope(swapped[..., QK_NOPE:QK_NOPE + QK_ROPE])
    out["mla_w_uq"] = jnp.concatenate([main[..., :QK_NOPE], pe], axis=-1).reshape(nq, -1)
    gkv = g["mla_w_ukv"]
    nkv = gkv.shape[0]
    out["mla_w_ukv"] = jnp.concatenate([gkv[:, :hw].reshape(nkv, N_HEADS, HEAD_PAD)[..., :QK_NOPE],
                                        gkv[:, hw:].reshape(nkv, N_HEADS, HEAD_PAD)[..., :V_HEAD]],
                                       axis=-1).reshape(nkv, -1)
    go = g["mla_w_o"]
    out["mla_w_o"] = go.reshape(N_HEADS, HEAD_PAD, -1)[:, :V_HEAD].reshape(N_HEADS * V_HEAD, -1)
    return out


def _rope_tables(positions):
    s = positions.shape[0]
    inv = ROPE_THETA ** (-jnp.arange(0, QK_ROPE, 2, dtype=F32) / QK_ROPE)
    ang = positions.astype(F32)[:, None] * inv
    cos, sin = jnp.cos(ang), jnp.sin(ang)
    tail = jnp.zeros((s, HEAD_PAD - QK_NOPE - QK_ROPE), F32)
    tc = jnp.concatenate([jnp.ones((s, QK_NOPE), F32), cos, cos, tail], axis=1)
    ts = jnp.concatenate([jnp.zeros((s, QK_NOPE), F32), -sin, sin, tail], axis=1)
    return tc, ts


def _row(v):
    return v.reshape(1, -1)


def _layer_fwd(x, h, w, g_next, tc, ts, late_weights=None):
    d = x.shape[1]
    lay = _InLayout(d)
    cw = d // 2
    blk = lambda off, width: off // width
    p = _mm("mm_in", h, w["w_in"], tb=True)
    z1 = _glu_conv_fwd(p, blk(lay.a, 2 * _LANES), w["conv_dw_w"], _row(w["conv_dw_b"]))
    ln_a = [_row(w["conv_ln_g"]), _row(w["conv_ln_b"])]
    (z3,) = _row_fwd("ln_silu_fwd", _f_ln_silu, [(z1, cw, 0)], ln_a, [(cw, _MXU_DT)])
    ya = _mm("mm_conv_out", z3, w["conv_out_w"])
    ln_b = [_row(w["sg_ln_g"]), _row(w["sg_ln_b"])]
    u, vn = _row_fwd("sg_pre_fwd", _f_sg_pre, [(p, cw, blk(lay.b, cw)), (p, cw, blk(lay.b + cw, cw))], ln_b,
                     [(cw, F32), (cw, _MXU_DT)])
    bcol = w["sg_b"].reshape(SG_GROUPS, SG_CHUNK, 1)
    ub = _sg_mix_fwd(u, vn, w["sg_w"], bcol)
    yb = _mm("mm_sg_out", ub, w["sg_out_w"])
    (qn,) = _row_fwd("q_norm_fwd", _f_rms, [(p, 384, blk(lay.q, 384))], [_row(w["mla_q_norm_g"])], [(384, _MXU_DT)])
    (kvn,) = _row_fwd("kv_norm_fwd", _f_rms, [(p, 256, blk(lay.kv, 256))], [_row(w["mla_kv_norm_g"])],
                      [(256, _MXU_DT)])
    q2 = _mm("mm_uq", qn, w["mla_w_uq"])
    kv2 = _mm("mm_ukv", kvn, w["mla_w_ukv"])
    qf, kf, vf = _rope_fwd(q2, kv2, p, blk(lay.krm, HEAD_PAD), blk(lay.krs, HEAD_PAD), tc, ts)
    o = _attn_fwd(qf, kf, vf)
    yc = _mm("mm_o", o, w["mla_w_o"])
    gate_rows = [(p, d, 0), (p, d, 1), (p, d, 2)]
    (merged,) = _row_fwd("merge_fwd", _f_merge, gate_rows + [(ya, d, 0), (yb, d, 0), (yc, d, 0)], [], [(d, _MXU_DT)])
    if late_weights is not None:
        w = {**w, **late_weights(merged)}
    t = _mm("mm_out", merged, w["w_out"])
    x1, h2 = _row_fwd("resid_mix_fwd", _f_resid_rms_rms, [(x, d, 0), (t, d, 0)],
                      [_row(w["mix_post_g"]), _row(w["ffn_pre_g"])], [(d, F32), (d, _MXU_DT)])
    up = _mm("mm_up", h2, w["ffn_w_up"])
    act = _conv_geglu_fwd(up, w["ffn_dw_w"], _row(w["ffn_dw_b"]))
    dn = _mm("mm_down", act, w["ffn_w_down"])
    if g_next is None:
        (x2,) = _row_fwd("resid_ffn_last_fwd", _f_resid_rms, [(x1, d, 0), (dn, d, 0)], [_row(w["ffn_post_g"])],
                         [(d, F32)])
        h_next = None
    else:
        x2, h_next = _row_fwd("resid_ffn_fwd", _f_resid_rms_rms, [(x1, d, 0), (dn, d, 0)],
                              [_row(w["ffn_post_g"]), _row(g_next)], [(d, F32), (d, _MXU_DT)])
    saved = dict(x=x, h=h, p=p, z1=z1, z3=z3, ya=ya, u=u, vn=vn, ub=ub, yb=yb, qn=qn, kvn=kvn, qf=qf, kf=kf, vf=vf, o=o,
                 yc=yc, merged=merged, t=t, x1=x1, h2=h2, up=up, act=act, dn=dn, bcol=bcol)
    return x2, h_next, saved


def _layer_bwd(dx2, dh_next, w, g_next, sv, tc, ts):
    d = dx2.shape[1]
    lay = _InLayout(d)
    cw = d // 2
    blk = lambda off, width: off // width
    lo = _MXU_DT
    g = {}
    x1, dn = sv["x1"], sv["dn"]
    if dh_next is None:
        dx1, ddn, g["ffn_post_g"] = _row_bwd("resid_ffn_last_bwd", _f_resid_rms, [(x1, d, 0), (dn, d, 0)],
                                             [_row(w["ffn_post_g"])], [(dx2, d, 0)], [(0, F32), (1, lo)], [0])
    else:
        dx1, ddn, g["ffn_post_g"], g["next_pre_g"] = _row_bwd(
            "resid_ffn_bwd", _f_resid_rms_rms, [(x1, d, 0), (dn, d, 0)], [_row(w["ffn_post_g"]), _row(g_next)],
            [(dx2, d, 0), (dh_next, d, 0)], [(0, F32), (1, lo)], [0, 1])
    dact = _mm("mm_down_dx", ddn, w["ffn_w_down"], tb=True)
    g["ffn_w_down"] = _mm("mm_down_dw", sv["act"], ddn, ta=True)
    dup, dw_halves, db_halves = _conv_geglu_bwd(sv["up"], w["ffn_dw_w"], _row(w["ffn_dw_b"]), dact)
    g["ffn_dw_w"] = jnp.concatenate([dw_halves[0], dw_halves[1]], axis=1)
    g["ffn_dw_b"] = jnp.concatenate([db_halves[0], db_halves[1]], axis=1)
    dh2 = _mm("mm_up_dx", dup, w["ffn_w_up"], tb=True, a_halves=True)
    g["ffn_w_up"] = _mm("mm_up_dw", sv["h2"], dup, ta=True, b_halves=True, out_quarters=True)
    dx, dt, g["mix_post_g"], g["ffn_pre_g"] = _row_bwd(
        "resid_mix_bwd", _f_resid_rms_rms, [(sv["x"], d, 0), (sv["t"], d, 0)],
        [_row(w["mix_post_g"]), _row(w["ffn_pre_g"])], [(dx1, d, 0), (dh2, d, 0)], [(0, F32), (1, lo)], [0, 1])
    dmerged = _mm("mm_out_dx", dt, w["w_out"], tb=True)
    g["w_out"] = _mm("mm_out_dw", sv["merged"], dt, ta=True)
    p = sv["p"]
    gate_rows = [(p, d, 0), (p, d, 1), (p, d, 2)]
    dp, dya, dyb, dyc = _row_bwd(
        "merge_bwd", _f_merge, gate_rows + [(sv["ya"], d, 0), (sv["yb"], d, 0), (sv["yc"], d, 0)], [],
        [(dmerged, d, 0)], [((0, 1, 2), lo), ((3,), lo), ((4,), lo), ((5,), lo)], [], place=(lay.width, 0))
    do = _mm("mm_o_dx", dyc, w["mla_w_o"], tb=True, out_dtype=lo)
    g["mla_w_o"] = _mm("mm_o_dw", sv["o"], dyc, ta=True)
    dqf, dkf, dvf = _attn_bwd(sv["qf"], sv["kf"], sv["vf"], do)
    dq2, dkv2, dp = _rope_bwd(dqf, dkf, dvf, tc, ts, dp, blk(lay.krm, 3 * HEAD_PAD))
    dkvn = _mm("mm_ukv_dx", dkv2, w["mla_w_ukv"], tb=True)
    g["mla_w_ukv"] = _mm("mm_ukv_dw", sv["kvn"], dkv2, ta=True)
    dqn = _mm("mm_uq_dx", dq2, w["mla_w_uq"], tb=True)
    g["mla_w_uq"] = _mm("mm_uq_dw", sv["qn"], dq2, ta=True)
    dp, g["mla_q_norm_g"] = _row_bwd("q_norm_bwd", _f_rms, [(p, 384, blk(lay.q, 384))], [_row(w["mla_q_norm_g"])],
                                     [(dqn, 384, 0)], [((0,), lo)], [0], place=(lay.width, blk(lay.q, 384)), into=dp)
    dp, g["mla_kv_norm_g"] = _row_bwd("kv_norm_bwd", _f_rms, [(p, 256, blk(lay.kv, 256))],
                                      [_row(w["mla_kv_norm_g"])], [(dkvn, 256, 0)], [((0,), lo)], [0],
                                      place=(lay.width, blk(lay.kv, 256)), into=dp)
    dub = _mm("mm_sg_out_dx", dyb, w["sg_out_w"], tb=True)
    g["sg_out_w"] = _mm("mm_sg_out_dw", sv["ub"], dyb, ta=True)
    du, dvn, g["sg_w"], dbcol = _sg_mix_bwd(sv["u"], sv["vn"], w["sg_w"], sv["bcol"], dub)
    g["sg_b"] = dbcol.reshape(SG_GROUPS, SG_CHUNK)
    dp, g["sg_ln_g"], g["sg_ln_b"] = _row_bwd(
        "sg_pre_bwd", _f_sg_pre, [(p, cw, blk(lay.b, cw)), (p, cw, blk(lay.b + cw, cw))],
        [_row(w["sg_ln_g"]), _row(w["sg_ln_b"])], [(du, cw, 0), (dvn, cw, 0)], [((0, 1), lo)], [0, 1],
        place=(lay.width, blk(lay.b, d)), into=dp)
    dz3 = _mm("mm_conv_out_dx", dya, w["conv_out_w"], tb=True)
    g["conv_out_w"] = _mm("mm_conv_out_dw", sv["z3"], dya, ta=True)
    dz1, g["conv_ln_g"], g["conv_ln_b"] = _row_bwd(
        "ln_silu_bwd", _f_ln_silu, [(sv["z1"], cw, 0)], [_row(w["conv_ln_g"]), _row(w["conv_ln_b"])],
        [(dz3, cw, 0)], [((0,), F32)], [0, 1])
    dp, g["conv_dw_w"], g["conv_dw_b"] = _glu_conv_bwd(p, blk(lay.a, 2 * _LANES), w["conv_dw_w"], dz1, dp)
    dh = _mm("mm_in_dx", dp, w["w_in"])
    g["w_in"] = _mm("mm_in_dw", dp, sv["h"], ta=True)
    return dx, dh, g


def _local_step(x, positions, target, layers):
    d = x.shape[1]
    tc, ts = _rope_tables(positions)
    ws = [_prep_layer(wl, d) for wl in layers]
    depth = len(ws)
    (h,) = _row_fwd("rms_first_fwd", _f_rms, [(x, d, 0)], [_row(ws[0]["mix_pre_g"])], [(d, _MXU_DT)])
    saved = []
    for l in range(depth):
        g_next = ws[l + 1]["mix_pre_g"] if l + 1 < depth else None
        x, h, sv = _layer_fwd(x, h, ws[l], g_next, tc, ts)
        saved.append(sv)
    loss, dx = _loss_head(x, target)
    grads = [None] * depth
    dh = None
    for l in reversed(range(depth)):
        g_next = ws[l + 1]["mix_pre_g"] if l + 1 < depth else None
        dx, dh, g = _layer_bwd(dx, dh, ws[l], g_next, saved[l], tc, ts)
        if "next_pre_g" in g:
            grads[l + 1]["mix_pre_g"] = g.pop("next_pre_g")
        grads[l] = g
    x0 = saved[0]["x"]
    grad_x, grads[0]["mix_pre_g"] = _row_bwd("rms_first_bwd", _f_x_rms, [(x0, d, 0)], [_row(ws[0]["mix_pre_g"])],
                                             [(dx, d, 0), (dh, d, 0)], [(0, F32)], [0])
    return loss, grad_x, [_unprep_grads(g, d) for g in grads]


_MATRICES = ("w_in", "conv_out_w", "sg_out_w", "mla_w_uq", "mla_w_ukv", "mla_w_o", "w_out", "ffn_w_up", "ffn_w_down")
_F32_GATHERED = ("conv_dw_w", "ffn_dw_w")
_RS_DT = jnp.bfloat16


_ROW_SHARDED = SHARDED_MID + ("w_in",)


_GATHERED = SHARDED + _F32_GATHERED
_RS_CORE0 = ("w_in", "ffn_w_down")
_LATE_WEIGHTS = ("w_out", "ffn_w_up", "ffn_dw_w", "ffn_w_down")


def _layer_shards(w, l):
    hi = {n: w[n][l].astype(jnp.bfloat16) for n in SHARDED}
    lo = [(w[n][l] - hi[n].astype(F32)).astype(jnp.bfloat16) for n in _F32_GATHERED]
    return [hi[n] for n in SHARDED] + lo


def _layer_weights(names, gathered):
    wl = {}
    for n, g in zip(names, gathered):
        whole = jnp.concatenate([g[j] for j in range(4)], axis=0 if n in _ROW_SHARDED else 1)
        if n in wl:
            wl[n] = wl[n].astype(F32) + whole.astype(F32)
        else:
            wl[n] = whole.astype(_MXU_DT) if n in _MATRICES else whole
    return wl


def _by_destination(name, gl):
    if gl.ndim == 3:
        return gl
    k, n = gl.shape
    if name in _ROW_SHARDED:
        return gl.reshape(4, k // 4, n)
    return gl.reshape(k, 4, n // 4).transpose(1, 0, 2)


def kernel(x, positions, mix_pre_g, mix_post_g, ffn_pre_g, ffn_post_g, w_in, conv_dw_w, conv_dw_b, conv_ln_g, conv_ln_b, conv_out_w, sg_ln_g, sg_ln_b, sg_w, sg_b, sg_out_w, mla_q_norm_g, mla_w_uq, mla_kv_norm_g, mla_w_ukv, mla_w_o, w_out, ffn_w_up, ffn_dw_w, ffn_dw_b, ffn_w_down, loss_target, m_mix_pre_g, m_mix_post_g, m_ffn_pre_g, m_ffn_post_g, m_w_in, m_conv_dw_w, m_conv_dw_b, m_conv_ln_g, m_conv_ln_b, m_conv_out_w, m_sg_ln_g, m_sg_ln_b, m_sg_w, m_sg_b, m_sg_out_w, m_mla_q_norm_g, m_mla_w_uq, m_mla_kv_norm_g, m_mla_w_ukv, m_mla_w_o, m_w_out, m_ffn_w_up, m_ffn_dw_w, m_ffn_dw_b, m_ffn_w_down, v_mix_pre_g, v_mix_post_g, v_ffn_pre_g, v_ffn_post_g, v_w_in, v_conv_dw_w, v_conv_dw_b, v_conv_ln_g, v_conv_ln_b, v_conv_out_w, v_sg_ln_g, v_sg_ln_b, v_sg_w, v_sg_b, v_sg_out_w, v_mla_q_norm_g, v_mla_w_uq, v_mla_kv_norm_g, v_mla_w_ukv, v_mla_w_o, v_w_out, v_ffn_w_up, v_ffn_dw_w, v_ffn_dw_b, v_ffn_w_down):
    args = dict(locals())
    w = {n: args[n] for n in WEIGHTS}
    m = {n: args["m_" + n] for n in WEIGHTS}
    v = {n: args["v_" + n] for n in WEIGHTS}
    depth = mix_pre_g.shape[0]

    assert depth == 2, "the two cores of a chip split the communication by layer"
    for t in (w, m, v):
        t["w_in"] = jnp.swapaxes(t["w_in"], 1, 2)
    d = x.shape[-1]
    mesh_x, mesh_y, mesh_c = _mesh_pos()
    my_chip = 2 * mesh_x + mesh_y
    names = list(SHARDED)
    whole = lambda ref, chip, k: ref
    to_my_slot = lambda ref, chip, k: ref.at[2 * lax.axis_index("x") + lax.axis_index("y")]
    block_of_chip = lambda ref, chip, k: ref.at[_chip_index(chip)]
    slot_k = lambda ref, chip, k: ref.at[k]

    late = [i for i, n in enumerate(_GATHERED) if n in _LATE_WEIGHTS]
    early = [i for i in range(len(_GATHERED)) if i not in late]
    pick = lambda seq, idx: [seq[i] for i in idx]
    gathered_names = list(_GATHERED)
    shards0, shards1 = _layer_shards(w, 0), _layer_shards(w, 1)
    replicated = lambda l: {n: w[n][l] for n in REPLICATED}
    land_of = lambda shards: [jax.ShapeDtypeStruct((4,) + a.shape, a.dtype) for a in shards]

    gathered0 = _ag_forward("ag_layer0_early", pick(shards0, early), None, 0, False)
    late0 = pick(shards0, late)
    sems0_s, sems0_r, late0, lands0, token0 = _split_start("ag0_start", [0] * len(late), late0, land_of(late0), whole,
                                                           to_my_slot, gathered0[0])
    ag_owner = [1] * len(shards1)
    sems_s, sems_r, shards1, lands1, token = _split_start("ag1_start", ag_owner, shards1, land_of(shards1), whole,
                                                          to_my_slot, gathered0[0])
    tc, ts = _rope_tables(positions[0])
    ws0 = _prep_layer({**replicated(0), **_layer_weights(pick(gathered_names, early), gathered0)}, d)

    def late_weights(merged):
        got = _split_wait("ag0_wait", [0] * len(late), sems0_s, sems0_r, late0, lands0, [merged], whole, to_my_slot)
        got = _ag_forward("ag_layer0_late", got[0], got[1], 0, True)
        ws0.update(_layer_weights(pick(gathered_names, late), got))
        return ws0

    x0 = x[0] + (token0[0, 0] + token[0, 0])
    (h0,) = _row_fwd("rms_first_fwd", _f_rms, [(x0, d, 0)], [_row(ws0["mix_pre_g"])], [(d, _MXU_DT)])
    x1, h1, sv0 = _layer_fwd(x0, h0, ws0, w["mix_pre_g"][1], tc, ts, late_weights)
    shards1, lands1 = _split_wait("ag1_wait", ag_owner, sems_s, sems_r, shards1, lands1, [x1], whole, to_my_slot)
    gathered1 = _ag_forward("ag_layer1", shards1, lands1, 1, True)
    ws1 = _prep_layer({**replicated(1), **_layer_weights(gathered_names, gathered1)}, d)
    x2, _, sv1 = _layer_fwd(x1, h1, ws1, None, tc, ts)
    loss, dx = _loss_head(x2, loss_target[0])
    loss = lax.psum(loss[0, 0], ("x", "y", "c"))

    owners = [0 if n in _RS_CORE0 else 1 for n in names]
    flags = [jnp.stack([my_chip, (mesh_c == o).astype(jnp.int32)]).astype(jnp.int32) for o in owners]

    def pair_sums(l, grads_l):
        gd = [_by_destination(n, grads_l[n]) for n in names]
        got = _pair_exchange("rs_swap%d" % l, gd, owners, True)
        return [_add_pair("rs_pair%d_%s" % (l, n), g, a, fl) for n, g, a, fl in zip(names, gd, got, flags)]

    dx, dh, gk1 = _layer_bwd(dx, None, ws1, None, sv1, tc, ts)
    grads1 = _unprep_grads(gk1, d)
    t1 = pair_sums(1, grads1)
    lands = [jax.ShapeDtypeStruct((3,) + t.shape[1:], t.dtype) for t in t1]
    sems_s, sems_r, t1, b1, token = _split_start("rs1_start", owners, t1, lands, block_of_chip, slot_k, dh)
    dx = dx + token[0, 0]
    dx, dh, gk0 = _layer_bwd(dx, dh, ws0, ws1["mix_pre_g"], sv0, tc, ts)
    grads1["mix_pre_g"] = gk0.pop("next_pre_g")
    grad_x, gk0["mix_pre_g"] = _row_bwd("rms_first_bwd", _f_x_rms, [(x0, d, 0)], [_row(ws0["mix_pre_g"])],
                                        [(dx, d, 0), (dh, d, 0)], [(0, F32)], [0])
    t1, b1 = _split_wait("rs1_wait", owners, sems_s, sems_r, t1, b1, [grad_x], block_of_chip, slot_k)
    grads0 = _unprep_grads(gk0, d)
    t0 = pair_sums(0, grads0)
    lands = [jax.ShapeDtypeStruct((3,) + t.shape[1:], t.dtype) for t in t0]
    sems_s, sems_r, t0, b0, token = _split_start("rs0_start", owners, t0, lands, block_of_chip, slot_k, grad_x)

    def finish(l, sums, updates):
        sums = _pair_exchange("rs_join%d" % l, sums, owners, False, layer=l)
        updates = [_adamw("adamw%d_%s" % (l, n), w[n], sums[i], m[n], v[n], l, updates[i])
                   for i, n in enumerate(names)]
        return sums, updates

    sums = [_add_quads("rs_sum1_" + n, t1[i], b1[i], 1, flags[i], None, token) for i, n in enumerate(names)]
    sums, updates = finish(1, sums, [None] * len(names))
    out = {}
    rep = list(REPLICATED)
    grads = [grads0, grads1]
    g_rep = [jnp.stack([grads[l][n].reshape(w[n].shape[1:]) for l in range(depth)]) for n in rep]
    for n, *res in zip(rep, *_all_reduce_adamw(g_rep, [w[n] for n in rep], [m[n] for n in rep], [v[n] for n in rep])):
        out[n] = tuple(res)
    hidden = [u[0] for u in updates] + [out[rep[0]][0]]
    t0, b0 = _split_wait("rs0_wait", owners, sems_s, sems_r, t0, b0, hidden, block_of_chip, slot_k)
    sums = [_add_quads("rs_sum0_" + n, t0[i], b0[i], 0, flags[i], sums[i]) for i, n in enumerate(names)]
    sums, updates = finish(0, sums, updates)
    for n, gr, upd in zip(names, sums, updates):
        out[n] = (gr, *upd)
    out["w_in"] = tuple(jnp.swapaxes(a, 1, 2) for a in out["w_in"])
    return (loss, grad_x[None], *[out[n][i] for i in range(4) for n in WEIGHTS])
```

```python
import functools
import math

import jax
import jax.numpy as jnp
from jax import lax
from jax.experimental import pallas as pl
from jax.experimental.pallas import tpu as pltpu

F32 = jnp.float32
_MXU_DT = jnp.bfloat16
_VMEM_LIMIT = 48 * 1024 * 1024
_LANES = 128
_MESH = pl.DeviceIdType.MESH

N_HEADS = 8
QK_NOPE = 64
QK_ROPE = 32
V_HEAD = 64
HEAD_PAD = 128
SG_GROUPS = 4
SG_CHUNK = 128
CONV_K = 31
FFN_K = 3
ROPE_THETA = 10000.0
EPS = 1e-6
ADAM_LR, ADAM_B1, ADAM_B2, ADAM_EPS, ADAM_WD, ADAM_STEP = 0.001, 0.9, 0.999, 1e-08, 0.01, 10

SHARDED_LAST = ("w_in", "conv_dw_w", "conv_out_w", "sg_out_w", "mla_w_uq", "mla_w_ukv", "mla_w_o", "ffn_w_up",
                "ffn_dw_w")
SHARDED_MID = ("w_out", "ffn_w_down")
SHARDED = SHARDED_LAST + SHARDED_MID
WEIGHTS = ("mix_pre_g", "mix_post_g", "ffn_pre_g", "ffn_post_g", "w_in", "conv_dw_w", "conv_dw_b", "conv_ln_g",
           "conv_ln_b", "conv_out_w", "sg_ln_g", "sg_ln_b", "sg_w", "sg_b", "sg_out_w", "mla_q_norm_g", "mla_w_uq",
           "mla_kv_norm_g", "mla_w_ukv", "mla_w_o", "w_out", "ffn_w_up", "ffn_dw_w", "ffn_dw_b", "ffn_w_down")
REPLICATED = tuple(n for n in WEIGHTS if n not in SHARDED)


def _cparams(sem=None):
    return pltpu.CompilerParams(dimension_semantics=sem, vmem_limit_bytes=_VMEM_LIMIT)


def _pick(n, cands):
    for c in cands:
        if n % c == 0:
            return c
    return n


def _largest_tile(dim, cap):
    for t in range(min(cap, dim) // _LANES * _LANES, 0, -_LANES):
        if dim % t == 0:
            return t
    return dim


_MM_VMEM_BUDGET = 36 * 1024 * 1024
_MM_TM_CAP, _MM_TN_CAP, _MM_TK_CAP = 1024, 1536, 3072


def _mm(name, a, b, *, ta=False, tb=False, out_dtype=F32, a_halves=False, b_halves=False, out_quarters=False):
    assert not (a_halves and ta) and not (b_halves and tb)
    if a_halves:
        m, kdim = a.shape[1], 2 * a.shape[2]
    else:
        (kdim, m) = a.shape if ta else a.shape[::-1]
    if b_halves:
        kdim2, n = b.shape[1], 2 * b.shape[2]
    else:
        (n, kdim2) = b.shape if tb else b.shape[::-1]
    assert kdim == kdim2, (a.shape, b.shape, ta, tb)
    tk = _largest_tile(kdim // 2 if a_halves else kdim, _MM_TK_CAP)
    tn = _largest_tile(n // 4 if out_quarters else (n // 2 if b_halves else n), _MM_TN_CAP)
    nk = kdim // tk
    ab, bb, ob = a.dtype.itemsize, b.dtype.itemsize, jnp.dtype(out_dtype).itemsize
    tm = _largest_tile(m, _MM_TM_CAP)
    vmem = lambda t: 2 * (t * tk * ab + tk * tn * bb + t * tn * ob) + (t * tn * 4 if nk > 1 else 0)
    while vmem(tm) > _MM_VMEM_BUDGET and tm > _LANES:
        tm = _largest_tile(m, tm - _LANES)
    dims = (((0 if ta else 1,), (1 if tb else 0,)), ((), ()))

    def dot(a_ref, b_ref):
        return lax.dot_general(a_ref[...].astype(_MXU_DT), b_ref[...].astype(_MXU_DT), dims,
                               preferred_element_type=F32)

    def body_one(a_ref, b_ref, o_ref):
        o_ref[...] = dot(a_ref, b_ref).astype(o_ref.dtype)

    def body_acc(a_ref, b_ref, o_ref, acc_ref):
        k = pl.program_id(2)

        @pl.when(k == 0)
        def _():
            acc_ref[...] = jnp.zeros_like(acc_ref)

        acc_ref[...] += dot(a_ref, b_ref)

        @pl.when(k == nk - 1)
        def _():
            o_ref[...] = acc_ref[...].astype(o_ref.dtype)

    if a_halves:
        per = nk // 2
        a_spec = pl.BlockSpec((None, tm, tk), lambda i, j, k: (k // per, i, k % per))
    elif ta:
        a_spec = pl.BlockSpec((tk, tm), lambda i, j, k: (k, i))
    else:
        a_spec = pl.BlockSpec((tm, tk), lambda i, j, k: (i, k))
    if b_halves:
        per_b = n // 2 // tn
        b_spec = pl.BlockSpec((None, tk, tn), lambda i, j, k: (j // per_b, k, j % per_b))
    elif tb:
        b_spec = pl.BlockSpec((tn, tk), lambda i, j, k: (j, k))
    else:
        b_spec = pl.BlockSpec((tk, tn), lambda i, j, k: (k, j))
    if out_quarters:
        per_o = n // 4 // tn
        o_spec = pl.BlockSpec((None, tm, tn), lambda i, j, k: (j // per_o, i, j % per_o))
        o_shape = jax.ShapeDtypeStruct((4, m, n // 4), out_dtype)
    else:
        o_spec = pl.BlockSpec((tm, tn), lambda i, j, k: (i, j))
        o_shape = jax.ShapeDtypeStruct((m, n), out_dtype)
    return pl.pallas_call(
        body_one if nk == 1 else body_acc, name=name, grid=(m // tm, n // tn, nk),
        in_specs=[a_spec, b_spec], out_specs=o_spec, out_shape=o_shape,
        scratch_shapes=[] if nk == 1 else [pltpu.VMEM((tm, tn), F32)],
        compiler_params=_cparams(("parallel", "parallel", "arbitrary")),
    )(a, b)


def _row_spec(tm, width, idx):
    return pl.BlockSpec((tm, width), lambda i: (i, idx))


def _full_spec(shape):
    zeros = (0,) * len(shape)
    return pl.BlockSpec(shape, lambda i: zeros)


def _row_fwd(name, fn, rows, params, outs, tm=256):
    s = rows[0][0].shape[0]
    nr, npar = len(rows), len(params)

    def body(*refs):
        vals = [r[...].astype(F32) for r in refs[:nr + npar]]
        res = fn(*vals)
        for o_ref, r in zip(refs[nr + npar:], res):
            o_ref[...] = r.astype(o_ref.dtype)

    return pl.pallas_call(
        body, name=name, grid=(s // tm,),
        in_specs=[_row_spec(tm, w, i) for _, w, i in rows] + [_full_spec(p.shape) for p in params],
        out_specs=[_row_spec(tm, w, 0) for w, _ in outs],
        out_shape=[jax.ShapeDtypeStruct((s, w), dt) for w, dt in outs],
        compiler_params=_cparams(("parallel",)),
    )(*[r[0] for r in rows], *params)


def _row_bwd(name, fn, rows, params, cots, row_grads, param_grads, tm=256, place=None, into=None):
    s = rows[0][0].shape[0]
    nr, npar, nc = len(rows), len(params), len(cots)
    row_grads = [((idxs,) if isinstance(idxs, int) else tuple(idxs), dt) for idxs, dt in row_grads]
    widths = [sum(rows[i][1] for i in idxs) for idxs, _ in row_grads]

    def body(*refs):
        i = pl.program_id(0)
        vals = [r[...].astype(F32) for r in refs[:nr + npar]]
        cvals = tuple(r[...].astype(F32) for r in refs[nr + npar:nr + npar + nc])
        _, vjp = jax.vjp(fn, *vals)
        grads = vjp(cvals)
        outs = refs[nr + npar + nc + (into is not None):]
        for o_ref, (idxs, _) in zip(outs, row_grads):
            pos = 0
            for idx in idxs:
                o_ref[:, pos:pos + rows[idx][1]] = grads[idx].astype(o_ref.dtype)
                pos += rows[idx][1]
        for o_ref, idx in zip(outs[len(row_grads):], param_grads):
            @pl.when(i == 0)
            def _(o_ref=o_ref):
                o_ref[...] = jnp.zeros_like(o_ref)

            o_ref[...] += grads[nr + idx]

    out_specs = [_row_spec(tm, w, 0) for w in widths] + [_full_spec(params[idx].shape) for idx in param_grads]
    out_shape = ([jax.ShapeDtypeStruct((s, w), dt) for w, (_, dt) in zip(widths, row_grads)]
                 + [jax.ShapeDtypeStruct(params[idx].shape, F32) for idx in param_grads])
    extra, aliases = [], {}
    if place is not None:
        out_specs[0] = _row_spec(tm, widths[0], place[1])
        out_shape[0] = jax.ShapeDtypeStruct((s, place[0]), row_grads[0][1])
    if into is not None:
        extra, aliases = [into], {nr + npar + nc: 0}
    return pl.pallas_call(
        body, name=name, grid=(s // tm,),
        in_specs=([_row_spec(tm, w, i) for _, w, i in rows] + [_full_spec(p.shape) for p in params]
                  + [_row_spec(tm, w, i) for _, w, i in cots] + [_ANY] * len(extra)),
        out_specs=out_specs, out_shape=out_shape, input_output_aliases=aliases,
        compiler_params=_cparams(("arbitrary",)),
    )(*[r[0] for r in rows], *params, *[c[0] for c in cots], *extra)


def _rms(x, g):
    return x * lax.rsqrt(jnp.mean(x * x, axis=-1, keepdims=True) + EPS) * g


def _ln(x, g, b):
    mu = jnp.mean(x, axis=-1, keepdims=True)
    xc = x - mu
    var = jnp.mean(xc * xc, axis=-1, keepdims=True)
    return xc * lax.rsqrt(var + EPS) * g + b


def _sigmoid(x):
    return 1.0 / (1.0 + jnp.exp(-x))


def _gelu(x):
    return x * (0.5 * (1.0 + jnp.tanh(math.sqrt(2.0 / math.pi) * (x + 0.044715 * (x * x * x)))))


def _f_rms(x, g):
    return (_rms(x, g),)


def _f_x_rms(x, g):
    return (x, _rms(x, g))


def _f_ln_silu(z, g, b):
    y = _ln(z, g, b)
    return (y * _sigmoid(y),)


def _f_sg_pre(bu, bv, g, b):
    return (_gelu(bu), _ln(_gelu(bv), g, b))


def _f_merge(g0, g1, g2, ya, yb, yc):
    return (_sigmoid(g0) * ya + _sigmoid(g1) * yb + _sigmoid(g2) * yc,)


def _f_resid_rms(x, t, g_post):
    return (x + _rms(t, g_post),)


def _f_resid_rms_rms(x, t, g_post, g_next):
    x1 = x + _rms(t, g_post)
    return (x1, _rms(x1, g_next))


def _f_geglu(zg, zv):
    return _gelu(zg) * zv


_CONV_TILE_ELEMS = 16 * 1024


def _conv_tr(c):
    return _CONV_TILE_ELEMS // c


def _conv_tile(zp_ref, w_ref, bias, k_taps, off, r0):
    c = zp_ref.shape[1]
    tr = _conv_tr(c)
    acc = jnp.broadcast_to(bias, (tr, c))
    for k in range(k_taps):
        acc = acc + w_ref[k:k + 1, :] * zp_ref[r0 + off + k:r0 + off + k + tr, :]
    return acc


def _conv_bwd_input_tile(dzp_ref, w_ref, k_taps, r0):
    c = dzp_ref.shape[1]
    tr = _conv_tr(c)
    acc = jnp.zeros((tr, c), F32)
    for k in range(k_taps):
        s0 = r0 + (k_taps - 1) - k
        acc = acc + w_ref[k:k + 1, :] * dzp_ref[s0:s0 + tr, :]
    return acc


def _conv_bwd_weight(dzp_ref, zp_ref, dw_ref, db_ref, k_taps, off, s):
    c = zp_ref.shape[1]
    tr = _conv_tr(c)
    fold = lambda v: jnp.sum(v.reshape(tr // 8, 8, c), axis=0)
    for k in range(k_taps):
        acc = jnp.zeros((8, c), F32)
        for r in range(s // tr):
            r0 = r * tr
            acc = acc + fold(dzp_ref[r0:r0 + tr, :] * zp_ref[r0 + off + k:r0 + off + k + tr, :])
        dw_ref[k:k + 1, :] = jnp.sum(acc, axis=0, keepdims=True)
    acc = jnp.zeros((8, c), F32)
    for r in range(s // tr):
        acc = acc + fold(dzp_ref[r * tr:(r + 1) * tr, :])
    db_ref[...] = jnp.sum(acc, axis=0, keepdims=True)


def _glu_conv_fwd(p, blk0, w, b):
    s = p.shape[0]
    k_taps, c = w.shape
    cb, pad = _LANES, 32
    off = pad - (k_taps - 1)

    def body(a_ref, w_ref, b_ref, o_ref, zp_ref):
        zp_ref[0:pad, :] = jnp.zeros((pad, cb), F32)
        zp_ref[pad:pad + s, :] = a_ref[:, 0:cb] * _sigmoid(a_ref[:, cb:2 * cb])
        tr = _conv_tr(cb)
        for r in range(s // tr):
            o_ref[r * tr:(r + 1) * tr, :] = _conv_tile(zp_ref, w_ref, b_ref[...], k_taps, off, r * tr)

    return pl.pallas_call(
        body, name="glu_conv_fwd", grid=(c // cb,),
        in_specs=[pl.BlockSpec((s, 2 * cb), lambda j: (0, blk0 + j)),
                  pl.BlockSpec((k_taps, cb), lambda j: (0, j)), pl.BlockSpec((1, cb), lambda j: (0, j))],
        out_specs=pl.BlockSpec((s, cb), lambda j: (0, j)),
        out_shape=jax.ShapeDtypeStruct((s, c), F32),
        scratch_shapes=[pltpu.VMEM((s + pad, cb), F32)],
        compiler_params=_cparams(("parallel",)),
    )(p, w, b)


def _glu_conv_bwd(p, blk0, w, dz, dp):
    s = p.shape[0]
    k_taps, c = w.shape
    cb, pad = _LANES, 32
    off = pad - (k_taps - 1)

    def body(a_ref, w_ref, dz_ref, dp_in, da_ref, dw_ref, db_ref, zp_ref, dzp_ref):
        zp_ref[0:pad, :] = jnp.zeros((pad, cb), F32)
        zp_ref[pad:pad + s, :] = a_ref[:, 0:cb] * _sigmoid(a_ref[:, cb:2 * cb])
        dzp_ref[0:s, :] = dz_ref[...]
        dzp_ref[s:s + pad, :] = jnp.zeros((pad, cb), F32)
        tr = _conv_tr(cb)
        for r in range(s // tr):
            rows = slice(r * tr, (r + 1) * tr)
            dz0 = _conv_bwd_input_tile(dzp_ref, w_ref, k_taps, r * tr)
            sg = _sigmoid(a_ref[rows, cb:2 * cb])
            da_ref[rows, 0:cb] = (dz0 * sg).astype(da_ref.dtype)
            da_ref[rows, cb:2 * cb] = (dz0 * a_ref[rows, 0:cb] * sg * (1.0 - sg)).astype(da_ref.dtype)
        _conv_bwd_weight(dzp_ref, zp_ref, dw_ref, db_ref, k_taps, off, s)

    return pl.pallas_call(
        body, name="glu_conv_bwd", grid=(c // cb,),
        in_specs=[pl.BlockSpec((s, 2 * cb), lambda j: (0, blk0 + j)),
                  pl.BlockSpec((k_taps, cb), lambda j: (0, j)), pl.BlockSpec((s, cb), lambda j: (0, j)), _ANY],
        out_specs=[pl.BlockSpec((s, 2 * cb), lambda j: (0, blk0 + j)),
                   pl.BlockSpec((k_taps, cb), lambda j: (0, j)), pl.BlockSpec((1, cb), lambda j: (0, j))],
        out_shape=[jax.ShapeDtypeStruct(dp.shape, dp.dtype),
                   jax.ShapeDtypeStruct((k_taps, c), F32), jax.ShapeDtypeStruct((1, c), F32)],
        scratch_shapes=[pltpu.VMEM((s + pad, cb), F32), pltpu.VMEM((s + pad, cb), F32)],
        input_output_aliases={3: 0},
        compiler_params=_cparams(("parallel",)),
    )(p, w, dz, dp)


def _conv_geglu_fwd(up, w, b):
    s, f2 = up.shape
    f = f2 // 2
    k_taps = w.shape[0]
    cb, pad = _FFN_CB, 8
    off = pad - (k_taps - 1)
    nb = f // cb

    def body(ug_ref, uv_ref, wg_ref, wv_ref, bg_ref, bv_ref, o_ref, z_ref, w_ref, b_ref):
        _pair(w_ref, wg_ref[...], wv_ref[...], cb)
        _pair(b_ref, bg_ref[...], bv_ref[...], cb)
        z_ref[0:pad, :] = jnp.zeros((pad, 2 * cb), F32)
        z_ref[pad:pad + s, 0:cb] = ug_ref[...]
        z_ref[pad:pad + s, cb:2 * cb] = uv_ref[...]
        tr = _conv_tr(2 * cb)
        for r in range(s // tr):
            z = _conv_tile(z_ref, w_ref, b_ref[...], k_taps, off, r * tr)
            o_ref[r * tr:(r + 1) * tr, :] = _f_geglu(z[:, 0:cb], z[:, cb:2 * cb]).astype(o_ref.dtype)

    two = lambda rows_: [pl.BlockSpec((rows_, cb), lambda j: (0, j)), pl.BlockSpec((rows_, cb), lambda j: (0, nb + j))]
    return pl.pallas_call(
        body, name="conv_geglu_fwd", grid=(nb,),
        in_specs=two(s) + two(k_taps) + two(1),
        out_specs=pl.BlockSpec((s, cb), lambda j: (0, j)),
        out_shape=jax.ShapeDtypeStruct((s, f), _MXU_DT),
        scratch_shapes=[pltpu.VMEM((s + pad, 2 * cb), F32), pltpu.VMEM((k_taps, 2 * cb), F32),
                        pltpu.VMEM((1, 2 * cb), F32)],
        compiler_params=_cparams(("parallel",)),
    )(up, up, w, w, b, b)


def _pair(dst_ref, first, second, cb):
    dst_ref[:, 0:cb] = first
    dst_ref[:, cb:2 * cb] = second


def _conv_geglu_bwd(up, w, b, dact):
    s, f2 = up.shape
    f = f2 // 2
    k_taps = w.shape[0]
    cb, pad = _FFN_CB, 8
    off = pad - (k_taps - 1)
    nb = f // cb

    def body(ug_ref, uv_ref, wg_ref, wv_ref, bg_ref, bv_ref, da_ref, du_ref, dw_ref, db_ref, z_ref, dz_ref, w_ref,
             b_ref, dw_sc, db_sc):
        _pair(w_ref, wg_ref[...], wv_ref[...], cb)
        _pair(b_ref, bg_ref[...], bv_ref[...], cb)
        z_ref[0:pad, :] = jnp.zeros((pad, 2 * cb), F32)
        z_ref[pad:pad + s, 0:cb] = ug_ref[...]
        z_ref[pad:pad + s, cb:2 * cb] = uv_ref[...]
        dz_ref[s:s + pad, :] = jnp.zeros((pad, 2 * cb), F32)
        tr = _conv_tr(2 * cb)
        for r in range(s // tr):
            rows = slice(r * tr, (r + 1) * tr)
            z = _conv_tile(z_ref, w_ref, b_ref[...], k_taps, off, r * tr)
            _, vjp = jax.vjp(_f_geglu, z[:, 0:cb], z[:, cb:2 * cb])
            dzg, dzv = vjp(da_ref[rows, :].astype(F32))
            dz_ref[rows, 0:cb] = dzg
            dz_ref[rows, cb:2 * cb] = dzv
        for r in range(s // tr):
            rows = slice(r * tr, (r + 1) * tr)
            du = _conv_bwd_input_tile(dz_ref, w_ref, k_taps, r * tr).astype(du_ref.dtype)
            du_ref[0, rows, :] = du[:, 0:cb]
            du_ref[1, rows, :] = du[:, cb:2 * cb]
        _conv_bwd_weight(dz_ref, z_ref, dw_sc, db_sc, k_taps, off, s)
        for half in range(2):
            dw_ref[half] = dw_sc[:, half * cb:(half + 1) * cb]
            db_ref[half] = db_sc[:, half * cb:(half + 1) * cb]

    two = lambda rows_: [pl.BlockSpec((rows_, cb), lambda j: (0, j)), pl.BlockSpec((rows_, cb), lambda j: (0, nb + j))]
    both = lambda rows_: pl.BlockSpec((2, rows_, cb), lambda j: (0, 0, j))
    return pl.pallas_call(
        body, name="conv_geglu_bwd", grid=(nb,),
        in_specs=two(s) + two(k_taps) + two(1) + [pl.BlockSpec((s, cb), lambda j: (0, j))],
        out_specs=[both(s), both(k_taps), both(1)],
        out_shape=[jax.ShapeDtypeStruct((2, s, f), _MXU_DT), jax.ShapeDtypeStruct((2, k_taps, f), F32),
                   jax.ShapeDtypeStruct((2, 1, f), F32)],
        scratch_shapes=[pltpu.VMEM((s + pad, 2 * cb), F32), pltpu.VMEM((s + pad, 2 * cb), F32),
                        pltpu.VMEM((k_taps, 2 * cb), F32), pltpu.VMEM((1, 2 * cb), F32),
                        pltpu.VMEM((k_taps, 2 * cb), F32), pltpu.VMEM((1, 2 * cb), F32)],
        compiler_params=_cparams(("parallel",)),
    )(up, up, w, w, b, b, dact)


def _tril_mask():
    t = lax.broadcasted_iota(jnp.int32, (SG_CHUNK, SG_CHUNK), 0)
    s = lax.broadcasted_iota(jnp.int32, (SG_CHUNK, SG_CHUNK), 1)
    return t >= s


def _sg_mix_fwd(u, vn, w, bcol):
    s, c = u.shape
    gw = c // SG_GROUPS

    def body(u_ref, v_ref, w_ref, b_ref, o_ref):
        wm = jnp.where(_tril_mask(), w_ref[0], 0.0).astype(_MXU_DT)
        for n in range(s // SG_CHUNK):
            rows = slice(n * SG_CHUNK, (n + 1) * SG_CHUNK)
            mixed = jnp.dot(wm, v_ref[rows, :], preferred_element_type=F32) + b_ref[0]
            o_ref[rows, :] = (u_ref[rows, :] * mixed).astype(o_ref.dtype)

    return pl.pallas_call(
        body, name="sg_mix_fwd", grid=(SG_GROUPS,),
        in_specs=[pl.BlockSpec((s, gw), lambda g: (0, g)), pl.BlockSpec((s, gw), lambda g: (0, g)),
                  pl.BlockSpec((1, SG_CHUNK, SG_CHUNK), lambda g: (g, 0, 0)),
                  pl.BlockSpec((1, SG_CHUNK, 1), lambda g: (g, 0, 0))],
        out_specs=pl.BlockSpec((s, gw), lambda g: (0, g)),
        out_shape=jax.ShapeDtypeStruct((s, c), _MXU_DT),
        compiler_params=_cparams(("parallel",)),
    )(u, vn, w, bcol)


def _sg_mix_bwd(u, vn, w, bcol, dub):
    s, c = u.shape
    gw = c // SG_GROUPS

    def body(u_ref, v_ref, w_ref, b_ref, d_ref, du_ref, dv_ref, dw_ref, db_ref):
        mask = _tril_mask()
        wm = jnp.where(mask, w_ref[0], 0.0).astype(_MXU_DT)
        dw = jnp.zeros((SG_CHUNK, SG_CHUNK), F32)
        db = jnp.zeros((SG_CHUNK, 1), F32)
        for n in range(s // SG_CHUNK):
            rows = slice(n * SG_CHUNK, (n + 1) * SG_CHUNK)
            v = v_ref[rows, :]
            d = d_ref[rows, :].astype(F32)
            mixed = jnp.dot(wm, v, preferred_element_type=F32) + b_ref[0]
            du_ref[rows, :] = d * mixed
            dmix = d * u_ref[rows, :]
            dmix_lo = dmix.astype(_MXU_DT)
            dv_ref[rows, :] = lax.dot_general(wm, dmix_lo, (((0,), (0,)), ((), ())), preferred_element_type=F32)
            dw = dw + lax.dot_general(dmix_lo, v, (((1,), (1,)), ((), ())), preferred_element_type=F32)
            db = db + jnp.sum(dmix, axis=1, keepdims=True)
        dw_ref[0] = jnp.where(mask, dw, 0.0)
        db_ref[0] = db

    return pl.pallas_call(
        body, name="sg_mix_bwd", grid=(SG_GROUPS,),
        in_specs=[pl.BlockSpec((s, gw), lambda g: (0, g)), pl.BlockSpec((s, gw), lambda g: (0, g)),
                  pl.BlockSpec((1, SG_CHUNK, SG_CHUNK), lambda g: (g, 0, 0)),
                  pl.BlockSpec((1, SG_CHUNK, 1), lambda g: (g, 0, 0)), pl.BlockSpec((s, gw), lambda g: (0, g))],
        out_specs=[pl.BlockSpec((s, gw), lambda g: (0, g)), pl.BlockSpec((s, gw), lambda g: (0, g)),
                   pl.BlockSpec((1, SG_CHUNK, SG_CHUNK), lambda g: (g, 0, 0)),
                   pl.BlockSpec((1, SG_CHUNK, 1), lambda g: (g, 0, 0))],
        out_shape=[jax.ShapeDtypeStruct((s, c), F32), jax.ShapeDtypeStruct((s, c), F32),
                   jax.ShapeDtypeStruct((SG_GROUPS, SG_CHUNK, SG_CHUNK), F32),
                   jax.ShapeDtypeStruct((SG_GROUPS, SG_CHUNK, 1), F32)],
        compiler_params=_cparams(("parallel",)),
    )(u, vn, w, bcol, dub)


def _rope_fwd(q2, kv2, p, krm_idx, krs_idx, tc, ts):
    s = q2.shape[0]
    hw = N_HEADS * HEAD_PAD
    tm = 256

    def body(qm_ref, qs_ref, kn_ref, v_ref, krm_ref, krs_ref, tc_ref, ts_ref, q_ref, k_ref, vo_ref):
        tcv, tsv = tc_ref[...], ts_ref[...]
        kpe = krm_ref[...] * tcv + krs_ref[...] * tsv
        for h in range(N_HEADS):
            cols = slice(h * HEAD_PAD, (h + 1) * HEAD_PAD)
            q_ref[:, cols] = (qm_ref[:, cols] * tcv + qs_ref[:, cols] * tsv).astype(q_ref.dtype)
            k_ref[:, cols] = (kn_ref[:, cols] + kpe).astype(k_ref.dtype)
        vo_ref[...] = v_ref[...].astype(vo_ref.dtype)

    return pl.pallas_call(
        body, name="rope_fwd", grid=(s // tm,),
        in_specs=[_row_spec(tm, hw, 0), _row_spec(tm, hw, 1), _row_spec(tm, hw, 0), _row_spec(tm, hw, 1),
                  _row_spec(tm, HEAD_PAD, krm_idx), _row_spec(tm, HEAD_PAD, krs_idx),
                  _row_spec(tm, HEAD_PAD, 0), _row_spec(tm, HEAD_PAD, 0)],
        out_specs=[_row_spec(tm, hw, 0)] * 3,
        out_shape=[jax.ShapeDtypeStruct((s, hw), _MXU_DT)] * 3,
        compiler_params=_cparams(("parallel",)),
    )(q2, q2, kv2, kv2, p, p, tc, ts)


def _rope_bwd(dq, dk, dv, tc, ts, dp, kr_blk):
    s = dq.shape[0]
    hw = N_HEADS * HEAD_PAD
    tm = 256

    def body(dq_ref, dk_ref, dv_ref, tc_ref, ts_ref, dp_in, dq2_ref, dkv2_ref, dkr_ref):
        tcv, tsv = tc_ref[...], ts_ref[...]
        dkpe = jnp.zeros((tm, HEAD_PAD), F32)
        for h in range(N_HEADS):
            cols = slice(h * HEAD_PAD, (h + 1) * HEAD_PAD)
            dqh = dq_ref[:, cols]
            dq2_ref[:, cols] = (dqh * tcv).astype(dq2_ref.dtype)
            dq2_ref[:, hw + h * HEAD_PAD:hw + (h + 1) * HEAD_PAD] = (dqh * tsv).astype(dq2_ref.dtype)
            dkpe = dkpe + dk_ref[:, cols]
        dkv2_ref[:, 0:hw] = dk_ref[...].astype(dkv2_ref.dtype)
        dkv2_ref[:, hw:2 * hw] = dv_ref[...].astype(dkv2_ref.dtype)
        dkr_ref[:, 0:HEAD_PAD] = (dkpe * tcv).astype(dkr_ref.dtype)
        dkr_ref[:, HEAD_PAD:2 * HEAD_PAD] = (dkpe * tsv).astype(dkr_ref.dtype)
        dkr_ref[:, 2 * HEAD_PAD:3 * HEAD_PAD] = jnp.zeros((tm, HEAD_PAD), dkr_ref.dtype)

    return pl.pallas_call(
        body, name="rope_bwd", grid=(s // tm,),
        in_specs=[_row_spec(tm, hw, 0)] * 3 + [_row_spec(tm, HEAD_PAD, 0)] * 2 + [_ANY],
        out_specs=[_row_spec(tm, 2 * hw, 0), _row_spec(tm, 2 * hw, 0), _row_spec(tm, 3 * HEAD_PAD, kr_blk)],
        out_shape=[jax.ShapeDtypeStruct((s, 2 * hw), _MXU_DT), jax.ShapeDtypeStruct((s, 2 * hw), _MXU_DT),
                   jax.ShapeDtypeStruct(dp.shape, dp.dtype)],
        input_output_aliases={5: 2},
        compiler_params=_cparams(("parallel",)),
    )(dq, dk, dv, tc, ts, dp)


_ATTN_TQ = 256
_ATTN_SCALE = (QK_NOPE + QK_ROPE) ** -0.5


def _attn_probs(q, k, i):
    s = k.shape[0]
    sc = lax.dot_general(q, k, (((1,), (1,)), ((), ())), preferred_element_type=F32) * _ATTN_SCALE
    row = i * _ATTN_TQ + lax.broadcasted_iota(jnp.int32, (_ATTN_TQ, s), 0)
    col = lax.broadcasted_iota(jnp.int32, (_ATTN_TQ, s), 1)
    sc = jnp.where(row >= col, sc, jnp.finfo(F32).min)
    e = jnp.exp(sc - jnp.max(sc, axis=1, keepdims=True))
    return e * (1.0 / jnp.sum(e, axis=1, keepdims=True))


def _per_query_block(s, fn):
    i = pl.program_id(1)
    for n in range(s // _ATTN_TQ):
        @pl.when(i == n)
        def _(n=n):
            fn(n, (n + 1) * _ATTN_TQ)


def _attn_fwd(q, k, v):
    s = q.shape[0]

    def body(q_ref, k_ref, v_ref, o_ref):
        def block(n, kl):
            p = _attn_probs(q_ref[...], k_ref[0:kl, :], n)
            o_ref[...] = jnp.dot(p.astype(_MXU_DT), v_ref[0:kl, :], preferred_element_type=F32).astype(o_ref.dtype)

        _per_query_block(s, block)

    qspec = pl.BlockSpec((_ATTN_TQ, HEAD_PAD), lambda h, i: (i, h))
    kspec = pl.BlockSpec((s, HEAD_PAD), lambda h, i: (0, h))
    return pl.pallas_call(
        body, name="attn_fwd", grid=(N_HEADS, s // _ATTN_TQ),
        in_specs=[qspec, kspec, kspec], out_specs=qspec,
        out_shape=jax.ShapeDtypeStruct(q.shape, _MXU_DT),
        compiler_params=_cparams(("parallel", "parallel")),
    )(q, k, v)


def _attn_bwd(q, k, v, do):
    s = q.shape[0]

    def body(q_ref, k_ref, v_ref, do_ref, dq_ref, dk_ref, dv_ref):
        i = pl.program_id(1)

        @pl.when(i == 0)
        def _():
            dk_ref[...] = jnp.zeros_like(dk_ref)
            dv_ref[...] = jnp.zeros_like(dv_ref)

        def block(n, kl):
            qv, kv, dov = q_ref[...], k_ref[0:kl, :], do_ref[...]
            p = _attn_probs(qv, kv, n)
            dp = lax.dot_general(dov, v_ref[0:kl, :], (((1,), (1,)), ((), ())), preferred_element_type=F32)
            delta = jnp.sum(p * dp, axis=1, keepdims=True)
            ds = (p * (dp - delta) * _ATTN_SCALE).astype(_MXU_DT)
            dq_ref[...] = jnp.dot(ds, kv, preferred_element_type=F32)
            dk_ref[0:kl, :] += lax.dot_general(ds, qv, (((0,), (0,)), ((), ())), preferred_element_type=F32)
            dv_ref[0:kl, :] += lax.dot_general(p.astype(_MXU_DT), dov, (((0,), (0,)), ((), ())),
                                               preferred_element_type=F32)

        _per_query_block(s, block)

    qspec = pl.BlockSpec((_ATTN_TQ, HEAD_PAD), lambda h, i: (i, h))
    kspec = pl.BlockSpec((s, HEAD_PAD), lambda h, i: (0, h))
    return pl.pallas_call(
        body, name="attn_bwd", grid=(N_HEADS, s // _ATTN_TQ),
        in_specs=[qspec, kspec, kspec, qspec], out_specs=[qspec, kspec, kspec],
        out_shape=[jax.ShapeDtypeStruct(q.shape, F32)] * 3,
        compiler_params=_cparams(("parallel", "arbitrary")),
    )(q, k, v, do)


def _loss_head(y, target):
    s, d = y.shape
    tm = 256

    def body(y_ref, t_ref, loss_ref, dy_ref):
        @pl.when(pl.program_id(0) == 0)
        def _():
            loss_ref[...] = jnp.zeros_like(loss_ref)

        err = y_ref[...] - t_ref[...]
        loss_ref[...] += 0.5 * jnp.sum(jnp.mean(err * err, axis=-1, keepdims=True), axis=0, keepdims=True)
        dy_ref[...] = err * (1.0 / d)

    return pl.pallas_call(
        body, name="loss_head", grid=(s // tm,),
        in_specs=[_row_spec(tm, d, 0), _row_spec(tm, d, 0)],
        out_specs=[_full_spec((1, 1)), _row_spec(tm, d, 0)],
        out_shape=[jax.ShapeDtypeStruct((1, 1), F32), jax.ShapeDtypeStruct((s, d), F32)],
        compiler_params=_cparams(("arbitrary",)),
    )(y, target)


def _adamw_math(w, g, m, v):
    mn = ADAM_B1 * m + (1.0 - ADAM_B1) * g
    vn = ADAM_B2 * v + (1.0 - ADAM_B2) * (g * g)
    m_hat = mn / (1.0 - ADAM_B1 ** ADAM_STEP)
    v_hat = vn / (1.0 - ADAM_B2 ** ADAM_STEP)
    return -ADAM_LR * (m_hat / (jnp.sqrt(v_hat) + ADAM_EPS) + ADAM_WD * w), mn, vn


def _adamw(name, w, g, m, v, layer, into):
    _, k, n = w.shape
    tk, tn = _slab_block(k, n)

    def body(w_ref, g_ref, m_ref, v_ref, *rest):
        d_ref, mo_ref, vo_ref = rest[-3:]
        d_ref[...], mo_ref[...], vo_ref[...] = _adamw_math(w_ref[...], g_ref[...], m_ref[...], v_ref[...])

    spec = pl.BlockSpec((1, tk, tn), lambda j, jn: (layer, j, jn))
    extra = [] if into is None else list(into)
    return pl.pallas_call(
        body, name=name, grid=(k // tk, n // tn), in_specs=[spec] * 4 + [_ANY] * len(extra), out_specs=[spec] * 3,
        out_shape=[jax.ShapeDtypeStruct(w.shape, F32)] * 3,
        input_output_aliases={4 + i: i for i in range(len(extra))},
        compiler_params=_cparams(("parallel", "parallel")),
    )(w, g, m, v, *extra)


_ANY = pl.BlockSpec(memory_space=pl.ANY)


def _mesh_pos():
    return lax.axis_index("x"), lax.axis_index("y"), lax.axis_index("c")


def _other_chips(x, y):
    return [(1 - x, y), (x, 1 - y), (1 - x, 1 - y)]


def _remote(src, dst, send_sem, recv_sem, to):
    return pltpu.make_async_remote_copy(src_ref=src, dst_ref=dst, send_sem=send_sem, recv_sem=recv_sem,
                                        device_id=to, device_id_type=_MESH)


_HBM = pl.BlockSpec(memory_space=pltpu.HBM)
_SEM = pl.BlockSpec(memory_space=pltpu.SEMAPHORE)
_EFFECT = pltpu.SideEffectType.DATAFLOW_SIDE_EFFECTING


def _in_hbm(a):
    return pltpu.with_memory_space_constraint(a, pltpu.HBM)


def _chip_index(chip):
    return 2 * chip[0] + chip[1]


def _ag_forward(name, shards, lands, layer, have_remote):
    n = len(shards)

    def body(*refs):
        ins = refs[:n]
        outs = refs[2 * n:3 * n] if lands is not None else refs[n:2 * n]
        send_sems, recv_sems = refs[-2:]
        x, y, c = _mesh_pos()
        sibling = (x, y, 1 - c)
        chips = _other_chips(x, y)

        def copy(a, k, src, dst, to):
            return _remote(src, dst, send_sems.at[a, k], recv_sems.at[a, k], to)

        own = [copy(a, 6, ins[a], outs[a].at[2 * x + y], sibling) for a in range(n)]
        for cp in own:
            cp.start()

        @pl.when(c == layer)
        def _():
            started = []
            if not have_remote:
                for k, chip in enumerate(chips):
                    for a in range(n):
                        cp = copy(a, k, ins[a], outs[a].at[2 * x + y], (*chip, c))
                        cp.start()
                        started.append(cp)
            for k, chip in enumerate(chips):
                for a in range(n):
                    landed = outs[a].at[_chip_index(chip)]
                    if not have_remote:
                        copy(a, k, ins[a], landed, (*chip, c)).wait_recv()
                    cp = copy(a, 3 + k, landed, landed, sibling)
                    cp.start()
                    started.append(cp)
            for cp in started:
                cp.wait_send()

        @pl.when(c != layer)
        def _():
            for k, chip in enumerate(chips):
                for a in range(n):
                    copy(a, 3 + k, ins[a], outs[a].at[_chip_index(chip)], sibling).wait_recv()

        for cp in own:
            cp.wait()

    out_shape = [jax.ShapeDtypeStruct((4,) + a.shape, a.dtype) for a in shards]
    extra = [] if lands is None else list(lands)
    return pl.pallas_call(
        body, name=name, in_specs=[_ANY] * (n + len(extra)), out_specs=[_ANY] * n,
        out_shape=out_shape, input_output_aliases={n + a: a for a in range(len(extra))},
        scratch_shapes=[pltpu.SemaphoreType.DMA((n, 7)), pltpu.SemaphoreType.DMA((n, 7))],
    )(*shards, *extra)


def _owner_sends(owners, srcs, dsts, send_sems, recv_sems, do):
    x, y, c = _mesh_pos()
    for core in (0, 1):
        mine = [a for a in range(len(srcs)) if owners[a] == core]
        if mine:
            @pl.when(c == core)
            def _(mine=mine):
                for k, chip in enumerate(_other_chips(x, y)):
                    for a in mine:
                        do(_remote(srcs[a](chip, k), dsts[a](chip, k), send_sems.at[3 * a + k],
                                   recv_sems.at[3 * a + k], (*chip, c)))


def _split_start(name, owners, sources, land_shapes, src_of, dst_of, after):
    n = len(sources)

    def body(*refs):
        srcs, lands = refs[:n], refs[n:2 * n]
        send_sems, recv_sems = refs[2 * n + 1], refs[2 * n + 2]
        token = refs[-1]
        _owner_sends(owners, [functools.partial(src_of, srcs[a]) for a in range(n)],
                     [functools.partial(dst_of, lands[a]) for a in range(n)], send_sems, recv_sems,
                     lambda cp: cp.start())
        token[...] = jnp.zeros_like(token)

    lands = [_in_hbm(lax.empty(s.shape, s.dtype)) for s in land_shapes]
    outs = pl.pallas_call(
        body, name=name,
        out_shape=([pltpu.SemaphoreType.DMA((3 * n,)), pltpu.SemaphoreType.DMA((3 * n,))]
                   + [pltpu.HBM(a.shape, a.dtype) for a in sources] + [pltpu.HBM(s.shape, s.dtype) for s in land_shapes]
                   + [jax.ShapeDtypeStruct((8, _LANES), F32)]),
        in_specs=[_HBM] * (2 * n) + [_ANY],
        out_specs=[_SEM, _SEM] + [_HBM] * (2 * n) + [pl.BlockSpec(memory_space=pltpu.VMEM)],
        input_output_aliases={i: 2 + i for i in range(2 * n)},
        compiler_params=pltpu.CompilerParams(has_side_effects=_EFFECT),
    )(*[_in_hbm(a) for a in sources], *lands, after)
    return outs[0], outs[1], outs[2:2 + n], outs[2 + n:2 + 2 * n], outs[-1]


def _split_wait(name, owners, send_sems, recv_sems, sources, lands, after, src_of, dst_of):
    n = len(sources)

    def body(*refs):
        srcs, lnds = refs[:n], refs[n:2 * n]
        s_sems, r_sems = refs[2 * n], refs[2 * n + 1]

        def wait(cp):
            cp.wait_send()
            cp.wait_recv()

        _owner_sends(owners, [functools.partial(src_of, srcs[a]) for a in range(n)],
                     [functools.partial(dst_of, lnds[a]) for a in range(n)], s_sems, r_sems, wait)

    outs = pl.pallas_call(
        body, name=name,
        out_shape=[pltpu.HBM(a.shape, a.dtype) for a in sources] + [pltpu.HBM(a.shape, a.dtype) for a in lands],
        in_specs=[_HBM] * (2 * n) + [_SEM, _SEM] + [_ANY] * len(after), out_specs=[_HBM] * (2 * n),
        input_output_aliases={i: i for i in range(2 * n)},
        compiler_params=pltpu.CompilerParams(has_side_effects=_EFFECT),
    )(*sources, *lands, send_sems, recv_sems, *after)
    return outs[:n], outs[n:]


def _pair_exchange(name, arrays, owners, to_owner, layer=None):
    n = len(arrays)

    def body(*refs):
        ins, outs, (send_sems, recv_sems) = refs[:n], refs[n:2 * n], refs[2 * n:]
        x, y, c = _mesh_pos()
        part = (lambda r: r) if layer is None else (lambda r: r.at[layer])
        copies = [_remote(part(ins[a]), part(outs[a]), send_sems.at[a], recv_sems.at[a], (x, y, 1 - c))
                  for a in range(n)]
        for core in (0, 1):
            sends = [copies[a] for a in range(n) if (owners[a] != core) == to_owner]
            recvs = [copies[a] for a in range(n) if (owners[a] == core) == to_owner]

            @pl.when(c == core)
            def _(sends=sends, recvs=recvs):
                for cp in sends:
                    cp.start()
                for cp in recvs:
                    cp.wait_recv()
                for cp in sends:
                    cp.wait_send()

    return pl.pallas_call(
        body, name=name, in_specs=[_ANY] * n, out_specs=[_ANY] * n,
        out_shape=[jax.ShapeDtypeStruct(g.shape, g.dtype) for g in arrays],
        input_output_aliases={} if to_owner else {a: a for a in range(n)},
        scratch_shapes=[pltpu.SemaphoreType.DMA((n,)), pltpu.SemaphoreType.DMA((n,))],
    )(*arrays)


def _rs_scatter(name, ts, owners):
    n = len(ts)

    def body(*refs):
        ins, outs, (send_sems, recv_sems) = refs[:n], refs[n:2 * n], refs[2 * n:]
        srcs = [lambda chip, k, r=ins[a]: r.at[_chip_index(chip)] for a in range(n)]
        dsts = [lambda chip, k, r=outs[a]: r.at[k] for a in range(n)]
        _owner_sends(owners, srcs, dsts, send_sems, recv_sems, lambda cp: cp.start())
        _owner_sends(owners, srcs, dsts, send_sems, recv_sems, lambda cp: cp.wait())

    return pl.pallas_call(
        body, name=name, in_specs=[_ANY] * n, out_specs=[_ANY] * n,
        out_shape=[jax.ShapeDtypeStruct((3,) + t.shape[1:], t.dtype) for t in ts],
        scratch_shapes=[pltpu.SemaphoreType.DMA((3 * n,)), pltpu.SemaphoreType.DMA((3 * n,))],
    )(*ts)


def _add_pair(name, g, a, flags):
    _, k, n = g.shape
    tk, tn = _slab_block(k, n)

    def body(flags_ref, g_ref, a_ref, o_ref):
        o_ref[...] = (g_ref[...] + a_ref[...]).astype(o_ref.dtype)

    spec = pl.BlockSpec((1, tk, tn), lambda j, i, jn, fl: (j * fl[1], i * fl[1], jn * fl[1]))
    return pl.pallas_call(
        body, name=name,
        grid_spec=pltpu.PrefetchScalarGridSpec(num_scalar_prefetch=1, grid=(4, k // tk, n // tn),
                                               in_specs=[spec, spec], out_specs=spec),
        out_shape=jax.ShapeDtypeStruct(g.shape, _RS_DT),
        compiler_params=_cparams(("arbitrary", "arbitrary", "arbitrary")),
    )(flags, g, a)


def _add_quads(name, t, b, layer, flags, into, after=None):
    _, k, n = t.shape
    tk, tn = _slab_block(k, n)

    def body(flags_ref, t_ref, b_ref, *rest):
        o_ref = rest[-1]
        f = lambda v: v.astype(F32)
        o_ref[0] = ((f(t_ref[0]) + f(b_ref[0])) + f(b_ref[1])) + f(b_ref[2])

    extra = ([] if into is None else [into]) + ([] if after is None else [after])
    return pl.pallas_call(
        body, name=name,
        grid_spec=pltpu.PrefetchScalarGridSpec(
            num_scalar_prefetch=1, grid=(k // tk, n // tn),
            in_specs=[pl.BlockSpec((1, tk, tn), lambda i, jn, fl: (fl[0], i * fl[1], jn * fl[1])),
                      pl.BlockSpec((3, tk, tn), lambda i, jn, fl: (0, i * fl[1], jn * fl[1]))] + [_ANY] * len(extra),
            out_specs=pl.BlockSpec((1, tk, tn), lambda i, jn, fl: (layer, i * fl[1], jn * fl[1]))),
        out_shape=jax.ShapeDtypeStruct((2, k, n), F32),
        input_output_aliases={} if into is None else {3: 0},
        compiler_params=_cparams(("arbitrary", "arbitrary")),
    )(flags, t, b, *extra)


def _slab_block(k, n, itemsize=4):
    tk = (1 << 20) // (n * itemsize) // 16 * 16
    while 0 < tk < k and k % tk:
        tk -= 16
    if 0 < tk < k:
        return tk, n
    if k * n * itemsize <= (2 << 20) or n % _LANES:
        return k, n
    tn = max(_LANES, (2 << 20) // (k * itemsize) // _LANES * _LANES)
    while n % tn:
        tn -= _LANES
    return k, tn


def _all_reduce_adamw(gs, ws, ms, vs):
    n = len(gs)

    def body(*refs):
        g_refs, w_refs, m_refs, v_refs = (refs[i * n:(i + 1) * n] for i in range(4))
        gsum, delta, m_out, v_out = (refs[(4 + i) * n:(5 + i) * n] for i in range(4))
        slots = refs[8 * n:9 * n]
        send_sems, recv_sems = refs[9 * n:]
        x, y, c = _mesh_pos()
        me = 4 * x + 2 * y + c
        copies = []
        for rel in range(1, 8):
            bx, by, bc = (rel >> 2) & 1, (rel >> 1) & 1, rel & 1
            peer = (1 - x if bx else x, 1 - y if by else y, 1 - c if bc else c)
            for a in range(n):
                cp = _remote(g_refs[a], slots[a].at[me], send_sems.at[a, rel - 1], recv_sems.at[a, rel - 1], peer)
                cp.start()
                copies.append(cp)
        for a in range(n):
            slots[a][me] = g_refs[a][...]
        for cp in copies:
            cp.wait()
        for a in range(n):
            acc = slots[a][0]
            for d in range(1, 8):
                acc = acc + slots[a][d]
            gsum[a][...] = acc
            delta[a][...], m_out[a][...], v_out[a][...] = _adamw_math(w_refs[a][...], acc, m_refs[a][...],
                                                                      v_refs[a][...])

    vmem = pl.BlockSpec(memory_space=pltpu.VMEM)
    outs = pl.pallas_call(
        body, name="all_reduce_adamw", in_specs=[vmem] * (4 * n), out_specs=[vmem] * (4 * n),
        out_shape=[jax.ShapeDtypeStruct(g.shape, F32) for g in gs] * 4,
        scratch_shapes=([pltpu.VMEM((8,) + g.shape, F32) for g in gs]
                        + [pltpu.SemaphoreType.DMA((n, 7)), pltpu.SemaphoreType.DMA((n, 7))]),
        compiler_params=pltpu.CompilerParams(vmem_limit_bytes=_VMEM_LIMIT),
    )(*gs, *ws, *ms, *vs)
    return outs[:n], outs[n:2 * n], outs[2 * n:3 * n], outs[3 * n:]


def _swap_rope(a):
    h = QK_ROPE // 2
    return jnp.concatenate([a[..., h:], a[..., :h]], axis=-1)


def _swap_rope_rows(a):
    h = QK_ROPE // 2
    return jnp.concatenate([a[h:], a[:h]], axis=0)


_FFN_CB = 256


def _interleave_rows(a, cb):
    r, c = a.shape
    return a.reshape(2, r // (2 * cb), cb, c).transpose(1, 0, 2, 3).reshape(r, c)


def _deinterleave_rows(a, cb):
    r, c = a.shape
    return a.reshape(r // (2 * cb), 2, cb, c).transpose(1, 0, 2, 3).reshape(r, c)


class _InLayout:
    def __init__(self, d):
        self.d = d
        self.gates = 0
        self.a = 3 * d
        self.b = 4 * d
        self.kv = 5 * d
        self.q = self.kv + 256
        self.krm = self.q + 384
        self.krs = self.krm + HEAD_PAD
        self.width = self.krs + 2 * HEAD_PAD


def _prep_layer(wl, d):
    lay = _InLayout(d)
    w_in = wl["w_in"]
    dt = w_in.dtype
    a, b = w_in[0:d], w_in[d:2 * d]
    q, kv = w_in[2 * d:2 * d + 384], w_in[2 * d + 384:2 * d + 640]
    kr = w_in[2 * d + 640:2 * d + 640 + QK_ROPE]
    gates = w_in[2 * d + 640 + QK_ROPE:]
    z = lambda n: jnp.zeros((n, d), dt)
    krm = jnp.concatenate([z(QK_NOPE), kr, z(HEAD_PAD - QK_NOPE - QK_ROPE)], axis=0)
    krs = jnp.concatenate([z(QK_NOPE), _swap_rope_rows(kr), z(HEAD_PAD - QK_NOPE - QK_ROPE)], axis=0)
    out = dict(wl)
    out["w_in"] = jnp.concatenate([gates, _interleave_rows(a, _LANES), b, kv, q, krm, krs,
                                   z(lay.width - lay.krs - HEAD_PAD)], axis=0)
    uq = wl["mla_w_uq"].reshape(-1, N_HEADS, QK_NOPE + QK_ROPE)
    nq = uq.shape[0]
    nope, pe = uq[..., :QK_NOPE], uq[..., QK_NOPE:]
    zq = lambda n: jnp.zeros((nq, N_HEADS, n), dt)
    main = jnp.concatenate([nope, pe, zq(HEAD_PAD - QK_NOPE - QK_ROPE)], axis=-1).reshape(nq, -1)
    swapped = jnp.concatenate([zq(QK_NOPE), _swap_rope(pe), zq(HEAD_PAD - QK_NOPE - QK_ROPE)], axis=-1).reshape(nq, -1)
    out["mla_w_uq"] = jnp.concatenate([main, swapped], axis=1)
    ukv = wl["mla_w_ukv"].reshape(-1, N_HEADS, QK_NOPE + V_HEAD)
    nkv = ukv.shape[0]
    zk = jnp.zeros((nkv, N_HEADS, HEAD_PAD - QK_NOPE), dt)
    zv = jnp.zeros((nkv, N_HEADS, HEAD_PAD - V_HEAD), dt)
    out["mla_w_ukv"] = jnp.concatenate([jnp.concatenate([ukv[..., :QK_NOPE], zk], axis=-1).reshape(nkv, -1),
                                        jnp.concatenate([ukv[..., QK_NOPE:], zv], axis=-1).reshape(nkv, -1)], axis=1)
    wo = wl["mla_w_o"].reshape(N_HEADS, V_HEAD, -1)
    out["mla_w_o"] = jnp.concatenate([wo, jnp.zeros((N_HEADS, HEAD_PAD - V_HEAD, wo.shape[-1]), dt)],
                                     axis=1).reshape(N_HEADS * HEAD_PAD, -1)
    return out


def _unprep_grads(g, d):
    lay = _InLayout(d)
    gi = g["w_in"]
    kr = (gi[lay.krm + QK_NOPE:lay.krm + QK_NOPE + QK_ROPE]
          + _swap_rope_rows(gi[lay.krs + QK_NOPE:lay.krs + QK_NOPE + QK_ROPE]))
    out = dict(g)
    out["w_in"] = jnp.concatenate([_deinterleave_rows(gi[lay.a:lay.a + d], _LANES), gi[lay.b:lay.b + d],
                                   gi[lay.q:lay.q + 384], gi[lay.kv:lay.kv + 256], kr,
                                   gi[lay.gates:lay.gates + 3 * d]], axis=0)
    hw = N_HEADS * HEAD_PAD
    gq = g["mla_w_uq"]
    nq = gq.shape[0]
    main = gq[:, :hw].reshape(nq, N_HEADS, HEAD_PAD)
    swapped = gq[:, hw:].reshape(nq, N_HEADS, HEAD_PAD)
    pe = main[..., QK_NOPE:QK_NOPE + QK_ROPE] + _swap_rope(swapped[..., QK_NOPE:QK_NOPE + QK_ROPE])
    out["mla_w_uq"] = jnp.concatenate([main[..., :QK_NOPE], pe], axis=-1).reshape(nq, -1)
    gkv = g["mla_w_ukv"]
    nkv = gkv.shape[0]
    out["mla_w_ukv"] = jnp.concatenate([gkv[:, :hw].reshape(nkv, N_HEADS, HEAD_PAD)[..., :QK_NOPE],
                                        gkv[:, hw:].reshape(nkv, N_HEADS, HEAD_PAD)[..., :V_HEAD]],
                                       axis=-1).reshape(nkv, -1)
    go = g["mla_w_o"]
    out["mla_w_o"] = go.reshape(N_HEADS, HEAD_PAD, -1)[:, :V_HEAD].reshape(N_HEADS * V_HEAD, -1)
    return out


def _rope_tables(positions):
    s = positions.shape[0]
    inv = ROPE_THETA ** (-jnp.arange(0, QK_ROPE, 2, dtype=F32) / QK_ROPE)
    ang = positions.astype(F32)[:, None] * inv
    cos, sin = jnp.cos(ang), jnp.sin(ang)
    tail = jnp.zeros((s, HEAD_PAD - QK_NOPE - QK_ROPE), F32)
    tc = jnp.concatenate([jnp.ones((s, QK_NOPE), F32), cos, cos, tail], axis=1)
    ts = jnp.concatenate([jnp.zeros((s, QK_NOPE), F32), -sin, sin, tail], axis=1)
    return tc, ts


def _row(v):
    return v.reshape(1, -1)


def _layer_fwd(x, h, w, g_next, tc, ts, late_weights=None):
    d = x.shape[1]
    lay = _InLayout(d)
    cw = d // 2
    blk = lambda off, width: off // width
    p = _mm("mm_in", h, w["w_in"], tb=True)
    z1 = _glu_conv_fwd(p, blk(lay.a, 2 * _LANES), w["conv_dw_w"], _row(w["conv_dw_b"]))
    ln_a = [_row(w["conv_ln_g"]), _row(w["conv_ln_b"])]
    (z3,) = _row_fwd("ln_silu_fwd", _f_ln_silu, [(z1, cw, 0)], ln_a, [(cw, _MXU_DT)])
    ya = _mm("mm_conv_out", z3, w["conv_out_w"])
    ln_b = [_row(w["sg_ln_g"]), _row(w["sg_ln_b"])]
    u, vn = _row_fwd("sg_pre_fwd", _f_sg_pre, [(p, cw, blk(lay.b, cw)), (p, cw, blk(lay.b + cw, cw))], ln_b,
                     [(cw, F32), (cw, _MXU_DT)])
    bcol = w["sg_b"].reshape(SG_GROUPS, SG_CHUNK, 1)
    ub = _sg_mix_fwd(u, vn, w["sg_w"], bcol)
    yb = _mm("mm_sg_out", ub, w["sg_out_w"])
    (qn,) = _row_fwd("q_norm_fwd", _f_rms, [(p, 384, blk(lay.q, 384))], [_row(w["mla_q_norm_g"])], [(384, _MXU_DT)])
    (kvn,) = _row_fwd("kv_norm_fwd", _f_rms, [(p, 256, blk(lay.kv, 256))], [_row(w["mla_kv_norm_g"])],
                      [(256, _MXU_DT)])
    q2 = _mm("mm_uq", qn, w["mla_w_uq"])
    kv2 = _mm("mm_ukv", kvn, w["mla_w_ukv"])
    qf, kf, vf = _rope_fwd(q2, kv2, p, blk(lay.krm, HEAD_PAD), blk(lay.krs, HEAD_PAD), tc, ts)
    o = _attn_fwd(qf, kf, vf)
    yc = _mm("mm_o", o, w["mla_w_o"])
    gate_rows = [(p, d, 0), (p, d, 1), (p, d, 2)]
    (merged,) = _row_fwd("merge_fwd", _f_merge, gate_rows + [(ya, d, 0), (yb, d, 0), (yc, d, 0)], [], [(d, _MXU_DT)])
    if late_weights is not None:
        w = {**w, **late_weights(merged)}
    t = _mm("mm_out", merged, w["w_out"])
    x1, h2 = _row_fwd("resid_mix_fwd", _f_resid_rms_rms, [(x, d, 0), (t, d, 0)],
                      [_row(w["mix_post_g"]), _row(w["ffn_pre_g"])], [(d, F32), (d, _MXU_DT)])
    up = _mm("mm_up", h2, w["ffn_w_up"])
    act = _conv_geglu_fwd(up, w["ffn_dw_w"], _row(w["ffn_dw_b"]))
    dn = _mm("mm_down", act, w["ffn_w_down"])
    if g_next is None:
        (x2,) = _row_fwd("resid_ffn_last_fwd", _f_resid_rms, [(x1, d, 0), (dn, d, 0)], [_row(w["ffn_post_g"])],
                         [(d, F32)])
        h_next = None
    else:
        x2, h_next = _row_fwd("resid_ffn_fwd", _f_resid_rms_rms, [(x1, d, 0), (dn, d, 0)],
                              [_row(w["ffn_post_g"]), _row(g_next)], [(d, F32), (d, _MXU_DT)])
    saved = dict(x=x, h=h, p=p, z1=z1, z3=z3, ya=ya, u=u, vn=vn, ub=ub, yb=yb, qn=qn, kvn=kvn, qf=qf, kf=kf, vf=vf, o=o,
                 yc=yc, merged=merged, t=t, x1=x1, h2=h2, up=up, act=act, dn=dn, bcol=bcol)
    return x2, h_next, saved


def _layer_bwd(dx2, dh_next, w, g_next, sv, tc, ts, on_late_grads=None):
    d = dx2.shape[1]
    lay = _InLayout(d)
    cw = d // 2
    blk = lambda off, width: off // width
    lo = _MXU_DT
    g = {}
    x1, dn = sv["x1"], sv["dn"]
    if dh_next is None:
        dx1, ddn, g["ffn_post_g"] = _row_bwd("resid_ffn_last_bwd", _f_resid_rms, [(x1, d, 0), (dn, d, 0)],
                                             [_row(w["ffn_post_g"])], [(dx2, d, 0)], [(0, F32), (1, lo)], [0])
    else:
        dx1, ddn, g["ffn_post_g"], g["next_pre_g"] = _row_bwd(
            "resid_ffn_bwd", _f_resid_rms_rms, [(x1, d, 0), (dn, d, 0)], [_row(w["ffn_post_g"]), _row(g_next)],
            [(dx2, d, 0), (dh_next, d, 0)], [(0, F32), (1, lo)], [0, 1])
    dact = _mm("mm_down_dx", ddn, w["ffn_w_down"], tb=True)
    g["ffn_w_down"] = _mm("mm_down_dw", sv["act"], ddn, ta=True)
    dup, dw_halves, db_halves = _conv_geglu_bwd(sv["up"], w["ffn_dw_w"], _row(w["ffn_dw_b"]), dact)
    g["ffn_dw_w"] = jnp.concatenate([dw_halves[0], dw_halves[1]], axis=1)
    g["ffn_dw_b"] = jnp.concatenate([db_halves[0], db_halves[1]], axis=1)
    dh2 = _mm("mm_up_dx", dup, w["ffn_w_up"], tb=True, a_halves=True)
    g["ffn_w_up"] = _mm("mm_up_dw", sv["h2"], dup, ta=True, b_halves=True, out_quarters=True)
    dx, dt, g["mix_post_g"], g["ffn_pre_g"] = _row_bwd(
        "resid_mix_bwd", _f_resid_rms_rms, [(sv["x"], d, 0), (sv["t"], d, 0)],
        [_row(w["mix_post_g"]), _row(w["ffn_pre_g"])], [(dx1, d, 0), (dh2, d, 0)], [(0, F32), (1, lo)], [0, 1])
    dmerged = _mm("mm_out_dx", dt, w["w_out"], tb=True)
    g["w_out"] = _mm("mm_out_dw", sv["merged"], dt, ta=True)
    if on_late_grads is not None:
        dmerged = on_late_grads(g, dmerged)
    p = sv["p"]
    gate_rows = [(p, d, 0), (p, d, 1), (p, d, 2)]
    dp, dya, dyb, dyc = _row_bwd(
        "merge_bwd", _f_merge, gate_rows + [(sv["ya"], d, 0), (sv["yb"], d, 0), (sv["yc"], d, 0)], [],
        [(dmerged, d, 0)], [((0, 1, 2), lo), ((3,), lo), ((4,), lo), ((5,), lo)], [], place=(lay.width, 0))
    do = _mm("mm_o_dx", dyc, w["mla_w_o"], tb=True, out_dtype=lo)
    g["mla_w_o"] = _mm("mm_o_dw", sv["o"], dyc, ta=True)
    dqf, dkf, dvf = _attn_bwd(sv["qf"], sv["kf"], sv["vf"], do)
    dq2, dkv2, dp = _rope_bwd(dqf, dkf, dvf, tc, ts, dp, blk(lay.krm, 3 * HEAD_PAD))
    dkvn = _mm("mm_ukv_dx", dkv2, w["mla_w_ukv"], tb=True)
    g["mla_w_ukv"] = _mm("mm_ukv_dw", sv["kvn"], dkv2, ta=True)
    dqn = _mm("mm_uq_dx", dq2, w["mla_w_uq"], tb=True)
    g["mla_w_uq"] = _mm("mm_uq_dw", sv["qn"], dq2, ta=True)
    dp, g["mla_q_norm_g"] = _row_bwd("q_norm_bwd", _f_rms, [(p, 384, blk(lay.q, 384))], [_row(w["mla_q_norm_g"])],
                                     [(dqn, 384, 0)], [((0,), lo)], [0], place=(lay.width, blk(lay.q, 384)), into=dp)
    dp, g["mla_kv_norm_g"] = _row_bwd("kv_norm_bwd", _f_rms, [(p, 256, blk(lay.kv, 256))],
                                      [_row(w["mla_kv_norm_g"])], [(dkvn, 256, 0)], [((0,), lo)], [0],
                                      place=(lay.width, blk(lay.kv, 256)), into=dp)
    dub = _mm("mm_sg_out_dx", dyb, w["sg_out_w"], tb=True)
    g["sg_out_w"] = _mm("mm_sg_out_dw", sv["ub"], dyb, ta=True)
    du, dvn, g["sg_w"], dbcol = _sg_mix_bwd(sv["u"], sv["vn"], w["sg_w"], sv["bcol"], dub)
    g["sg_b"] = dbcol.reshape(SG_GROUPS, SG_CHUNK)
    dp, g["sg_ln_g"], g["sg_ln_b"] = _row_bwd(
        "sg_pre_bwd", _f_sg_pre, [(p, cw, blk(lay.b, cw)), (p, cw, blk(lay.b + cw, cw))],
        [_row(w["sg_ln_g"]), _row(w["sg_ln_b"])], [(du, cw, 0), (dvn, cw, 0)], [((0, 1), lo)], [0, 1],
        place=(lay.width, blk(lay.b, d)), into=dp)
    dz3 = _mm("mm_conv_out_dx", dya, w["conv_out_w"], tb=True)
    g["conv_out_w"] = _mm("mm_conv_out_dw", sv["z3"], dya, ta=True)
    dz1, g["conv_ln_g"], g["conv_ln_b"] = _row_bwd(
        "ln_silu_bwd", _f_ln_silu, [(sv["z1"], cw, 0)], [_row(w["conv_ln_g"]), _row(w["conv_ln_b"])],
        [(dz3, cw, 0)], [((0,), F32)], [0, 1])
    dp, g["conv_dw_w"], g["conv_dw_b"] = _glu_conv_bwd(p, blk(lay.a, 2 * _LANES), w["conv_dw_w"], dz1, dp)
    dh = _mm("mm_in_dx", dp, w["w_in"])
    g["w_in"] = _mm("mm_in_dw", dp, sv["h"], ta=True)
    return dx, dh, g


def _local_step(x, positions, target, layers):
    d = x.shape[1]
    tc, ts = _rope_tables(positions)
    ws = [_prep_layer(wl, d) for wl in layers]
    depth = len(ws)
    (h,) = _row_fwd("rms_first_fwd", _f_rms, [(x, d, 0)], [_row(ws[0]["mix_pre_g"])], [(d, _MXU_DT)])
    saved = []
    for l in range(depth):
        g_next = ws[l + 1]["mix_pre_g"] if l + 1 < depth else None
        x, h, sv = _layer_fwd(x, h, ws[l], g_next, tc, ts)
        saved.append(sv)
    loss, dx = _loss_head(x, target)
    grads = [None] * depth
    dh = None
    for l in reversed(range(depth)):
        g_next = ws[l + 1]["mix_pre_g"] if l + 1 < depth else None
        dx, dh, g = _layer_bwd(dx, dh, ws[l], g_next, saved[l], tc, ts)
        if "next_pre_g" in g:
            grads[l + 1]["mix_pre_g"] = g.pop("next_pre_g")
        grads[l] = g
    x0 = saved[0]["x"]
    grad_x, grads[0]["mix_pre_g"] = _row_bwd("rms_first_bwd", _f_x_rms, [(x0, d, 0)], [_row(ws[0]["mix_pre_g"])],
                                             [(dx, d, 0), (dh, d, 0)], [(0, F32)], [0])
    return loss, grad_x, [_unprep_grads(g, d) for g in grads]


_MATRICES = ("w_in", "conv_out_w", "sg_out_w", "mla_w_uq", "mla_w_ukv", "mla_w_o", "w_out", "ffn_w_up", "ffn_w_down")
_F32_GATHERED = ("conv_dw_w", "ffn_dw_w")
_RS_DT = jnp.bfloat16


_ROW_SHARDED = SHARDED_MID + ("w_in",)


_GATHERED = SHARDED + _F32_GATHERED
_RS_CORE0 = ("w_in", "ffn_w_down")
_LATE_WEIGHTS = ("w_out", "ffn_w_up", "ffn_dw_w", "ffn_w_down")


def _layer_shards(w, l):
    hi = {n: w[n][l].astype(jnp.bfloat16) for n in SHARDED}
    lo = [(w[n][l] - hi[n].astype(F32)).astype(jnp.bfloat16) for n in _F32_GATHERED]
    return [hi[n] for n in SHARDED] + lo


def _layer_weights(names, gathered):
    wl = {}
    for n, g in zip(names, gathered):
        whole = jnp.concatenate([g[j] for j in range(4)], axis=0 if n in _ROW_SHARDED else 1)
        if n in wl:
            wl[n] = wl[n].astype(F32) + whole.astype(F32)
        else:
            wl[n] = whole.astype(_MXU_DT) if n in _MATRICES else whole
    return wl


def _by_destination(name, gl):
    if gl.ndim == 3:
        return gl
    k, n = gl.shape
    if name in _ROW_SHARDED:
        return gl.reshape(4, k // 4, n)
    return gl.reshape(k, 4, n // 4).transpose(1, 0, 2)


def kernel(x, positions, mix_pre_g, mix_post_g, ffn_pre_g, ffn_post_g, w_in, conv_dw_w, conv_dw_b, conv_ln_g, conv_ln_b, conv_out_w, sg_ln_g, sg_ln_b, sg_w, sg_b, sg_out_w, mla_q_norm_g, mla_w_uq, mla_kv_norm_g, mla_w_ukv, mla_w_o, w_out, ffn_w_up, ffn_dw_w, ffn_dw_b, ffn_w_down, loss_target, m_mix_pre_g, m_mix_post_g, m_ffn_pre_g, m_ffn_post_g, m_w_in, m_conv_dw_w, m_conv_dw_b, m_conv_ln_g, m_conv_ln_b, m_conv_out_w, m_sg_ln_g, m_sg_ln_b, m_sg_w, m_sg_b, m_sg_out_w, m_mla_q_norm_g, m_mla_w_uq, m_mla_kv_norm_g, m_mla_w_ukv, m_mla_w_o, m_w_out, m_ffn_w_up, m_ffn_dw_w, m_ffn_dw_b, m_ffn_w_down, v_mix_pre_g, v_mix_post_g, v_ffn_pre_g, v_ffn_post_g, v_w_in, v_conv_dw_w, v_conv_dw_b, v_conv_ln_g, v_conv_ln_b, v_conv_out_w, v_sg_ln_g, v_sg_ln_b, v_sg_w, v_sg_b, v_sg_out_w, v_mla_q_norm_g, v_mla_w_uq, v_mla_kv_norm_g, v_mla_w_ukv, v_mla_w_o, v_w_out, v_ffn_w_up, v_ffn_dw_w, v_ffn_dw_b, v_ffn_w_down):
    args = dict(locals())
    w = {n: args[n] for n in WEIGHTS}
    m = {n: args["m_" + n] for n in WEIGHTS}
    v = {n: args["v_" + n] for n in WEIGHTS}
    depth = mix_pre_g.shape[0]

    assert depth == 2, "the two cores of a chip split the communication by layer"
    for t in (w, m, v):
        t["w_in"] = jnp.swapaxes(t["w_in"], 1, 2)
    d = x.shape[-1]
    mesh_x, mesh_y, mesh_c = _mesh_pos()
    my_chip = 2 * mesh_x + mesh_y
    names = list(SHARDED)
    whole = lambda ref, chip, k: ref
    to_my_slot = lambda ref, chip, k: ref.at[2 * lax.axis_index("x") + lax.axis_index("y")]
    block_of_chip = lambda ref, chip, k: ref.at[_chip_index(chip)]
    slot_k = lambda ref, chip, k: ref.at[k]

    late = [i for i, n in enumerate(_GATHERED) if n in _LATE_WEIGHTS]
    early = [i for i in range(len(_GATHERED)) if i not in late]
    pick = lambda seq, idx: [seq[i] for i in idx]
    gathered_names = list(_GATHERED)
    shards0, shards1 = _layer_shards(w, 0), _layer_shards(w, 1)
    replicated = lambda l: {n: w[n][l] for n in REPLICATED}
    land_of = lambda shards: [jax.ShapeDtypeStruct((4,) + a.shape, a.dtype) for a in shards]

    gathered0 = _ag_forward("ag_layer0_early", pick(shards0, early), None, 0, False)
    late0 = pick(shards0, late)
    sems0_s, sems0_r, late0, lands0, token0 = _split_start("ag0_start", [0] * len(late), late0, land_of(late0), whole,
                                                           to_my_slot, gathered0[0])
    ag_owner = [1] * len(shards1)
    sems_s, sems_r, shards1, lands1, token = _split_start("ag1_start", ag_owner, shards1, land_of(shards1), whole,
                                                          to_my_slot, gathered0[0])
    tc, ts = _rope_tables(positions[0])
    ws0 = _prep_layer({**replicated(0), **_layer_weights(pick(gathered_names, early), gathered0)}, d)

    def late_weights(merged):
        got = _split_wait("ag0_wait", [0] * len(late), sems0_s, sems0_r, late0, lands0, [merged], whole, to_my_slot)
        got = _ag_forward("ag_layer0_late", got[0], got[1], 0, True)
        ws0.update(_layer_weights(pick(gathered_names, late), got))
        return ws0

    x0 = x[0] + (token0[0, 0] + token[0, 0])
    (h0,) = _row_fwd("rms_first_fwd", _f_rms, [(x0, d, 0)], [_row(ws0["mix_pre_g"])], [(d, _MXU_DT)])
    x1, h1, sv0 = _layer_fwd(x0, h0, ws0, w["mix_pre_g"][1], tc, ts, late_weights)
    shards1, lands1 = _split_wait("ag1_wait", ag_owner, sems_s, sems_r, shards1, lands1, [x1], whole, to_my_slot)
    gathered1 = _ag_forward("ag_layer1", shards1, lands1, 1, True)
    ws1 = _prep_layer({**replicated(1), **_layer_weights(gathered_names, gathered1)}, d)
    x2, _, sv1 = _layer_fwd(x1, h1, ws1, None, tc, ts)
    loss, dx = _loss_head(x2, loss_target[0])
    loss = lax.psum(loss[0, 0], ("x", "y", "c"))

    owner = {n: 0 if n in _RS_CORE0 else 1 for n in names}
    flag = {n: jnp.stack([my_chip, (mesh_c == owner[n]).astype(jnp.int32)]).astype(jnp.int32) for n in names}
    owners = [owner[n] for n in names]

    def scatter_start(tag, group, grads_l, after):
        own = [owner[n] for n in group]
        gd = [_by_destination(n, grads_l[n]) for n in group]
        got = _pair_exchange("rs_swap" + tag, gd, own, True)
        t = [_add_pair("rs_pair%s_%s" % (tag, n), g, a, flag[n]) for n, g, a in zip(group, gd, got)]
        lands = [jax.ShapeDtypeStruct((3,) + a.shape[1:], a.dtype) for a in t]
        sems_s, sems_r, t, b, token = _split_start("rs%s_start" % tag, own, t, lands, block_of_chip, slot_k, after)
        return (tag, group, own, sems_s, sems_r, t, b), token

    def scatter_wait(handle, after):
        tag, group, own, sems_s, sems_r, t, b = handle
        t, b = _split_wait("rs%s_wait" % tag, own, sems_s, sems_r, t, b, after, block_of_chip, slot_k)
        return dict(zip(group, zip(t, b)))

    dx, dh, gk1 = _layer_bwd(dx, None, ws1, None, sv1, tc, ts)
    grads1 = _unprep_grads(gk1, d)
    handle1, token = scatter_start("1", names, grads1, dh)
    dx = dx + token[0, 0]
    late_group = [n for n in names if n in _LATE_WEIGHTS]
    early_group = [n for n in names if n not in _LATE_WEIGHTS]
    handles0 = []

    def on_late_grads(g, value):
        handle, tok = scatter_start("0_late", late_group, g, value)
        handles0.append(handle)
        return value + tok[0, 0]

    dx, dh, gk0 = _layer_bwd(dx, dh, ws0, ws1["mix_pre_g"], sv0, tc, ts, on_late_grads)
    grads1["mix_pre_g"] = gk0.pop("next_pre_g")
    grad_x, gk0["mix_pre_g"] = _row_bwd("rms_first_bwd", _f_x_rms, [(x0, d, 0)], [_row(ws0["mix_pre_g"])],
                                        [(dx, d, 0), (dh, d, 0)], [(0, F32)], [0])
    tb1 = scatter_wait(handle1, [grad_x])
    grads0 = _unprep_grads(gk0, d)
    handle, token = scatter_start("0_early", early_group, grads0, grad_x)
    handles0.append(handle)

    def finish(l, sums, updates):
        sums = _pair_exchange("rs_join%d" % l, sums, owners, False, layer=l)
        updates = [_adamw("adamw%d_%s" % (l, n), w[n], sums[i], m[n], v[n], l, updates[i])
                   for i, n in enumerate(names)]
        return sums, updates

    sums = [_add_quads("rs_sum1_" + n, *tb1[n], 1, flag[n], None, token) for n in names]
    sums, updates = finish(1, sums, [None] * len(names))
    out = {}
    rep = list(REPLICATED)
    grads = [grads0, grads1]
    g_rep = [jnp.stack([grads[l][n].reshape(w[n].shape[1:]) for l in range(depth)]) for n in rep]
    for n, *res in zip(rep, *_all_reduce_adamw(g_rep, [w[n] for n in rep], [m[n] for n in rep], [v[n] for n in rep])):
        out[n] = tuple(res)
    hidden = [u[0] for u in updates] + [out[rep[0]][0]]
    tb0 = {**scatter_wait(handles0[0], hidden), **scatter_wait(handles0[1], hidden)}
    sums = [_add_quads("rs_sum0_" + n, *tb0[n], 0, flag[n], sums[i]) for i, n in enumerate(names)]
    sums, updates = finish(0, sums, updates)
    for n, gr, upd in zip(names, sums, updates):
        out[n] = (gr, *upd)
    out["w_in"] = tuple(jnp.swapaxes(a, 1, 2) for a in out["w_in"])
    return (loss, grad_x[None], *[out[n][i] for i in range(4) for n in WEIGHTS])
```

```python
import functools
import math

import jax
import jax.numpy as jnp
from jax import lax
from jax.experimental import pallas as pl
from jax.experimental.pallas import tpu as pltpu

F32 = jnp.float32
_MXU_DT = jnp.bfloat16
_VMEM_LIMIT = 48 * 1024 * 1024
_LANES = 128
_MESH = pl.DeviceIdType.MESH

N_HEADS = 8
QK_NOPE = 64
QK_ROPE = 32
V_HEAD = 64
HEAD_PAD = 128
SG_GROUPS = 4
SG_CHUNK = 128
CONV_K = 31
FFN_K = 3
ROPE_THETA = 10000.0
EPS = 1e-6
ADAM_LR, ADAM_B1, ADAM_B2, ADAM_EPS, ADAM_WD, ADAM_STEP = 0.001, 0.9, 0.999, 1e-08, 0.01, 10

SHARDED_LAST = ("w_in", "conv_dw_w", "conv_out_w", "sg_out_w", "mla_w_uq", "mla_w_ukv", "mla_w_o", "ffn_w_up",
                "ffn_dw_w")
SHARDED_MID = ("w_out", "ffn_w_down")
SHARDED = SHARDED_LAST + SHARDED_MID
WEIGHTS = ("mix_pre_g", "mix_post_g", "ffn_pre_g", "ffn_post_g", "w_in", "conv_dw_w", "conv_dw_b", "conv_ln_g",
           "conv_ln_b", "conv_out_w", "sg_ln_g", "sg_ln_b", "sg_w", "sg_b", "sg_out_w", "mla_q_norm_g", "mla_w_uq",
           "mla_kv_norm_g", "mla_w_ukv", "mla_w_o", "w_out", "ffn_w_up", "ffn_dw_w", "ffn_dw_b", "ffn_w_down")
REPLICATED = tuple(n for n in WEIGHTS if n not in SHARDED)


def _cparams(sem=None):
    return pltpu.CompilerParams(dimension_semantics=sem, vmem_limit_bytes=_VMEM_LIMIT)


def _pick(n, cands):
    for c in cands:
        if n % c == 0:
            return c
    return n


def _largest_tile(dim, cap):
    for t in range(min(cap, dim) // _LANES * _LANES, 0, -_LANES):
        if dim % t == 0:
            return t
    return dim


_MM_VMEM_BUDGET = 36 * 1024 * 1024
_MM_TM_CAP, _MM_TN_CAP, _MM_TK_CAP = 1024, 1536, 3072


def _mm(name, a, b, *, ta=False, tb=False, out_dtype=F32, a_halves=False, b_halves=False, out_quarters=False):
    assert not (a_halves and ta) and not (b_halves and tb)
    if a_halves:
        m, kdim = a.shape[1], 2 * a.shape[2]
    else:
        (kdim, m) = a.shape if ta else a.shape[::-1]
    if b_halves:
        kdim2, n = b.shape[1], 2 * b.shape[2]
    else:
        (n, kdim2) = b.shape if tb else b.shape[::-1]
    assert kdim == kdim2, (a.shape, b.shape, ta, tb)
    tk = _largest_tile(kdim // 2 if a_halves else kdim, _MM_TK_CAP)
    tn = _largest_tile(n // 4 if out_quarters else (n // 2 if b_halves else n), _MM_TN_CAP)
    nk = kdim // tk
    ab, bb, ob = a.dtype.itemsize, b.dtype.itemsize, jnp.dtype(out_dtype).itemsize
    tm = _largest_tile(m, _MM_TM_CAP)
    vmem = lambda t: 2 * (t * tk * ab + tk * tn * bb + t * tn * ob) + (t * tn * 4 if nk > 1 else 0)
    while vmem(tm) > _MM_VMEM_BUDGET and tm > _LANES:
        tm = _largest_tile(m, tm - _LANES)
    dims = (((0 if ta else 1,), (1 if tb else 0,)), ((), ()))

    def dot(a_ref, b_ref):
        return lax.dot_general(a_ref[...].astype(_MXU_DT), b_ref[...].astype(_MXU_DT), dims,
                               preferred_element_type=F32)

    def body_one(a_ref, b_ref, o_ref):
        o_ref[...] = dot(a_ref, b_ref).astype(o_ref.dtype)

    def body_acc(a_ref, b_ref, o_ref, acc_ref):
        k = pl.program_id(2)

        @pl.when(k == 0)
        def _():
            acc_ref[...] = jnp.zeros_like(acc_ref)

        acc_ref[...] += dot(a_ref, b_ref)

        @pl.when(k == nk - 1)
        def _():
            o_ref[...] = acc_ref[...].astype(o_ref.dtype)

    if a_halves:
        per = nk // 2
        a_spec = pl.BlockSpec((None, tm, tk), lambda i, j, k: (k // per, i, k % per))
    elif ta:
        a_spec = pl.BlockSpec((tk, tm), lambda i, j, k: (k, i))
    else:
        a_spec = pl.BlockSpec((tm, tk), lambda i, j, k: (i, k))
    if b_halves:
        per_b = n // 2 // tn
        b_spec = pl.BlockSpec((None, tk, tn), lambda i, j, k: (j // per_b, k, j % per_b))
    elif tb:
        b_spec = pl.BlockSpec((tn, tk), lambda i, j, k: (j, k))
    else:
        b_spec = pl.BlockSpec((tk, tn), lambda i, j, k: (k, j))
    if out_quarters:
        per_o = n // 4 // tn
        o_spec = pl.BlockSpec((None, tm, tn), lambda i, j, k: (j // per_o, i, j % per_o))
        o_shape = jax.ShapeDtypeStruct((4, m, n // 4), out_dtype)
    else:
        o_spec = pl.BlockSpec((tm, tn), lambda i, j, k: (i, j))
        o_shape = jax.ShapeDtypeStruct((m, n), out_dtype)
    return pl.pallas_call(
        body_one if nk == 1 else body_acc, name=name, grid=(m // tm, n // tn, nk),
        in_specs=[a_spec, b_spec], out_specs=o_spec, out_shape=o_shape,
        scratch_shapes=[] if nk == 1 else [pltpu.VMEM((tm, tn), F32)],
        compiler_params=_cparams(("parallel", "parallel", "arbitrary")),
    )(a, b)


def _row_spec(tm, width, idx):
    return pl.BlockSpec((tm, width), lambda i: (i, idx))


def _full_spec(shape):
    zeros = (0,) * len(shape)
    return pl.BlockSpec(shape, lambda i: zeros)


def _row_fwd(name, fn, rows, params, outs, tm=256):
    s = rows[0][0].shape[0]
    nr, npar = len(rows), len(params)

    def body(*refs):
        vals = [r[...].astype(F32) for r in refs[:nr + npar]]
        res = fn(*vals)
        for o_ref, r in zip(refs[nr + npar:], res):
            o_ref[...] = r.astype(o_ref.dtype)

    return pl.pallas_call(
        body, name=name, grid=(s // tm,),
        in_specs=[_row_spec(tm, w, i) for _, w, i in rows] + [_full_spec(p.shape) for p in params],
        out_specs=[_row_spec(tm, w, 0) for w, _ in outs],
        out_shape=[jax.ShapeDtypeStruct((s, w), dt) for w, dt in outs],
        compiler_params=_cparams(("parallel",)),
    )(*[r[0] for r in rows], *params)


def _row_bwd(name, fn, rows, params, cots, row_grads, param_grads, tm=256, place=None, into=None):
    s = rows[0][0].shape[0]
    nr, npar, nc = len(rows), len(params), len(cots)
    row_grads = [((idxs,) if isinstance(idxs, int) else tuple(idxs), dt) for idxs, dt in row_grads]
    widths = [sum(rows[i][1] for i in idxs) for idxs, _ in row_grads]

    def body(*refs):
        i = pl.program_id(0)
        vals = [r[...].astype(F32) for r in refs[:nr + npar]]
        cvals = tuple(r[...].astype(F32) for r in refs[nr + npar:nr + npar + nc])
        _, vjp = jax.vjp(fn, *vals)
        grads = vjp(cvals)
        outs = refs[nr + npar + nc + (into is not None):]
        for o_ref, (idxs, _) in zip(outs, row_grads):
            pos = 0
            for idx in idxs:
                o_ref[:, pos:pos + rows[idx][1]] = grads[idx].astype(o_ref.dtype)
                pos += rows[idx][1]
        for o_ref, idx in zip(outs[len(row_grads):], param_grads):
            @pl.when(i == 0)
            def _(o_ref=o_ref):
                o_ref[...] = jnp.zeros_like(o_ref)

            o_ref[...] += grads[nr + idx]

    out_specs = [_row_spec(tm, w, 0) for w in widths] + [_full_spec(params[idx].shape) for idx in param_grads]
    out_shape = ([jax.ShapeDtypeStruct((s, w), dt) for w, (_, dt) in zip(widths, row_grads)]
                 + [jax.ShapeDtypeStruct(params[idx].shape, F32) for idx in param_grads])
    extra, aliases = [], {}
    if place is not None:
        out_specs[0] = _row_spec(tm, widths[0], place[1])
        out_shape[0] = jax.ShapeDtypeStruct((s, place[0]), row_grads[0][1])
    if into is not None:
        extra, aliases = [into], {nr + npar + nc: 0}
    return pl.pallas_call(
        body, name=name, grid=(s // tm,),
        in_specs=([_row_spec(tm, w, i) for _, w, i in rows] + [_full_spec(p.shape) for p in params]
                  + [_row_spec(tm, w, i) for _, w, i in cots] + [_ANY] * len(extra)),
        out_specs=out_specs, out_shape=out_shape, input_output_aliases=aliases,
        compiler_params=_cparams(("arbitrary",)),
    )(*[r[0] for r in rows], *params, *[c[0] for c in cots], *extra)


def _rms(x, g):
    return x * lax.rsqrt(jnp.mean(x * x, axis=-1, keepdims=True) + EPS) * g


def _ln(x, g, b):
    mu = jnp.mean(x, axis=-1, keepdims=True)
    xc = x - mu
    var = jnp.mean(xc * xc, axis=-1, keepdims=True)
    return xc * lax.rsqrt(var + EPS) * g + b


def _sigmoid(x):
    return 1.0 / (1.0 + jnp.exp(-x))


def _gelu(x):
    return x * (0.5 * (1.0 + jnp.tanh(math.sqrt(2.0 / math.pi) * (x + 0.044715 * (x * x * x)))))


def _f_rms(x, g):
    return (_rms(x, g),)


def _f_x_rms(x, g):
    return (x, _rms(x, g))


def _f_ln_silu(z, g, b):
    y = _ln(z, g, b)
    return (y * _sigmoid(y),)


def _f_sg_pre(bu, bv, g, b):
    return (_gelu(bu), _ln(_gelu(bv), g, b))


def _f_merge(g0, g1, g2, ya, yb, yc):
    return (_sigmoid(g0) * ya + _sigmoid(g1) * yb + _sigmoid(g2) * yc,)


def _f_resid_rms(x, t, g_post):
    return (x + _rms(t, g_post),)


def _f_resid_rms_rms(x, t, g_post, g_next):
    x1 = x + _rms(t, g_post)
    return (x1, _rms(x1, g_next))


def _f_geglu(zg, zv):
    return _gelu(zg) * zv


_CONV_TILE_ELEMS = 16 * 1024


def _conv_tr(c):
    return _CONV_TILE_ELEMS // c


def _conv_tile(zp_ref, w_ref, bias, k_taps, off, r0):
    c = zp_ref.shape[1]
    tr = _conv_tr(c)
    acc = jnp.broadcast_to(bias, (tr, c))
    for k in range(k_taps):
        acc = acc + w_ref[k:k + 1, :] * zp_ref[r0 + off + k:r0 + off + k + tr, :]
    return acc


def _conv_bwd_input_tile(dzp_ref, w_ref, k_taps, r0):
    c = dzp_ref.shape[1]
    tr = _conv_tr(c)
    acc = jnp.zeros((tr, c), F32)
    for k in range(k_taps):
        s0 = r0 + (k_taps - 1) - k
        acc = acc + w_ref[k:k + 1, :] * dzp_ref[s0:s0 + tr, :]
    return acc


def _conv_bwd_weight(dzp_ref, zp_ref, dw_ref, db_ref, k_taps, off, s):
    c = zp_ref.shape[1]
    tr = _conv_tr(c)
    fold = lambda v: jnp.sum(v.reshape(tr // 8, 8, c), axis=0)
    for k in range(k_taps):
        acc = jnp.zeros((8, c), F32)
        for r in range(s // tr):
            r0 = r * tr
            acc = acc + fold(dzp_ref[r0:r0 + tr, :] * zp_ref[r0 + off + k:r0 + off + k + tr, :])
        dw_ref[k:k + 1, :] = jnp.sum(acc, axis=0, keepdims=True)
    acc = jnp.zeros((8, c), F32)
    for r in range(s // tr):
        acc = acc + fold(dzp_ref[r * tr:(r + 1) * tr, :])
    db_ref[...] = jnp.sum(acc, axis=0, keepdims=True)


def _glu_conv_fwd(p, blk0, w, b):
    s = p.shape[0]
    k_taps, c = w.shape
    cb, pad = _LANES, 32
    off = pad - (k_taps - 1)

    def body(a_ref, w_ref, b_ref, o_ref, zp_ref):
        zp_ref[0:pad, :] = jnp.zeros((pad, cb), F32)
        zp_ref[pad:pad + s, :] = a_ref[:, 0:cb] * _sigmoid(a_ref[:, cb:2 * cb])
        tr = _conv_tr(cb)
        for r in range(s // tr):
            o_ref[r * tr:(r + 1) * tr, :] = _conv_tile(zp_ref, w_ref, b_ref[...], k_taps, off, r * tr)

    return pl.pallas_call(
        body, name="glu_conv_fwd", grid=(c // cb,),
        in_specs=[pl.BlockSpec((s, 2 * cb), lambda j: (0, blk0 + j)),
                  pl.BlockSpec((k_taps, cb), lambda j: (0, j)), pl.BlockSpec((1, cb), lambda j: (0, j))],
        out_specs=pl.BlockSpec((s, cb), lambda j: (0, j)),
        out_shape=jax.ShapeDtypeStruct((s, c), F32),
        scratch_shapes=[pltpu.VMEM((s + pad, cb), F32)],
        compiler_params=_cparams(("parallel",)),
    )(p, w, b)


def _glu_conv_bwd(p, blk0, w, dz, dp):
    s = p.shape[0]
    k_taps, c = w.shape
    cb, pad = _LANES, 32
    off = pad - (k_taps - 1)

    def body(a_ref, w_ref, dz_ref, dp_in, da_ref, dw_ref, db_ref, zp_ref, dzp_ref):
        zp_ref[0:pad, :] = jnp.zeros((pad, cb), F32)
        zp_ref[pad:pad + s, :] = a_ref[:, 0:cb] * _sigmoid(a_ref[:, cb:2 * cb])
        dzp_ref[0:s, :] = dz_ref[...]
        dzp_ref[s:s + pad, :] = jnp.zeros((pad, cb), F32)
        tr = _conv_tr(cb)
        for r in range(s // tr):
            rows = slice(r * tr, (r + 1) * tr)
            dz0 = _conv_bwd_input_tile(dzp_ref, w_ref, k_taps, r * tr)
            sg = _sigmoid(a_ref[rows, cb:2 * cb])
            da_ref[rows, 0:cb] = (dz0 * sg).astype(da_ref.dtype)
            da_ref[rows, cb:2 * cb] = (dz0 * a_ref[rows, 0:cb] * sg * (1.0 - sg)).astype(da_ref.dtype)
        _conv_bwd_weight(dzp_ref, zp_ref, dw_ref, db_ref, k_taps, off, s)

    return pl.pallas_call(
        body, name="glu_conv_bwd", grid=(c // cb,),
        in_specs=[pl.BlockSpec((s, 2 * cb), lambda j: (0, blk0 + j)),
                  pl.BlockSpec((k_taps, cb), lambda j: (0, j)), pl.BlockSpec((s, cb), lambda j: (0, j)), _ANY],
        out_specs=[pl.BlockSpec((s, 2 * cb), lambda j: (0, blk0 + j)),
                   pl.BlockSpec((k_taps, cb), lambda j: (0, j)), pl.BlockSpec((1, cb), lambda j: (0, j))],
        out_shape=[jax.ShapeDtypeStruct(dp.shape, dp.dtype),
                   jax.ShapeDtypeStruct((k_taps, c), F32), jax.ShapeDtypeStruct((1, c), F32)],
        scratch_shapes=[pltpu.VMEM((s + pad, cb), F32), pltpu.VMEM((s + pad, cb), F32)],
        input_output_aliases={3: 0},
        compiler_params=_cparams(("parallel",)),
    )(p, w, dz, dp)


def _conv_geglu_fwd(up, w, b):
    s, f2 = up.shape
    f = f2 // 2
    k_taps = w.shape[0]
    cb, pad = _FFN_CB, 8
    off = pad - (k_taps - 1)
    nb = f // cb

    def body(ug_ref, uv_ref, wg_ref, wv_ref, bg_ref, bv_ref, o_ref, z_ref, w_ref, b_ref):
        _pair(w_ref, wg_ref[...], wv_ref[...], cb)
        _pair(b_ref, bg_ref[...], bv_ref[...], cb)
        z_ref[0:pad, :] = jnp.zeros((pad, 2 * cb), F32)
        z_ref[pad:pad + s, 0:cb] = ug_ref[...]
        z_ref[pad:pad + s, cb:2 * cb] = uv_ref[...]
        tr = _conv_tr(2 * cb)
        for r in range(s // tr):
            z = _conv_tile(z_ref, w_ref, b_ref[...], k_taps, off, r * tr)
            o_ref[r * tr:(r + 1) * tr, :] = _f_geglu(z[:, 0:cb], z[:, cb:2 * cb]).astype(o_ref.dtype)

    two = lambda rows_: [pl.BlockSpec((rows_, cb), lambda j: (0, j)), pl.BlockSpec((rows_, cb), lambda j: (0, nb + j))]
    return pl.pallas_call(
        body, name="conv_geglu_fwd", grid=(nb,),
        in_specs=two(s) + two(k_taps) + two(1),
        out_specs=pl.BlockSpec((s, cb), lambda j: (0, j)),
        out_shape=jax.ShapeDtypeStruct((s, f), _MXU_DT),
        scratch_shapes=[pltpu.VMEM((s + pad, 2 * cb), F32), pltpu.VMEM((k_taps, 2 * cb), F32),
                        pltpu.VMEM((1, 2 * cb), F32)],
        compiler_params=_cparams(("parallel",)),
    )(up, up, w, w, b, b)


def _pair(dst_ref, first, second, cb):
    dst_ref[:, 0:cb] = first
    dst_ref[:, cb:2 * cb] = second


def _conv_geglu_bwd(up, w, b, dact):
    s, f2 = up.shape
    f = f2 // 2
    k_taps = w.shape[0]
    cb, pad = _FFN_CB, 8
    off = pad - (k_taps - 1)
    nb = f // cb

    def body(ug_ref, uv_ref, wg_ref, wv_ref, bg_ref, bv_ref, da_ref, du_ref, dw_ref, db_ref, z_ref, dz_ref, w_ref,
             b_ref, dw_sc, db_sc):
        _pair(w_ref, wg_ref[...], wv_ref[...], cb)
        _pair(b_ref, bg_ref[...], bv_ref[...], cb)
        z_ref[0:pad, :] = jnp.zeros((pad, 2 * cb), F32)
        z_ref[pad:pad + s, 0:cb] = ug_ref[...]
        z_ref[pad:pad + s, cb:2 * cb] = uv_ref[...]
        dz_ref[s:s + pad, :] = jnp.zeros((pad, 2 * cb), F32)
        tr = _conv_tr(2 * cb)
        for r in range(s // tr):
            rows = slice(r * tr, (r + 1) * tr)
            z = _conv_tile(z_ref, w_ref, b_ref[...], k_taps, off, r * tr)
            _, vjp = jax.vjp(_f_geglu, z[:, 0:cb], z[:, cb:2 * cb])
            dzg, dzv = vjp(da_ref[rows, :].astype(F32))
            dz_ref[rows, 0:cb] = dzg
            dz_ref[rows, cb:2 * cb] = dzv
        for r in range(s // tr):
            rows = slice(r * tr, (r + 1) * tr)
            du = _conv_bwd_input_tile(dz_ref, w_ref, k_taps, r * tr).astype(du_ref.dtype)
            du_ref[0, rows, :] = du[:, 0:cb]
            du_ref[1, rows, :] = du[:, cb:2 * cb]
        _conv_bwd_weight(dz_ref, z_ref, dw_sc, db_sc, k_taps, off, s)
        for half in range(2):
            dw_ref[half] = dw_sc[:, half * cb:(half + 1) * cb]
            db_ref[half] = db_sc[:, half * cb:(half + 1) * cb]

    two = lambda rows_: [pl.BlockSpec((rows_, cb), lambda j: (0, j)), pl.BlockSpec((rows_, cb), lambda j: (0, nb + j))]
    both = lambda rows_: pl.BlockSpec((2, rows_, cb), lambda j: (0, 0, j))
    return pl.pallas_call(
        body, name="conv_geglu_bwd", grid=(nb,),
        in_specs=two(s) + two(k_taps) + two(1) + [pl.BlockSpec((s, cb), lambda j: (0, j))],
        out_specs=[both(s), both(k_taps), both(1)],
        out_shape=[jax.ShapeDtypeStruct((2, s, f), _MXU_DT), jax.ShapeDtypeStruct((2, k_taps, f), F32),
                   jax.ShapeDtypeStruct((2, 1, f), F32)],
        scratch_shapes=[pltpu.VMEM((s + pad, 2 * cb), F32), pltpu.VMEM((s + pad, 2 * cb), F32),
                        pltpu.VMEM((k_taps, 2 * cb), F32), pltpu.VMEM((1, 2 * cb), F32),
                        pltpu.VMEM((k_taps, 2 * cb), F32), pltpu.VMEM((1, 2 * cb), F32)],
        compiler_params=_cparams(("parallel",)),
    )(up, up, w, w, b, b, dact)


def _tril_mask():
    t = lax.broadcasted_iota(jnp.int32, (SG_CHUNK, SG_CHUNK), 0)
    s = lax.broadcasted_iota(jnp.int32, (SG_CHUNK, SG_CHUNK), 1)
    return t >= s


def _sg_mix_fwd(u, vn, w, bcol):
    s, c = u.shape
    gw = c // SG_GROUPS

    def body(u_ref, v_ref, w_ref, b_ref, o_ref):
        wm = jnp.where(_tril_mask(), w_ref[0], 0.0).astype(_MXU_DT)
        for n in range(s // SG_CHUNK):
            rows = slice(n * SG_CHUNK, (n + 1) * SG_CHUNK)
            mixed = jnp.dot(wm, v_ref[rows, :], preferred_element_type=F32) + b_ref[0]
            o_ref[rows, :] = (u_ref[rows, :] * mixed).astype(o_ref.dtype)

    return pl.pallas_call(
        body, name="sg_mix_fwd", grid=(SG_GROUPS,),
        in_specs=[pl.BlockSpec((s, gw), lambda g: (0, g)), pl.BlockSpec((s, gw), lambda g: (0, g)),
                  pl.BlockSpec((1, SG_CHUNK, SG_CHUNK), lambda g: (g, 0, 0)),
                  pl.BlockSpec((1, SG_CHUNK, 1), lambda g: (g, 0, 0))],
        out_specs=pl.BlockSpec((s, gw), lambda g: (0, g)),
        out_shape=jax.ShapeDtypeStruct((s, c), _MXU_DT),
        compiler_params=_cparams(("parallel",)),
    )(u, vn, w, bcol)


def _sg_mix_bwd(u, vn, w, bcol, dub):
    s, c = u.shape
    gw = c // SG_GROUPS

    def body(u_ref, v_ref, w_ref, b_ref, d_ref, du_ref, dv_ref, dw_ref, db_ref):
        mask = _tril_mask()
        wm = jnp.where(mask, w_ref[0], 0.0).astype(_MXU_DT)
        dw = jnp.zeros((SG_CHUNK, SG_CHUNK), F32)
        db = jnp.zeros((SG_CHUNK, 1), F32)
        for n in range(s // SG_CHUNK):
            rows = slice(n * SG_CHUNK, (n + 1) * SG_CHUNK)
            v = v_ref[rows, :]
            d = d_ref[rows, :].astype(F32)
            mixed = jnp.dot(wm, v, preferred_element_type=F32) + b_ref[0]
            du_ref[rows, :] = d * mixed
            dmix = d * u_ref[rows, :]
            dmix_lo = dmix.astype(_MXU_DT)
            dv_ref[rows, :] = lax.dot_general(wm, dmix_lo, (((0,), (0,)), ((), ())), preferred_element_type=F32)
            dw = dw + lax.dot_general(dmix_lo, v, (((1,), (1,)), ((), ())), preferred_element_type=F32)
            db = db + jnp.sum(dmix, axis=1, keepdims=True)
        dw_ref[0] = jnp.where(mask, dw, 0.0)
        db_ref[0] = db

    return pl.pallas_call(
        body, name="sg_mix_bwd", grid=(SG_GROUPS,),
        in_specs=[pl.BlockSpec((s, gw), lambda g: (0, g)), pl.BlockSpec((s, gw), lambda g: (0, g)),
                  pl.BlockSpec((1, SG_CHUNK, SG_CHUNK), lambda g: (g, 0, 0)),
                  pl.BlockSpec((1, SG_CHUNK, 1), lambda g: (g, 0, 0)), pl.BlockSpec((s, gw), lambda g: (0, g))],
        out_specs=[pl.BlockSpec((s, gw), lambda g: (0, g)), pl.BlockSpec((s, gw), lambda g: (0, g)),
                   pl.BlockSpec((1, SG_CHUNK, SG_CHUNK), lambda g: (g, 0, 0)),
                   pl.BlockSpec((1, SG_CHUNK, 1), lambda g: (g, 0, 0))],
        out_shape=[jax.ShapeDtypeStruct((s, c), F32), jax.ShapeDtypeStruct((s, c), F32),
                   jax.ShapeDtypeStruct((SG_GROUPS, SG_CHUNK, SG_CHUNK), F32),
                   jax.ShapeDtypeStruct((SG_GROUPS, SG_CHUNK, 1), F32)],
        compiler_params=_cparams(("parallel",)),
    )(u, vn, w, bcol, dub)


def _rope_fwd(q2, kv2, p, krm_idx, krs_idx, tc, ts):
    s = q2.shape[0]
    hw = N_HEADS * HEAD_PAD
    tm = 256

    def body(qm_ref, qs_ref, kn_ref, v_ref, krm_ref, krs_ref, tc_ref, ts_ref, q_ref, k_ref, vo_ref):
        tcv, tsv = tc_ref[...], ts_ref[...]
        kpe = krm_ref[...] * tcv + krs_ref[...] * tsv
        for h in range(N_HEADS):
            cols = slice(h * HEAD_PAD, (h + 1) * HEAD_PAD)
            q_ref[:, cols] = (qm_ref[:, cols] * tcv + qs_ref[:, cols] * tsv).astype(q_ref.dtype)
            k_ref[:, cols] = (kn_ref[:, cols] + kpe).astype(k_ref.dtype)
        vo_ref[...] = v_ref[...].astype(vo_ref.dtype)

    return pl.pallas_call(
        body, name="rope_fwd", grid=(s // tm,),
        in_specs=[_row_spec(tm, hw, 0), _row_spec(tm, hw, 1), _row_spec(tm, hw, 0), _row_spec(tm, hw, 1),
                  _row_spec(tm, HEAD_PAD, krm_idx), _row_spec(tm, HEAD_PAD, krs_idx),
                  _row_spec(tm, HEAD_PAD, 0), _row_spec(tm, HEAD_PAD, 0)],
        out_specs=[_row_spec(tm, hw, 0)] * 3,
        out_shape=[jax.ShapeDtypeStruct((s, hw), _MXU_DT)] * 3,
        compiler_params=_cparams(("parallel",)),
    )(q2, q2, kv2, kv2, p, p, tc, ts)


def _rope_bwd(dq, dk, dv, tc, ts, dp, kr_blk):
    s = dq.shape[0]
    hw = N_HEADS * HEAD_PAD
    tm = 256

    def body(dq_ref, dk_ref, dv_ref, tc_ref, ts_ref, dp_in, dq2_ref, dkv2_ref, dkr_ref):
        tcv, tsv = tc_ref[...], ts_ref[...]
        dkpe = jnp.zeros((tm, HEAD_PAD), F32)
        for h in range(N_HEADS):
            cols = slice(h * HEAD_PAD, (h + 1) * HEAD_PAD)
            dqh = dq_ref[:, cols]
            dq2_ref[:, cols] = (dqh * tcv).astype(dq2_ref.dtype)
            dq2_ref[:, hw + h * HEAD_PAD:hw + (h + 1) * HEAD_PAD] = (dqh * tsv).astype(dq2_ref.dtype)
            dkpe = dkpe + dk_ref[:, cols]
        dkv2_ref[:, 0:hw] = dk_ref[...].astype(dkv2_ref.dtype)
        dkv2_ref[:, hw:2 * hw] = dv_ref[...].astype(dkv2_ref.dtype)
        dkr_ref[:, 0:HEAD_PAD] = (dkpe * tcv).astype(dkr_ref.dtype)
        dkr_ref[:, HEAD_PAD:2 * HEAD_PAD] = (dkpe * tsv).astype(dkr_ref.dtype)
        dkr_ref[:, 2 * HEAD_PAD:3 * HEAD_PAD] = jnp.zeros((tm, HEAD_PAD), dkr_ref.dtype)

    return pl.pallas_call(
        body, name="rope_bwd", grid=(s // tm,),
        in_specs=[_row_spec(tm, hw, 0)] * 3 + [_row_spec(tm, HEAD_PAD, 0)] * 2 + [_ANY],
        out_specs=[_row_spec(tm, 2 * hw, 0), _row_spec(tm, 2 * hw, 0), _row_spec(tm, 3 * HEAD_PAD, kr_blk)],
        out_shape=[jax.ShapeDtypeStruct((s, 2 * hw), _MXU_DT), jax.ShapeDtypeStruct((s, 2 * hw), _MXU_DT),
                   jax.ShapeDtypeStruct(dp.shape, dp.dtype)],
        input_output_aliases={5: 2},
        compiler_params=_cparams(("parallel",)),
    )(dq, dk, dv, tc, ts, dp)


_ATTN_TQ = 512
_ATTN_SCALE = (QK_NOPE + QK_ROPE) ** -0.5


def _attn_probs(q, k, i):
    s = k.shape[0]
    sc = lax.dot_general(q, k, (((1,), (1,)), ((), ())), preferred_element_type=F32) * _ATTN_SCALE
    row = i * _ATTN_TQ + lax.broadcasted_iota(jnp.int32, (_ATTN_TQ, s), 0)
    col = lax.broadcasted_iota(jnp.int32, (_ATTN_TQ, s), 1)
    sc = jnp.where(row >= col, sc, jnp.finfo(F32).min)
    e = jnp.exp(sc - jnp.max(sc, axis=1, keepdims=True))
    return e * (1.0 / jnp.sum(e, axis=1, keepdims=True))


def _per_query_block(s, fn):
    i = pl.program_id(1)
    for n in range(s // _ATTN_TQ):
        @pl.when(i == n)
        def _(n=n):
            fn(n, (n + 1) * _ATTN_TQ)


def _attn_fwd(q, k, v):
    s = q.shape[0]

    def body(q_ref, k_ref, v_ref, o_ref):
        def block(n, kl):
            p = _attn_probs(q_ref[...], k_ref[0:kl, :], n)
            o_ref[...] = jnp.dot(p.astype(_MXU_DT), v_ref[0:kl, :], preferred_element_type=F32).astype(o_ref.dtype)

        _per_query_block(s, block)

    qspec = pl.BlockSpec((_ATTN_TQ, HEAD_PAD), lambda h, i: (i, h))
    kspec = pl.BlockSpec((s, HEAD_PAD), lambda h, i: (0, h))
    return pl.pallas_call(
        body, name="attn_fwd", grid=(N_HEADS, s // _ATTN_TQ),
        in_specs=[qspec, kspec, kspec], out_specs=qspec,
        out_shape=jax.ShapeDtypeStruct(q.shape, _MXU_DT),
        compiler_params=_cparams(("parallel", "parallel")),
    )(q, k, v)


def _attn_bwd(q, k, v, do):
    s = q.shape[0]

    def body(q_ref, k_ref, v_ref, do_ref, dq_ref, dk_ref, dv_ref):
        i = pl.program_id(1)

        @pl.when(i == 0)
        def _():
            dk_ref[...] = jnp.zeros_like(dk_ref)
            dv_ref[...] = jnp.zeros_like(dv_ref)

        def block(n, kl):
            qv, kv, dov = q_ref[...], k_ref[0:kl, :], do_ref[...]
            p = _attn_probs(qv, kv, n)
            dp = lax.dot_general(dov, v_ref[0:kl, :], (((1,), (1,)), ((), ())), preferred_element_type=F32)
            delta = jnp.sum(p * dp, axis=1, keepdims=True)
            ds = (p * (dp - delta) * _ATTN_SCALE).astype(_MXU_DT)
            dq_ref[...] = jnp.dot(ds, kv, preferred_element_type=F32)
            dk_ref[0:kl, :] += lax.dot_general(ds, qv, (((0,), (0,)), ((), ())), preferred_element_type=F32)
            dv_ref[0:kl, :] += lax.dot_general(p.astype(_MXU_DT), dov, (((0,), (0,)), ((), ())),
                                               preferred_element_type=F32)

        _per_query_block(s, block)

    qspec = pl.BlockSpec((_ATTN_TQ, HEAD_PAD), lambda h, i: (i, h))
    kspec = pl.BlockSpec((s, HEAD_PAD), lambda h, i: (0, h))
    return pl.pallas_call(
        body, name="attn_bwd", grid=(N_HEADS, s // _ATTN_TQ),
        in_specs=[qspec, kspec, kspec, qspec], out_specs=[qspec, kspec, kspec],
        out_shape=[jax.ShapeDtypeStruct(q.shape, F32)] * 3,
        compiler_params=_cparams(("parallel", "arbitrary")),
    )(q, k, v, do)


def _loss_head(y, target):
    s, d = y.shape
    tm = 256

    def body(y_ref, t_ref, loss_ref, dy_ref):
        @pl.when(pl.program_id(0) == 0)
        def _():
            loss_ref[...] = jnp.zeros_like(loss_ref)

        err = y_ref[...] - t_ref[...]
        loss_ref[...] += 0.5 * jnp.sum(jnp.mean(err * err, axis=-1, keepdims=True), axis=0, keepdims=True)
        dy_ref[...] = err * (1.0 / d)

    return pl.pallas_call(
        body, name="loss_head", grid=(s // tm,),
        in_specs=[_row_spec(tm, d, 0), _row_spec(tm, d, 0)],
        out_specs=[_full_spec((1, 1)), _row_spec(tm, d, 0)],
        out_shape=[jax.ShapeDtypeStruct((1, 1), F32), jax.ShapeDtypeStruct((s, d), F32)],
        compiler_params=_cparams(("arbitrary",)),
    )(y, target)


def _adamw_math(w, g, m, v):
    mn = ADAM_B1 * m + (1.0 - ADAM_B1) * g
    vn = ADAM_B2 * v + (1.0 - ADAM_B2) * (g * g)
    m_hat = mn / (1.0 - ADAM_B1 ** ADAM_STEP)
    v_hat = vn / (1.0 - ADAM_B2 ** ADAM_STEP)
    return -ADAM_LR * (m_hat / (jnp.sqrt(v_hat) + ADAM_EPS) + ADAM_WD * w), mn, vn


def _adamw(name, w, g, m, v, layer, into):
    _, k, n = w.shape
    tk, tn = _slab_block(k, n)

    def body(w_ref, g_ref, m_ref, v_ref, *rest):
        d_ref, mo_ref, vo_ref = rest[-3:]
        d_ref[...], mo_ref[...], vo_ref[...] = _adamw_math(w_ref[...], g_ref[...], m_ref[...], v_ref[...])

    spec = pl.BlockSpec((1, tk, tn), lambda j, jn: (layer, j, jn))
    extra = [] if into is None else list(into)
    return pl.pallas_call(
        body, name=name, grid=(k // tk, n // tn), in_specs=[spec] * 4 + [_ANY] * len(extra), out_specs=[spec] * 3,
        out_shape=[jax.ShapeDtypeStruct(w.shape, F32)] * 3,
        input_output_aliases={4 + i: i for i in range(len(extra))},
        compiler_params=_cparams(("parallel", "parallel")),
    )(w, g, m, v, *extra)


_ANY = pl.BlockSpec(memory_space=pl.ANY)


def _mesh_pos():
    return lax.axis_index("x"), lax.axis_index("y"), lax.axis_index("c")


def _other_chips(x, y):
    return [(1 - x, y), (x, 1 - y), (1 - x, 1 - y)]


def _remote(src, dst, send_sem, recv_sem, to):
    return pltpu.make_async_remote_copy(src_ref=src, dst_ref=dst, send_sem=send_sem, recv_sem=recv_sem,
                                        device_id=to, device_id_type=_MESH)


_HBM = pl.BlockSpec(memory_space=pltpu.HBM)
_SEM = pl.BlockSpec(memory_space=pltpu.SEMAPHORE)
_EFFECT = pltpu.SideEffectType.DATAFLOW_SIDE_EFFECTING


def _in_hbm(a):
    return pltpu.with_memory_space_constraint(a, pltpu.HBM)


def _chip_index(chip):
    return 2 * chip[0] + chip[1]


def _ag_forward(name, shards, lands, layer, have_remote):
    n = len(shards)

    def body(*refs):
        ins = refs[:n]
        outs = refs[2 * n:3 * n] if lands is not None else refs[n:2 * n]
        send_sems, recv_sems = refs[-2:]
        x, y, c = _mesh_pos()
        sibling = (x, y, 1 - c)
        chips = _other_chips(x, y)

        def copy(a, k, src, dst, to):
            return _remote(src, dst, send_sems.at[a, k], recv_sems.at[a, k], to)

        own = [copy(a, 6, ins[a], outs[a].at[2 * x + y], sibling) for a in range(n)]
        for cp in own:
            cp.start()

        @pl.when(c == layer)
        def _():
            started = []
            if not have_remote:
                for k, chip in enumerate(chips):
                    for a in range(n):
                        cp = copy(a, k, ins[a], outs[a].at[2 * x + y], (*chip, c))
                        cp.start()
                        started.append(cp)
            for k, chip in enumerate(chips):
                for a in range(n):
                    landed = outs[a].at[_chip_index(chip)]
                    if not have_remote:
                        copy(a, k, ins[a], landed, (*chip, c)).wait_recv()
                    cp = copy(a, 3 + k, landed, landed, sibling)
                    cp.start()
                    started.append(cp)
            for cp in started:
                cp.wait_send()

        @pl.when(c != layer)
        def _():
            for k, chip in enumerate(chips):
                for a in range(n):
                    copy(a, 3 + k, ins[a], outs[a].at[_chip_index(chip)], sibling).wait_recv()

        for cp in own:
            cp.wait()

    out_shape = [jax.ShapeDtypeStruct((4,) + a.shape, a.dtype) for a in shards]
    extra = [] if lands is None else list(lands)
    return pl.pallas_call(
        body, name=name, in_specs=[_ANY] * (n + len(extra)), out_specs=[_ANY] * n,
        out_shape=out_shape, input_output_aliases={n + a: a for a in range(len(extra))},
        scratch_shapes=[pltpu.SemaphoreType.DMA((n, 7)), pltpu.SemaphoreType.DMA((n, 7))],
    )(*shards, *extra)


def _owner_sends(owners, srcs, dsts, send_sems, recv_sems, do):
    x, y, c = _mesh_pos()
    for core in (0, 1):
        mine = [a for a in range(len(srcs)) if owners[a] == core]
        if mine:
            @pl.when(c == core)
            def _(mine=mine):
                for k, chip in enumerate(_other_chips(x, y)):
                    for a in mine:
                        do(_remote(srcs[a](chip, k), dsts[a](chip, k), send_sems.at[3 * a + k],
                                   recv_sems.at[3 * a + k], (*chip, c)))


def _split_start(name, owners, sources, land_shapes, src_of, dst_of, after):
    n = len(sources)

    def body(*refs):
        srcs, lands = refs[:n], refs[n:2 * n]
        send_sems, recv_sems = refs[2 * n + 1], refs[2 * n + 2]
        token = refs[-1]
        _owner_sends(owners, [functools.partial(src_of, srcs[a]) for a in range(n)],
                     [functools.partial(dst_of, lands[a]) for a in range(n)], send_sems, recv_sems,
                     lambda cp: cp.start())
        token[...] = jnp.zeros_like(token)

    lands = [_in_hbm(lax.empty(s.shape, s.dtype)) for s in land_shapes]
    outs = pl.pallas_call(
        body, name=name,
        out_shape=([pltpu.SemaphoreType.DMA((3 * n,)), pltpu.SemaphoreType.DMA((3 * n,))]
                   + [pltpu.HBM(a.shape, a.dtype) for a in sources] + [pltpu.HBM(s.shape, s.dtype) for s in land_shapes]
                   + [jax.ShapeDtypeStruct((8, _LANES), F32)]),
        in_specs=[_HBM] * (2 * n) + [_ANY],
        out_specs=[_SEM, _SEM] + [_HBM] * (2 * n) + [pl.BlockSpec(memory_space=pltpu.VMEM)],
        input_output_aliases={i: 2 + i for i in range(2 * n)},
        compiler_params=pltpu.CompilerParams(has_side_effects=_EFFECT),
    )(*[_in_hbm(a) for a in sources], *lands, after)
    return outs[0], outs[1], outs[2:2 + n], outs[2 + n:2 + 2 * n], outs[-1]


def _split_wait(name, owners, send_sems, recv_sems, sources, lands, after, src_of, dst_of):
    n = len(sources)

    def body(*refs):
        srcs, lnds = refs[:n], refs[n:2 * n]
        s_sems, r_sems = refs[2 * n], refs[2 * n + 1]

        def wait(cp):
            cp.wait_send()
            cp.wait_recv()

        _owner_sends(owners, [functools.partial(src_of, srcs[a]) for a in range(n)],
                     [functools.partial(dst_of, lnds[a]) for a in range(n)], s_sems, r_sems, wait)

    outs = pl.pallas_call(
        body, name=name,
        out_shape=[pltpu.HBM(a.shape, a.dtype) for a in sources] + [pltpu.HBM(a.shape, a.dtype) for a in lands],
        in_specs=[_HBM] * (2 * n) + [_SEM, _SEM] + [_ANY] * len(after), out_specs=[_HBM] * (2 * n),
        input_output_aliases={i: i for i in range(2 * n)},
        compiler_params=pltpu.CompilerParams(has_side_effects=_EFFECT),
    )(*sources, *lands, send_sems, recv_sems, *after)
    return outs[:n], outs[n:]


def _pair_exchange(name, arrays, owners, to_owner, layer=None):
    n = len(arrays)

    def body(*refs):
        ins, outs, (send_sems, recv_sems) = refs[:n], refs[n:2 * n], refs[2 * n:]
        x, y, c = _mesh_pos()
        part = (lambda r: r) if layer is None else (lambda r: r.at[layer])
        copies = [_remote(part(ins[a]), part(outs[a]), send_sems.at[a], recv_sems.at[a], (x, y, 1 - c))
                  for a in range(n)]
        for core in (0, 1):
            sends = [copies[a] for a in range(n) if (owners[a] != core) == to_owner]
            recvs = [copies[a] for a in range(n) if (owners[a] == core) == to_owner]

            @pl.when(c == core)
            def _(sends=sends, recvs=recvs):
                for cp in sends:
                    cp.start()
                for cp in recvs:
                    cp.wait_recv()
                for cp in sends:
                    cp.wait_send()

    return pl.pallas_call(
        body, name=name, in_specs=[_ANY] * n, out_specs=[_ANY] * n,
        out_shape=[jax.ShapeDtypeStruct(g.shape, g.dtype) for g in arrays],
        input_output_aliases={} if to_owner else {a: a for a in range(n)},
        scratch_shapes=[pltpu.SemaphoreType.DMA((n,)), pltpu.SemaphoreType.DMA((n,))],
    )(*arrays)


def _swap_split(name, arrays, lands, owners, sems, after):
    n = len(arrays)
    starting = sems is None

    def body(*refs):
        ins, lnds = refs[:n], refs[n:2 * n]
        send_sems, recv_sems = (refs[2 * n + 1], refs[2 * n + 2]) if starting else (refs[2 * n], refs[2 * n + 1])
        x, y, c = _mesh_pos()
        copies = [_remote(ins[a], lnds[a], send_sems.at[a], recv_sems.at[a], (x, y, 1 - c)) for a in range(n)]
        for core in (0, 1):
            sends = [copies[a] for a in range(n) if owners[a] != core]
            recvs = [copies[a] for a in range(n) if owners[a] == core]

            @pl.when(c == core)
            def _(sends=sends, recvs=recvs):
                if starting:
                    for cp in sends:
                        cp.start()
                else:
                    for cp in recvs:
                        cp.wait_recv()
                    for cp in sends:
                        cp.wait_send()

        if starting:
            refs[-1][...] = jnp.zeros_like(refs[-1])

    hbm = lambda arrs: [pltpu.HBM(a.shape, a.dtype) for a in arrs]
    if starting:
        zones = [_in_hbm(lax.empty(s.shape, s.dtype)) for s in lands]
        outs = pl.pallas_call(
            body, name=name,
            out_shape=([pltpu.SemaphoreType.DMA((n,)), pltpu.SemaphoreType.DMA((n,))] + hbm(arrays) + hbm(lands)
                       + [jax.ShapeDtypeStruct((8, _LANES), F32)]),
            in_specs=[_HBM] * (2 * n) + [_ANY],
            out_specs=[_SEM, _SEM] + [_HBM] * (2 * n) + [pl.BlockSpec(memory_space=pltpu.VMEM)],
            input_output_aliases={i: 2 + i for i in range(2 * n)},
            compiler_params=pltpu.CompilerParams(has_side_effects=_EFFECT),
        )(*[_in_hbm(a) for a in arrays], *zones, after)
        return outs[0], outs[1], outs[2:2 + n], outs[2 + n:2 + 2 * n], outs[-1]
    outs = pl.pallas_call(
        body, name=name, out_shape=hbm(arrays) + hbm(lands),
        in_specs=[_HBM] * (2 * n) + [_SEM, _SEM] + [_ANY] * len(after), out_specs=[_HBM] * (2 * n),
        input_output_aliases={i: i for i in range(2 * n)},
        compiler_params=pltpu.CompilerParams(has_side_effects=_EFFECT),
    )(*arrays, *lands, *sems, *after)
    return outs[:n], outs[n:]


def _add_pair(name, g, a, flags):
    _, k, n = g.shape
    tk, tn = _slab_block(k, n)

    def body(flags_ref, g_ref, a_ref, o_ref):
        o_ref[...] = (g_ref[...] + a_ref[...]).astype(o_ref.dtype)

    spec = pl.BlockSpec((1, tk, tn), lambda j, i, jn, fl: (j * fl[1], i * fl[1], jn * fl[1]))
    return pl.pallas_call(
        body, name=name,
        grid_spec=pltpu.PrefetchScalarGridSpec(num_scalar_prefetch=1, grid=(4, k // tk, n // tn),
                                               in_specs=[spec, spec], out_specs=spec),
        out_shape=jax.ShapeDtypeStruct(g.shape, _RS_DT),
        compiler_params=_cparams(("arbitrary", "arbitrary", "arbitrary")),
    )(flags, g, a)


def _add_quads(name, t, b, layer, flags, into, after=None):
    _, k, n = t.shape
    tk, tn = _slab_block(k, n)

    def body(flags_ref, t_ref, b_ref, *rest):
        o_ref = rest[-1]
        f = lambda v: v.astype(F32)
        o_ref[0] = ((f(t_ref[0]) + f(b_ref[0])) + f(b_ref[1])) + f(b_ref[2])

    extra = ([] if into is None else [into]) + ([] if after is None else [after])
    return pl.pallas_call(
        body, name=name,
        grid_spec=pltpu.PrefetchScalarGridSpec(
            num_scalar_prefetch=1, grid=(k // tk, n // tn),
            in_specs=[pl.BlockSpec((1, tk, tn), lambda i, jn, fl: (fl[0], i * fl[1], jn * fl[1])),
                      pl.BlockSpec((3, tk, tn), lambda i, jn, fl: (0, i * fl[1], jn * fl[1]))] + [_ANY] * len(extra),
            out_specs=pl.BlockSpec((1, tk, tn), lambda i, jn, fl: (layer, i * fl[1], jn * fl[1]))),
        out_shape=jax.ShapeDtypeStruct((2, k, n), F32),
        input_output_aliases={} if into is None else {3: 0},
        compiler_params=_cparams(("arbitrary", "arbitrary")),
    )(flags, t, b, *extra)


def _slab_block(k, n, itemsize=4):
    tk = (1 << 20) // (n * itemsize) // 16 * 16
    while 0 < tk < k and k % tk:
        tk -= 16
    if 0 < tk < k:
        return tk, n
    if k * n * itemsize <= (2 << 20) or n % _LANES:
        return k, n
    tn = max(_LANES, (2 << 20) // (k * itemsize) // _LANES * _LANES)
    while n % tn:
        tn -= _LANES
    return k, tn


def _all_reduce_adamw(gs, ws, ms, vs):
    n = len(gs)

    def body(*refs):
        g_refs, w_refs, m_refs, v_refs = (refs[i * n:(i + 1) * n] for i in range(4))
        gsum, delta, m_out, v_out = (refs[(4 + i) * n:(5 + i) * n] for i in range(4))
        slots = refs[8 * n:9 * n]
        send_sems, recv_sems = refs[9 * n:]
        x, y, c = _mesh_pos()
        me = 4 * x + 2 * y + c
        copies = []
        for rel in range(1, 8):
            bx, by, bc = (rel >> 2) & 1, (rel >> 1) & 1, rel & 1
            peer = (1 - x if bx else x, 1 - y if by else y, 1 - c if bc else c)
            for a in range(n):
                cp = _remote(g_refs[a], slots[a].at[me], send_sems.at[a, rel - 1], recv_sems.at[a, rel - 1], peer)
                cp.start()
                copies.append(cp)
        for a in range(n):
            slots[a][me] = g_refs[a][...]
        for cp in copies:
            cp.wait()
        for a in range(n):
            acc = slots[a][0]
            for d in range(1, 8):
                acc = acc + slots[a][d]
            gsum[a][...] = acc
            delta[a][...], m_out[a][...], v_out[a][...] = _adamw_math(w_refs[a][...], acc, m_refs[a][...],
                                                                      v_refs[a][...])

    vmem = pl.BlockSpec(memory_space=pltpu.VMEM)
    outs = pl.pallas_call(
        body, name="all_reduce_adamw", in_specs=[vmem] * (4 * n), out_specs=[vmem] * (4 * n),
        out_shape=[jax.ShapeDtypeStruct(g.shape, F32) for g in gs] * 4,
        scratch_shapes=([pltpu.VMEM((8,) + g.shape, F32) for g in gs]
                        + [pltpu.SemaphoreType.DMA((n, 7)), pltpu.SemaphoreType.DMA((n, 7))]),
        compiler_params=pltpu.CompilerParams(vmem_limit_bytes=_VMEM_LIMIT),
    )(*gs, *ws, *ms, *vs)
    return outs[:n], outs[n:2 * n], outs[2 * n:3 * n], outs[3 * n:]


def _swap_rope(a):
    h = QK_ROPE // 2
    return jnp.concatenate([a[..., h:], a[..., :h]], axis=-1)


def _swap_rope_rows(a):
    h = QK_ROPE // 2
    return jnp.concatenate([a[h:], a[:h]], axis=0)


_FFN_CB = 256


def _interleave_rows(a, cb):
    r, c = a.shape
    return a.reshape(2, r // (2 * cb), cb, c).transpose(1, 0, 2, 3).reshape(r, c)


def _deinterleave_rows(a, cb):
    r, c = a.shape
    return a.reshape(r // (2 * cb), 2, cb, c).transpose(1, 0, 2, 3).reshape(r, c)


class _InLayout:
    def __init__(self, d):
        self.d = d
        self.gates = 0
        self.a = 3 * d
        self.b = 4 * d
        self.kv = 5 * d
        self.q = self.kv + 256
        self.krm = self.q + 384
        self.krs = self.krm + HEAD_PAD
        self.width = self.krs + 2 * HEAD_PAD


def _prep_layer(wl, d):
    lay = _InLayout(d)
    w_in = wl["w_in"]
    dt = w_in.dtype
    a, b = w_in[0:d], w_in[d:2 * d]
    q, kv = w_in[2 * d:2 * d + 384], w_in[2 * d + 384:2 * d + 640]
    kr = w_in[2 * d + 640:2 * d + 640 + QK_ROPE]
    gates = w_in[2 * d + 640 + QK_ROPE:]
    z = lambda n: jnp.zeros((n, d), dt)
    krm = jnp.concatenate([z(QK_NOPE), kr, z(HEAD_PAD - QK_NOPE - QK_ROPE)], axis=0)
    krs = jnp.concatenate([z(QK_NOPE), _swap_rope_rows(kr), z(HEAD_PAD - QK_NOPE - QK_ROPE)], axis=0)
    out = dict(wl)
    out["w_in"] = jnp.concatenate([gates, _interleave_rows(a, _LANES), b, kv, q, krm, krs,
                                   z(lay.width - lay.krs - HEAD_PAD)], axis=0)
    uq = wl["mla_w_uq"].reshape(-1, N_HEADS, QK_NOPE + QK_ROPE)
    nq = uq.shape[0]
    nope, pe = uq[..., :QK_NOPE], uq[..., QK_NOPE:]
    zq = lambda n: jnp.zeros((nq, N_HEADS, n), dt)
    main = jnp.concatenate([nope, pe, zq(HEAD_PAD - QK_NOPE - QK_ROPE)], axis=-1).reshape(nq, -1)
    swapped = jnp.concatenate([zq(QK_NOPE), _swap_rope(pe), zq(HEAD_PAD - QK_NOPE - QK_ROPE)], axis=-1).reshape(nq, -1)
    out["mla_w_uq"] = jnp.concatenate([main, swapped], axis=1)
    ukv = wl["mla_w_ukv"].reshape(-1, N_HEADS, QK_NOPE + V_HEAD)
    nkv = ukv.shape[0]
    zk = jnp.zeros((nkv, N_HEADS, HEAD_PAD - QK_NOPE), dt)
    zv = jnp.zeros((nkv, N_HEADS, HEAD_PAD - V_HEAD), dt)
    out["mla_w_ukv"] = jnp.concatenate([jnp.concatenate([ukv[..., :QK_NOPE], zk], axis=-1).reshape(nkv, -1),
                                        jnp.concatenate([ukv[..., QK_NOPE:], zv], axis=-1).reshape(nkv, -1)], axis=1)
    wo = wl["mla_w_o"].reshape(N_HEADS, V_HEAD, -1)
    out["mla_w_o"] = jnp.concatenate([wo, jnp.zeros((N_HEADS, HEAD_PAD - V_HEAD, wo.shape[-1]), dt)],
                                     axis=1).reshape(N_HEADS * HEAD_PAD, -1)
    return out


def _unprep_grads(g, d):
    lay = _InLayout(d)
    gi = g["w_in"]
    kr = (gi[lay.krm + QK_NOPE:lay.krm + QK_NOPE + QK_ROPE]
          + _swap_rope_rows(gi[lay.krs + QK_NOPE:lay.krs + QK_NOPE + QK_ROPE]))
    out = dict(g)
    out["w_in"] = jnp.concatenate([_deinterleave_rows(gi[lay.a:lay.a + d], _LANES), gi[lay.b:lay.b + d],
                                   gi[lay.q:lay.q + 384], gi[lay.kv:lay.kv + 256], kr,
                                   gi[lay.gates:lay.gates + 3 * d]], axis=0)
    hw = N_HEADS * HEAD_PAD
    gq = g["mla_w_uq"]
    nq = gq.shape[0]
    main = gq[:, :hw].reshape(nq, N_HEADS, HEAD_PAD)
    swapped = gq[:, hw:].reshape(nq, N_HEADS, HEAD_PAD)
    pe = main[..., QK_NOPE:QK_NOPE + QK_ROPE] + _swap_rope(swapped[..., QK_NOPE:QK_NOPE + QK_ROPE])
    out["mla_w_uq"] = jnp.concatenate([main[..., :QK_NOPE], pe], axis=-1).reshape(nq, -1)
    gkv = g["mla_w_ukv"]
    nkv = gkv.shape[0]
    out["mla_w_ukv"] = jnp.concatenate([gkv[:, :hw].reshape(nkv, N_HEADS, HEAD_PAD)[..., :QK_NOPE],
                                        gkv[:, hw:].reshape(nkv, N_HEADS, HEAD_PAD)[..., :V_HEAD]],
                                       axis=-1).reshape(nkv, -1)
    go = g["mla_w_o"]
    out["mla_w_o"] = go.reshape(N_HEADS, HEAD_PAD, -1)[:, :V_HEAD].reshape(N_HEADS * V_HEAD, -1)
    return out


def _rope_tables(positions):
    s = positions.shape[0]
    inv = ROPE_THETA ** (-jnp.arange(0, QK_ROPE, 2, dtype=F32) / QK_ROPE)
    ang = positions.astype(F32)[:, None] * inv
    cos, sin = jnp.cos(ang), jnp.sin(ang)
    tail = jnp.zeros((s, HEAD_PAD - QK_NOPE - QK_ROPE), F32)
    tc = jnp.concatenate([jnp.ones((s, QK_NOPE), F32), cos, cos, tail], axis=1)
    ts = jnp.concatenate([jnp.zeros((s, QK_NOPE), F32), -sin, sin, tail], axis=1)
    return tc, ts


def _row(v):
    return v.reshape(1, -1)


def _layer_fwd(x, h, w, g_next, tc, ts, late_weights=None):
    d = x.shape[1]
    lay = _InLayout(d)
    cw = d // 2
    blk = lambda off, width: off // width
    p = _mm("mm_in", h, w["w_in"], tb=True)
    z1 = _glu_conv_fwd(p, blk(lay.a, 2 * _LANES), w["conv_dw_w"], _row(w["conv_dw_b"]))
    ln_a = [_row(w["conv_ln_g"]), _row(w["conv_ln_b"])]
    (z3,) = _row_fwd("ln_silu_fwd", _f_ln_silu, [(z1, cw, 0)], ln_a, [(cw, _MXU_DT)])
    ya = _mm("mm_conv_out", z3, w["conv_out_w"])
    ln_b = [_row(w["sg_ln_g"]), _row(w["sg_ln_b"])]
    u, vn = _row_fwd("sg_pre_fwd", _f_sg_pre, [(p, cw, blk(lay.b, cw)), (p, cw, blk(lay.b + cw, cw))], ln_b,
                     [(cw, F32), (cw, _MXU_DT)])
    bcol = w["sg_b"].reshape(SG_GROUPS, SG_CHUNK, 1)
    ub = _sg_mix_fwd(u, vn, w["sg_w"], bcol)
    yb = _mm("mm_sg_out", ub, w["sg_out_w"])
    (qn,) = _row_fwd("q_norm_fwd", _f_rms, [(p, 384, blk(lay.q, 384))], [_row(w["mla_q_norm_g"])], [(384, _MXU_DT)])
    (kvn,) = _row_fwd("kv_norm_fwd", _f_rms, [(p, 256, blk(lay.kv, 256))], [_row(w["mla_kv_norm_g"])],
                      [(256, _MXU_DT)])
    q2 = _mm("mm_uq", qn, w["mla_w_uq"])
    kv2 = _mm("mm_ukv", kvn, w["mla_w_ukv"])
    qf, kf, vf = _rope_fwd(q2, kv2, p, blk(lay.krm, HEAD_PAD), blk(lay.krs, HEAD_PAD), tc, ts)
    o = _attn_fwd(qf, kf, vf)
    yc = _mm("mm_o", o, w["mla_w_o"])
    gate_rows = [(p, d, 0), (p, d, 1), (p, d, 2)]
    (merged,) = _row_fwd("merge_fwd", _f_merge, gate_rows + [(ya, d, 0), (yb, d, 0), (yc, d, 0)], [], [(d, _MXU_DT)])
    if late_weights is not None:
        w = {**w, **late_weights(merged)}
    t = _mm("mm_out", merged, w["w_out"])
    x1, h2 = _row_fwd("resid_mix_fwd", _f_resid_rms_rms, [(x, d, 0), (t, d, 0)],
                      [_row(w["mix_post_g"]), _row(w["ffn_pre_g"])], [(d, F32), (d, _MXU_DT)])
    up = _mm("mm_up", h2, w["ffn_w_up"])
    act = _conv_geglu_fwd(up, w["ffn_dw_w"], _row(w["ffn_dw_b"]))
    dn = _mm("mm_down", act, w["ffn_w_down"])
    if g_next is None:
        (x2,) = _row_fwd("resid_ffn_last_fwd", _f_resid_rms, [(x1, d, 0), (dn, d, 0)], [_row(w["ffn_post_g"])],
                         [(d, F32)])
        h_next = None
    else:
        x2, h_next = _row_fwd("resid_ffn_fwd", _f_resid_rms_rms, [(x1, d, 0), (dn, d, 0)],
                              [_row(w["ffn_post_g"]), _row(g_next)], [(d, F32), (d, _MXU_DT)])
    saved = dict(x=x, h=h, p=p, z1=z1, z3=z3, ya=ya, u=u, vn=vn, ub=ub, yb=yb, qn=qn, kvn=kvn, qf=qf, kf=kf, vf=vf, o=o,
                 yc=yc, merged=merged, t=t, x1=x1, h2=h2, up=up, act=act, dn=dn, bcol=bcol)
    return x2, h_next, saved


def _layer_bwd(dx2, dh_next, w, g_next, sv, tc, ts, on_late_grads=None):
    d = dx2.shape[1]
    lay = _InLayout(d)
    cw = d // 2
    blk = lambda off, width: off // width
    lo = _MXU_DT
    g = {}
    x1, dn = sv["x1"], sv["dn"]
    if dh_next is None:
        dx1, ddn, g["ffn_post_g"] = _row_bwd("resid_ffn_last_bwd", _f_resid_rms, [(x1, d, 0), (dn, d, 0)],
                                             [_row(w["ffn_post_g"])], [(dx2, d, 0)], [(0, F32), (1, lo)], [0])
    else:
        dx1, ddn, g["ffn_post_g"], g["next_pre_g"] = _row_bwd(
            "resid_ffn_bwd", _f_resid_rms_rms, [(x1, d, 0), (dn, d, 0)], [_row(w["ffn_post_g"]), _row(g_next)],
            [(dx2, d, 0), (dh_next, d, 0)], [(0, F32), (1, lo)], [0, 1])
    dact = _mm("mm_down_dx", ddn, w["ffn_w_down"], tb=True)
    g["ffn_w_down"] = _mm("mm_down_dw", sv["act"], ddn, ta=True)
    dup, dw_halves, db_halves = _conv_geglu_bwd(sv["up"], w["ffn_dw_w"], _row(w["ffn_dw_b"]), dact)
    g["ffn_dw_w"] = jnp.concatenate([dw_halves[0], dw_halves[1]], axis=1)
    g["ffn_dw_b"] = jnp.concatenate([db_halves[0], db_halves[1]], axis=1)
    dh2 = _mm("mm_up_dx", dup, w["ffn_w_up"], tb=True, a_halves=True)
    g["ffn_w_up"] = _mm("mm_up_dw", sv["h2"], dup, ta=True, b_halves=True, out_quarters=True)
    dx, dt, g["mix_post_g"], g["ffn_pre_g"] = _row_bwd(
        "resid_mix_bwd", _f_resid_rms_rms, [(sv["x"], d, 0), (sv["t"], d, 0)],
        [_row(w["mix_post_g"]), _row(w["ffn_pre_g"])], [(dx1, d, 0), (dh2, d, 0)], [(0, F32), (1, lo)], [0, 1])
    dmerged = _mm("mm_out_dx", dt, w["w_out"], tb=True)
    g["w_out"] = _mm("mm_out_dw", sv["merged"], dt, ta=True)
    if on_late_grads is not None:
        dmerged = on_late_grads(g, dmerged)
    p = sv["p"]
    gate_rows = [(p, d, 0), (p, d, 1), (p, d, 2)]
    dp, dya, dyb, dyc = _row_bwd(
        "merge_bwd", _f_merge, gate_rows + [(sv["ya"], d, 0), (sv["yb"], d, 0), (sv["yc"], d, 0)], [],
        [(dmerged, d, 0)], [((0, 1, 2), lo), ((3,), lo), ((4,), lo), ((5,), lo)], [], place=(lay.width, 0))
    do = _mm("mm_o_dx", dyc, w["mla_w_o"], tb=True, out_dtype=lo)
    g["mla_w_o"] = _mm("mm_o_dw", sv["o"], dyc, ta=True)
    dqf, dkf, dvf = _attn_bwd(sv["qf"], sv["kf"], sv["vf"], do)
    dq2, dkv2, dp = _rope_bwd(dqf, dkf, dvf, tc, ts, dp, blk(lay.krm, 3 * HEAD_PAD))
    dkvn = _mm("mm_ukv_dx", dkv2, w["mla_w_ukv"], tb=True)
    g["mla_w_ukv"] = _mm("mm_ukv_dw", sv["kvn"], dkv2, ta=True)
    dqn = _mm("mm_uq_dx", dq2, w["mla_w_uq"], tb=True)
    g["mla_w_uq"] = _mm("mm_uq_dw", sv["qn"], dq2, ta=True)
    dp, g["mla_q_norm_g"] = _row_bwd("q_norm_bwd", _f_rms, [(p, 384, blk(lay.q, 384))], [_row(w["mla_q_norm_g"])],
                                     [(dqn, 384, 0)], [((0,), lo)], [0], place=(lay.width, blk(lay.q, 384)), into=dp)
    dp, g["mla_kv_norm_g"] = _row_bwd("kv_norm_bwd", _f_rms, [(p, 256, blk(lay.kv, 256))],
                                      [_row(w["mla_kv_norm_g"])], [(dkvn, 256, 0)], [((0,), lo)], [0],
                                      place=(lay.width, blk(lay.kv, 256)), into=dp)
    dub = _mm("mm_sg_out_dx", dyb, w["sg_out_w"], tb=True)
    g["sg_out_w"] = _mm("mm_sg_out_dw", sv["ub"], dyb, ta=True)
    du, dvn, g["sg_w"], dbcol = _sg_mix_bwd(sv["u"], sv["vn"], w["sg_w"], sv["bcol"], dub)
    g["sg_b"] = dbcol.reshape(SG_GROUPS, SG_CHUNK)
    dp, g["sg_ln_g"], g["sg_ln_b"] = _row_bwd(
        "sg_pre_bwd", _f_sg_pre, [(p, cw, blk(lay.b, cw)), (p, cw, blk(lay.b + cw, cw))],
        [_row(w["sg_ln_g"]), _row(w["sg_ln_b"])], [(du, cw, 0), (dvn, cw, 0)], [((0, 1), lo)], [0, 1],
        place=(lay.width, blk(lay.b, d)), into=dp)
    dz3 = _mm("mm_conv_out_dx", dya, w["conv_out_w"], tb=True)
    g["conv_out_w"] = _mm("mm_conv_out_dw", sv["z3"], dya, ta=True)
    dz1, g["conv_ln_g"], g["conv_ln_b"] = _row_bwd(
        "ln_silu_bwd", _f_ln_silu, [(sv["z1"], cw, 0)], [_row(w["conv_ln_g"]), _row(w["conv_ln_b"])],
        [(dz3, cw, 0)], [((0,), F32)], [0, 1])
    dp, g["conv_dw_w"], g["conv_dw_b"] = _glu_conv_bwd(p, blk(lay.a, 2 * _LANES), w["conv_dw_w"], dz1, dp)
    dh = _mm("mm_in_dx", dp, w["w_in"])
    g["w_in"] = _mm("mm_in_dw", dp, sv["h"], ta=True)
    return dx, dh, g


def _local_step(x, positions, target, layers):
    d = x.shape[1]
    tc, ts = _rope_tables(positions)
    ws = [_prep_layer(wl, d) for wl in layers]
    depth = len(ws)
    (h,) = _row_fwd("rms_first_fwd", _f_rms, [(x, d, 0)], [_row(ws[0]["mix_pre_g"])], [(d, _MXU_DT)])
    saved = []
    for l in range(depth):
        g_next = ws[l + 1]["mix_pre_g"] if l + 1 < depth else None
        x, h, sv = _layer_fwd(x, h, ws[l], g_next, tc, ts)
        saved.append(sv)
    loss, dx = _loss_head(x, target)
    grads = [None] * depth
    dh = None
    for l in reversed(range(depth)):
        g_next = ws[l + 1]["mix_pre_g"] if l + 1 < depth else None
        dx, dh, g = _layer_bwd(dx, dh, ws[l], g_next, saved[l], tc, ts)
        if "next_pre_g" in g:
            grads[l + 1]["mix_pre_g"] = g.pop("next_pre_g")
        grads[l] = g
    x0 = saved[0]["x"]
    grad_x, grads[0]["mix_pre_g"] = _row_bwd("rms_first_bwd", _f_x_rms, [(x0, d, 0)], [_row(ws[0]["mix_pre_g"])],
                                             [(dx, d, 0), (dh, d, 0)], [(0, F32)], [0])
    return loss, grad_x, [_unprep_grads(g, d) for g in grads]


_MATRICES = ("w_in", "conv_out_w", "sg_out_w", "mla_w_uq", "mla_w_ukv", "mla_w_o", "w_out", "ffn_w_up", "ffn_w_down")
_F32_GATHERED = ("conv_dw_w", "ffn_dw_w")
_RS_DT = jnp.bfloat16


_ROW_SHARDED = SHARDED_MID + ("w_in",)


_GATHERED = SHARDED + _F32_GATHERED
_RS_CORE0 = ("w_in", "ffn_w_down")
_LATE_WEIGHTS = ("w_out", "ffn_w_up", "ffn_dw_w", "ffn_w_down")


def _layer_shards(w, l):
    hi = {n: w[n][l].astype(jnp.bfloat16) for n in SHARDED}
    lo = [(w[n][l] - hi[n].astype(F32)).astype(jnp.bfloat16) for n in _F32_GATHERED]
    return [hi[n] for n in SHARDED] + lo


def _layer_weights(names, gathered):
    wl = {}
    for n, g in zip(names, gathered):
        if n in _ROW_SHARDED and g.shape[1] % 16 == 0:
            whole = g.reshape(-1, g.shape[2])
        else:
            whole = jnp.concatenate([g[j] for j in range(4)], axis=0 if n in _ROW_SHARDED else 1)
        if n in wl:
            wl[n] = wl[n].astype(F32) + whole.astype(F32)
        else:
            wl[n] = whole.astype(_MXU_DT) if n in _MATRICES else whole
    return wl


def _by_destination(name, gl):
    if gl.ndim == 3:
        return gl
    k, n = gl.shape
    if name in _ROW_SHARDED:
        return gl.reshape(4, k // 4, n)
    return gl.reshape(k, 4, n // 4).transpose(1, 0, 2)


def kernel(x, positions, mix_pre_g, mix_post_g, ffn_pre_g, ffn_post_g, w_in, conv_dw_w, conv_dw_b, conv_ln_g, conv_ln_b, conv_out_w, sg_ln_g, sg_ln_b, sg_w, sg_b, sg_out_w, mla_q_norm_g, mla_w_uq, mla_kv_norm_g, mla_w_ukv, mla_w_o, w_out, ffn_w_up, ffn_dw_w, ffn_dw_b, ffn_w_down, loss_target, m_mix_pre_g, m_mix_post_g, m_ffn_pre_g, m_ffn_post_g, m_w_in, m_conv_dw_w, m_conv_dw_b, m_conv_ln_g, m_conv_ln_b, m_conv_out_w, m_sg_ln_g, m_sg_ln_b, m_sg_w, m_sg_b, m_sg_out_w, m_mla_q_norm_g, m_mla_w_uq, m_mla_kv_norm_g, m_mla_w_ukv, m_mla_w_o, m_w_out, m_ffn_w_up, m_ffn_dw_w, m_ffn_dw_b, m_ffn_w_down, v_mix_pre_g, v_mix_post_g, v_ffn_pre_g, v_ffn_post_g, v_w_in, v_conv_dw_w, v_conv_dw_b, v_conv_ln_g, v_conv_ln_b, v_conv_out_w, v_sg_ln_g, v_sg_ln_b, v_sg_w, v_sg_b, v_sg_out_w, v_mla_q_norm_g, v_mla_w_uq, v_mla_kv_norm_g, v_mla_w_ukv, v_mla_w_o, v_w_out, v_ffn_w_up, v_ffn_dw_w, v_ffn_dw_b, v_ffn_w_down):
    args = dict(locals())
    w = {n: args[n] for n in WEIGHTS}
    m = {n: args["m_" + n] for n in WEIGHTS}
    v = {n: args["v_" + n] for n in WEIGHTS}
    depth = mix_pre_g.shape[0]

    assert depth == 2, "the two cores of a chip split the communication by layer"
    for t in (w, m, v):
        t["w_in"] = jnp.swapaxes(t["w_in"], 1, 2)
    d = x.shape[-1]
    mesh_x, mesh_y, mesh_c = _mesh_pos()
    my_chip = 2 * mesh_x + mesh_y
    names = list(SHARDED)
    whole = lambda ref, chip, k: ref
    to_my_slot = lambda ref, chip, k: ref.at[2 * lax.axis_index("x") + lax.axis_index("y")]
    block_of_chip = lambda ref, chip, k: ref.at[_chip_index(chip)]
    slot_k = lambda ref, chip, k: ref.at[k]

    late = [i for i, n in enumerate(_GATHERED) if n in _LATE_WEIGHTS]
    early = [i for i in range(len(_GATHERED)) if i not in late]
    pick = lambda seq, idx: [seq[i] for i in idx]
    gathered_names = list(_GATHERED)
    shards0, shards1 = _layer_shards(w, 0), _layer_shards(w, 1)
    replicated = lambda l: {n: w[n][l] for n in REPLICATED}
    land_of = lambda shards: [jax.ShapeDtypeStruct((4,) + a.shape, a.dtype) for a in shards]

    gathered0 = _ag_forward("ag_layer0_early", pick(shards0, early), None, 0, False)
    late0 = pick(shards0, late)
    sems0_s, sems0_r, late0, lands0, token0 = _split_start("ag0_start", [0] * len(late), late0, land_of(late0), whole,
                                                           to_my_slot, gathered0[0])
    ag_owner = [1] * len(shards1)
    sems_s, sems_r, shards1, lands1, token = _split_start("ag1_start", ag_owner, shards1, land_of(shards1), whole,
                                                          to_my_slot, gathered0[0])
    tc, ts = _rope_tables(positions[0])
    ws0 = _prep_layer({**replicated(0), **_layer_weights(pick(gathered_names, early), gathered0)}, d)

    def late_weights(merged):
        got = _split_wait("ag0_wait", [0] * len(late), sems0_s, sems0_r, late0, lands0, [merged], whole, to_my_slot)
        got = _ag_forward("ag_layer0_late", got[0], got[1], 0, True)
        ws0.update(_layer_weights(pick(gathered_names, late), got))
        return ws0

    x0 = x[0] + (token0[0, 0] + token[0, 0])
    (h0,) = _row_fwd("rms_first_fwd", _f_rms, [(x0, d, 0)], [_row(ws0["mix_pre_g"])], [(d, _MXU_DT)])
    x1, h1, sv0 = _layer_fwd(x0, h0, ws0, w["mix_pre_g"][1], tc, ts, late_weights)
    shards1, lands1 = _split_wait("ag1_wait", ag_owner, sems_s, sems_r, shards1, lands1, [x1], whole, to_my_slot)
    gathered1 = _ag_forward("ag_layer1", shards1, lands1, 1, True)
    ws1 = _prep_layer({**replicated(1), **_layer_weights(gathered_names, gathered1)}, d)
    x2, _, sv1 = _layer_fwd(x1, h1, ws1, None, tc, ts)
    loss, dx = _loss_head(x2, loss_target[0])
    loss = lax.psum(loss[0, 0], ("x", "y", "c"))

    owner = {n: 0 if n in _RS_CORE0 else 1 for n in names}
    flag = {n: jnp.stack([my_chip, (mesh_c == owner[n]).astype(jnp.int32)]).astype(jnp.int32) for n in names}
    owners = [owner[n] for n in names]

    def scatter_start(tag, group, gd, got, after):
        own = [owner[n] for n in group]
        t = [_add_pair("rs_pair%s_%s" % (tag, n), g, a, flag[n]) for n, g, a in zip(group, gd, got)]
        lands = [jax.ShapeDtypeStruct((3,) + a.shape[1:], a.dtype) for a in t]
        sems_s, sems_r, t, b, token = _split_start("rs%s_start" % tag, own, t, lands, block_of_chip, slot_k, after)
        return (tag, group, own, sems_s, sems_r, t, b), token

    def scatter_wait(handle, after):
        tag, group, own, sems_s, sems_r, t, b = handle
        t, b = _split_wait("rs%s_wait" % tag, own, sems_s, sems_r, t, b, after, block_of_chip, slot_k)
        return dict(zip(group, zip(t, b)))

    def swap_and_start(tag, group, grads_l, after):
        gd = [_by_destination(n, grads_l[n]) for n in group]
        got = _pair_exchange("rs_swap" + tag, gd, [owner[n] for n in group], True)
        return scatter_start(tag, group, gd, got, after)

    dx, dh, gk1 = _layer_bwd(dx, None, ws1, None, sv1, tc, ts)
    grads1 = _unprep_grads(gk1, d)
    gd1 = [_by_destination(n, grads1[n]) for n in names]
    swap1 = _swap_split("rs_swap1_start", gd1, [jax.ShapeDtypeStruct(g.shape, g.dtype) for g in gd1], owners, None,
                        dh)
    dx = dx + swap1[4][0, 0]
    late_group = [n for n in names if n in _LATE_WEIGHTS]
    early_group = [n for n in names if n not in _LATE_WEIGHTS]
    handles = []

    def on_late_grads(g, value):
        gd, got = _swap_split("rs_swap1_wait", swap1[2], swap1[3], owners, swap1[:2], [value])
        handle1, tok1 = scatter_start("1", names, gd, got, value)
        handle0, tok0 = swap_and_start("0_late", late_group, g, value)
        handles.extend([handle1, handle0])
        return value + (tok1[0, 0] + tok0[0, 0])

    dx, dh, gk0 = _layer_bwd(dx, dh, ws0, ws1["mix_pre_g"], sv0, tc, ts, on_late_grads)
    grads1["mix_pre_g"] = gk0.pop("next_pre_g")
    grad_x, gk0["mix_pre_g"] = _row_bwd("rms_first_bwd", _f_x_rms, [(x0, d, 0)], [_row(ws0["mix_pre_g"])],
                                        [(dx, d, 0), (dh, d, 0)], [(0, F32)], [0])
    tb1 = scatter_wait(handles[0], [grad_x])
    grads0 = _unprep_grads(gk0, d)
    handle, token = swap_and_start("0_early", early_group, grads0, grad_x)
    handles.append(handle)

    def finish(l, sums, updates):
        sums = _pair_exchange("rs_join%d" % l, sums, owners, False, layer=l)
        updates = [_adamw("adamw%d_%s" % (l, n), w[n], sums[i], m[n], v[n], l, updates[i])
                   for i, n in enumerate(names)]
        return sums, updates

    sums = [_add_quads("rs_sum1_" + n, *tb1[n], 1, flag[n], None, token) for n in names]
    sums, updates = finish(1, sums, [None] * len(names))
    out = {}
    rep = list(REPLICATED)
    grads = [grads0, grads1]
    g_rep = [jnp.stack([grads[l][n].reshape(w[n].shape[1:]) for l in range(depth)]) for n in rep]
    for n, *res in zip(rep, *_all_reduce_adamw(g_rep, [w[n] for n in rep], [m[n] for n in rep], [v[n] for n in rep])):
        out[n] = tuple(res)
    hidden = [u[0] for u in updates] + [out[rep[0]][0]]
    tb0 = {**scatter_wait(handles[1], hidden), **scatter_wait(handles[2], hidden)}
    sums = [_add_quads("rs_sum0_" + n, *tb0[n], 0, flag[n], sums[i]) for i, n in enumerate(names)]
    sums, updates = finish(0, sums, updates)
    for n, gr, upd in zip(names, sums, updates):
        out[n] = (gr, *upd)
    out["w_in"] = tuple(jnp.swapaxes(a, 1, 2) for a in out["w_in"])
    return (loss, grad_x[None], *[out[n][i] for i in range(4) for n in WEIGHTS])
```

```python
import functools
import math

import jax
import jax.numpy as jnp
from jax import lax
from jax.experimental import pallas as pl
from jax.experimental.pallas import tpu as pltpu

F32 = jnp.float32
_MXU_DT = jnp.bfloat16
_VMEM_LIMIT = 48 * 1024 * 1024
_LANES = 128
_MESH = pl.DeviceIdType.MESH

N_HEADS = 8
QK_NOPE = 64
QK_ROPE = 32
V_HEAD = 64
HEAD_PAD = 128
SG_GROUPS = 4
SG_CHUNK = 128
CONV_K = 31
FFN_K = 3
ROPE_THETA = 10000.0
EPS = 1e-6
ADAM_LR, ADAM_B1, ADAM_B2, ADAM_EPS, ADAM_WD, ADAM_STEP = 0.001, 0.9, 0.999, 1e-08, 0.01, 10

SHARDED_LAST = ("w_in", "conv_dw_w", "conv_out_w", "sg_out_w", "mla_w_uq", "mla_w_ukv", "mla_w_o", "ffn_w_up",
                "ffn_dw_w")
SHARDED_MID = ("w_out", "ffn_w_down")
SHARDED = SHARDED_LAST + SHARDED_MID
WEIGHTS = ("mix_pre_g", "mix_post_g", "ffn_pre_g", "ffn_post_g", "w_in", "conv_dw_w", "conv_dw_b", "conv_ln_g",
           "conv_ln_b", "conv_out_w", "sg_ln_g", "sg_ln_b", "sg_w", "sg_b", "sg_out_w", "mla_q_norm_g", "mla_w_uq",
           "mla_kv_norm_g", "mla_w_ukv", "mla_w_o", "w_out", "ffn_w_up", "ffn_dw_w", "ffn_dw_b", "ffn_w_down")
REPLICATED = tuple(n for n in WEIGHTS if n not in SHARDED)


def _cparams(sem=None):
    return pltpu.CompilerParams(dimension_semantics=sem, vmem_limit_bytes=_VMEM_LIMIT)


def _pick(n, cands):
    for c in cands:
        if n % c == 0:
            return c
    return n


def _largest_tile(dim, cap):
    for t in range(min(cap, dim) // _LANES * _LANES, 0, -_LANES):
        if dim % t == 0:
            return t
    return dim


_MM_VMEM_BUDGET = 36 * 1024 * 1024
_MM_TM_CAP, _MM_TN_CAP, _MM_TK_CAP = 1024, 1536, 3072


def _mm(name, a, b, *, ta=False, tb=False, out_dtype=F32, a_halves=False, b_halves=False, out_quarters=False):
    assert not (a_halves and ta) and not (b_halves and tb)
    if a_halves:
        m, kdim = a.shape[1], 2 * a.shape[2]
    else:
        (kdim, m) = a.shape if ta else a.shape[::-1]
    if b_halves:
        kdim2, n = b.shape[1], 2 * b.shape[2]
    else:
        (n, kdim2) = b.shape if tb else b.shape[::-1]
    assert kdim == kdim2, (a.shape, b.shape, ta, tb)
    tk = _largest_tile(kdim // 2 if a_halves else kdim, _MM_TK_CAP)
    tn = _largest_tile(n // 4 if out_quarters else (n // 2 if b_halves else n), _MM_TN_CAP)
    nk = kdim // tk
    ab, bb, ob = a.dtype.itemsize, b.dtype.itemsize, jnp.dtype(out_dtype).itemsize
    tm = _largest_tile(m, _MM_TM_CAP)
    vmem = lambda t: 2 * (t * tk * ab + tk * tn * bb + t * tn * ob) + (t * tn * 4 if nk > 1 else 0)
    while vmem(tm) > _MM_VMEM_BUDGET and tm > _LANES:
        tm = _largest_tile(m, tm - _LANES)
    dims = (((0 if ta else 1,), (1 if tb else 0,)), ((), ()))

    def dot(a_ref, b_ref):
        return lax.dot_general(a_ref[...].astype(_MXU_DT), b_ref[...].astype(_MXU_DT), dims,
                               preferred_element_type=F32)

    def body_one(a_ref, b_ref, o_ref):
        o_ref[...] = dot(a_ref, b_ref).astype(o_ref.dtype)

    def body_acc(a_ref, b_ref, o_ref, acc_ref):
        k = pl.program_id(2)

        @pl.when(k == 0)
        def _():
            acc_ref[...] = jnp.zeros_like(acc_ref)

        acc_ref[...] += dot(a_ref, b_ref)

        @pl.when(k == nk - 1)
        def _():
            o_ref[...] = acc_ref[...].astype(o_ref.dtype)

    if a_halves:
        per = nk // 2
        a_spec = pl.BlockSpec((None, tm, tk), lambda i, j, k: (k // per, i, k % per))
    elif ta:
        a_spec = pl.BlockSpec((tk, tm), lambda i, j, k: (k, i))
    else:
        a_spec = pl.BlockSpec((tm, tk), lambda i, j, k: (i, k))
    if b_halves:
        per_b = n // 2 // tn
        b_spec = pl.BlockSpec((None, tk, tn), lambda i, j, k: (j // per_b, k, j % per_b))
    elif tb:
        b_spec = pl.BlockSpec((tn, tk), lambda i, j, k: (j, k))
    else:
        b_spec = pl.BlockSpec((tk, tn), lambda i, j, k: (k, j))
    if out_quarters:
        per_o = n // 4 // tn
        o_spec = pl.BlockSpec((None, tm, tn), lambda i, j, k: (j // per_o, i, j % per_o))
        o_shape = jax.ShapeDtypeStruct((4, m, n // 4), out_dtype)
    else:
        o_spec = pl.BlockSpec((tm, tn), lambda i, j, k: (i, j))
        o_shape = jax.ShapeDtypeStruct((m, n), out_dtype)
    return pl.pallas_call(
        body_one if nk == 1 else body_acc, name=name, grid=(m // tm, n // tn, nk),
        in_specs=[a_spec, b_spec], out_specs=o_spec, out_shape=o_shape,
        scratch_shapes=[] if nk == 1 else [pltpu.VMEM((tm, tn), F32)],
        compiler_params=_cparams(("parallel", "parallel", "arbitrary")),
    )(a, b)


def _row_spec(tm, width, idx):
    return pl.BlockSpec((tm, width), lambda i: (i, idx))


def _full_spec(shape):
    zeros = (0,) * len(shape)
    return pl.BlockSpec(shape, lambda i: zeros)


def _row_fwd(name, fn, rows, params, outs, tm=256):
    s = rows[0][0].shape[0]
    nr, npar = len(rows), len(params)

    def body(*refs):
        vals = [r[...].astype(F32) for r in refs[:nr + npar]]
        res = fn(*vals)
        for o_ref, r in zip(refs[nr + npar:], res):
            o_ref[...] = r.astype(o_ref.dtype)

    return pl.pallas_call(
        body, name=name, grid=(s // tm,),
        in_specs=[_row_spec(tm, w, i) for _, w, i in rows] + [_full_spec(p.shape) for p in params],
        out_specs=[_row_spec(tm, w, 0) for w, _ in outs],
        out_shape=[jax.ShapeDtypeStruct((s, w), dt) for w, dt in outs],
        compiler_params=_cparams(("parallel",)),
    )(*[r[0] for r in rows], *params)


def _row_bwd(name, fn, rows, params, cots, row_grads, param_grads, tm=256, place=None, into=None):
    s = rows[0][0].shape[0]
    nr, npar, nc = len(rows), len(params), len(cots)
    row_grads = [((idxs,) if isinstance(idxs, int) else tuple(idxs), dt) for idxs, dt in row_grads]
    widths = [sum(rows[i][1] for i in idxs) for idxs, _ in row_grads]

    def body(*refs):
        i = pl.program_id(0)
        vals = [r[...].astype(F32) for r in refs[:nr + npar]]
        cvals = tuple(r[...].astype(F32) for r in refs[nr + npar:nr + npar + nc])
        _, vjp = jax.vjp(fn, *vals)
        grads = vjp(cvals)
        outs = refs[nr + npar + nc + (into is not None):]
        for o_ref, (idxs, _) in zip(outs, row_grads):
            pos = 0
            for idx in idxs:
                o_ref[:, pos:pos + rows[idx][1]] = grads[idx].astype(o_ref.dtype)
                pos += rows[idx][1]
        for o_ref, idx in zip(outs[len(row_grads):], param_grads):
            @pl.when(i == 0)
            def _(o_ref=o_ref):
                o_ref[...] = jnp.zeros_like(o_ref)

            o_ref[...] += grads[nr + idx]

    out_specs = [_row_spec(tm, w, 0) for w in widths] + [_full_spec(params[idx].shape) for idx in param_grads]
    out_shape = ([jax.ShapeDtypeStruct((s, w), dt) for w, (_, dt) in zip(widths, row_grads)]
                 + [jax.ShapeDtypeStruct(params[idx].shape, F32) for idx in param_grads])
    extra, aliases = [], {}
    if place is not None:
        out_specs[0] = _row_spec(tm, widths[0], place[1])
        out_shape[0] = jax.ShapeDtypeStruct((s, place[0]), row_grads[0][1])
    if into is not None:
        extra, aliases = [into], {nr + npar + nc: 0}
    return pl.pallas_call(
        body, name=name, grid=(s // tm,),
        in_specs=([_row_spec(tm, w, i) for _, w, i in rows] + [_full_spec(p.shape) for p in params]
                  + [_row_spec(tm, w, i) for _, w, i in cots] + [_ANY] * len(extra)),
        out_specs=out_specs, out_shape=out_shape, input_output_aliases=aliases,
        compiler_params=_cparams(("arbitrary",)),
    )(*[r[0] for r in rows], *params, *[c[0] for c in cots], *extra)


def _rms(x, g):
    return x * lax.rsqrt(jnp.mean(x * x, axis=-1, keepdims=True) + EPS) * g


def _ln(x, g, b):
    mu = jnp.mean(x, axis=-1, keepdims=True)
    xc = x - mu
    var = jnp.mean(xc * xc, axis=-1, keepdims=True)
    return xc * lax.rsqrt(var + EPS) * g + b


def _sigmoid(x):
    return 1.0 / (1.0 + jnp.exp(-x))


def _gelu(x):
    return x * (0.5 * (1.0 + jnp.tanh(math.sqrt(2.0 / math.pi) * (x + 0.044715 * (x * x * x)))))


def _f_rms(x, g):
    return (_rms(x, g),)


def _f_x_rms(x, g):
    return (x, _rms(x, g))


def _f_ln_silu(z, g, b):
    y = _ln(z, g, b)
    return (y * _sigmoid(y),)


def _f_sg_pre(bu, bv, g, b):
    return (_gelu(bu), _ln(_gelu(bv), g, b))


def _f_merge(g0, g1, g2, ya, yb, yc):
    return (_sigmoid(g0) * ya + _sigmoid(g1) * yb + _sigmoid(g2) * yc,)


def _f_resid_rms(x, t, g_post):
    return (x + _rms(t, g_post),)


def _f_resid_rms_rms(x, t, g_post, g_next):
    x1 = x + _rms(t, g_post)
    return (x1, _rms(x1, g_next))


def _f_geglu(zg, zv):
    return _gelu(zg) * zv


_CONV_TILE_ELEMS = 16 * 1024


def _conv_tr(c):
    return _CONV_TILE_ELEMS // c


def _conv_tile(zp_ref, w_ref, bias, k_taps, off, r0):
    c = zp_ref.shape[1]
    tr = _conv_tr(c)
    acc = jnp.broadcast_to(bias, (tr, c))
    for k in range(k_taps):
        acc = acc + w_ref[k:k + 1, :] * zp_ref[r0 + off + k:r0 + off + k + tr, :]
    return acc


def _conv_bwd_input_tile(dzp_ref, w_ref, k_taps, r0):
    c = dzp_ref.shape[1]
    tr = _conv_tr(c)
    acc = jnp.zeros((tr, c), F32)
    for k in range(k_taps):
        s0 = r0 + (k_taps - 1) - k
        acc = acc + w_ref[k:k + 1, :] * dzp_ref[s0:s0 + tr, :]
    return acc


def _conv_bwd_weight(dzp_ref, zp_ref, dw_ref, db_ref, k_taps, off, s):
    c = zp_ref.shape[1]
    tr = _conv_tr(c)
    fold = lambda v: jnp.sum(v.reshape(tr // 8, 8, c), axis=0)
    for k in range(k_taps):
        acc = jnp.zeros((8, c), F32)
        for r in range(s // tr):
            r0 = r * tr
            acc = acc + fold(dzp_ref[r0:r0 + tr, :] * zp_ref[r0 + off + k:r0 + off + k + tr, :])
        dw_ref[k:k + 1, :] = jnp.sum(acc, axis=0, keepdims=True)
    acc = jnp.zeros((8, c), F32)
    for r in range(s // tr):
        acc = acc + fold(dzp_ref[r * tr:(r + 1) * tr, :])
    db_ref[...] = jnp.sum(acc, axis=0, keepdims=True)


def _glu_conv_fwd(p, blk0, w, b):
    s = p.shape[0]
    k_taps, c = w.shape
    cb, pad = _LANES, 32
    off = pad - (k_taps - 1)

    def body(a_ref, w_ref, b_ref, o_ref, zp_ref):
        zp_ref[0:pad, :] = jnp.zeros((pad, cb), F32)
        zp_ref[pad:pad + s, :] = a_ref[:, 0:cb] * _sigmoid(a_ref[:, cb:2 * cb])
        tr = _conv_tr(cb)
        for r in range(s // tr):
            o_ref[r * tr:(r + 1) * tr, :] = _conv_tile(zp_ref, w_ref, b_ref[...], k_taps, off, r * tr)

    return pl.pallas_call(
        body, name="glu_conv_fwd", grid=(c // cb,),
        in_specs=[pl.BlockSpec((s, 2 * cb), lambda j: (0, blk0 + j)),
                  pl.BlockSpec((k_taps, cb), lambda j: (0, j)), pl.BlockSpec((1, cb), lambda j: (0, j))],
        out_specs=pl.BlockSpec((s, cb), lambda j: (0, j)),
        out_shape=jax.ShapeDtypeStruct((s, c), F32),
        scratch_shapes=[pltpu.VMEM((s + pad, cb), F32)],
        compiler_params=_cparams(("parallel",)),
    )(p, w, b)


def _glu_conv_bwd(p, blk0, w, dz, dp):
    s = p.shape[0]
    k_taps, c = w.shape
    cb, pad = _LANES, 32
    off = pad - (k_taps - 1)

    def body(a_ref, w_ref, dz_ref, dp_in, da_ref, dw_ref, db_ref, zp_ref, dzp_ref):
        zp_ref[0:pad, :] = jnp.zeros((pad, cb), F32)
        zp_ref[pad:pad + s, :] = a_ref[:, 0:cb] * _sigmoid(a_ref[:, cb:2 * cb])
        dzp_ref[0:s, :] = dz_ref[...]
        dzp_ref[s:s + pad, :] = jnp.zeros((pad, cb), F32)
        tr = _conv_tr(cb)
        for r in range(s // tr):
            rows = slice(r * tr, (r + 1) * tr)
            dz0 = _conv_bwd_input_tile(dzp_ref, w_ref, k_taps, r * tr)
            sg = _sigmoid(a_ref[rows, cb:2 * cb])
            da_ref[rows, 0:cb] = (dz0 * sg).astype(da_ref.dtype)
            da_ref[rows, cb:2 * cb] = (dz0 * a_ref[rows, 0:cb] * sg * (1.0 - sg)).astype(da_ref.dtype)
        _conv_bwd_weight(dzp_ref, zp_ref, dw_ref, db_ref, k_taps, off, s)

    return pl.pallas_call(
        body, name="glu_conv_bwd", grid=(c // cb,),
        in_specs=[pl.BlockSpec((s, 2 * cb), lambda j: (0, blk0 + j)),
                  pl.BlockSpec((k_taps, cb), lambda j: (0, j)), pl.BlockSpec((s, cb), lambda j: (0, j)), _ANY],
        out_specs=[pl.BlockSpec((s, 2 * cb), lambda j: (0, blk0 + j)),
                   pl.BlockSpec((k_taps, cb), lambda j: (0, j)), pl.BlockSpec((1, cb), lambda j: (0, j))],
        out_shape=[jax.ShapeDtypeStruct(dp.shape, dp.dtype),
                   jax.ShapeDtypeStruct((k_taps, c), F32), jax.ShapeDtypeStruct((1, c), F32)],
        scratch_shapes=[pltpu.VMEM((s + pad, cb), F32), pltpu.VMEM((s + pad, cb), F32)],
        input_output_aliases={3: 0},
        compiler_params=_cparams(("parallel",)),
    )(p, w, dz, dp)


def _conv_geglu_fwd(up, w, b):
    s, f2 = up.shape
    f = f2 // 2
    k_taps = w.shape[0]
    cb, pad = _FFN_CB, 8
    off = pad - (k_taps - 1)
    nb = f // cb

    def body(ug_ref, uv_ref, wg_ref, wv_ref, bg_ref, bv_ref, o_ref, z_ref, w_ref, b_ref):
        _pair(w_ref, wg_ref[...], wv_ref[...], cb)
        _pair(b_ref, bg_ref[...], bv_ref[...], cb)
        z_ref[0:pad, :] = jnp.zeros((pad, 2 * cb), F32)
        z_ref[pad:pad + s, 0:cb] = ug_ref[...]
        z_ref[pad:pad + s, cb:2 * cb] = uv_ref[...]
        tr = _conv_tr(2 * cb)
        for r in range(s // tr):
            z = _conv_tile(z_ref, w_ref, b_ref[...], k_taps, off, r * tr)
            o_ref[r * tr:(r + 1) * tr, :] = _f_geglu(z[:, 0:cb], z[:, cb:2 * cb]).astype(o_ref.dtype)

    two = lambda rows_: [pl.BlockSpec((rows_, cb), lambda j: (0, j)), pl.BlockSpec((rows_, cb), lambda j: (0, nb + j))]
    return pl.pallas_call(
        body, name="conv_geglu_fwd", grid=(nb,),
        in_specs=two(s) + two(k_taps) + two(1),
        out_specs=pl.BlockSpec((s, cb), lambda j: (0, j)),
        out_shape=jax.ShapeDtypeStruct((s, f), _MXU_DT),
        scratch_shapes=[pltpu.VMEM((s + pad, 2 * cb), F32), pltpu.VMEM((k_taps, 2 * cb), F32),
                        pltpu.VMEM((1, 2 * cb), F32)],
        compiler_params=_cparams(("parallel",)),
    )(up, up, w, w, b, b)


def _pair(dst_ref, first, second, cb):
    dst_ref[:, 0:cb] = first
    dst_ref[:, cb:2 * cb] = second


def _conv_geglu_bwd(up, w, b, dact):
    s, f2 = up.shape
    f = f2 // 2
    k_taps = w.shape[0]
    cb, pad = _FFN_CB, 8
    off = pad - (k_taps - 1)
    nb = f // cb

    def body(ug_ref, uv_ref, wg_ref, wv_ref, bg_ref, bv_ref, da_ref, du_ref, dw_ref, db_ref, z_ref, dz_ref, w_ref,
             b_ref, dw_sc, db_sc):
        _pair(w_ref, wg_ref[...], wv_ref[...], cb)
        _pair(b_ref, bg_ref[...], bv_ref[...], cb)
        z_ref[0:pad, :] = jnp.zeros((pad, 2 * cb), F32)
        z_ref[pad:pad + s, 0:cb] = ug_ref[...]
        z_ref[pad:pad + s, cb:2 * cb] = uv_ref[...]
        dz_ref[s:s + pad, :] = jnp.zeros((pad, 2 * cb), F32)
        tr = _conv_tr(2 * cb)
        for r in range(s // tr):
            rows = slice(r * tr, (r + 1) * tr)
            z = _conv_tile(z_ref, w_ref, b_ref[...], k_taps, off, r * tr)
            _, vjp = jax.vjp(_f_geglu, z[:, 0:cb], z[:, cb:2 * cb])
            dzg, dzv = vjp(da_ref[rows, :].astype(F32))
            dz_ref[rows, 0:cb] = dzg
            dz_ref[rows, cb:2 * cb] = dzv
        for r in range(s // tr):
            rows = slice(r * tr, (r + 1) * tr)
            du = _conv_bwd_input_tile(dz_ref, w_ref, k_taps, r * tr).astype(du_ref.dtype)
            du_ref[0, rows, :] = du[:, 0:cb]
            du_ref[1, rows, :] = du[:, cb:2 * cb]
        _conv_bwd_weight(dz_ref, z_ref, dw_sc, db_sc, k_taps, off, s)
        for half in range(2):
            dw_ref[half] = dw_sc[:, half * cb:(half + 1) * cb]
            db_ref[half] = db_sc[:, half * cb:(half + 1) * cb]

    two = lambda rows_: [pl.BlockSpec((rows_, cb), lambda j: (0, j)), pl.BlockSpec((rows_, cb), lambda j: (0, nb + j))]
    both = lambda rows_: pl.BlockSpec((2, rows_, cb), lambda j: (0, 0, j))
    return pl.pallas_call(
        body, name="conv_geglu_bwd", grid=(nb,),
        in_specs=two(s) + two(k_taps) + two(1) + [pl.BlockSpec((s, cb), lambda j: (0, j))],
        out_specs=[both(s), both(k_taps), both(1)],
        out_shape=[jax.ShapeDtypeStruct((2, s, f), _MXU_DT), jax.ShapeDtypeStruct((2, k_taps, f), F32),
                   jax.ShapeDtypeStruct((2, 1, f), F32)],
        scratch_shapes=[pltpu.VMEM((s + pad, 2 * cb), F32), pltpu.VMEM((s + pad, 2 * cb), F32),
                        pltpu.VMEM((k_taps, 2 * cb), F32), pltpu.VMEM((1, 2 * cb), F32),
                        pltpu.VMEM((k_taps, 2 * cb), F32), pltpu.VMEM((1, 2 * cb), F32)],
        compiler_params=_cparams(("parallel",)),
    )(up, up, w, w, b, b, dact)


def _tril_mask():
    t = lax.broadcasted_iota(jnp.int32, (SG_CHUNK, SG_CHUNK), 0)
    s = lax.broadcasted_iota(jnp.int32, (SG_CHUNK, SG_CHUNK), 1)
    return t >= s


def _sg_mix_fwd(u, vn, w, bcol):
    s, c = u.shape
    gw = c // SG_GROUPS

    def body(u_ref, v_ref, w_ref, b_ref, o_ref):
        wm = jnp.where(_tril_mask(), w_ref[0], 0.0).astype(_MXU_DT)
        for n in range(s // SG_CHUNK):
            rows = slice(n * SG_CHUNK, (n + 1) * SG_CHUNK)
            mixed = jnp.dot(wm, v_ref[rows, :], preferred_element_type=F32) + b_ref[0]
            o_ref[rows, :] = (u_ref[rows, :] * mixed).astype(o_ref.dtype)

    return pl.pallas_call(
        body, name="sg_mix_fwd", grid=(SG_GROUPS,),
        in_specs=[pl.BlockSpec((s, gw), lambda g: (0, g)), pl.BlockSpec((s, gw), lambda g: (0, g)),
                  pl.BlockSpec((1, SG_CHUNK, SG_CHUNK), lambda g: (g, 0, 0)),
                  pl.BlockSpec((1, SG_CHUNK, 1), lambda g: (g, 0, 0))],
        out_specs=pl.BlockSpec((s, gw), lambda g: (0, g)),
        out_shape=jax.ShapeDtypeStruct((s, c), _MXU_DT),
        compiler_params=_cparams(("parallel",)),
    )(u, vn, w, bcol)


def _sg_mix_bwd(u, vn, w, bcol, dub):
    s, c = u.shape
    gw = c // SG_GROUPS

    def body(u_ref, v_ref, w_ref, b_ref, d_ref, du_ref, dv_ref, dw_ref, db_ref):
        mask = _tril_mask()
        wm = jnp.where(mask, w_ref[0], 0.0).astype(_MXU_DT)
        dw = jnp.zeros((SG_CHUNK, SG_CHUNK), F32)
        db = jnp.zeros((SG_CHUNK, 1), F32)
        for n in range(s // SG_CHUNK):
            rows = slice(n * SG_CHUNK, (n + 1) * SG_CHUNK)
            v = v_ref[rows, :]
            d = d_ref[rows, :].astype(F32)
            mixed = jnp.dot(wm, v, preferred_element_type=F32) + b_ref[0]
            du_ref[rows, :] = d * mixed
            dmix = d * u_ref[rows, :]
            dmix_lo = dmix.astype(_MXU_DT)
            dv_ref[rows, :] = lax.dot_general(wm, dmix_lo, (((0,), (0,)), ((), ())), preferred_element_type=F32)
            dw = dw + lax.dot_general(dmix_lo, v, (((1,), (1,)), ((), ())), preferred_element_type=F32)
            db = db + jnp.sum(dmix, axis=1, keepdims=True)
        dw_ref[0] = jnp.where(mask, dw, 0.0)
        db_ref[0] = db

    return pl.pallas_call(
        body, name="sg_mix_bwd", grid=(SG_GROUPS,),
        in_specs=[pl.BlockSpec((s, gw), lambda g: (0, g)), pl.BlockSpec((s, gw), lambda g: (0, g)),
                  pl.BlockSpec((1, SG_CHUNK, SG_CHUNK), lambda g: (g, 0, 0)),
                  pl.BlockSpec((1, SG_CHUNK, 1), lambda g: (g, 0, 0)), pl.BlockSpec((s, gw), lambda g: (0, g))],
        out_specs=[pl.BlockSpec((s, gw), lambda g: (0, g)), pl.BlockSpec((s, gw), lambda g: (0, g)),
                   pl.BlockSpec((1, SG_CHUNK, SG_CHUNK), lambda g: (g, 0, 0)),
                   pl.BlockSpec((1, SG_CHUNK, 1), lambda g: (g, 0, 0))],
        out_shape=[jax.ShapeDtypeStruct((s, c), F32), jax.ShapeDtypeStruct((s, c), F32),
                   jax.ShapeDtypeStruct((SG_GROUPS, SG_CHUNK, SG_CHUNK), F32),
                   jax.ShapeDtypeStruct((SG_GROUPS, SG_CHUNK, 1), F32)],
        compiler_params=_cparams(("parallel",)),
    )(u, vn, w, bcol, dub)


def _rope_fwd(q2, kv2, p, krm_idx, krs_idx, tc, ts):
    s = q2.shape[0]
    hw = N_HEADS * HEAD_PAD
    tm = 256

    def body(qm_ref, qs_ref, kn_ref, v_ref, krm_ref, krs_ref, tc_ref, ts_ref, q_ref, k_ref, vo_ref):
        tcv, tsv = tc_ref[...], ts_ref[...]
        kpe = krm_ref[...] * tcv + krs_ref[...] * tsv
        for h in range(N_HEADS):
            cols = slice(h * HEAD_PAD, (h + 1) * HEAD_PAD)
            q_ref[:, cols] = (qm_ref[:, cols] * tcv + qs_ref[:, cols] * tsv).astype(q_ref.dtype)
            k_ref[:, cols] = (kn_ref[:, cols] + kpe).astype(k_ref.dtype)
        vo_ref[...] = v_ref[...].astype(vo_ref.dtype)

    return pl.pallas_call(
        body, name="rope_fwd", grid=(s // tm,),
        in_specs=[_row_spec(tm, hw, 0), _row_spec(tm, hw, 1), _row_spec(tm, hw, 0), _row_spec(tm, hw, 1),
                  _row_spec(tm, HEAD_PAD, krm_idx), _row_spec(tm, HEAD_PAD, krs_idx),
                  _row_spec(tm, HEAD_PAD, 0), _row_spec(tm, HEAD_PAD, 0)],
        out_specs=[_row_spec(tm, hw, 0)] * 3,
        out_shape=[jax.ShapeDtypeStruct((s, hw), _MXU_DT)] * 3,
        compiler_params=_cparams(("parallel",)),
    )(q2, q2, kv2, kv2, p, p, tc, ts)


def _rope_bwd(dq, dk, dv, tc, ts, dp, kr_blk):
    s = dq.shape[0]
    hw = N_HEADS * HEAD_PAD
    tm = 256

    def body(dq_ref, dk_ref, dv_ref, tc_ref, ts_ref, dp_in, dq2_ref, dkv2_ref, dkr_ref):
        tcv, tsv = tc_ref[...], ts_ref[...]
        dkpe = jnp.zeros((tm, HEAD_PAD), F32)
        for h in range(N_HEADS):
            cols = slice(h * HEAD_PAD, (h + 1) * HEAD_PAD)
            dqh = dq_ref[:, cols]
            dq2_ref[:, cols] = (dqh * tcv).astype(dq2_ref.dtype)
            dq2_ref[:, hw + h * HEAD_PAD:hw + (h + 1) * HEAD_PAD] = (dqh * tsv).astype(dq2_ref.dtype)
            dkpe = dkpe + dk_ref[:, cols]
        dkv2_ref[:, 0:hw] = dk_ref[...].astype(dkv2_ref.dtype)
        dkv2_ref[:, hw:2 * hw] = dv_ref[...].astype(dkv2_ref.dtype)
        dkr_ref[:, 0:HEAD_PAD] = (dkpe * tcv).astype(dkr_ref.dtype)
        dkr_ref[:, HEAD_PAD:2 * HEAD_PAD] = (dkpe * tsv).astype(dkr_ref.dtype)
        dkr_ref[:, 2 * HEAD_PAD:3 * HEAD_PAD] = jnp.zeros((tm, HEAD_PAD), dkr_ref.dtype)

    return pl.pallas_call(
        body, name="rope_bwd", grid=(s // tm,),
        in_specs=[_row_spec(tm, hw, 0)] * 3 + [_row_spec(tm, HEAD_PAD, 0)] * 2 + [_ANY],
        out_specs=[_row_spec(tm, 2 * hw, 0), _row_spec(tm, 2 * hw, 0), _row_spec(tm, 3 * HEAD_PAD, kr_blk)],
        out_shape=[jax.ShapeDtypeStruct((s, 2 * hw), _MXU_DT), jax.ShapeDtypeStruct((s, 2 * hw), _MXU_DT),
                   jax.ShapeDtypeStruct(dp.shape, dp.dtype)],
        input_output_aliases={5: 2},
        compiler_params=_cparams(("parallel",)),
    )(dq, dk, dv, tc, ts, dp)


_ATTN_TQ = 512
_ATTN_SCALE = (QK_NOPE + QK_ROPE) ** -0.5


def _attn_probs(q, k, i):
    s = k.shape[0]
    sc = lax.dot_general(q, k, (((1,), (1,)), ((), ())), preferred_element_type=F32) * _ATTN_SCALE
    row = i * _ATTN_TQ + lax.broadcasted_iota(jnp.int32, (_ATTN_TQ, s), 0)
    col = lax.broadcasted_iota(jnp.int32, (_ATTN_TQ, s), 1)
    sc = jnp.where(row >= col, sc, jnp.finfo(F32).min)
    e = jnp.exp(sc - jnp.max(sc, axis=1, keepdims=True))
    return e * (1.0 / jnp.sum(e, axis=1, keepdims=True))


def _per_query_block(s, fn):
    i = pl.program_id(1)
    for n in range(s // _ATTN_TQ):
        @pl.when(i == n)
        def _(n=n):
            fn(n, (n + 1) * _ATTN_TQ)


def _attn_fwd(q, k, v):
    s = q.shape[0]

    def body(q_ref, k_ref, v_ref, o_ref):
        def block(n, kl):
            p = _attn_probs(q_ref[...], k_ref[0:kl, :], n)
            o_ref[...] = jnp.dot(p.astype(_MXU_DT), v_ref[0:kl, :], preferred_element_type=F32).astype(o_ref.dtype)

        _per_query_block(s, block)

    qspec = pl.BlockSpec((_ATTN_TQ, HEAD_PAD), lambda h, i: (i, h))
    kspec = pl.BlockSpec((s, HEAD_PAD), lambda h, i: (0, h))
    return pl.pallas_call(
        body, name="attn_fwd", grid=(N_HEADS, s // _ATTN_TQ),
        in_specs=[qspec, kspec, kspec], out_specs=qspec,
        out_shape=jax.ShapeDtypeStruct(q.shape, _MXU_DT),
        compiler_params=_cparams(("parallel", "parallel")),
    )(q, k, v)


def _attn_bwd(q, k, v, do):
    s = q.shape[0]

    def body(q_ref, k_ref, v_ref, do_ref, dq_ref, dk_ref, dv_ref):
        i = pl.program_id(1)

        @pl.when(i == 0)
        def _():
            dk_ref[...] = jnp.zeros_like(dk_ref)
            dv_ref[...] = jnp.zeros_like(dv_ref)

        def block(n, kl):
            qv, kv, dov = q_ref[...], k_ref[0:kl, :], do_ref[...]
            p = _attn_probs(qv, kv, n)
            dp = lax.dot_general(dov, v_ref[0:kl, :], (((1,), (1,)), ((), ())), preferred_element_type=F32)
            delta = jnp.sum(p * dp, axis=1, keepdims=True)
            ds = (p * (dp - delta) * _ATTN_SCALE).astype(_MXU_DT)
            dq_ref[...] = jnp.dot(ds, kv, preferred_element_type=F32)
            dk_ref[0:kl, :] += lax.dot_general(ds, qv, (((0,), (0,)), ((), ())), preferred_element_type=F32)
            dv_ref[0:kl, :] += lax.dot_general(p.astype(_MXU_DT), dov, (((0,), (0,)), ((), ())),
                                               preferred_element_type=F32)

        _per_query_block(s, block)

    qspec = pl.BlockSpec((_ATTN_TQ, HEAD_PAD), lambda h, i: (i, h))
    kspec = pl.BlockSpec((s, HEAD_PAD), lambda h, i: (0, h))
    return pl.pallas_call(
        body, name="attn_bwd", grid=(N_HEADS, s // _ATTN_TQ),
        in_specs=[qspec, kspec, kspec, qspec], out_specs=[qspec, kspec, kspec],
        out_shape=[jax.ShapeDtypeStruct(q.shape, F32)] * 3,
        compiler_params=_cparams(("parallel", "arbitrary")),
    )(q, k, v, do)


def _loss_head(y, target):
    s, d = y.shape
    tm = 256

    def body(y_ref, t_ref, loss_ref, dy_ref):
        @pl.when(pl.program_id(0) == 0)
        def _():
            loss_ref[...] = jnp.zeros_like(loss_ref)

        err = y_ref[...] - t_ref[...]
        loss_ref[...] += 0.5 * jnp.sum(jnp.mean(err * err, axis=-1, keepdims=True), axis=0, keepdims=True)
        dy_ref[...] = err * (1.0 / d)

    return pl.pallas_call(
        body, name="loss_head", grid=(s // tm,),
        in_specs=[_row_spec(tm, d, 0), _row_spec(tm, d, 0)],
        out_specs=[_full_spec((1, 1)), _row_spec(tm, d, 0)],
        out_shape=[jax.ShapeDtypeStruct((1, 1), F32), jax.ShapeDtypeStruct((s, d), F32)],
        compiler_params=_cparams(("arbitrary",)),
    )(y, target)


def _adamw_math(w, g, m, v):
    mn = ADAM_B1 * m + (1.0 - ADAM_B1) * g
    vn = ADAM_B2 * v + (1.0 - ADAM_B2) * (g * g)
    m_hat = mn / (1.0 - ADAM_B1 ** ADAM_STEP)
    v_hat = vn / (1.0 - ADAM_B2 ** ADAM_STEP)
    return -ADAM_LR * (m_hat / (jnp.sqrt(v_hat) + ADAM_EPS) + ADAM_WD * w), mn, vn


def _adamw(name, w, g, m, v, layer, into):
    _, k, n = w.shape
    tk, tn = _slab_block(k, n)

    def body(w_ref, g_ref, m_ref, v_ref, *rest):
        d_ref, mo_ref, vo_ref = rest[-3:]
        d_ref[...], mo_ref[...], vo_ref[...] = _adamw_math(w_ref[...], g_ref[...], m_ref[...], v_ref[...])

    spec = pl.BlockSpec((1, tk, tn), lambda j, jn: (layer, j, jn))
    extra = [] if into is None else list(into)
    return pl.pallas_call(
        body, name=name, grid=(k // tk, n // tn), in_specs=[spec] * 4 + [_ANY] * len(extra), out_specs=[spec] * 3,
        out_shape=[jax.ShapeDtypeStruct(w.shape, F32)] * 3,
        input_output_aliases={4 + i: i for i in range(len(extra))},
        compiler_params=_cparams(("parallel", "parallel")),
    )(w, g, m, v, *extra)


_ANY = pl.BlockSpec(memory_space=pl.ANY)


def _mesh_pos():
    return lax.axis_index("x"), lax.axis_index("y"), lax.axis_index("c")


def _other_chips(x, y):
    return [(1 - x, y), (x, 1 - y), (1 - x, 1 - y)]


def _remote(src, dst, send_sem, recv_sem, to):
    return pltpu.make_async_remote_copy(src_ref=src, dst_ref=dst, send_sem=send_sem, recv_sem=recv_sem,
                                        device_id=to, device_id_type=_MESH)


_HBM = pl.BlockSpec(memory_space=pltpu.HBM)
_SEM = pl.BlockSpec(memory_space=pltpu.SEMAPHORE)
_EFFECT = pltpu.SideEffectType.DATAFLOW_SIDE_EFFECTING


def _in_hbm(a):
    return pltpu.with_memory_space_constraint(a, pltpu.HBM)


def _chip_index(chip):
    return 2 * chip[0] + chip[1]


def _ag_forward(name, shards, lands, layer, have_remote):
    n = len(shards)

    def body(*refs):
        ins = refs[:n]
        outs = refs[2 * n:3 * n] if lands is not None else refs[n:2 * n]
        send_sems, recv_sems = refs[-2:]
        x, y, c = _mesh_pos()
        sibling = (x, y, 1 - c)
        chips = _other_chips(x, y)

        def copy(a, k, src, dst, to):
            return _remote(src, dst, send_sems.at[a, k], recv_sems.at[a, k], to)

        own = [copy(a, 6, ins[a], outs[a].at[2 * x + y], sibling) for a in range(n)]
        for cp in own:
            cp.start()

        @pl.when(c == layer)
        def _():
            started = []
            if not have_remote:
                for k, chip in enumerate(chips):
                    for a in range(n):
                        cp = copy(a, k, ins[a], outs[a].at[2 * x + y], (*chip, c))
                        cp.start()
                        started.append(cp)
            for k, chip in enumerate(chips):
                for a in range(n):
                    landed = outs[a].at[_chip_index(chip)]
                    if not have_remote:
                        copy(a, k, ins[a], landed, (*chip, c)).wait_recv()
                    cp = copy(a, 3 + k, landed, landed, sibling)
                    cp.start()
                    started.append(cp)
            for cp in started:
                cp.wait_send()

        @pl.when(c != layer)
        def _():
            for k, chip in enumerate(chips):
                for a in range(n):
                    copy(a, 3 + k, ins[a], outs[a].at[_chip_index(chip)], sibling).wait_recv()

        for cp in own:
            cp.wait()

    out_shape = [jax.ShapeDtypeStruct((4,) + a.shape, a.dtype) for a in shards]
    extra = [] if lands is None else list(lands)
    return pl.pallas_call(
        body, name=name, in_specs=[_ANY] * (n + len(extra)), out_specs=[_ANY] * n,
        out_shape=out_shape, input_output_aliases={n + a: a for a in range(len(extra))},
        scratch_shapes=[pltpu.SemaphoreType.DMA((n, 7)), pltpu.SemaphoreType.DMA((n, 7))],
    )(*shards, *extra)


def _owner_sends(owners, srcs, dsts, send_sems, recv_sems, do):
    x, y, c = _mesh_pos()
    for core in (0, 1):
        mine = [a for a in range(len(srcs)) if owners[a] == core]
        if mine:
            @pl.when(c == core)
            def _(mine=mine):
                for k, chip in enumerate(_other_chips(x, y)):
                    for a in mine:
                        do(_remote(srcs[a](chip, k), dsts[a](chip, k), send_sems.at[3 * a + k],
                                   recv_sems.at[3 * a + k], (*chip, c)))


def _split_start(name, owners, sources, land_shapes, src_of, dst_of, after):
    n = len(sources)

    def body(*refs):
        srcs, lands = refs[:n], refs[n:2 * n]
        send_sems, recv_sems = refs[2 * n + 1], refs[2 * n + 2]
        token = refs[-1]
        _owner_sends(owners, [functools.partial(src_of, srcs[a]) for a in range(n)],
                     [functools.partial(dst_of, lands[a]) for a in range(n)], send_sems, recv_sems,
                     lambda cp: cp.start())
        token[...] = jnp.zeros_like(token)

    lands = [_in_hbm(lax.empty(s.shape, s.dtype)) for s in land_shapes]
    outs = pl.pallas_call(
        body, name=name,
        out_shape=([pltpu.SemaphoreType.DMA((3 * n,)), pltpu.SemaphoreType.DMA((3 * n,))]
                   + [pltpu.HBM(a.shape, a.dtype) for a in sources] + [pltpu.HBM(s.shape, s.dtype) for s in land_shapes]
                   + [jax.ShapeDtypeStruct((8, _LANES), F32)]),
        in_specs=[_HBM] * (2 * n) + [_ANY],
        out_specs=[_SEM, _SEM] + [_HBM] * (2 * n) + [pl.BlockSpec(memory_space=pltpu.VMEM)],
        input_output_aliases={i: 2 + i for i in range(2 * n)},
        compiler_params=pltpu.CompilerParams(has_side_effects=_EFFECT),
    )(*[_in_hbm(a) for a in sources], *lands, after)
    return outs[0], outs[1], outs[2:2 + n], outs[2 + n:2 + 2 * n], outs[-1]


def _split_wait(name, owners, send_sems, recv_sems, sources, lands, after, src_of, dst_of):
    n = len(sources)

    def body(*refs):
        srcs, lnds = refs[:n], refs[n:2 * n]
        s_sems, r_sems = refs[2 * n], refs[2 * n + 1]

        def wait(cp):
            cp.wait_send()
            cp.wait_recv()

        _owner_sends(owners, [functools.partial(src_of, srcs[a]) for a in range(n)],
                     [functools.partial(dst_of, lnds[a]) for a in range(n)], s_sems, r_sems, wait)

    outs = pl.pallas_call(
        body, name=name,
        out_shape=[pltpu.HBM(a.shape, a.dtype) for a in sources] + [pltpu.HBM(a.shape, a.dtype) for a in lands],
        in_specs=[_HBM] * (2 * n) + [_SEM, _SEM] + [_ANY] * len(after), out_specs=[_HBM] * (2 * n),
        input_output_aliases={i: i for i in range(2 * n)},
        compiler_params=pltpu.CompilerParams(has_side_effects=_EFFECT),
    )(*sources, *lands, send_sems, recv_sems, *after)
    return outs[:n], outs[n:]


def _pair_exchange(name, arrays, owners, to_owner, layer=None):
    n = len(arrays)

    def body(*refs):
        ins, outs, (send_sems, recv_sems) = refs[:n], refs[n:2 * n], refs[2 * n:]
        x, y, c = _mesh_pos()
        part = (lambda r: r) if layer is None else (lambda r: r.at[layer])
        copies = [_remote(part(ins[a]), part(outs[a]), send_sems.at[a], recv_sems.at[a], (x, y, 1 - c))
                  for a in range(n)]
        for core in (0, 1):
            sends = [copies[a] for a in range(n) if (owners[a] != core) == to_owner]
            recvs = [copies[a] for a in range(n) if (owners[a] == core) == to_owner]

            @pl.when(c == core)
            def _(sends=sends, recvs=recvs):
                for cp in sends:
                    cp.start()
                for cp in recvs:
                    cp.wait_recv()
                for cp in sends:
                    cp.wait_send()

    return pl.pallas_call(
        body, name=name, in_specs=[_ANY] * n, out_specs=[_ANY] * n,
        out_shape=[jax.ShapeDtypeStruct(g.shape, g.dtype) for g in arrays],
        input_output_aliases={} if to_owner else {a: a for a in range(n)},
        scratch_shapes=[pltpu.SemaphoreType.DMA((n,)), pltpu.SemaphoreType.DMA((n,))],
    )(*arrays)


def _swap_split(name, arrays, lands, owners, sems, after):
    n = len(arrays)
    starting = sems is None

    def body(*refs):
        ins, lnds = refs[:n], refs[n:2 * n]
        send_sems, recv_sems = (refs[2 * n + 1], refs[2 * n + 2]) if starting else (refs[2 * n], refs[2 * n + 1])
        x, y, c = _mesh_pos()
        copies = [_remote(ins[a], lnds[a], send_sems.at[a], recv_sems.at[a], (x, y, 1 - c)) for a in range(n)]
        for core in (0, 1):
            sends = [copies[a] for a in range(n) if owners[a] != core]
            recvs = [copies[a] for a in range(n) if owners[a] == core]

            @pl.when(c == core)
            def _(sends=sends, recvs=recvs):
                if starting:
                    for cp in sends:
                        cp.start()
                else:
                    for cp in recvs:
                        cp.wait_recv()
                    for cp in sends:
                        cp.wait_send()

        if starting:
            refs[-1][...] = jnp.zeros_like(refs[-1])

    hbm = lambda arrs: [pltpu.HBM(a.shape, a.dtype) for a in arrs]
    if starting:
        zones = [_in_hbm(lax.empty(s.shape, s.dtype)) for s in lands]
        outs = pl.pallas_call(
            body, name=name,
            out_shape=([pltpu.SemaphoreType.DMA((n,)), pltpu.SemaphoreType.DMA((n,))] + hbm(arrays) + hbm(lands)
                       + [jax.ShapeDtypeStruct((8, _LANES), F32)]),
            in_specs=[_HBM] * (2 * n) + [_ANY],
            out_specs=[_SEM, _SEM] + [_HBM] * (2 * n) + [pl.BlockSpec(memory_space=pltpu.VMEM)],
            input_output_aliases={i: 2 + i for i in range(2 * n)},
            compiler_params=pltpu.CompilerParams(has_side_effects=_EFFECT),
        )(*[_in_hbm(a) for a in arrays], *zones, after)
        return outs[0], outs[1], outs[2:2 + n], outs[2 + n:2 + 2 * n], outs[-1]
    outs = pl.pallas_call(
        body, name=name, out_shape=hbm(arrays) + hbm(lands),
        in_specs=[_HBM] * (2 * n) + [_SEM, _SEM] + [_ANY] * len(after), out_specs=[_HBM] * (2 * n),
        input_output_aliases={i: i for i in range(2 * n)},
        compiler_params=pltpu.CompilerParams(has_side_effects=_EFFECT),
    )(*arrays, *lands, *sems, *after)
    return outs[:n], outs[n:]


def _add_pair(name, g, a, flags):
    _, k, n = g.shape
    tk, tn = _slab_block(k, n)

    def body(flags_ref, g_ref, a_ref, o_ref):
        o_ref[...] = (g_ref[...] + a_ref[...]).astype(o_ref.dtype)

    spec = pl.BlockSpec((1, tk, tn), lambda j, i, jn, fl: (j * fl[1], i * fl[1], jn * fl[1]))
    return pl.pallas_call(
        body, name=name,
        grid_spec=pltpu.PrefetchScalarGridSpec(num_scalar_prefetch=1, grid=(4, k // tk, n // tn),
                                               in_specs=[spec, spec], out_specs=spec),
        out_shape=jax.ShapeDtypeStruct(g.shape, _RS_DT),
        compiler_params=_cparams(("arbitrary", "arbitrary", "arbitrary")),
    )(flags, g, a)


def _add_quads(name, t, b, layer, flags, into, after=None):
    _, k, n = t.shape
    tk, tn = _slab_block(k, n)

    def body(flags_ref, t_ref, b_ref, *rest):
        o_ref = rest[-1]
        f = lambda v: v.astype(F32)
        o_ref[0] = ((f(t_ref[0]) + f(b_ref[0])) + f(b_ref[1])) + f(b_ref[2])

    extra = ([] if into is None else [into]) + ([] if after is None else [after])
    return pl.pallas_call(
        body, name=name,
        grid_spec=pltpu.PrefetchScalarGridSpec(
            num_scalar_prefetch=1, grid=(k // tk, n // tn),
            in_specs=[pl.BlockSpec((1, tk, tn), lambda i, jn, fl: (fl[0], i * fl[1], jn * fl[1])),
                      pl.BlockSpec((3, tk, tn), lambda i, jn, fl: (0, i * fl[1], jn * fl[1]))] + [_ANY] * len(extra),
            out_specs=pl.BlockSpec((1, tk, tn), lambda i, jn, fl: (layer, i * fl[1], jn * fl[1]))),
        out_shape=jax.ShapeDtypeStruct((2, k, n), F32),
        input_output_aliases={} if into is None else {3: 0},
        compiler_params=_cparams(("arbitrary", "arbitrary")),
    )(flags, t, b, *extra)


def _slab_block(k, n, itemsize=4):
    tk = (1 << 20) // (n * itemsize) // 16 * 16
    while 0 < tk < k and k % tk:
        tk -= 16
    if 0 < tk < k:
        return tk, n
    if k * n * itemsize <= (2 << 20) or n % _LANES:
        return k, n
    tn = max(_LANES, (2 << 20) // (k * itemsize) // _LANES * _LANES)
    while n % tn:
        tn -= _LANES
    return k, tn


def _all_reduce_adamw(gs, ws, ms, vs):
    n = len(gs)

    def body(*refs):
        g_refs, w_refs, m_refs, v_refs = (refs[i * n:(i + 1) * n] for i in range(4))
        gsum, delta, m_out, v_out = (refs[(4 + i) * n:(5 + i) * n] for i in range(4))
        slots = refs[8 * n:9 * n]
        send_sems, recv_sems = refs[9 * n:]
        x, y, c = _mesh_pos()
        me = 4 * x + 2 * y + c
        copies = []
        for rel in range(1, 8):
            bx, by, bc = (rel >> 2) & 1, (rel >> 1) & 1, rel & 1
            peer = (1 - x if bx else x, 1 - y if by else y, 1 - c if bc else c)
            for a in range(n):
                cp = _remote(g_refs[a], slots[a].at[me], send_sems.at[a, rel - 1], recv_sems.at[a, rel - 1], peer)
                cp.start()
                copies.append(cp)
        for a in range(n):
            slots[a][me] = g_refs[a][...]
        for cp in copies:
            cp.wait()
        for a in range(n):
            acc = slots[a][0]
            for d in range(1, 8):
                acc = acc + slots[a][d]
            gsum[a][...] = acc
            delta[a][...], m_out[a][...], v_out[a][...] = _adamw_math(w_refs[a][...], acc, m_refs[a][...],
                                                                      v_refs[a][...])

    vmem = pl.BlockSpec(memory_space=pltpu.VMEM)
    outs = pl.pallas_call(
        body, name="all_reduce_adamw", in_specs=[vmem] * (4 * n), out_specs=[vmem] * (4 * n),
        out_shape=[jax.ShapeDtypeStruct(g.shape, F32) for g in gs] * 4,
        scratch_shapes=([pltpu.VMEM((8,) + g.shape, F32) for g in gs]
                        + [pltpu.SemaphoreType.DMA((n, 7)), pltpu.SemaphoreType.DMA((n, 7))]),
        compiler_params=pltpu.CompilerParams(vmem_limit_bytes=_VMEM_LIMIT),
    )(*gs, *ws, *ms, *vs)
    return outs[:n], outs[n:2 * n], outs[2 * n:3 * n], outs[3 * n:]


def _swap_rope(a):
    h = QK_ROPE // 2
    return jnp.concatenate([a[..., h:], a[..., :h]], axis=-1)


def _swap_rope_rows(a):
    h = QK_ROPE // 2
    return jnp.concatenate([a[h:], a[:h]], axis=0)


_FFN_CB = 256


def _interleave_rows(a, cb):
    r, c = a.shape
    return a.reshape(2, r // (2 * cb), cb, c).transpose(1, 0, 2, 3).reshape(r, c)


def _deinterleave_rows(a, cb):
    r, c = a.shape
    return a.reshape(r // (2 * cb), 2, cb, c).transpose(1, 0, 2, 3).reshape(r, c)


class _InLayout:
    def __init__(self, d):
        self.d = d
        self.gates = 0
        self.a = 3 * d
        self.b = 4 * d
        self.kv = 5 * d
        self.q = self.kv + 256
        self.krm = self.q + 384
        self.krs = self.krm + HEAD_PAD
        self.width = self.krs + 2 * HEAD_PAD


def _prep_layer(wl, d):
    lay = _InLayout(d)
    w_in = wl["w_in"]
    dt = w_in.dtype
    a, b = w_in[0:d], w_in[d:2 * d]
    q, kv = w_in[2 * d:2 * d + 384], w_in[2 * d + 384:2 * d + 640]
    kr = w_in[2 * d + 640:2 * d + 640 + QK_ROPE]
    gates = w_in[2 * d + 640 + QK_ROPE:]
    z = lambda n: jnp.zeros((n, d), dt)
    krm = jnp.concatenate([z(QK_NOPE), kr, z(HEAD_PAD - QK_NOPE - QK_ROPE)], axis=0)
    krs = jnp.concatenate([z(QK_NOPE), _swap_rope_rows(kr), z(HEAD_PAD - QK_NOPE - QK_ROPE)], axis=0)
    out = dict(wl)
    out["w_in"] = jnp.concatenate([gates, _interleave_rows(a, _LANES), b, kv, q, krm, krs,
                                   z(lay.width - lay.krs - HEAD_PAD)], axis=0)
    uq = wl["mla_w_uq"].reshape(-1, N_HEADS, QK_NOPE + QK_ROPE)
    nq = uq.shape[0]
    nope, pe = uq[..., :QK_NOPE], uq[..., QK_NOPE:]
    zq = lambda n: jnp.zeros((nq, N_HEADS, n), dt)
    main = jnp.concatenate([nope, pe, zq(HEAD_PAD - QK_NOPE - QK_ROPE)], axis=-1).reshape(nq, -1)
    swapped = jnp.concatenate([zq(QK_NOPE), _swap_rope(pe), zq(HEAD_PAD - QK_NOPE - QK_ROPE)], axis=-1).reshape(nq, -1)
    out["mla_w_uq"] = jnp.concatenate([main, swapped], axis=1)
    ukv = wl["mla_w_ukv"].reshape(-1, N_HEADS, QK_NOPE + V_HEAD)
    nkv = ukv.shape[0]
    zk = jnp.zeros((nkv, N_HEADS, HEAD_PAD - QK_NOPE), dt)
    zv = jnp.zeros((nkv, N_HEADS, HEAD_PAD - V_HEAD), dt)
    out["mla_w_ukv"] = jnp.concatenate([jnp.concatenate([ukv[..., :QK_NOPE], zk], axis=-1).reshape(nkv, -1),
                                        jnp.concatenate([ukv[..., QK_NOPE:], zv], axis=-1).reshape(nkv, -1)], axis=1)
    wo = wl["mla_w_o"].reshape(N_HEADS, V_HEAD, -1)
    out["mla_w_o"] = jnp.concatenate([wo, jnp.zeros((N_HEADS, HEAD_PAD - V_HEAD, wo.shape[-1]), dt)],
                                     axis=1).reshape(N_HEADS * HEAD_PAD, -1)
    return out


def _unprep_grads(g, d):
    lay = _InLayout(d)
    gi = g["w_in"]
    kr = (gi[lay.krm + QK_NOPE:lay.krm + QK_NOPE + QK_ROPE]
          + _swap_rope_rows(gi[lay.krs + QK_NOPE:lay.krs + QK_NOPE + QK_ROPE]))
    out = dict(g)
    out["w_in"] = jnp.concatenate([_deinterleave_rows(gi[lay.a:lay.a + d], _LANES), gi[lay.b:lay.b + d],
                                   gi[lay.q:lay.q + 384], gi[lay.kv:lay.kv + 256], kr,
                                   gi[lay.gates:lay.gates + 3 * d]], axis=0)
    hw = N_HEADS * HEAD_PAD
    gq = g["mla_w_uq"]
    nq = gq.shape[0]
    main = gq[:, :hw].reshape(nq, N_HEADS, HEAD_PAD)
    swapped = gq[:, hw:].reshape(nq, N_HEADS, HEAD_PAD)
    pe = main[..., QK_NOPE:QK_NOPE + QK_ROPE] + _swap_rope(swapped[..., QK_NOPE:QK_NOPE + QK_ROPE])
    out["mla_w_uq"] = jnp.concatenate([main[..., :QK_NOPE], pe], axis=-1).reshape(nq, -1)
    gkv = g["mla_w_ukv"]
    nkv = gkv.shape[0]
    out["mla_w_ukv"] = jnp.concatenate([gkv[:, :hw].reshape(nkv, N_HEADS, HEAD_PAD)[..., :QK_NOPE],
                                        gkv[:, hw:].reshape(nkv, N_HEADS, HEAD_PAD)[..., :V_HEAD]],
                                       axis=-1).reshape(nkv, -1)
    go = g["mla_w_o"]
    out["mla_w_o"] = go.reshape(N_HEADS, HEAD_PAD, -1)[:, :V_HEAD].reshape(N_HEADS * V_HEAD, -1)
    return out


def _rope_tables(positions):
    s = positions.shape[0]
    inv = ROPE_THETA ** (-jnp.arange(0, QK_ROPE, 2, dtype=F32) / QK_ROPE)
    ang = positions.astype(F32)[:, None] * inv
    cos, sin = jnp.cos(ang), jnp.sin(ang)
    tail = jnp.zeros((s, HEAD_PAD - QK_NOPE - QK_ROPE), F32)
    tc = jnp.concatenate([jnp.ones((s, QK_NOPE), F32), cos, cos, tail], axis=1)
    ts = jnp.concatenate([jnp.zeros((s, QK_NOPE), F32), -sin, sin, tail], axis=1)
    return tc, ts


def _row(v):
    return v.reshape(1, -1)


def _layer_fwd(x, h, w, g_next, tc, ts, late_weights=None):
    d = x.shape[1]
    lay = _InLayout(d)
    cw = d // 2
    blk = lambda off, width: off // width
    p = _mm("mm_in", h, w["w_in"], tb=True)
    z1 = _glu_conv_fwd(p, blk(lay.a, 2 * _LANES), w["conv_dw_w"], _row(w["conv_dw_b"]))
    ln_a = [_row(w["conv_ln_g"]), _row(w["conv_ln_b"])]
    (z3,) = _row_fwd("ln_silu_fwd", _f_ln_silu, [(z1, cw, 0)], ln_a, [(cw, _MXU_DT)])
    ya = _mm("mm_conv_out", z3, w["conv_out_w"])
    ln_b = [_row(w["sg_ln_g"]), _row(w["sg_ln_b"])]
    u, vn = _row_fwd("sg_pre_fwd", _f_sg_pre, [(p, cw, blk(lay.b, cw)), (p, cw, blk(lay.b + cw, cw))], ln_b,
                     [(cw, F32), (cw, _MXU_DT)])
    bcol = w["sg_b"].reshape(SG_GROUPS, SG_CHUNK, 1)
    ub = _sg_mix_fwd(u, vn, w["sg_w"], bcol)
    yb = _mm("mm_sg_out", ub, w["sg_out_w"])
    (qn,) = _row_fwd("q_norm_fwd", _f_rms, [(p, 384, blk(lay.q, 384))], [_row(w["mla_q_norm_g"])], [(384, _MXU_DT)])
    (kvn,) = _row_fwd("kv_norm_fwd", _f_rms, [(p, 256, blk(lay.kv, 256))], [_row(w["mla_kv_norm_g"])],
                      [(256, _MXU_DT)])
    q2 = _mm("mm_uq", qn, w["mla_w_uq"])
    kv2 = _mm("mm_ukv", kvn, w["mla_w_ukv"])
    qf, kf, vf = _rope_fwd(q2, kv2, p, blk(lay.krm, HEAD_PAD), blk(lay.krs, HEAD_PAD), tc, ts)
    o = _attn_fwd(qf, kf, vf)
    yc = _mm("mm_o", o, w["mla_w_o"])
    gate_rows = [(p, d, 0), (p, d, 1), (p, d, 2)]
    (merged,) = _row_fwd("merge_fwd", _f_merge, gate_rows + [(ya, d, 0), (yb, d, 0), (yc, d, 0)], [], [(d, _MXU_DT)])
    if late_weights is not None:
        w = {**w, **late_weights(merged)}
    t = _mm("mm_out", merged, w["w_out"])
    x1, h2 = _row_fwd("resid_mix_fwd", _f_resid_rms_rms, [(x, d, 0), (t, d, 0)],
                      [_row(w["mix_post_g"]), _row(w["ffn_pre_g"])], [(d, F32), (d, _MXU_DT)])
    up = _mm("mm_up", h2, w["ffn_w_up"])
    act = _conv_geglu_fwd(up, w["ffn_dw_w"], _row(w["ffn_dw_b"]))
    dn = _mm("mm_down", act, w["ffn_w_down"])
    if g_next is None:
        (x2,) = _row_fwd("resid_ffn_last_fwd", _f_resid_rms, [(x1, d, 0), (dn, d, 0)], [_row(w["ffn_post_g"])],
                         [(d, F32)])
        h_next = None
    else:
        x2, h_next = _row_fwd("resid_ffn_fwd", _f_resid_rms_rms, [(x1, d, 0), (dn, d, 0)],
                              [_row(w["ffn_post_g"]), _row(g_next)], [(d, F32), (d, _MXU_DT)])
    saved = dict(x=x, h=h, p=p, z1=z1, z3=z3, ya=ya, u=u, vn=vn, ub=ub, yb=yb, qn=qn, kvn=kvn, qf=qf, kf=kf, vf=vf, o=o,
                 yc=yc, merged=merged, t=t, x1=x1, h2=h2, up=up, act=act, dn=dn, bcol=bcol)
    return x2, h_next, saved


def _layer_bwd(dx2, dh_next, w, g_next, sv, tc, ts, on_late_grads=None, on_attention_done=None):
    d = dx2.shape[1]
    lay = _InLayout(d)
    cw = d // 2
    blk = lambda off, width: off // width
    lo = _MXU_DT
    g = {}
    x1, dn = sv["x1"], sv["dn"]
    if dh_next is None:
        dx1, ddn, g["ffn_post_g"] = _row_bwd("resid_ffn_last_bwd", _f_resid_rms, [(x1, d, 0), (dn, d, 0)],
                                             [_row(w["ffn_post_g"])], [(dx2, d, 0)], [(0, F32), (1, lo)], [0])
    else:
        dx1, ddn, g["ffn_post_g"], g["next_pre_g"] = _row_bwd(
            "resid_ffn_bwd", _f_resid_rms_rms, [(x1, d, 0), (dn, d, 0)], [_row(w["ffn_post_g"]), _row(g_next)],
            [(dx2, d, 0), (dh_next, d, 0)], [(0, F32), (1, lo)], [0, 1])
    dact = _mm("mm_down_dx", ddn, w["ffn_w_down"], tb=True)
    g["ffn_w_down"] = _mm("mm_down_dw", sv["act"], ddn, ta=True)
    dup, dw_halves, db_halves = _conv_geglu_bwd(sv["up"], w["ffn_dw_w"], _row(w["ffn_dw_b"]), dact)
    g["ffn_dw_w"] = jnp.concatenate([dw_halves[0], dw_halves[1]], axis=1)
    g["ffn_dw_b"] = jnp.concatenate([db_halves[0], db_halves[1]], axis=1)
    dh2 = _mm("mm_up_dx", dup, w["ffn_w_up"], tb=True, a_halves=True)
    g["ffn_w_up"] = _mm("mm_up_dw", sv["h2"], dup, ta=True, b_halves=True, out_quarters=True)
    dx, dt, g["mix_post_g"], g["ffn_pre_g"] = _row_bwd(
        "resid_mix_bwd", _f_resid_rms_rms, [(sv["x"], d, 0), (sv["t"], d, 0)],
        [_row(w["mix_post_g"]), _row(w["ffn_pre_g"])], [(dx1, d, 0), (dh2, d, 0)], [(0, F32), (1, lo)], [0, 1])
    dmerged = _mm("mm_out_dx", dt, w["w_out"], tb=True)
    g["w_out"] = _mm("mm_out_dw", sv["merged"], dt, ta=True)
    if on_late_grads is not None:
        dmerged = on_late_grads(g, dmerged)
    p = sv["p"]
    gate_rows = [(p, d, 0), (p, d, 1), (p, d, 2)]
    dp, dya, dyb, dyc = _row_bwd(
        "merge_bwd", _f_merge, gate_rows + [(sv["ya"], d, 0), (sv["yb"], d, 0), (sv["yc"], d, 0)], [],
        [(dmerged, d, 0)], [((0, 1, 2), lo), ((3,), lo), ((4,), lo), ((5,), lo)], [], place=(lay.width, 0))
    do = _mm("mm_o_dx", dyc, w["mla_w_o"], tb=True, out_dtype=lo)
    g["mla_w_o"] = _mm("mm_o_dw", sv["o"], dyc, ta=True)
    dqf, dkf, dvf = _attn_bwd(sv["qf"], sv["kf"], sv["vf"], do)
    dq2, dkv2, dp = _rope_bwd(dqf, dkf, dvf, tc, ts, dp, blk(lay.krm, 3 * HEAD_PAD))
    dkvn = _mm("mm_ukv_dx", dkv2, w["mla_w_ukv"], tb=True)
    g["mla_w_ukv"] = _mm("mm_ukv_dw", sv["kvn"], dkv2, ta=True)
    dqn = _mm("mm_uq_dx", dq2, w["mla_w_uq"], tb=True)
    g["mla_w_uq"] = _mm("mm_uq_dw", sv["qn"], dq2, ta=True)
    dp, g["mla_q_norm_g"] = _row_bwd("q_norm_bwd", _f_rms, [(p, 384, blk(lay.q, 384))], [_row(w["mla_q_norm_g"])],
                                     [(dqn, 384, 0)], [((0,), lo)], [0], place=(lay.width, blk(lay.q, 384)), into=dp)
    dp, g["mla_kv_norm_g"] = _row_bwd("kv_norm_bwd", _f_rms, [(p, 256, blk(lay.kv, 256))],
                                      [_row(w["mla_kv_norm_g"])], [(dkvn, 256, 0)], [((0,), lo)], [0],
                                      place=(lay.width, blk(lay.kv, 256)), into=dp)
    if on_attention_done is not None:
        dyb = on_attention_done(dyb)
    dub = _mm("mm_sg_out_dx", dyb, w["sg_out_w"], tb=True)
    g["sg_out_w"] = _mm("mm_sg_out_dw", sv["ub"], dyb, ta=True)
    du, dvn, g["sg_w"], dbcol = _sg_mix_bwd(sv["u"], sv["vn"], w["sg_w"], sv["bcol"], dub)
    g["sg_b"] = dbcol.reshape(SG_GROUPS, SG_CHUNK)
    dp, g["sg_ln_g"], g["sg_ln_b"] = _row_bwd(
        "sg_pre_bwd", _f_sg_pre, [(p, cw, blk(lay.b, cw)), (p, cw, blk(lay.b + cw, cw))],
        [_row(w["sg_ln_g"]), _row(w["sg_ln_b"])], [(du, cw, 0), (dvn, cw, 0)], [((0, 1), lo)], [0, 1],
        place=(lay.width, blk(lay.b, d)), into=dp)
    dz3 = _mm("mm_conv_out_dx", dya, w["conv_out_w"], tb=True)
    g["conv_out_w"] = _mm("mm_conv_out_dw", sv["z3"], dya, ta=True)
    dz1, g["conv_ln_g"], g["conv_ln_b"] = _row_bwd(
        "ln_silu_bwd", _f_ln_silu, [(sv["z1"], cw, 0)], [_row(w["conv_ln_g"]), _row(w["conv_ln_b"])],
        [(dz3, cw, 0)], [((0,), F32)], [0, 1])
    dp, g["conv_dw_w"], g["conv_dw_b"] = _glu_conv_bwd(p, blk(lay.a, 2 * _LANES), w["conv_dw_w"], dz1, dp)
    dh = _mm("mm_in_dx", dp, w["w_in"])
    g["w_in"] = _mm("mm_in_dw", dp, sv["h"], ta=True)
    return dx, dh, g


def _local_step(x, positions, target, layers):
    d = x.shape[1]
    tc, ts = _rope_tables(positions)
    ws = [_prep_layer(wl, d) for wl in layers]
    depth = len(ws)
    (h,) = _row_fwd("rms_first_fwd", _f_rms, [(x, d, 0)], [_row(ws[0]["mix_pre_g"])], [(d, _MXU_DT)])
    saved = []
    for l in range(depth):
        g_next = ws[l + 1]["mix_pre_g"] if l + 1 < depth else None
        x, h, sv = _layer_fwd(x, h, ws[l], g_next, tc, ts)
        saved.append(sv)
    loss, dx = _loss_head(x, target)
    grads = [None] * depth
    dh = None
    for l in reversed(range(depth)):
        g_next = ws[l + 1]["mix_pre_g"] if l + 1 < depth else None
        dx, dh, g = _layer_bwd(dx, dh, ws[l], g_next, saved[l], tc, ts)
        if "next_pre_g" in g:
            grads[l + 1]["mix_pre_g"] = g.pop("next_pre_g")
        grads[l] = g
    x0 = saved[0]["x"]
    grad_x, grads[0]["mix_pre_g"] = _row_bwd("rms_first_bwd", _f_x_rms, [(x0, d, 0)], [_row(ws[0]["mix_pre_g"])],
                                             [(dx, d, 0), (dh, d, 0)], [(0, F32)], [0])
    return loss, grad_x, [_unprep_grads(g, d) for g in grads]


_MATRICES = ("w_in", "conv_out_w", "sg_out_w", "mla_w_uq", "mla_w_ukv", "mla_w_o", "w_out", "ffn_w_up", "ffn_w_down")
_F32_GATHERED = ("conv_dw_w", "ffn_dw_w")
_RS_DT = jnp.bfloat16


_ROW_SHARDED = SHARDED_MID + ("w_in",)


_GATHERED = SHARDED + _F32_GATHERED
_RS_CORE0 = ("w_in", "ffn_w_down")
_LATE_WEIGHTS = ("w_out", "ffn_w_up", "ffn_dw_w", "ffn_w_down")


def _layer_shards(w, l):
    hi = {n: w[n][l].astype(jnp.bfloat16) for n in SHARDED}
    lo = [(w[n][l] - hi[n].astype(F32)).astype(jnp.bfloat16) for n in _F32_GATHERED]
    return [hi[n] for n in SHARDED] + lo


def _layer_weights(names, gathered):
    wl = {}
    for n, g in zip(names, gathered):
        if n in _ROW_SHARDED and g.shape[1] % 16 == 0:
            whole = g.reshape(-1, g.shape[2])
        else:
            whole = jnp.concatenate([g[j] for j in range(4)], axis=0 if n in _ROW_SHARDED else 1)
        if n in wl:
            wl[n] = wl[n].astype(F32) + whole.astype(F32)
        else:
            wl[n] = whole.astype(_MXU_DT) if n in _MATRICES else whole
    return wl


def _by_destination(name, gl):
    if gl.ndim == 3:
        return gl
    k, n = gl.shape
    if name in _ROW_SHARDED:
        return gl.reshape(4, k // 4, n)
    return gl.reshape(k, 4, n // 4).transpose(1, 0, 2)


def kernel(x, positions, mix_pre_g, mix_post_g, ffn_pre_g, ffn_post_g, w_in, conv_dw_w, conv_dw_b, conv_ln_g, conv_ln_b, conv_out_w, sg_ln_g, sg_ln_b, sg_w, sg_b, sg_out_w, mla_q_norm_g, mla_w_uq, mla_kv_norm_g, mla_w_ukv, mla_w_o, w_out, ffn_w_up, ffn_dw_w, ffn_dw_b, ffn_w_down, loss_target, m_mix_pre_g, m_mix_post_g, m_ffn_pre_g, m_ffn_post_g, m_w_in, m_conv_dw_w, m_conv_dw_b, m_conv_ln_g, m_conv_ln_b, m_conv_out_w, m_sg_ln_g, m_sg_ln_b, m_sg_w, m_sg_b, m_sg_out_w, m_mla_q_norm_g, m_mla_w_uq, m_mla_kv_norm_g, m_mla_w_ukv, m_mla_w_o, m_w_out, m_ffn_w_up, m_ffn_dw_w, m_ffn_dw_b, m_ffn_w_down, v_mix_pre_g, v_mix_post_g, v_ffn_pre_g, v_ffn_post_g, v_w_in, v_conv_dw_w, v_conv_dw_b, v_conv_ln_g, v_conv_ln_b, v_conv_out_w, v_sg_ln_g, v_sg_ln_b, v_sg_w, v_sg_b, v_sg_out_w, v_mla_q_norm_g, v_mla_w_uq, v_mla_kv_norm_g, v_mla_w_ukv, v_mla_w_o, v_w_out, v_ffn_w_up, v_ffn_dw_w, v_ffn_dw_b, v_ffn_w_down):
    args = dict(locals())
    w = {n: args[n] for n in WEIGHTS}
    m = {n: args["m_" + n] for n in WEIGHTS}
    v = {n: args["v_" + n] for n in WEIGHTS}
    depth = mix_pre_g.shape[0]

    assert depth == 2, "the two cores of a chip split the communication by layer"
    for t in (w, m, v):
        t["w_in"] = jnp.swapaxes(t["w_in"], 1, 2)
    d = x.shape[-1]
    mesh_x, mesh_y, mesh_c = _mesh_pos()
    my_chip = 2 * mesh_x + mesh_y
    names = list(SHARDED)
    whole = lambda ref, chip, k: ref
    to_my_slot = lambda ref, chip, k: ref.at[2 * lax.axis_index("x") + lax.axis_index("y")]
    block_of_chip = lambda ref, chip, k: ref.at[_chip_index(chip)]
    slot_k = lambda ref, chip, k: ref.at[k]

    late = [i for i, n in enumerate(_GATHERED) if n in _LATE_WEIGHTS]
    early = [i for i in range(len(_GATHERED)) if i not in late]
    pick = lambda seq, idx: [seq[i] for i in idx]
    gathered_names = list(_GATHERED)
    shards0, shards1 = _layer_shards(w, 0), _layer_shards(w, 1)
    replicated = lambda l: {n: w[n][l] for n in REPLICATED}
    land_of = lambda shards: [jax.ShapeDtypeStruct((4,) + a.shape, a.dtype) for a in shards]

    gathered0 = _ag_forward("ag_layer0_early", pick(shards0, early), None, 0, False)
    late0 = pick(shards0, late)
    sems0_s, sems0_r, late0, lands0, token0 = _split_start("ag0_start", [0] * len(late), late0, land_of(late0), whole,
                                                           to_my_slot, gathered0[0])
    ag_owner = [1] * len(shards1)
    sems_s, sems_r, shards1, lands1, token = _split_start("ag1_start", ag_owner, shards1, land_of(shards1), whole,
                                                          to_my_slot, gathered0[0])
    tc, ts = _rope_tables(positions[0])
    ws0 = _prep_layer({**replicated(0), **_layer_weights(pick(gathered_names, early), gathered0)}, d)

    def late_weights(merged):
        got = _split_wait("ag0_wait", [0] * len(late), sems0_s, sems0_r, late0, lands0, [merged], whole, to_my_slot)
        got = _ag_forward("ag_layer0_late", got[0], got[1], 0, True)
        ws0.update(_layer_weights(pick(gathered_names, late), got))
        return ws0

    x0 = x[0] + (token0[0, 0] + token[0, 0])
    (h0,) = _row_fwd("rms_first_fwd", _f_rms, [(x0, d, 0)], [_row(ws0["mix_pre_g"])], [(d, _MXU_DT)])
    x1, h1, sv0 = _layer_fwd(x0, h0, ws0, w["mix_pre_g"][1], tc, ts, late_weights)
    shards1, lands1 = _split_wait("ag1_wait", ag_owner, sems_s, sems_r, shards1, lands1, [x1], whole, to_my_slot)
    gathered1 = _ag_forward("ag_layer1", shards1, lands1, 1, True)
    ws1 = _prep_layer({**replicated(1), **_layer_weights(gathered_names, gathered1)}, d)
    x2, _, sv1 = _layer_fwd(x1, h1, ws1, None, tc, ts)
    loss, dx = _loss_head(x2, loss_target[0])
    loss = lax.psum(loss[0, 0], ("x", "y", "c"))

    owner = {n: 0 if n in _RS_CORE0 else 1 for n in names}
    flag = {n: jnp.stack([my_chip, (mesh_c == owner[n]).astype(jnp.int32)]).astype(jnp.int32) for n in names}
    owners = [owner[n] for n in names]

    def scatter_start(tag, group, gd, got, after):
        own = [owner[n] for n in group]
        t = [_add_pair("rs_pair%s_%s" % (tag, n), g, a, flag[n]) for n, g, a in zip(group, gd, got)]
        lands = [jax.ShapeDtypeStruct((3,) + a.shape[1:], a.dtype) for a in t]
        sems_s, sems_r, t, b, token = _split_start("rs%s_start" % tag, own, t, lands, block_of_chip, slot_k, after)
        return (tag, group, own, sems_s, sems_r, t, b), token

    def scatter_wait(handle, after):
        tag, group, own, sems_s, sems_r, t, b = handle
        t, b = _split_wait("rs%s_wait" % tag, own, sems_s, sems_r, t, b, after, block_of_chip, slot_k)
        return dict(zip(group, zip(t, b)))

    def swap_and_start(tag, group, grads_l, after):
        gd = [_by_destination(n, grads_l[n]) for n in group]
        got = _pair_exchange("rs_swap" + tag, gd, [owner[n] for n in group], True)
        return scatter_start(tag, group, gd, got, after)

    dx, dh, gk1 = _layer_bwd(dx, None, ws1, None, sv1, tc, ts)
    grads1 = _unprep_grads(gk1, d)
    gd1 = [_by_destination(n, grads1[n]) for n in names]
    swap1 = _swap_split("rs_swap1_start", gd1, [jax.ShapeDtypeStruct(g.shape, g.dtype) for g in gd1], owners, None,
                        dh)
    dx = dx + swap1[4][0, 0]
    late_group = [n for n in names if n in _LATE_WEIGHTS]
    early_group = [n for n in names if n not in _LATE_WEIGHTS]
    handles = []

    own_late = [owner[n] for n in late_group]
    pending = []

    def on_late_grads(g, value):
        gd, got = _swap_split("rs_swap1_wait", swap1[2], swap1[3], owners, swap1[:2], [value])
        handle1, tok1 = scatter_start("1", names, gd, got, value)
        handles.append(handle1)
        gd = [_by_destination(n, g[n]) for n in late_group]
        swap = _swap_split("rs_swap0_late_start", gd, [jax.ShapeDtypeStruct(a.shape, a.dtype) for a in gd], own_late,
                           None, value)
        pending.append(swap)
        return value + (tok1[0, 0] + swap[4][0, 0])

    def on_attention_done(value):
        swap = pending[0]
        gd, got = _swap_split("rs_swap0_late_wait", swap[2], swap[3], own_late, swap[:2], [value])
        handle0, tok0 = scatter_start("0_late", late_group, gd, got, value)
        handles.append(handle0)
        return value + tok0[0, 0].astype(value.dtype)

    dx, dh, gk0 = _layer_bwd(dx, dh, ws0, ws1["mix_pre_g"], sv0, tc, ts, on_late_grads, on_attention_done)
    grads1["mix_pre_g"] = gk0.pop("next_pre_g")
    grad_x, gk0["mix_pre_g"] = _row_bwd("rms_first_bwd", _f_x_rms, [(x0, d, 0)], [_row(ws0["mix_pre_g"])],
                                        [(dx, d, 0), (dh, d, 0)], [(0, F32)], [0])
    tb1 = scatter_wait(handles[0], [grad_x])
    grads0 = _unprep_grads(gk0, d)
    handle, token = swap_and_start("0_early", early_group, grads0, grad_x)
    handles.append(handle)

    def finish(l, sums, updates):
        sums = _pair_exchange("rs_join%d" % l, sums, owners, False, layer=l)
        updates = [_adamw("adamw%d_%s" % (l, n), w[n], sums[i], m[n], v[n], l, updates[i])
                   for i, n in enumerate(names)]
        return sums, updates

    sums = [_add_quads("rs_sum1_" + n, *tb1[n], 1, flag[n], None, token) for n in names]
    sums, updates = finish(1, sums, [None] * len(names))
    out = {}
    rep = list(REPLICATED)
    grads = [grads0, grads1]
    g_rep = [jnp.stack([grads[l][n].reshape(w[n].shape[1:]) for l in range(depth)]) for n in rep]
    for n, *res in zip(rep, *_all_reduce_adamw(g_rep, [w[n] for n in rep], [m[n] for n in rep], [v[n] for n in rep])):
        out[n] = tuple(res)
    hidden = [u[0] for u in updates] + [out[rep[0]][0]]
    tb0 = {**scatter_wait(handles[1], hidden), **scatter_wait(handles[2], hidden)}
    sums = [_add_quads("rs_sum0_" + n, *tb0[n], 0, flag[n], sums[i]) for i, n in enumerate(names)]
    sums, updates = finish(0, sums, updates)
    for n, gr, upd in zip(names, sums, updates):
        out[n] = (gr, *upd)
    out["w_in"] = tuple(jnp.swapaxes(a, 1, 2) for a in out["w_in"])
    return (loss, grad_x[None], *[out[n][i] for i in range(4) for n in WEIGHTS])
```

```python
import functools
import math

import jax
import jax.numpy as jnp
from jax import lax
from jax.experimental import pallas as pl
from jax.experimental.pallas import tpu as pltpu

F32 = jnp.float32
_MXU_DT = jnp.bfloat16
_VMEM_LIMIT = 48 * 1024 * 1024
_LANES = 128
_MESH = pl.DeviceIdType.MESH

N_HEADS = 8
QK_NOPE = 64
QK_ROPE = 32
V_HEAD = 64
HEAD_PAD = 128
SG_GROUPS = 4
SG_CHUNK = 128
CONV_K = 31
FFN_K = 3
ROPE_THETA = 10000.0
EPS = 1e-6
ADAM_LR, ADAM_B1, ADAM_B2, ADAM_EPS, ADAM_WD, ADAM_STEP = 0.001, 0.9, 0.999, 1e-08, 0.01, 10

SHARDED_LAST = ("w_in", "conv_dw_w", "conv_out_w", "sg_out_w", "mla_w_uq", "mla_w_ukv", "mla_w_o", "ffn_w_up",
                "ffn_dw_w")
SHARDED_MID = ("w_out", "ffn_w_down")
SHARDED = SHARDED_LAST + SHARDED_MID
WEIGHTS = ("mix_pre_g", "mix_post_g", "ffn_pre_g", "ffn_post_g", "w_in", "conv_dw_w", "conv_dw_b", "conv_ln_g",
           "conv_ln_b", "conv_out_w", "sg_ln_g", "sg_ln_b", "sg_w", "sg_b", "sg_out_w", "mla_q_norm_g", "mla_w_uq",
           "mla_kv_norm_g", "mla_w_ukv", "mla_w_o", "w_out", "ffn_w_up", "ffn_dw_w", "ffn_dw_b", "ffn_w_down")
REPLICATED = tuple(n for n in WEIGHTS if n not in SHARDED)


def _cparams(sem=None):
    return pltpu.CompilerParams(dimension_semantics=sem, vmem_limit_bytes=_VMEM_LIMIT)


def _pick(n, cands):
    for c in cands:
        if n % c == 0:
            return c
    return n


def _largest_tile(dim, cap):
    for t in range(min(cap, dim) // _LANES * _LANES, 0, -_LANES):
        if dim % t == 0:
            return t
    return dim


_MM_VMEM_BUDGET = 36 * 1024 * 1024
_MM_TM_CAP, _MM_TN_CAP, _MM_TK_CAP = 1024, 1536, 3072


def _mm(name, a, b, *, ta=False, tb=False, out_dtype=F32, a_halves=False, b_halves=False, out_quarters=False):
    assert not (a_halves and ta) and not (b_halves and tb)
    if a_halves:
        m, kdim = a.shape[1], 2 * a.shape[2]
    else:
        (kdim, m) = a.shape if ta else a.shape[::-1]
    if b_halves:
        kdim2, n = b.shape[1], 2 * b.shape[2]
    else:
        (n, kdim2) = b.shape if tb else b.shape[::-1]
    assert kdim == kdim2, (a.shape, b.shape, ta, tb)
    tk = _largest_tile(kdim // 2 if a_halves else kdim, _MM_TK_CAP)
    tn = _largest_tile(n // 4 if out_quarters else (n // 2 if b_halves else n), _MM_TN_CAP)
    nk = kdim // tk
    ab, bb, ob = a.dtype.itemsize, b.dtype.itemsize, jnp.dtype(out_dtype).itemsize
    tm = _largest_tile(m, _MM_TM_CAP)
    vmem = lambda t: 2 * (t * tk * ab + tk * tn * bb + t * tn * ob) + (t * tn * 4 if nk > 1 else 0)
    while vmem(tm) > _MM_VMEM_BUDGET and tm > _LANES:
        tm = _largest_tile(m, tm - _LANES)
    dims = (((0 if ta else 1,), (1 if tb else 0,)), ((), ()))

    def dot(a_ref, b_ref):
        return lax.dot_general(a_ref[...].astype(_MXU_DT), b_ref[...].astype(_MXU_DT), dims,
                               preferred_element_type=F32)

    def body_one(a_ref, b_ref, o_ref):
        o_ref[...] = dot(a_ref, b_ref).astype(o_ref.dtype)

    def body_acc(a_ref, b_ref, o_ref, acc_ref):
        k = pl.program_id(2)

        @pl.when(k == 0)
        def _():
            acc_ref[...] = jnp.zeros_like(acc_ref)

        acc_ref[...] += dot(a_ref, b_ref)

        @pl.when(k == nk - 1)
        def _():
            o_ref[...] = acc_ref[...].astype(o_ref.dtype)

    if a_halves:
        per = nk // 2
        a_spec = pl.BlockSpec((None, tm, tk), lambda i, j, k: (k // per, i, k % per))
    elif ta:
        a_spec = pl.BlockSpec((tk, tm), lambda i, j, k: (k, i))
    else:
        a_spec = pl.BlockSpec((tm, tk), lambda i, j, k: (i, k))
    if b_halves:
        per_b = n // 2 // tn
        b_spec = pl.BlockSpec((None, tk, tn), lambda i, j, k: (j // per_b, k, j % per_b))
    elif tb:
        b_spec = pl.BlockSpec((tn, tk), lambda i, j, k: (j, k))
    else:
        b_spec = pl.BlockSpec((tk, tn), lambda i, j, k: (k, j))
    if out_quarters:
        per_o = n // 4 // tn
        o_spec = pl.BlockSpec((None, tm, tn), lambda i, j, k: (j // per_o, i, j % per_o))
        o_shape = jax.ShapeDtypeStruct((4, m, n // 4), out_dtype)
    else:
        o_spec = pl.BlockSpec((tm, tn), lambda i, j, k: (i, j))
        o_shape = jax.ShapeDtypeStruct((m, n), out_dtype)
    return pl.pallas_call(
        body_one if nk == 1 else body_acc, name=name, grid=(m // tm, n // tn, nk),
        in_specs=[a_spec, b_spec], out_specs=o_spec, out_shape=o_shape,
        scratch_shapes=[] if nk == 1 else [pltpu.VMEM((tm, tn), F32)],
        compiler_params=_cparams(("parallel", "parallel", "arbitrary")),
    )(_in_hbm(a), _in_hbm(b))


def _row_spec(tm, width, idx):
    return pl.BlockSpec((tm, width), lambda i: (i, idx))


def _full_spec(shape):
    zeros = (0,) * len(shape)
    return pl.BlockSpec(shape, lambda i: zeros)


def _row_fwd(name, fn, rows, params, outs, tm=256):
    s = rows[0][0].shape[0]
    nr, npar = len(rows), len(params)

    def body(*refs):
        vals = [r[...].astype(F32) for r in refs[:nr + npar]]
        res = fn(*vals)
        for o_ref, r in zip(refs[nr + npar:], res):
            o_ref[...] = r.astype(o_ref.dtype)

    return pl.pallas_call(
        body, name=name, grid=(s // tm,),
        in_specs=[_row_spec(tm, w, i) for _, w, i in rows] + [_full_spec(p.shape) for p in params],
        out_specs=[_row_spec(tm, w, 0) for w, _ in outs],
        out_shape=[jax.ShapeDtypeStruct((s, w), dt) for w, dt in outs],
        compiler_params=_cparams(("parallel",)),
    )(*[_in_hbm(r[0]) for r in rows], *[_in_hbm(p) for p in params])


def _row_bwd(name, fn, rows, params, cots, row_grads, param_grads, tm=256, place=None, into=None):
    s = rows[0][0].shape[0]
    nr, npar, nc = len(rows), len(params), len(cots)
    row_grads = [((idxs,) if isinstance(idxs, int) else tuple(idxs), dt) for idxs, dt in row_grads]
    widths = [sum(rows[i][1] for i in idxs) for idxs, _ in row_grads]

    def body(*refs):
        i = pl.program_id(0)
        vals = [r[...].astype(F32) for r in refs[:nr + npar]]
        cvals = tuple(r[...].astype(F32) for r in refs[nr + npar:nr + npar + nc])
        _, vjp = jax.vjp(fn, *vals)
        grads = vjp(cvals)
        outs = refs[nr + npar + nc + (into is not None):]
        for o_ref, (idxs, _) in zip(outs, row_grads):
            pos = 0
            for idx in idxs:
                o_ref[:, pos:pos + rows[idx][1]] = grads[idx].astype(o_ref.dtype)
                pos += rows[idx][1]
        for o_ref, idx in zip(outs[len(row_grads):], param_grads):
            @pl.when(i == 0)
            def _(o_ref=o_ref):
                o_ref[...] = jnp.zeros_like(o_ref)

            o_ref[...] += grads[nr + idx]

    out_specs = [_row_spec(tm, w, 0) for w in widths] + [_full_spec(params[idx].shape) for idx in param_grads]
    out_shape = ([jax.ShapeDtypeStruct((s, w), dt) for w, (_, dt) in zip(widths, row_grads)]
                 + [jax.ShapeDtypeStruct(params[idx].shape, F32) for idx in param_grads])
    extra, aliases = [], {}
    if place is not None:
        out_specs[0] = _row_spec(tm, widths[0], place[1])
        out_shape[0] = jax.ShapeDtypeStruct((s, place[0]), row_grads[0][1])
    if into is not None:
        extra, aliases = [into], {nr + npar + nc: 0}
    return pl.pallas_call(
        body, name=name, grid=(s // tm,),
        in_specs=([_row_spec(tm, w, i) for _, w, i in rows] + [_full_spec(p.shape) for p in params]
                  + [_row_spec(tm, w, i) for _, w, i in cots] + [_ANY] * len(extra)),
        out_specs=out_specs, out_shape=out_shape, input_output_aliases=aliases,
        compiler_params=_cparams(("arbitrary",)),
    )(*[_in_hbm(r[0]) for r in rows], *[_in_hbm(p) for p in params], *[_in_hbm(c[0]) for c in cots], *extra)


def _rms(x, g):
    return x * lax.rsqrt(jnp.mean(x * x, axis=-1, keepdims=True) + EPS) * g


def _ln(x, g, b):
    mu = jnp.mean(x, axis=-1, keepdims=True)
    xc = x - mu
    var = jnp.mean(xc * xc, axis=-1, keepdims=True)
    return xc * lax.rsqrt(var + EPS) * g + b


def _sigmoid(x):
    return 1.0 / (1.0 + jnp.exp(-x))


def _gelu(x):
    return x * (0.5 * (1.0 + jnp.tanh(math.sqrt(2.0 / math.pi) * (x + 0.044715 * (x * x * x)))))


def _f_rms(x, g):
    return (_rms(x, g),)


def _f_x_rms(x, g):
    return (x, _rms(x, g))


def _f_ln_silu(z, g, b):
    y = _ln(z, g, b)
    return (y * _sigmoid(y),)


def _f_sg_pre(bu, bv, g, b):
    return (_gelu(bu), _ln(_gelu(bv), g, b))


def _f_merge(g0, g1, g2, ya, yb, yc):
    return (_sigmoid(g0) * ya + _sigmoid(g1) * yb + _sigmoid(g2) * yc,)


def _f_resid_rms(x, t, g_post):
    return (x + _rms(t, g_post),)


def _f_resid_rms_rms(x, t, g_post, g_next):
    x1 = x + _rms(t, g_post)
    return (x1, _rms(x1, g_next))


def _f_geglu(zg, zv):
    return _gelu(zg) * zv


_CONV_TILE_ELEMS = 16 * 1024


def _conv_tr(c):
    return _CONV_TILE_ELEMS // c


def _conv_tile(zp_ref, w_ref, bias, k_taps, off, r0):
    c = zp_ref.shape[1]
    tr = _conv_tr(c)
    acc = jnp.broadcast_to(bias, (tr, c))
    for k in range(k_taps):
        acc = acc + w_ref[k:k + 1, :] * zp_ref[r0 + off + k:r0 + off + k + tr, :]
    return acc


def _conv_bwd_input_tile(dzp_ref, w_ref, k_taps, r0):
    c = dzp_ref.shape[1]
    tr = _conv_tr(c)
    acc = jnp.zeros((tr, c), F32)
    for k in range(k_taps):
        s0 = r0 + (k_taps - 1) - k
        acc = acc + w_ref[k:k + 1, :] * dzp_ref[s0:s0 + tr, :]
    return acc


def _conv_bwd_weight(dzp_ref, zp_ref, dw_ref, db_ref, k_taps, off, s):
    c = zp_ref.shape[1]
    tr = _conv_tr(c)
    fold = lambda v: jnp.sum(v.reshape(tr // 8, 8, c), axis=0)
    for k in range(k_taps):
        acc = jnp.zeros((8, c), F32)
        for r in range(s // tr):
            r0 = r * tr
            acc = acc + fold(dzp_ref[r0:r0 + tr, :] * zp_ref[r0 + off + k:r0 + off + k + tr, :])
        dw_ref[k:k + 1, :] = jnp.sum(acc, axis=0, keepdims=True)
    acc = jnp.zeros((8, c), F32)
    for r in range(s // tr):
        acc = acc + fold(dzp_ref[r * tr:(r + 1) * tr, :])
    db_ref[...] = jnp.sum(acc, axis=0, keepdims=True)


def _glu_conv_fwd(p, blk0, w, b):
    s = p.shape[0]
    k_taps, c = w.shape
    cb, pad = _LANES, 32
    off = pad - (k_taps - 1)

    def body(a_ref, w_ref, b_ref, o_ref, zp_ref):
        zp_ref[0:pad, :] = jnp.zeros((pad, cb), F32)
        zp_ref[pad:pad + s, :] = a_ref[:, 0:cb] * _sigmoid(a_ref[:, cb:2 * cb])
        tr = _conv_tr(cb)
        for r in range(s // tr):
            o_ref[r * tr:(r + 1) * tr, :] = _conv_tile(zp_ref, w_ref, b_ref[...], k_taps, off, r * tr)

    return pl.pallas_call(
        body, name="glu_conv_fwd", grid=(c // cb,),
        in_specs=[pl.BlockSpec((s, 2 * cb), lambda j: (0, blk0 + j)),
                  pl.BlockSpec((k_taps, cb), lambda j: (0, j)), pl.BlockSpec((1, cb), lambda j: (0, j))],
        out_specs=pl.BlockSpec((s, cb), lambda j: (0, j)),
        out_shape=jax.ShapeDtypeStruct((s, c), F32),
        scratch_shapes=[pltpu.VMEM((s + pad, cb), F32)],
        compiler_params=_cparams(("parallel",)),
    )(*_hbm_all(p, w, b))


def _glu_conv_bwd(p, blk0, w, dz, dp):
    s = p.shape[0]
    k_taps, c = w.shape
    cb, pad = _LANES, 32
    off = pad - (k_taps - 1)

    def body(a_ref, w_ref, dz_ref, dp_in, da_ref, dw_ref, db_ref, zp_ref, dzp_ref):
        zp_ref[0:pad, :] = jnp.zeros((pad, cb), F32)
        zp_ref[pad:pad + s, :] = a_ref[:, 0:cb] * _sigmoid(a_ref[:, cb:2 * cb])
        dzp_ref[0:s, :] = dz_ref[...]
        dzp_ref[s:s + pad, :] = jnp.zeros((pad, cb), F32)
        tr = _conv_tr(cb)
        for r in range(s // tr):
            rows = slice(r * tr, (r + 1) * tr)
            dz0 = _conv_bwd_input_tile(dzp_ref, w_ref, k_taps, r * tr)
            sg = _sigmoid(a_ref[rows, cb:2 * cb])
            da_ref[rows, 0:cb] = (dz0 * sg).astype(da_ref.dtype)
            da_ref[rows, cb:2 * cb] = (dz0 * a_ref[rows, 0:cb] * sg * (1.0 - sg)).astype(da_ref.dtype)
        _conv_bwd_weight(dzp_ref, zp_ref, dw_ref, db_ref, k_taps, off, s)

    return pl.pallas_call(
        body, name="glu_conv_bwd", grid=(c // cb,),
        in_specs=[pl.BlockSpec((s, 2 * cb), lambda j: (0, blk0 + j)),
                  pl.BlockSpec((k_taps, cb), lambda j: (0, j)), pl.BlockSpec((s, cb), lambda j: (0, j)), _ANY],
        out_specs=[pl.BlockSpec((s, 2 * cb), lambda j: (0, blk0 + j)),
                   pl.BlockSpec((k_taps, cb), lambda j: (0, j)), pl.BlockSpec((1, cb), lambda j: (0, j))],
        out_shape=[jax.ShapeDtypeStruct(dp.shape, dp.dtype),
                   jax.ShapeDtypeStruct((k_taps, c), F32), jax.ShapeDtypeStruct((1, c), F32)],
        scratch_shapes=[pltpu.VMEM((s + pad, cb), F32), pltpu.VMEM((s + pad, cb), F32)],
        input_output_aliases={3: 0},
        compiler_params=_cparams(("parallel",)),
    )(*_hbm_all(p, w, dz), dp)


def _conv_geglu_fwd(up, w, b):
    s, f2 = up.shape
    f = f2 // 2
    k_taps = w.shape[0]
    cb, pad = _FFN_CB, 8
    off = pad - (k_taps - 1)
    nb = f // cb

    def body(ug_ref, uv_ref, wg_ref, wv_ref, bg_ref, bv_ref, o_ref, z_ref, w_ref, b_ref):
        _pair(w_ref, wg_ref[...], wv_ref[...], cb)
        _pair(b_ref, bg_ref[...], bv_ref[...], cb)
        z_ref[0:pad, :] = jnp.zeros((pad, 2 * cb), F32)
        z_ref[pad:pad + s, 0:cb] = ug_ref[...]
        z_ref[pad:pad + s, cb:2 * cb] = uv_ref[...]
        tr = _conv_tr(2 * cb)
        for r in range(s // tr):
            z = _conv_tile(z_ref, w_ref, b_ref[...], k_taps, off, r * tr)
            o_ref[r * tr:(r + 1) * tr, :] = _f_geglu(z[:, 0:cb], z[:, cb:2 * cb]).astype(o_ref.dtype)

    two = lambda rows_: [pl.BlockSpec((rows_, cb), lambda j: (0, j)), pl.BlockSpec((rows_, cb), lambda j: (0, nb + j))]
    return pl.pallas_call(
        body, name="conv_geglu_fwd", grid=(nb,),
        in_specs=two(s) + two(k_taps) + two(1),
        out_specs=pl.BlockSpec((s, cb), lambda j: (0, j)),
        out_shape=jax.ShapeDtypeStruct((s, f), _MXU_DT),
        scratch_shapes=[pltpu.VMEM((s + pad, 2 * cb), F32), pltpu.VMEM((k_taps, 2 * cb), F32),
                        pltpu.VMEM((1, 2 * cb), F32)],
        compiler_params=_cparams(("parallel",)),
    )(*_hbm_all(up, up, w, w, b, b))


def _pair(dst_ref, first, second, cb):
    dst_ref[:, 0:cb] = first
    dst_ref[:, cb:2 * cb] = second


def _conv_geglu_bwd(up, w, b, dact):
    s, f2 = up.shape
    f = f2 // 2
    k_taps = w.shape[0]
    cb, pad = _FFN_CB, 8
    off = pad - (k_taps - 1)
    nb = f // cb

    def body(ug_ref, uv_ref, wg_ref, wv_ref, bg_ref, bv_ref, da_ref, du_ref, dw_ref, db_ref, z_ref, dz_ref, w_ref,
             b_ref, dw_sc, db_sc):
        _pair(w_ref, wg_ref[...], wv_ref[...], cb)
        _pair(b_ref, bg_ref[...], bv_ref[...], cb)
        z_ref[0:pad, :] = jnp.zeros((pad, 2 * cb), F32)
        z_ref[pad:pad + s, 0:cb] = ug_ref[...]
        z_ref[pad:pad + s, cb:2 * cb] = uv_ref[...]
        dz_ref[s:s + pad, :] = jnp.zeros((pad, 2 * cb), F32)
        tr = _conv_tr(2 * cb)
        for r in range(s // tr):
            rows = slice(r * tr, (r + 1) * tr)
            z = _conv_tile(z_ref, w_ref, b_ref[...], k_taps, off, r * tr)
            _, vjp = jax.vjp(_f_geglu, z[:, 0:cb], z[:, cb:2 * cb])
            dzg, dzv = vjp(da_ref[rows, :].astype(F32))
            dz_ref[rows, 0:cb] = dzg
            dz_ref[rows, cb:2 * cb] = dzv
        for r in range(s // tr):
            rows = slice(r * tr, (r + 1) * tr)
            du = _conv_bwd_input_tile(dz_ref, w_ref, k_taps, r * tr).astype(du_ref.dtype)
            du_ref[0, rows, :] = du[:, 0:cb]
            du_ref[1, rows, :] = du[:, cb:2 * cb]
        _conv_bwd_weight(dz_ref, z_ref, dw_sc, db_sc, k_taps, off, s)
        for half in range(2):
            dw_ref[half] = dw_sc[:, half * cb:(half + 1) * cb]
            db_ref[half] = db_sc[:, half * cb:(half + 1) * cb]

    two = lambda rows_: [pl.BlockSpec((rows_, cb), lambda j: (0, j)), pl.BlockSpec((rows_, cb), lambda j: (0, nb + j))]
    both = lambda rows_: pl.BlockSpec((2, rows_, cb), lambda j: (0, 0, j))
    return pl.pallas_call(
        body, name="conv_geglu_bwd", grid=(nb,),
        in_specs=two(s) + two(k_taps) + two(1) + [pl.BlockSpec((s, cb), lambda j: (0, j))],
        out_specs=[both(s), both(k_taps), both(1)],
        out_shape=[jax.ShapeDtypeStruct((2, s, f), _MXU_DT), jax.ShapeDtypeStruct((2, k_taps, f), F32),
                   jax.ShapeDtypeStruct((2, 1, f), F32)],
        scratch_shapes=[pltpu.VMEM((s + pad, 2 * cb), F32), pltpu.VMEM((s + pad, 2 * cb), F32),
                        pltpu.VMEM((k_taps, 2 * cb), F32), pltpu.VMEM((1, 2 * cb), F32),
                        pltpu.VMEM((k_taps, 2 * cb), F32), pltpu.VMEM((1, 2 * cb), F32)],
        compiler_params=_cparams(("parallel",)),
    )(*_hbm_all(up, up, w, w, b, b, dact))


def _tril_mask():
    t = lax.broadcasted_iota(jnp.int32, (SG_CHUNK, SG_CHUNK), 0)
    s = lax.broadcasted_iota(jnp.int32, (SG_CHUNK, SG_CHUNK), 1)
    return t >= s


def _sg_mix_fwd(u, vn, w, bcol):
    s, c = u.shape
    gw = c // SG_GROUPS

    def body(u_ref, v_ref, w_ref, b_ref, o_ref):
        wm = jnp.where(_tril_mask(), w_ref[0], 0.0).astype(_MXU_DT)
        for n in range(s // SG_CHUNK):
            rows = slice(n * SG_CHUNK, (n + 1) * SG_CHUNK)
            mixed = jnp.dot(wm, v_ref[rows, :], preferred_element_type=F32) + b_ref[0]
            o_ref[rows, :] = (u_ref[rows, :] * mixed).astype(o_ref.dtype)

    return pl.pallas_call(
        body, name="sg_mix_fwd", grid=(SG_GROUPS,),
        in_specs=[pl.BlockSpec((s, gw), lambda g: (0, g)), pl.BlockSpec((s, gw), lambda g: (0, g)),
                  pl.BlockSpec((1, SG_CHUNK, SG_CHUNK), lambda g: (g, 0, 0)),
                  pl.BlockSpec((1, SG_CHUNK, 1), lambda g: (g, 0, 0))],
        out_specs=pl.BlockSpec((s, gw), lambda g: (0, g)),
        out_shape=jax.ShapeDtypeStruct((s, c), _MXU_DT),
        compiler_params=_cparams(("parallel",)),
    )(*_hbm_all(u, vn, w, bcol))


def _sg_mix_bwd(u, vn, w, bcol, dub):
    s, c = u.shape
    gw = c // SG_GROUPS

    def body(u_ref, v_ref, w_ref, b_ref, d_ref, du_ref, dv_ref, dw_ref, db_ref):
        mask = _tril_mask()
        wm = jnp.where(mask, w_ref[0], 0.0).astype(_MXU_DT)
        dw = jnp.zeros((SG_CHUNK, SG_CHUNK), F32)
        db = jnp.zeros((SG_CHUNK, 1), F32)
        for n in range(s // SG_CHUNK):
            rows = slice(n * SG_CHUNK, (n + 1) * SG_CHUNK)
            v = v_ref[rows, :]
            d = d_ref[rows, :].astype(F32)
            mixed = jnp.dot(wm, v, preferred_element_type=F32) + b_ref[0]
            du_ref[rows, :] = d * mixed
            dmix = d * u_ref[rows, :]
            dmix_lo = dmix.astype(_MXU_DT)
            dv_ref[rows, :] = lax.dot_general(wm, dmix_lo, (((0,), (0,)), ((), ())), preferred_element_type=F32)
            dw = dw + lax.dot_general(dmix_lo, v, (((1,), (1,)), ((), ())), preferred_element_type=F32)
            db = db + jnp.sum(dmix, axis=1, keepdims=True)
        dw_ref[0] = jnp.where(mask, dw, 0.0)
        db_ref[0] = db

    return pl.pallas_call(
        body, name="sg_mix_bwd", grid=(SG_GROUPS,),
        in_specs=[pl.BlockSpec((s, gw), lambda g: (0, g)), pl.BlockSpec((s, gw), lambda g: (0, g)),
                  pl.BlockSpec((1, SG_CHUNK, SG_CHUNK), lambda g: (g, 0, 0)),
                  pl.BlockSpec((1, SG_CHUNK, 1), lambda g: (g, 0, 0)), pl.BlockSpec((s, gw), lambda g: (0, g))],
        out_specs=[pl.BlockSpec((s, gw), lambda g: (0, g)), pl.BlockSpec((s, gw), lambda g: (0, g)),
                   pl.BlockSpec((1, SG_CHUNK, SG_CHUNK), lambda g: (g, 0, 0)),
                   pl.BlockSpec((1, SG_CHUNK, 1), lambda g: (g, 0, 0))],
        out_shape=[jax.ShapeDtypeStruct((s, c), F32), jax.ShapeDtypeStruct((s, c), F32),
                   jax.ShapeDtypeStruct((SG_GROUPS, SG_CHUNK, SG_CHUNK), F32),
                   jax.ShapeDtypeStruct((SG_GROUPS, SG_CHUNK, 1), F32)],
        compiler_params=_cparams(("parallel",)),
    )(*_hbm_all(u, vn, w, bcol, dub))


def _rope_fwd(q2, kv2, p, krm_idx, krs_idx, tc, ts):
    s = q2.shape[0]
    hw = N_HEADS * HEAD_PAD
    tm = 256

    def body(qm_ref, qs_ref, kn_ref, v_ref, krm_ref, krs_ref, tc_ref, ts_ref, q_ref, k_ref, vo_ref):
        tcv, tsv = tc_ref[...], ts_ref[...]
        kpe = krm_ref[...] * tcv + krs_ref[...] * tsv
        for h in range(N_HEADS):
            cols = slice(h * HEAD_PAD, (h + 1) * HEAD_PAD)
            q_ref[:, cols] = (qm_ref[:, cols] * tcv + qs_ref[:, cols] * tsv).astype(q_ref.dtype)
            k_ref[:, cols] = (kn_ref[:, cols] + kpe).astype(k_ref.dtype)
        vo_ref[...] = v_ref[...].astype(vo_ref.dtype)

    return pl.pallas_call(
        body, name="rope_fwd", grid=(s // tm,),
        in_specs=[_row_spec(tm, hw, 0), _row_spec(tm, hw, 1), _row_spec(tm, hw, 0), _row_spec(tm, hw, 1),
                  _row_spec(tm, HEAD_PAD, krm_idx), _row_spec(tm, HEAD_PAD, krs_idx),
                  _row_spec(tm, HEAD_PAD, 0), _row_spec(tm, HEAD_PAD, 0)],
        out_specs=[_row_spec(tm, hw, 0)] * 3,
        out_shape=[jax.ShapeDtypeStruct((s, hw), _MXU_DT)] * 3,
        compiler_params=_cparams(("parallel",)),
    )(*_hbm_all(q2, q2, kv2, kv2, p, p, tc, ts))


def _rope_bwd(dq, dk, dv, tc, ts, dp, kr_blk):
    s = dq.shape[0]
    hw = N_HEADS * HEAD_PAD
    tm = 256

    def body(dq_ref, dk_ref, dv_ref, tc_ref, ts_ref, dp_in, dq2_ref, dkv2_ref, dkr_ref):
        tcv, tsv = tc_ref[...], ts_ref[...]
        dkpe = jnp.zeros((tm, HEAD_PAD), F32)
        for h in range(N_HEADS):
            cols = slice(h * HEAD_PAD, (h + 1) * HEAD_PAD)
            dqh = dq_ref[:, cols]
            dq2_ref[:, cols] = (dqh * tcv).astype(dq2_ref.dtype)
            dq2_ref[:, hw + h * HEAD_PAD:hw + (h + 1) * HEAD_PAD] = (dqh * tsv).astype(dq2_ref.dtype)
            dkpe = dkpe + dk_ref[:, cols]
        dkv2_ref[:, 0:hw] = dk_ref[...].astype(dkv2_ref.dtype)
        dkv2_ref[:, hw:2 * hw] = dv_ref[...].astype(dkv2_ref.dtype)
        dkr_ref[:, 0:HEAD_PAD] = (dkpe * tcv).astype(dkr_ref.dtype)
        dkr_ref[:, HEAD_PAD:2 * HEAD_PAD] = (dkpe * tsv).astype(dkr_ref.dtype)
        dkr_ref[:, 2 * HEAD_PAD:3 * HEAD_PAD] = jnp.zeros((tm, HEAD_PAD), dkr_ref.dtype)

    return pl.pallas_call(
        body, name="rope_bwd", grid=(s // tm,),
        in_specs=[_row_spec(tm, hw, 0)] * 3 + [_row_spec(tm, HEAD_PAD, 0)] * 2 + [_ANY],
        out_specs=[_row_spec(tm, 2 * hw, 0), _row_spec(tm, 2 * hw, 0), _row_spec(tm, 3 * HEAD_PAD, kr_blk)],
        out_shape=[jax.ShapeDtypeStruct((s, 2 * hw), _MXU_DT), jax.ShapeDtypeStruct((s, 2 * hw), _MXU_DT),
                   jax.ShapeDtypeStruct(dp.shape, dp.dtype)],
        input_output_aliases={5: 2},
        compiler_params=_cparams(("parallel",)),
    )(*_hbm_all(dq, dk, dv, tc, ts), dp)


_ATTN_TQ = 512
_ATTN_SCALE = (QK_NOPE + QK_ROPE) ** -0.5


def _attn_probs(q, k, i):
    s = k.shape[0]
    sc = lax.dot_general(q, k, (((1,), (1,)), ((), ())), preferred_element_type=F32) * _ATTN_SCALE
    row = i * _ATTN_TQ + lax.broadcasted_iota(jnp.int32, (_ATTN_TQ, s), 0)
    col = lax.broadcasted_iota(jnp.int32, (_ATTN_TQ, s), 1)
    sc = jnp.where(row >= col, sc, jnp.finfo(F32).min)
    e = jnp.exp(sc - jnp.max(sc, axis=1, keepdims=True))
    return e * (1.0 / jnp.sum(e, axis=1, keepdims=True))


def _per_query_block(s, fn):
    i = pl.program_id(1)
    for n in range(s // _ATTN_TQ):
        @pl.when(i == n)
        def _(n=n):
            fn(n, (n + 1) * _ATTN_TQ)


def _attn_fwd(q, k, v):
    s = q.shape[0]

    def body(q_ref, k_ref, v_ref, o_ref):
        def block(n, kl):
            p = _attn_probs(q_ref[...], k_ref[0:kl, :], n)
            o_ref[...] = jnp.dot(p.astype(_MXU_DT), v_ref[0:kl, :], preferred_element_type=F32).astype(o_ref.dtype)

        _per_query_block(s, block)

    qspec = pl.BlockSpec((_ATTN_TQ, HEAD_PAD), lambda h, i: (i, h))
    kspec = pl.BlockSpec((s, HEAD_PAD), lambda h, i: (0, h))
    return pl.pallas_call(
        body, name="attn_fwd", grid=(N_HEADS, s // _ATTN_TQ),
        in_specs=[qspec, kspec, kspec], out_specs=qspec,
        out_shape=jax.ShapeDtypeStruct(q.shape, _MXU_DT),
        compiler_params=_cparams(("parallel", "parallel")),
    )(*_hbm_all(q, k, v))


def _attn_bwd(q, k, v, do):
    s = q.shape[0]

    def body(q_ref, k_ref, v_ref, do_ref, dq_ref, dk_ref, dv_ref):
        i = pl.program_id(1)

        @pl.when(i == 0)
        def _():
            dk_ref[...] = jnp.zeros_like(dk_ref)
            dv_ref[...] = jnp.zeros_like(dv_ref)

        def block(n, kl):
            qv, kv, dov = q_ref[...], k_ref[0:kl, :], do_ref[...]
            p = _attn_probs(qv, kv, n)
            dp = lax.dot_general(dov, v_ref[0:kl, :], (((1,), (1,)), ((), ())), preferred_element_type=F32)
            delta = jnp.sum(p * dp, axis=1, keepdims=True)
            ds = (p * (dp - delta) * _ATTN_SCALE).astype(_MXU_DT)
            dq_ref[...] = jnp.dot(ds, kv, preferred_element_type=F32)
            dk_ref[0:kl, :] += lax.dot_general(ds, qv, (((0,), (0,)), ((), ())), preferred_element_type=F32)
            dv_ref[0:kl, :] += lax.dot_general(p.astype(_MXU_DT), dov, (((0,), (0,)), ((), ())),
                                               preferred_element_type=F32)

        _per_query_block(s, block)

    qspec = pl.BlockSpec((_ATTN_TQ, HEAD_PAD), lambda h, i: (i, h))
    kspec = pl.BlockSpec((s, HEAD_PAD), lambda h, i: (0, h))
    return pl.pallas_call(
        body, name="attn_bwd", grid=(N_HEADS, s // _ATTN_TQ),
        in_specs=[qspec, kspec, kspec, qspec], out_specs=[qspec, kspec, kspec],
        out_shape=[jax.ShapeDtypeStruct(q.shape, F32)] * 3,
        compiler_params=_cparams(("parallel", "arbitrary")),
    )(*_hbm_all(q, k, v, do))


def _loss_head(y, target):
    s, d = y.shape
    tm = 256

    def body(y_ref, t_ref, loss_ref, dy_ref):
        @pl.when(pl.program_id(0) == 0)
        def _():
            loss_ref[...] = jnp.zeros_like(loss_ref)

        err = y_ref[...] - t_ref[...]
        loss_ref[...] += 0.5 * jnp.sum(jnp.mean(err * err, axis=-1, keepdims=True), axis=0, keepdims=True)
        dy_ref[...] = err * (1.0 / d)

    return pl.pallas_call(
        body, name="loss_head", grid=(s // tm,),
        in_specs=[_row_spec(tm, d, 0), _row_spec(tm, d, 0)],
        out_specs=[_full_spec((1, 1)), _row_spec(tm, d, 0)],
        out_shape=[jax.ShapeDtypeStruct((1, 1), F32), jax.ShapeDtypeStruct((s, d), F32)],
        compiler_params=_cparams(("arbitrary",)),
    )(*_hbm_all(y, target))


def _adamw_math(w, g, m, v):
    mn = ADAM_B1 * m + (1.0 - ADAM_B1) * g
    vn = ADAM_B2 * v + (1.0 - ADAM_B2) * (g * g)
    m_hat = mn / (1.0 - ADAM_B1 ** ADAM_STEP)
    v_hat = vn / (1.0 - ADAM_B2 ** ADAM_STEP)
    return -ADAM_LR * (m_hat / (jnp.sqrt(v_hat) + ADAM_EPS) + ADAM_WD * w), mn, vn


def _adamw(name, w, g, m, v, layer, into):
    _, k, n = w.shape
    tk, tn = _slab_block(k, n)

    def body(w_ref, g_ref, m_ref, v_ref, *rest):
        d_ref, mo_ref, vo_ref = rest[-3:]
        d_ref[...], mo_ref[...], vo_ref[...] = _adamw_math(w_ref[...], g_ref[...], m_ref[...], v_ref[...])

    spec = pl.BlockSpec((1, tk, tn), lambda j, jn: (layer, j, jn))
    extra = [] if into is None else list(into)
    return pl.pallas_call(
        body, name=name, grid=(k // tk, n // tn), in_specs=[spec] * 4 + [_ANY] * len(extra), out_specs=[spec] * 3,
        out_shape=[jax.ShapeDtypeStruct(w.shape, F32)] * 3,
        input_output_aliases={4 + i: i for i in range(len(extra))},
        compiler_params=_cparams(("parallel", "parallel")),
    )(_in_hbm(w), _in_hbm(g), _in_hbm(m), _in_hbm(v), *extra)


_ANY = pl.BlockSpec(memory_space=pl.ANY)


def _mesh_pos():
    return lax.axis_index("x"), lax.axis_index("y"), lax.axis_index("c")


def _other_chips(x, y):
    return [(1 - x, y), (x, 1 - y), (1 - x, 1 - y)]


def _remote(src, dst, send_sem, recv_sem, to):
    return pltpu.make_async_remote_copy(src_ref=src, dst_ref=dst, send_sem=send_sem, recv_sem=recv_sem,
                                        device_id=to, device_id_type=_MESH)


_HBM = pl.BlockSpec(memory_space=pltpu.HBM)
_SEM = pl.BlockSpec(memory_space=pltpu.SEMAPHORE)
_EFFECT = pltpu.SideEffectType.DATAFLOW_SIDE_EFFECTING


def _in_hbm(a):
    return pltpu.with_memory_space_constraint(a, pltpu.HBM)


def _hbm_all(*arrays):
    return tuple(_in_hbm(a) for a in arrays)


def _chip_index(chip):
    return 2 * chip[0] + chip[1]


def _ag_forward(name, shards, lands, layer, have_remote):
    n = len(shards)

    def body(*refs):
        ins = refs[:n]
        outs = refs[2 * n:3 * n] if lands is not None else refs[n:2 * n]
        send_sems, recv_sems = refs[-2:]
        x, y, c = _mesh_pos()
        sibling = (x, y, 1 - c)
        chips = _other_chips(x, y)

        def copy(a, k, src, dst, to):
            return _remote(src, dst, send_sems.at[a, k], recv_sems.at[a, k], to)

        own = [copy(a, 6, ins[a], outs[a].at[2 * x + y], sibling) for a in range(n)]
        for cp in own:
            cp.start()

        @pl.when(c == layer)
        def _():
            started = []
            if not have_remote:
                for k, chip in enumerate(chips):
                    for a in range(n):
                        cp = copy(a, k, ins[a], outs[a].at[2 * x + y], (*chip, c))
                        cp.start()
                        started.append(cp)
            for k, chip in enumerate(chips):
                for a in range(n):
                    landed = outs[a].at[_chip_index(chip)]
                    if not have_remote:
                        copy(a, k, ins[a], landed, (*chip, c)).wait_recv()
                    cp = copy(a, 3 + k, landed, landed, sibling)
                    cp.start()
                    started.append(cp)
            for cp in started:
                cp.wait_send()

        @pl.when(c != layer)
        def _():
            for k, chip in enumerate(chips):
                for a in range(n):
                    copy(a, 3 + k, ins[a], outs[a].at[_chip_index(chip)], sibling).wait_recv()

        for cp in own:
            cp.wait()

    out_shape = [jax.ShapeDtypeStruct((4,) + a.shape, a.dtype) for a in shards]
    extra = [] if lands is None else list(lands)
    return pl.pallas_call(
        body, name=name, in_specs=[_ANY] * (n + len(extra)), out_specs=[_ANY] * n,
        out_shape=out_shape, input_output_aliases={n + a: a for a in range(len(extra))},
        scratch_shapes=[pltpu.SemaphoreType.DMA((n, 7)), pltpu.SemaphoreType.DMA((n, 7))],
    )(*shards, *extra)


def _owner_sends(owners, srcs, dsts, send_sems, recv_sems, do):
    x, y, c = _mesh_pos()
    for core in (0, 1):
        mine = [a for a in range(len(srcs)) if owners[a] == core]
        if mine:
            @pl.when(c == core)
            def _(mine=mine):
                for k, chip in enumerate(_other_chips(x, y)):
                    for a in mine:
                        do(_remote(srcs[a](chip, k), dsts[a](chip, k), send_sems.at[3 * a + k],
                                   recv_sems.at[3 * a + k], (*chip, c)))


def _split_start(name, owners, sources, land_shapes, src_of, dst_of, after):
    n = len(sources)

    def body(*refs):
        srcs, lands = refs[:n], refs[n:2 * n]
        send_sems, recv_sems = refs[2 * n + 1], refs[2 * n + 2]
        token = refs[-1]
        _owner_sends(owners, [functools.partial(src_of, srcs[a]) for a in range(n)],
                     [functools.partial(dst_of, lands[a]) for a in range(n)], send_sems, recv_sems,
                     lambda cp: cp.start())
        token[...] = jnp.zeros_like(token)

    lands = [_in_hbm(lax.empty(s.shape, s.dtype)) for s in land_shapes]
    outs = pl.pallas_call(
        body, name=name,
        out_shape=([pltpu.SemaphoreType.DMA((3 * n,)), pltpu.SemaphoreType.DMA((3 * n,))]
                   + [pltpu.HBM(a.shape, a.dtype) for a in sources] + [pltpu.HBM(s.shape, s.dtype) for s in land_shapes]
                   + [jax.ShapeDtypeStruct((8, _LANES), F32)]),
        in_specs=[_HBM] * (2 * n) + [_ANY],
        out_specs=[_SEM, _SEM] + [_HBM] * (2 * n) + [pl.BlockSpec(memory_space=pltpu.VMEM)],
        input_output_aliases={i: 2 + i for i in range(2 * n)},
        compiler_params=pltpu.CompilerParams(has_side_effects=_EFFECT),
    )(*[_in_hbm(a) for a in sources], *lands, after)
    return outs[0], outs[1], outs[2:2 + n], outs[2 + n:2 + 2 * n], outs[-1]


def _split_wait(name, owners, send_sems, recv_sems, sources, lands, after, src_of, dst_of):
    n = len(sources)

    def body(*refs):
        srcs, lnds = refs[:n], refs[n:2 * n]
        s_sems, r_sems = refs[2 * n], refs[2 * n + 1]

        def wait(cp):
            cp.wait_send()
            cp.wait_recv()

        _owner_sends(owners, [functools.partial(src_of, srcs[a]) for a in range(n)],
                     [functools.partial(dst_of, lnds[a]) for a in range(n)], s_sems, r_sems, wait)

    outs = pl.pallas_call(
        body, name=name,
        out_shape=[pltpu.HBM(a.shape, a.dtype) for a in sources] + [pltpu.HBM(a.shape, a.dtype) for a in lands],
        in_specs=[_HBM] * (2 * n) + [_SEM, _SEM] + [_ANY] * len(after), out_specs=[_HBM] * (2 * n),
        input_output_aliases={i: i for i in range(2 * n)},
        compiler_params=pltpu.CompilerParams(has_side_effects=_EFFECT),
    )(*sources, *lands, send_sems, recv_sems, *after)
    return outs[:n], outs[n:]


def _pair_exchange(name, arrays, owners, to_owner, layer=None):
    n = len(arrays)

    def body(*refs):
        ins, outs, (send_sems, recv_sems) = refs[:n], refs[n:2 * n], refs[2 * n:]
        x, y, c = _mesh_pos()
        part = (lambda r: r) if layer is None else (lambda r: r.at[layer])
        copies = [_remote(part(ins[a]), part(outs[a]), send_sems.at[a], recv_sems.at[a], (x, y, 1 - c))
                  for a in range(n)]
        for core in (0, 1):
            sends = [copies[a] for a in range(n) if (owners[a] != core) == to_owner]
            recvs = [copies[a] for a in range(n) if (owners[a] == core) == to_owner]

            @pl.when(c == core)
            def _(sends=sends, recvs=recvs):
                for cp in sends:
                    cp.start()
                for cp in recvs:
                    cp.wait_recv()
                for cp in sends:
                    cp.wait_send()

    return pl.pallas_call(
        body, name=name, in_specs=[_ANY] * n, out_specs=[_ANY] * n,
        out_shape=[jax.ShapeDtypeStruct(g.shape, g.dtype) for g in arrays],
        input_output_aliases={} if to_owner else {a: a for a in range(n)},
        scratch_shapes=[pltpu.SemaphoreType.DMA((n,)), pltpu.SemaphoreType.DMA((n,))],
    )(*arrays)


def _swap_split(name, arrays, lands, owners, sems, after):
    n = len(arrays)
    starting = sems is None

    def body(*refs):
        ins, lnds = refs[:n], refs[n:2 * n]
        send_sems, recv_sems = (refs[2 * n + 1], refs[2 * n + 2]) if starting else (refs[2 * n], refs[2 * n + 1])
        x, y, c = _mesh_pos()
        copies = [_remote(ins[a], lnds[a], send_sems.at[a], recv_sems.at[a], (x, y, 1 - c)) for a in range(n)]
        for core in (0, 1):
            sends = [copies[a] for a in range(n) if owners[a] != core]
            recvs = [copies[a] for a in range(n) if owners[a] == core]

            @pl.when(c == core)
            def _(sends=sends, recvs=recvs):
                if starting:
                    for cp in sends:
                        cp.start()
                else:
                    for cp in recvs:
                        cp.wait_recv()
                    for cp in sends:
                        cp.wait_send()

        if starting:
            refs[-1][...] = jnp.zeros_like(refs[-1])

    hbm = lambda arrs: [pltpu.HBM(a.shape, a.dtype) for a in arrs]
    if starting:
        zones = [_in_hbm(lax.empty(s.shape, s.dtype)) for s in lands]
        outs = pl.pallas_call(
            body, name=name,
            out_shape=([pltpu.SemaphoreType.DMA((n,)), pltpu.SemaphoreType.DMA((n,))] + hbm(arrays) + hbm(lands)
                       + [jax.ShapeDtypeStruct((8, _LANES), F32)]),
            in_specs=[_HBM] * (2 * n) + [_ANY],
            out_specs=[_SEM, _SEM] + [_HBM] * (2 * n) + [pl.BlockSpec(memory_space=pltpu.VMEM)],
            input_output_aliases={i: 2 + i for i in range(2 * n)},
            compiler_params=pltpu.CompilerParams(has_side_effects=_EFFECT),
        )(*[_in_hbm(a) for a in arrays], *zones, after)
        return outs[0], outs[1], outs[2:2 + n], outs[2 + n:2 + 2 * n], outs[-1]
    outs = pl.pallas_call(
        body, name=name, out_shape=hbm(arrays) + hbm(lands),
        in_specs=[_HBM] * (2 * n) + [_SEM, _SEM] + [_ANY] * len(after), out_specs=[_HBM] * (2 * n),
        input_output_aliases={i: i for i in range(2 * n)},
        compiler_params=pltpu.CompilerParams(has_side_effects=_EFFECT),
    )(*arrays, *lands, *sems, *after)
    return outs[:n], outs[n:]


def _add_pair(name, g, a, flags):
    _, k, n = g.shape
    tk, tn = _slab_block(k, n)

    def body(flags_ref, g_ref, a_ref, o_ref):
        o_ref[...] = (g_ref[...] + a_ref[...]).astype(o_ref.dtype)

    spec = pl.BlockSpec((1, tk, tn), lambda j, i, jn, fl: (j * fl[1], i * fl[1], jn * fl[1]))
    return pl.pallas_call(
        body, name=name,
        grid_spec=pltpu.PrefetchScalarGridSpec(num_scalar_prefetch=1, grid=(4, k // tk, n // tn),
                                               in_specs=[spec, spec], out_specs=spec),
        out_shape=jax.ShapeDtypeStruct(g.shape, _RS_DT),
        compiler_params=_cparams(("arbitrary", "arbitrary", "arbitrary")),
    )(flags, *_hbm_all(g, a))


def _add_quads(name, t, b, layer, flags, into, after=None):
    _, k, n = t.shape
    tk, tn = _slab_block(k, n)

    def body(flags_ref, t_ref, b_ref, *rest):
        o_ref = rest[-1]
        f = lambda v: v.astype(F32)
        o_ref[0] = ((f(t_ref[0]) + f(b_ref[0])) + f(b_ref[1])) + f(b_ref[2])

    extra = ([] if into is None else [into]) + ([] if after is None else [after])
    return pl.pallas_call(
        body, name=name,
        grid_spec=pltpu.PrefetchScalarGridSpec(
            num_scalar_prefetch=1, grid=(k // tk, n // tn),
            in_specs=[pl.BlockSpec((1, tk, tn), lambda i, jn, fl: (fl[0], i * fl[1], jn * fl[1])),
                      pl.BlockSpec((3, tk, tn), lambda i, jn, fl: (0, i * fl[1], jn * fl[1]))] + [_ANY] * len(extra),
            out_specs=pl.BlockSpec((1, tk, tn), lambda i, jn, fl: (layer, i * fl[1], jn * fl[1]))),
        out_shape=jax.ShapeDtypeStruct((2, k, n), F32),
        input_output_aliases={} if into is None else {3: 0},
        compiler_params=_cparams(("arbitrary", "arbitrary")),
    )(flags, *_hbm_all(t, b), *extra)


def _slab_block(k, n, itemsize=4):
    tk = (1 << 20) // (n * itemsize) // 16 * 16
    while 0 < tk < k and k % tk:
        tk -= 16
    if 0 < tk < k:
        return tk, n
    if k * n * itemsize <= (2 << 20) or n % _LANES:
        return k, n
    tn = max(_LANES, (2 << 20) // (k * itemsize) // _LANES * _LANES)
    while n % tn:
        tn -= _LANES
    return k, tn


def _all_reduce_adamw(gs, ws, ms, vs):
    n = len(gs)

    def body(*refs):
        g_refs, w_refs, m_refs, v_refs = (refs[i * n:(i + 1) * n] for i in range(4))
        gsum, delta, m_out, v_out = (refs[(4 + i) * n:(5 + i) * n] for i in range(4))
        slots = refs[8 * n:9 * n]
        send_sems, recv_sems = refs[9 * n:]
        x, y, c = _mesh_pos()
        me = 4 * x + 2 * y + c
        copies = []
        for rel in range(1, 8):
            bx, by, bc = (rel >> 2) & 1, (rel >> 1) & 1, rel & 1
            peer = (1 - x if bx else x, 1 - y if by else y, 1 - c if bc else c)
            for a in range(n):
                cp = _remote(g_refs[a], slots[a].at[me], send_sems.at[a, rel - 1], recv_sems.at[a, rel - 1], peer)
                cp.start()
                copies.append(cp)
        for a in range(n):
            slots[a][me] = g_refs[a][...]
        for cp in copies:
            cp.wait()
        for a in range(n):
            acc = slots[a][0]
            for d in range(1, 8):
                acc = acc + slots[a][d]
            gsum[a][...] = acc
            delta[a][...], m_out[a][...], v_out[a][...] = _adamw_math(w_refs[a][...], acc, m_refs[a][...],
                                                                      v_refs[a][...])

    vmem = pl.BlockSpec(memory_space=pltpu.VMEM)
    outs = pl.pallas_call(
        body, name="all_reduce_adamw", in_specs=[vmem] * (4 * n), out_specs=[vmem] * (4 * n),
        out_shape=[jax.ShapeDtypeStruct(g.shape, F32) for g in gs] * 4,
        scratch_shapes=([pltpu.VMEM((8,) + g.shape, F32) for g in gs]
                        + [pltpu.SemaphoreType.DMA((n, 7)), pltpu.SemaphoreType.DMA((n, 7))]),
        compiler_params=pltpu.CompilerParams(vmem_limit_bytes=_VMEM_LIMIT),
    )(*gs, *ws, *ms, *vs)
    return outs[:n], outs[n:2 * n], outs[2 * n:3 * n], outs[3 * n:]


def _swap_rope(a):
    h = QK_ROPE // 2
    return jnp.concatenate([a[..., h:], a[..., :h]], axis=-1)


def _swap_rope_rows(a):
    h = QK_ROPE // 2
    return jnp.concatenate([a[h:], a[:h]], axis=0)


_FFN_CB = 256


def _interleave_rows(a, cb):
    r, c = a.shape
    return a.reshape(2, r // (2 * cb), cb, c).transpose(1, 0, 2, 3).reshape(r, c)


def _deinterleave_rows(a, cb):
    r, c = a.shape
    return a.reshape(r // (2 * cb), 2, cb, c).transpose(1, 0, 2, 3).reshape(r, c)


class _InLayout:
    def __init__(self, d):
        self.d = d
        self.gates = 0
        self.a = 3 * d
        self.b = 4 * d
        self.kv = 5 * d
        self.q = self.kv + 256
        self.krm = self.q + 384
        self.krs = self.krm + HEAD_PAD
        self.width = self.krs + 2 * HEAD_PAD


def _prep_layer(wl, d):
    lay = _InLayout(d)
    w_in = wl["w_in"]
    dt = w_in.dtype
    a, b = w_in[0:d], w_in[d:2 * d]
    q, kv = w_in[2 * d:2 * d + 384], w_in[2 * d + 384:2 * d + 640]
    kr = w_in[2 * d + 640:2 * d + 640 + QK_ROPE]
    gates = w_in[2 * d + 640 + QK_ROPE:]
    z = lambda n: jnp.zeros((n, d), dt)
    krm = jnp.concatenate([z(QK_NOPE), kr, z(HEAD_PAD - QK_NOPE - QK_ROPE)], axis=0)
    krs = jnp.concatenate([z(QK_NOPE), _swap_rope_rows(kr), z(HEAD_PAD - QK_NOPE - QK_ROPE)], axis=0)
    out = dict(wl)
    out["w_in"] = jnp.concatenate([gates, _interleave_rows(a, _LANES), b, kv, q, krm, krs,
                                   z(lay.width - lay.krs - HEAD_PAD)], axis=0)
    uq = wl["mla_w_uq"].reshape(-1, N_HEADS, QK_NOPE + QK_ROPE)
    nq = uq.shape[0]
    nope, pe = uq[..., :QK_NOPE], uq[..., QK_NOPE:]
    zq = lambda n: jnp.zeros((nq, N_HEADS, n), dt)
    main = jnp.concatenate([nope, pe, zq(HEAD_PAD - QK_NOPE - QK_ROPE)], axis=-1).reshape(nq, -1)
    swapped = jnp.concatenate([zq(QK_NOPE), _swap_rope(pe), zq(HEAD_PAD - QK_NOPE - QK_ROPE)], axis=-1).reshape(nq, -1)
    out["mla_w_uq"] = jnp.concatenate([main, swapped], axis=1)
    ukv = wl["mla_w_ukv"].reshape(-1, N_HEADS, QK_NOPE + V_HEAD)
    nkv = ukv.shape[0]
    zk = jnp.zeros((nkv, N_HEADS, HEAD_PAD - QK_NOPE), dt)
    zv = jnp.zeros((nkv, N_HEADS, HEAD_PAD - V_HEAD), dt)
    out["mla_w_ukv"] = jnp.concatenate([jnp.concatenate([ukv[..., :QK_NOPE], zk], axis=-1).reshape(nkv, -1),
                                        jnp.concatenate([ukv[..., QK_NOPE:], zv], axis=-1).reshape(nkv, -1)], axis=1)
    wo = wl["mla_w_o"].reshape(N_HEADS, V_HEAD, -1)
    out["mla_w_o"] = jnp.concatenate([wo, jnp.zeros((N_HEADS, HEAD_PAD - V_HEAD, wo.shape[-1]), dt)],
                                     axis=1).reshape(N_HEADS * HEAD_PAD, -1)
    return out


def _unprep_grads(g, d):
    lay = _InLayout(d)
    gi = g["w_in"]
    kr = (gi[lay.krm + QK_NOPE:lay.krm + QK_NOPE + QK_ROPE]
          + _swap_rope_rows(gi[lay.krs + QK_NOPE:lay.krs + QK_NOPE + QK_ROPE]))
    out = dict(g)
    out["w_in"] = jnp.concatenate([_deinterleave_rows(gi[lay.a:lay.a + d], _LANES), gi[lay.b:lay.b + d],
                                   gi[lay.q:lay.q + 384], gi[lay.kv:lay.kv + 256], kr,
                                   gi[lay.gates:lay.gates + 3 * d]], axis=0)
    hw = N_HEADS * HEAD_PAD
    gq = g["mla_w_uq"]
    nq = gq.shape[0]
    main = gq[:, :hw].reshape(nq, N_HEADS, HEAD_PAD)
    swapped = gq[:, hw:].reshape(nq, N_HEADS, HEAD_PAD)
    pe = main[..., QK_NOPE:QK_NOPE + QK_ROPE] + _swap_rope(swapped[..., QK_NOPE:QK_NOPE + QK_ROPE])
    out["mla_w_uq"] = jnp.concatenate([main[..., :QK_NOPE], pe], axis=-1).reshape(nq, -1)
    gkv = g["mla_w_ukv"]
    nkv = gkv.shape[0]
    out["mla_w_ukv"] = jnp.concatenate([gkv[:, :hw].reshape(nkv, N_HEADS, HEAD_PAD)[..., :QK_NOPE],
                                        gkv[:, hw:].reshape(nkv, N_HEADS, HEAD_PAD)[..., :V_HEAD]],
                                       axis=-1).reshape(nkv, -1)
    go = g["mla_w_o"]
    out["mla_w_o"] = go.reshape(N_HEADS, HEAD_PAD, -1)[:, :V_HEAD].reshape(N_HEADS * V_HEAD, -1)
    return out


def _rope_tables(positions):
    s = positions.shape[0]
    inv = ROPE_THETA ** (-jnp.arange(0, QK_ROPE, 2, dtype=F32) / QK_ROPE)
    ang = positions.astype(F32)[:, None] * inv
    cos, sin = jnp.cos(ang), jnp.sin(ang)
    tail = jnp.zeros((s, HEAD_PAD - QK_NOPE - QK_ROPE), F32)
    tc = jnp.concatenate([jnp.ones((s, QK_NOPE), F32), cos, cos, tail], axis=1)
    ts = jnp.concatenate([jnp.zeros((s, QK_NOPE), F32), -sin, sin, tail], axis=1)
    return tc, ts


def _row(v):
    return v.reshape(1, -1)


def _layer_fwd(x, h, w, g_next, tc, ts, late_weights=None):
    d = x.shape[1]
    lay = _InLayout(d)
    cw = d // 2
    blk = lambda off, width: off // width
    p = _mm("mm_in", h, w["w_in"], tb=True)
    z1 = _glu_conv_fwd(p, blk(lay.a, 2 * _LANES), w["conv_dw_w"], _row(w["conv_dw_b"]))
    ln_a = [_row(w["conv_ln_g"]), _row(w["conv_ln_b"])]
    (z3,) = _row_fwd("ln_silu_fwd", _f_ln_silu, [(z1, cw, 0)], ln_a, [(cw, _MXU_DT)])
    ya = _mm("mm_conv_out", z3, w["conv_out_w"])
    ln_b = [_row(w["sg_ln_g"]), _row(w["sg_ln_b"])]
    u, vn = _row_fwd("sg_pre_fwd", _f_sg_pre, [(p, cw, blk(lay.b, cw)), (p, cw, blk(lay.b + cw, cw))], ln_b,
                     [(cw, F32), (cw, _MXU_DT)])
    bcol = w["sg_b"].reshape(SG_GROUPS, SG_CHUNK, 1)
    ub = _sg_mix_fwd(u, vn, w["sg_w"], bcol)
    yb = _mm("mm_sg_out", ub, w["sg_out_w"])
    (qn,) = _row_fwd("q_norm_fwd", _f_rms, [(p, 384, blk(lay.q, 384))], [_row(w["mla_q_norm_g"])], [(384, _MXU_DT)])
    (kvn,) = _row_fwd("kv_norm_fwd", _f_rms, [(p, 256, blk(lay.kv, 256))], [_row(w["mla_kv_norm_g"])],
                      [(256, _MXU_DT)])
    q2 = _mm("mm_uq", qn, w["mla_w_uq"])
    kv2 = _mm("mm_ukv", kvn, w["mla_w_ukv"])
    qf, kf, vf = _rope_fwd(q2, kv2, p, blk(lay.krm, HEAD_PAD), blk(lay.krs, HEAD_PAD), tc, ts)
    o = _attn_fwd(qf, kf, vf)
    yc = _mm("mm_o", o, w["mla_w_o"])
    gate_rows = [(p, d, 0), (p, d, 1), (p, d, 2)]
    (merged,) = _row_fwd("merge_fwd", _f_merge, gate_rows + [(ya, d, 0), (yb, d, 0), (yc, d, 0)], [], [(d, _MXU_DT)])
    if late_weights is not None:
        w = {**w, **late_weights(merged)}
    t = _mm("mm_out", merged, w["w_out"])
    x1, h2 = _row_fwd("resid_mix_fwd", _f_resid_rms_rms, [(x, d, 0), (t, d, 0)],
                      [_row(w["mix_post_g"]), _row(w["ffn_pre_g"])], [(d, F32), (d, _MXU_DT)])
    up = _mm("mm_up", h2, w["ffn_w_up"])
    act = _conv_geglu_fwd(up, w["ffn_dw_w"], _row(w["ffn_dw_b"]))
    dn = _mm("mm_down", act, w["ffn_w_down"])
    if g_next is None:
        (x2,) = _row_fwd("resid_ffn_last_fwd", _f_resid_rms, [(x1, d, 0), (dn, d, 0)], [_row(w["ffn_post_g"])],
                         [(d, F32)])
        h_next = None
    else:
        x2, h_next = _row_fwd("resid_ffn_fwd", _f_resid_rms_rms, [(x1, d, 0), (dn, d, 0)],
                              [_row(w["ffn_post_g"]), _row(g_next)], [(d, F32), (d, _MXU_DT)])
    saved = dict(x=x, h=h, p=p, z1=z1, z3=z3, ya=ya, u=u, vn=vn, ub=ub, yb=yb, qn=qn, kvn=kvn, qf=qf, kf=kf, vf=vf, o=o,
                 yc=yc, merged=merged, t=t, x1=x1, h2=h2, up=up, act=act, dn=dn, bcol=bcol)
    return x2, h_next, saved


def _layer_bwd(dx2, dh_next, w, g_next, sv, tc, ts, on_late_grads=None, on_attention_done=None):
    d = dx2.shape[1]
    lay = _InLayout(d)
    cw = d // 2
    blk = lambda off, width: off // width
    lo = _MXU_DT
    g = {}
    x1, dn = sv["x1"], sv["dn"]
    if dh_next is None:
        dx1, ddn, g["ffn_post_g"] = _row_bwd("resid_ffn_last_bwd", _f_resid_rms, [(x1, d, 0), (dn, d, 0)],
                                             [_row(w["ffn_post_g"])], [(dx2, d, 0)], [(0, F32), (1, lo)], [0])
    else:
        dx1, ddn, g["ffn_post_g"], g["next_pre_g"] = _row_bwd(
            "resid_ffn_bwd", _f_resid_rms_rms, [(x1, d, 0), (dn, d, 0)], [_row(w["ffn_post_g"]), _row(g_next)],
            [(dx2, d, 0), (dh_next, d, 0)], [(0, F32), (1, lo)], [0, 1])
    dact = _mm("mm_down_dx", ddn, w["ffn_w_down"], tb=True)
    g["ffn_w_down"] = _mm("mm_down_dw", sv["act"], ddn, ta=True)
    dup, dw_halves, db_halves = _conv_geglu_bwd(sv["up"], w["ffn_dw_w"], _row(w["ffn_dw_b"]), dact)
    g["ffn_dw_w"] = jnp.concatenate([dw_halves[0], dw_halves[1]], axis=1)
    g["ffn_dw_b"] = jnp.concatenate([db_halves[0], db_halves[1]], axis=1)
    dh2 = _mm("mm_up_dx", dup, w["ffn_w_up"], tb=True, a_halves=True)
    g["ffn_w_up"] = _mm("mm_up_dw", sv["h2"], dup, ta=True, b_halves=True, out_quarters=True)
    dx, dt, g["mix_post_g"], g["ffn_pre_g"] = _row_bwd(
        "resid_mix_bwd", _f_resid_rms_rms, [(sv["x"], d, 0), (sv["t"], d, 0)],
        [_row(w["mix_post_g"]), _row(w["ffn_pre_g"])], [(dx1, d, 0), (dh2, d, 0)], [(0, F32), (1, lo)], [0, 1])
    dmerged = _mm("mm_out_dx", dt, w["w_out"], tb=True)
    g["w_out"] = _mm("mm_out_dw", sv["merged"], dt, ta=True)
    if on_late_grads is not None:
        dmerged = on_late_grads(g, dmerged)
    p = sv["p"]
    gate_rows = [(p, d, 0), (p, d, 1), (p, d, 2)]
    dp, dya, dyb, dyc = _row_bwd(
        "merge_bwd", _f_merge, gate_rows + [(sv["ya"], d, 0), (sv["yb"], d, 0), (sv["yc"], d, 0)], [],
        [(dmerged, d, 0)], [((0, 1, 2), lo), ((3,), lo), ((4,), lo), ((5,), lo)], [], place=(lay.width, 0))
    do = _mm("mm_o_dx", dyc, w["mla_w_o"], tb=True, out_dtype=lo)
    g["mla_w_o"] = _mm("mm_o_dw", sv["o"], dyc, ta=True)
    dqf, dkf, dvf = _attn_bwd(sv["qf"], sv["kf"], sv["vf"], do)
    dq2, dkv2, dp = _rope_bwd(dqf, dkf, dvf, tc, ts, dp, blk(lay.krm, 3 * HEAD_PAD))
    dkvn = _mm("mm_ukv_dx", dkv2, w["mla_w_ukv"], tb=True)
    g["mla_w_ukv"] = _mm("mm_ukv_dw", sv["kvn"], dkv2, ta=True)
    dqn = _mm("mm_uq_dx", dq2, w["mla_w_uq"], tb=True)
    g["mla_w_uq"] = _mm("mm_uq_dw", sv["qn"], dq2, ta=True)
    dp, g["mla_q_norm_g"] = _row_bwd("q_norm_bwd", _f_rms, [(p, 384, blk(lay.q, 384))], [_row(w["mla_q_norm_g"])],
                                     [(dqn, 384, 0)], [((0,), lo)], [0], place=(lay.width, blk(lay.q, 384)), into=dp)
    dp, g["mla_kv_norm_g"] = _row_bwd("kv_norm_bwd", _f_rms, [(p, 256, blk(lay.kv, 256))],
                                      [_row(w["mla_kv_norm_g"])], [(dkvn, 256, 0)], [((0,), lo)], [0],
                                      place=(lay.width, blk(lay.kv, 256)), into=dp)
    if on_attention_done is not None:
        dyb = on_attention_done(dyb)
    dub = _mm("mm_sg_out_dx", dyb, w["sg_out_w"], tb=True)
    g["sg_out_w"] = _mm("mm_sg_out_dw", sv["ub"], dyb, ta=True)
    du, dvn, g["sg_w"], dbcol = _sg_mix_bwd(sv["u"], sv["vn"], w["sg_w"], sv["bcol"], dub)
    g["sg_b"] = dbcol.reshape(SG_GROUPS, SG_CHUNK)
    dp, g["sg_ln_g"], g["sg_ln_b"] = _row_bwd(
        "sg_pre_bwd", _f_sg_pre, [(p, cw, blk(lay.b, cw)), (p, cw, blk(lay.b + cw, cw))],
        [_row(w["sg_ln_g"]), _row(w["sg_ln_b"])], [(du, cw, 0), (dvn, cw, 0)], [((0, 1), lo)], [0, 1],
        place=(lay.width, blk(lay.b, d)), into=dp)
    dz3 = _mm("mm_conv_out_dx", dya, w["conv_out_w"], tb=True)
    g["conv_out_w"] = _mm("mm_conv_out_dw", sv["z3"], dya, ta=True)
    dz1, g["conv_ln_g"], g["conv_ln_b"] = _row_bwd(
        "ln_silu_bwd", _f_ln_silu, [(sv["z1"], cw, 0)], [_row(w["conv_ln_g"]), _row(w["conv_ln_b"])],
        [(dz3, cw, 0)], [((0,), F32)], [0, 1])
    dp, g["conv_dw_w"], g["conv_dw_b"] = _glu_conv_bwd(p, blk(lay.a, 2 * _LANES), w["conv_dw_w"], dz1, dp)
    dh = _mm("mm_in_dx", dp, w["w_in"])
    g["w_in"] = _mm("mm_in_dw", dp, sv["h"], ta=True)
    return dx, dh, g


def _local_step(x, positions, target, layers):
    d = x.shape[1]
    tc, ts = _rope_tables(positions)
    ws = [_prep_layer(wl, d) for wl in layers]
    depth = len(ws)
    (h,) = _row_fwd("rms_first_fwd", _f_rms, [(x, d, 0)], [_row(ws[0]["mix_pre_g"])], [(d, _MXU_DT)])
    saved = []
    for l in range(depth):
        g_next = ws[l + 1]["mix_pre_g"] if l + 1 < depth else None
        x, h, sv = _layer_fwd(x, h, ws[l], g_next, tc, ts)
        saved.append(sv)
    loss, dx = _loss_head(x, target)
    grads = [None] * depth
    dh = None
    for l in reversed(range(depth)):
        g_next = ws[l + 1]["mix_pre_g"] if l + 1 < depth else None
        dx, dh, g = _layer_bwd(dx, dh, ws[l], g_next, saved[l], tc, ts)
        if "next_pre_g" in g:
            grads[l + 1]["mix_pre_g"] = g.pop("next_pre_g")
        grads[l] = g
    x0 = saved[0]["x"]
    grad_x, grads[0]["mix_pre_g"] = _row_bwd("rms_first_bwd", _f_x_rms, [(x0, d, 0)], [_row(ws[0]["mix_pre_g"])],
                                             [(dx, d, 0), (dh, d, 0)], [(0, F32)], [0])
    return loss, grad_x, [_unprep_grads(g, d) for g in grads]


_MATRICES = ("w_in", "conv_out_w", "sg_out_w", "mla_w_uq", "mla_w_ukv", "mla_w_o", "w_out", "ffn_w_up", "ffn_w_down")
_F32_GATHERED = ("conv_dw_w", "ffn_dw_w")
_RS_DT = jnp.bfloat16


_ROW_SHARDED = SHARDED_MID + ("w_in",)


_GATHERED = SHARDED + _F32_GATHERED
_RS_CORE0 = ("w_in", "ffn_w_down")
_LATE_WEIGHTS = ("w_out", "ffn_w_up", "ffn_dw_w", "ffn_w_down")


def _layer_shards(w, l):
    hi = {n: w[n][l].astype(jnp.bfloat16) for n in SHARDED}
    lo = [(w[n][l] - hi[n].astype(F32)).astype(jnp.bfloat16) for n in _F32_GATHERED]
    return [hi[n] for n in SHARDED] + lo


def _layer_weights(names, gathered):
    wl = {}
    for n, g in zip(names, gathered):
        if n in _ROW_SHARDED and g.shape[1] % 16 == 0:
            whole = g.reshape(-1, g.shape[2])
        else:
            whole = jnp.concatenate([g[j] for j in range(4)], axis=0 if n in _ROW_SHARDED else 1)
        if n in wl:
            wl[n] = wl[n].astype(F32) + whole.astype(F32)
        else:
            wl[n] = whole.astype(_MXU_DT) if n in _MATRICES else whole
    return wl


def _by_destination(name, gl):
    if gl.ndim == 3:
        return gl
    k, n = gl.shape
    if name in _ROW_SHARDED:
        return gl.reshape(4, k // 4, n)
    return gl.reshape(k, 4, n // 4).transpose(1, 0, 2)


def kernel(x, positions, mix_pre_g, mix_post_g, ffn_pre_g, ffn_post_g, w_in, conv_dw_w, conv_dw_b, conv_ln_g, conv_ln_b, conv_out_w, sg_ln_g, sg_ln_b, sg_w, sg_b, sg_out_w, mla_q_norm_g, mla_w_uq, mla_kv_norm_g, mla_w_ukv, mla_w_o, w_out, ffn_w_up, ffn_dw_w, ffn_dw_b, ffn_w_down, loss_target, m_mix_pre_g, m_mix_post_g, m_ffn_pre_g, m_ffn_post_g, m_w_in, m_conv_dw_w, m_conv_dw_b, m_conv_ln_g, m_conv_ln_b, m_conv_out_w, m_sg_ln_g, m_sg_ln_b, m_sg_w, m_sg_b, m_sg_out_w, m_mla_q_norm_g, m_mla_w_uq, m_mla_kv_norm_g, m_mla_w_ukv, m_mla_w_o, m_w_out, m_ffn_w_up, m_ffn_dw_w, m_ffn_dw_b, m_ffn_w_down, v_mix_pre_g, v_mix_post_g, v_ffn_pre_g, v_ffn_post_g, v_w_in, v_conv_dw_w, v_conv_dw_b, v_conv_ln_g, v_conv_ln_b, v_conv_out_w, v_sg_ln_g, v_sg_ln_b, v_sg_w, v_sg_b, v_sg_out_w, v_mla_q_norm_g, v_mla_w_uq, v_mla_kv_norm_g, v_mla_w_ukv, v_mla_w_o, v_w_out, v_ffn_w_up, v_ffn_dw_w, v_ffn_dw_b, v_ffn_w_down):
    args = dict(locals())
    w = {n: args[n] for n in WEIGHTS}
    m = {n: args["m_" + n] for n in WEIGHTS}
    v = {n: args["v_" + n] for n in WEIGHTS}
    depth = mix_pre_g.shape[0]

    assert depth == 2, "the two cores of a chip split the communication by layer"
    for t in (w, m, v):
        t["w_in"] = jnp.swapaxes(t["w_in"], 1, 2)
    d = x.shape[-1]
    mesh_x, mesh_y, mesh_c = _mesh_pos()
    my_chip = 2 * mesh_x + mesh_y
    names = list(SHARDED)
    whole = lambda ref, chip, k: ref
    to_my_slot = lambda ref, chip, k: ref.at[2 * lax.axis_index("x") + lax.axis_index("y")]
    block_of_chip = lambda ref, chip, k: ref.at[_chip_index(chip)]
    slot_k = lambda ref, chip, k: ref.at[k]

    late = [i for i, n in enumerate(_GATHERED) if n in _LATE_WEIGHTS]
    early = [i for i in range(len(_GATHERED)) if i not in late]
    pick = lambda seq, idx: [seq[i] for i in idx]
    gathered_names = list(_GATHERED)
    shards0, shards1 = _layer_shards(w, 0), _layer_shards(w, 1)
    replicated = lambda l: {n: w[n][l] for n in REPLICATED}
    land_of = lambda shards: [jax.ShapeDtypeStruct((4,) + a.shape, a.dtype) for a in shards]

    gathered0 = _ag_forward("ag_layer0_early", pick(shards0, early), None, 0, False)
    late0 = pick(shards0, late)
    sems0_s, sems0_r, late0, lands0, token0 = _split_start("ag0_start", [0] * len(late), late0, land_of(late0), whole,
                                                           to_my_slot, gathered0[0])
    ag_owner = [1] * len(shards1)
    sems_s, sems_r, shards1, lands1, token = _split_start("ag1_start", ag_owner, shards1, land_of(shards1), whole,
                                                          to_my_slot, gathered0[0])
    tc, ts = _rope_tables(positions[0])
    ws0 = _prep_layer({**replicated(0), **_layer_weights(pick(gathered_names, early), gathered0)}, d)

    def late_weights(merged):
        got = _split_wait("ag0_wait", [0] * len(late), sems0_s, sems0_r, late0, lands0, [merged], whole, to_my_slot)
        got = _ag_forward("ag_layer0_late", got[0], got[1], 0, True)
        ws0.update(_layer_weights(pick(gathered_names, late), got))
        return ws0

    x0 = x[0] + (token0[0, 0] + token[0, 0])
    (h0,) = _row_fwd("rms_first_fwd", _f_rms, [(x0, d, 0)], [_row(ws0["mix_pre_g"])], [(d, _MXU_DT)])
    x1, h1, sv0 = _layer_fwd(x0, h0, ws0, w["mix_pre_g"][1], tc, ts, late_weights)
    shards1, lands1 = _split_wait("ag1_wait", ag_owner, sems_s, sems_r, shards1, lands1, [x1], whole, to_my_slot)
    gathered1 = _ag_forward("ag_layer1", shards1, lands1, 1, True)
    ws1 = _prep_layer({**replicated(1), **_layer_weights(gathered_names, gathered1)}, d)
    x2, _, sv1 = _layer_fwd(x1, h1, ws1, None, tc, ts)
    loss, dx = _loss_head(x2, loss_target[0])
    loss = lax.psum(loss[0, 0], ("x", "y", "c"))

    owner = {n: 0 if n in _RS_CORE0 else 1 for n in names}
    flag = {n: jnp.stack([my_chip, (mesh_c == owner[n]).astype(jnp.int32)]).astype(jnp.int32) for n in names}
    owners = [owner[n] for n in names]

    def scatter_start(tag, group, gd, got, after):
        own = [owner[n] for n in group]
        t = [_add_pair("rs_pair%s_%s" % (tag, n), g, a, flag[n]) for n, g, a in zip(group, gd, got)]
        lands = [jax.ShapeDtypeStruct((3,) + a.shape[1:], a.dtype) for a in t]
        sems_s, sems_r, t, b, token = _split_start("rs%s_start" % tag, own, t, lands, block_of_chip, slot_k, after)
        return (tag, group, own, sems_s, sems_r, t, b), token

    def scatter_wait(handle, after):
        tag, group, own, sems_s, sems_r, t, b = handle
        t, b = _split_wait("rs%s_wait" % tag, own, sems_s, sems_r, t, b, after, block_of_chip, slot_k)
        return dict(zip(group, zip(t, b)))

    def swap_and_start(tag, group, grads_l, after):
        gd = [_by_destination(n, grads_l[n]) for n in group]
        got = _pair_exchange("rs_swap" + tag, gd, [owner[n] for n in group], True)
        return scatter_start(tag, group, gd, got, after)

    dx, dh, gk1 = _layer_bwd(dx, None, ws1, None, sv1, tc, ts)
    grads1 = _unprep_grads(gk1, d)
    gd1 = [_by_destination(n, grads1[n]) for n in names]
    swap1 = _swap_split("rs_swap1_start", gd1, [jax.ShapeDtypeStruct(g.shape, g.dtype) for g in gd1], owners, None,
                        dh)
    dx = dx + swap1[4][0, 0]
    late_group = [n for n in names if n in _LATE_WEIGHTS]
    early_group = [n for n in names if n not in _LATE_WEIGHTS]
    handles = []

    own_late = [owner[n] for n in late_group]
    pending = []

    def on_late_grads(g, value):
        gd, got = _swap_split("rs_swap1_wait", swap1[2], swap1[3], owners, swap1[:2], [value])
        handle1, tok1 = scatter_start("1", names, gd, got, value)
        handles.append(handle1)
        gd = [_by_destination(n, g[n]) for n in late_group]
        swap = _swap_split("rs_swap0_late_start", gd, [jax.ShapeDtypeStruct(a.shape, a.dtype) for a in gd], own_late,
                           None, value)
        pending.append(swap)
        return value + (tok1[0, 0] + swap[4][0, 0])

    def on_attention_done(value):
        swap = pending[0]
        gd, got = _swap_split("rs_swap0_late_wait", swap[2], swap[3], own_late, swap[:2], [value])
        handle0, tok0 = scatter_start("0_late", late_group, gd, got, value)
        handles.append(handle0)
        return value + tok0[0, 0].astype(value.dtype)

    dx, dh, gk0 = _layer_bwd(dx, dh, ws0, ws1["mix_pre_g"], sv0, tc, ts, on_late_grads, on_attention_done)
    grads1["mix_pre_g"] = gk0.pop("next_pre_g")
    grad_x, gk0["mix_pre_g"] = _row_bwd("rms_first_bwd", _f_x_rms, [(x0, d, 0)], [_row(ws0["mix_pre_g"])],
                                        [(dx, d, 0), (dh, d, 0)], [(0, F32)], [0])
    tb1 = scatter_wait(handles[0], [grad_x])
    grads0 = _unprep_grads(gk0, d)
    handle, token = swap_and_start("0_early", early_group, grads0, grad_x)
    handles.append(handle)

    def finish(l, sums, updates):
        sums = _pair_exchange("rs_join%d" % l, sums, owners, False, layer=l)
        updates = [_adamw("adamw%d_%s" % (l, n), w[n], sums[i], m[n], v[n], l, updates[i])
                   for i, n in enumerate(names)]
        return sums, updates

    sums = [_add_quads("rs_sum1_" + n, *tb1[n], 1, flag[n], None, token) for n in names]
    sums, updates = finish(1, sums, [None] * len(names))
    out = {}
    rep = list(REPLICATED)
    grads = [grads0, grads1]
    g_rep = [jnp.stack([grads[l][n].reshape(w[n].shape[1:]) for l in range(depth)]) for n in rep]
    for n, *res in zip(rep, *_all_reduce_adamw(g_rep, [w[n] for n in rep], [m[n] for n in rep], [v[n] for n in rep])):
        out[n] = tuple(res)
    hidden = [u[0] for u in updates] + [out[rep[0]][0]]
    tb0 = {**scatter_wait(handles[1], hidden), **scatter_wait(handles[2], hidden)}
    sums = [_add_quads("rs_sum0_" + n, *tb0[n], 0, flag[n], sums[i]) for i, n in enumerate(names)]
    sums, updates = finish(0, sums, updates)
    for n, gr, upd in zip(names, sums, updates):
        out[n] = (gr, *upd)
    out["w_in"] = tuple(jnp.swapaxes(a, 1, 2) for a in out["w_in"])
    return (loss, grad_x[None], *[out[n][i] for i in range(4) for n in WEIGHTS])
```

```python
import functools
import math

import jax
import jax.numpy as jnp
from jax import lax
from jax.experimental import pallas as pl
from jax.experimental.pallas import tpu as pltpu

F32 = jnp.float32
_MXU_DT = jnp.bfloat16
_VMEM_LIMIT = 48 * 1024 * 1024
_LANES = 128
_MESH = pl.DeviceIdType.MESH

N_HEADS = 8
QK_NOPE = 64
QK_ROPE = 32
V_HEAD = 64
HEAD_PAD = 128
SG_GROUPS = 4
SG_CHUNK = 128
CONV_K = 31
FFN_K = 3
ROPE_THETA = 10000.0
EPS = 1e-6
ADAM_LR, ADAM_B1, ADAM_B2, ADAM_EPS, ADAM_WD, ADAM_STEP = 0.001, 0.9, 0.999, 1e-08, 0.01, 10

SHARDED_LAST = ("w_in", "conv_dw_w", "conv_out_w", "sg_out_w", "mla_w_uq", "mla_w_ukv", "mla_w_o", "ffn_w_up",
                "ffn_dw_w")
SHARDED_MID = ("w_out", "ffn_w_down")
SHARDED = SHARDED_LAST + SHARDED_MID
WEIGHTS = ("mix_pre_g", "mix_post_g", "ffn_pre_g", "ffn_post_g", "w_in", "conv_dw_w", "conv_dw_b", "conv_ln_g",
           "conv_ln_b", "conv_out_w", "sg_ln_g", "sg_ln_b", "sg_w", "sg_b", "sg_out_w", "mla_q_norm_g", "mla_w_uq",
           "mla_kv_norm_g", "mla_w_ukv", "mla_w_o", "w_out", "ffn_w_up", "ffn_dw_w", "ffn_dw_b", "ffn_w_down")
REPLICATED = tuple(n for n in WEIGHTS if n not in SHARDED)


def _cparams(sem=None):
    return pltpu.CompilerParams(dimension_semantics=sem, vmem_limit_bytes=_VMEM_LIMIT)


def _pick(n, cands):
    for c in cands:
        if n % c == 0:
            return c
    return n


def _largest_tile(dim, cap):
    for t in range(min(cap, dim) // _LANES * _LANES, 0, -_LANES):
        if dim % t == 0:
            return t
    return dim


_MM_VMEM_BUDGET = 36 * 1024 * 1024
_MM_TM_CAP, _MM_TN_CAP, _MM_TK_CAP = 1024, 1536, 3072


def _mm(name, a, b, *, ta=False, tb=False, out_dtype=F32, a_halves=False, b_halves=False, out_quarters=False):
    assert not (a_halves and ta) and not (b_halves and tb)
    if a_halves:
        m, kdim = a.shape[1], 2 * a.shape[2]
    else:
        (kdim, m) = a.shape if ta else a.shape[::-1]
    if b_halves:
        kdim2, n = b.shape[1], 2 * b.shape[2]
    else:
        (n, kdim2) = b.shape if tb else b.shape[::-1]
    assert kdim == kdim2, (a.shape, b.shape, ta, tb)
    tk = _largest_tile(kdim // 2 if a_halves else kdim, _MM_TK_CAP)
    tn = _largest_tile(n // 4 if out_quarters else (n // 2 if b_halves else n), _MM_TN_CAP)
    nk = kdim // tk
    ab, bb, ob = a.dtype.itemsize, b.dtype.itemsize, jnp.dtype(out_dtype).itemsize
    tm = _largest_tile(m, _MM_TM_CAP)
    vmem = lambda t: 2 * (t * tk * ab + tk * tn * bb + t * tn * ob) + (t * tn * 4 if nk > 1 else 0)
    while vmem(tm) > _MM_VMEM_BUDGET and tm > _LANES:
        tm = _largest_tile(m, tm - _LANES)
    dims = (((0 if ta else 1,), (1 if tb else 0,)), ((), ()))

    def dot(a_ref, b_ref):
        return lax.dot_general(a_ref[...].astype(_MXU_DT), b_ref[...].astype(_MXU_DT), dims,
                               preferred_element_type=F32)

    def body_one(a_ref, b_ref, o_ref):
        o_ref[...] = dot(a_ref, b_ref).astype(o_ref.dtype)

    def body_acc(a_ref, b_ref, o_ref, acc_ref):
        k = pl.program_id(2)

        @pl.when(k == 0)
        def _():
            acc_ref[...] = jnp.zeros_like(acc_ref)

        acc_ref[...] += dot(a_ref, b_ref)

        @pl.when(k == nk - 1)
        def _():
            o_ref[...] = acc_ref[...].astype(o_ref.dtype)

    if a_halves:
        per = nk // 2
        a_spec = pl.BlockSpec((None, tm, tk), lambda i, j, k: (k // per, i, k % per))
    elif ta:
        a_spec = pl.BlockSpec((tk, tm), lambda i, j, k: (k, i))
    else:
        a_spec = pl.BlockSpec((tm, tk), lambda i, j, k: (i, k))
    if b_halves:
        per_b = n // 2 // tn
        b_spec = pl.BlockSpec((None, tk, tn), lambda i, j, k: (j // per_b, k, j % per_b))
    elif tb:
        b_spec = pl.BlockSpec((tn, tk), lambda i, j, k: (j, k))
    else:
        b_spec = pl.BlockSpec((tk, tn), lambda i, j, k: (k, j))
    if out_quarters:
        per_o = n // 4 // tn
        o_spec = pl.BlockSpec((None, tm, tn), lambda i, j, k: (j // per_o, i, j % per_o))
        o_shape = jax.ShapeDtypeStruct((4, m, n // 4), out_dtype)
    else:
        o_spec = pl.BlockSpec((tm, tn), lambda i, j, k: (i, j))
        o_shape = jax.ShapeDtypeStruct((m, n), out_dtype)
    return pl.pallas_call(
        body_one if nk == 1 else body_acc, name=name, grid=(m // tm, n // tn, nk),
        in_specs=[a_spec, b_spec], out_specs=o_spec, out_shape=o_shape,
        scratch_shapes=[] if nk == 1 else [pltpu.VMEM((tm, tn), F32)],
        compiler_params=_cparams(("parallel", "parallel", "arbitrary")),
    )(a, b)


def _row_spec(tm, width, idx):
    return pl.BlockSpec((tm, width), lambda i: (i, idx))


def _full_spec(shape):
    zeros = (0,) * len(shape)
    return pl.BlockSpec(shape, lambda i: zeros)


def _row_fwd(name, fn, rows, params, outs, tm=256):
    s = rows[0][0].shape[0]
    nr, npar = len(rows), len(params)

    def body(*refs):
        vals = [r[...].astype(F32) for r in refs[:nr + npar]]
        res = fn(*vals)
        for o_ref, r in zip(refs[nr + npar:], res):
            o_ref[...] = r.astype(o_ref.dtype)

    return pl.pallas_call(
        body, name=name, grid=(s // tm,),
        in_specs=[_row_spec(tm, w, i) for _, w, i in rows] + [_full_spec(p.shape) for p in params],
        out_specs=[_row_spec(tm, w, 0) for w, _ in outs],
        out_shape=[jax.ShapeDtypeStruct((s, w), dt) for w, dt in outs],
        compiler_params=_cparams(("parallel",)),
    )(*[r[0] for r in rows], *params)


def _row_bwd(name, fn, rows, params, cots, row_grads, param_grads, tm=256, place=None, into=None):
    s = rows[0][0].shape[0]
    nr, npar, nc = len(rows), len(params), len(cots)
    row_grads = [((idxs,) if isinstance(idxs, int) else tuple(idxs), dt) for idxs, dt in row_grads]
    widths = [sum(rows[i][1] for i in idxs) for idxs, _ in row_grads]

    def body(*refs):
        i = pl.program_id(0)
        vals = [r[...].astype(F32) for r in refs[:nr + npar]]
        cvals = tuple(r[...].astype(F32) for r in refs[nr + npar:nr + npar + nc])
        _, vjp = jax.vjp(fn, *vals)
        grads = vjp(cvals)
        outs = refs[nr + npar + nc + (into is not None):]
        for o_ref, (idxs, _) in zip(outs, row_grads):
            pos = 0
            for idx in idxs:
                o_ref[:, pos:pos + rows[idx][1]] = grads[idx].astype(o_ref.dtype)
                pos += rows[idx][1]
        for o_ref, idx in zip(outs[len(row_grads):], param_grads):
            @pl.when(i == 0)
            def _(o_ref=o_ref):
                o_ref[...] = jnp.zeros_like(o_ref)

            o_ref[...] += grads[nr + idx]

    out_specs = [_row_spec(tm, w, 0) for w in widths] + [_full_spec(params[idx].shape) for idx in param_grads]
    out_shape = ([jax.ShapeDtypeStruct((s, w), dt) for w, (_, dt) in zip(widths, row_grads)]
                 + [jax.ShapeDtypeStruct(params[idx].shape, F32) for idx in param_grads])
    extra, aliases = [], {}
    if place is not None:
        out_specs[0] = _row_spec(tm, widths[0], place[1])
        out_shape[0] = jax.ShapeDtypeStruct((s, place[0]), row_grads[0][1])
    if into is not None:
        extra, aliases = [into], {nr + npar + nc: 0}
    return pl.pallas_call(
        body, name=name, grid=(s // tm,),
        in_specs=([_row_spec(tm, w, i) for _, w, i in rows] + [_full_spec(p.shape) for p in params]
                  + [_row_spec(tm, w, i) for _, w, i in cots] + [_ANY] * len(extra)),
        out_specs=out_specs, out_shape=out_shape, input_output_aliases=aliases,
        compiler_params=_cparams(("arbitrary",)),
    )(*[r[0] for r in rows], *params, *[c[0] for c in cots], *extra)


def _rms(x, g):
    return x * lax.rsqrt(jnp.mean(x * x, axis=-1, keepdims=True) + EPS) * g


def _ln(x, g, b):
    mu = jnp.mean(x, axis=-1, keepdims=True)
    xc = x - mu
    var = jnp.mean(xc * xc, axis=-1, keepdims=True)
    return xc * lax.rsqrt(var + EPS) * g + b


def _sigmoid(x):
    return 1.0 / (1.0 + jnp.exp(-x))


def _gelu(x):
    return x * (0.5 * (1.0 + jnp.tanh(math.sqrt(2.0 / math.pi) * (x + 0.044715 * (x * x * x)))))


def _f_rms(x, g):
    return (_rms(x, g),)


def _f_x_rms(x, g):
    return (x, _rms(x, g))


def _f_ln_silu(z, g, b):
    y = _ln(z, g, b)
    return (y * _sigmoid(y),)


def _f_sg_pre(bu, bv, g, b):
    return (_gelu(bu), _ln(_gelu(bv), g, b))


def _f_merge(g0, g1, g2, ya, yb, yc):
    return (_sigmoid(g0) * ya + _sigmoid(g1) * yb + _sigmoid(g2) * yc,)


def _f_resid_rms(x, t, g_post):
    return (x + _rms(t, g_post),)


def _f_resid_rms_rms(x, t, g_post, g_next):
    x1 = x + _rms(t, g_post)
    return (x1, _rms(x1, g_next))


def _f_geglu(zg, zv):
    return _gelu(zg) * zv


_CONV_TILE_ELEMS = 16 * 1024


def _conv_tr(c):
    return _CONV_TILE_ELEMS // c


def _conv_tile(zp_ref, w_ref, bias, k_taps, off, r0):
    c = zp_ref.shape[1]
    tr = _conv_tr(c)
    acc = jnp.broadcast_to(bias, (tr, c))
    for k in range(k_taps):
        acc = acc + w_ref[k:k + 1, :] * zp_ref[r0 + off + k:r0 + off + k + tr, :]
    return acc


def _conv_bwd_input_tile(dzp_ref, w_ref, k_taps, r0):
    c = dzp_ref.shape[1]
    tr = _conv_tr(c)
    acc = jnp.zeros((tr, c), F32)
    for k in range(k_taps):
        s0 = r0 + (k_taps - 1) - k
        acc = acc + w_ref[k:k + 1, :] * dzp_ref[s0:s0 + tr, :]
    return acc


def _conv_bwd_weight(dzp_ref, zp_ref, dw_ref, db_ref, k_taps, off, s):
    c = zp_ref.shape[1]
    tr = _conv_tr(c)
    fold = lambda v: jnp.sum(v.reshape(tr // 8, 8, c), axis=0)
    for k in range(k_taps):
        acc = jnp.zeros((8, c), F32)
        for r in range(s // tr):
            r0 = r * tr
            acc = acc + fold(dzp_ref[r0:r0 + tr, :] * zp_ref[r0 + off + k:r0 + off + k + tr, :])
        dw_ref[k:k + 1, :] = jnp.sum(acc, axis=0, keepdims=True)
    acc = jnp.zeros((8, c), F32)
    for r in range(s // tr):
        acc = acc + fold(dzp_ref[r * tr:(r + 1) * tr, :])
    db_ref[...] = jnp.sum(acc, axis=0, keepdims=True)


def _glu_conv_fwd(p, blk0, w, b):
    s = p.shape[0]
    k_taps, c = w.shape
    cb, pad = _LANES, 32
    off = pad - (k_taps - 1)

    def body(a_ref, w_ref, b_ref, o_ref, zp_ref):
        zp_ref[0:pad, :] = jnp.zeros((pad, cb), F32)
        zp_ref[pad:pad + s, :] = a_ref[:, 0:cb] * _sigmoid(a_ref[:, cb:2 * cb])
        tr = _conv_tr(cb)
        for r in range(s // tr):
            o_ref[r * tr:(r + 1) * tr, :] = _conv_tile(zp_ref, w_ref, b_ref[...], k_taps, off, r * tr)

    return pl.pallas_call(
        body, name="glu_conv_fwd", grid=(c // cb,),
        in_specs=[pl.BlockSpec((s, 2 * cb), lambda j: (0, blk0 + j)),
                  pl.BlockSpec((k_taps, cb), lambda j: (0, j)), pl.BlockSpec((1, cb), lambda j: (0, j))],
        out_specs=pl.BlockSpec((s, cb), lambda j: (0, j)),
        out_shape=jax.ShapeDtypeStruct((s, c), F32),
        scratch_shapes=[pltpu.VMEM((s + pad, cb), F32)],
        compiler_params=_cparams(("parallel",)),
    )(p, w, b)


def _glu_conv_bwd(p, blk0, w, dz, dp):
    s = p.shape[0]
    k_taps, c = w.shape
    cb, pad = _LANES, 32
    off = pad - (k_taps - 1)

    def body(a_ref, w_ref, dz_ref, dp_in, da_ref, dw_ref, db_ref, zp_ref, dzp_ref):
        zp_ref[0:pad, :] = jnp.zeros((pad, cb), F32)
        zp_ref[pad:pad + s, :] = a_ref[:, 0:cb] * _sigmoid(a_ref[:, cb:2 * cb])
        dzp_ref[0:s, :] = dz_ref[...]
        dzp_ref[s:s + pad, :] = jnp.zeros((pad, cb), F32)
        tr = _conv_tr(cb)
        for r in range(s // tr):
            rows = slice(r * tr, (r + 1) * tr)
            dz0 = _conv_bwd_input_tile(dzp_ref, w_ref, k_taps, r * tr)
            sg = _sigmoid(a_ref[rows, cb:2 * cb])
            da_ref[rows, 0:cb] = (dz0 * sg).astype(da_ref.dtype)
            da_ref[rows, cb:2 * cb] = (dz0 * a_ref[rows, 0:cb] * sg * (1.0 - sg)).astype(da_ref.dtype)
        _conv_bwd_weight(dzp_ref, zp_ref, dw_ref, db_ref, k_taps, off, s)

    return pl.pallas_call(
        body, name="glu_conv_bwd", grid=(c // cb,),
        in_specs=[pl.BlockSpec((s, 2 * cb), lambda j: (0, blk0 + j)),
                  pl.BlockSpec((k_taps, cb), lambda j: (0, j)), pl.BlockSpec((s, cb), lambda j: (0, j)), _ANY],
        out_specs=[pl.BlockSpec((s, 2 * cb), lambda j: (0, blk0 + j)),
                   pl.BlockSpec((k_taps, cb), lambda j: (0, j)), pl.BlockSpec((1, cb), lambda j: (0, j))],
        out_shape=[jax.ShapeDtypeStruct(dp.shape, dp.dtype),
                   jax.ShapeDtypeStruct((k_taps, c), F32), jax.ShapeDtypeStruct((1, c), F32)],
        scratch_shapes=[pltpu.VMEM((s + pad, cb), F32), pltpu.VMEM((s + pad, cb), F32)],
        input_output_aliases={3: 0},
        compiler_params=_cparams(("parallel",)),
    )(p, w, dz, dp)


def _conv_geglu_fwd(up, w, b):
    s, f2 = up.shape
    f = f2 // 2
    k_taps = w.shape[0]
    cb, pad = _FFN_CB, 8
    off = pad - (k_taps - 1)
    nb = f // cb

    def body(ug_ref, uv_ref, wg_ref, wv_ref, bg_ref, bv_ref, o_ref, z_ref, w_ref, b_ref):
        _pair(w_ref, wg_ref[...], wv_ref[...], cb)
        _pair(b_ref, bg_ref[...], bv_ref[...], cb)
        z_ref[0:pad, :] = jnp.zeros((pad, 2 * cb), F32)
        z_ref[pad:pad + s, 0:cb] = ug_ref[...]
        z_ref[pad:pad + s, cb:2 * cb] = uv_ref[...]
        tr = _conv_tr(2 * cb)
        for r in range(s // tr):
            z = _conv_tile(z_ref, w_ref, b_ref[...], k_taps, off, r * tr)
            o_ref[r * tr:(r + 1) * tr, :] = _f_geglu(z[:, 0:cb], z[:, cb:2 * cb]).astype(o_ref.dtype)

    two = lambda rows_: [pl.BlockSpec((rows_, cb), lambda j: (0, j)), pl.BlockSpec((rows_, cb), lambda j: (0, nb + j))]
    return pl.pallas_call(
        body, name="conv_geglu_fwd", grid=(nb,),
        in_specs=two(s) + two(k_taps) + two(1),
        out_specs=pl.BlockSpec((s, cb), lambda j: (0, j)),
        out_shape=jax.ShapeDtypeStruct((s, f), _MXU_DT),
        scratch_shapes=[pltpu.VMEM((s + pad, 2 * cb), F32), pltpu.VMEM((k_taps, 2 * cb), F32),
                        pltpu.VMEM((1, 2 * cb), F32)],
        compiler_params=_cparams(("parallel",)),
    )(up, up, w, w, b, b)


def _pair(dst_ref, first, second, cb):
    dst_ref[:, 0:cb] = first
    dst_ref[:, cb:2 * cb] = second


def _conv_geglu_bwd(up, w, b, dact):
    s, f2 = up.shape
    f = f2 // 2
    k_taps = w.shape[0]
    cb, pad = _FFN_CB, 8
    off = pad - (k_taps - 1)
    nb = f // cb

    def body(ug_ref, uv_ref, wg_ref, wv_ref, bg_ref, bv_ref, da_ref, du_ref, dw_ref, db_ref, z_ref, dz_ref, w_ref,
             b_ref, dw_sc, db_sc):
        _pair(w_ref, wg_ref[...], wv_ref[...], cb)
        _pair(b_ref, bg_ref[...], bv_ref[...], cb)
        z_ref[0:pad, :] = jnp.zeros((pad, 2 * cb), F32)
        z_ref[pad:pad + s, 0:cb] = ug_ref[...]
        z_ref[pad:pad + s, cb:2 * cb] = uv_ref[...]
        dz_ref[s:s + pad, :] = jnp.zeros((pad, 2 * cb), F32)
        tr = _conv_tr(2 * cb)
        for r in range(s // tr):
            rows = slice(r * tr, (r + 1) * tr)
            z = _conv_tile(z_ref, w_ref, b_ref[...], k_taps, off, r * tr)
            _, vjp = jax.vjp(_f_geglu, z[:, 0:cb], z[:, cb:2 * cb])
            dzg, dzv = vjp(da_ref[rows, :].astype(F32))
            dz_ref[rows, 0:cb] = dzg
            dz_ref[rows, cb:2 * cb] = dzv
        for r in range(s // tr):
            rows = slice(r * tr, (r + 1) * tr)
            du = _conv_bwd_input_tile(dz_ref, w_ref, k_taps, r * tr).astype(du_ref.dtype)
            du_ref[0, rows, :] = du[:, 0:cb]
            du_ref[1, rows, :] = du[:, cb:2 * cb]
        _conv_bwd_weight(dz_ref, z_ref, dw_sc, db_sc, k_taps, off, s)
        for half in range(2):
            dw_ref[half] = dw_sc[:, half * cb:(half + 1) * cb]
            db_ref[half] = db_sc[:, half * cb:(half + 1) * cb]

    two = lambda rows_: [pl.BlockSpec((rows_, cb), lambda j: (0, j)), pl.BlockSpec((rows_, cb), lambda j: (0, nb + j))]
    both = lambda rows_: pl.BlockSpec((2, rows_, cb), lambda j: (0, 0, j))
    return pl.pallas_call(
        body, name="conv_geglu_bwd", grid=(nb,),
        in_specs=two(s) + two(k_taps) + two(1) + [pl.BlockSpec((s, cb), lambda j: (0, j))],
        out_specs=[both(s), both(k_taps), both(1)],
        out_shape=[jax.ShapeDtypeStruct((2, s, f), _MXU_DT), jax.ShapeDtypeStruct((2, k_taps, f), F32),
                   jax.ShapeDtypeStruct((2, 1, f), F32)],
        scratch_shapes=[pltpu.VMEM((s + pad, 2 * cb), F32), pltpu.VMEM((s + pad, 2 * cb), F32),
                        pltpu.VMEM((k_taps, 2 * cb), F32), pltpu.VMEM((1, 2 * cb), F32),
                        pltpu.VMEM((k_taps, 2 * cb), F32), pltpu.VMEM((1, 2 * cb), F32)],
        compiler_params=_cparams(("parallel",)),
    )(up, up, w, w, b, b, dact)


def _tril_mask():
    t = lax.broadcasted_iota(jnp.int32, (SG_CHUNK, SG_CHUNK), 0)
    s = lax.broadcasted_iota(jnp.int32, (SG_CHUNK, SG_CHUNK), 1)
    return t >= s


def _sg_mix_fwd(u, vn, w, bcol):
    s, c = u.shape
    gw = c // SG_GROUPS

    def body(u_ref, v_ref, w_ref, b_ref, o_ref):
        wm = jnp.where(_tril_mask(), w_ref[0], 0.0).astype(_MXU_DT)
        for n in range(s // SG_CHUNK):
            rows = slice(n * SG_CHUNK, (n + 1) * SG_CHUNK)
            mixed = jnp.dot(wm, v_ref[rows, :], preferred_element_type=F32) + b_ref[0]
            o_ref[rows, :] = (u_ref[rows, :] * mixed).astype(o_ref.dtype)

    return pl.pallas_call(
        body, name="sg_mix_fwd", grid=(SG_GROUPS,),
        in_specs=[pl.BlockSpec((s, gw), lambda g: (0, g)), pl.BlockSpec((s, gw), lambda g: (0, g)),
                  pl.BlockSpec((1, SG_CHUNK, SG_CHUNK), lambda g: (g, 0, 0)),
                  pl.BlockSpec((1, SG_CHUNK, 1), lambda g: (g, 0, 0))],
        out_specs=pl.BlockSpec((s, gw), lambda g: (0, g)),
        out_shape=jax.ShapeDtypeStruct((s, c), _MXU_DT),
        compiler_params=_cparams(("parallel",)),
    )(u, vn, w, bcol)


def _sg_mix_bwd(u, vn, w, bcol, dub):
    s, c = u.shape
    gw = c // SG_GROUPS

    def body(u_ref, v_ref, w_ref, b_ref, d_ref, du_ref, dv_ref, dw_ref, db_ref):
        mask = _tril_mask()
        wm = jnp.where(mask, w_ref[0], 0.0).astype(_MXU_DT)
        dw = jnp.zeros((SG_CHUNK, SG_CHUNK), F32)
        db = jnp.zeros((SG_CHUNK, 1), F32)
        for n in range(s // SG_CHUNK):
            rows = slice(n * SG_CHUNK, (n + 1) * SG_CHUNK)
            v = v_ref[rows, :]
            d = d_ref[rows, :].astype(F32)
            mixed = jnp.dot(wm, v, preferred_element_type=F32) + b_ref[0]
            du_ref[rows, :] = d * mixed
            dmix = d * u_ref[rows, :]
            dmix_lo = dmix.astype(_MXU_DT)
            dv_ref[rows, :] = lax.dot_general(wm, dmix_lo, (((0,), (0,)), ((), ())), preferred_element_type=F32)
            dw = dw + lax.dot_general(dmix_lo, v, (((1,), (1,)), ((), ())), preferred_element_type=F32)
            db = db + jnp.sum(dmix, axis=1, keepdims=True)
        dw_ref[0] = jnp.where(mask, dw, 0.0)
        db_ref[0] = db

    return pl.pallas_call(
        body, name="sg_mix_bwd", grid=(SG_GROUPS,),
        in_specs=[pl.BlockSpec((s, gw), lambda g: (0, g)), pl.BlockSpec((s, gw), lambda g: (0, g)),
                  pl.BlockSpec((1, SG_CHUNK, SG_CHUNK), lambda g: (g, 0, 0)),
                  pl.BlockSpec((1, SG_CHUNK, 1), lambda g: (g, 0, 0)), pl.BlockSpec((s, gw), lambda g: (0, g))],
        out_specs=[pl.BlockSpec((s, gw), lambda g: (0, g)), pl.BlockSpec((s, gw), lambda g: (0, g)),
                   pl.BlockSpec((1, SG_CHUNK, SG_CHUNK), lambda g: (g, 0, 0)),
                   pl.BlockSpec((1, SG_CHUNK, 1), lambda g: (g, 0, 0))],
        out_shape=[jax.ShapeDtypeStruct((s, c), F32), jax.ShapeDtypeStruct((s, c), F32),
                   jax.ShapeDtypeStruct((SG_GROUPS, SG_CHUNK, SG_CHUNK), F32),
                   jax.ShapeDtypeStruct((SG_GROUPS, SG_CHUNK, 1), F32)],
        compiler_params=_cparams(("parallel",)),
    )(u, vn, w, bcol, dub)


def _rope_fwd(q2, kv2, p, krm_idx, krs_idx, tc, ts):
    s = q2.shape[0]
    hw = N_HEADS * HEAD_PAD
    tm = 256

    def body(qm_ref, qs_ref, kn_ref, v_ref, krm_ref, krs_ref, tc_ref, ts_ref, q_ref, k_ref, vo_ref):
        tcv, tsv = tc_ref[...], ts_ref[...]
        kpe = krm_ref[...] * tcv + krs_ref[...] * tsv
        for h in range(N_HEADS):
            cols = slice(h * HEAD_PAD, (h + 1) * HEAD_PAD)
            q_ref[:, cols] = (qm_ref[:, cols] * tcv + qs_ref[:, cols] * tsv).astype(q_ref.dtype)
            k_ref[:, cols] = (kn_ref[:, cols] + kpe).astype(k_ref.dtype)
        vo_ref[...] = v_ref[...].astype(vo_ref.dtype)

    return pl.pallas_call(
        body, name="rope_fwd", grid=(s // tm,),
        in_specs=[_row_spec(tm, hw, 0), _row_spec(tm, hw, 1), _row_spec(tm, hw, 0), _row_spec(tm, hw, 1),
                  _row_spec(tm, HEAD_PAD, krm_idx), _row_spec(tm, HEAD_PAD, krs_idx),
                  _row_spec(tm, HEAD_PAD, 0), _row_spec(tm, HEAD_PAD, 0)],
        out_specs=[_row_spec(tm, hw, 0)] * 3,
        out_shape=[jax.ShapeDtypeStruct((s, hw), _MXU_DT)] * 3,
        compiler_params=_cparams(("parallel",)),
    )(q2, q2, kv2, kv2, p, p, tc, ts)


def _rope_bwd(dq, dk, dv, tc, ts, dp, kr_blk):
    s = dq.shape[0]
    hw = N_HEADS * HEAD_PAD
    tm = 256

    def body(dq_ref, dk_ref, dv_ref, tc_ref, ts_ref, dp_in, dq2_ref, dkv2_ref, dkr_ref):
        tcv, tsv = tc_ref[...], ts_ref[...]
        dkpe = jnp.zeros((tm, HEAD_PAD), F32)
        for h in range(N_HEADS):
            cols = slice(h * HEAD_PAD, (h + 1) * HEAD_PAD)
            dqh = dq_ref[:, cols]
            dq2_ref[:, cols] = (dqh * tcv).astype(dq2_ref.dtype)
            dq2_ref[:, hw + h * HEAD_PAD:hw + (h + 1) * HEAD_PAD] = (dqh * tsv).astype(dq2_ref.dtype)
            dkpe = dkpe + dk_ref[:, cols]
        dkv2_ref[:, 0:hw] = dk_ref[...].astype(dkv2_ref.dtype)
        dkv2_ref[:, hw:2 * hw] = dv_ref[...].astype(dkv2_ref.dtype)
        dkr_ref[:, 0:HEAD_PAD] = (dkpe * tcv).astype(dkr_ref.dtype)
        dkr_ref[:, HEAD_PAD:2 * HEAD_PAD] = (dkpe * tsv).astype(dkr_ref.dtype)
        dkr_ref[:, 2 * HEAD_PAD:3 * HEAD_PAD] = jnp.zeros((tm, HEAD_PAD), dkr_ref.dtype)

    return pl.pallas_call(
        body, name="rope_bwd", grid=(s // tm,),
        in_specs=[_row_spec(tm, hw, 0)] * 3 + [_row_spec(tm, HEAD_PAD, 0)] * 2 + [_ANY],
        out_specs=[_row_spec(tm, 2 * hw, 0), _row_spec(tm, 2 * hw, 0), _row_spec(tm, 3 * HEAD_PAD, kr_blk)],
        out_shape=[jax.ShapeDtypeStruct((s, 2 * hw), _MXU_DT), jax.ShapeDtypeStruct((s, 2 * hw), _MXU_DT),
                   jax.ShapeDtypeStruct(dp.shape, dp.dtype)],
        input_output_aliases={5: 2},
        compiler_params=_cparams(("parallel",)),
    )(dq, dk, dv, tc, ts, dp)


_ATTN_TQ = 512
_ATTN_SCALE = (QK_NOPE + QK_ROPE) ** -0.5


def _attn_probs(q, k, i):
    s = k.shape[0]
    sc = lax.dot_general(q, k, (((1,), (1,)), ((), ())), preferred_element_type=F32) * _ATTN_SCALE
    row = i * _ATTN_TQ + lax.broadcasted_iota(jnp.int32, (_ATTN_TQ, s), 0)
    col = lax.broadcasted_iota(jnp.int32, (_ATTN_TQ, s), 1)
    sc = jnp.where(row >= col, sc, jnp.finfo(F32).min)
    e = jnp.exp(sc - jnp.max(sc, axis=1, keepdims=True))
    return e * (1.0 / jnp.sum(e, axis=1, keepdims=True))


def _per_query_block(s, fn):
    i = pl.program_id(1)
    for n in range(s // _ATTN_TQ):
        @pl.when(i == n)
        def _(n=n):
            fn(n, (n + 1) * _ATTN_TQ)


def _attn_fwd(q, k, v):
    s = q.shape[0]

    def body(q_ref, k_ref, v_ref, o_ref):
        def block(n, kl):
            p = _attn_probs(q_ref[...], k_ref[0:kl, :], n)
            o_ref[...] = jnp.dot(p.astype(_MXU_DT), v_ref[0:kl, :], preferred_element_type=F32).astype(o_ref.dtype)

        _per_query_block(s, block)

    qspec = pl.BlockSpec((_ATTN_TQ, HEAD_PAD), lambda h, i: (i, h))
    kspec = pl.BlockSpec((s, HEAD_PAD), lambda h, i: (0, h))
    return pl.pallas_call(
        body, name="attn_fwd", grid=(N_HEADS, s // _ATTN_TQ),
        in_specs=[qspec, kspec, kspec], out_specs=qspec,
        out_shape=jax.ShapeDtypeStruct(q.shape, _MXU_DT),
        compiler_params=_cparams(("parallel", "parallel")),
    )(q, k, v)


def _attn_bwd(q, k, v, do):
    s = q.shape[0]

    def body(q_ref, k_ref, v_ref, do_ref, dq_ref, dk_ref, dv_ref):
        i = pl.program_id(1)

        @pl.when(i == 0)
        def _():
            dk_ref[...] = jnp.zeros_like(dk_ref)
            dv_ref[...] = jnp.zeros_like(dv_ref)

        def block(n, kl):
            qv, kv, dov = q_ref[...], k_ref[0:kl, :], do_ref[...]
            p = _attn_probs(qv, kv, n)
            dp = lax.dot_general(dov, v_ref[0:kl, :], (((1,), (1,)), ((), ())), preferred_element_type=F32)
            delta = jnp.sum(p * dp, axis=1, keepdims=True)
            ds = (p * (dp - delta) * _ATTN_SCALE).astype(_MXU_DT)
            dq_ref[...] = jnp.dot(ds, kv, preferred_element_type=F32)
            dk_ref[0:kl, :] += lax.dot_general(ds, qv, (((0,), (0,)), ((), ())), preferred_element_type=F32)
            dv_ref[0:kl, :] += lax.dot_general(p.astype(_MXU_DT), dov, (((0,), (0,)), ((), ())),
                                               preferred_element_type=F32)

        _per_query_block(s, block)

    qspec = pl.BlockSpec((_ATTN_TQ, HEAD_PAD), lambda h, i: (i, h))
    kspec = pl.BlockSpec((s, HEAD_PAD), lambda h, i: (0, h))
    return pl.pallas_call(
        body, name="attn_bwd", grid=(N_HEADS, s // _ATTN_TQ),
        in_specs=[qspec, kspec, kspec, qspec], out_specs=[qspec, kspec, kspec],
        out_shape=[jax.ShapeDtypeStruct(q.shape, F32)] * 3,
        compiler_params=_cparams(("parallel", "arbitrary")),
    )(q, k, v, do)


def _loss_head(y, target):
    s, d = y.shape
    tm = 256

    def body(y_ref, t_ref, loss_ref, dy_ref):
        @pl.when(pl.program_id(0) == 0)
        def _():
            loss_ref[...] = jnp.zeros_like(loss_ref)

        err = y_ref[...] - t_ref[...]
        loss_ref[...] += 0.5 * jnp.sum(jnp.mean(err * err, axis=-1, keepdims=True), axis=0, keepdims=True)
        dy_ref[...] = err * (1.0 / d)

    return pl.pallas_call(
        body, name="loss_head", grid=(s // tm,),
        in_specs=[_row_spec(tm, d, 0), _row_spec(tm, d, 0)],
        out_specs=[_full_spec((1, 1)), _row_spec(tm, d, 0)],
        out_shape=[jax.ShapeDtypeStruct((1, 1), F32), jax.ShapeDtypeStruct((s, d), F32)],
        compiler_params=_cparams(("arbitrary",)),
    )(y, target)


def _adamw_math(w, g, m, v):
    mn = ADAM_B1 * m + (1.0 - ADAM_B1) * g
    vn = ADAM_B2 * v + (1.0 - ADAM_B2) * (g * g)
    m_hat = mn / (1.0 - ADAM_B1 ** ADAM_STEP)
    v_hat = vn / (1.0 - ADAM_B2 ** ADAM_STEP)
    return -ADAM_LR * (m_hat / (jnp.sqrt(v_hat) + ADAM_EPS) + ADAM_WD * w), mn, vn


def _adamw(name, w, g, m, v, layer, into):
    _, k, n = w.shape
    tk, tn = _slab_block(k, n)

    def body(w_ref, g_ref, m_ref, v_ref, *rest):
        d_ref, mo_ref, vo_ref = rest[-3:]
        d_ref[...], mo_ref[...], vo_ref[...] = _adamw_math(w_ref[...], g_ref[...], m_ref[...], v_ref[...])

    spec = pl.BlockSpec((1, tk, tn), lambda j, jn: (layer, j, jn))
    extra = [] if into is None else list(into)
    return pl.pallas_call(
        body, name=name, grid=(k // tk, n // tn), in_specs=[spec] * 4 + [_ANY] * len(extra), out_specs=[spec] * 3,
        out_shape=[jax.ShapeDtypeStruct(w.shape, F32)] * 3,
        input_output_aliases={4 + i: i for i in range(len(extra))},
        compiler_params=_cparams(("parallel", "parallel")),
    )(w, g, m, v, *extra)


_ANY = pl.BlockSpec(memory_space=pl.ANY)


def _mesh_pos():
    return lax.axis_index("x"), lax.axis_index("y"), lax.axis_index("c")


def _other_chips(x, y):
    return [(1 - x, y), (x, 1 - y), (1 - x, 1 - y)]


def _remote(src, dst, send_sem, recv_sem, to):
    return pltpu.make_async_remote_copy(src_ref=src, dst_ref=dst, send_sem=send_sem, recv_sem=recv_sem,
                                        device_id=to, device_id_type=_MESH)


_HBM = pl.BlockSpec(memory_space=pltpu.HBM)
_SEM = pl.BlockSpec(memory_space=pltpu.SEMAPHORE)
_EFFECT = pltpu.SideEffectType.DATAFLOW_SIDE_EFFECTING


def _in_hbm(a):
    return pltpu.with_memory_space_constraint(a, pltpu.HBM)


def _chip_index(chip):
    return 2 * chip[0] + chip[1]


def _ag_forward(name, shards, lands, layer, have_remote):
    n = len(shards)

    def body(*refs):
        ins = refs[:n]
        outs = refs[2 * n:3 * n] if lands is not None else refs[n:2 * n]
        send_sems, recv_sems = refs[-2:]
        x, y, c = _mesh_pos()
        sibling = (x, y, 1 - c)
        chips = _other_chips(x, y)

        def copy(a, k, src, dst, to):
            return _remote(src, dst, send_sems.at[a, k], recv_sems.at[a, k], to)

        own = [copy(a, 6, ins[a], outs[a].at[2 * x + y], sibling) for a in range(n)]
        for cp in own:
            cp.start()

        @pl.when(c == layer)
        def _():
            started = []
            if not have_remote:
                for k, chip in enumerate(chips):
                    for a in range(n):
                        cp = copy(a, k, ins[a], outs[a].at[2 * x + y], (*chip, c))
                        cp.start()
                        started.append(cp)
            for k, chip in enumerate(chips):
                for a in range(n):
                    landed = outs[a].at[_chip_index(chip)]
                    if not have_remote:
                        copy(a, k, ins[a], landed, (*chip, c)).wait_recv()
                    cp = copy(a, 3 + k, landed, landed, sibling)
                    cp.start()
                    started.append(cp)
            for cp in started:
                cp.wait_send()

        @pl.when(c != layer)
        def _():
            for k, chip in enumerate(chips):
                for a in range(n):
                    copy(a, 3 + k, ins[a], outs[a].at[_chip_index(chip)], sibling).wait_recv()

        for cp in own:
            cp.wait()

    out_shape = [jax.ShapeDtypeStruct((4,) + a.shape, a.dtype) for a in shards]
    extra = [] if lands is None else list(lands)
    return pl.pallas_call(
        body, name=name, in_specs=[_ANY] * (n + len(extra)), out_specs=[_ANY] * n,
        out_shape=out_shape, input_output_aliases={n + a: a for a in range(len(extra))},
        scratch_shapes=[pltpu.SemaphoreType.DMA((n, 7)), pltpu.SemaphoreType.DMA((n, 7))],
    )(*shards, *extra)


def _owner_sends(owners, srcs, dsts, send_sems, recv_sems, do):
    x, y, c = _mesh_pos()
    for core in (0, 1):
        mine = [a for a in range(len(srcs)) if owners[a] == core]
        if mine:
            @pl.when(c == core)
            def _(mine=mine):
                for k, chip in enumerate(_other_chips(x, y)):
                    for a in mine:
                        do(_remote(srcs[a](chip, k), dsts[a](chip, k), send_sems.at[3 * a + k],
                                   recv_sems.at[3 * a + k], (*chip, c)))


def _split_start(name, owners, sources, land_shapes, src_of, dst_of, after):
    n = len(sources)

    def body(*refs):
        srcs, lands = refs[:n], refs[n:2 * n]
        send_sems, recv_sems = refs[2 * n + 1], refs[2 * n + 2]
        token = refs[-1]
        _owner_sends(owners, [functools.partial(src_of, srcs[a]) for a in range(n)],
                     [functools.partial(dst_of, lands[a]) for a in range(n)], send_sems, recv_sems,
                     lambda cp: cp.start())
        token[...] = jnp.zeros_like(token)

    lands = [_in_hbm(lax.empty(s.shape, s.dtype)) for s in land_shapes]
    outs = pl.pallas_call(
        body, name=name,
        out_shape=([pltpu.SemaphoreType.DMA((3 * n,)), pltpu.SemaphoreType.DMA((3 * n,))]
                   + [pltpu.HBM(a.shape, a.dtype) for a in sources] + [pltpu.HBM(s.shape, s.dtype) for s in land_shapes]
                   + [jax.ShapeDtypeStruct((8, _LANES), F32)]),
        in_specs=[_HBM] * (2 * n) + [_ANY],
        out_specs=[_SEM, _SEM] + [_HBM] * (2 * n) + [pl.BlockSpec(memory_space=pltpu.VMEM)],
        input_output_aliases={i: 2 + i for i in range(2 * n)},
        compiler_params=pltpu.CompilerParams(has_side_effects=_EFFECT),
    )(*[_in_hbm(a) for a in sources], *lands, after)
    return outs[0], outs[1], outs[2:2 + n], outs[2 + n:2 + 2 * n], outs[-1]


def _split_wait(name, owners, send_sems, recv_sems, sources, lands, after, src_of, dst_of):
    n = len(sources)

    def body(*refs):
        srcs, lnds = refs[:n], refs[n:2 * n]
        s_sems, r_sems = refs[2 * n], refs[2 * n + 1]

        def wait(cp):
            cp.wait_send()
            cp.wait_recv()

        _owner_sends(owners, [functools.partial(src_of, srcs[a]) for a in range(n)],
                     [functools.partial(dst_of, lnds[a]) for a in range(n)], s_sems, r_sems, wait)

    outs = pl.pallas_call(
        body, name=name,
        out_shape=[pltpu.HBM(a.shape, a.dtype) for a in sources] + [pltpu.HBM(a.shape, a.dtype) for a in lands],
        in_specs=[_HBM] * (2 * n) + [_SEM, _SEM] + [_ANY] * len(after), out_specs=[_HBM] * (2 * n),
        input_output_aliases={i: i for i in range(2 * n)},
        compiler_params=pltpu.CompilerParams(has_side_effects=_EFFECT),
    )(*sources, *lands, send_sems, recv_sems, *after)
    return outs[:n], outs[n:]


def _pair_exchange(name, arrays, owners, to_owner, layer=None):
    n = len(arrays)

    def body(*refs):
        ins, outs, (send_sems, recv_sems) = refs[:n], refs[n:2 * n], refs[2 * n:]
        x, y, c = _mesh_pos()
        part = (lambda r: r) if layer is None else (lambda r: r.at[layer])
        copies = [_remote(part(ins[a]), part(outs[a]), send_sems.at[a], recv_sems.at[a], (x, y, 1 - c))
                  for a in range(n)]
        for core in (0, 1):
            sends = [copies[a] for a in range(n) if (owners[a] != core) == to_owner]
            recvs = [copies[a] for a in range(n) if (owners[a] == core) == to_owner]

            @pl.when(c == core)
            def _(sends=sends, recvs=recvs):
                for cp in sends:
                    cp.start()
                for cp in recvs:
                    cp.wait_recv()
                for cp in sends:
                    cp.wait_send()

    return pl.pallas_call(
        body, name=name, in_specs=[_ANY] * n, out_specs=[_ANY] * n,
        out_shape=[jax.ShapeDtypeStruct(g.shape, g.dtype) for g in arrays],
        input_output_aliases={} if to_owner else {a: a for a in range(n)},
        scratch_shapes=[pltpu.SemaphoreType.DMA((n,)), pltpu.SemaphoreType.DMA((n,))],
    )(*arrays)


def _swap_split(name, arrays, lands, owners, sems, after):
    n = len(arrays)
    starting = sems is None

    def body(*refs):
        ins, lnds = refs[:n], refs[n:2 * n]
        send_sems, recv_sems = (refs[2 * n + 1], refs[2 * n + 2]) if starting else (refs[2 * n], refs[2 * n + 1])
        x, y, c = _mesh_pos()
        copies = [_remote(ins[a], lnds[a], send_sems.at[a], recv_sems.at[a], (x, y, 1 - c)) for a in range(n)]
        for core in (0, 1):
            sends = [copies[a] for a in range(n) if owners[a] != core]
            recvs = [copies[a] for a in range(n) if owners[a] == core]

            @pl.when(c == core)
            def _(sends=sends, recvs=recvs):
                if starting:
                    for cp in sends:
                        cp.start()
                else:
                    for cp in recvs:
                        cp.wait_recv()
                    for cp in sends:
                        cp.wait_send()

        if starting:
            refs[-1][...] = jnp.zeros_like(refs[-1])

    hbm = lambda arrs: [pltpu.HBM(a.shape, a.dtype) for a in arrs]
    if starting:
        zones = [_in_hbm(lax.empty(s.shape, s.dtype)) for s in lands]
        outs = pl.pallas_call(
            body, name=name,
            out_shape=([pltpu.SemaphoreType.DMA((n,)), pltpu.SemaphoreType.DMA((n,))] + hbm(arrays) + hbm(lands)
                       + [jax.ShapeDtypeStruct((8, _LANES), F32)]),
            in_specs=[_HBM] * (2 * n) + [_ANY],
            out_specs=[_SEM, _SEM] + [_HBM] * (2 * n) + [pl.BlockSpec(memory_space=pltpu.VMEM)],
            input_output_aliases={i: 2 + i for i in range(2 * n)},
            compiler_params=pltpu.CompilerParams(has_side_effects=_EFFECT),
        )(*[_in_hbm(a) for a in arrays], *zones, after)
        return outs[0], outs[1], outs[2:2 + n], outs[2 + n:2 + 2 * n], outs[-1]
    outs = pl.pallas_call(
        body, name=name, out_shape=hbm(arrays) + hbm(lands),
        in_specs=[_HBM] * (2 * n) + [_SEM, _SEM] + [_ANY] * len(after), out_specs=[_HBM] * (2 * n),
        input_output_aliases={i: i for i in range(2 * n)},
        compiler_params=pltpu.CompilerParams(has_side_effects=_EFFECT),
    )(*arrays, *lands, *sems, *after)
    return outs[:n], outs[n:]


def _add_pair(name, g, a, flags):
    _, k, n = g.shape
    tk, tn = _slab_block(k, n)

    def body(flags_ref, g_ref, a_ref, o_ref):
        o_ref[...] = (g_ref[...] + a_ref[...]).astype(o_ref.dtype)

    spec = pl.BlockSpec((1, tk, tn), lambda j, i, jn, fl: (j * fl[1], i * fl[1], jn * fl[1]))
    return pl.pallas_call(
        body, name=name,
        grid_spec=pltpu.PrefetchScalarGridSpec(num_scalar_prefetch=1, grid=(4, k // tk, n // tn),
                                               in_specs=[spec, spec], out_specs=spec),
        out_shape=jax.ShapeDtypeStruct(g.shape, _RS_DT),
        compiler_params=_cparams(("arbitrary", "arbitrary", "arbitrary")),
    )(flags, g, a)


def _add_quads(name, t, b, layer, flags, into, after=None):
    _, k, n = t.shape
    tk, tn = _slab_block(k, n)

    def body(flags_ref, t_ref, b_ref, *rest):
        o_ref = rest[-1]
        f = lambda v: v.astype(F32)
        o_ref[0] = ((f(t_ref[0]) + f(b_ref[0])) + f(b_ref[1])) + f(b_ref[2])

    extra = ([] if into is None else [into]) + ([] if after is None else [after])
    return pl.pallas_call(
        body, name=name,
        grid_spec=pltpu.PrefetchScalarGridSpec(
            num_scalar_prefetch=1, grid=(k // tk, n // tn),
            in_specs=[pl.BlockSpec((1, tk, tn), lambda i, jn, fl: (fl[0], i * fl[1], jn * fl[1])),
                      pl.BlockSpec((3, tk, tn), lambda i, jn, fl: (0, i * fl[1], jn * fl[1]))] + [_ANY] * len(extra),
            out_specs=pl.BlockSpec((1, tk, tn), lambda i, jn, fl: (layer, i * fl[1], jn * fl[1]))),
        out_shape=jax.ShapeDtypeStruct((2, k, n), F32),
        input_output_aliases={} if into is None else {3: 0},
        compiler_params=_cparams(("arbitrary", "arbitrary")),
    )(flags, t, b, *extra)


def _slab_block(k, n, itemsize=4):
    tk = (1 << 20) // (n * itemsize) // 16 * 16
    while 0 < tk < k and k % tk:
        tk -= 16
    if 0 < tk < k:
        return tk, n
    if k * n * itemsize <= (2 << 20) or n % _LANES:
        return k, n
    tn = max(_LANES, (2 << 20) // (k * itemsize) // _LANES * _LANES)
    while n % tn:
        tn -= _LANES
    return k, tn


def _all_reduce_adamw(gs, ws, ms, vs):
    n = len(gs)

    def body(*refs):
        g_refs, w_refs, m_refs, v_refs = (refs[i * n:(i + 1) * n] for i in range(4))
        gsum, delta, m_out, v_out = (refs[(4 + i) * n:(5 + i) * n] for i in range(4))
        pair, chips = refs[8 * n:9 * n], refs[9 * n:10 * n]
        send_sems, recv_sems = refs[10 * n:]
        x, y, c = _mesh_pos()
        my_chip = 2 * x + y
        to_sibling = [_remote(g_refs[a], pair[a].at[c], send_sems.at[a, 0], recv_sems.at[a, 0], (x, y, 1 - c))
                      for a in range(n)]
        for cp in to_sibling:
            cp.start()
        for a in range(n):
            pair[a][c] = g_refs[a][...]
        copies = []
        for a in range(n):
            to_sibling[a].wait()
            chips[a][my_chip] = pair[a][0] + pair[a][1]
            for k, chip in enumerate(_other_chips(x, y)):
                mine = chips[a].at[my_chip]
                cp = _remote(mine, mine, send_sems.at[a, 1 + k], recv_sems.at[a, 1 + k], (*chip, c))
                cp.start()
                copies.append(cp)
        for cp in copies:
            cp.wait()
        for a in range(n):
            acc = ((chips[a][0] + chips[a][1]) + chips[a][2]) + chips[a][3]
            gsum[a][...] = acc
            delta[a][...], m_out[a][...], v_out[a][...] = _adamw_math(w_refs[a][...], acc, m_refs[a][...],
                                                                      v_refs[a][...])

    vmem = pl.BlockSpec(memory_space=pltpu.VMEM)
    outs = pl.pallas_call(
        body, name="all_reduce_adamw", in_specs=[vmem] * (4 * n), out_specs=[vmem] * (4 * n),
        out_shape=[jax.ShapeDtypeStruct(g.shape, F32) for g in gs] * 4,
        scratch_shapes=([pltpu.VMEM((2,) + g.shape, F32) for g in gs] + [pltpu.VMEM((4,) + g.shape, F32) for g in gs]
                        + [pltpu.SemaphoreType.DMA((n, 4)), pltpu.SemaphoreType.DMA((n, 4))]),
        compiler_params=pltpu.CompilerParams(vmem_limit_bytes=_VMEM_LIMIT),
    )(*gs, *ws, *ms, *vs)
    return outs[:n], outs[n:2 * n], outs[2 * n:3 * n], outs[3 * n:]


def _swap_rope(a):
    h = QK_ROPE // 2
    return jnp.concatenate([a[..., h:], a[..., :h]], axis=-1)


def _swap_rope_rows(a):
    h = QK_ROPE // 2
    return jnp.concatenate([a[h:], a[:h]], axis=0)


_FFN_CB = 256


def _interleave_rows(a, cb):
    r, c = a.shape
    return a.reshape(2, r // (2 * cb), cb, c).transpose(1, 0, 2, 3).reshape(r, c)


def _deinterleave_rows(a, cb):
    r, c = a.shape
    return a.reshape(r // (2 * cb), 2, cb, c).transpose(1, 0, 2, 3).reshape(r, c)


class _InLayout:
    def __init__(self, d):
        self.d = d
        self.gates = 0
        self.a = 3 * d
        self.b = 4 * d
        self.kv = 5 * d
        self.q = self.kv + 256
        self.krm = self.q + 384
        self.krs = self.krm + HEAD_PAD
        self.width = self.krs + 2 * HEAD_PAD


def _prep_layer(wl, d):
    lay = _InLayout(d)
    w_in = wl["w_in"]
    dt = w_in.dtype
    a, b = w_in[0:d], w_in[d:2 * d]
    q, kv = w_in[2 * d:2 * d + 384], w_in[2 * d + 384:2 * d + 640]
    kr = w_in[2 * d + 640:2 * d + 640 + QK_ROPE]
    gates = w_in[2 * d + 640 + QK_ROPE:]
    z = lambda n: jnp.zeros((n, d), dt)
    krm = jnp.concatenate([z(QK_NOPE), kr, z(HEAD_PAD - QK_NOPE - QK_ROPE)], axis=0)
    krs = jnp.concatenate([z(QK_NOPE), _swap_rope_rows(kr), z(HEAD_PAD - QK_NOPE - QK_ROPE)], axis=0)
    out = dict(wl)
    out["w_in"] = jnp.concatenate([gates, _interleave_rows(a, _LANES), b, kv, q, krm, krs,
                                   z(lay.width - lay.krs - HEAD_PAD)], axis=0)
    uq = wl["mla_w_uq"].reshape(-1, N_HEADS, QK_NOPE + QK_ROPE)
    nq = uq.shape[0]
    nope, pe = uq[..., :QK_NOPE], uq[..., QK_NOPE:]
    zq = lambda n: jnp.zeros((nq, N_HEADS, n), dt)
    main = jnp.concatenate([nope, pe, zq(HEAD_PAD - QK_NOPE - QK_ROPE)], axis=-1).reshape(nq, -1)
    swapped = jnp.concatenate([zq(QK_NOPE), _swap_rope(pe), zq(HEAD_PAD - QK_NOPE - QK_ROPE)], axis=-1).reshape(nq, -1)
    out["mla_w_uq"] = jnp.concatenate([main, swapped], axis=1)
    ukv = wl["mla_w_ukv"].reshape(-1, N_HEADS, QK_NOPE + V_HEAD)
    nkv = ukv.shape[0]
    zk = jnp.zeros((nkv, N_HEADS, HEAD_PAD - QK_NOPE), dt)
    zv = jnp.zeros((nkv, N_HEADS, HEAD_PAD - V_HEAD), dt)
    out["mla_w_ukv"] = jnp.concatenate([jnp.concatenate([ukv[..., :QK_NOPE], zk], axis=-1).reshape(nkv, -1),
                                        jnp.concatenate([ukv[..., QK_NOPE:], zv], axis=-1).reshape(nkv, -1)], axis=1)
    wo = wl["mla_w_o"].reshape(N_HEADS, V_HEAD, -1)
    out["mla_w_o"] = jnp.concatenate([wo, jnp.zeros((N_HEADS, HEAD_PAD - V_HEAD, wo.shape[-1]), dt)],
                                     axis=1).reshape(N_HEADS * HEAD_PAD, -1)
    return out


def _unprep_grads(g, d):
    lay = _InLayout(d)
    gi = g["w_in"]
    kr = (gi[lay.krm + QK_NOPE:lay.krm + QK_NOPE + QK_ROPE]
          + _swap_rope_rows(gi[lay.krs + QK_NOPE:lay.krs + QK_NOPE + QK_ROPE]))
    out = dict(g)
    out["w_in"] = jnp.concatenate([_deinterleave_rows(gi[lay.a:lay.a + d], _LANES), gi[lay.b:lay.b + d],
                                   gi[lay.q:lay.q + 384], gi[lay.kv:lay.kv + 256], kr,
                                   gi[lay.gates:lay.gates + 3 * d]], axis=0)
    hw = N_HEADS * HEAD_PAD
    gq = g["mla_w_uq"]
    nq = gq.shape[0]
    main = gq[:, :hw].reshape(nq, N_HEADS, HEAD_PAD)
    swapped = gq[:, hw:].reshape(nq, N_HEADS, HEAD_PAD)
    pe = main[..., QK_NOPE:QK_NOPE + QK_ROPE] + _swap_rope(swapped[..., QK_NOPE:QK_NOPE + QK_ROPE])
    out["mla_w_uq"] = jnp.concatenate([main[..., :QK_NOPE], pe], axis=-1).reshape(nq, -1)
    gkv = g["mla_w_ukv"]
    nkv = gkv.shape[0]
    out["mla_w_ukv"] = jnp.concatenate([gkv[:, :hw].reshape(nkv, N_HEADS, HEAD_PAD)[..., :QK_NOPE],
                                        gkv[:, hw:].reshape(nkv, N_HEADS, HEAD_PAD)[..., :V_HEAD]],
                                       axis=-1).reshape(nkv, -1)
    go = g["mla_w_o"]
    out["mla_w_o"] = go.reshape(N_HEADS, HEAD_PAD, -1)[:, :V_HEAD].reshape(N_HEADS * V_HEAD, -1)
    return out


def _rope_tables(positions):
    s = positions.shape[0]
    inv = ROPE_THETA ** (-jnp.arange(0, QK_ROPE, 2, dtype=F32) / QK_ROPE)
    ang = positions.astype(F32)[:, None] * inv
    cos, sin = jnp.cos(ang), jnp.sin(ang)
    tail = jnp.zeros((s, HEAD_PAD - QK_NOPE - QK_ROPE), F32)
    tc = jnp.concatenate([jnp.ones((s, QK_NOPE), F32), cos, cos, tail], axis=1)
    ts = jnp.concatenate([jnp.zeros((s, QK_NOPE), F32), -sin, sin, tail], axis=1)
    return tc, ts


def _row(v):
    return v.reshape(1, -1)


def _layer_fwd(x, h, w, g_next, tc, ts, late_weights=None):
    d = x.shape[1]
    lay = _InLayout(d)
    cw = d // 2
    blk = lambda off, width: off // width
    p = _mm("mm_in", h, w["w_in"], tb=True)
    z1 = _glu_conv_fwd(p, blk(lay.a, 2 * _LANES), w["conv_dw_w"], _row(w["conv_dw_b"]))
    ln_a = [_row(w["conv_ln_g"]), _row(w["conv_ln_b"])]
    (z3,) = _row_fwd("ln_silu_fwd", _f_ln_silu, [(z1, cw, 0)], ln_a, [(cw, _MXU_DT)])
    ya = _mm("mm_conv_out", z3, w["conv_out_w"])
    ln_b = [_row(w["sg_ln_g"]), _row(w["sg_ln_b"])]
    u, vn = _row_fwd("sg_pre_fwd", _f_sg_pre, [(p, cw, blk(lay.b, cw)), (p, cw, blk(lay.b + cw, cw))], ln_b,
                     [(cw, F32), (cw, _MXU_DT)])
    bcol = w["sg_b"].reshape(SG_GROUPS, SG_CHUNK, 1)
    ub = _sg_mix_fwd(u, vn, w["sg_w"], bcol)
    yb = _mm("mm_sg_out", ub, w["sg_out_w"])
    (qn,) = _row_fwd("q_norm_fwd", _f_rms, [(p, 384, blk(lay.q, 384))], [_row(w["mla_q_norm_g"])], [(384, _MXU_DT)])
    (kvn,) = _row_fwd("kv_norm_fwd", _f_rms, [(p, 256, blk(lay.kv, 256))], [_row(w["mla_kv_norm_g"])],
                      [(256, _MXU_DT)])
    q2 = _mm("mm_uq", qn, w["mla_w_uq"])
    kv2 = _mm("mm_ukv", kvn, w["mla_w_ukv"])
    qf, kf, vf = _rope_fwd(q2, kv2, p, blk(lay.krm, HEAD_PAD), blk(lay.krs, HEAD_PAD), tc, ts)
    o = _attn_fwd(qf, kf, vf)
    yc = _mm("mm_o", o, w["mla_w_o"])
    gate_rows = [(p, d, 0), (p, d, 1), (p, d, 2)]
    (merged,) = _row_fwd("merge_fwd", _f_merge, gate_rows + [(ya, d, 0), (yb, d, 0), (yc, d, 0)], [], [(d, _MXU_DT)])
    if late_weights is not None:
        w = {**w, **late_weights(merged)}
    t = _mm("mm_out", merged, w["w_out"])
    x1, h2 = _row_fwd("resid_mix_fwd", _f_resid_rms_rms, [(x, d, 0), (t, d, 0)],
                      [_row(w["mix_post_g"]), _row(w["ffn_pre_g"])], [(d, F32), (d, _MXU_DT)])
    up = _mm("mm_up", h2, w["ffn_w_up"])
    act = _conv_geglu_fwd(up, w["ffn_dw_w"], _row(w["ffn_dw_b"]))
    dn = _mm("mm_down", act, w["ffn_w_down"])
    if g_next is None:
        (x2,) = _row_fwd("resid_ffn_last_fwd", _f_resid_rms, [(x1, d, 0), (dn, d, 0)], [_row(w["ffn_post_g"])],
                         [(d, F32)])
        h_next = None
    else:
        x2, h_next = _row_fwd("resid_ffn_fwd", _f_resid_rms_rms, [(x1, d, 0), (dn, d, 0)],
                              [_row(w["ffn_post_g"]), _row(g_next)], [(d, F32), (d, _MXU_DT)])
    saved = dict(x=x, h=h, p=p, z1=z1, z3=z3, ya=ya, u=u, vn=vn, ub=ub, yb=yb, qn=qn, kvn=kvn, qf=qf, kf=kf, vf=vf, o=o,
                 yc=yc, merged=merged, t=t, x1=x1, h2=h2, up=up, act=act, dn=dn, bcol=bcol)
    return x2, h_next, saved


def _layer_bwd(dx2, dh_next, w, g_next, sv, tc, ts, on_late_grads=None, on_attention_done=None):
    d = dx2.shape[1]
    lay = _InLayout(d)
    cw = d // 2
    blk = lambda off, width: off // width
    lo = _MXU_DT
    g = {}
    x1, dn = sv["x1"], sv["dn"]
    if dh_next is None:
        dx1, ddn, g["ffn_post_g"] = _row_bwd("resid_ffn_last_bwd", _f_resid_rms, [(x1, d, 0), (dn, d, 0)],
                                             [_row(w["ffn_post_g"])], [(dx2, d, 0)], [(0, F32), (1, lo)], [0])
    else:
        dx1, ddn, g["ffn_post_g"], g["next_pre_g"] = _row_bwd(
            "resid_ffn_bwd", _f_resid_rms_rms, [(x1, d, 0), (dn, d, 0)], [_row(w["ffn_post_g"]), _row(g_next)],
            [(dx2, d, 0), (dh_next, d, 0)], [(0, F32), (1, lo)], [0, 1])
    dact = _mm("mm_down_dx", ddn, w["ffn_w_down"], tb=True)
    g["ffn_w_down"] = _mm("mm_down_dw", sv["act"], ddn, ta=True)
    dup, dw_halves, db_halves = _conv_geglu_bwd(sv["up"], w["ffn_dw_w"], _row(w["ffn_dw_b"]), dact)
    g["ffn_dw_w"] = jnp.concatenate([dw_halves[0], dw_halves[1]], axis=1)
    g["ffn_dw_b"] = jnp.concatenate([db_halves[0], db_halves[1]], axis=1)
    dh2 = _mm("mm_up_dx", dup, w["ffn_w_up"], tb=True, a_halves=True)
    g["ffn_w_up"] = _mm("mm_up_dw", sv["h2"], dup, ta=True, b_halves=True, out_quarters=True)
    dx, dt, g["mix_post_g"], g["ffn_pre_g"] = _row_bwd(
        "resid_mix_bwd", _f_resid_rms_rms, [(sv["x"], d, 0), (sv["t"], d, 0)],
        [_row(w["mix_post_g"]), _row(w["ffn_pre_g"])], [(dx1, d, 0), (dh2, d, 0)], [(0, F32), (1, lo)], [0, 1])
    dmerged = _mm("mm_out_dx", dt, w["w_out"], tb=True)
    g["w_out"] = _mm("mm_out_dw", sv["merged"], dt, ta=True)
    if on_late_grads is not None:
        dmerged = on_late_grads(g, dmerged)
    p = sv["p"]
    gate_rows = [(p, d, 0), (p, d, 1), (p, d, 2)]
    dp, dya, dyb, dyc = _row_bwd(
        "merge_bwd", _f_merge, gate_rows + [(sv["ya"], d, 0), (sv["yb"], d, 0), (sv["yc"], d, 0)], [],
        [(dmerged, d, 0)], [((0, 1, 2), lo), ((3,), lo), ((4,), lo), ((5,), lo)], [], place=(lay.width, 0))
    do = _mm("mm_o_dx", dyc, w["mla_w_o"], tb=True, out_dtype=lo)
    g["mla_w_o"] = _mm("mm_o_dw", sv["o"], dyc, ta=True)
    dqf, dkf, dvf = _attn_bwd(sv["qf"], sv["kf"], sv["vf"], do)
    dq2, dkv2, dp = _rope_bwd(dqf, dkf, dvf, tc, ts, dp, blk(lay.krm, 3 * HEAD_PAD))
    dkvn = _mm("mm_ukv_dx", dkv2, w["mla_w_ukv"], tb=True)
    g["mla_w_ukv"] = _mm("mm_ukv_dw", sv["kvn"], dkv2, ta=True)
    dqn = _mm("mm_uq_dx", dq2, w["mla_w_uq"], tb=True)
    g["mla_w_uq"] = _mm("mm_uq_dw", sv["qn"], dq2, ta=True)
    dp, g["mla_q_norm_g"] = _row_bwd("q_norm_bwd", _f_rms, [(p, 384, blk(lay.q, 384))], [_row(w["mla_q_norm_g"])],
                                     [(dqn, 384, 0)], [((0,), lo)], [0], place=(lay.width, blk(lay.q, 384)), into=dp)
    dp, g["mla_kv_norm_g"] = _row_bwd("kv_norm_bwd", _f_rms, [(p, 256, blk(lay.kv, 256))],
                                      [_row(w["mla_kv_norm_g"])], [(dkvn, 256, 0)], [((0,), lo)], [0],
                                      place=(lay.width, blk(lay.kv, 256)), into=dp)
    if on_attention_done is not None:
        dyb = on_attention_done(dyb)
    dub = _mm("mm_sg_out_dx", dyb, w["sg_out_w"], tb=True)
    g["sg_out_w"] = _mm("mm_sg_out_dw", sv["ub"], dyb, ta=True)
    du, dvn, g["sg_w"], dbcol = _sg_mix_bwd(sv["u"], sv["vn"], w["sg_w"], sv["bcol"], dub)
    g["sg_b"] = dbcol.reshape(SG_GROUPS, SG_CHUNK)
    dp, g["sg_ln_g"], g["sg_ln_b"] = _row_bwd(
        "sg_pre_bwd", _f_sg_pre, [(p, cw, blk(lay.b, cw)), (p, cw, blk(lay.b + cw, cw))],
        [_row(w["sg_ln_g"]), _row(w["sg_ln_b"])], [(du, cw, 0), (dvn, cw, 0)], [((0, 1), lo)], [0, 1],
        place=(lay.width, blk(lay.b, d)), into=dp)
    dz3 = _mm("mm_conv_out_dx", dya, w["conv_out_w"], tb=True)
    g["conv_out_w"] = _mm("mm_conv_out_dw", sv["z3"], dya, ta=True)
    dz1, g["conv_ln_g"], g["conv_ln_b"] = _row_bwd(
        "ln_silu_bwd", _f_ln_silu, [(sv["z1"], cw, 0)], [_row(w["conv_ln_g"]), _row(w["conv_ln_b"])],
        [(dz3, cw, 0)], [((0,), F32)], [0, 1])
    dp, g["conv_dw_w"], g["conv_dw_b"] = _glu_conv_bwd(p, blk(lay.a, 2 * _LANES), w["conv_dw_w"], dz1, dp)
    dh = _mm("mm_in_dx", dp, w["w_in"])
    g["w_in"] = _mm("mm_in_dw", dp, sv["h"], ta=True)
    return dx, dh, g


def _local_step(x, positions, target, layers):
    d = x.shape[1]
    tc, ts = _rope_tables(positions)
    ws = [_prep_layer(wl, d) for wl in layers]
    depth = len(ws)
    (h,) = _row_fwd("rms_first_fwd", _f_rms, [(x, d, 0)], [_row(ws[0]["mix_pre_g"])], [(d, _MXU_DT)])
    saved = []
    for l in range(depth):
        g_next = ws[l + 1]["mix_pre_g"] if l + 1 < depth else None
        x, h, sv = _layer_fwd(x, h, ws[l], g_next, tc, ts)
        saved.append(sv)
    loss, dx = _loss_head(x, target)
    grads = [None] * depth
    dh = None
    for l in reversed(range(depth)):
        g_next = ws[l + 1]["mix_pre_g"] if l + 1 < depth else None
        dx, dh, g = _layer_bwd(dx, dh, ws[l], g_next, saved[l], tc, ts)
        if "next_pre_g" in g:
            grads[l + 1]["mix_pre_g"] = g.pop("next_pre_g")
        grads[l] = g
    x0 = saved[0]["x"]
    grad_x, grads[0]["mix_pre_g"] = _row_bwd("rms_first_bwd", _f_x_rms, [(x0, d, 0)], [_row(ws[0]["mix_pre_g"])],
                                             [(dx, d, 0), (dh, d, 0)], [(0, F32)], [0])
    return loss, grad_x, [_unprep_grads(g, d) for g in grads]


_MATRICES = ("w_in", "conv_out_w", "sg_out_w", "mla_w_uq", "mla_w_ukv", "mla_w_o", "w_out", "ffn_w_up", "ffn_w_down")
_F32_GATHERED = ("conv_dw_w", "ffn_dw_w")
_RS_DT = jnp.bfloat16


_ROW_SHARDED = SHARDED_MID + ("w_in",)


_GATHERED = SHARDED + _F32_GATHERED
_RS_CORE0 = ("w_in", "ffn_w_down")
_LATE_WEIGHTS = ("w_out", "ffn_w_up", "ffn_dw_w", "ffn_w_down")


def _layer_shards(w, l):
    hi = {n: w[n][l].astype(jnp.bfloat16) for n in SHARDED}
    lo = [(w[n][l] - hi[n].astype(F32)).astype(jnp.bfloat16) for n in _F32_GATHERED]
    return [hi[n] for n in SHARDED] + lo


def _layer_weights(names, gathered):
    wl = {}
    for n, g in zip(names, gathered):
        if n in _ROW_SHARDED and g.shape[1] % 16 == 0:
            whole = g.reshape(-1, g.shape[2])
        else:
            whole = jnp.concatenate([g[j] for j in range(4)], axis=0 if n in _ROW_SHARDED else 1)
        if n in wl:
            wl[n] = wl[n].astype(F32) + whole.astype(F32)
        else:
            wl[n] = whole.astype(_MXU_DT) if n in _MATRICES else whole
    return wl


def _by_destination(name, gl):
    if gl.ndim == 3:
        return gl
    k, n = gl.shape
    if name in _ROW_SHARDED:
        return gl.reshape(4, k // 4, n)
    return gl.reshape(k, 4, n // 4).transpose(1, 0, 2)


def kernel(x, positions, mix_pre_g, mix_post_g, ffn_pre_g, ffn_post_g, w_in, conv_dw_w, conv_dw_b, conv_ln_g, conv_ln_b, conv_out_w, sg_ln_g, sg_ln_b, sg_w, sg_b, sg_out_w, mla_q_norm_g, mla_w_uq, mla_kv_norm_g, mla_w_ukv, mla_w_o, w_out, ffn_w_up, ffn_dw_w, ffn_dw_b, ffn_w_down, loss_target, m_mix_pre_g, m_mix_post_g, m_ffn_pre_g, m_ffn_post_g, m_w_in, m_conv_dw_w, m_conv_dw_b, m_conv_ln_g, m_conv_ln_b, m_conv_out_w, m_sg_ln_g, m_sg_ln_b, m_sg_w, m_sg_b, m_sg_out_w, m_mla_q_norm_g, m_mla_w_uq, m_mla_kv_norm_g, m_mla_w_ukv, m_mla_w_o, m_w_out, m_ffn_w_up, m_ffn_dw_w, m_ffn_dw_b, m_ffn_w_down, v_mix_pre_g, v_mix_post_g, v_ffn_pre_g, v_ffn_post_g, v_w_in, v_conv_dw_w, v_conv_dw_b, v_conv_ln_g, v_conv_ln_b, v_conv_out_w, v_sg_ln_g, v_sg_ln_b, v_sg_w, v_sg_b, v_sg_out_w, v_mla_q_norm_g, v_mla_w_uq, v_mla_kv_norm_g, v_mla_w_ukv, v_mla_w_o, v_w_out, v_ffn_w_up, v_ffn_dw_w, v_ffn_dw_b, v_ffn_w_down):
    args = dict(locals())
    w = {n: args[n] for n in WEIGHTS}
    m = {n: args["m_" + n] for n in WEIGHTS}
    v = {n: args["v_" + n] for n in WEIGHTS}
    depth = mix_pre_g.shape[0]

    assert depth == 2, "the two cores of a chip split the communication by layer"
    for t in (w, m, v):
        t["w_in"] = jnp.swapaxes(t["w_in"], 1, 2)
    d = x.shape[-1]
    mesh_x, mesh_y, mesh_c = _mesh_pos()
    my_chip = 2 * mesh_x + mesh_y
    names = list(SHARDED)
    whole = lambda ref, chip, k: ref
    to_my_slot = lambda ref, chip, k: ref.at[2 * lax.axis_index("x") + lax.axis_index("y")]
    block_of_chip = lambda ref, chip, k: ref.at[_chip_index(chip)]
    slot_k = lambda ref, chip, k: ref.at[k]

    late = [i for i, n in enumerate(_GATHERED) if n in _LATE_WEIGHTS]
    early = [i for i in range(len(_GATHERED)) if i not in late]
    pick = lambda seq, idx: [seq[i] for i in idx]
    gathered_names = list(_GATHERED)
    shards0, shards1 = _layer_shards(w, 0), _layer_shards(w, 1)
    replicated = lambda l: {n: w[n][l] for n in REPLICATED}
    land_of = lambda shards: [jax.ShapeDtypeStruct((4,) + a.shape, a.dtype) for a in shards]

    gathered0 = _ag_forward("ag_layer0_early", pick(shards0, early), None, 0, False)
    late0 = pick(shards0, late)
    sems0_s, sems0_r, late0, lands0, token0 = _split_start("ag0_start", [0] * len(late), late0, land_of(late0), whole,
                                                           to_my_slot, gathered0[0])
    ag_owner = [1] * len(shards1)
    sems_s, sems_r, shards1, lands1, token = _split_start("ag1_start", ag_owner, shards1, land_of(shards1), whole,
                                                          to_my_slot, gathered0[0])
    tc, ts = _rope_tables(positions[0])
    ws0 = _prep_layer({**replicated(0), **_layer_weights(pick(gathered_names, early), gathered0)}, d)

    def late_weights(merged):
        got = _split_wait("ag0_wait", [0] * len(late), sems0_s, sems0_r, late0, lands0, [merged], whole, to_my_slot)
        got = _ag_forward("ag_layer0_late", got[0], got[1], 0, True)
        ws0.update(_layer_weights(pick(gathered_names, late), got))
        return ws0

    x0 = x[0] + (token0[0, 0] + token[0, 0])
    (h0,) = _row_fwd("rms_first_fwd", _f_rms, [(x0, d, 0)], [_row(ws0["mix_pre_g"])], [(d, _MXU_DT)])
    x1, h1, sv0 = _layer_fwd(x0, h0, ws0, w["mix_pre_g"][1], tc, ts, late_weights)
    shards1, lands1 = _split_wait("ag1_wait", ag_owner, sems_s, sems_r, shards1, lands1, [x1], whole, to_my_slot)
    gathered1 = _ag_forward("ag_layer1", shards1, lands1, 1, True)
    ws1 = _prep_layer({**replicated(1), **_layer_weights(gathered_names, gathered1)}, d)
    x2, _, sv1 = _layer_fwd(x1, h1, ws1, None, tc, ts)
    loss, dx = _loss_head(x2, loss_target[0])
    loss = lax.psum(loss[0, 0], ("x", "y", "c"))

    owner = {n: 0 if n in _RS_CORE0 else 1 for n in names}
    flag = {n: jnp.stack([my_chip, (mesh_c == owner[n]).astype(jnp.int32)]).astype(jnp.int32) for n in names}
    owners = [owner[n] for n in names]

    def scatter_start(tag, group, gd, got, after):
        own = [owner[n] for n in group]
        t = [_add_pair("rs_pair%s_%s" % (tag, n), g, a, flag[n]) for n, g, a in zip(group, gd, got)]
        lands = [jax.ShapeDtypeStruct((3,) + a.shape[1:], a.dtype) for a in t]
        sems_s, sems_r, t, b, token = _split_start("rs%s_start" % tag, own, t, lands, block_of_chip, slot_k, after)
        return (tag, group, own, sems_s, sems_r, t, b), token

    def scatter_wait(handle, after):
        tag, group, own, sems_s, sems_r, t, b = handle
        t, b = _split_wait("rs%s_wait" % tag, own, sems_s, sems_r, t, b, after, block_of_chip, slot_k)
        return dict(zip(group, zip(t, b)))

    def swap_and_start(tag, group, grads_l, after):
        gd = [_by_destination(n, grads_l[n]) for n in group]
        got = _pair_exchange("rs_swap" + tag, gd, [owner[n] for n in group], True)
        return scatter_start(tag, group, gd, got, after)

    dx, dh, gk1 = _layer_bwd(dx, None, ws1, None, sv1, tc, ts)
    grads1 = _unprep_grads(gk1, d)
    gd1 = [_by_destination(n, grads1[n]) for n in names]
    swap1 = _swap_split("rs_swap1_start", gd1, [jax.ShapeDtypeStruct(g.shape, g.dtype) for g in gd1], owners, None,
                        dh)
    dx = dx + swap1[4][0, 0]
    late_group = [n for n in names if n in _LATE_WEIGHTS]
    early_group = [n for n in names if n not in _LATE_WEIGHTS]
    handles = []

    own_late = [owner[n] for n in late_group]
    pending = []

    def on_late_grads(g, value):
        gd, got = _swap_split("rs_swap1_wait", swap1[2], swap1[3], owners, swap1[:2], [value])
        handle1, tok1 = scatter_start("1", names, gd, got, value)
        handles.append(handle1)
        gd = [_by_destination(n, g[n]) for n in late_group]
        swap = _swap_split("rs_swap0_late_start", gd, [jax.ShapeDtypeStruct(a.shape, a.dtype) for a in gd], own_late,
                           None, value)
        pending.append(swap)
        return value + (tok1[0, 0] + swap[4][0, 0])

    def on_attention_done(value):
        swap = pending[0]
        gd, got = _swap_split("rs_swap0_late_wait", swap[2], swap[3], own_late, swap[:2], [value])
        handle0, tok0 = scatter_start("0_late", late_group, gd, got, value)
        handles.append(handle0)
        return value + tok0[0, 0].astype(value.dtype)

    dx, dh, gk0 = _layer_bwd(dx, dh, ws0, ws1["mix_pre_g"], sv0, tc, ts, on_late_grads, on_attention_done)
    grads1["mix_pre_g"] = gk0.pop("next_pre_g")
    grad_x, gk0["mix_pre_g"] = _row_bwd("rms_first_bwd", _f_x_rms, [(x0, d, 0)], [_row(ws0["mix_pre_g"])],
                                        [(dx, d, 0), (dh, d, 0)], [(0, F32)], [0])
    tb1 = scatter_wait(handles[0], [grad_x])
    grads0 = _unprep_grads(gk0, d)
    handle, token = swap_and_start("0_early", early_group, grads0, grad_x)
    handles.append(handle)

    def finish(l, sums, updates):
        sums = _pair_exchange("rs_join%d" % l, sums, owners, False, layer=l)
        updates = [_adamw("adamw%d_%s" % (l, n), w[n], sums[i], m[n], v[n], l, updates[i])
                   for i, n in enumerate(names)]
        return sums, updates

    sums = [_add_quads("rs_sum1_" + n, *tb1[n], 1, flag[n], None, token) for n in names]
    sums, updates = finish(1, sums, [None] * len(names))
    out = {}
    rep = list(REPLICATED)
    grads = [grads0, grads1]
    g_rep = [jnp.stack([grads[l][n].reshape(w[n].shape[1:]) for l in range(depth)]) for n in rep]
    for n, *res in zip(rep, *_all_reduce_adamw(g_rep, [w[n] for n in rep], [m[n] for n in rep], [v[n] for n in rep])):
        out[n] = tuple(res)
    hidden = [u[0] for u in updates] + [out[rep[0]][0]]
    tb0 = {**scatter_wait(handles[1], hidden), **scatter_wait(handles[2], hidden)}
    sums = [_add_quads("rs_sum0_" + n, *tb0[n], 0, flag[n], sums[i]) for i, n in enumerate(names)]
    sums, updates = finish(0, sums, updates)
    for n, gr, upd in zip(names, sums, updates):
        out[n] = (gr, *upd)
    out["w_in"] = tuple(jnp.swapaxes(a, 1, 2) for a in out["w_in"])
    return (loss, grad_x[None], *[out[n][i] for i in range(4) for n in WEIGHTS])
```

```python
import functools
import math

import jax
import jax.numpy as jnp
from jax import lax
from jax.experimental import pallas as pl
from jax.experimental.pallas import tpu as pltpu

F32 = jnp.float32
_MXU_DT = jnp.bfloat16
_VMEM_LIMIT = 48 * 1024 * 1024
_LANES = 128
_MESH = pl.DeviceIdType.MESH

N_HEADS = 8
QK_NOPE = 64
QK_ROPE = 32
V_HEAD = 64
HEAD_PAD = 128
SG_GROUPS = 4
SG_CHUNK = 128
CONV_K = 31
FFN_K = 3
ROPE_THETA = 10000.0
EPS = 1e-6
ADAM_LR, ADAM_B1, ADAM_B2, ADAM_EPS, ADAM_WD, ADAM_STEP = 0.001, 0.9, 0.999, 1e-08, 0.01, 10

SHARDED_LAST = ("w_in", "conv_dw_w", "conv_out_w", "sg_out_w", "mla_w_uq", "mla_w_ukv", "mla_w_o", "ffn_w_up",
                "ffn_dw_w")
SHARDED_MID = ("w_out", "ffn_w_down")
SHARDED = SHARDED_LAST + SHARDED_MID
WEIGHTS = ("mix_pre_g", "mix_post_g", "ffn_pre_g", "ffn_post_g", "w_in", "conv_dw_w", "conv_dw_b", "conv_ln_g",
           "conv_ln_b", "conv_out_w", "sg_ln_g", "sg_ln_b", "sg_w", "sg_b", "sg_out_w", "mla_q_norm_g", "mla_w_uq",
           "mla_kv_norm_g", "mla_w_ukv", "mla_w_o", "w_out", "ffn_w_up", "ffn_dw_w", "ffn_dw_b", "ffn_w_down")
REPLICATED = tuple(n for n in WEIGHTS if n not in SHARDED)


def _cparams(sem=None):
    return pltpu.CompilerParams(dimension_semantics=sem, vmem_limit_bytes=_VMEM_LIMIT)


def _pick(n, cands):
    for c in cands:
        if n % c == 0:
            return c
    return n


def _largest_tile(dim, cap):
    for t in range(min(cap, dim) // _LANES * _LANES, 0, -_LANES):
        if dim % t == 0:
            return t
    return dim


_MM_VMEM_BUDGET = 36 * 1024 * 1024
_MM_TM_CAP, _MM_TN_CAP, _MM_TK_CAP = 1024, 1536, 3072


def _mm(name, a, b, *, ta=False, tb=False, out_dtype=F32, a_halves=False, b_halves=False, out_quarters=False):
    assert not (a_halves and ta) and not (b_halves and tb)
    if a_halves:
        m, kdim = a.shape[1], 2 * a.shape[2]
    else:
        (kdim, m) = a.shape if ta else a.shape[::-1]
    if b_halves:
        kdim2, n = b.shape[1], 2 * b.shape[2]
    else:
        (n, kdim2) = b.shape if tb else b.shape[::-1]
    assert kdim == kdim2, (a.shape, b.shape, ta, tb)
    tk = _largest_tile(kdim // 2 if a_halves else kdim, _MM_TK_CAP)
    tn = _largest_tile(n // 4 if out_quarters else (n // 2 if b_halves else n), _MM_TN_CAP)
    nk = kdim // tk
    ab, bb, ob = a.dtype.itemsize, b.dtype.itemsize, jnp.dtype(out_dtype).itemsize
    tm = _largest_tile(m, _MM_TM_CAP)
    vmem = lambda t: 2 * (t * tk * ab + tk * tn * bb + t * tn * ob) + (t * tn * 4 if nk > 1 else 0)
    while vmem(tm) > _MM_VMEM_BUDGET and tm > _LANES:
        tm = _largest_tile(m, tm - _LANES)
    dims = (((0 if ta else 1,), (1 if tb else 0,)), ((), ()))

    def dot(a_ref, b_ref):
        return lax.dot_general(a_ref[...].astype(_MXU_DT), b_ref[...].astype(_MXU_DT), dims,
                               preferred_element_type=F32)

    def body_one(a_ref, b_ref, o_ref):
        o_ref[...] = dot(a_ref, b_ref).astype(o_ref.dtype)

    def body_acc(a_ref, b_ref, o_ref, acc_ref):
        k = pl.program_id(2)

        @pl.when(k == 0)
        def _():
            acc_ref[...] = jnp.zeros_like(acc_ref)

        acc_ref[...] += dot(a_ref, b_ref)

        @pl.when(k == nk - 1)
        def _():
            o_ref[...] = acc_ref[...].astype(o_ref.dtype)

    if a_halves:
        per = nk // 2
        a_spec = pl.BlockSpec((None, tm, tk), lambda i, j, k: (k // per, i, k % per))
    elif ta:
        a_spec = pl.BlockSpec((tk, tm), lambda i, j, k: (k, i))
    else:
        a_spec = pl.BlockSpec((tm, tk), lambda i, j, k: (i, k))
    if b_halves:
        per_b = n // 2 // tn
        b_spec = pl.BlockSpec((None, tk, tn), lambda i, j, k: (j // per_b, k, j % per_b))
    elif tb:
        b_spec = pl.BlockSpec((tn, tk), lambda i, j, k: (j, k))
    else:
        b_spec = pl.BlockSpec((tk, tn), lambda i, j, k: (k, j))
    if out_quarters:
        per_o = n // 4 // tn
        o_spec = pl.BlockSpec((None, tm, tn), lambda i, j, k: (j // per_o, i, j % per_o))
        o_shape = jax.ShapeDtypeStruct((4, m, n // 4), out_dtype)
    else:
        o_spec = pl.BlockSpec((tm, tn), lambda i, j, k: (i, j))
        o_shape = jax.ShapeDtypeStruct((m, n), out_dtype)
    return pl.pallas_call(
        body_one if nk == 1 else body_acc, name=name, grid=(m // tm, n // tn, nk),
        in_specs=[a_spec, b_spec], out_specs=o_spec, out_shape=o_shape,
        scratch_shapes=[] if nk == 1 else [pltpu.VMEM((tm, tn), F32)],
        compiler_params=_cparams(("parallel", "parallel", "arbitrary")),
    )(a, b)


def _row_spec(tm, width, idx):
    return pl.BlockSpec((tm, width), lambda i: (i, idx))


def _full_spec(shape):
    zeros = (0,) * len(shape)
    return pl.BlockSpec(shape, lambda i: zeros)


def _row_fwd(name, fn, rows, params, outs, tm=256):
    s = rows[0][0].shape[0]
    nr, npar = len(rows), len(params)

    def body(*refs):
        vals = [r[...].astype(F32) for r in refs[:nr + npar]]
        res = fn(*vals)
        for o_ref, r in zip(refs[nr + npar:], res):
            o_ref[...] = r.astype(o_ref.dtype)

    return pl.pallas_call(
        body, name=name, grid=(s // tm,),
        in_specs=[_row_spec(tm, w, i) for _, w, i in rows] + [_full_spec(p.shape) for p in params],
        out_specs=[_row_spec(tm, w, 0) for w, _ in outs],
        out_shape=[jax.ShapeDtypeStruct((s, w), dt) for w, dt in outs],
        compiler_params=_cparams(("parallel",)),
    )(*[r[0] for r in rows], *params)


def _row_bwd(name, fn, rows, params, cots, row_grads, param_grads, tm=256, place=None, into=None):
    s = rows[0][0].shape[0]
    nr, npar, nc = len(rows), len(params), len(cots)
    row_grads = [((idxs,) if isinstance(idxs, int) else tuple(idxs), dt) for idxs, dt in row_grads]
    widths = [sum(rows[i][1] for i in idxs) for idxs, _ in row_grads]

    def body(*refs):
        i = pl.program_id(0)
        vals = [r[...].astype(F32) for r in refs[:nr + npar]]
        cvals = tuple(r[...].astype(F32) for r in refs[nr + npar:nr + npar + nc])
        _, vjp = jax.vjp(fn, *vals)
        grads = vjp(cvals)
        outs = refs[nr + npar + nc + (into is not None):]
        for o_ref, (idxs, _) in zip(outs, row_grads):
            pos = 0
            for idx in idxs:
                o_ref[:, pos:pos + rows[idx][1]] = grads[idx].astype(o_ref.dtype)
                pos += rows[idx][1]
        for o_ref, idx in zip(outs[len(row_grads):], param_grads):
            @pl.when(i == 0)
            def _(o_ref=o_ref):
                o_ref[...] = jnp.zeros_like(o_ref)

            o_ref[...] += grads[nr + idx]

    out_specs = [_row_spec(tm, w, 0) for w in widths] + [_full_spec(params[idx].shape) for idx in param_grads]
    out_shape = ([jax.ShapeDtypeStruct((s, w), dt) for w, (_, dt) in zip(widths, row_grads)]
                 + [jax.ShapeDtypeStruct(params[idx].shape, F32) for idx in param_grads])
    extra, aliases = [], {}
    if place is not None:
        out_specs[0] = _row_spec(tm, widths[0], place[1])
        out_shape[0] = jax.ShapeDtypeStruct((s, place[0]), row_grads[0][1])
    if into is not None:
        extra, aliases = [into], {nr + npar + nc: 0}
    return pl.pallas_call(
        body, name=name, grid=(s // tm,),
        in_specs=([_row_spec(tm, w, i) for _, w, i in rows] + [_full_spec(p.shape) for p in params]
                  + [_row_spec(tm, w, i) for _, w, i in cots] + [_ANY] * len(extra)),
        out_specs=out_specs, out_shape=out_shape, input_output_aliases=aliases,
        compiler_params=_cparams(("arbitrary",)),
    )(*[r[0] for r in rows], *params, *[c[0] for c in cots], *extra)


def _rms(x, g):
    return x * lax.rsqrt(jnp.mean(x * x, axis=-1, keepdims=True) + EPS) * g


def _ln(x, g, b):
    mu = jnp.mean(x, axis=-1, keepdims=True)
    xc = x - mu
    var = jnp.mean(xc * xc, axis=-1, keepdims=True)
    return xc * lax.rsqrt(var + EPS) * g + b


def _sigmoid(x):
    return 1.0 / (1.0 + jnp.exp(-x))


def _gelu(x):
    return x * (0.5 * (1.0 + jnp.tanh(math.sqrt(2.0 / math.pi) * (x + 0.044715 * (x * x * x)))))


def _f_rms(x, g):
    return (_rms(x, g),)


def _f_x_rms(x, g):
    return (x, _rms(x, g))


def _f_ln_silu(z, g, b):
    y = _ln(z, g, b)
    return (y * _sigmoid(y),)


def _f_sg_pre(bu, bv, g, b):
    return (_gelu(bu), _ln(_gelu(bv), g, b))


def _f_merge(g0, g1, g2, ya, yb, yc):
    return (_sigmoid(g0) * ya + _sigmoid(g1) * yb + _sigmoid(g2) * yc,)


def _f_resid_rms(x, t, g_post):
    return (x + _rms(t, g_post),)


def _f_resid_rms_rms(x, t, g_post, g_next):
    x1 = x + _rms(t, g_post)
    return (x1, _rms(x1, g_next))


def _f_geglu(zg, zv):
    return _gelu(zg) * zv


_CONV_TILE_ELEMS = 16 * 1024


def _conv_tr(c):
    return _CONV_TILE_ELEMS // c


def _conv_tile(zp_ref, w_ref, bias, k_taps, off, r0):
    c = zp_ref.shape[1]
    tr = _conv_tr(c)
    acc = jnp.broadcast_to(bias, (tr, c))
    for k in range(k_taps):
        acc = acc + w_ref[k:k + 1, :] * zp_ref[r0 + off + k:r0 + off + k + tr, :]
    return acc


def _conv_bwd_input_tile(dzp_ref, w_ref, k_taps, r0):
    c = dzp_ref.shape[1]
    tr = _conv_tr(c)
    acc = jnp.zeros((tr, c), F32)
    for k in range(k_taps):
        s0 = r0 + (k_taps - 1) - k
        acc = acc + w_ref[k:k + 1, :] * dzp_ref[s0:s0 + tr, :]
    return acc


def _conv_bwd_weight(dzp_ref, zp_ref, dw_ref, db_ref, k_taps, off, s):
    c = zp_ref.shape[1]
    tr = _conv_tr(c)
    fold = lambda v: jnp.sum(v.reshape(tr // 8, 8, c), axis=0)
    for k in range(k_taps):
        acc = jnp.zeros((8, c), F32)
        for r in range(s // tr):
            r0 = r * tr
            acc = acc + fold(dzp_ref[r0:r0 + tr, :] * zp_ref[r0 + off + k:r0 + off + k + tr, :])
        dw_ref[k:k + 1, :] = jnp.sum(acc, axis=0, keepdims=True)
    acc = jnp.zeros((8, c), F32)
    for r in range(s // tr):
        acc = acc + fold(dzp_ref[r * tr:(r + 1) * tr, :])
    db_ref[...] = jnp.sum(acc, axis=0, keepdims=True)


def _glu_conv_fwd(p, blk0, w, b):
    s = p.shape[0]
    k_taps, c = w.shape
    cb, pad = _LANES, 32
    off = pad - (k_taps - 1)

    def body(a_ref, w_ref, b_ref, o_ref, zp_ref):
        zp_ref[0:pad, :] = jnp.zeros((pad, cb), F32)
        zp_ref[pad:pad + s, :] = a_ref[:, 0:cb] * _sigmoid(a_ref[:, cb:2 * cb])
        tr = _conv_tr(cb)
        for r in range(s // tr):
            o_ref[r * tr:(r + 1) * tr, :] = _conv_tile(zp_ref, w_ref, b_ref[...], k_taps, off, r * tr)

    return pl.pallas_call(
        body, name="glu_conv_fwd", grid=(c // cb,),
        in_specs=[pl.BlockSpec((s, 2 * cb), lambda j: (0, blk0 + j)),
                  pl.BlockSpec((k_taps, cb), lambda j: (0, j)), pl.BlockSpec((1, cb), lambda j: (0, j))],
        out_specs=pl.BlockSpec((s, cb), lambda j: (0, j)),
        out_shape=jax.ShapeDtypeStruct((s, c), F32),
        scratch_shapes=[pltpu.VMEM((s + pad, cb), F32)],
        compiler_params=_cparams(("parallel",)),
    )(p, w, b)


def _glu_conv_bwd(p, blk0, w, dz, dp):
    s = p.shape[0]
    k_taps, c = w.shape
    cb, pad = _LANES, 32
    off = pad - (k_taps - 1)

    def body(a_ref, w_ref, dz_ref, dp_in, da_ref, dw_ref, db_ref, zp_ref, dzp_ref):
        zp_ref[0:pad, :] = jnp.zeros((pad, cb), F32)
        zp_ref[pad:pad + s, :] = a_ref[:, 0:cb] * _sigmoid(a_ref[:, cb:2 * cb])
        dzp_ref[0:s, :] = dz_ref[...]
        dzp_ref[s:s + pad, :] = jnp.zeros((pad, cb), F32)
        tr = _conv_tr(cb)
        for r in range(s // tr):
            rows = slice(r * tr, (r + 1) * tr)
            dz0 = _conv_bwd_input_tile(dzp_ref, w_ref, k_taps, r * tr)
            sg = _sigmoid(a_ref[rows, cb:2 * cb])
            da_ref[rows, 0:cb] = (dz0 * sg).astype(da_ref.dtype)
            da_ref[rows, cb:2 * cb] = (dz0 * a_ref[rows, 0:cb] * sg * (1.0 - sg)).astype(da_ref.dtype)
        _conv_bwd_weight(dzp_ref, zp_ref, dw_ref, db_ref, k_taps, off, s)

    return pl.pallas_call(
        body, name="glu_conv_bwd", grid=(c // cb,),
        in_specs=[pl.BlockSpec((s, 2 * cb), lambda j: (0, blk0 + j)),
                  pl.BlockSpec((k_taps, cb), lambda j: (0, j)), pl.BlockSpec((s, cb), lambda j: (0, j)), _ANY],
        out_specs=[pl.BlockSpec((s, 2 * cb), lambda j: (0, blk0 + j)),
                   pl.BlockSpec((k_taps, cb), lambda j: (0, j)), pl.BlockSpec((1, cb), lambda j: (0, j))],
        out_shape=[jax.ShapeDtypeStruct(dp.shape, dp.dtype),
                   jax.ShapeDtypeStruct((k_taps, c), F32), jax.ShapeDtypeStruct((1, c), F32)],
        scratch_shapes=[pltpu.VMEM((s + pad, cb), F32), pltpu.VMEM((s + pad, cb), F32)],
        input_output_aliases={3: 0},
        compiler_params=_cparams(("parallel",)),
    )(p, w, dz, dp)


def _conv_geglu_fwd(up, w, b):
    s, f2 = up.shape
    f = f2 // 2
    k_taps = w.shape[0]
    cb, pad = _FFN_CB, 8
    off = pad - (k_taps - 1)
    nb = f // cb

    def body(ug_ref, uv_ref, wg_ref, wv_ref, bg_ref, bv_ref, o_ref, z_ref, w_ref, b_ref):
        _pair(w_ref, wg_ref[...], wv_ref[...], cb)
        _pair(b_ref, bg_ref[...], bv_ref[...], cb)
        z_ref[0:pad, :] = jnp.zeros((pad, 2 * cb), F32)
        z_ref[pad:pad + s, 0:cb] = ug_ref[...]
        z_ref[pad:pad + s, cb:2 * cb] = uv_ref[...]
        tr = _conv_tr(2 * cb)
        for r in range(s // tr):
            z = _conv_tile(z_ref, w_ref, b_ref[...], k_taps, off, r * tr)
            o_ref[r * tr:(r + 1) * tr, :] = _f_geglu(z[:, 0:cb], z[:, cb:2 * cb]).astype(o_ref.dtype)

    two = lambda rows_: [pl.BlockSpec((rows_, cb), lambda j: (0, j)), pl.BlockSpec((rows_, cb), lambda j: (0, nb + j))]
    return pl.pallas_call(
        body, name="conv_geglu_fwd", grid=(nb,),
        in_specs=two(s) + two(k_taps) + two(1),
        out_specs=pl.BlockSpec((s, cb), lambda j: (0, j)),
        out_shape=jax.ShapeDtypeStruct((s, f), _MXU_DT),
        scratch_shapes=[pltpu.VMEM((s + pad, 2 * cb), F32), pltpu.VMEM((k_taps, 2 * cb), F32),
                        pltpu.VMEM((1, 2 * cb), F32)],
        compiler_params=_cparams(("parallel",)),
    )(up, up, w, w, b, b)


def _pair(dst_ref, first, second, cb):
    dst_ref[:, 0:cb] = first
    dst_ref[:, cb:2 * cb] = second


def _conv_geglu_bwd(up, w, b, dact):
    s, f2 = up.shape
    f = f2 // 2
    k_taps = w.shape[0]
    cb, pad = _FFN_CB, 8
    off = pad - (k_taps - 1)
    nb = f // cb

    def body(ug_ref, uv_ref, wg_ref, wv_ref, bg_ref, bv_ref, da_ref, du_ref, dw_ref, db_ref, z_ref, dz_ref, w_ref,
             b_ref, dw_sc, db_sc):
        _pair(w_ref, wg_ref[...], wv_ref[...], cb)
        _pair(b_ref, bg_ref[...], bv_ref[...], cb)
        z_ref[0:pad, :] = jnp.zeros((pad, 2 * cb), F32)
        z_ref[pad:pad + s, 0:cb] = ug_ref[...]
        z_ref[pad:pad + s, cb:2 * cb] = uv_ref[...]
        dz_ref[s:s + pad, :] = jnp.zeros((pad, 2 * cb), F32)
        tr = _conv_tr(2 * cb)
        for r in range(s // tr):
            rows = slice(r * tr, (r + 1) * tr)
            z = _conv_tile(z_ref, w_ref, b_ref[...], k_taps, off, r * tr)
            _, vjp = jax.vjp(_f_geglu, z[:, 0:cb], z[:, cb:2 * cb])
            dzg, dzv = vjp(da_ref[rows, :].astype(F32))
            dz_ref[rows, 0:cb] = dzg
            dz_ref[rows, cb:2 * cb] = dzv
        for r in range(s // tr):
            rows = slice(r * tr, (r + 1) * tr)
            du = _conv_bwd_input_tile(dz_ref, w_ref, k_taps, r * tr).astype(du_ref.dtype)
            du_ref[0, rows, :] = du[:, 0:cb]
            du_ref[1, rows, :] = du[:, cb:2 * cb]
        _conv_bwd_weight(dz_ref, z_ref, dw_sc, db_sc, k_taps, off, s)
        for half in range(2):
            dw_ref[half] = dw_sc[:, half * cb:(half + 1) * cb]
            db_ref[half] = db_sc[:, half * cb:(half + 1) * cb]

    two = lambda rows_: [pl.BlockSpec((rows_, cb), lambda j: (0, j)), pl.BlockSpec((rows_, cb), lambda j: (0, nb + j))]
    both = lambda rows_: pl.BlockSpec((2, rows_, cb), lambda j: (0, 0, j))
    return pl.pallas_call(
        body, name="conv_geglu_bwd", grid=(nb,),
        in_specs=two(s) + two(k_taps) + two(1) + [pl.BlockSpec((s, cb), lambda j: (0, j))],
        out_specs=[both(s), both(k_taps), both(1)],
        out_shape=[jax.ShapeDtypeStruct((2, s, f), _MXU_DT), jax.ShapeDtypeStruct((2, k_taps, f), F32),
                   jax.ShapeDtypeStruct((2, 1, f), F32)],
        scratch_shapes=[pltpu.VMEM((s + pad, 2 * cb), F32), pltpu.VMEM((s + pad, 2 * cb), F32),
                        pltpu.VMEM((k_taps, 2 * cb), F32), pltpu.VMEM((1, 2 * cb), F32),
                        pltpu.VMEM((k_taps, 2 * cb), F32), pltpu.VMEM((1, 2 * cb), F32)],
        compiler_params=_cparams(("parallel",)),
    )(up, up, w, w, b, b, dact)


def _tril_mask():
    t = lax.broadcasted_iota(jnp.int32, (SG_CHUNK, SG_CHUNK), 0)
    s = lax.broadcasted_iota(jnp.int32, (SG_CHUNK, SG_CHUNK), 1)
    return t >= s


def _sg_mix_fwd(u, vn, w, bcol):
    s, c = u.shape
    gw = c // SG_GROUPS

    def body(u_ref, v_ref, w_ref, b_ref, o_ref):
        wm = jnp.where(_tril_mask(), w_ref[0], 0.0).astype(_MXU_DT)
        for n in range(s // SG_CHUNK):
            rows = slice(n * SG_CHUNK, (n + 1) * SG_CHUNK)
            mixed = jnp.dot(wm, v_ref[rows, :], preferred_element_type=F32) + b_ref[0]
            o_ref[rows, :] = (u_ref[rows, :] * mixed).astype(o_ref.dtype)

    return pl.pallas_call(
        body, name="sg_mix_fwd", grid=(SG_GROUPS,),
        in_specs=[pl.BlockSpec((s, gw), lambda g: (0, g)), pl.BlockSpec((s, gw), lambda g: (0, g)),
                  pl.BlockSpec((1, SG_CHUNK, SG_CHUNK), lambda g: (g, 0, 0)),
                  pl.BlockSpec((1, SG_CHUNK, 1), lambda g: (g, 0, 0))],
        out_specs=pl.BlockSpec((s, gw), lambda g: (0, g)),
        out_shape=jax.ShapeDtypeStruct((s, c), _MXU_DT),
        compiler_params=_cparams(("parallel",)),
    )(u, vn, w, bcol)


def _sg_mix_bwd(u, vn, w, bcol, dub):
    s, c = u.shape
    gw = c // SG_GROUPS

    def body(u_ref, v_ref, w_ref, b_ref, d_ref, du_ref, dv_ref, dw_ref, db_ref):
        mask = _tril_mask()
        wm = jnp.where(mask, w_ref[0], 0.0).astype(_MXU_DT)
        dw = jnp.zeros((SG_CHUNK, SG_CHUNK), F32)
        db = jnp.zeros((SG_CHUNK, 1), F32)
        for n in range(s // SG_CHUNK):
            rows = slice(n * SG_CHUNK, (n + 1) * SG_CHUNK)
            v = v_ref[rows, :]
            d = d_ref[rows, :].astype(F32)
            mixed = jnp.dot(wm, v, preferred_element_type=F32) + b_ref[0]
            du_ref[rows, :] = d * mixed
            dmix = d * u_ref[rows, :]
            dmix_lo = dmix.astype(_MXU_DT)
            dv_ref[rows, :] = lax.dot_general(wm, dmix_lo, (((0,), (0,)), ((), ())), preferred_element_type=F32)
            dw = dw + lax.dot_general(dmix_lo, v, (((1,), (1,)), ((), ())), preferred_element_type=F32)
            db = db + jnp.sum(dmix, axis=1, keepdims=True)
        dw_ref[0] = jnp.where(mask, dw, 0.0)
        db_ref[0] = db

    return pl.pallas_call(
        body, name="sg_mix_bwd", grid=(SG_GROUPS,),
        in_specs=[pl.BlockSpec((s, gw), lambda g: (0, g)), pl.BlockSpec((s, gw), lambda g: (0, g)),
                  pl.BlockSpec((1, SG_CHUNK, SG_CHUNK), lambda g: (g, 0, 0)),
                  pl.BlockSpec((1, SG_CHUNK, 1), lambda g: (g, 0, 0)), pl.BlockSpec((s, gw), lambda g: (0, g))],
        out_specs=[pl.BlockSpec((s, gw), lambda g: (0, g)), pl.BlockSpec((s, gw), lambda g: (0, g)),
                   pl.BlockSpec((1, SG_CHUNK, SG_CHUNK), lambda g: (g, 0, 0)),
                   pl.BlockSpec((1, SG_CHUNK, 1), lambda g: (g, 0, 0))],
        out_shape=[jax.ShapeDtypeStruct((s, c), F32), jax.ShapeDtypeStruct((s, c), F32),
                   jax.ShapeDtypeStruct((SG_GROUPS, SG_CHUNK, SG_CHUNK), F32),
                   jax.ShapeDtypeStruct((SG_GROUPS, SG_CHUNK, 1), F32)],
        compiler_params=_cparams(("parallel",)),
    )(u, vn, w, bcol, dub)


def _rope_fwd(q2, kv2, p, krm_idx, krs_idx, tc, ts):
    s = q2.shape[0]
    hw = N_HEADS * HEAD_PAD
    tm = 256

    def body(qm_ref, qs_ref, kn_ref, v_ref, krm_ref, krs_ref, tc_ref, ts_ref, q_ref, k_ref, vo_ref):
        tcv, tsv = tc_ref[...], ts_ref[...]
        kpe = krm_ref[...] * tcv + krs_ref[...] * tsv
        for h in range(N_HEADS):
            cols = slice(h * HEAD_PAD, (h + 1) * HEAD_PAD)
            q_ref[:, cols] = (qm_ref[:, cols] * tcv + qs_ref[:, cols] * tsv).astype(q_ref.dtype)
            k_ref[:, cols] = (kn_ref[:, cols] + kpe).astype(k_ref.dtype)
        vo_ref[...] = v_ref[...].astype(vo_ref.dtype)

    return pl.pallas_call(
        body, name="rope_fwd", grid=(s // tm,),
        in_specs=[_row_spec(tm, hw, 0), _row_spec(tm, hw, 1), _row_spec(tm, hw, 0), _row_spec(tm, hw, 1),
                  _row_spec(tm, HEAD_PAD, krm_idx), _row_spec(tm, HEAD_PAD, krs_idx),
                  _row_spec(tm, HEAD_PAD, 0), _row_spec(tm, HEAD_PAD, 0)],
        out_specs=[_row_spec(tm, hw, 0)] * 3,
        out_shape=[jax.ShapeDtypeStruct((s, hw), _MXU_DT)] * 3,
        compiler_params=_cparams(("parallel",)),
    )(q2, q2, kv2, kv2, p, p, tc, ts)


def _rope_bwd(dq, dk, dv, tc, ts, dp, kr_blk):
    s = dq.shape[0]
    hw = N_HEADS * HEAD_PAD
    tm = 256

    def body(dq_ref, dk_ref, dv_ref, tc_ref, ts_ref, dp_in, dq2_ref, dkv2_ref, dkr_ref):
        tcv, tsv = tc_ref[...], ts_ref[...]
        dkpe = jnp.zeros((tm, HEAD_PAD), F32)
        for h in range(N_HEADS):
            cols = slice(h * HEAD_PAD, (h + 1) * HEAD_PAD)
            dqh = dq_ref[:, cols]
            dq2_ref[:, cols] = (dqh * tcv).astype(dq2_ref.dtype)
            dq2_ref[:, hw + h * HEAD_PAD:hw + (h + 1) * HEAD_PAD] = (dqh * tsv).astype(dq2_ref.dtype)
            dkpe = dkpe + dk_ref[:, cols]
        dkv2_ref[:, 0:hw] = dk_ref[...].astype(dkv2_ref.dtype)
        dkv2_ref[:, hw:2 * hw] = dv_ref[...].astype(dkv2_ref.dtype)
        dkr_ref[:, 0:HEAD_PAD] = (dkpe * tcv).astype(dkr_ref.dtype)
        dkr_ref[:, HEAD_PAD:2 * HEAD_PAD] = (dkpe * tsv).astype(dkr_ref.dtype)
        dkr_ref[:, 2 * HEAD_PAD:3 * HEAD_PAD] = jnp.zeros((tm, HEAD_PAD), dkr_ref.dtype)

    return pl.pallas_call(
        body, name="rope_bwd", grid=(s // tm,),
        in_specs=[_row_spec(tm, hw, 0)] * 3 + [_row_spec(tm, HEAD_PAD, 0)] * 2 + [_ANY],
        out_specs=[_row_spec(tm, 2 * hw, 0), _row_spec(tm, 2 * hw, 0), _row_spec(tm, 3 * HEAD_PAD, kr_blk)],
        out_shape=[jax.ShapeDtypeStruct((s, 2 * hw), _MXU_DT), jax.ShapeDtypeStruct((s, 2 * hw), _MXU_DT),
                   jax.ShapeDtypeStruct(dp.shape, dp.dtype)],
        input_output_aliases={5: 2},
        compiler_params=_cparams(("parallel",)),
    )(dq, dk, dv, tc, ts, dp)


_ATTN_TQ = 512
_ATTN_SCALE = (QK_NOPE + QK_ROPE) ** -0.5


def _attn_probs(q, k, i):
    s = k.shape[0]
    sc = lax.dot_general(q, k, (((1,), (1,)), ((), ())), preferred_element_type=F32) * _ATTN_SCALE
    row = i * _ATTN_TQ + lax.broadcasted_iota(jnp.int32, (_ATTN_TQ, s), 0)
    col = lax.broadcasted_iota(jnp.int32, (_ATTN_TQ, s), 1)
    sc = jnp.where(row >= col, sc, jnp.finfo(F32).min)
    e = jnp.exp(sc - jnp.max(sc, axis=1, keepdims=True))
    return e * (1.0 / jnp.sum(e, axis=1, keepdims=True))


def _per_query_block(s, fn):
    i = pl.program_id(1)
    for n in range(s // _ATTN_TQ):
        @pl.when(i == n)
        def _(n=n):
            fn(n, (n + 1) * _ATTN_TQ)


def _attn_fwd(q, k, v):
    s = q.shape[0]

    def body(q_ref, k_ref, v_ref, o_ref):
        def block(n, kl):
            p = _attn_probs(q_ref[...], k_ref[0:kl, :], n)
            o_ref[...] = jnp.dot(p.astype(_MXU_DT), v_ref[0:kl, :], preferred_element_type=F32).astype(o_ref.dtype)

        _per_query_block(s, block)

    qspec = pl.BlockSpec((_ATTN_TQ, HEAD_PAD), lambda h, i: (i, h))
    kspec = pl.BlockSpec((s, HEAD_PAD), lambda h, i: (0, h))
    return pl.pallas_call(
        body, name="attn_fwd", grid=(N_HEADS, s // _ATTN_TQ),
        in_specs=[qspec, kspec, kspec], out_specs=qspec,
        out_shape=jax.ShapeDtypeStruct(q.shape, _MXU_DT),
        compiler_params=_cparams(("parallel", "parallel")),
    )(q, k, v)


def _attn_bwd(q, k, v, do):
    s = q.shape[0]

    def body(q_ref, k_ref, v_ref, do_ref, dq_ref, dk_ref, dv_ref):
        i = pl.program_id(1)

        @pl.when(i == 0)
        def _():
            dk_ref[...] = jnp.zeros_like(dk_ref)
            dv_ref[...] = jnp.zeros_like(dv_ref)

        def block(n, kl):
            qv, kv, dov = q_ref[...], k_ref[0:kl, :], do_ref[...]
            p = _attn_probs(qv, kv, n)
            dp = lax.dot_general(dov, v_ref[0:kl, :], (((1,), (1,)), ((), ())), preferred_element_type=F32)
            delta = jnp.sum(p * dp, axis=1, keepdims=True)
            ds = (p * (dp - delta) * _ATTN_SCALE).astype(_MXU_DT)
            dq_ref[...] = jnp.dot(ds, kv, preferred_element_type=F32)
            dk_ref[0:kl, :] += lax.dot_general(ds, qv, (((0,), (0,)), ((), ())), preferred_element_type=F32)
            dv_ref[0:kl, :] += lax.dot_general(p.astype(_MXU_DT), dov, (((0,), (0,)), ((), ())),
                                               preferred_element_type=F32)

        _per_query_block(s, block)

    qspec = pl.BlockSpec((_ATTN_TQ, HEAD_PAD), lambda h, i: (i, h))
    kspec = pl.BlockSpec((s, HEAD_PAD), lambda h, i: (0, h))
    return pl.pallas_call(
        body, name="attn_bwd", grid=(N_HEADS, s // _ATTN_TQ),
        in_specs=[qspec, kspec, kspec, qspec], out_specs=[qspec, kspec, kspec],
        out_shape=[jax.ShapeDtypeStruct(q.shape, F32)] * 3,
        compiler_params=_cparams(("parallel", "arbitrary")),
    )(q, k, v, do)


def _loss_head(y, target):
    s, d = y.shape
    tm = 256

    def body(y_ref, t_ref, loss_ref, dy_ref):
        @pl.when(pl.program_id(0) == 0)
        def _():
            loss_ref[...] = jnp.zeros_like(loss_ref)

        err = y_ref[...] - t_ref[...]
        loss_ref[...] += 0.5 * jnp.sum(jnp.mean(err * err, axis=-1, keepdims=True), axis=0, keepdims=True)
        dy_ref[...] = err * (1.0 / d)

    return pl.pallas_call(
        body, name="loss_head", grid=(s // tm,),
        in_specs=[_row_spec(tm, d, 0), _row_spec(tm, d, 0)],
        out_specs=[_full_spec((1, 1)), _row_spec(tm, d, 0)],
        out_shape=[jax.ShapeDtypeStruct((1, 1), F32), jax.ShapeDtypeStruct((s, d), F32)],
        compiler_params=_cparams(("arbitrary",)),
    )(y, target)


def _adamw_math(w, g, m, v):
    mn = ADAM_B1 * m + (1.0 - ADAM_B1) * g
    vn = ADAM_B2 * v + (1.0 - ADAM_B2) * (g * g)
    m_hat = mn / (1.0 - ADAM_B1 ** ADAM_STEP)
    v_hat = vn / (1.0 - ADAM_B2 ** ADAM_STEP)
    return -ADAM_LR * (m_hat / (jnp.sqrt(v_hat) + ADAM_EPS) + ADAM_WD * w), mn, vn


def _adamw(name, w, g, m, v, layer, into):
    _, k, n = w.shape
    tk, tn = _slab_block(k, n)

    def body(w_ref, g_ref, m_ref, v_ref, *rest):
        d_ref, mo_ref, vo_ref = rest[-3:]
        d_ref[...], mo_ref[...], vo_ref[...] = _adamw_math(w_ref[...], g_ref[...], m_ref[...], v_ref[...])

    spec = pl.BlockSpec((1, tk, tn), lambda j, jn: (layer, j, jn))
    extra = [] if into is None else list(into)
    return pl.pallas_call(
        body, name=name, grid=(k // tk, n // tn), in_specs=[spec] * 4 + [_ANY] * len(extra), out_specs=[spec] * 3,
        out_shape=[jax.ShapeDtypeStruct(w.shape, F32)] * 3,
        input_output_aliases={4 + i: i for i in range(len(extra))},
        compiler_params=_cparams(("parallel", "parallel")),
    )(w, g, m, v, *extra)


_ANY = pl.BlockSpec(memory_space=pl.ANY)


def _mesh_pos():
    return lax.axis_index("x"), lax.axis_index("y"), lax.axis_index("c")


def _other_chips(x, y):
    return [(1 - x, y), (x, 1 - y), (1 - x, 1 - y)]


def _remote(src, dst, send_sem, recv_sem, to):
    return pltpu.make_async_remote_copy(src_ref=src, dst_ref=dst, send_sem=send_sem, recv_sem=recv_sem,
                                        device_id=to, device_id_type=_MESH)


_HBM = pl.BlockSpec(memory_space=pltpu.HBM)
_SEM = pl.BlockSpec(memory_space=pltpu.SEMAPHORE)
_EFFECT = pltpu.SideEffectType.DATAFLOW_SIDE_EFFECTING


def _in_hbm(a):
    return pltpu.with_memory_space_constraint(a, pltpu.HBM)


def _chip_index(chip):
    return 2 * chip[0] + chip[1]


def _ag_forward(name, shards, lands, layer, have_remote):
    n = len(shards)

    def body(*refs):
        ins = refs[:n]
        outs = refs[2 * n:3 * n] if lands is not None else refs[n:2 * n]
        send_sems, recv_sems = refs[-2:]
        x, y, c = _mesh_pos()
        sibling = (x, y, 1 - c)
        chips = _other_chips(x, y)

        def copy(a, k, src, dst, to):
            return _remote(src, dst, send_sems.at[a, k], recv_sems.at[a, k], to)

        own = [copy(a, 6, ins[a], outs[a].at[2 * x + y], sibling) for a in range(n)]
        for cp in own:
            cp.start()

        @pl.when(c == layer)
        def _():
            started = []
            if not have_remote:
                for k, chip in enumerate(chips):
                    for a in range(n):
                        cp = copy(a, k, ins[a], outs[a].at[2 * x + y], (*chip, c))
                        cp.start()
                        started.append(cp)
            for k, chip in enumerate(chips):
                for a in range(n):
                    landed = outs[a].at[_chip_index(chip)]
                    if not have_remote:
                        copy(a, k, ins[a], landed, (*chip, c)).wait_recv()
                    cp = copy(a, 3 + k, landed, landed, sibling)
                    cp.start()
                    started.append(cp)
            for cp in started:
                cp.wait_send()

        @pl.when(c != layer)
        def _():
            for k, chip in enumerate(chips):
                for a in range(n):
                    copy(a, 3 + k, ins[a], outs[a].at[_chip_index(chip)], sibling).wait_recv()

        for cp in own:
            cp.wait()

    out_shape = [jax.ShapeDtypeStruct((4,) + a.shape, a.dtype) for a in shards]
    extra = [] if lands is None else list(lands)
    return pl.pallas_call(
        body, name=name, in_specs=[_ANY] * (n + len(extra)), out_specs=[_ANY] * n,
        out_shape=out_shape, input_output_aliases={n + a: a for a in range(len(extra))},
        scratch_shapes=[pltpu.SemaphoreType.DMA((n, 7)), pltpu.SemaphoreType.DMA((n, 7))],
    )(*shards, *extra)


def _owner_sends(owners, srcs, dsts, send_sems, recv_sems, do):
    x, y, c = _mesh_pos()
    for core in (0, 1):
        mine = [a for a in range(len(srcs)) if owners[a] == core]
        if mine:
            @pl.when(c == core)
            def _(mine=mine):
                for k, chip in enumerate(_other_chips(x, y)):
                    for a in mine:
                        do(_remote(srcs[a](chip, k), dsts[a](chip, k), send_sems.at[3 * a + k],
                                   recv_sems.at[3 * a + k], (*chip, c)))


def _split_start(name, owners, sources, land_shapes, src_of, dst_of, after):
    n = len(sources)

    def body(*refs):
        srcs, lands = refs[:n], refs[n:2 * n]
        send_sems, recv_sems = refs[2 * n + 1], refs[2 * n + 2]
        token = refs[-1]
        _owner_sends(owners, [functools.partial(src_of, srcs[a]) for a in range(n)],
                     [functools.partial(dst_of, lands[a]) for a in range(n)], send_sems, recv_sems,
                     lambda cp: cp.start())
        token[...] = jnp.zeros_like(token)

    lands = [_in_hbm(lax.empty(s.shape, s.dtype)) for s in land_shapes]
    outs = pl.pallas_call(
        body, name=name,
        out_shape=([pltpu.SemaphoreType.DMA((3 * n,)), pltpu.SemaphoreType.DMA((3 * n,))]
                   + [pltpu.HBM(a.shape, a.dtype) for a in sources] + [pltpu.HBM(s.shape, s.dtype) for s in land_shapes]
                   + [jax.ShapeDtypeStruct((8, _LANES), F32)]),
        in_specs=[_HBM] * (2 * n) + [_ANY],
        out_specs=[_SEM, _SEM] + [_HBM] * (2 * n) + [pl.BlockSpec(memory_space=pltpu.VMEM)],
        input_output_aliases={i: 2 + i for i in range(2 * n)},
        compiler_params=pltpu.CompilerParams(has_side_effects=_EFFECT),
    )(*[_in_hbm(a) for a in sources], *lands, after)
    return outs[0], outs[1], outs[2:2 + n], outs[2 + n:2 + 2 * n], outs[-1]


def _split_wait(name, owners, send_sems, recv_sems, sources, lands, after, src_of, dst_of):
    n = len(sources)

    def body(*refs):
        srcs, lnds = refs[:n], refs[n:2 * n]
        s_sems, r_sems = refs[2 * n], refs[2 * n + 1]

        def wait(cp):
            cp.wait_send()
            cp.wait_recv()

        _owner_sends(owners, [functools.partial(src_of, srcs[a]) for a in range(n)],
                     [functools.partial(dst_of, lnds[a]) for a in range(n)], s_sems, r_sems, wait)

    outs = pl.pallas_call(
        body, name=name,
        out_shape=[pltpu.HBM(a.shape, a.dtype) for a in sources] + [pltpu.HBM(a.shape, a.dtype) for a in lands],
        in_specs=[_HBM] * (2 * n) + [_SEM, _SEM] + [_ANY] * len(after), out_specs=[_HBM] * (2 * n),
        input_output_aliases={i: i for i in range(2 * n)},
        compiler_params=pltpu.CompilerParams(has_side_effects=_EFFECT),
    )(*sources, *lands, send_sems, recv_sems, *after)
    return outs[:n], outs[n:]


def _pair_exchange(name, arrays, owners, to_owner, layer=None):
    n = len(arrays)

    def body(*refs):
        ins, outs, (send_sems, recv_sems) = refs[:n], refs[n:2 * n], refs[2 * n:]
        x, y, c = _mesh_pos()
        part = (lambda r: r) if layer is None else (lambda r: r.at[layer])
        copies = [_remote(part(ins[a]), part(outs[a]), send_sems.at[a], recv_sems.at[a], (x, y, 1 - c))
                  for a in range(n)]
        for core in (0, 1):
            sends = [copies[a] for a in range(n) if (owners[a] != core) == to_owner]
            recvs = [copies[a] for a in range(n) if (owners[a] == core) == to_owner]

            @pl.when(c == core)
            def _(sends=sends, recvs=recvs):
                for cp in sends:
                    cp.start()
                for cp in recvs:
                    cp.wait_recv()
                for cp in sends:
                    cp.wait_send()

    return pl.pallas_call(
        body, name=name, in_specs=[_ANY] * n, out_specs=[_ANY] * n,
        out_shape=[jax.ShapeDtypeStruct(g.shape, g.dtype) for g in arrays],
        input_output_aliases={} if to_owner else {a: a for a in range(n)},
        scratch_shapes=[pltpu.SemaphoreType.DMA((n,)), pltpu.SemaphoreType.DMA((n,))],
    )(*arrays)


def _swap_split(name, arrays, lands, owners, sems, after):
    n = len(arrays)
    starting = sems is None

    def body(*refs):
        ins, lnds = refs[:n], refs[n:2 * n]
        send_sems, recv_sems = (refs[2 * n + 1], refs[2 * n + 2]) if starting else (refs[2 * n], refs[2 * n + 1])
        x, y, c = _mesh_pos()
        copies = [_remote(ins[a], lnds[a], send_sems.at[a], recv_sems.at[a], (x, y, 1 - c)) for a in range(n)]
        for core in (0, 1):
            sends = [copies[a] for a in range(n) if owners[a] != core]
            recvs = [copies[a] for a in range(n) if owners[a] == core]

            @pl.when(c == core)
            def _(sends=sends, recvs=recvs):
                if starting:
                    for cp in sends:
                        cp.start()
                else:
                    for cp in recvs:
                        cp.wait_recv()
                    for cp in sends:
                        cp.wait_send()

        if starting:
            refs[-1][...] = jnp.zeros_like(refs[-1])

    hbm = lambda arrs: [pltpu.HBM(a.shape, a.dtype) for a in arrs]
    if starting:
        zones = [_in_hbm(lax.empty(s.shape, s.dtype)) for s in lands]
        outs = pl.pallas_call(
            body, name=name,
            out_shape=([pltpu.SemaphoreType.DMA((n,)), pltpu.SemaphoreType.DMA((n,))] + hbm(arrays) + hbm(lands)
                       + [jax.ShapeDtypeStruct((8, _LANES), F32)]),
            in_specs=[_HBM] * (2 * n) + [_ANY],
            out_specs=[_SEM, _SEM] + [_HBM] * (2 * n) + [pl.BlockSpec(memory_space=pltpu.VMEM)],
            input_output_aliases={i: 2 + i for i in range(2 * n)},
            compiler_params=pltpu.CompilerParams(has_side_effects=_EFFECT),
        )(*[_in_hbm(a) for a in arrays], *zones, after)
        return outs[0], outs[1], outs[2:2 + n], outs[2 + n:2 + 2 * n], outs[-1]
    outs = pl.pallas_call(
        body, name=name, out_shape=hbm(arrays) + hbm(lands),
        in_specs=[_HBM] * (2 * n) + [_SEM, _SEM] + [_ANY] * len(after), out_specs=[_HBM] * (2 * n),
        input_output_aliases={i: i for i in range(2 * n)},
        compiler_params=pltpu.CompilerParams(has_side_effects=_EFFECT),
    )(*arrays, *lands, *sems, *after)
    return outs[:n], outs[n:]


def _add_pair(name, g, a, flags):
    _, k, n = g.shape
    tk, tn = _slab_block(k, n)

    def body(flags_ref, g_ref, a_ref, o_ref):
        o_ref[...] = (g_ref[...] + a_ref[...]).astype(o_ref.dtype)

    spec = pl.BlockSpec((1, tk, tn), lambda j, i, jn, fl: (j * fl[1], i * fl[1], jn * fl[1]))
    return pl.pallas_call(
        body, name=name,
        grid_spec=pltpu.PrefetchScalarGridSpec(num_scalar_prefetch=1, grid=(4, k // tk, n // tn),
                                               in_specs=[spec, spec], out_specs=spec),
        out_shape=jax.ShapeDtypeStruct(g.shape, _RS_DT),
        compiler_params=_cparams(("arbitrary", "arbitrary", "arbitrary")),
    )(flags, g, a)


def _add_quads(name, t, b, layer, flags, into, after=None):
    _, k, n = t.shape
    tk, tn = _slab_block(k, n)

    def body(flags_ref, t_ref, b_ref, *rest):
        o_ref = rest[-1]
        f = lambda v: v.astype(F32)
        o_ref[0] = ((f(t_ref[0]) + f(b_ref[0])) + f(b_ref[1])) + f(b_ref[2])

    extra = ([] if into is None else [into]) + ([] if after is None else [after])
    return pl.pallas_call(
        body, name=name,
        grid_spec=pltpu.PrefetchScalarGridSpec(
            num_scalar_prefetch=1, grid=(k // tk, n // tn),
            in_specs=[pl.BlockSpec((1, tk, tn), lambda i, jn, fl: (fl[0], i * fl[1], jn * fl[1])),
                      pl.BlockSpec((3, tk, tn), lambda i, jn, fl: (0, i * fl[1], jn * fl[1]))] + [_ANY] * len(extra),
            out_specs=pl.BlockSpec((1, tk, tn), lambda i, jn, fl: (layer, i * fl[1], jn * fl[1]))),
        out_shape=jax.ShapeDtypeStruct((2, k, n), F32),
        input_output_aliases={} if into is None else {3: 0},
        compiler_params=_cparams(("arbitrary", "arbitrary")),
    )(flags, t, b, *extra)


def _slab_block(k, n, itemsize=4):
    tk = (1 << 20) // (n * itemsize) // 16 * 16
    while 0 < tk < k and k % tk:
        tk -= 16
    if 0 < tk < k:
        return tk, n
    if k * n * itemsize <= (2 << 20) or n % _LANES:
        return k, n
    tn = max(_LANES, (2 << 20) // (k * itemsize) // _LANES * _LANES)
    while n % tn:
        tn -= _LANES
    return k, tn


def _all_reduce_adamw(gs, ws, ms, vs):
    n = len(gs)

    def body(*refs):
        g_refs, w_refs, m_refs, v_refs = (refs[i * n:(i + 1) * n] for i in range(4))
        gsum, delta, m_out, v_out = (refs[(4 + i) * n:(5 + i) * n] for i in range(4))
        pair, chips = refs[8 * n:9 * n], refs[9 * n:10 * n]
        send_sems, recv_sems = refs[10 * n:]
        x, y, c = _mesh_pos()
        my_chip = 2 * x + y
        to_sibling = [_remote(g_refs[a], pair[a].at[c], send_sems.at[a, 0], recv_sems.at[a, 0], (x, y, 1 - c))
                      for a in range(n)]
        for cp in to_sibling:
            cp.start()
        for a in range(n):
            pair[a][c] = g_refs[a][...]
        copies = []
        for a in range(n):
            to_sibling[a].wait()
            chips[a][my_chip] = pair[a][0] + pair[a][1]
            for k, chip in enumerate(_other_chips(x, y)):
                mine = chips[a].at[my_chip]
                cp = _remote(mine, mine, send_sems.at[a, 1 + k], recv_sems.at[a, 1 + k], (*chip, c))
                cp.start()
                copies.append(cp)
        for cp in copies:
            cp.wait()
        for a in range(n):
            acc = ((chips[a][0] + chips[a][1]) + chips[a][2]) + chips[a][3]
            gsum[a][...] = acc
            delta[a][...], m_out[a][...], v_out[a][...] = _adamw_math(w_refs[a][...], acc, m_refs[a][...],
                                                                      v_refs[a][...])

    vmem = pl.BlockSpec(memory_space=pltpu.VMEM)
    outs = pl.pallas_call(
        body, name="all_reduce_adamw", in_specs=[vmem] * (4 * n), out_specs=[vmem] * (4 * n),
        out_shape=[jax.ShapeDtypeStruct(g.shape, F32) for g in gs] * 4,
        scratch_shapes=([pltpu.VMEM((2,) + g.shape, F32) for g in gs] + [pltpu.VMEM((4,) + g.shape, F32) for g in gs]
                        + [pltpu.SemaphoreType.DMA((n, 4)), pltpu.SemaphoreType.DMA((n, 4))]),
        compiler_params=pltpu.CompilerParams(vmem_limit_bytes=_VMEM_LIMIT),
    )(*gs, *ws, *ms, *vs)
    return outs[:n], outs[n:2 * n], outs[2 * n:3 * n], outs[3 * n:]


def _swap_rope(a):
    h = QK_ROPE // 2
    return jnp.concatenate([a[..., h:], a[..., :h]], axis=-1)


def _swap_rope_rows(a):
    h = QK_ROPE // 2
    return jnp.concatenate([a[h:], a[:h]], axis=0)


_FFN_CB = 256


def _interleave_rows(a, cb):
    r, c = a.shape
    return a.reshape(2, r // (2 * cb), cb, c).transpose(1, 0, 2, 3).reshape(r, c)


def _deinterleave_rows(a, cb):
    r, c = a.shape
    return a.reshape(r // (2 * cb), 2, cb, c).transpose(1, 0, 2, 3).reshape(r, c)


class _InLayout:
    def __init__(self, d):
        self.d = d
        self.gates = 0
        self.a = 3 * d
        self.b = 4 * d
        self.kv = 5 * d
        self.q = self.kv + 256
        self.krm = self.q + 384
        self.krs = self.krm + HEAD_PAD
        self.width = self.krs + 2 * HEAD_PAD


def _prep_layer(wl, d):
    lay = _InLayout(d)
    w_in = wl["w_in"]
    dt = w_in.dtype
    a, b = w_in[0:d], w_in[d:2 * d]
    q, kv = w_in[2 * d:2 * d + 384], w_in[2 * d + 384:2 * d + 640]
    kr = w_in[2 * d + 640:2 * d + 640 + QK_ROPE]
    gates = w_in[2 * d + 640 + QK_ROPE:]
    z = lambda n: jnp.zeros((n, d), dt)
    krm = jnp.concatenate([z(QK_NOPE), kr, z(HEAD_PAD - QK_NOPE - QK_ROPE)], axis=0)
    krs = jnp.concatenate([z(QK_NOPE), _swap_rope_rows(kr), z(HEAD_PAD - QK_NOPE - QK_ROPE)], axis=0)
    out = dict(wl)
    out["w_in"] = jnp.concatenate([gates, _interleave_rows(a, _LANES), b, kv, q, krm, krs,
                                   z(lay.width - lay.krs - HEAD_PAD)], axis=0)
    uq = wl["mla_w_uq"].reshape(-1, N_HEADS, QK_NOPE + QK_ROPE)
    nq = uq.shape[0]
    nope, pe = uq[..., :QK_NOPE], uq[..., QK_NOPE:]
    zq = lambda n: jnp.zeros((nq, N_HEADS, n), dt)
    main = jnp.concatenate([nope, pe, zq(HEAD_PAD - QK_NOPE - QK_ROPE)], axis=-1).reshape(nq, -1)
    swapped = jnp.concatenate([zq(QK_NOPE), _swap_rope(pe), zq(HEAD_PAD - QK_NOPE - QK_ROPE)], axis=-1).reshape(nq, -1)
    out["mla_w_uq"] = jnp.concatenate([main, swapped], axis=1)
    ukv = wl["mla_w_ukv"].reshape(-1, N_HEADS, QK_NOPE + V_HEAD)
    nkv = ukv.shape[0]
    zk = jnp.zeros((nkv, N_HEADS, HEAD_PAD - QK_NOPE), dt)
    zv = jnp.zeros((nkv, N_HEADS, HEAD_PAD - V_HEAD), dt)
    out["mla_w_ukv"] = jnp.concatenate([jnp.concatenate([ukv[..., :QK_NOPE], zk], axis=-1).reshape(nkv, -1),
                                        jnp.concatenate([ukv[..., QK_NOPE:], zv], axis=-1).reshape(nkv, -1)], axis=1)
    wo = wl["mla_w_o"].reshape(N_HEADS, V_HEAD, -1)
    out["mla_w_o"] = jnp.concatenate([wo, jnp.zeros((N_HEADS, HEAD_PAD - V_HEAD, wo.shape[-1]), dt)],
                                     axis=1).reshape(N_HEADS * HEAD_PAD, -1)
    return out


def _unprep_grads(g, d):
    lay = _InLayout(d)
    gi = g["w_in"]
    kr = (gi[lay.krm + QK_NOPE:lay.krm + QK_NOPE + QK_ROPE]
          + _swap_rope_rows(gi[lay.krs + QK_NOPE:lay.krs + QK_NOPE + QK_ROPE]))
    out = dict(g)
    out["w_in"] = jnp.concatenate([_deinterleave_rows(gi[lay.a:lay.a + d], _LANES), gi[lay.b:lay.b + d],
                                   gi[lay.q:lay.q + 384], gi[lay.kv:lay.kv + 256], kr,
                                   gi[lay.gates:lay.gates + 3 * d]], axis=0)
    hw = N_HEADS * HEAD_PAD
    gq = g["mla_w_uq"]
    nq = gq.shape[0]
    main = gq[:, :hw].reshape(nq, N_HEADS, HEAD_PAD)
    swapped = gq[:, hw:].reshape(nq, N_HEADS, HEAD_PAD)
    pe = main[..., QK_NOPE:QK_NOPE + QK_ROPE] + _swap_rope(swapped[..., QK_NOPE:QK_NOPE + QK_ROPE])
    out["mla_w_uq"] = jnp.concatenate([main[..., :QK_NOPE], pe], axis=-1).reshape(nq, -1)
    gkv = g["mla_w_ukv"]
    nkv = gkv.shape[0]
    out["mla_w_ukv"] = jnp.concatenate([gkv[:, :hw].reshape(nkv, N_HEADS, HEAD_PAD)[..., :QK_NOPE],
                                        gkv[:, hw:].reshape(nkv, N_HEADS, HEAD_PAD)[..., :V_HEAD]],
                                       axis=-1).reshape(nkv, -1)
    go = g["mla_w_o"]
    out["mla_w_o"] = go.reshape(N_HEADS, HEAD_PAD, -1)[:, :V_HEAD].reshape(N_HEADS * V_HEAD, -1)
    return out


def _rope_tables(positions):
    s = positions.shape[0]
    inv = ROPE_THETA ** (-jnp.arange(0, QK_ROPE, 2, dtype=F32) / QK_ROPE)
    ang = positions.astype(F32)[:, None] * inv
    cos, sin = jnp.cos(ang), jnp.sin(ang)
    tail = jnp.zeros((s, HEAD_PAD - QK_NOPE - QK_ROPE), F32)
    tc = jnp.concatenate([jnp.ones((s, QK_NOPE), F32), cos, cos, tail], axis=1)
    ts = jnp.concatenate([jnp.zeros((s, QK_NOPE), F32), -sin, sin, tail], axis=1)
    return tc, ts


def _row(v):
    return v.reshape(1, -1)


def _layer_fwd(x, h, w, g_next, tc, ts, late_weights=None):
    d = x.shape[1]
    lay = _InLayout(d)
    cw = d // 2
    blk = lambda off, width: off // width
    p = _mm("mm_in", h, w["w_in"], tb=True)
    z1 = _glu_conv_fwd(p, blk(lay.a, 2 * _LANES), w["conv_dw_w"], _row(w["conv_dw_b"]))
    ln_a = [_row(w["conv_ln_g"]), _row(w["conv_ln_b"])]
    (z3,) = _row_fwd("ln_silu_fwd", _f_ln_silu, [(z1, cw, 0)], ln_a, [(cw, _MXU_DT)])
    ya = _mm("mm_conv_out", z3, w["conv_out_w"])
    ln_b = [_row(w["sg_ln_g"]), _row(w["sg_ln_b"])]
    u, vn = _row_fwd("sg_pre_fwd", _f_sg_pre, [(p, cw, blk(lay.b, cw)), (p, cw, blk(lay.b + cw, cw))], ln_b,
                     [(cw, F32), (cw, _MXU_DT)])
    bcol = w["sg_b"].reshape(SG_GROUPS, SG_CHUNK, 1)
    ub = _sg_mix_fwd(u, vn, w["sg_w"], bcol)
    yb = _mm("mm_sg_out", ub, w["sg_out_w"])
    (qn,) = _row_fwd("q_norm_fwd", _f_rms, [(p, 384, blk(lay.q, 384))], [_row(w["mla_q_norm_g"])], [(384, _MXU_DT)])
    (kvn,) = _row_fwd("kv_norm_fwd", _f_rms, [(p, 256, blk(lay.kv, 256))], [_row(w["mla_kv_norm_g"])],
                      [(256, _MXU_DT)])
    q2 = _mm("mm_uq", qn, w["mla_w_uq"])
    kv2 = _mm("mm_ukv", kvn, w["mla_w_ukv"])
    qf, kf, vf = _rope_fwd(q2, kv2, p, blk(lay.krm, HEAD_PAD), blk(lay.krs, HEAD_PAD), tc, ts)
    o = _attn_fwd(qf, kf, vf)
    yc = _mm("mm_o", o, w["mla_w_o"])
    gate_rows = [(p, d, 0), (p, d, 1), (p, d, 2)]
    (merged,) = _row_fwd("merge_fwd", _f_merge, gate_rows + [(ya, d, 0), (yb, d, 0), (yc, d, 0)], [], [(d, _MXU_DT)])
    if late_weights is not None:
        w = {**w, **late_weights(merged)}
    t = _mm("mm_out", merged, w["w_out"])
    x1, h2 = _row_fwd("resid_mix_fwd", _f_resid_rms_rms, [(x, d, 0), (t, d, 0)],
                      [_row(w["mix_post_g"]), _row(w["ffn_pre_g"])], [(d, F32), (d, _MXU_DT)])
    up = _mm("mm_up", h2, w["ffn_w_up"])
    act = _conv_geglu_fwd(up, w["ffn_dw_w"], _row(w["ffn_dw_b"]))
    dn = _mm("mm_down", act, w["ffn_w_down"])
    if g_next is None:
        (x2,) = _row_fwd("resid_ffn_last_fwd", _f_resid_rms, [(x1, d, 0), (dn, d, 0)], [_row(w["ffn_post_g"])],
                         [(d, F32)])
        h_next = None
    else:
        x2, h_next = _row_fwd("resid_ffn_fwd", _f_resid_rms_rms, [(x1, d, 0), (dn, d, 0)],
                              [_row(w["ffn_post_g"]), _row(g_next)], [(d, F32), (d, _MXU_DT)])
    saved = dict(x=x, h=h, p=p, z1=z1, z3=z3, ya=ya, u=u, vn=vn, ub=ub, yb=yb, qn=qn, kvn=kvn, qf=qf, kf=kf, vf=vf, o=o,
                 yc=yc, merged=merged, t=t, x1=x1, h2=h2, up=up, act=act, dn=dn, bcol=bcol)
    return x2, h_next, saved


def _layer_bwd(dx2, dh_next, w, g_next, sv, tc, ts, on_late_grads=None, on_attention_done=None):
    d = dx2.shape[1]
    lay = _InLayout(d)
    cw = d // 2
    blk = lambda off, width: off // width
    lo = _MXU_DT
    g = {}
    x1, dn = sv["x1"], sv["dn"]
    if dh_next is None:
        dx1, ddn, g["ffn_post_g"] = _row_bwd("resid_ffn_last_bwd", _f_resid_rms, [(x1, d, 0), (dn, d, 0)],
                                             [_row(w["ffn_post_g"])], [(dx2, d, 0)], [(0, F32), (1, lo)], [0])
    else:
        dx1, ddn, g["ffn_post_g"], g["next_pre_g"] = _row_bwd(
            "resid_ffn_bwd", _f_resid_rms_rms, [(x1, d, 0), (dn, d, 0)], [_row(w["ffn_post_g"]), _row(g_next)],
            [(dx2, d, 0), (dh_next, d, 0)], [(0, F32), (1, lo)], [0, 1])
    dact = _mm("mm_down_dx", ddn, w["ffn_w_down"], tb=True)
    g["ffn_w_down"] = _mm("mm_down_dw", sv["act"], ddn, ta=True)
    dup, dw_halves, db_halves = _conv_geglu_bwd(sv["up"], w["ffn_dw_w"], _row(w["ffn_dw_b"]), dact)
    g["ffn_dw_w"] = jnp.concatenate([dw_halves[0], dw_halves[1]], axis=1)
    g["ffn_dw_b"] = jnp.concatenate([db_halves[0], db_halves[1]], axis=1)
    dh2 = _mm("mm_up_dx", dup, w["ffn_w_up"], tb=True, a_halves=True)
    g["ffn_w_up"] = _mm("mm_up_dw", sv["h2"], dup, ta=True, b_halves=True, out_quarters=True)
    dx, dt, g["mix_post_g"], g["ffn_pre_g"] = _row_bwd(
        "resid_mix_bwd", _f_resid_rms_rms, [(sv["x"], d, 0), (sv["t"], d, 0)],
        [_row(w["mix_post_g"]), _row(w["ffn_pre_g"])], [(dx1, d, 0), (dh2, d, 0)], [(0, F32), (1, lo)], [0, 1])
    dmerged = _mm("mm_out_dx", dt, w["w_out"], tb=True)
    g["w_out"] = _mm("mm_out_dw", sv["merged"], dt, ta=True)
    if on_late_grads is not None:
        dmerged = on_late_grads(g, dmerged)
    p = sv["p"]
    gate_rows = [(p, d, 0), (p, d, 1), (p, d, 2)]
    dp, dya, dyb, dyc = _row_bwd(
        "merge_bwd", _f_merge, gate_rows + [(sv["ya"], d, 0), (sv["yb"], d, 0), (sv["yc"], d, 0)], [],
        [(dmerged, d, 0)], [((0, 1, 2), lo), ((3,), lo), ((4,), lo), ((5,), lo)], [], place=(lay.width, 0))
    do = _mm("mm_o_dx", dyc, w["mla_w_o"], tb=True, out_dtype=lo)
    g["mla_w_o"] = _mm("mm_o_dw", sv["o"], dyc, ta=True)
    dqf, dkf, dvf = _attn_bwd(sv["qf"], sv["kf"], sv["vf"], do)
    dq2, dkv2, dp = _rope_bwd(dqf, dkf, dvf, tc, ts, dp, blk(lay.krm, 3 * HEAD_PAD))
    dkvn = _mm("mm_ukv_dx", dkv2, w["mla_w_ukv"], tb=True)
    g["mla_w_ukv"] = _mm("mm_ukv_dw", sv["kvn"], dkv2, ta=True)
    dqn = _mm("mm_uq_dx", dq2, w["mla_w_uq"], tb=True)
    g["mla_w_uq"] = _mm("mm_uq_dw", sv["qn"], dq2, ta=True)
    dp, g["mla_q_norm_g"] = _row_bwd("q_norm_bwd", _f_rms, [(p, 384, blk(lay.q, 384))], [_row(w["mla_q_norm_g"])],
                                     [(dqn, 384, 0)], [((0,), lo)], [0], place=(lay.width, blk(lay.q, 384)), into=dp)
    dp, g["mla_kv_norm_g"] = _row_bwd("kv_norm_bwd", _f_rms, [(p, 256, blk(lay.kv, 256))],
                                      [_row(w["mla_kv_norm_g"])], [(dkvn, 256, 0)], [((0,), lo)], [0],
                                      place=(lay.width, blk(lay.kv, 256)), into=dp)
    if on_attention_done is not None:
        dyb = on_attention_done(dyb)
    dub = _mm("mm_sg_out_dx", dyb, w["sg_out_w"], tb=True)
    g["sg_out_w"] = _mm("mm_sg_out_dw", sv["ub"], dyb, ta=True)
    du, dvn, g["sg_w"], dbcol = _sg_mix_bwd(sv["u"], sv["vn"], w["sg_w"], sv["bcol"], dub)
    g["sg_b"] = dbcol.reshape(SG_GROUPS, SG_CHUNK)
    dp, g["sg_ln_g"], g["sg_ln_b"] = _row_bwd(
        "sg_pre_bwd", _f_sg_pre, [(p, cw, blk(lay.b, cw)), (p, cw, blk(lay.b + cw, cw))],
        [_row(w["sg_ln_g"]), _row(w["sg_ln_b"])], [(du, cw, 0), (dvn, cw, 0)], [((0, 1), lo)], [0, 1],
        place=(lay.width, blk(lay.b, d)), into=dp)
    dz3 = _mm("mm_conv_out_dx", dya, w["conv_out_w"], tb=True)
    g["conv_out_w"] = _mm("mm_conv_out_dw", sv["z3"], dya, ta=True)
    dz1, g["conv_ln_g"], g["conv_ln_b"] = _row_bwd(
        "ln_silu_bwd", _f_ln_silu, [(sv["z1"], cw, 0)], [_row(w["conv_ln_g"]), _row(w["conv_ln_b"])],
        [(dz3, cw, 0)], [((0,), F32)], [0, 1])
    dp, g["conv_dw_w"], g["conv_dw_b"] = _glu_conv_bwd(p, blk(lay.a, 2 * _LANES), w["conv_dw_w"], dz1, dp)
    dh = _mm("mm_in_dx", dp, w["w_in"])
    g["w_in"] = _mm("mm_in_dw", dp, sv["h"], ta=True)
    return dx, dh, g


def _local_step(x, positions, target, layers):
    d = x.shape[1]
    tc, ts = _rope_tables(positions)
    ws = [_prep_layer(wl, d) for wl in layers]
    depth = len(ws)
    (h,) = _row_fwd("rms_first_fwd", _f_rms, [(x, d, 0)], [_row(ws[0]["mix_pre_g"])], [(d, _MXU_DT)])
    saved = []
    for l in range(depth):
        g_next = ws[l + 1]["mix_pre_g"] if l + 1 < depth else None
        x, h, sv = _layer_fwd(x, h, ws[l], g_next, tc, ts)
        saved.append(sv)
    loss, dx = _loss_head(x, target)
    grads = [None] * depth
    dh = None
    for l in reversed(range(depth)):
        g_next = ws[l + 1]["mix_pre_g"] if l + 1 < depth else None
        dx, dh, g = _layer_bwd(dx, dh, ws[l], g_next, saved[l], tc, ts)
        if "next_pre_g" in g:
            grads[l + 1]["mix_pre_g"] = g.pop("next_pre_g")
        grads[l] = g
    x0 = saved[0]["x"]
    grad_x, grads[0]["mix_pre_g"] = _row_bwd("rms_first_bwd", _f_x_rms, [(x0, d, 0)], [_row(ws[0]["mix_pre_g"])],
                                             [(dx, d, 0), (dh, d, 0)], [(0, F32)], [0])
    return loss, grad_x, [_unprep_grads(g, d) for g in grads]


_MATRICES = ("w_in", "conv_out_w", "sg_out_w", "mla_w_uq", "mla_w_ukv", "mla_w_o", "w_out", "ffn_w_up", "ffn_w_down")
_F32_GATHERED = ("conv_dw_w", "ffn_dw_w")
_RS_DT = jnp.bfloat16


_ROW_SHARDED = SHARDED_MID + ("w_in",)


_GATHERED = SHARDED + _F32_GATHERED
_RS_CORE0 = ("w_in", "ffn_w_down")
_LATE_WEIGHTS = ("w_out", "ffn_w_up", "ffn_dw_w", "ffn_w_down")


def _layer_shards(w, l):
    hi = {n: w[n][l].astype(jnp.bfloat16) for n in SHARDED}
    lo = [(w[n][l] - hi[n].astype(F32)).astype(jnp.bfloat16) for n in _F32_GATHERED]
    return [hi[n] for n in SHARDED] + lo


def _layer_weights(names, gathered):
    wl = {}
    for n, g in zip(names, gathered):
        if n in _ROW_SHARDED and g.shape[1] % 16 == 0:
            whole = g.reshape(-1, g.shape[2])
        else:
            whole = jnp.concatenate([g[j] for j in range(4)], axis=0 if n in _ROW_SHARDED else 1)
        if n in wl:
            wl[n] = wl[n].astype(F32) + whole.astype(F32)
        else:
            wl[n] = whole.astype(_MXU_DT) if n in _MATRICES else whole
    return wl


def _by_destination(name, gl):
    if gl.ndim == 3:
        return gl
    k, n = gl.shape
    if name in _ROW_SHARDED:
        return gl.reshape(4, k // 4, n)
    return gl.reshape(k, 4, n // 4).transpose(1, 0, 2)


def kernel(x, positions, mix_pre_g, mix_post_g, ffn_pre_g, ffn_post_g, w_in, conv_dw_w, conv_dw_b, conv_ln_g, conv_ln_b, conv_out_w, sg_ln_g, sg_ln_b, sg_w, sg_b, sg_out_w, mla_q_norm_g, mla_w_uq, mla_kv_norm_g, mla_w_ukv, mla_w_o, w_out, ffn_w_up, ffn_dw_w, ffn_dw_b, ffn_w_down, loss_target, m_mix_pre_g, m_mix_post_g, m_ffn_pre_g, m_ffn_post_g, m_w_in, m_conv_dw_w, m_conv_dw_b, m_conv_ln_g, m_conv_ln_b, m_conv_out_w, m_sg_ln_g, m_sg_ln_b, m_sg_w, m_sg_b, m_sg_out_w, m_mla_q_norm_g, m_mla_w_uq, m_mla_kv_norm_g, m_mla_w_ukv, m_mla_w_o, m_w_out, m_ffn_w_up, m_ffn_dw_w, m_ffn_dw_b, m_ffn_w_down, v_mix_pre_g, v_mix_post_g, v_ffn_pre_g, v_ffn_post_g, v_w_in, v_conv_dw_w, v_conv_dw_b, v_conv_ln_g, v_conv_ln_b, v_conv_out_w, v_sg_ln_g, v_sg_ln_b, v_sg_w, v_sg_b, v_sg_out_w, v_mla_q_norm_g, v_mla_w_uq, v_mla_kv_norm_g, v_mla_w_ukv, v_mla_w_o, v_w_out, v_ffn_w_up, v_ffn_dw_w, v_ffn_dw_b, v_ffn_w_down):
    args = dict(locals())
    w = {n: args[n] for n in WEIGHTS}
    m = {n: args["m_" + n] for n in WEIGHTS}
    v = {n: args["v_" + n] for n in WEIGHTS}
    depth = mix_pre_g.shape[0]

    assert depth == 2, "the two cores of a chip split the communication by layer"
    for t in (w, m, v):
        t["w_in"] = jnp.swapaxes(t["w_in"], 1, 2)
    d = x.shape[-1]
    mesh_x, mesh_y, mesh_c = _mesh_pos()
    my_chip = 2 * mesh_x + mesh_y
    names = list(SHARDED)
    whole = lambda ref, chip, k: ref
    to_my_slot = lambda ref, chip, k: ref.at[2 * lax.axis_index("x") + lax.axis_index("y")]
    block_of_chip = lambda ref, chip, k: ref.at[_chip_index(chip)]
    slot_k = lambda ref, chip, k: ref.at[k]

    late = [i for i, n in enumerate(_GATHERED) if n in _LATE_WEIGHTS]
    early = [i for i in range(len(_GATHERED)) if i not in late]
    pick = lambda seq, idx: [seq[i] for i in idx]
    gathered_names = list(_GATHERED)
    shards0, shards1 = _layer_shards(w, 0), _layer_shards(w, 1)
    replicated = lambda l: {n: w[n][l] for n in REPLICATED}
    land_of = lambda shards: [jax.ShapeDtypeStruct((4,) + a.shape, a.dtype) for a in shards]

    gathered0 = _ag_forward("ag_layer0_early", pick(shards0, early), None, 0, False)
    late0 = pick(shards0, late)
    sems0_s, sems0_r, late0, lands0, token0 = _split_start("ag0_start", [0] * len(late), late0, land_of(late0), whole,
                                                           to_my_slot, gathered0[0])
    ag_owner = [1] * len(shards1)
    sems_s, sems_r, shards1, lands1, token = _split_start("ag1_start", ag_owner, shards1, land_of(shards1), whole,
                                                          to_my_slot, gathered0[0])
    tc, ts = _rope_tables(positions[0])
    ws0 = _prep_layer({**replicated(0), **_layer_weights(pick(gathered_names, early), gathered0)}, d)

    def late_weights(merged):
        got = _split_wait("ag0_wait", [0] * len(late), sems0_s, sems0_r, late0, lands0, [merged], whole, to_my_slot)
        got = _ag_forward("ag_layer0_late", got[0], got[1], 0, True)
        ws0.update(_layer_weights(pick(gathered_names, late), got))
        return ws0

    x0 = x[0] + (token0[0, 0] + token[0, 0])
    (h0,) = _row_fwd("rms_first_fwd", _f_rms, [(x0, d, 0)], [_row(ws0["mix_pre_g"])], [(d, _MXU_DT)])
    x1, h1, sv0 = _layer_fwd(x0, h0, ws0, w["mix_pre_g"][1], tc, ts, late_weights)
    shards1, lands1 = _split_wait("ag1_wait", ag_owner, sems_s, sems_r, shards1, lands1, [x1], whole, to_my_slot)
    gathered1 = _ag_forward("ag_layer1", shards1, lands1, 1, True)
    ws1 = _prep_layer({**replicated(1), **_layer_weights(gathered_names, gathered1)}, d)
    x2, _, sv1 = _layer_fwd(x1, h1, ws1, None, tc, ts)
    loss, dx = _loss_head(x2, loss_target[0])
    loss = jnp.broadcast_to(loss, (8, _LANES))

    owner = {n: 0 if n in _RS_CORE0 else 1 for n in names}
    flag = {n: jnp.stack([my_chip, (mesh_c == owner[n]).astype(jnp.int32)]).astype(jnp.int32) for n in names}
    owners = [owner[n] for n in names]

    def scatter_start(tag, group, gd, got, after):
        own = [owner[n] for n in group]
        t = [_add_pair("rs_pair%s_%s" % (tag, n), g, a, flag[n]) for n, g, a in zip(group, gd, got)]
        lands = [jax.ShapeDtypeStruct((3,) + a.shape[1:], a.dtype) for a in t]
        sems_s, sems_r, t, b, token = _split_start("rs%s_start" % tag, own, t, lands, block_of_chip, slot_k, after)
        return (tag, group, own, sems_s, sems_r, t, b), token

    def scatter_wait(handle, after):
        tag, group, own, sems_s, sems_r, t, b = handle
        t, b = _split_wait("rs%s_wait" % tag, own, sems_s, sems_r, t, b, after, block_of_chip, slot_k)
        return dict(zip(group, zip(t, b)))

    def swap_and_start(tag, group, grads_l, after):
        gd = [_by_destination(n, grads_l[n]) for n in group]
        got = _pair_exchange("rs_swap" + tag, gd, [owner[n] for n in group], True)
        return scatter_start(tag, group, gd, got, after)

    dx, dh, gk1 = _layer_bwd(dx, None, ws1, None, sv1, tc, ts)
    grads1 = _unprep_grads(gk1, d)
    gd1 = [_by_destination(n, grads1[n]) for n in names]
    swap1 = _swap_split("rs_swap1_start", gd1, [jax.ShapeDtypeStruct(g.shape, g.dtype) for g in gd1], owners, None,
                        dh)
    dx = dx + swap1[4][0, 0]
    late_group = [n for n in names if n in _LATE_WEIGHTS]
    early_group = [n for n in names if n not in _LATE_WEIGHTS]
    handles = []

    own_late = [owner[n] for n in late_group]
    pending = []

    def on_late_grads(g, value):
        gd, got = _swap_split("rs_swap1_wait", swap1[2], swap1[3], owners, swap1[:2], [value])
        handle1, tok1 = scatter_start("1", names, gd, got, value)
        handles.append(handle1)
        gd = [_by_destination(n, g[n]) for n in late_group]
        swap = _swap_split("rs_swap0_late_start", gd, [jax.ShapeDtypeStruct(a.shape, a.dtype) for a in gd], own_late,
                           None, value)
        pending.append(swap)
        return value + (tok1[0, 0] + swap[4][0, 0])

    def on_attention_done(value):
        swap = pending[0]
        gd, got = _swap_split("rs_swap0_late_wait", swap[2], swap[3], own_late, swap[:2], [value])
        handle0, tok0 = scatter_start("0_late", late_group, gd, got, value)
        handles.append(handle0)
        return value + tok0[0, 0].astype(value.dtype)

    dx, dh, gk0 = _layer_bwd(dx, dh, ws0, ws1["mix_pre_g"], sv0, tc, ts, on_late_grads, on_attention_done)
    grads1["mix_pre_g"] = gk0.pop("next_pre_g")
    grad_x, gk0["mix_pre_g"] = _row_bwd("rms_first_bwd", _f_x_rms, [(x0, d, 0)], [_row(ws0["mix_pre_g"])],
                                        [(dx, d, 0), (dh, d, 0)], [(0, F32)], [0])
    tb1 = scatter_wait(handles[0], [grad_x])
    grads0 = _unprep_grads(gk0, d)
    handle, token = swap_and_start("0_early", early_group, grads0, grad_x)
    handles.append(handle)

    def finish(l, sums, updates):
        sums = _pair_exchange("rs_join%d" % l, sums, owners, False, layer=l)
        updates = [_adamw("adamw%d_%s" % (l, n), w[n], sums[i], m[n], v[n], l, updates[i])
                   for i, n in enumerate(names)]
        return sums, updates

    sums = [_add_quads("rs_sum1_" + n, *tb1[n], 1, flag[n], None, token) for n in names]
    sums, updates = finish(1, sums, [None] * len(names))
    out = {}
    rep = list(REPLICATED)
    grads = [grads0, grads1]
    g_rep = [jnp.stack([grads[l][n].reshape(w[n].shape[1:]) for l in range(depth)]) for n in rep]
    one = [jnp.ones_like(loss)]
    reduced = _all_reduce_adamw(g_rep + [loss], [w[n] for n in rep] + one, [m[n] for n in rep] + one,
                                [v[n] for n in rep] + one)
    loss = reduced[0][-1][0, 0]
    for n, *res in zip(rep, *reduced):
        out[n] = tuple(res)
    hidden = [u[0] for u in updates] + [out[rep[0]][0]]
    tb0 = {**scatter_wait(handles[1], hidden), **scatter_wait(handles[2], hidden)}
    sums = [_add_quads("rs_sum0_" + n, *tb0[n], 0, flag[n], sums[i]) for i, n in enumerate(names)]
    sums, updates = finish(0, sums, updates)
    for n, gr, upd in zip(names, sums, updates):
        out[n] = (gr, *upd)
    out["w_in"] = tuple(jnp.swapaxes(a, 1, 2) for a in out["w_in"])
    return (loss, grad_x[None], *[out[n][i] for i in range(4) for n in WEIGHTS])
```
